```python
import jax
import jax.numpy as jnp
from jax import lax
import numpy as np

D_MODEL = 1024
BATCH = 2
SEQ = 8192
DEPTH = 2
DEC_BATCH = 16
DEC_SEQ = 16
PAST_LEN = 4096

CHUNK = 64
N_AB = (DEPTH + 1) // 2
N_C = DEPTH // 2

H_A = 4
DK_A = 128
DV_A = 128
CONV_W = 4
QKV_A = H_A * (2 * DK_A + DV_A)
H_B = 4
DQK_B = 64
DV_B = 128
H_C = 4
DK_C = 256
DV_C = 512
ROPE_BASE = 10000.0

AB_SIZES = (QKV_A, H_A, H_A, H_A * DV_A,
            H_B * DQK_B, H_B * DQK_B, H_B * DV_B, 2 * H_B, H_B * DV_B)
IN_AB = sum(AB_SIZES)
MIX_AB = H_A * DV_A + H_B * DV_B
C_SIZES = (H_C * DK_C, H_C * DK_C, H_C * DV_C, H_C * DV_C)
IN_C = sum(C_SIZES)
MIX_C = H_C * DV_C

N_EXPERTS = 16
N_GROUPS = 4
EXPERTS_PER_GROUP = N_EXPERTS // N_GROUPS
TOP_K = 2
D_FF = 512

N_MOD = 6
EPS = 1e-6

kernel_name = 'hybrid_streaming_encoder_step'


def _split(x, sizes):
    return jnp.split(x, np.cumsum(sizes)[:-1].tolist(), axis=-1)


def rmsnorm(x, gain=None):
    x32 = x.astype(jnp.float32)
    y = x32 * lax.rsqrt(jnp.mean(x32 * x32, axis=-1, keepdims=True) + EPS)
    if gain is not None:
        y = y * gain.astype(jnp.float32)
    return y


def l2norm(x):
    return x * lax.rsqrt(jnp.sum(x * x, axis=-1, keepdims=True) + EPS)


def to_chunks(x, L):
    B, T, H, d = x.shape
    return x.reshape(B, T // L, L, H, d).transpose(1, 0, 3, 2, 4)


def from_chunks(y):
    N, B, H, L, d = y.shape
    return y.transpose(1, 0, 3, 2, 4).reshape(B, N * L, H, d)


def causal_conv(x, buf, w):
    xp = jnp.concatenate([buf, x], axis=1)
    y = lax.conv_general_dilated(xp, w[:, None, :], window_strides=(1,), padding='VALID',
                                 dimension_numbers=('NWC', 'WIO', 'NWC'),
                                 feature_group_count=x.shape[-1])
    return y, xp[:, xp.shape[1] - (CONV_W - 1):]


def rope(x, pos):
    half = x.shape[-1] // 2
    inv = jnp.power(ROPE_BASE, -jnp.linspace(0.0, 1.0, half, dtype=jnp.float32))
    ang = pos[:, None] * inv[None, :]
    cos = jnp.cos(ang)[None, :, None, :]
    sin = jnp.sin(ang)[None, :, None, :]
    x1, x2 = x[..., :half], x[..., half:]
    return jnp.concatenate([x1 * cos - x2 * sin, x1 * sin + x2 * cos], axis=-1)


def gdn_chunk(S, inp):
    q, k, v, g, beta = inp
    L = q.shape[-2]
    tri = jnp.tril(jnp.ones((L, L), bool))
    strict = jnp.tril(jnp.ones((L, L), bool), -1)
    gc = jnp.cumsum(g, axis=-1)
    diff = gc[..., :, None] - gc[..., None, :]
    decay = jnp.where(tri, jnp.exp(jnp.where(tri, diff, 0.0)), 0.0)
    kb = k * beta[..., None]
    A = jnp.where(strict, jnp.einsum('bhid,bhjd->bhij', kb, k) * decay, 0.0)
    M = A + jnp.eye(L, dtype=A.dtype)
    rhs = jnp.concatenate([kb * jnp.exp(gc)[..., None], v * beta[..., None]], axis=-1)
    sol = lax.linalg.triangular_solve(M, rhs, left_side=True, lower=True, unit_diagonal=True)
    w, u = sol[..., :DK_A], sol[..., DK_A:]
    v_new = u - jnp.einsum('bhld,bhde->bhle', w, S)
    attn = jnp.where(tri, jnp.einsum('bhid,bhjd->bhij', q, k) * decay, 0.0)
    o = (jnp.einsum('bhld,bhde->bhle', q * jnp.exp(gc)[..., None], S)
         + jnp.einsum('bhij,bhje->bhie', attn, v_new))
    gl = gc[..., -1]
    S = (S * jnp.exp(gl)[..., None, None]
         + jnp.einsum('bhld,bhle->bhde', k * jnp.exp(gl[..., None] - gc)[..., None], v_new))
    return S, o


def mlstm_chunk(carry, inp):
    C, n, m = carry
    q, k, v, log_i, log_f = inp
    L = q.shape[-2]
    tri = jnp.tril(jnp.ones((L, L), bool))
    b = jnp.cumsum(log_f, axis=-1)
    D = jnp.where(tri, b[..., :, None] - b[..., None, :] + log_i[..., None, :], -jnp.inf)
    a0 = b + m[..., None]
    m_t = jnp.maximum(a0, jnp.max(D, axis=-1))
    w0 = jnp.exp(a0 - m_t)
    s = jnp.einsum('bhid,bhjd->bhij', q, k) * jnp.exp(D - m_t[..., None])
    num = w0[..., None] * jnp.einsum('bhld,bhde->bhle', q, C) + jnp.einsum('bhij,bhje->bhie', s, v)
    den = w0 * jnp.einsum('bhld,bhd->bhl', q, n) + jnp.sum(s, axis=-1)
    h = num / jnp.maximum(jnp.abs(den), jnp.exp(-m_t))[..., None]
    m_new = m_t[..., -1]
    wk = jnp.exp(b[..., -1:] - b + log_i - m_new[..., None])[..., None]
    w0L = w0[..., -1]
    C_new = w0L[..., None, None] * C + jnp.einsum('bhld,bhle->bhde', k * wk, v)
    n_new = w0L[..., None] * n + jnp.sum(k * wk, axis=-2)
    return (C_new, n_new, m_new), h


def mixer_ab(h, conv_buf, s_a, c_b, n_b, m_b, w_in, conv_a, a_log, dt_bias, norm_a,
             gate_bias_b, norm_b, w_out):
    f32 = jnp.float32
    B, T, _ = h.shape
    L = min(CHUNK, T)
    proj = jnp.einsum('btd,de->bte', h, w_in.astype(f32))
    qkv_a, a_a, beta_a, z_a, q_b, k_b, v_b, if_b, o_b = _split(proj, AB_SIZES)
    qkv_a, conv_new = causal_conv(qkv_a, conv_buf.astype(f32), conv_a.astype(f32))
    q_a, k_a, v_a = _split(jax.nn.silu(qkv_a), (H_A * DK_A, H_A * DK_A, H_A * DV_A))
    q_a = l2norm(q_a.reshape(B, T, H_A, DK_A)) * DK_A ** -0.5
    k_a = l2norm(k_a.reshape(B, T, H_A, DK_A))
    v_a = v_a.reshape(B, T, H_A, DV_A)
    log_alpha = -jnp.exp(a_log.astype(f32)) * jax.nn.softplus(a_a + dt_bias.astype(f32))
    beta = jax.nn.sigmoid(beta_a)
    s_a_new, o_a = lax.scan(gdn_chunk, s_a.astype(f32),
                            (to_chunks(q_a, L), to_chunks(k_a, L), to_chunks(v_a, L),
                             to_chunks(log_alpha[..., None], L)[..., 0],
                             to_chunks(beta[..., None], L)[..., 0]))
    o_a = rmsnorm(from_chunks(o_a), norm_a) * jax.nn.silu(z_a.reshape(B, T, H_A, DV_A))
    q_b = q_b.reshape(B, T, H_B, DQK_B) * DQK_B ** -0.5
    k_b = k_b.reshape(B, T, H_B, DQK_B)
    v_b = v_b.reshape(B, T, H_B, DV_B)
    gates = if_b + gate_bias_b.astype(f32)
    log_i = gates[..., :H_B]
    log_f = jax.nn.log_sigmoid(gates[..., H_B:])
    (c_new, n_new, m_new), h_b = lax.scan(
        mlstm_chunk, (c_b.astype(f32), n_b.astype(f32), m_b.astype(f32)),
        (to_chunks(q_b, L), to_chunks(k_b, L), to_chunks(v_b, L),
         to_chunks(log_i[..., None], L)[..., 0], to_chunks(log_f[..., None], L)[..., 0]))
    o_bb = rmsnorm(from_chunks(h_b), norm_b) * jax.nn.sigmoid(o_b.reshape(B, T, H_B, DV_B))
    mixed = jnp.concatenate([o_a.reshape(B, T, -1), o_bb.reshape(B, T, -1)], axis=-1)
    y = jnp.einsum('bte,ed->btd', mixed, w_out.astype(f32))
    return y, (conv_new, s_a_new, c_new, n_new, m_new)


def mixer_c(h, s_c, w_in, w_out, pos0):
    f32 = jnp.float32
    B, T, _ = h.shape
    L = min(CHUNK, T)
    q, k, v, g = _split(jnp.einsum('btd,de->bte', h, w_in.astype(f32)), C_SIZES)
    pos = pos0 + jnp.arange(T, dtype=f32)
    q = rope(q.reshape(B, T, H_C, DK_C), pos)
    k = rope(k.reshape(B, T, H_C, DK_C), pos) * DK_C ** -0.5
    v = v.reshape(B, T, H_C, DV_C)
    log_gamma = jnp.log1p(-jnp.exp2(-5.0 - jnp.arange(H_C, dtype=f32)))
    idx = jnp.arange(L, dtype=f32)
    rel = idx[:, None] - idx[None, :]
    intra = jnp.where(rel >= 0, jnp.exp(log_gamma[:, None, None] * jnp.maximum(rel, 0.0)), 0.0)
    q_decay = jnp.exp(log_gamma[:, None] * (idx + 1.0))[..., None]
    k_decay = jnp.exp(log_gamma[:, None] * (L - 1.0 - idx))[..., None]
    s_decay = jnp.exp(log_gamma * L)[:, None, None]

    def step(S, inp):
        qc, kc, vc = inp
        o = (jnp.einsum('bhij,bhje->bhie', jnp.einsum('bhid,bhjd->bhij', qc, kc) * intra, vc)
             + jnp.einsum('bhld,bhde->bhle', qc * q_decay, S))
        S = s_decay * S + jnp.einsum('bhld,bhle->bhde', kc * k_decay, vc)
        return S, o

    s_new, o = lax.scan(step, s_c.astype(f32), (to_chunks(q, L), to_chunks(k, L), to_chunks(v, L)))
    o = rmsnorm(from_chunks(o)) * jax.nn.silu(g.reshape(B, T, H_C, DV_C))
    return jnp.einsum('bte,ed->btd', o.reshape(B, T, -1), w_out.astype(f32)), s_new


def moe(h, w_router, router_bias, w_gate, w_up, w_down):
    f32 = jnp.float32
    B, T, D = h.shape
    xf = h.reshape(B * T, D)
    scores = jax.nn.sigmoid(xf @ w_router.astype(f32))
    sel = (scores + router_bias.astype(f32)).reshape(-1, N_GROUPS, EXPERTS_PER_GROUP)
    group_score = jnp.sum(lax.top_k(sel, TOP_K)[0], axis=-1)
    best = jnp.argmax(group_score, axis=-1)
    in_group = jnp.take_along_axis(sel, best[:, None, None], axis=1)[:, 0]
    _, local = lax.top_k(in_group, TOP_K)
    idx = best[:, None] * EXPERTS_PER_GROUP + local
    wts = jnp.take_along_axis(scores, idx, axis=-1)
    wts = wts / jnp.sum(wts, axis=-1, keepdims=True)
    combine = jnp.einsum('nk,nke->ne', wts, jax.nn.one_hot(idx, N_EXPERTS, dtype=wts.dtype))
    g = jnp.einsum('nd,edf->nef', xf, w_gate.astype(f32))
    u = jnp.einsum('nd,edf->nef', xf, w_up.astype(f32))
    a = jax.nn.silu(g) * u * combine[..., None]
    return jnp.einsum('nef,efd->nd', a, w_down.astype(f32)).reshape(B, T, D)


def trunk(x, c, pos0, conv_s, a_s, bc_s, bn_s, bm_s, cs_s,
          w_mod, b_mod, norm_mix, norm_ffn, w_in_ab, conv_a, a_log, dt_bias, norm_a,
          gate_bias_b, norm_b, w_out_ab, w_in_c, w_out_c, w_router, router_bias,
          w_gate, w_up, w_down, norm_final):
    f32 = jnp.float32
    h_res = x.astype(f32)
    mod_all = (jnp.einsum('bd,lde->lbe', jax.nn.silu(c.astype(f32)), w_mod.astype(f32))
               + b_mod.astype(f32)[:, None])
    new_ab, new_c = [], []
    for layer in range(DEPTH):
        sh1, sc1, g1, sh2, sc2, g2 = jnp.split(mod_all[layer], N_MOD, axis=-1)
        hn = rmsnorm(h_res, norm_mix[layer]) * (1.0 + sc1[:, None]) + sh1[:, None]
        i = layer // 2
        if layer % 2 == 0:
            y, st = mixer_ab(hn, conv_s[i], a_s[i], bc_s[i], bn_s[i], bm_s[i], w_in_ab[i], conv_a[i],
                             a_log[i], dt_bias[i], norm_a[i], gate_bias_b[i], norm_b[i], w_out_ab[i])
            new_ab.append(st)
        else:
            y, st = mixer_c(hn, cs_s[i], w_in_c[i], w_out_c[i], pos0)
            new_c.append(st)
        h_res = h_res + g1[:, None] * y
        hn = rmsnorm(h_res, norm_ffn[layer]) * (1.0 + sc2[:, None]) + sh2[:, None]
        h_res = h_res + g2[:, None] * moe(hn, w_router, router_bias, w_gate[layer], w_up[layer], w_down[layer])
    y = rmsnorm(h_res, norm_final).astype(x.dtype)
    ab = [jnp.stack(s).astype(x.dtype) for s in zip(*new_ab)]
    cs = jnp.stack(new_c).astype(x.dtype)
    return y, ab[0], ab[1], ab[2], ab[3], ab[4], cs


def setup_inputs(seed: int = 0) -> dict:
    key = jax.random.key(seed)
    ks = iter(jax.random.split(key, 48))
    f32 = jnp.float32

    def nrm(shape, scale):
        return scale * jax.random.normal(next(ks), shape, f32)

    D = D_MODEL
    a_log = jnp.log(jax.random.uniform(next(ks), (N_AB, H_A), f32, 1.0, 16.0))
    dt = jnp.exp(jax.random.uniform(next(ks), (N_AB, H_A), f32, np.log(1e-3), np.log(1e-1)))
    dt_bias = dt + jnp.log(-jnp.expm1(-dt))
    gate_bias_b = jnp.concatenate([nrm((N_AB, H_B), 0.1), 3.0 + nrm((N_AB, H_B), 0.5)], axis=-1)
    return {
        'x_prompt': nrm((BATCH, SEQ, D), 1.0),
        'x_sample': nrm((DEC_BATCH, DEC_SEQ, D), 1.0),
        'c_prompt': nrm((BATCH, D), 1.0),
        'c_sample': nrm((DEC_BATCH, D), 1.0),
        'state_a_conv': nrm((N_AB, DEC_BATCH, CONV_W - 1, QKV_A), 1.0),
        'state_a_rec': nrm((N_AB, DEC_BATCH, H_A, DK_A, DV_A), 0.3),
        'state_b_c': nrm((N_AB, DEC_BATCH, H_B, DQK_B, DV_B), 0.3),
        'state_b_n': nrm((N_AB, DEC_BATCH, H_B, DQK_B), 0.3),
        'state_b_m': nrm((N_AB, DEC_BATCH, H_B), 1.0),
        'state_c_rec': nrm((N_C, DEC_BATCH, H_C, DK_C, DV_C), 1.0),
        'w_mod': nrm((DEPTH, D, N_MOD * D), 0.5 * D ** -0.5),
        'b_mod': nrm((DEPTH, N_MOD * D), 0.02),
        'norm_mix': 1.0 + nrm((DEPTH, D), 0.05),
        'norm_ffn': 1.0 + nrm((DEPTH, D), 0.05),
        'w_in_ab': nrm((N_AB, D, IN_AB), D ** -0.5),
        'conv_a': nrm((N_AB, CONV_W, QKV_A), CONV_W ** -0.5),
        'a_log': a_log,
        'dt_bias': dt_bias,
        'norm_a': 1.0 + nrm((N_AB, DV_A), 0.05),
        'gate_bias_b': gate_bias_b,
        'norm_b': 1.0 + nrm((N_AB, DV_B), 0.05),
        'w_out_ab': nrm((N_AB, MIX_AB, D), MIX_AB ** -0.5),
        'w_in_c': nrm((N_C, D, IN_C), D ** -0.5),
        'w_out_c': nrm((N_C, MIX_C, D), MIX_C ** -0.5),
        'w_router': nrm((D, N_EXPERTS), D ** -0.5),
        'router_bias': nrm((N_EXPERTS,), 0.01),
        'w_gate': nrm((DEPTH, N_EXPERTS, D, D_FF), D ** -0.5),
        'w_up': nrm((DEPTH, N_EXPERTS, D, D_FF), D ** -0.5),
        'w_down': nrm((DEPTH, N_EXPERTS, D_FF, D), D_FF ** -0.5),
        'norm_final': 1.0 + nrm((D,), 0.05),
    }


def reference(x_prompt, x_sample, c_prompt, c_sample, state_a_conv, state_a_rec, state_b_c,
              state_b_n, state_b_m, state_c_rec, w_mod, b_mod, norm_mix, norm_ffn, w_in_ab, conv_a,
              a_log, dt_bias, norm_a, gate_bias_b, norm_b, w_out_ab, w_in_c, w_out_c, w_router,
              router_bias, w_gate, w_up, w_down, norm_final):
    f32 = jnp.float32
    bp = x_prompt.shape[0]
    (y_prompt, p_a_conv, p_a_rec, p_b_c, p_b_n, p_b_m, p_c_rec) = trunk(
        x_prompt, c_prompt, 0,
        jnp.zeros((N_AB, bp, CONV_W - 1, QKV_A), f32), jnp.zeros((N_AB, bp, H_A, DK_A, DV_A), f32),
        jnp.zeros((N_AB, bp, H_B, DQK_B, DV_B), f32), jnp.zeros((N_AB, bp, H_B, DQK_B), f32),
        jnp.zeros((N_AB, bp, H_B), f32), jnp.zeros((N_C, bp, H_C, DK_C, DV_C), f32),
        w_mod, b_mod, norm_mix, norm_ffn, w_in_ab, conv_a, a_log, dt_bias, norm_a, gate_bias_b,
        norm_b, w_out_ab, w_in_c, w_out_c, w_router, router_bias, w_gate, w_up, w_down, norm_final)
    (y_sample, s_a_conv, s_a_rec, s_b_c, s_b_n, s_b_m, s_c_rec) = trunk(
        x_sample, c_sample, PAST_LEN,
        state_a_conv, state_a_rec, state_b_c, state_b_n, state_b_m, state_c_rec,
        w_mod, b_mod, norm_mix, norm_ffn, w_in_ab, conv_a, a_log, dt_bias, norm_a, gate_bias_b,
        norm_b, w_out_ab, w_in_c, w_out_c, w_router, router_bias, w_gate, w_up, w_down, norm_final)
    return (y_prompt, y_sample, p_a_conv, p_a_rec, p_b_c, p_b_n, p_b_m, p_c_rec,
            s_a_conv, s_a_rec, s_b_c, s_b_n, s_b_m, s_c_rec)
```

```python
import functools

import numpy as np
import jax
import jax.numpy as jnp
from jax import lax
from jax.experimental import pallas as pl
from jax.experimental.pallas import tpu as pltpu

F32 = jnp.float32
BF16 = jnp.bfloat16
I32 = jnp.int32

D_MODEL = 1024
DEPTH = 2
CHUNK = 64
H_A, DK_A, DV_A, CONV_W = 4, 128, 128, 4
QKV_A = H_A * (2 * DK_A + DV_A)
H_B, DQK_B, DV_B = 4, 64, 128
H_C, DK_C, DV_C = 4, 256, 512
ROPE_BASE = 10000.0
PAST_LEN = 4096
AB_SIZES = (QKV_A, H_A, H_A, H_A * DV_A, H_B * DQK_B, H_B * DQK_B, H_B * DV_B, 2 * H_B, H_B * DV_B)
MIX_AB = H_A * DV_A + H_B * DV_B
MIX_C = H_C * DV_C
IN_C = 2 * H_C * DK_C + 2 * H_C * DV_C
N_EXPERTS, N_GROUPS, EPG, D_FF = 16, 4, 4, 512
N_MOD = 6
EPS = 1e-6

LANES = 128
AB_MAIN = 3584
N_GATE_COLS = 16
HX_W = D_MODEL + LANES
N_BUCKETS = N_GROUPS * 6
MOE_TM = 256
VMEM_LIMIT = 48 * 1024 * 1024

NN = ((1,), (0,))
NT = ((1,), (1,))
TN = ((0,), (0,))


def _dot(a, b, dims=NN):
    return lax.dot_general(a, b, (dims, ((), ())), preferred_element_type=F32)


def _mmb(a, b, dims=NN):
    return _dot(a.astype(BF16), b.astype(BF16), dims)


def _split2(x):
    hi = x.astype(BF16)
    lo = (x - hi.astype(F32)).astype(BF16)
    return hi, lo


def _split3(x):
    hi = x.astype(BF16)
    r = x - hi.astype(F32)
    mid = r.astype(BF16)
    lo = (r - mid.astype(F32)).astype(BF16)
    return hi, mid, lo


def _mm3(a, b, dims=NN):
    ah, al = _split2(a)
    bh, bl = _split2(b)
    return _dot(ah, bh, dims) + (_dot(ah, bl, dims) + _dot(al, bh, dims))


def _mm_mask_l(mask_bf16, x):
    h, m, l = _split3(x)
    return _dot(mask_bf16, h) + (_dot(mask_bf16, m) + _dot(mask_bf16, l))


def _mm_mask_r(x, mask_bf16):
    h, m, l = _split3(x)
    return _dot(h, mask_bf16) + (_dot(m, mask_bf16) + _dot(l, mask_bf16))


def _sigmoid(x):
    return 1.0 / (1.0 + jnp.exp(-x))


def _silu(x):
    return x * _sigmoid(x)


def _softplus(x):
    return jnp.maximum(x, 0.0) + jnp.log(1.0 + jnp.exp(-jnp.abs(x)))


def _rms(x, eps=EPS):
    return x * lax.rsqrt(jnp.mean(x * x, axis=-1, keepdims=True) + eps)


def _cparams(sem):
    return pltpu.CompilerParams(dimension_semantics=sem, vmem_limit_bytes=VMEM_LIMIT)


def _mod_kernel(c_ref, w_ref, b_ref, o_ref):
    c = c_ref[...]
    o_ref[0] = _mm3(_silu(c), w_ref[0]) + b_ref[0]


def _mod_call(c_all, w_mod, b_mod):
    bt = c_all.shape[0]
    e = w_mod.shape[-1]
    tn = 1024
    return pl.pallas_call(
        _mod_kernel,
        grid=(DEPTH, e // tn),
        in_specs=[pl.BlockSpec((bt, D_MODEL), lambda l, j: (0, 0)),
                  pl.BlockSpec((1, D_MODEL, tn), lambda l, j: (l, 0, j)),
                  pl.BlockSpec((1, 1, tn), lambda l, j: (l, 0, j))],
        out_specs=pl.BlockSpec((1, bt, tn), lambda l, j: (l, 0, j)),
        out_shape=jax.ShapeDtypeStruct((DEPTH, bt, e), F32),
        compiler_params=_cparams(("parallel", "parallel")),
        name="mod",
    )(c_all, w_mod, b_mod.reshape(DEPTH, 1, e))


def _ln_mm_kernel(*refs, has_prev, has_gates):
    it = iter(refs)
    x_ref = next(it)
    if has_prev:
        yp_ref, gp_ref = next(it), next(it)
    gain_ref, sc_ref, sh_ref, w_ref = next(it), next(it), next(it), next(it)
    if has_gates:
        wg_ref, wgt_ref = next(it), next(it)
    if has_prev:
        h_ref = next(it)
    proj_ref = next(it)
    if has_gates:
        g_ref, gt_ref = next(it), next(it)
    hn_s = next(it)

    bb, tt, d = x_ref.shape
    rows = bb * tt

    @pl.when(pl.program_id(2) == 0)
    def _():
        x = x_ref[...]
        if has_prev:
            x = x + gp_ref[...] * yp_ref[...].reshape(bb, tt, d)
            h_ref[...] = x
        hn = _rms(x) * gain_ref[...]
        hn = hn * (1.0 + sc_ref[...]) + sh_ref[...]
        hn2 = hn.reshape(rows, d)
        hn_s[...] = hn2.astype(BF16)
        if has_gates:
            g_ref[...] = _mm3(hn2, wg_ref[...]).reshape(bb, tt, LANES)
            gt_ref[...] = _mm3(wgt_ref[...], hn2, NT)

    tn = proj_ref.shape[-1]
    proj_ref[...] = _dot(hn_s[...], w_ref[...]).reshape(bb, tt, tn).astype(proj_ref.dtype)


def _ln_mm_call(x, prev, gain, sc, sh, w, gates_w, *, bb, tt, tn, out_dtype, name):
    b, t, d = x.shape
    e = w.shape[1]
    rows = bb * tt
    nti = t // tt
    grid = (b // bb, nti, e // tn)
    has_prev = prev is not None
    has_gates = gates_w is not None
    xspec = pl.BlockSpec((bb, tt, d), lambda bi, i, j: (bi, i, 0))
    mspec = pl.BlockSpec((bb, 1, d), lambda bi, i, j: (bi, 0, 0))
    in_specs, args = [xspec], [x]
    if has_prev:
        y2d, off, gp = prev
        in_specs += [pl.BlockSpec((rows, d), lambda bi, i, j: (off + bi * nti + i, 0)), mspec]
        args += [y2d, gp]
    in_specs += [pl.BlockSpec((1, 1, d), lambda bi, i, j: (0, 0, 0)), mspec, mspec,
                 pl.BlockSpec((d, tn), lambda bi, i, j: (0, j))]
    args += [gain.reshape(1, 1, d), sc, sh, w]
    if has_gates:
        wg, wgt = gates_w
        in_specs += [pl.BlockSpec((d, LANES), lambda bi, i, j: (0, 0)),
                     pl.BlockSpec((N_GATE_COLS, d), lambda bi, i, j: (0, 0))]
        args += [wg, wgt]
    out_specs, out_shape = [], []
    if has_prev:
        out_specs.append(xspec)
        out_shape.append(jax.ShapeDtypeStruct((b, t, d), F32))
    out_specs.append(pl.BlockSpec((bb, tt, tn), lambda bi, i, j: (bi, i, j)))
    out_shape.append(jax.ShapeDtypeStruct((b, t, e), out_dtype))
    if has_gates:
        out_specs += [pl.BlockSpec((bb, tt, LANES), lambda bi, i, j: (bi, i, 0)),
                      pl.BlockSpec((N_GATE_COLS, rows), lambda bi, i, j: (0, bi * nti + i))]
        out_shape += [jax.ShapeDtypeStruct((b, t, LANES), F32),
                      jax.ShapeDtypeStruct((N_GATE_COLS, b * t), F32)]
    return pl.pallas_call(
        functools.partial(_ln_mm_kernel, has_prev=has_prev, has_gates=has_gates),
        grid=grid, in_specs=in_specs, out_specs=out_specs, out_shape=out_shape,
        scratch_shapes=[pltpu.VMEM((rows, d), BF16)],
        compiler_params=_cparams(("parallel", "parallel", "arbitrary")),
        name=name,
    )(*args)


def _tri_inv(a, eye, length):
    x = -a
    p = eye + x
    n = 2
    while n < length:
        x = _mm3(x, x)
        p = p + _mm3(p, x)
        n *= 2
    return p


def _mixer_ab_kernel(qkv_ref, z_ref, qkb_ref, vb_ref, ob_ref, g_ref, gt_ref,
                     conv0_ref, sa0_ref, cb0_ref, nb0_ref, mb0_ref,
                     convw_ref, gpar_ref, gpart_ref, na_ref, nbn_ref,
                     mixed_ref, conv_o, sa_o, cb_o, nb_o, mb_o,
                     xbuf, *, L, CB):
    TB = L * CB

    @pl.when(pl.program_id(1) == 0)
    def _():
        xbuf[5:8, :] = conv0_ref[0]
        sa_o[...] = sa0_ref[...]
        cb_o[...] = cb0_ref[...]
        nb_o[...] = nb0_ref[...]
        mb_o[...] = mb0_ref[...]

    xbuf[8:8 + TB, :] = qkv_ref[0]
    conv_o[0] = xbuf[5 + TB:8 + TB, :]

    ii = lax.broadcasted_iota(I32, (L, L), 0)
    jj = lax.broadcasted_iota(I32, (L, L), 1)
    tri = ii >= jj
    strict = ii > jj
    eye = (ii == jj).astype(F32)
    tril_b = tri.astype(BF16)
    triu_b = (ii <= jj).astype(BF16)
    lane = lax.broadcasted_iota(I32, (L, LANES), 1)
    rowi = lax.broadcasted_iota(I32, (N_GATE_COLS, L), 0)
    gpar = gpar_ref[...]
    gpart = gpart_ref[...]
    nea = -jnp.exp(gpar[0:1, :])
    neat = -jnp.exp(gpart[:, 0:1])
    norm_a = na_ref[...]
    norm_b = nbn_ref[...]

    for c in range(CB):
        r0 = c * L
        pre = g_ref[0, r0:r0 + L, :] + gpar[1:2, :]
        csrc = jnp.where(lane < 4, nea * _softplus(pre),
                         jnp.where(lane >= 12, -_softplus(-pre), 0.0))
        csrc = jnp.where(lane < N_GATE_COLS, csrc, 0.0)
        gcum = _mm_mask_l(tril_b, csrc)
        beta_all = _sigmoid(pre)
        pret = gt_ref[0, :, r0:r0 + L] + gpart[:, 1:2]
        csrct = jnp.where(rowi < 4, neat * _softplus(pret),
                          jnp.where(rowi >= 12, -_softplus(-pret), 0.0))
        gcumt = _mm_mask_r(csrct, triu_b)

        for h in range(H_A):
            def conv_tile(col0):
                acc = None
                for j in range(CONV_W):
                    term = convw_ref[j:j + 1, col0:col0 + DK_A] * xbuf[5 + j + r0:5 + j + r0 + L, col0:col0 + DK_A]
                    acc = term if acc is None else acc + term
                return _silu(acc)

            q = conv_tile(h * DK_A)
            k = conv_tile(H_A * DK_A + h * DK_A)
            v = conv_tile(2 * H_A * DK_A + h * DV_A)
            q = q * lax.rsqrt(jnp.sum(q * q, axis=-1, keepdims=True) + EPS) * (DK_A ** -0.5)
            k = k * lax.rsqrt(jnp.sum(k * k, axis=-1, keepdims=True) + EPS)
            gc_c = gcum[:, h:h + 1]
            gc_r = gcumt[h:h + 1, :]
            decay = jnp.where(tri, jnp.exp(jnp.where(tri, gc_c - gc_r, 0.0)), 0.0)
            beta_c = beta_all[:, 4 + h:5 + h]
            kb = k * beta_c
            a = jnp.where(strict, _mm3(kb, k, NT) * decay, 0.0)
            tinv = _tri_inv(a, eye, L)
            eg = jnp.exp(gc_c)
            w = _mm3(tinv, kb * eg)
            u = _mm3(tinv, v * beta_c)
            s = sa_o[0, h]
            v_new = u - _mm3(w, s)
            attn = jnp.where(tri, _mm3(q, k, NT) * decay, 0.0)
            o = _mm3(q * eg, s) + _mm3(attn, v_new)
            gl = gc_c[L - 1:L, :]
            sa_o[0, h] = s * jnp.exp(gl) + _mm3(k * jnp.exp(gl - gc_c), v_new, TN)
            zg = z_ref[0, r0:r0 + L, h * DV_A:(h + 1) * DV_A]
            o = _rms(o) * norm_a * _silu(zg)
            mixed_ref[0, r0:r0 + L, h * DV_A:(h + 1) * DV_A] = o.astype(mixed_ref.dtype)

        for h in range(H_B):
            q = qkb_ref[0, r0:r0 + L, h * DQK_B:(h + 1) * DQK_B] * (DQK_B ** -0.5)
            k = qkb_ref[0, r0:r0 + L, H_B * DQK_B + h * DQK_B:H_B * DQK_B + (h + 1) * DQK_B]
            v = vb_ref[0, r0:r0 + L, h * DV_B:(h + 1) * DV_B]
            b_c = gcum[:, 12 + h:13 + h]
            b_r = gcumt[12 + h:13 + h, :]
            li_c = pre[:, 8 + h:9 + h]
            li_r = pret[8 + h:9 + h, :]
            dm = jnp.where(tri, b_c - b_r + li_r, -jnp.inf)
            m_prev = mb_o[0, h]
            a0 = b_c + m_prev
            m_t = jnp.maximum(a0, jnp.max(dm, axis=-1, keepdims=True))
            w0 = jnp.exp(a0 - m_t)
            sm = _mm3(q, k, NT) * jnp.exp(dm - m_t)
            cst = cb_o[0, h]
            nst = nb_o[0, h]
            num = w0 * _mm3(q, cst) + _mm3(sm, v)
            den = w0 * jnp.sum(q * nst, axis=-1, keepdims=True) + jnp.sum(sm, axis=-1, keepdims=True)
            hh = num / jnp.maximum(jnp.abs(den), jnp.exp(-m_t))
            m_new = m_t[L - 1:L, :]
            wk = jnp.exp(b_c[L - 1:L, :] - b_c + li_c - m_new)
            w0l = w0[L - 1:L, :]
            kw = k * wk
            cb_o[0, h] = w0l * cst + _mm3(kw, v, TN)
            nb_o[0, h] = w0l * nst + jnp.sum(kw, axis=0, keepdims=True)
            mb_o[0, h] = m_new
            og = ob_ref[0, r0:r0 + L, h * DV_B:(h + 1) * DV_B]
            o = _rms(hh) * norm_b * _sigmoid(og)
            c0 = H_A * DV_A + h * DV_B
            mixed_ref[0, r0:r0 + L, c0:c0 + DV_B] = o.astype(mixed_ref.dtype)

    xbuf[5:8, :] = xbuf[5 + TB:8 + TB, :]


def _mixer_ab_call(proj, gates, gates_t, conv0, sa0, cb0, nb0, mb0, convw, gpar, gpart, norm_a, norm_b, *, L, CB, name):
    b, t, _ = proj.shape
    tb = L * CB
    nt = t // tb

    def col(width, idx):
        return pl.BlockSpec((1, tb, width), lambda bi, i: (bi, i, idx))

    def const(shape):
        return pl.BlockSpec(shape, lambda bi, i: (0,) * len(shape))

    def state(shape):
        return pl.BlockSpec((1,) + shape, lambda bi, i: (bi,) + (0,) * len(shape))

    in_specs = [col(QKV_A, 0), col(512, 3), col(512, 4), col(512, 5), col(512, 6),
                col(LANES, 0), pl.BlockSpec((1, N_GATE_COLS, tb), lambda bi, i: (bi, 0, i)),
                state((CONV_W - 1, QKV_A)), state((H_A, DK_A, DV_A)), state((H_B, DQK_B, DV_B)),
                state((H_B, 1, DQK_B)), state((H_B, 1, 1)),
                const((CONV_W, QKV_A)), const((8, LANES)), const((N_GATE_COLS, LANES)),
                const((1, DV_A)), const((1, DV_B))]
    out_specs = [pl.BlockSpec((1, tb, MIX_AB), lambda bi, i: (bi, i, 0)),
                 state((CONV_W - 1, QKV_A)), state((H_A, DK_A, DV_A)), state((H_B, DQK_B, DV_B)),
                 state((H_B, 1, DQK_B)), state((H_B, 1, 1))]
    out_shape = [jax.ShapeDtypeStruct((b, t, MIX_AB), BF16),
                 jax.ShapeDtypeStruct((b, CONV_W - 1, QKV_A), F32),
                 jax.ShapeDtypeStruct((b, H_A, DK_A, DV_A), F32),
                 jax.ShapeDtypeStruct((b, H_B, DQK_B, DV_B), F32),
                 jax.ShapeDtypeStruct((b, H_B, 1, DQK_B), F32),
                 jax.ShapeDtypeStruct((b, H_B, 1, 1), F32)]
    return pl.pallas_call(
        functools.partial(_mixer_ab_kernel, L=L, CB=CB),
        grid=(b, nt), in_specs=in_specs, out_specs=out_specs, out_shape=out_shape,
        scratch_shapes=[pltpu.VMEM((8 + tb, QKV_A), F32)],
        compiler_params=_cparams(("parallel", "arbitrary")),
        name=name,
    )(proj, proj, proj, proj, proj, gates, gates_t, conv0, sa0, cb0, nb0, mb0, convw, gpar, gpart, norm_a, norm_b)


def _mixer_c_kernel(q_ref, k_ref, v_ref, g_ref, cos_ref, sin_ref, s0_ref, mixed_ref, s_o, *, L, CB):
    @pl.when(pl.program_id(1) == 0)
    def _():
        s_o[...] = s0_ref[...]

    ii = lax.broadcasted_iota(I32, (L, L), 0)
    jj = lax.broadcasted_iota(I32, (L, L), 1)
    rel = (ii - jj).astype(F32)
    idx = lax.broadcasted_iota(I32, (L, 1), 0).astype(F32)
    half = DK_C // 2

    def rope(x, cos, sin):
        x1, x2 = x[:, :half], x[:, half:]
        return jnp.concatenate([x1 * cos - x2 * sin, x1 * sin + x2 * cos], axis=-1)

    for c in range(CB):
        r0 = c * L
        cos = cos_ref[r0:r0 + L, :]
        sin = sin_ref[r0:r0 + L, :]
        for h in range(H_C):
            lg = float(np.log1p(-np.exp2(-5.0 - h)))
            intra = jnp.where(rel >= 0, jnp.exp(lg * jnp.maximum(rel, 0.0)), 0.0)
            q_decay = jnp.exp(lg * (idx + 1.0))
            k_decay = jnp.exp(lg * (L - 1.0 - idx))
            s_decay = float(np.exp(lg * L))
            q = rope(q_ref[0, r0:r0 + L, h * DK_C:(h + 1) * DK_C].astype(F32), cos, sin)
            k = rope(k_ref[0, r0:r0 + L, h * DK_C:(h + 1) * DK_C].astype(F32), cos, sin) * (DK_C ** -0.5)
            v = v_ref[0, r0:r0 + L, h * DV_C:(h + 1) * DV_C]
            s = s_o[0, h]
            o = _mmb(_mmb(q, k, NT) * intra, v) + _mmb(q * q_decay, s)
            s_o[0, h] = s_decay * s + _mmb(k * k_decay, v, TN)
            gg = g_ref[0, r0:r0 + L, h * DV_C:(h + 1) * DV_C].astype(F32)
            o = _rms(o) * _silu(gg)
            mixed_ref[0, r0:r0 + L, h * DV_C:(h + 1) * DV_C] = o.astype(mixed_ref.dtype)


def _mixer_c_call(proj, cos, sin, s0, *, L, CB, name):
    b, t, _ = proj.shape
    tb = L * CB
    nt = t // tb
    qk_w, v_w = H_C * DK_C, H_C * DV_C
    sspec = pl.BlockSpec((1, H_C, DK_C, DV_C), lambda bi, i: (bi, 0, 0, 0))
    tspec = pl.BlockSpec((tb, DK_C // 2), lambda bi, i: (i, 0))
    return pl.pallas_call(
        functools.partial(_mixer_c_kernel, L=L, CB=CB),
        grid=(b, nt),
        in_specs=[pl.BlockSpec((1, tb, qk_w), lambda bi, i: (bi, i, 0)),
                  pl.BlockSpec((1, tb, qk_w), lambda bi, i: (bi, i, 1)),
                  pl.BlockSpec((1, tb, v_w), lambda bi, i: (bi, i, 1)),
                  pl.BlockSpec((1, tb, v_w), lambda bi, i: (bi, i, 2)),
                  tspec, tspec, sspec],
        out_specs=[pl.BlockSpec((1, tb, MIX_C), lambda bi, i: (bi, i, 0)), sspec],
        out_shape=[jax.ShapeDtypeStruct((b, t, MIX_C), BF16),
                   jax.ShapeDtypeStruct((b, H_C, DK_C, DV_C), F32)],
        compiler_params=_cparams(("parallel", "arbitrary")),
        name=name,
    )(proj, proj, proj, proj, cos, sin, s0)


def _out_route_kernel(mixed_ref, wout_ref, h_ref, g1_ref, gain_ref, sc_ref, sh_ref, wrt_ref, rb_ref, hx_in_ref,
                      hnew_ref, hx_ref, bkt_ref):
    del hx_in_ref
    bb, tt, m = mixed_ref.shape
    d = h_ref.shape[-1]
    rows = bb * tt
    y = _dot(mixed_ref[...].reshape(rows, m), wout_ref[...])
    h = h_ref[...] + g1_ref[...] * y.reshape(bb, tt, d)
    hnew_ref[...] = h
    hn = _rms(h) * gain_ref[...]
    hn = (hn * (1.0 + sc_ref[...]) + sh_ref[...]).reshape(rows, d)
    hx_ref[:, 0:d] = hn

    logits = _mm3(wrt_ref[...], hn, NT)
    score = _sigmoid(logits)
    sel = score + rb_ref[...]

    def row(a, r):
        return a[r:r + 1, :]

    gscore = []
    for g in range(N_GROUPS):
        a, b, c, e = (row(sel, EPG * g + i) for i in range(EPG))
        hi_ab, lo_ab, hi_ce, lo_ce = jnp.maximum(a, b), jnp.minimum(a, b), jnp.maximum(c, e), jnp.minimum(c, e)
        top1 = jnp.maximum(hi_ab, hi_ce)
        top2 = jnp.maximum(jnp.maximum(lo_ab, lo_ce), jnp.minimum(hi_ab, hi_ce))
        gscore.append(top1 + top2)
    best = jnp.zeros((1, rows), I32)
    bestv = gscore[0]
    for g in range(1, N_GROUPS):
        upd = gscore[g] > bestv
        best = jnp.where(upd, g, best)
        bestv = jnp.where(upd, gscore[g], bestv)

    def pick(a, i):
        out = row(a, i)
        for g in range(1, N_GROUPS):
            out = jnp.where(best == g, row(a, EPG * g + i), out)
        return out

    vsel = [pick(sel, i) for i in range(EPG)]
    vsc = [pick(score, i) for i in range(EPG)]
    i1 = jnp.zeros((1, rows), I32)
    v1 = vsel[0]
    for i in range(1, EPG):
        upd = vsel[i] > v1
        i1 = jnp.where(upd, i, i1)
        v1 = jnp.where(upd, vsel[i], v1)
    i2 = jnp.full((1, rows), -1, I32)
    v2 = jnp.full((1, rows), -jnp.inf, F32)
    for i in range(EPG):
        cand = jnp.where(i1 == i, -jnp.inf, vsel[i])
        upd = cand > v2
        i2 = jnp.where(upd, i, i2)
        v2 = jnp.where(upd, cand, v2)
    s1 = jnp.zeros((1, rows), F32)
    s2 = jnp.zeros((1, rows), F32)
    for i in range(EPG):
        s1 = jnp.where(i1 == i, vsc[i], s1)
        s2 = jnp.where(i2 == i, vsc[i], s2)
    den = s1 + s2
    w1 = s1 / den
    w2 = s2 / den
    first_lo = i1 < i2
    lo = jnp.minimum(i1, i2)
    hi = jnp.maximum(i1, i2)
    wlo = jnp.where(first_lo, w1, w2)
    whi = jnp.where(first_lo, w2, w1)
    pair = jnp.where(lo == 0, hi - 1, jnp.where(lo == 1, hi + 1, 5))
    bkt_ref[...] = (best * 6 + pair).reshape(1, 1, rows)

    aux = jnp.concatenate([wlo, whi, jnp.zeros((LANES - 2, rows), F32)], axis=0)
    hx_ref[:, d:d + LANES] = aux.T


def _out_route_call(mixed, wout, h, g1, gain, sc, sh, wrt, rb, hx_buf, row_block_off, *, bb, tt, name):
    b, t, m = mixed.shape
    d = h.shape[-1]
    rows = bb * tt
    nti = t // tt
    ntot = hx_buf.shape[0]
    xspec = pl.BlockSpec((bb, tt, d), lambda bi, i: (bi, i, 0))
    mspec = pl.BlockSpec((bb, 1, d), lambda bi, i: (bi, 0, 0))
    return pl.pallas_call(
        _out_route_kernel,
        grid=(b // bb, nti),
        in_specs=[pl.BlockSpec((bb, tt, m), lambda bi, i: (bi, i, 0)),
                  pl.BlockSpec((m, d), lambda bi, i: (0, 0)),
                  xspec, mspec,
                  pl.BlockSpec((1, 1, d), lambda bi, i: (0, 0, 0)), mspec, mspec,
                  pl.BlockSpec((N_EXPERTS, d), lambda bi, i: (0, 0)),
                  pl.BlockSpec((N_EXPERTS, 1), lambda bi, i: (0, 0)),
                  pl.BlockSpec(memory_space=pl.ANY)],
        out_specs=[xspec,
                   pl.BlockSpec((rows, HX_W), lambda bi, i: (row_block_off + bi * nti + i, 0)),
                   pl.BlockSpec((1, 1, rows), lambda bi, i: (bi * nti + i, 0, 0))],
        out_shape=[jax.ShapeDtypeStruct((b, t, d), F32),
                   jax.ShapeDtypeStruct((ntot, HX_W), F32),
                   jax.ShapeDtypeStruct(((b // bb) * nti, 1, rows), I32)],
        input_output_aliases={9: 1},
        compiler_params=_cparams(("parallel", "parallel")),
        name=name,
    )(mixed, wout, h, g1, gain.reshape(1, 1, d), sc, sh, wrt, rb, hx_buf)


def _moe_kernel(ea_ref, eb_ref, valid_ref, xs_ref, wga_ref, wgb_ref, wua_ref, wub_ref, wda_ref, wdb_ref, ys_ref):
    t = pl.program_id(0)

    @pl.when(valid_ref[t] > 0)
    def _():
        x = xs_ref[:, 0:D_MODEL].astype(BF16)
        wlo = xs_ref[:, D_MODEL:D_MODEL + 1]
        whi = xs_ref[:, D_MODEL + 1:D_MODEL + 2]
        act_a = (_silu(_dot(x, wga_ref[0])) * _dot(x, wua_ref[0]) * wlo).astype(BF16)
        act_b = (_silu(_dot(x, wgb_ref[0])) * _dot(x, wub_ref[0]) * whi).astype(BF16)
        ys_ref[...] = _dot(act_a, wda_ref[0]) + _dot(act_b, wdb_ref[0])

    @pl.when(valid_ref[t] == 0)
    def _():
        ys_ref[...] = jnp.zeros_like(ys_ref)


def _moe_call(ea, eb, valid, xs, wg, wu, wd):
    p = xs.shape[0]
    n_tiles = p // MOE_TM
    wa_in = pl.BlockSpec((1, D_MODEL, D_FF), lambda t, ea, eb, va: (ea[t], 0, 0))
    wb_in = pl.BlockSpec((1, D_MODEL, D_FF), lambda t, ea, eb, va: (eb[t], 0, 0))
    wa_out = pl.BlockSpec((1, D_FF, D_MODEL), lambda t, ea, eb, va: (ea[t], 0, 0))
    wb_out = pl.BlockSpec((1, D_FF, D_MODEL), lambda t, ea, eb, va: (eb[t], 0, 0))
    return pl.pallas_call(
        _moe_kernel,
        grid_spec=pltpu.PrefetchScalarGridSpec(
            num_scalar_prefetch=3, grid=(n_tiles,),
            in_specs=[pl.BlockSpec((MOE_TM, HX_W), lambda t, ea, eb, va: (t, 0)),
                      wa_in, wb_in, wa_in, wb_in, wa_out, wb_out],
            out_specs=pl.BlockSpec((MOE_TM, D_MODEL), lambda t, ea, eb, va: (t, 0))),
        out_shape=jax.ShapeDtypeStruct((p, D_MODEL), F32),
        compiler_params=_cparams(("arbitrary",)),
        name="moe",
    )(ea, eb, valid, xs, wg, wg, wu, wu, wd, wd)


_PAIRS = [(0, 1), (0, 2), (0, 3), (1, 2), (1, 3), (2, 3)]
_EA_TABLE = np.array([g * EPG + p[0] for g in range(N_GROUPS) for p in _PAIRS], np.int32)
_EB_TABLE = np.array([g * EPG + p[1] for g in range(N_GROUPS) for p in _PAIRS], np.int32)


def _moe(hx, bkt, wg, wu, wd):
    n = hx.shape[0]
    tm = MOE_TM
    ar = jnp.arange(N_BUCKETS, dtype=I32)
    oh = (bkt[:, None] == ar[None, :]).astype(I32)
    cs = jnp.cumsum(oh, axis=0)
    rank = jnp.sum(oh * cs, axis=1) - 1
    counts = cs[-1]
    padded = ((counts + tm - 1) // tm) * tm
    ends = jnp.cumsum(padded)
    offs = ends - padded
    pos = jnp.sum(oh * offs[None, :], axis=1) + rank
    n_tiles = -(-n // tm) + N_BUCKETS
    p = n_tiles * tm
    src = jnp.zeros((p,), I32).at[pos].set(jnp.arange(n, dtype=I32))
    tstart = jnp.arange(n_tiles, dtype=I32) * tm
    tb = jnp.sum((ends[None, :] <= tstart[:, None]).astype(I32), axis=1)
    n_used = ends[-1] // tm
    valid = (jnp.arange(n_tiles, dtype=I32) < n_used).astype(I32)
    last_b = tb[jnp.maximum(n_used - 1, 0)]
    tb = jnp.clip(jnp.where(valid > 0, tb, last_b), 0, N_BUCKETS - 1)
    ea = jnp.asarray(_EA_TABLE)[tb]
    eb = jnp.asarray(_EB_TABLE)[tb]
    xs = jnp.take(hx, src, axis=0)
    ys = _moe_call(ea, eb, valid, xs, wg, wu, wd)
    return jnp.take(ys, pos, axis=0)


def _final_kernel(h_ref, y_ref, g_ref, gain_ref, o_ref):
    bb, tt, d = h_ref.shape
    h = h_ref[...] + g_ref[...] * y_ref[...].reshape(bb, tt, d)
    o_ref[...] = _rms(h) * gain_ref[...]


def _final_call(h, y2d, off, g2, gain, *, bb, tt, name):
    b, t, d = h.shape
    nti = t // tt
    rows = bb * tt
    xspec = pl.BlockSpec((bb, tt, d), lambda bi, i: (bi, i, 0))
    return pl.pallas_call(
        _final_kernel,
        grid=(b // bb, nti),
        in_specs=[xspec, pl.BlockSpec((rows, d), lambda bi, i: (off + bi * nti + i, 0)),
                  pl.BlockSpec((bb, 1, d), lambda bi, i: (bi, 0, 0)),
                  pl.BlockSpec((1, 1, d), lambda bi, i: (0, 0, 0))],
        out_specs=xspec,
        out_shape=jax.ShapeDtypeStruct((b, t, d), F32),
        compiler_params=_cparams(("parallel", "parallel")),
        name=name,
    )(h, y2d, g2, gain.reshape(1, 1, d))


def _group_cfg(b, t):
    if t >= 512:
        return dict(bb=1, tt=512, L=CHUNK, CB=4)
    return dict(bb=b, tt=t, L=min(CHUNK, t), CB=1)


def _rope_tables(pos0, t):
    half = DK_C // 2
    inv = jnp.power(ROPE_BASE, -jnp.linspace(0.0, 1.0, half, dtype=F32))
    ang = (pos0 + jnp.arange(t, dtype=F32))[:, None] * inv[None, :]
    return jnp.cos(ang), jnp.sin(ang)


def kernel(x_prompt, x_sample, c_prompt, c_sample, state_a_conv, state_a_rec, state_b_c, state_b_n, state_b_m,
           state_c_rec, w_mod, b_mod, norm_mix, norm_ffn, w_in_ab, conv_a, a_log, dt_bias, norm_a, gate_bias_b,
           norm_b, w_out_ab, w_in_c, w_out_c, w_router, router_bias, w_gate, w_up, w_down, norm_final):
    d = D_MODEL
    bp, tp, _ = x_prompt.shape
    bs, ts, _ = x_sample.shape
    n_ab, n_c = w_in_ab.shape[0], w_in_c.shape[0]

    mod_all = _mod_call(jnp.concatenate([c_prompt, c_sample], axis=0), w_mod, b_mod)

    o = np.cumsum((0,) + AB_SIZES)
    main_cols = np.concatenate([np.arange(o[0], o[1]), np.arange(o[3], o[4]), np.arange(o[4], o[5]),
                                np.arange(o[5], o[6]), np.arange(o[6], o[7]), np.arange(o[8], o[9])])
    gate_cols = np.concatenate([np.arange(o[1], o[2]), np.arange(o[2], o[3]), np.arange(o[7], o[8])])
    w_ab_main = w_in_ab[:, :, main_cols].astype(BF16)
    w_ab_gate = w_in_ab[:, :, gate_cols]
    w_ab_gate_pad = jnp.pad(w_ab_gate, ((0, 0), (0, 0), (0, LANES - N_GATE_COLS)))
    w_ab_gate_t = jnp.swapaxes(w_ab_gate, 1, 2)
    w_c = w_in_c.astype(BF16)
    w_out_ab_b = w_out_ab.astype(BF16)
    w_out_c_b = w_out_c.astype(BF16)
    wg_b, wu_b, wd_b = w_gate.astype(BF16), w_up.astype(BF16), w_down.astype(BF16)
    wrt = w_router.T
    rb = router_bias.reshape(N_EXPERTS, 1)
    zeros4 = jnp.zeros((n_ab, H_A), F32)
    gpar_row0 = jnp.concatenate([a_log, jnp.zeros((n_ab, LANES - H_A), F32)], axis=1)
    gpar_row1 = jnp.concatenate([dt_bias, zeros4, gate_bias_b, jnp.zeros((n_ab, LANES - 16), F32)], axis=1)
    gpar = jnp.concatenate([gpar_row0[:, None], gpar_row1[:, None], jnp.zeros((n_ab, 6, LANES), F32)], axis=1)
    gpart = jnp.swapaxes(jnp.concatenate([gpar[:, :2, :N_GATE_COLS], jnp.zeros((n_ab, LANES - 2, N_GATE_COLS), F32)],
                                         axis=1), 1, 2)

    groups = []
    zf = lambda *s: jnp.zeros(s, F32)
    groups.append(dict(
        x=x_prompt, b=bp, t=tp, pos0=0.0, mod=mod_all[:, :bp], row0=0,
        conv=zf(n_ab, bp, CONV_W - 1, QKV_A), sa=zf(n_ab, bp, H_A, DK_A, DV_A), cb=zf(n_ab, bp, H_B, DQK_B, DV_B),
        nb=zf(n_ab, bp, H_B, DQK_B), mb=zf(n_ab, bp, H_B), sc=zf(n_c, bp, H_C, DK_C, DV_C)))
    groups.append(dict(
        x=x_sample, b=bs, t=ts, pos0=float(PAST_LEN), mod=mod_all[:, bp:], row0=bp * tp,
        conv=state_a_conv, sa=state_a_rec, cb=state_b_c, nb=state_b_n, mb=state_b_m, sc=state_c_rec))
    n_tot = bp * tp + bs * ts
    for g in groups:
        g.update(_group_cfg(g["b"], g["t"]))
        g["h"] = g["x"]
        g["prev"] = None
        g["new_ab"], g["new_c"] = [], []
        g["cos"], g["sin"] = _rope_tables(g["pos0"], g["t"])
        g["roff"] = g["row0"] // (g["bb"] * g["tt"])

    for layer in range(DEPTH):
        li = layer // 2
        hx_buf = jnp.zeros((n_tot, HX_W), F32)
        bkts = []
        for gi, g in enumerate(groups):
            b, t, bb, tt = g["b"], g["t"], g["bb"], g["tt"]
            mods = [m.reshape(b, 1, d) for m in jnp.split(g["mod"][layer], N_MOD, axis=-1)]
            sh1, sc1, g1, sh2, sc2, g2 = mods
            tag = f"l{layer}g{gi}"
            if layer % 2 == 0:
                outs = _ln_mm_call(g["h"], g["prev"], norm_mix[layer], sc1, sh1, w_ab_main[li],
                                   (w_ab_gate_pad[li], w_ab_gate_t[li]), bb=bb, tt=tt, tn=512, out_dtype=F32,
                                   name="ln_ab_" + tag)
                if g["prev"] is not None:
                    g["h"], outs = outs[0], outs[1:]
                proj, gates, gates_t = outs
                gates_t = gates_t.reshape(N_GATE_COLS, b, t).transpose(1, 0, 2)
                mixed, conv_n, sa_n, cb_n, nb_n, mb_n = _mixer_ab_call(
                    proj, gates, gates_t, g["conv"][li], g["sa"][li], g["cb"][li],
                    g["nb"][li].reshape(b, H_B, 1, DQK_B), g["mb"][li].reshape(b, H_B, 1, 1),
                    conv_a[li], gpar[li], gpart[li], norm_a[li].reshape(1, DV_A), norm_b[li].reshape(1, DV_B),
                    L=g["L"], CB=g["CB"], name="mixer_ab_" + tag)
                g["new_ab"].append((conv_n, sa_n, cb_n, nb_n.reshape(b, H_B, DQK_B), mb_n.reshape(b, H_B)))
                wout = w_out_ab_b[li]
            else:
                outs = _ln_mm_call(g["h"], g["prev"], norm_mix[layer], sc1, sh1, w_c[li], None,
                                   bb=bb, tt=tt, tn=512, out_dtype=BF16, name="ln_c_" + tag)
                if g["prev"] is not None:
                    g["h"], outs = outs[0], outs[1:]
                (proj,) = outs
                mixed, sc_n = _mixer_c_call(proj, g["cos"], g["sin"], g["sc"][li], L=g["L"], CB=g["CB"],
                                            name="mixer_c_" + tag)
                g["new_c"].append(sc_n)
                wout = w_out_c_b[li]
            g["h"], hx_buf, bkt = _out_route_call(mixed, wout, g["h"], g1, norm_ffn[layer], sc2, sh2, wrt, rb,
                                                  hx_buf, g["roff"], bb=bb, tt=tt, name="out_route_" + tag)
            bkts.append(bkt.reshape(-1))
            g["g2"] = g2
        y_moe = _moe(hx_buf, jnp.concatenate(bkts), wg_b[layer], wu_b[layer], wd_b[layer])
        for g in groups:
            g["prev"] = (y_moe, g["roff"], g["g2"])

    outs = []
    for gi, g in enumerate(groups):
        y2d, off, g2 = g["prev"]
        y = _final_call(g["h"], y2d, off, g2, norm_final, bb=g["bb"], tt=g["tt"], name=f"final_g{gi}")
        ab = [jnp.stack(s) for s in zip(*g["new_ab"])]
        outs.append((y, ab[0], ab[1], ab[2], ab[3], ab[4], jnp.stack(g["new_c"])))
    p, s = outs
    return (p[0], s[0], p[1], p[2], p[3], p[4], p[5], p[6], s[1], s[2], s[3], s[4], s[5], s[6])
```

```python
import functools

import numpy as np
import jax
import jax.numpy as jnp
from jax import lax
from jax.experimental import pallas as pl
from jax.experimental.pallas import tpu as pltpu

F32 = jnp.float32
BF16 = jnp.bfloat16
I32 = jnp.int32

D_MODEL = 1024
DEPTH = 2
CHUNK = 64
H_A, DK_A, DV_A, CONV_W = 4, 128, 128, 4
QKV_A = H_A * (2 * DK_A + DV_A)
H_B, DQK_B, DV_B = 4, 64, 128
H_C, DK_C, DV_C = 4, 256, 512
ROPE_BASE = 10000.0
PAST_LEN = 4096
AB_SIZES = (QKV_A, H_A, H_A, H_A * DV_A, H_B * DQK_B, H_B * DQK_B, H_B * DV_B, 2 * H_B, H_B * DV_B)
MIX_AB = H_A * DV_A + H_B * DV_B
MIX_C = H_C * DV_C
IN_C = 2 * H_C * DK_C + 2 * H_C * DV_C
N_EXPERTS, N_GROUPS, EPG, D_FF = 16, 4, 4, 512
N_MOD = 6
EPS = 1e-6

LANES = 128
AB_MAIN = 3584
N_GATE_COLS = 16
HX_W = D_MODEL + LANES
N_BUCKETS = N_GROUPS * 6
MOE_TM = 256
VMEM_LIMIT = 48 * 1024 * 1024

NN = ((1,), (0,))
NT = ((1,), (1,))
TN = ((0,), (0,))


def _dot(a, b, dims=NN):
    return lax.dot_general(a, b, (dims, ((), ())), preferred_element_type=F32)


def _mmb(a, b, dims=NN):
    return _dot(a.astype(BF16), b.astype(BF16), dims)


def _split2(x):
    hi = x.astype(BF16)
    lo = (x - hi.astype(F32)).astype(BF16)
    return hi, lo


def _split3(x):
    hi = x.astype(BF16)
    r = x - hi.astype(F32)
    mid = r.astype(BF16)
    lo = (r - mid.astype(F32)).astype(BF16)
    return hi, mid, lo


def _mm3(a, b, dims=NN):
    ah, al = _split2(a)
    bh, bl = _split2(b)
    return _dot(ah, bh, dims) + (_dot(ah, bl, dims) + _dot(al, bh, dims))


def _mm_mask_l(mask_bf16, x):
    h, m, l = _split3(x)
    return _dot(mask_bf16, h) + (_dot(mask_bf16, m) + _dot(mask_bf16, l))


def _mm_mask_r(x, mask_bf16):
    h, m, l = _split3(x)
    return _dot(h, mask_bf16) + (_dot(m, mask_bf16) + _dot(l, mask_bf16))


def _sigmoid(x):
    return 1.0 / (1.0 + jnp.exp(-x))


def _silu(x):
    return x * _sigmoid(x)


def _softplus(x):
    return jnp.maximum(x, 0.0) + jnp.log(1.0 + jnp.exp(-jnp.abs(x)))


def _rms(x, eps=EPS):
    return x * lax.rsqrt(jnp.mean(x * x, axis=-1, keepdims=True) + eps)


def _cparams(sem):
    return pltpu.CompilerParams(dimension_semantics=sem, vmem_limit_bytes=VMEM_LIMIT)


def _mod_kernel(c_ref, w_ref, b_ref, o_ref):
    c = c_ref[...]
    o_ref[0] = _mm3(_silu(c), w_ref[0]) + b_ref[0]


def _mod_call(c_all, w_mod, b_mod):
    bt = c_all.shape[0]
    e = w_mod.shape[-1]
    tn = 1024
    return pl.pallas_call(
        _mod_kernel,
        grid=(DEPTH, e // tn),
        in_specs=[pl.BlockSpec((bt, D_MODEL), lambda l, j: (0, 0)),
                  pl.BlockSpec((1, D_MODEL, tn), lambda l, j: (l, 0, j)),
                  pl.BlockSpec((1, 1, tn), lambda l, j: (l, 0, j))],
        out_specs=pl.BlockSpec((1, bt, tn), lambda l, j: (l, 0, j)),
        out_shape=jax.ShapeDtypeStruct((DEPTH, bt, e), F32),
        compiler_params=_cparams(("parallel", "parallel")),
        name="mod",
    )(c_all, w_mod, b_mod.reshape(DEPTH, 1, e))


def _ln_mm_kernel(*refs, has_prev, has_gates):
    it = iter(refs)
    x_ref = next(it)
    if has_prev:
        yp_ref, gp_ref = next(it), next(it)
    gain_ref, sc_ref, sh_ref, w_ref = next(it), next(it), next(it), next(it)
    if has_gates:
        wg_ref, wgt_ref = next(it), next(it)
    if has_prev:
        h_ref = next(it)
    proj_ref = next(it)
    if has_gates:
        g_ref, gt_ref = next(it), next(it)
    hn_s = next(it)

    bb, tt, d = x_ref.shape
    rows = bb * tt

    @pl.when(pl.program_id(2) == 0)
    def _():
        x = x_ref[...]
        if has_prev:
            x = x + gp_ref[...] * yp_ref[...].reshape(bb, tt, d)
            h_ref[...] = x
        hn = _rms(x) * gain_ref[...]
        hn = hn * (1.0 + sc_ref[...]) + sh_ref[...]
        hn2 = hn.reshape(rows, d)
        hn_s[...] = hn2.astype(BF16)
        if has_gates:
            g_ref[...] = _mm3(hn2, wg_ref[...]).reshape(bb, tt, LANES)
            gt_ref[...] = _mm3(wgt_ref[...], hn2, NT)

    tn = proj_ref.shape[-1]
    proj_ref[...] = _dot(hn_s[...], w_ref[...]).reshape(bb, tt, tn).astype(proj_ref.dtype)


def _ln_mm_call(x, prev, gain, sc, sh, w, gates_w, *, bb, tt, tn, out_dtype, name):
    b, t, d = x.shape
    e = w.shape[1]
    rows = bb * tt
    nti = t // tt
    grid = (b // bb, nti, e // tn)
    has_prev = prev is not None
    has_gates = gates_w is not None
    xspec = pl.BlockSpec((bb, tt, d), lambda bi, i, j: (bi, i, 0))
    mspec = pl.BlockSpec((bb, 1, d), lambda bi, i, j: (bi, 0, 0))
    in_specs, args = [xspec], [x]
    if has_prev:
        y2d, off, gp = prev
        in_specs += [pl.BlockSpec((rows, d), lambda bi, i, j: (off + bi * nti + i, 0)), mspec]
        args += [y2d, gp]
    in_specs += [pl.BlockSpec((1, 1, d), lambda bi, i, j: (0, 0, 0)), mspec, mspec,
                 pl.BlockSpec((d, tn), lambda bi, i, j: (0, j))]
    args += [gain.reshape(1, 1, d), sc, sh, w]
    if has_gates:
        wg, wgt = gates_w
        in_specs += [pl.BlockSpec((d, LANES), lambda bi, i, j: (0, 0)),
                     pl.BlockSpec((N_GATE_COLS, d), lambda bi, i, j: (0, 0))]
        args += [wg, wgt]
    out_specs, out_shape = [], []
    if has_prev:
        out_specs.append(xspec)
        out_shape.append(jax.ShapeDtypeStruct((b, t, d), F32))
    out_specs.append(pl.BlockSpec((bb, tt, tn), lambda bi, i, j: (bi, i, j)))
    out_shape.append(jax.ShapeDtypeStruct((b, t, e), out_dtype))
    if has_gates:
        out_specs += [pl.BlockSpec((bb, tt, LANES), lambda bi, i, j: (bi, i, 0)),
                      pl.BlockSpec((N_GATE_COLS, rows), lambda bi, i, j: (0, bi * nti + i))]
        out_shape += [jax.ShapeDtypeStruct((b, t, LANES), F32),
                      jax.ShapeDtypeStruct((N_GATE_COLS, b * t), F32)]
    return pl.pallas_call(
        functools.partial(_ln_mm_kernel, has_prev=has_prev, has_gates=has_gates),
        grid=grid, in_specs=in_specs, out_specs=out_specs, out_shape=out_shape,
        scratch_shapes=[pltpu.VMEM((rows, d), BF16)],
        compiler_params=_cparams(("parallel", "parallel", "arbitrary")),
        name=name,
    )(*args)


def _tri_inv_all(a_list, eye, length, mm):
    xs = [-a for a in a_list]
    ps = [eye + x for x in xs]
    n = 2
    while n < length:
        xs = [mm(x, x) for x in xs]
        ps = [p + mm(p, x) for p, x in zip(ps, xs)]
        n *= 2
    return ps


def _mixer_ab_kernel(qkv_ref, z_ref, qkb_ref, vb_ref, ob_ref, g_ref, gt_ref,
                     conv0_ref, sa0_ref, cb0_ref, nb0_ref, mb0_ref,
                     convw_ref, gpar_ref, gpart_ref, na_ref, nbn_ref,
                     mixed_ref, conv_o, sa_o, cb_o, nb_o, mb_o,
                     xbuf, *, L, CB):
    TB = L * CB

    @pl.when(pl.program_id(1) == 0)
    def _():
        xbuf[5:8, :] = conv0_ref[0]
        sa_o[...] = sa0_ref[...]
        cb_o[...] = cb0_ref[...]
        nb_o[...] = nb0_ref[...]
        mb_o[...] = mb0_ref[...]

    xbuf[8:8 + TB, :] = qkv_ref[0]
    conv_o[0] = xbuf[5 + TB:8 + TB, :]

    ii = lax.broadcasted_iota(I32, (L, L), 0)
    jj = lax.broadcasted_iota(I32, (L, L), 1)
    tri = ii >= jj
    strict = ii > jj
    eye = (ii == jj).astype(F32)
    tril_b = tri.astype(BF16)
    triu_b = (ii <= jj).astype(BF16)
    lane = lax.broadcasted_iota(I32, (L, LANES), 1)
    rowi = lax.broadcasted_iota(I32, (N_GATE_COLS, L), 0)
    gpar = gpar_ref[...]
    gpart = gpart_ref[...]
    nea = -jnp.exp(gpar[0:1, :])
    neat = -jnp.exp(gpart[:, 0:1])
    norm_a = na_ref[...]
    norm_b = nbn_ref[...]

    mm = _mmb
    mm_inv = _mm3

    def conv_tile(r0, col0):
        acc = None
        for j in range(CONV_W):
            term = convw_ref[j:j + 1, col0:col0 + DK_A] * xbuf[5 + j + r0:5 + j + r0 + L, col0:col0 + DK_A]
            acc = term if acc is None else acc + term
        return _silu(acc)

    gates = []
    for c in range(CB):
        r0 = c * L
        pre = g_ref[0, r0:r0 + L, :] + gpar[1:2, :]
        csrc = jnp.where(lane < 4, nea * _softplus(pre),
                         jnp.where(lane >= 12, -_softplus(-pre), 0.0))
        csrc = jnp.where(lane < N_GATE_COLS, csrc, 0.0)
        gcum = _mm_mask_l(tril_b, csrc)
        pret = gt_ref[0, :, r0:r0 + L] + gpart[:, 1:2]
        csrct = jnp.where(rowi < 4, neat * _softplus(pret),
                          jnp.where(rowi >= 12, -_softplus(-pret), 0.0))
        gcumt = _mm_mask_r(csrct, triu_b)
        gates.append((pre, pret, gcum, gcumt, _sigmoid(pre)))

    ga = []
    a_list = []
    for c in range(CB):
        r0 = c * L
        pre, pret, gcum, gcumt, beta_all = gates[c]
        for h in range(H_A):
            q = conv_tile(r0, h * DK_A)
            k = conv_tile(r0, H_A * DK_A + h * DK_A)
            v = conv_tile(r0, 2 * H_A * DK_A + h * DV_A)
            q = q * lax.rsqrt(jnp.sum(q * q, axis=-1, keepdims=True) + EPS) * (DK_A ** -0.5)
            k = k * lax.rsqrt(jnp.sum(k * k, axis=-1, keepdims=True) + EPS)
            gc_c = gcum[:, h:h + 1]
            gc_r = gcumt[h:h + 1, :]
            decay = jnp.where(tri, jnp.exp(jnp.where(tri, gc_c - gc_r, 0.0)), 0.0)
            beta_c = beta_all[:, 4 + h:5 + h]
            kb = k * beta_c
            a_list.append(jnp.where(strict, mm_inv(kb, k, NT) * decay, 0.0))
            eg = jnp.exp(gc_c)
            gl = gc_c[L - 1:L, :]
            attn = jnp.where(tri, mm(q, k, NT) * decay, 0.0)
            ga.append(dict(kbeg=kb * eg, vb=v * beta_c, qeg=q * eg, attn=attn,
                           kdec=k * jnp.exp(gl - gc_c), sdec=jnp.exp(gl)))
    tinv = _tri_inv_all(a_list, eye, L, mm_inv)
    for d, t in zip(ga, tinv):
        d["w"] = mm_inv(t, d.pop("kbeg"))
        d["u"] = mm_inv(t, d.pop("vb"))

    gb = []
    for c in range(CB):
        r0 = c * L
        pre, pret, gcum, gcumt, _ = gates[c]
        for h in range(H_B):
            q = qkb_ref[0, r0:r0 + L, h * DQK_B:(h + 1) * DQK_B] * (DQK_B ** -0.5)
            k = qkb_ref[0, r0:r0 + L, H_B * DQK_B + h * DQK_B:H_B * DQK_B + (h + 1) * DQK_B]
            b_c = gcum[:, 12 + h:13 + h]
            b_r = gcumt[12 + h:13 + h, :]
            li_r = pret[8 + h:9 + h, :]
            dm = jnp.where(tri, b_c - b_r + li_r, -jnp.inf)
            gb.append(dict(q=q, k=k, qk=mm(q, k, NT), dm=dm, dmax=jnp.max(dm, axis=-1, keepdims=True),
                           b_c=b_c, li_c=pre[:, 8 + h:9 + h]))

    for c in range(CB):
        r0 = c * L
        for h in range(H_A):
            d = ga[c * H_A + h]
            s = sa_o[0, h]
            v_new = d["u"] - mm(d["w"], s)
            o = mm(d["qeg"], s) + mm(d["attn"], v_new)
            sa_o[0, h] = s * d["sdec"] + mm(d["kdec"], v_new, TN)
            zg = z_ref[0, r0:r0 + L, h * DV_A:(h + 1) * DV_A]
            o = _rms(o) * norm_a * _silu(zg)
            mixed_ref[0, r0:r0 + L, h * DV_A:(h + 1) * DV_A] = o.astype(mixed_ref.dtype)
        for h in range(H_B):
            d = gb[c * H_B + h]
            q, k, b_c = d["q"], d["k"], d["b_c"]
            v = vb_ref[0, r0:r0 + L, h * DV_B:(h + 1) * DV_B]
            m_prev = mb_o[0, h]
            a0 = b_c + m_prev
            m_t = jnp.maximum(a0, d["dmax"])
            w0 = jnp.exp(a0 - m_t)
            sm = d["qk"] * jnp.exp(d["dm"] - m_t)
            cst = cb_o[0, h]
            nst = nb_o[0, h]
            num = w0 * mm(q, cst) + mm(sm, v)
            den = w0 * jnp.sum(q * nst, axis=-1, keepdims=True) + jnp.sum(sm, axis=-1, keepdims=True)
            hh = num / jnp.maximum(jnp.abs(den), jnp.exp(-m_t))
            m_new = m_t[L - 1:L, :]
            wk = jnp.exp(b_c[L - 1:L, :] - b_c + d["li_c"] - m_new)
            w0l = w0[L - 1:L, :]
            kw = k * wk
            cb_o[0, h] = w0l * cst + mm(kw, v, TN)
            nb_o[0, h] = w0l * nst + jnp.sum(kw, axis=0, keepdims=True)
            mb_o[0, h] = m_new
            og = ob_ref[0, r0:r0 + L, h * DV_B:(h + 1) * DV_B]
            o = _rms(hh) * norm_b * _sigmoid(og)
            c0 = H_A * DV_A + h * DV_B
            mixed_ref[0, r0:r0 + L, c0:c0 + DV_B] = o.astype(mixed_ref.dtype)

    xbuf[5:8, :] = xbuf[5 + TB:8 + TB, :]


def _mixer_ab_call(proj, gates, gates_t, conv0, sa0, cb0, nb0, mb0, convw, gpar, gpart, norm_a, norm_b, *, L, CB, name):
    b, t, _ = proj.shape
    tb = L * CB
    nt = t // tb

    def col(width, idx):
        return pl.BlockSpec((1, tb, width), lambda bi, i: (bi, i, idx))

    def const(shape):
        return pl.BlockSpec(shape, lambda bi, i: (0,) * len(shape))

    def state(shape):
        return pl.BlockSpec((1,) + shape, lambda bi, i: (bi,) + (0,) * len(shape))

    in_specs = [col(QKV_A, 0), col(512, 3), col(512, 4), col(512, 5), col(512, 6),
                col(LANES, 0), pl.BlockSpec((1, N_GATE_COLS, tb), lambda bi, i: (bi, 0, i)),
                state((CONV_W - 1, QKV_A)), state((H_A, DK_A, DV_A)), state((H_B, DQK_B, DV_B)),
                state((H_B, 1, DQK_B)), state((H_B, 1, 1)),
                const((CONV_W, QKV_A)), const((8, LANES)), const((N_GATE_COLS, LANES)),
                const((1, DV_A)), const((1, DV_B))]
    out_specs = [pl.BlockSpec((1, tb, MIX_AB), lambda bi, i: (bi, i, 0)),
                 state((CONV_W - 1, QKV_A)), state((H_A, DK_A, DV_A)), state((H_B, DQK_B, DV_B)),
                 state((H_B, 1, DQK_B)), state((H_B, 1, 1))]
    out_shape = [jax.ShapeDtypeStruct((b, t, MIX_AB), BF16),
                 jax.ShapeDtypeStruct((b, CONV_W - 1, QKV_A), F32),
                 jax.ShapeDtypeStruct((b, H_A, DK_A, DV_A), F32),
                 jax.ShapeDtypeStruct((b, H_B, DQK_B, DV_B), F32),
                 jax.ShapeDtypeStruct((b, H_B, 1, DQK_B), F32),
                 jax.ShapeDtypeStruct((b, H_B, 1, 1), F32)]
    return pl.pallas_call(
        functools.partial(_mixer_ab_kernel, L=L, CB=CB),
        grid=(b, nt), in_specs=in_specs, out_specs=out_specs, out_shape=out_shape,
        scratch_shapes=[pltpu.VMEM((8 + tb, QKV_A), F32)],
        compiler_params=_cparams(("parallel", "arbitrary")),
        name=name,
    )(proj, proj, proj, proj, proj, gates, gates_t, conv0, sa0, cb0, nb0, mb0, convw, gpar, gpart, norm_a, norm_b)


def _mixer_c_kernel(q_ref, k_ref, v_ref, g_ref, cos_ref, sin_ref, s0_ref, mixed_ref, s_o, *, L, CB):
    @pl.when(pl.program_id(1) == 0)
    def _():
        s_o[...] = s0_ref[...]

    ii = lax.broadcasted_iota(I32, (L, L), 0)
    jj = lax.broadcasted_iota(I32, (L, L), 1)
    rel = (ii - jj).astype(F32)
    idx = lax.broadcasted_iota(I32, (L, 1), 0).astype(F32)
    half = DK_C // 2

    def rope(x, cos, sin):
        x1, x2 = x[:, :half], x[:, half:]
        return jnp.concatenate([x1 * cos - x2 * sin, x1 * sin + x2 * cos], axis=-1)

    for c in range(CB):
        r0 = c * L
        cos = cos_ref[r0:r0 + L, :]
        sin = sin_ref[r0:r0 + L, :]
        for h in range(H_C):
            lg = float(np.log1p(-np.exp2(-5.0 - h)))
            intra = jnp.where(rel >= 0, jnp.exp(lg * jnp.maximum(rel, 0.0)), 0.0)
            q_decay = jnp.exp(lg * (idx + 1.0))
            k_decay = jnp.exp(lg * (L - 1.0 - idx))
            s_decay = float(np.exp(lg * L))
            q = rope(q_ref[0, r0:r0 + L, h * DK_C:(h + 1) * DK_C].astype(F32), cos, sin)
            k = rope(k_ref[0, r0:r0 + L, h * DK_C:(h + 1) * DK_C].astype(F32), cos, sin) * (DK_C ** -0.5)
            v = v_ref[0, r0:r0 + L, h * DV_C:(h + 1) * DV_C]
            s = s_o[0, h]
            o = _mmb(_mmb(q, k, NT) * intra, v) + _mmb(q * q_decay, s)
            s_o[0, h] = s_decay * s + _mmb(k * k_decay, v, TN)
            gg = g_ref[0, r0:r0 + L, h * DV_C:(h + 1) * DV_C].astype(F32)
            o = _rms(o) * _silu(gg)
            mixed_ref[0, r0:r0 + L, h * DV_C:(h + 1) * DV_C] = o.astype(mixed_ref.dtype)


def _mixer_c_call(proj, cos, sin, s0, *, L, CB, name):
    b, t, _ = proj.shape
    tb = L * CB
    nt = t // tb
    qk_w, v_w = H_C * DK_C, H_C * DV_C
    sspec = pl.BlockSpec((1, H_C, DK_C, DV_C), lambda bi, i: (bi, 0, 0, 0))
    tspec = pl.BlockSpec((tb, DK_C // 2), lambda bi, i: (i, 0))
    return pl.pallas_call(
        functools.partial(_mixer_c_kernel, L=L, CB=CB),
        grid=(b, nt),
        in_specs=[pl.BlockSpec((1, tb, qk_w), lambda bi, i: (bi, i, 0)),
                  pl.BlockSpec((1, tb, qk_w), lambda bi, i: (bi, i, 1)),
                  pl.BlockSpec((1, tb, v_w), lambda bi, i: (bi, i, 1)),
                  pl.BlockSpec((1, tb, v_w), lambda bi, i: (bi, i, 2)),
                  tspec, tspec, sspec],
        out_specs=[pl.BlockSpec((1, tb, MIX_C), lambda bi, i: (bi, i, 0)), sspec],
        out_shape=[jax.ShapeDtypeStruct((b, t, MIX_C), BF16),
                   jax.ShapeDtypeStruct((b, H_C, DK_C, DV_C), F32)],
        compiler_params=_cparams(("parallel", "arbitrary")),
        name=name,
    )(proj, proj, proj, proj, cos, sin, s0)


def _out_route_kernel(mixed_ref, wout_ref, h_ref, g1_ref, gain_ref, sc_ref, sh_ref, wrt_ref, rb_ref, hx_in_ref,
                      hnew_ref, hx_ref, bkt_ref):
    del hx_in_ref
    bb, tt, m = mixed_ref.shape
    d = h_ref.shape[-1]
    rows = bb * tt
    y = _dot(mixed_ref[...].reshape(rows, m), wout_ref[...])
    h = h_ref[...] + g1_ref[...] * y.reshape(bb, tt, d)
    hnew_ref[...] = h
    hn = _rms(h) * gain_ref[...]
    hn = (hn * (1.0 + sc_ref[...]) + sh_ref[...]).reshape(rows, d)
    hx_ref[:, 0:d] = hn

    logits = _mm3(wrt_ref[...], hn, NT)
    score = _sigmoid(logits)
    sel = score + rb_ref[...]

    def row(a, r):
        return a[r:r + 1, :]

    gscore = []
    for g in range(N_GROUPS):
        a, b, c, e = (row(sel, EPG * g + i) for i in range(EPG))
        hi_ab, lo_ab, hi_ce, lo_ce = jnp.maximum(a, b), jnp.minimum(a, b), jnp.maximum(c, e), jnp.minimum(c, e)
        top1 = jnp.maximum(hi_ab, hi_ce)
        top2 = jnp.maximum(jnp.maximum(lo_ab, lo_ce), jnp.minimum(hi_ab, hi_ce))
        gscore.append(top1 + top2)
    best = jnp.zeros((1, rows), I32)
    bestv = gscore[0]
    for g in range(1, N_GROUPS):
        upd = gscore[g] > bestv
        best = jnp.where(upd, g, best)
        bestv = jnp.where(upd, gscore[g], bestv)

    def pick(a, i):
        out = row(a, i)
        for g in range(1, N_GROUPS):
            out = jnp.where(best == g, row(a, EPG * g + i), out)
        return out

    vsel = [pick(sel, i) for i in range(EPG)]
    vsc = [pick(score, i) for i in range(EPG)]
    i1 = jnp.zeros((1, rows), I32)
    v1 = vsel[0]
    for i in range(1, EPG):
        upd = vsel[i] > v1
        i1 = jnp.where(upd, i, i1)
        v1 = jnp.where(upd, vsel[i], v1)
    i2 = jnp.full((1, rows), -1, I32)
    v2 = jnp.full((1, rows), -jnp.inf, F32)
    for i in range(EPG):
        cand = jnp.where(i1 == i, -jnp.inf, vsel[i])
        upd = cand > v2
        i2 = jnp.where(upd, i, i2)
        v2 = jnp.where(upd, cand, v2)
    s1 = jnp.zeros((1, rows), F32)
    s2 = jnp.zeros((1, rows), F32)
    for i in range(EPG):
        s1 = jnp.where(i1 == i, vsc[i], s1)
        s2 = jnp.where(i2 == i, vsc[i], s2)
    den = s1 + s2
    w1 = s1 / den
    w2 = s2 / den
    first_lo = i1 < i2
    lo = jnp.minimum(i1, i2)
    hi = jnp.maximum(i1, i2)
    wlo = jnp.where(first_lo, w1, w2)
    whi = jnp.where(first_lo, w2, w1)
    pair = jnp.where(lo == 0, hi - 1, jnp.where(lo == 1, hi + 1, 5))
    bkt_ref[...] = (best * 6 + pair).reshape(1, 1, rows)

    aux = jnp.concatenate([wlo, whi, jnp.zeros((LANES - 2, rows), F32)], axis=0)
    hx_ref[:, d:d + LANES] = aux.T


def _out_route_call(mixed, wout, h, g1, gain, sc, sh, wrt, rb, hx_buf, row_block_off, *, bb, tt, name):
    b, t, m = mixed.shape
    d = h.shape[-1]
    rows = bb * tt
    nti = t // tt
    ntot = hx_buf.shape[0]
    xspec = pl.BlockSpec((bb, tt, d), lambda bi, i: (bi, i, 0))
    mspec = pl.BlockSpec((bb, 1, d), lambda bi, i: (bi, 0, 0))
    return pl.pallas_call(
        _out_route_kernel,
        grid=(b // bb, nti),
        in_specs=[pl.BlockSpec((bb, tt, m), lambda bi, i: (bi, i, 0)),
                  pl.BlockSpec((m, d), lambda bi, i: (0, 0)),
                  xspec, mspec,
                  pl.BlockSpec((1, 1, d), lambda bi, i: (0, 0, 0)), mspec, mspec,
                  pl.BlockSpec((N_EXPERTS, d), lambda bi, i: (0, 0)),
                  pl.BlockSpec((N_EXPERTS, 1), lambda bi, i: (0, 0)),
                  pl.BlockSpec(memory_space=pl.ANY)],
        out_specs=[xspec,
                   pl.BlockSpec((rows, HX_W), lambda bi, i: (row_block_off + bi * nti + i, 0)),
                   pl.BlockSpec((1, 1, rows), lambda bi, i: (bi * nti + i, 0, 0))],
        out_shape=[jax.ShapeDtypeStruct((b, t, d), F32),
                   jax.ShapeDtypeStruct((ntot, HX_W), F32),
                   jax.ShapeDtypeStruct(((b // bb) * nti, 1, rows), I32)],
        input_output_aliases={9: 1},
        compiler_params=_cparams(("parallel", "parallel")),
        name=name,
    )(mixed, wout, h, g1, gain.reshape(1, 1, d), sc, sh, wrt, rb, hx_buf)


def _moe_kernel(ea_ref, eb_ref, valid_ref, xs_ref, wga_ref, wgb_ref, wua_ref, wub_ref, wda_ref, wdb_ref, ys_ref):
    t = pl.program_id(0)

    @pl.when(valid_ref[t] > 0)
    def _():
        x = xs_ref[:, 0:D_MODEL].astype(BF16)
        wlo = xs_ref[:, D_MODEL:D_MODEL + 1]
        whi = xs_ref[:, D_MODEL + 1:D_MODEL + 2]
        act_a = (_silu(_dot(x, wga_ref[0])) * _dot(x, wua_ref[0]) * wlo).astype(BF16)
        act_b = (_silu(_dot(x, wgb_ref[0])) * _dot(x, wub_ref[0]) * whi).astype(BF16)
        ys_ref[...] = _dot(act_a, wda_ref[0]) + _dot(act_b, wdb_ref[0])

    @pl.when(valid_ref[t] == 0)
    def _():
        ys_ref[...] = jnp.zeros_like(ys_ref)


def _moe_call(ea, eb, valid, xs, wg, wu, wd):
    p = xs.shape[0]
    n_tiles = p // MOE_TM
    wa_in = pl.BlockSpec((1, D_MODEL, D_FF), lambda t, ea, eb, va: (ea[t], 0, 0))
    wb_in = pl.BlockSpec((1, D_MODEL, D_FF), lambda t, ea, eb, va: (eb[t], 0, 0))
    wa_out = pl.BlockSpec((1, D_FF, D_MODEL), lambda t, ea, eb, va: (ea[t], 0, 0))
    wb_out = pl.BlockSpec((1, D_FF, D_MODEL), lambda t, ea, eb, va: (eb[t], 0, 0))
    return pl.pallas_call(
        _moe_kernel,
        grid_spec=pltpu.PrefetchScalarGridSpec(
            num_scalar_prefetch=3, grid=(n_tiles,),
            in_specs=[pl.BlockSpec((MOE_TM, HX_W), lambda t, ea, eb, va: (t, 0)),
                      wa_in, wb_in, wa_in, wb_in, wa_out, wb_out],
            out_specs=pl.BlockSpec((MOE_TM, D_MODEL), lambda t, ea, eb, va: (t, 0))),
        out_shape=jax.ShapeDtypeStruct((p, D_MODEL), F32),
        compiler_params=_cparams(("arbitrary",)),
        name="moe",
    )(ea, eb, valid, xs, wg, wg, wu, wu, wd, wd)


_PAIRS = [(0, 1), (0, 2), (0, 3), (1, 2), (1, 3), (2, 3)]
_EA_TABLE = np.array([g * EPG + p[0] for g in range(N_GROUPS) for p in _PAIRS], np.int32)
_EB_TABLE = np.array([g * EPG + p[1] for g in range(N_GROUPS) for p in _PAIRS], np.int32)


def _moe(hx, bkt, wg, wu, wd):
    n = hx.shape[0]
    tm = MOE_TM
    ar = jnp.arange(N_BUCKETS, dtype=I32)
    oh = (bkt[:, None] == ar[None, :]).astype(I32)
    cs = jnp.cumsum(oh, axis=0)
    rank = jnp.sum(oh * cs, axis=1) - 1
    counts = cs[-1]
    padded = ((counts + tm - 1) // tm) * tm
    ends = jnp.cumsum(padded)
    offs = ends - padded
    pos = jnp.sum(oh * offs[None, :], axis=1) + rank
    n_tiles = -(-n // tm) + N_BUCKETS
    p = n_tiles * tm
    src = jnp.zeros((p,), I32).at[pos].set(jnp.arange(n, dtype=I32))
    tstart = jnp.arange(n_tiles, dtype=I32) * tm
    tb = jnp.sum((ends[None, :] <= tstart[:, None]).astype(I32), axis=1)
    n_used = ends[-1] // tm
    valid = (jnp.arange(n_tiles, dtype=I32) < n_used).astype(I32)
    last_b = tb[jnp.maximum(n_used - 1, 0)]
    tb = jnp.clip(jnp.where(valid > 0, tb, last_b), 0, N_BUCKETS - 1)
    ea = jnp.asarray(_EA_TABLE)[tb]
    eb = jnp.asarray(_EB_TABLE)[tb]
    xs = jnp.take(hx, src, axis=0)
    ys = _moe_call(ea, eb, valid, xs, wg, wu, wd)
    return jnp.take(ys, pos, axis=0)


def _final_kernel(h_ref, y_ref, g_ref, gain_ref, o_ref):
    bb, tt, d = h_ref.shape
    h = h_ref[...] + g_ref[...] * y_ref[...].reshape(bb, tt, d)
    o_ref[...] = _rms(h) * gain_ref[...]


def _final_call(h, y2d, off, g2, gain, *, bb, tt, name):
    b, t, d = h.shape
    nti = t // tt
    rows = bb * tt
    xspec = pl.BlockSpec((bb, tt, d), lambda bi, i: (bi, i, 0))
    return pl.pallas_call(
        _final_kernel,
        grid=(b // bb, nti),
        in_specs=[xspec, pl.BlockSpec((rows, d), lambda bi, i: (off + bi * nti + i, 0)),
                  pl.BlockSpec((bb, 1, d), lambda bi, i: (bi, 0, 0)),
                  pl.BlockSpec((1, 1, d), lambda bi, i: (0, 0, 0))],
        out_specs=xspec,
        out_shape=jax.ShapeDtypeStruct((b, t, d), F32),
        compiler_params=_cparams(("parallel", "parallel")),
        name=name,
    )(h, y2d, g2, gain.reshape(1, 1, d))


def _group_cfg(b, t):
    if t >= 512:
        return dict(bb=1, tt=512, L=CHUNK, CB=4)
    return dict(bb=b, tt=t, L=min(CHUNK, t), CB=1)


def _rope_tables(pos0, t):
    half = DK_C // 2
    inv = jnp.power(ROPE_BASE, -jnp.linspace(0.0, 1.0, half, dtype=F32))
    ang = (pos0 + jnp.arange(t, dtype=F32))[:, None] * inv[None, :]
    return jnp.cos(ang), jnp.sin(ang)


def kernel(x_prompt, x_sample, c_prompt, c_sample, state_a_conv, state_a_rec, state_b_c, state_b_n, state_b_m,
           state_c_rec, w_mod, b_mod, norm_mix, norm_ffn, w_in_ab, conv_a, a_log, dt_bias, norm_a, gate_bias_b,
           norm_b, w_out_ab, w_in_c, w_out_c, w_router, router_bias, w_gate, w_up, w_down, norm_final):
    d = D_MODEL
    bp, tp, _ = x_prompt.shape
    bs, ts, _ = x_sample.shape
    n_ab, n_c = w_in_ab.shape[0], w_in_c.shape[0]

    mod_all = _mod_call(jnp.concatenate([c_prompt, c_sample], axis=0), w_mod, b_mod)

    o = np.cumsum((0,) + AB_SIZES)
    main_cols = np.concatenate([np.arange(o[0], o[1]), np.arange(o[3], o[4]), np.arange(o[4], o[5]),
                                np.arange(o[5], o[6]), np.arange(o[6], o[7]), np.arange(o[8], o[9])])
    gate_cols = np.concatenate([np.arange(o[1], o[2]), np.arange(o[2], o[3]), np.arange(o[7], o[8])])
    w_ab_main = w_in_ab[:, :, main_cols].astype(BF16)
    w_ab_gate = w_in_ab[:, :, gate_cols]
    w_ab_gate_pad = jnp.pad(w_ab_gate, ((0, 0), (0, 0), (0, LANES - N_GATE_COLS)))
    w_ab_gate_t = jnp.swapaxes(w_ab_gate, 1, 2)
    w_c = w_in_c.astype(BF16)
    w_out_ab_b = w_out_ab.astype(BF16)
    w_out_c_b = w_out_c.astype(BF16)
    wg_b, wu_b, wd_b = w_gate.astype(BF16), w_up.astype(BF16), w_down.astype(BF16)
    wrt = w_router.T
    rb = router_bias.reshape(N_EXPERTS, 1)
    zeros4 = jnp.zeros((n_ab, H_A), F32)
    gpar_row0 = jnp.concatenate([a_log, jnp.zeros((n_ab, LANES - H_A), F32)], axis=1)
    gpar_row1 = jnp.concatenate([dt_bias, zeros4, gate_bias_b, jnp.zeros((n_ab, LANES - 16), F32)], axis=1)
    gpar = jnp.concatenate([gpar_row0[:, None], gpar_row1[:, None], jnp.zeros((n_ab, 6, LANES), F32)], axis=1)
    gpart = jnp.swapaxes(jnp.concatenate([gpar[:, :2, :N_GATE_COLS], jnp.zeros((n_ab, LANES - 2, N_GATE_COLS), F32)],
                                         axis=1), 1, 2)

    groups = []
    zf = lambda *s: jnp.zeros(s, F32)
    groups.append(dict(
        x=x_prompt, b=bp, t=tp, pos0=0.0, mod=mod_all[:, :bp], row0=0,
        conv=zf(n_ab, bp, CONV_W - 1, QKV_A), sa=zf(n_ab, bp, H_A, DK_A, DV_A), cb=zf(n_ab, bp, H_B, DQK_B, DV_B),
        nb=zf(n_ab, bp, H_B, DQK_B), mb=zf(n_ab, bp, H_B), sc=zf(n_c, bp, H_C, DK_C, DV_C)))
    groups.append(dict(
        x=x_sample, b=bs, t=ts, pos0=float(PAST_LEN), mod=mod_all[:, bp:], row0=bp * tp,
        conv=state_a_conv, sa=state_a_rec, cb=state_b_c, nb=state_b_n, mb=state_b_m, sc=state_c_rec))
    n_tot = bp * tp + bs * ts
    for g in groups:
        g.update(_group_cfg(g["b"], g["t"]))
        g["h"] = g["x"]
        g["prev"] = None
        g["new_ab"], g["new_c"] = [], []
        g["cos"], g["sin"] = _rope_tables(g["pos0"], g["t"])
        g["roff"] = g["row0"] // (g["bb"] * g["tt"])

    for layer in range(DEPTH):
        li = layer // 2
        hx_buf = jnp.zeros((n_tot, HX_W), F32)
        bkts = []
        for gi, g in enumerate(groups):
            b, t, bb, tt = g["b"], g["t"], g["bb"], g["tt"]
            mods = [m.reshape(b, 1, d) for m in jnp.split(g["mod"][layer], N_MOD, axis=-1)]
            sh1, sc1, g1, sh2, sc2, g2 = mods
            tag = f"l{layer}g{gi}"
            if layer % 2 == 0:
                outs = _ln_mm_call(g["h"], g["prev"], norm_mix[layer], sc1, sh1, w_ab_main[li],
                                   (w_ab_gate_pad[li], w_ab_gate_t[li]), bb=bb, tt=tt, tn=512, out_dtype=F32,
                                   name="ln_ab_" + tag)
                if g["prev"] is not None:
                    g["h"], outs = outs[0], outs[1:]
                proj, gates, gates_t = outs
                gates_t = gates_t.reshape(N_GATE_COLS, b, t).transpose(1, 0, 2)
                mixed, conv_n, sa_n, cb_n, nb_n, mb_n = _mixer_ab_call(
                    proj, gates, gates_t, g["conv"][li], g["sa"][li], g["cb"][li],
                    g["nb"][li].reshape(b, H_B, 1, DQK_B), g["mb"][li].reshape(b, H_B, 1, 1),
                    conv_a[li], gpar[li], gpart[li], norm_a[li].reshape(1, DV_A), norm_b[li].reshape(1, DV_B),
                    L=g["L"], CB=g["CB"], name="mixer_ab_" + tag)
                g["new_ab"].append((conv_n, sa_n, cb_n, nb_n.reshape(b, H_B, DQK_B), mb_n.reshape(b, H_B)))
                wout = w_out_ab_b[li]
            else:
                outs = _ln_mm_call(g["h"], g["prev"], norm_mix[layer], sc1, sh1, w_c[li], None,
                                   bb=bb, tt=tt, tn=512, out_dtype=BF16, name="ln_c_" + tag)
                if g["prev"] is not None:
                    g["h"], outs = outs[0], outs[1:]
                (proj,) = outs
                mixed, sc_n = _mixer_c_call(proj, g["cos"], g["sin"], g["sc"][li], L=g["L"], CB=g["CB"],
                                            name="mixer_c_" + tag)
                g["new_c"].append(sc_n)
                wout = w_out_c_b[li]
            g["h"], hx_buf, bkt = _out_route_call(mixed, wout, g["h"], g1, norm_ffn[layer], sc2, sh2, wrt, rb,
                                                  hx_buf, g["roff"], bb=bb, tt=tt, name="out_route_" + tag)
            bkts.append(bkt.reshape(-1))
            g["g2"] = g2
        y_moe = _moe(hx_buf, jnp.concatenate(bkts), wg_b[layer], wu_b[layer], wd_b[layer])
        for g in groups:
            g["prev"] = (y_moe, g["roff"], g["g2"])

    outs = []
    for gi, g in enumerate(groups):
        y2d, off, g2 = g["prev"]
        y = _final_call(g["h"], y2d, off, g2, norm_final, bb=g["bb"], tt=g["tt"], name=f"final_g{gi}")
        ab = [jnp.stack(s) for s in zip(*g["new_ab"])]
        outs.append((y, ab[0], ab[1], ab[2], ab[3], ab[4], jnp.stack(g["new_c"])))
    p, s = outs
    return (p[0], s[0], p[1], p[2], p[3], p[4], p[5], p[6], s[1], s[2], s[3], s[4], s[5], s[6])
```

```python
import functools

import numpy as np
import jax
import jax.numpy as jnp
from jax import lax
from jax.experimental import pallas as pl
from jax.experimental.pallas import tpu as pltpu

F32 = jnp.float32
BF16 = jnp.bfloat16
I32 = jnp.int32

D_MODEL = 1024
DEPTH = 2
CHUNK = 64
H_A, DK_A, DV_A, CONV_W = 4, 128, 128, 4
QKV_A = H_A * (2 * DK_A + DV_A)
H_B, DQK_B, DV_B = 4, 64, 128
H_C, DK_C, DV_C = 4, 256, 512
ROPE_BASE = 10000.0
PAST_LEN = 4096
AB_SIZES = (QKV_A, H_A, H_A, H_A * DV_A, H_B * DQK_B, H_B * DQK_B, H_B * DV_B, 2 * H_B, H_B * DV_B)
MIX_AB = H_A * DV_A + H_B * DV_B
MIX_C = H_C * DV_C
IN_C = 2 * H_C * DK_C + 2 * H_C * DV_C
N_EXPERTS, N_GROUPS, EPG, D_FF = 16, 4, 4, 512
N_MOD = 6
EPS = 1e-6

LANES = 128
AB_MAIN = 3584
N_GATE_COLS = 16
HX_W = D_MODEL + LANES
N_BUCKETS = N_GROUPS * 6
MOE_TM = 256
VMEM_LIMIT = 48 * 1024 * 1024

NN = ((1,), (0,))
NT = ((1,), (1,))
TN = ((0,), (0,))


def _dot(a, b, dims=NN):
    return lax.dot_general(a, b, (dims, ((), ())), preferred_element_type=F32)


def _mmb(a, b, dims=NN):
    return _dot(a.astype(BF16), b.astype(BF16), dims)


def _split2(x):
    hi = x.astype(BF16)
    lo = (x - hi.astype(F32)).astype(BF16)
    return hi, lo


def _split3(x):
    hi = x.astype(BF16)
    r = x - hi.astype(F32)
    mid = r.astype(BF16)
    lo = (r - mid.astype(F32)).astype(BF16)
    return hi, mid, lo


def _mm3(a, b, dims=NN):
    ah, al = _split2(a)
    bh, bl = _split2(b)
    return _dot(ah, bh, dims) + (_dot(ah, bl, dims) + _dot(al, bh, dims))


def _mm_mask_l(mask_bf16, x):
    h, m, l = _split3(x)
    return _dot(mask_bf16, h) + (_dot(mask_bf16, m) + _dot(mask_bf16, l))


def _mm_mask_r(x, mask_bf16):
    h, m, l = _split3(x)
    return _dot(h, mask_bf16) + (_dot(m, mask_bf16) + _dot(l, mask_bf16))


def _sigmoid(x):
    return 1.0 / (1.0 + jnp.exp(-x))


def _silu(x):
    return x * _sigmoid(x)


def _softplus(x):
    return jnp.maximum(x, 0.0) + jnp.log(1.0 + jnp.exp(-jnp.abs(x)))


def _rms(x, eps=EPS):
    return x * lax.rsqrt(jnp.mean(x * x, axis=-1, keepdims=True) + eps)


def _cparams(sem):
    return pltpu.CompilerParams(dimension_semantics=sem, vmem_limit_bytes=VMEM_LIMIT)


def _mod_kernel(c_ref, w_ref, b_ref, o_ref):
    c = c_ref[...]
    o_ref[0] = _mm3(_silu(c), w_ref[0]) + b_ref[0]


def _mod_call(c_all, w_mod, b_mod):
    bt = c_all.shape[0]
    e = w_mod.shape[-1]
    tn = 1024
    return pl.pallas_call(
        _mod_kernel,
        grid=(DEPTH, e // tn),
        in_specs=[pl.BlockSpec((bt, D_MODEL), lambda l, j: (0, 0)),
                  pl.BlockSpec((1, D_MODEL, tn), lambda l, j: (l, 0, j)),
                  pl.BlockSpec((1, 1, tn), lambda l, j: (l, 0, j))],
        out_specs=pl.BlockSpec((1, bt, tn), lambda l, j: (l, 0, j)),
        out_shape=jax.ShapeDtypeStruct((DEPTH, bt, e), F32),
        compiler_params=_cparams(("parallel", "parallel")),
        name="mod",
    )(c_all, w_mod, b_mod.reshape(DEPTH, 1, e))


def _ln_mm_kernel(*refs, has_prev, has_gates):
    it = iter(refs)
    x_ref = next(it)
    if has_prev:
        yp_ref, gp_ref = next(it), next(it)
    gain_ref, sc_ref, sh_ref, w_ref = next(it), next(it), next(it), next(it)
    if has_gates:
        wg_ref, wgt_ref = next(it), next(it)
    if has_prev:
        h_ref = next(it)
    proj_ref = next(it)
    if has_gates:
        g_ref, gt_ref = next(it), next(it)
    hn_s = next(it)

    bb, tt, d = x_ref.shape
    rows = bb * tt

    @pl.when(pl.program_id(2) == 0)
    def _():
        x = x_ref[...]
        if has_prev:
            x = x + gp_ref[...] * yp_ref[...].reshape(bb, tt, d)
            h_ref[...] = x
        hn = _rms(x) * gain_ref[...]
        hn = hn * (1.0 + sc_ref[...]) + sh_ref[...]
        hn2 = hn.reshape(rows, d)
        hn_s[...] = hn2.astype(BF16)
        if has_gates:
            g_ref[...] = _mm3(hn2, wg_ref[...]).reshape(bb, tt, LANES)
            gt_ref[...] = _mm3(wgt_ref[...], hn2, NT)

    tn = proj_ref.shape[-1]
    proj_ref[...] = _dot(hn_s[...], w_ref[...]).reshape(bb, tt, tn).astype(proj_ref.dtype)


def _ln_mm_call(x, prev, gain, sc, sh, w, gates_w, *, bb, tt, tn, out_dtype, name):
    b, t, d = x.shape
    e = w.shape[1]
    rows = bb * tt
    nti = t // tt
    grid = (b // bb, nti, e // tn)
    has_prev = prev is not None
    has_gates = gates_w is not None
    xspec = pl.BlockSpec((bb, tt, d), lambda bi, i, j: (bi, i, 0))
    mspec = pl.BlockSpec((bb, 1, d), lambda bi, i, j: (bi, 0, 0))
    in_specs, args = [xspec], [x]
    if has_prev:
        y2d, off, gp = prev
        in_specs += [pl.BlockSpec((rows, d), lambda bi, i, j: (off + bi * nti + i, 0)), mspec]
        args += [y2d, gp]
    in_specs += [pl.BlockSpec((1, 1, d), lambda bi, i, j: (0, 0, 0)), mspec, mspec,
                 pl.BlockSpec((d, tn), lambda bi, i, j: (0, j))]
    args += [gain.reshape(1, 1, d), sc, sh, w]
    if has_gates:
        wg, wgt = gates_w
        in_specs += [pl.BlockSpec((d, LANES), lambda bi, i, j: (0, 0)),
                     pl.BlockSpec((N_GATE_COLS, d), lambda bi, i, j: (0, 0))]
        args += [wg, wgt]
    out_specs, out_shape = [], []
    if has_prev:
        out_specs.append(xspec)
        out_shape.append(jax.ShapeDtypeStruct((b, t, d), F32))
    out_specs.append(pl.BlockSpec((bb, tt, tn), lambda bi, i, j: (bi, i, j)))
    out_shape.append(jax.ShapeDtypeStruct((b, t, e), out_dtype))
    if has_gates:
        out_specs += [pl.BlockSpec((bb, tt, LANES), lambda bi, i, j: (bi, i, 0)),
                      pl.BlockSpec((N_GATE_COLS, rows), lambda bi, i, j: (0, bi * nti + i))]
        out_shape += [jax.ShapeDtypeStruct((b, t, LANES), F32),
                      jax.ShapeDtypeStruct((N_GATE_COLS, b * t), F32)]
    return pl.pallas_call(
        functools.partial(_ln_mm_kernel, has_prev=has_prev, has_gates=has_gates),
        grid=grid, in_specs=in_specs, out_specs=out_specs, out_shape=out_shape,
        scratch_shapes=[pltpu.VMEM((rows, d), BF16)],
        compiler_params=_cparams(("parallel", "parallel", "arbitrary")),
        name=name,
    )(*args)


def _tri_inv_all(a_list, eye, length, mm):
    xs = [-a for a in a_list]
    ps = [eye + x for x in xs]
    n = 2
    while n < length:
        xs = [mm(x, x) for x in xs]
        ps = [p + mm(p, x) for p, x in zip(ps, xs)]
        n *= 2
    return ps


def _mixer_ab_kernel(qkv_ref, z_ref, qkb_ref, vb_ref, ob_ref, g_ref, gt_ref,
                     conv0_ref, sa0_ref, cb0_ref, nb0_ref, mb0_ref,
                     convw_ref, gpar_ref, gpart_ref, na_ref, nbn_ref,
                     mixed_ref, conv_o, sa_o, cb_o, nb_o, mb_o,
                     xbuf, *, L, CB):
    TB = L * CB

    @pl.when(pl.program_id(1) == 0)
    def _():
        xbuf[5:8, :] = conv0_ref[0]
        sa_o[...] = sa0_ref[...]
        cb_o[...] = cb0_ref[...]
        nb_o[...] = nb0_ref[...]
        mb_o[...] = mb0_ref[...]

    xbuf[8:8 + TB, :] = qkv_ref[0]
    conv_o[0] = xbuf[5 + TB:8 + TB, :]

    ii = lax.broadcasted_iota(I32, (L, L), 0)
    jj = lax.broadcasted_iota(I32, (L, L), 1)
    tri = ii >= jj
    strict = ii > jj
    eye = (ii == jj).astype(F32)
    tril_b = tri.astype(BF16)
    triu_b = (ii <= jj).astype(BF16)
    lane = lax.broadcasted_iota(I32, (L, LANES), 1)
    rowi = lax.broadcasted_iota(I32, (N_GATE_COLS, L), 0)
    gpar = gpar_ref[...]
    gpart = gpart_ref[...]
    nea = -jnp.exp(gpar[0:1, :])
    neat = -jnp.exp(gpart[:, 0:1])
    norm_a = na_ref[...]
    norm_b = nbn_ref[...]

    mm = _mmb
    mm_inv = _mm3

    def conv_tile(r0, col0):
        acc = None
        for j in range(CONV_W):
            term = convw_ref[j:j + 1, col0:col0 + DK_A] * xbuf[5 + j + r0:5 + j + r0 + L, col0:col0 + DK_A]
            acc = term if acc is None else acc + term
        return _silu(acc)

    gates = []
    for c in range(CB):
        r0 = c * L
        pre = g_ref[0, r0:r0 + L, :] + gpar[1:2, :]
        csrc = jnp.where(lane < 4, nea * _softplus(pre),
                         jnp.where(lane >= 12, -_softplus(-pre), 0.0))
        csrc = jnp.where(lane < N_GATE_COLS, csrc, 0.0)
        gcum = _mm_mask_l(tril_b, csrc)
        pret = gt_ref[0, :, r0:r0 + L] + gpart[:, 1:2]
        csrct = jnp.where(rowi < 4, neat * _softplus(pret),
                          jnp.where(rowi >= 12, -_softplus(-pret), 0.0))
        gcumt = _mm_mask_r(csrct, triu_b)
        gates.append((pre, pret, gcum, gcumt, _sigmoid(pre)))

    ga = []
    a_list = []
    for c in range(CB):
        r0 = c * L
        pre, pret, gcum, gcumt, beta_all = gates[c]
        for h in range(H_A):
            q = conv_tile(r0, h * DK_A)
            k = conv_tile(r0, H_A * DK_A + h * DK_A)
            v = conv_tile(r0, 2 * H_A * DK_A + h * DV_A)
            q = q * lax.rsqrt(jnp.sum(q * q, axis=-1, keepdims=True) + EPS) * (DK_A ** -0.5)
            k = k * lax.rsqrt(jnp.sum(k * k, axis=-1, keepdims=True) + EPS)
            gc_c = gcum[:, h:h + 1]
            gc_r = gcumt[h:h + 1, :]
            decay = jnp.where(tri, jnp.exp(jnp.where(tri, gc_c - gc_r, 0.0)), 0.0)
            beta_c = beta_all[:, 4 + h:5 + h]
            kb = k * beta_c
            a_list.append(jnp.where(strict, mm_inv(kb, k, NT) * decay, 0.0))
            eg = jnp.exp(gc_c)
            gl = gc_c[L - 1:L, :]
            attn = jnp.where(tri, mm(q, k, NT) * decay, 0.0)
            ga.append(dict(kbeg=kb * eg, vb=v * beta_c, qeg=q * eg, attn=attn,
                           kdec=k * jnp.exp(gl - gc_c), sdec=jnp.exp(gl)))
    tinv = _tri_inv_all(a_list, eye, L, mm_inv)
    for d, t in zip(ga, tinv):
        d["w"] = mm_inv(t, d.pop("kbeg"))
        d["u"] = mm_inv(t, d.pop("vb"))

    gb = []
    for c in range(CB):
        r0 = c * L
        pre, pret, gcum, gcumt, _ = gates[c]
        for h in range(H_B):
            q = qkb_ref[0, r0:r0 + L, h * DQK_B:(h + 1) * DQK_B] * (DQK_B ** -0.5)
            k = qkb_ref[0, r0:r0 + L, H_B * DQK_B + h * DQK_B:H_B * DQK_B + (h + 1) * DQK_B]
            b_c = gcum[:, 12 + h:13 + h]
            b_r = gcumt[12 + h:13 + h, :]
            li_r = pret[8 + h:9 + h, :]
            dm = jnp.where(tri, b_c - b_r + li_r, -jnp.inf)
            gb.append(dict(q=q, k=k, qk=mm(q, k, NT), dm=dm, dmax=jnp.max(dm, axis=-1, keepdims=True),
                           b_c=b_c, li_c=pre[:, 8 + h:9 + h]))

    for c in range(CB):
        r0 = c * L
        for h in range(H_A):
            d = ga[c * H_A + h]
            s = sa_o[0, h]
            v_new = d["u"] - mm(d["w"], s)
            o = mm(d["qeg"], s) + mm(d["attn"], v_new)
            sa_o[0, h] = s * d["sdec"] + mm(d["kdec"], v_new, TN)
            zg = z_ref[0, r0:r0 + L, h * DV_A:(h + 1) * DV_A]
            o = _rms(o) * norm_a * _silu(zg)
            mixed_ref[0, r0:r0 + L, h * DV_A:(h + 1) * DV_A] = o.astype(mixed_ref.dtype)
        for h in range(H_B):
            d = gb[c * H_B + h]
            q, k, b_c = d["q"], d["k"], d["b_c"]
            v = vb_ref[0, r0:r0 + L, h * DV_B:(h + 1) * DV_B]
            m_prev = mb_o[0, h]
            a0 = b_c + m_prev
            m_t = jnp.maximum(a0, d["dmax"])
            w0 = jnp.exp(a0 - m_t)
            sm = d["qk"] * jnp.exp(d["dm"] - m_t)
            cst = cb_o[0, h]
            nst = nb_o[0, h]
            num = w0 * mm(q, cst) + mm(sm, v)
            den = w0 * jnp.sum(q * nst, axis=-1, keepdims=True) + jnp.sum(sm, axis=-1, keepdims=True)
            hh = num / jnp.maximum(jnp.abs(den), jnp.exp(-m_t))
            m_new = m_t[L - 1:L, :]
            wk = jnp.exp(b_c[L - 1:L, :] - b_c + d["li_c"] - m_new)
            w0l = w0[L - 1:L, :]
            kw = k * wk
            cb_o[0, h] = w0l * cst + mm(kw, v, TN)
            nb_o[0, h] = w0l * nst + jnp.sum(kw, axis=0, keepdims=True)
            mb_o[0, h] = m_new
            og = ob_ref[0, r0:r0 + L, h * DV_B:(h + 1) * DV_B]
            o = _rms(hh) * norm_b * _sigmoid(og)
            c0 = H_A * DV_A + h * DV_B
            mixed_ref[0, r0:r0 + L, c0:c0 + DV_B] = o.astype(mixed_ref.dtype)

    xbuf[5:8, :] = xbuf[5 + TB:8 + TB, :]


def _mixer_ab_call(proj, gates, gates_t, conv0, sa0, cb0, nb0, mb0, convw, gpar, gpart, norm_a, norm_b, *, L, CB, name):
    b, t, _ = proj.shape
    tb = L * CB
    nt = t // tb

    def col(width, idx):
        return pl.BlockSpec((1, tb, width), lambda bi, i: (bi, i, idx))

    def const(shape):
        return pl.BlockSpec(shape, lambda bi, i: (0,) * len(shape))

    def state(shape):
        return pl.BlockSpec((1,) + shape, lambda bi, i: (bi,) + (0,) * len(shape))

    in_specs = [col(QKV_A, 0), col(512, 3), col(512, 4), col(512, 5), col(512, 6),
                col(LANES, 0), pl.BlockSpec((1, N_GATE_COLS, tb), lambda bi, i: (bi, 0, i)),
                state((CONV_W - 1, QKV_A)), state((H_A, DK_A, DV_A)), state((H_B, DQK_B, DV_B)),
                state((H_B, 1, DQK_B)), state((H_B, 1, 1)),
                const((CONV_W, QKV_A)), const((8, LANES)), const((N_GATE_COLS, LANES)),
                const((1, DV_A)), const((1, DV_B))]
    out_specs = [pl.BlockSpec((1, tb, MIX_AB), lambda bi, i: (bi, i, 0)),
                 state((CONV_W - 1, QKV_A)), state((H_A, DK_A, DV_A)), state((H_B, DQK_B, DV_B)),
                 state((H_B, 1, DQK_B)), state((H_B, 1, 1))]
    out_shape = [jax.ShapeDtypeStruct((b, t, MIX_AB), BF16),
                 jax.ShapeDtypeStruct((b, CONV_W - 1, QKV_A), F32),
                 jax.ShapeDtypeStruct((b, H_A, DK_A, DV_A), F32),
                 jax.ShapeDtypeStruct((b, H_B, DQK_B, DV_B), F32),
                 jax.ShapeDtypeStruct((b, H_B, 1, DQK_B), F32),
                 jax.ShapeDtypeStruct((b, H_B, 1, 1), F32)]
    return pl.pallas_call(
        functools.partial(_mixer_ab_kernel, L=L, CB=CB),
        grid=(b, nt), in_specs=in_specs, out_specs=out_specs, out_shape=out_shape,
        scratch_shapes=[pltpu.VMEM((8 + tb, QKV_A), F32)],
        compiler_params=_cparams(("parallel", "arbitrary")),
        name=name,
    )(proj, proj, proj, proj, proj, gates, gates_t, conv0, sa0, cb0, nb0, mb0, convw, gpar, gpart, norm_a, norm_b)


def _mixer_c_kernel(q_ref, k_ref, v_ref, g_ref, cos_ref, sin_ref, s0_ref, mixed_ref, s_o, *, L, CB):
    @pl.when(pl.program_id(1) == 0)
    def _():
        s_o[...] = s0_ref[...]

    ii = lax.broadcasted_iota(I32, (L, L), 0)
    jj = lax.broadcasted_iota(I32, (L, L), 1)
    rel = (ii - jj).astype(F32)
    idx = lax.broadcasted_iota(I32, (L, 1), 0).astype(F32)
    half = DK_C // 2

    def rope(x, cos, sin):
        x1, x2 = x[:, :half], x[:, half:]
        return jnp.concatenate([x1 * cos - x2 * sin, x1 * sin + x2 * cos], axis=-1)

    for c in range(CB):
        r0 = c * L
        cos = cos_ref[r0:r0 + L, :]
        sin = sin_ref[r0:r0 + L, :]
        for h in range(H_C):
            lg = float(np.log1p(-np.exp2(-5.0 - h)))
            intra = jnp.where(rel >= 0, jnp.exp(lg * jnp.maximum(rel, 0.0)), 0.0)
            q_decay = jnp.exp(lg * (idx + 1.0))
            k_decay = jnp.exp(lg * (L - 1.0 - idx))
            s_decay = float(np.exp(lg * L))
            q = rope(q_ref[0, r0:r0 + L, h * DK_C:(h + 1) * DK_C].astype(F32), cos, sin)
            k = rope(k_ref[0, r0:r0 + L, h * DK_C:(h + 1) * DK_C].astype(F32), cos, sin) * (DK_C ** -0.5)
            v = v_ref[0, r0:r0 + L, h * DV_C:(h + 1) * DV_C]
            s = s_o[0, h]
            o = _mmb(_mmb(q, k, NT) * intra, v) + _mmb(q * q_decay, s)
            s_o[0, h] = s_decay * s + _mmb(k * k_decay, v, TN)
            gg = g_ref[0, r0:r0 + L, h * DV_C:(h + 1) * DV_C].astype(F32)
            o = _rms(o) * _silu(gg)
            mixed_ref[0, r0:r0 + L, h * DV_C:(h + 1) * DV_C] = o.astype(mixed_ref.dtype)


def _mixer_c_call(proj, cos, sin, s0, *, L, CB, name):
    b, t, _ = proj.shape
    tb = L * CB
    nt = t // tb
    qk_w, v_w = H_C * DK_C, H_C * DV_C
    sspec = pl.BlockSpec((1, H_C, DK_C, DV_C), lambda bi, i: (bi, 0, 0, 0))
    tspec = pl.BlockSpec((tb, DK_C // 2), lambda bi, i: (i, 0))
    return pl.pallas_call(
        functools.partial(_mixer_c_kernel, L=L, CB=CB),
        grid=(b, nt),
        in_specs=[pl.BlockSpec((1, tb, qk_w), lambda bi, i: (bi, i, 0)),
                  pl.BlockSpec((1, tb, qk_w), lambda bi, i: (bi, i, 1)),
                  pl.BlockSpec((1, tb, v_w), lambda bi, i: (bi, i, 1)),
                  pl.BlockSpec((1, tb, v_w), lambda bi, i: (bi, i, 2)),
                  tspec, tspec, sspec],
        out_specs=[pl.BlockSpec((1, tb, MIX_C), lambda bi, i: (bi, i, 0)), sspec],
        out_shape=[jax.ShapeDtypeStruct((b, t, MIX_C), BF16),
                   jax.ShapeDtypeStruct((b, H_C, DK_C, DV_C), F32)],
        compiler_params=_cparams(("parallel", "arbitrary")),
        name=name,
    )(proj, proj, proj, proj, cos, sin, s0)


def _out_route_kernel(mixed_ref, wout_ref, h_ref, g1_ref, gain_ref, sc_ref, sh_ref, wrt_ref, rb_ref, hx_in_ref,
                      hnew_ref, hx_ref, bkt_ref):
    del hx_in_ref
    bb, tt, m = mixed_ref.shape
    d = h_ref.shape[-1]
    rows = bb * tt
    y = _dot(mixed_ref[...].reshape(rows, m), wout_ref[...])
    h = h_ref[...] + g1_ref[...] * y.reshape(bb, tt, d)
    hnew_ref[...] = h
    hn = _rms(h) * gain_ref[...]
    hn = (hn * (1.0 + sc_ref[...]) + sh_ref[...]).reshape(rows, d)
    hx_ref[:, 0:d] = hn

    logits = _mm3(wrt_ref[...], hn, NT)
    score = _sigmoid(logits)
    sel = score + rb_ref[...]

    def row(a, r):
        return a[r:r + 1, :]

    gscore = []
    for g in range(N_GROUPS):
        a, b, c, e = (row(sel, EPG * g + i) for i in range(EPG))
        hi_ab, lo_ab, hi_ce, lo_ce = jnp.maximum(a, b), jnp.minimum(a, b), jnp.maximum(c, e), jnp.minimum(c, e)
        top1 = jnp.maximum(hi_ab, hi_ce)
        top2 = jnp.maximum(jnp.maximum(lo_ab, lo_ce), jnp.minimum(hi_ab, hi_ce))
        gscore.append(top1 + top2)
    best = jnp.zeros((1, rows), I32)
    bestv = gscore[0]
    for g in range(1, N_GROUPS):
        upd = gscore[g] > bestv
        best = jnp.where(upd, g, best)
        bestv = jnp.where(upd, gscore[g], bestv)

    def pick(a, i):
        out = row(a, i)
        for g in range(1, N_GROUPS):
            out = jnp.where(best == g, row(a, EPG * g + i), out)
        return out

    vsel = [pick(sel, i) for i in range(EPG)]
    vsc = [pick(score, i) for i in range(EPG)]
    i1 = jnp.zeros((1, rows), I32)
    v1 = vsel[0]
    for i in range(1, EPG):
        upd = vsel[i] > v1
        i1 = jnp.where(upd, i, i1)
        v1 = jnp.where(upd, vsel[i], v1)
    i2 = jnp.full((1, rows), -1, I32)
    v2 = jnp.full((1, rows), -jnp.inf, F32)
    for i in range(EPG):
        cand = jnp.where(i1 == i, -jnp.inf, vsel[i])
        upd = cand > v2
        i2 = jnp.where(upd, i, i2)
        v2 = jnp.where(upd, cand, v2)
    s1 = jnp.zeros((1, rows), F32)
    s2 = jnp.zeros((1, rows), F32)
    for i in range(EPG):
        s1 = jnp.where(i1 == i, vsc[i], s1)
        s2 = jnp.where(i2 == i, vsc[i], s2)
    den = s1 + s2
    w1 = s1 / den
    w2 = s2 / den
    first_lo = i1 < i2
    lo = jnp.minimum(i1, i2)
    hi = jnp.maximum(i1, i2)
    wlo = jnp.where(first_lo, w1, w2)
    whi = jnp.where(first_lo, w2, w1)
    pair = jnp.where(lo == 0, hi - 1, jnp.where(lo == 1, hi + 1, 5))
    bkt_ref[...] = (best * 6 + pair).reshape(1, 1, rows)

    aux = jnp.concatenate([wlo, whi, jnp.zeros((LANES - 2, rows), F32)], axis=0)
    hx_ref[:, d:d + LANES] = aux.T


def _out_route_call(mixed, wout, h, g1, gain, sc, sh, wrt, rb, hx_buf, row_block_off, *, bb, tt, name):
    b, t, m = mixed.shape
    d = h.shape[-1]
    rows = bb * tt
    nti = t // tt
    ntot = hx_buf.shape[0]
    xspec = pl.BlockSpec((bb, tt, d), lambda bi, i: (bi, i, 0))
    mspec = pl.BlockSpec((bb, 1, d), lambda bi, i: (bi, 0, 0))
    return pl.pallas_call(
        _out_route_kernel,
        grid=(b // bb, nti),
        in_specs=[pl.BlockSpec((bb, tt, m), lambda bi, i: (bi, i, 0)),
                  pl.BlockSpec((m, d), lambda bi, i: (0, 0)),
                  xspec, mspec,
                  pl.BlockSpec((1, 1, d), lambda bi, i: (0, 0, 0)), mspec, mspec,
                  pl.BlockSpec((N_EXPERTS, d), lambda bi, i: (0, 0)),
                  pl.BlockSpec((N_EXPERTS, 1), lambda bi, i: (0, 0)),
                  pl.BlockSpec(memory_space=pl.ANY)],
        out_specs=[xspec,
                   pl.BlockSpec((rows, HX_W), lambda bi, i: (row_block_off + bi * nti + i, 0)),
                   pl.BlockSpec((1, 1, rows), lambda bi, i: (bi * nti + i, 0, 0))],
        out_shape=[jax.ShapeDtypeStruct((b, t, d), F32),
                   jax.ShapeDtypeStruct((ntot, HX_W), F32),
                   jax.ShapeDtypeStruct(((b // bb) * nti, 1, rows), I32)],
        input_output_aliases={9: 1},
        compiler_params=_cparams(("parallel", "parallel")),
        name=name,
    )(mixed, wout, h, g1, gain.reshape(1, 1, d), sc, sh, wrt, rb, hx_buf)


def _moe_kernel(ea_ref, eb_ref, chg_ref, src_ref, hx_hbm, wga_ref, wgb_ref, wua_ref, wub_ref, wda_ref, wdb_ref,
                ys_ref, xbuf, wg_s, wu_s, wd_s, sem):
    t = pl.program_id(0)
    nt = pl.num_programs(0)
    slot = lax.rem(t, 2)

    def row_copy(tile, r, s):
        row = src_ref[tile * MOE_TM + r]
        return pltpu.make_async_copy(hx_hbm.at[pl.ds(row, 1), :], xbuf.at[s, pl.ds(r, 1), :], sem.at[s])

    def start_rows(tile, s):
        for r in range(MOE_TM):
            row_copy(tile, r, s).start()

    def wait_rows(tile, s):
        for r in range(MOE_TM):
            row_copy(tile, r, s).wait()

    @pl.when(t == 0)
    def _():
        start_rows(0, 0)

    wait_rows(t, slot)
    nxt = jnp.minimum(t + 1, nt - 1)
    start_rows(nxt, 1 - slot)

    @pl.when(chg_ref[t] > 0)
    def _():
        rc = 128
        for i in range(D_MODEL // rc):
            wg_s[0, i * rc:(i + 1) * rc, :] = wga_ref[0, 0, i * rc:(i + 1) * rc, :].astype(BF16)
            wg_s[1, i * rc:(i + 1) * rc, :] = wgb_ref[0, 0, i * rc:(i + 1) * rc, :].astype(BF16)
            wu_s[0, i * rc:(i + 1) * rc, :] = wua_ref[0, 0, i * rc:(i + 1) * rc, :].astype(BF16)
            wu_s[1, i * rc:(i + 1) * rc, :] = wub_ref[0, 0, i * rc:(i + 1) * rc, :].astype(BF16)
        for i in range(D_FF // rc):
            wd_s[0, i * rc:(i + 1) * rc, :] = wda_ref[0, 0, i * rc:(i + 1) * rc, :].astype(BF16)
            wd_s[1, i * rc:(i + 1) * rc, :] = wdb_ref[0, 0, i * rc:(i + 1) * rc, :].astype(BF16)

    x = xbuf[slot, :, 0:D_MODEL].astype(BF16)
    wlo = xbuf[slot, :, D_MODEL:D_MODEL + 1]
    whi = xbuf[slot, :, D_MODEL + 1:D_MODEL + 2]
    act_a = (_silu(_dot(x, wg_s[0])) * _dot(x, wu_s[0]) * wlo).astype(BF16)
    act_b = (_silu(_dot(x, wg_s[1])) * _dot(x, wu_s[1]) * whi).astype(BF16)
    ys_ref[...] = _dot(act_a, wd_s[0]) + _dot(act_b, wd_s[1])

    @pl.when(t == nt - 1)
    def _():
        wait_rows(nxt, 1 - slot)


def _moe_call(ea, eb, chg, src, hx, wg, wu, wd, layer):
    p = src.shape[0]
    n_tiles = p // MOE_TM
    wa_in = pl.BlockSpec((1, 1, D_MODEL, D_FF), lambda t, ea, eb, ch, sr: (layer, ea[t], 0, 0))
    wb_in = pl.BlockSpec((1, 1, D_MODEL, D_FF), lambda t, ea, eb, ch, sr: (layer, eb[t], 0, 0))
    wa_out = pl.BlockSpec((1, 1, D_FF, D_MODEL), lambda t, ea, eb, ch, sr: (layer, ea[t], 0, 0))
    wb_out = pl.BlockSpec((1, 1, D_FF, D_MODEL), lambda t, ea, eb, ch, sr: (layer, eb[t], 0, 0))
    return pl.pallas_call(
        _moe_kernel,
        grid_spec=pltpu.PrefetchScalarGridSpec(
            num_scalar_prefetch=4, grid=(n_tiles,),
            in_specs=[pl.BlockSpec(memory_space=pl.ANY), wa_in, wb_in, wa_in, wb_in, wa_out, wb_out],
            out_specs=pl.BlockSpec((MOE_TM, D_MODEL), lambda t, ea, eb, ch, sr: (t, 0)),
            scratch_shapes=[pltpu.VMEM((2, MOE_TM, HX_W), F32),
                            pltpu.VMEM((2, D_MODEL, D_FF), BF16), pltpu.VMEM((2, D_MODEL, D_FF), BF16),
                            pltpu.VMEM((2, D_FF, D_MODEL), BF16),
                            pltpu.SemaphoreType.DMA((2,))]),
        out_shape=jax.ShapeDtypeStruct((p, D_MODEL), F32),
        compiler_params=_cparams(("arbitrary",)),
        name="moe",
    )(ea, eb, chg, src, hx, wg, wg, wu, wu, wd, wd)


_PAIRS = [(0, 1), (0, 2), (0, 3), (1, 2), (1, 3), (2, 3)]
_EA_TABLE = np.array([g * EPG + p[0] for g in range(N_GROUPS) for p in _PAIRS], np.int32)
_EB_TABLE = np.array([g * EPG + p[1] for g in range(N_GROUPS) for p in _PAIRS], np.int32)


def _moe(hx, bkt, wg, wu, wd, layer):
    n = hx.shape[0]
    tm = MOE_TM
    ar = jnp.arange(N_BUCKETS, dtype=I32)
    oh = (bkt[:, None] == ar[None, :]).astype(I32)
    blk = LANES
    nb = -(-n // blk)
    ohb = jnp.pad(oh, ((0, nb * blk - n), (0, 0))).reshape(nb, blk, N_BUCKETS)
    tril = jnp.tril(jnp.ones((blk, blk), F32))
    inner = jnp.einsum('ij,bjk->bik', tril, ohb.astype(F32))
    tot = inner[:, -1, :]
    outer = jnp.cumsum(tot, axis=0) - tot
    cs = (inner + outer[:, None, :]).reshape(nb * blk, N_BUCKETS)[:n].astype(I32)
    rank = jnp.sum(oh * cs, axis=1) - 1
    counts = cs[-1]
    padded = ((counts + tm - 1) // tm) * tm
    ends = jnp.cumsum(padded)
    offs = ends - padded
    pos = jnp.sum(oh * offs[None, :], axis=1) + rank
    n_tiles = -(-n // tm) + N_BUCKETS
    p = n_tiles * tm
    src = jnp.zeros((p,), I32).at[pos].set(jnp.arange(n, dtype=I32))
    tstart = jnp.arange(n_tiles, dtype=I32) * tm
    tb = jnp.sum((ends[None, :] <= tstart[:, None]).astype(I32), axis=1)
    n_used = ends[-1] // tm
    valid = (jnp.arange(n_tiles, dtype=I32) < n_used).astype(I32)
    last_b = tb[jnp.maximum(n_used - 1, 0)]
    tb = jnp.clip(jnp.where(valid > 0, tb, last_b), 0, N_BUCKETS - 1)
    ea = jnp.asarray(_EA_TABLE)[tb]
    eb = jnp.asarray(_EB_TABLE)[tb]
    chg = jnp.concatenate([jnp.ones((1,), I32), (tb[1:] != tb[:-1]).astype(I32)])
    ys = _moe_call(ea, eb, chg, src, hx, wg, wu, wd, layer)
    return jnp.take(ys, pos, axis=0)


def _final_kernel(h_ref, y_ref, g_ref, gain_ref, o_ref):
    bb, tt, d = h_ref.shape
    h = h_ref[...] + g_ref[...] * y_ref[...].reshape(bb, tt, d)
    o_ref[...] = _rms(h) * gain_ref[...]


def _final_call(h, y2d, off, g2, gain, *, bb, tt, name):
    b, t, d = h.shape
    nti = t // tt
    rows = bb * tt
    xspec = pl.BlockSpec((bb, tt, d), lambda bi, i: (bi, i, 0))
    return pl.pallas_call(
        _final_kernel,
        grid=(b // bb, nti),
        in_specs=[xspec, pl.BlockSpec((rows, d), lambda bi, i: (off + bi * nti + i, 0)),
                  pl.BlockSpec((bb, 1, d), lambda bi, i: (bi, 0, 0)),
                  pl.BlockSpec((1, 1, d), lambda bi, i: (0, 0, 0))],
        out_specs=xspec,
        out_shape=jax.ShapeDtypeStruct((b, t, d), F32),
        compiler_params=_cparams(("parallel", "parallel")),
        name=name,
    )(h, y2d, g2, gain.reshape(1, 1, d))


def _group_cfg(b, t):
    if t >= 512:
        return dict(bb=1, tt=512, L=CHUNK, CB=4)
    return dict(bb=b, tt=t, L=min(CHUNK, t), CB=1)


def _rope_tables(pos0, t):
    half = DK_C // 2
    inv = jnp.power(ROPE_BASE, -jnp.linspace(0.0, 1.0, half, dtype=F32))
    ang = (pos0 + jnp.arange(t, dtype=F32))[:, None] * inv[None, :]
    return jnp.cos(ang), jnp.sin(ang)


def kernel(x_prompt, x_sample, c_prompt, c_sample, state_a_conv, state_a_rec, state_b_c, state_b_n, state_b_m,
           state_c_rec, w_mod, b_mod, norm_mix, norm_ffn, w_in_ab, conv_a, a_log, dt_bias, norm_a, gate_bias_b,
           norm_b, w_out_ab, w_in_c, w_out_c, w_router, router_bias, w_gate, w_up, w_down, norm_final):
    d = D_MODEL
    bp, tp, _ = x_prompt.shape
    bs, ts, _ = x_sample.shape
    n_ab, n_c = w_in_ab.shape[0], w_in_c.shape[0]

    mod_all = _mod_call(jnp.concatenate([c_prompt, c_sample], axis=0), w_mod, b_mod)

    o = np.cumsum((0,) + AB_SIZES)
    w_ab_main = jnp.concatenate([w_in_ab[:, :, o[0]:o[1]].astype(BF16), w_in_ab[:, :, o[3]:o[7]].astype(BF16),
                                 w_in_ab[:, :, o[8]:o[9]].astype(BF16)], axis=-1)
    w_ab_gate = jnp.concatenate([w_in_ab[:, :, o[1]:o[3]], w_in_ab[:, :, o[7]:o[8]]], axis=-1)
    w_ab_gate_pad = jnp.pad(w_ab_gate, ((0, 0), (0, 0), (0, LANES - N_GATE_COLS)))
    w_ab_gate_t = jnp.swapaxes(w_ab_gate, 1, 2)
    w_c = w_in_c.astype(BF16)
    w_out_ab_b = w_out_ab.astype(BF16)
    w_out_c_b = w_out_c.astype(BF16)
    wrt = w_router.T
    rb = router_bias.reshape(N_EXPERTS, 1)
    zeros4 = jnp.zeros((n_ab, H_A), F32)
    gpar_row0 = jnp.concatenate([a_log, jnp.zeros((n_ab, LANES - H_A), F32)], axis=1)
    gpar_row1 = jnp.concatenate([dt_bias, zeros4, gate_bias_b, jnp.zeros((n_ab, LANES - 16), F32)], axis=1)
    gpar = jnp.concatenate([gpar_row0[:, None], gpar_row1[:, None], jnp.zeros((n_ab, 6, LANES), F32)], axis=1)
    gpart = jnp.swapaxes(jnp.concatenate([gpar[:, :2, :N_GATE_COLS], jnp.zeros((n_ab, LANES - 2, N_GATE_COLS), F32)],
                                         axis=1), 1, 2)

    groups = []
    zf = lambda *s: jnp.zeros(s, F32)
    groups.append(dict(
        x=x_prompt, b=bp, t=tp, pos0=0.0, mod=mod_all[:, :bp], row0=0,
        conv=zf(n_ab, bp, CONV_W - 1, QKV_A), sa=zf(n_ab, bp, H_A, DK_A, DV_A), cb=zf(n_ab, bp, H_B, DQK_B, DV_B),
        nb=zf(n_ab, bp, H_B, DQK_B), mb=zf(n_ab, bp, H_B), sc=zf(n_c, bp, H_C, DK_C, DV_C)))
    groups.append(dict(
        x=x_sample, b=bs, t=ts, pos0=float(PAST_LEN), mod=mod_all[:, bp:], row0=bp * tp,
        conv=state_a_conv, sa=state_a_rec, cb=state_b_c, nb=state_b_n, mb=state_b_m, sc=state_c_rec))
    n_tot = bp * tp + bs * ts
    for g in groups:
        g.update(_group_cfg(g["b"], g["t"]))
        g["h"] = g["x"]
        g["prev"] = None
        g["new_ab"], g["new_c"] = [], []
        g["cos"], g["sin"] = _rope_tables(g["pos0"], g["t"])
        g["roff"] = g["row0"] // (g["bb"] * g["tt"])

    for layer in range(DEPTH):
        li = layer // 2
        hx_buf = jnp.zeros((n_tot, HX_W), F32)
        bkts = []
        for gi, g in enumerate(groups):
            b, t, bb, tt = g["b"], g["t"], g["bb"], g["tt"]
            mods = [m.reshape(b, 1, d) for m in jnp.split(g["mod"][layer], N_MOD, axis=-1)]
            sh1, sc1, g1, sh2, sc2, g2 = mods
            tag = f"l{layer}g{gi}"
            if layer % 2 == 0:
                outs = _ln_mm_call(g["h"], g["prev"], norm_mix[layer], sc1, sh1, w_ab_main[li],
                                   (w_ab_gate_pad[li], w_ab_gate_t[li]), bb=bb, tt=tt, tn=512, out_dtype=F32,
                                   name="ln_ab_" + tag)
                if g["prev"] is not None:
                    g["h"], outs = outs[0], outs[1:]
                proj, gates, gates_t = outs
                gates_t = gates_t.reshape(N_GATE_COLS, b, t).transpose(1, 0, 2)
                mixed, conv_n, sa_n, cb_n, nb_n, mb_n = _mixer_ab_call(
                    proj, gates, gates_t, g["conv"][li], g["sa"][li], g["cb"][li],
                    g["nb"][li].reshape(b, H_B, 1, DQK_B), g["mb"][li].reshape(b, H_B, 1, 1),
                    conv_a[li], gpar[li], gpart[li], norm_a[li].reshape(1, DV_A), norm_b[li].reshape(1, DV_B),
                    L=g["L"], CB=g["CB"], name="mixer_ab_" + tag)
                g["new_ab"].append((conv_n, sa_n, cb_n, nb_n.reshape(b, H_B, DQK_B), mb_n.reshape(b, H_B)))
                wout = w_out_ab_b[li]
            else:
                outs = _ln_mm_call(g["h"], g["prev"], norm_mix[layer], sc1, sh1, w_c[li], None,
                                   bb=bb, tt=tt, tn=512, out_dtype=BF16, name="ln_c_" + tag)
                if g["prev"] is not None:
                    g["h"], outs = outs[0], outs[1:]
                (proj,) = outs
                mixed, sc_n = _mixer_c_call(proj, g["cos"], g["sin"], g["sc"][li], L=g["L"], CB=g["CB"],
                                            name="mixer_c_" + tag)
                g["new_c"].append(sc_n)
                wout = w_out_c_b[li]
            g["h"], hx_buf, bkt = _out_route_call(mixed, wout, g["h"], g1, norm_ffn[layer], sc2, sh2, wrt, rb,
                                                  hx_buf, g["roff"], bb=bb, tt=tt, name="out_route_" + tag)
            bkts.append(bkt.reshape(-1))
            g["g2"] = g2
        y_moe = _moe(hx_buf, jnp.concatenate(bkts), w_gate, w_up, w_down, layer)
        for g in groups:
            g["prev"] = (y_moe, g["roff"], g["g2"])

    outs = []
    for gi, g in enumerate(groups):
        y2d, off, g2 = g["prev"]
        y = _final_call(g["h"], y2d, off, g2, norm_final, bb=g["bb"], tt=g["tt"], name=f"final_g{gi}")
        ab = [jnp.stack(s) for s in zip(*g["new_ab"])]
        outs.append((y, ab[0], ab[1], ab[2], ab[3], ab[4], jnp.stack(g["new_c"])))
    p, s = outs
    return (p[0], s[0], p[1], p[2], p[3], p[4], p[5], p[6], s[1], s[2], s[3], s[4], s[5], s[6])
```

```python
import functools

import numpy as np
import jax
import jax.numpy as jnp
from jax import lax
from jax.experimental import pallas as pl
from jax.experimental.pallas import tpu as pltpu

F32 = jnp.float32
BF16 = jnp.bfloat16
I32 = jnp.int32

D_MODEL = 1024
DEPTH = 2
CHUNK = 64
H_A, DK_A, DV_A, CONV_W = 4, 128, 128, 4
QKV_A = H_A * (2 * DK_A + DV_A)
H_B, DQK_B, DV_B = 4, 64, 128
H_C, DK_C, DV_C = 4, 256, 512
ROPE_BASE = 10000.0
PAST_LEN = 4096
AB_SIZES = (QKV_A, H_A, H_A, H_A * DV_A, H_B * DQK_B, H_B * DQK_B, H_B * DV_B, 2 * H_B, H_B * DV_B)
MIX_AB = H_A * DV_A + H_B * DV_B
MIX_C = H_C * DV_C
IN_C = 2 * H_C * DK_C + 2 * H_C * DV_C
N_EXPERTS, N_GROUPS, EPG, D_FF = 16, 4, 4, 512
N_MOD = 6
EPS = 1e-6

LANES = 128
AB_MAIN = 3584
N_GATE_COLS = 16
HX_S = D_MODEL // LANES
N_BUCKETS = N_GROUPS * 6
MOE_TM = 256
VMEM_LIMIT = 48 * 1024 * 1024

NN = ((1,), (0,))
NT = ((1,), (1,))
TN = ((0,), (0,))


def _dot(a, b, dims=NN):
    return lax.dot_general(a, b, (dims, ((), ())), preferred_element_type=F32)


def _mmb(a, b, dims=NN):
    return _dot(a.astype(BF16), b.astype(BF16), dims)


def _split2(x):
    hi = x.astype(BF16)
    lo = (x - hi.astype(F32)).astype(BF16)
    return hi, lo


def _split3(x):
    hi = x.astype(BF16)
    r = x - hi.astype(F32)
    mid = r.astype(BF16)
    lo = (r - mid.astype(F32)).astype(BF16)
    return hi, mid, lo


def _mm3(a, b, dims=NN):
    ah, al = _split2(a)
    bh, bl = _split2(b)
    return _dot(ah, bh, dims) + (_dot(ah, bl, dims) + _dot(al, bh, dims))


def _mm_mask_l(mask_bf16, x, dims=NN):
    h, m, l = _split3(x)
    return _dot(mask_bf16, h, dims) + (_dot(mask_bf16, m, dims) + _dot(mask_bf16, l, dims))


def _mm_mask_r(x, mask_bf16):
    h, m, l = _split3(x)
    return _dot(h, mask_bf16) + (_dot(m, mask_bf16) + _dot(l, mask_bf16))


def _sigmoid(x):
    return 1.0 / (1.0 + jnp.exp(-x))


def _silu(x):
    return x * _sigmoid(x)


def _softplus(x):
    return jnp.maximum(x, 0.0) + jnp.log(1.0 + jnp.exp(-jnp.abs(x)))


def _rms(x, eps=EPS):
    return x * lax.rsqrt(jnp.mean(x * x, axis=-1, keepdims=True) + eps)


def _cparams(sem):
    return pltpu.CompilerParams(dimension_semantics=sem, vmem_limit_bytes=VMEM_LIMIT)


def _mod_kernel(c_ref, w_ref, b_ref, o_ref):
    c = c_ref[...]
    o_ref[0] = _mm3(_silu(c), w_ref[0]) + b_ref[0]


def _mod_call(c_all, w_mod, b_mod):
    bt = c_all.shape[0]
    e = w_mod.shape[-1]
    tn = 1024
    return pl.pallas_call(
        _mod_kernel,
        grid=(DEPTH, e // tn),
        in_specs=[pl.BlockSpec((bt, D_MODEL), lambda l, j: (0, 0)),
                  pl.BlockSpec((1, D_MODEL, tn), lambda l, j: (l, 0, j)),
                  pl.BlockSpec((1, 1, tn), lambda l, j: (l, 0, j))],
        out_specs=pl.BlockSpec((1, bt, tn), lambda l, j: (l, 0, j)),
        out_shape=jax.ShapeDtypeStruct((DEPTH, bt, e), F32),
        compiler_params=_cparams(("parallel", "parallel")),
        name="mod",
    )(c_all, w_mod, b_mod.reshape(DEPTH, 1, e))


def _ln_mm_kernel(*refs, has_prev, has_gates):
    it = iter(refs)
    x_ref = next(it)
    if has_prev:
        yp_ref, gp_ref = next(it), next(it)
    gain_ref, sc_ref, sh_ref, w_ref = next(it), next(it), next(it), next(it)
    if has_gates:
        wg_ref, wgt_ref = next(it), next(it)
    if has_prev:
        h_ref = next(it)
    proj_ref = next(it)
    if has_gates:
        g_ref, gt_ref = next(it), next(it)
    hn_s = next(it)

    bb, tt, d = x_ref.shape
    rows = bb * tt

    @pl.when(pl.program_id(2) == 0)
    def _():
        x = x_ref[...]
        if has_prev:
            x = x + gp_ref[...] * yp_ref[...].reshape(bb, tt, d)
            h_ref[...] = x
        hn = _rms(x) * gain_ref[...]
        hn = hn * (1.0 + sc_ref[...]) + sh_ref[...]
        hn2 = hn.reshape(rows, d)
        hn_s[...] = hn2.astype(BF16)
        if has_gates:
            g_ref[...] = _mm3(hn2, wg_ref[...]).reshape(bb, tt, LANES)
            gt_ref[...] = _mm3(wgt_ref[...], hn2, NT)

    tn = proj_ref.shape[-1]
    proj_ref[...] = _dot(hn_s[...], w_ref[...]).reshape(bb, tt, tn).astype(proj_ref.dtype)


def _ln_mm_call(x, prev, gain, sc, sh, w, gates_w, *, bb, tt, tn, out_dtype, name):
    b, t, d = x.shape
    e = w.shape[1]
    rows = bb * tt
    nti = t // tt
    grid = (b // bb, nti, e // tn)
    has_prev = prev is not None
    has_gates = gates_w is not None
    xspec = pl.BlockSpec((bb, tt, d), lambda bi, i, j: (bi, i, 0))
    mspec = pl.BlockSpec((bb, 1, d), lambda bi, i, j: (bi, 0, 0))
    in_specs, args = [xspec], [x]
    if has_prev:
        y2d, off, gp = prev
        in_specs += [pl.BlockSpec((rows, d), lambda bi, i, j: (off + bi * nti + i, 0)), mspec]
        args += [y2d, gp]
    in_specs += [pl.BlockSpec((1, 1, d), lambda bi, i, j: (0, 0, 0)), mspec, mspec,
                 pl.BlockSpec((d, tn), lambda bi, i, j: (0, j))]
    args += [gain.reshape(1, 1, d), sc, sh, w]
    if has_gates:
        wg, wgt = gates_w
        in_specs += [pl.BlockSpec((d, LANES), lambda bi, i, j: (0, 0)),
                     pl.BlockSpec((N_GATE_COLS, d), lambda bi, i, j: (0, 0))]
        args += [wg, wgt]
    out_specs, out_shape = [], []
    if has_prev:
        out_specs.append(xspec)
        out_shape.append(jax.ShapeDtypeStruct((b, t, d), F32))
    out_specs.append(pl.BlockSpec((bb, tt, tn), lambda bi, i, j: (bi, i, j)))
    out_shape.append(jax.ShapeDtypeStruct((b, t, e), out_dtype))
    if has_gates:
        out_specs += [pl.BlockSpec((bb, tt, LANES), lambda bi, i, j: (bi, i, 0)),
                      pl.BlockSpec((N_GATE_COLS, rows), lambda bi, i, j: (0, bi * nti + i))]
        out_shape += [jax.ShapeDtypeStruct((b, t, LANES), F32),
                      jax.ShapeDtypeStruct((N_GATE_COLS, b * t), F32)]
    return pl.pallas_call(
        functools.partial(_ln_mm_kernel, has_prev=has_prev, has_gates=has_gates),
        grid=grid, in_specs=in_specs, out_specs=out_specs, out_shape=out_shape,
        scratch_shapes=[pltpu.VMEM((rows, d), BF16)],
        compiler_params=_cparams(("parallel", "parallel", "arbitrary")),
        name=name,
    )(*args)


def _tri_inv_all(a_list, eye, length, mm):
    xs = [-a for a in a_list]
    ps = [eye + x for x in xs]
    n = 2
    while n < length:
        xs = [mm(x, x) for x in xs]
        ps = [p + mm(p, x) for p, x in zip(ps, xs)]
        n *= 2
    return ps


def _mixer_ab_kernel(qkv_ref, z_ref, qkb_ref, vb_ref, ob_ref, g_ref, gt_ref,
                     conv0_ref, sa0_ref, cb0_ref, nb0_ref, mb0_ref,
                     convw_ref, gpar_ref, gpart_ref, na_ref, nbn_ref,
                     mixed_ref, conv_o, sa_o, cb_o, nb_o, mb_o,
                     xbuf, *, L, CB):
    TB = L * CB

    @pl.when(pl.program_id(1) == 0)
    def _():
        xbuf[5:8, :] = conv0_ref[0]
        sa_o[...] = sa0_ref[...]
        cb_o[...] = cb0_ref[...]
        nb_o[...] = nb0_ref[...]
        mb_o[...] = mb0_ref[...]

    xbuf[8:8 + TB, :] = qkv_ref[0]
    conv_o[0] = xbuf[5 + TB:8 + TB, :]

    ii = lax.broadcasted_iota(I32, (L, L), 0)
    jj = lax.broadcasted_iota(I32, (L, L), 1)
    tri = ii >= jj
    strict = ii > jj
    eye = (ii == jj).astype(F32)
    tril_b = tri.astype(BF16)
    triu_b = (ii <= jj).astype(BF16)
    lane = lax.broadcasted_iota(I32, (L, LANES), 1)
    rowi = lax.broadcasted_iota(I32, (N_GATE_COLS, L), 0)
    gpar = gpar_ref[...]
    gpart = gpart_ref[...]
    nea = -jnp.exp(gpar[0:1, :])
    neat = -jnp.exp(gpart[:, 0:1])
    norm_a = na_ref[...]
    norm_b = nbn_ref[...]

    mm = _mmb
    mm_inv = _mm3

    def conv_tile(r0, col0):
        acc = None
        for j in range(CONV_W):
            term = convw_ref[j:j + 1, col0:col0 + DK_A] * xbuf[5 + j + r0:5 + j + r0 + L, col0:col0 + DK_A]
            acc = term if acc is None else acc + term
        return _silu(acc)

    gates = []
    for c in range(CB):
        r0 = c * L
        pre = g_ref[0, r0:r0 + L, :] + gpar[1:2, :]
        csrc = jnp.where(lane < 4, nea * _softplus(pre),
                         jnp.where(lane >= 12, -_softplus(-pre), 0.0))
        csrc = jnp.where(lane < N_GATE_COLS, csrc, 0.0)
        gcum = _mm_mask_l(tril_b, csrc)
        pret = gt_ref[0, :, r0:r0 + L] + gpart[:, 1:2]
        csrct = jnp.where(rowi < 4, neat * _softplus(pret),
                          jnp.where(rowi >= 12, -_softplus(-pret), 0.0))
        gcumt = _mm_mask_r(csrct, triu_b)
        gates.append((pre, pret, gcum, gcumt, _sigmoid(pre)))

    ga = []
    a_list = []
    for c in range(CB):
        r0 = c * L
        pre, pret, gcum, gcumt, beta_all = gates[c]
        for h in range(H_A):
            q = conv_tile(r0, h * DK_A)
            k = conv_tile(r0, H_A * DK_A + h * DK_A)
            v = conv_tile(r0, 2 * H_A * DK_A + h * DV_A)
            q = q * lax.rsqrt(jnp.sum(q * q, axis=-1, keepdims=True) + EPS) * (DK_A ** -0.5)
            k = k * lax.rsqrt(jnp.sum(k * k, axis=-1, keepdims=True) + EPS)
            gc_c = gcum[:, h:h + 1]
            gc_r = gcumt[h:h + 1, :]
            decay = jnp.where(tri, jnp.exp(jnp.where(tri, gc_c - gc_r, 0.0)), 0.0)
            beta_c = beta_all[:, 4 + h:5 + h]
            kb = k * beta_c
            a_list.append(jnp.where(strict, mm_inv(kb, k, NT) * decay, 0.0))
            eg = jnp.exp(gc_c)
            gl = gc_c[L - 1:L, :]
            attn = jnp.where(tri, mm(q, k, NT) * decay, 0.0)
            ga.append(dict(kbeg=kb * eg, vb=v * beta_c, qeg=q * eg, attn=attn,
                           kdec=k * jnp.exp(gl - gc_c), sdec=jnp.exp(gl)))
    tinv = _tri_inv_all(a_list, eye, L, mm_inv)
    for d, t in zip(ga, tinv):
        d["w"] = mm_inv(t, d.pop("kbeg"))
        d["u"] = mm_inv(t, d.pop("vb"))

    gb = []
    for c in range(CB):
        r0 = c * L
        pre, pret, gcum, gcumt, _ = gates[c]
        for h in range(H_B):
            q = qkb_ref[0, r0:r0 + L, h * DQK_B:(h + 1) * DQK_B] * (DQK_B ** -0.5)
            k = qkb_ref[0, r0:r0 + L, H_B * DQK_B + h * DQK_B:H_B * DQK_B + (h + 1) * DQK_B]
            b_c = gcum[:, 12 + h:13 + h]
            b_r = gcumt[12 + h:13 + h, :]
            li_r = pret[8 + h:9 + h, :]
            dm = jnp.where(tri, b_c - b_r + li_r, -jnp.inf)
            gb.append(dict(q=q, k=k, qk=mm(q, k, NT), dm=dm, dmax=jnp.max(dm, axis=-1, keepdims=True),
                           b_c=b_c, li_c=pre[:, 8 + h:9 + h]))

    for c in range(CB):
        r0 = c * L
        for h in range(H_A):
            d = ga[c * H_A + h]
            s = sa_o[0, h]
            v_new = d["u"] - mm(d["w"], s)
            o = mm(d["qeg"], s) + mm(d["attn"], v_new)
            sa_o[0, h] = s * d["sdec"] + mm(d["kdec"], v_new, TN)
            zg = z_ref[0, r0:r0 + L, h * DV_A:(h + 1) * DV_A]
            o = _rms(o) * norm_a * _silu(zg)
            mixed_ref[0, r0:r0 + L, h * DV_A:(h + 1) * DV_A] = o.astype(mixed_ref.dtype)
        for h in range(H_B):
            d = gb[c * H_B + h]
            q, k, b_c = d["q"], d["k"], d["b_c"]
            v = vb_ref[0, r0:r0 + L, h * DV_B:(h + 1) * DV_B]
            m_prev = mb_o[0, h]
            a0 = b_c + m_prev
            m_t = jnp.maximum(a0, d["dmax"])
            w0 = jnp.exp(a0 - m_t)
            sm = d["qk"] * jnp.exp(d["dm"] - m_t)
            cst = cb_o[0, h]
            nst = nb_o[0, h]
            num = w0 * mm(q, cst) + mm(sm, v)
            den = w0 * jnp.sum(q * nst, axis=-1, keepdims=True) + jnp.sum(sm, axis=-1, keepdims=True)
            hh = num / jnp.maximum(jnp.abs(den), jnp.exp(-m_t))
            m_new = m_t[L - 1:L, :]
            wk = jnp.exp(b_c[L - 1:L, :] - b_c + d["li_c"] - m_new)
            w0l = w0[L - 1:L, :]
            kw = k * wk
            cb_o[0, h] = w0l * cst + mm(kw, v, TN)
            nb_o[0, h] = w0l * nst + jnp.sum(kw, axis=0, keepdims=True)
            mb_o[0, h] = m_new
            og = ob_ref[0, r0:r0 + L, h * DV_B:(h + 1) * DV_B]
            o = _rms(hh) * norm_b * _sigmoid(og)
            c0 = H_A * DV_A + h * DV_B
            mixed_ref[0, r0:r0 + L, c0:c0 + DV_B] = o.astype(mixed_ref.dtype)

    xbuf[5:8, :] = xbuf[5 + TB:8 + TB, :]


def _mixer_ab_call(proj, gates, gates_t, conv0, sa0, cb0, nb0, mb0, convw, gpar, gpart, norm_a, norm_b, *, L, CB, name):
    b, t, _ = proj.shape
    tb = L * CB
    nt = t // tb

    def col(width, idx):
        return pl.BlockSpec((1, tb, width), lambda bi, i: (bi, i, idx))

    def const(shape):
        return pl.BlockSpec(shape, lambda bi, i: (0,) * len(shape))

    def state(shape):
        return pl.BlockSpec((1,) + shape, lambda bi, i: (bi,) + (0,) * len(shape))

    in_specs = [col(QKV_A, 0), col(512, 3), col(512, 4), col(512, 5), col(512, 6),
                col(LANES, 0), pl.BlockSpec((1, N_GATE_COLS, tb), lambda bi, i: (bi, 0, i)),
                state((CONV_W - 1, QKV_A)), state((H_A, DK_A, DV_A)), state((H_B, DQK_B, DV_B)),
                state((H_B, 1, DQK_B)), state((H_B, 1, 1)),
                const((CONV_W, QKV_A)), const((8, LANES)), const((N_GATE_COLS, LANES)),
                const((1, DV_A)), const((1, DV_B))]
    out_specs = [pl.BlockSpec((1, tb, MIX_AB), lambda bi, i: (bi, i, 0)),
                 state((CONV_W - 1, QKV_A)), state((H_A, DK_A, DV_A)), state((H_B, DQK_B, DV_B)),
                 state((H_B, 1, DQK_B)), state((H_B, 1, 1))]
    out_shape = [jax.ShapeDtypeStruct((b, t, MIX_AB), BF16),
                 jax.ShapeDtypeStruct((b, CONV_W - 1, QKV_A), F32),
                 jax.ShapeDtypeStruct((b, H_A, DK_A, DV_A), F32),
                 jax.ShapeDtypeStruct((b, H_B, DQK_B, DV_B), F32),
                 jax.ShapeDtypeStruct((b, H_B, 1, DQK_B), F32),
                 jax.ShapeDtypeStruct((b, H_B, 1, 1), F32)]
    return pl.pallas_call(
        functools.partial(_mixer_ab_kernel, L=L, CB=CB),
        grid=(b, nt), in_specs=in_specs, out_specs=out_specs, out_shape=out_shape,
        scratch_shapes=[pltpu.VMEM((8 + tb, QKV_A), F32)],
        compiler_params=_cparams(("parallel", "arbitrary")),
        name=name,
    )(proj, proj, proj, proj, proj, gates, gates_t, conv0, sa0, cb0, nb0, mb0, convw, gpar, gpart, norm_a, norm_b)


def _mixer_c_kernel(q_ref, k_ref, v_ref, g_ref, cos_ref, sin_ref, s0_ref, mixed_ref, s_o, *, L, CB):
    @pl.when(pl.program_id(1) == 0)
    def _():
        s_o[...] = s0_ref[...]

    ii = lax.broadcasted_iota(I32, (L, L), 0)
    jj = lax.broadcasted_iota(I32, (L, L), 1)
    rel = (ii - jj).astype(F32)
    idx = lax.broadcasted_iota(I32, (L, 1), 0).astype(F32)
    half = DK_C // 2

    def rope(x, cos, sin):
        x1, x2 = x[:, :half], x[:, half:]
        return jnp.concatenate([x1 * cos - x2 * sin, x1 * sin + x2 * cos], axis=-1)

    for c in range(CB):
        r0 = c * L
        cos = cos_ref[r0:r0 + L, :]
        sin = sin_ref[r0:r0 + L, :]
        for h in range(H_C):
            lg = float(np.log1p(-np.exp2(-5.0 - h)))
            intra = jnp.where(rel >= 0, jnp.exp(lg * jnp.maximum(rel, 0.0)), 0.0)
            q_decay = jnp.exp(lg * (idx + 1.0))
            k_decay = jnp.exp(lg * (L - 1.0 - idx))
            s_decay = float(np.exp(lg * L))
            q = rope(q_ref[0, r0:r0 + L, h * DK_C:(h + 1) * DK_C].astype(F32), cos, sin)
            k = rope(k_ref[0, r0:r0 + L, h * DK_C:(h + 1) * DK_C].astype(F32), cos, sin) * (DK_C ** -0.5)
            v = v_ref[0, r0:r0 + L, h * DV_C:(h + 1) * DV_C]
            s = s_o[0, h]
            o = _mmb(_mmb(q, k, NT) * intra, v) + _mmb(q * q_decay, s)
            s_o[0, h] = s_decay * s + _mmb(k * k_decay, v, TN)
            gg = g_ref[0, r0:r0 + L, h * DV_C:(h + 1) * DV_C].astype(F32)
            o = _rms(o) * _silu(gg)
            mixed_ref[0, r0:r0 + L, h * DV_C:(h + 1) * DV_C] = o.astype(mixed_ref.dtype)


def _mixer_c_call(proj, cos, sin, s0, *, L, CB, name):
    b, t, _ = proj.shape
    tb = L * CB
    nt = t // tb
    qk_w, v_w = H_C * DK_C, H_C * DV_C
    sspec = pl.BlockSpec((1, H_C, DK_C, DV_C), lambda bi, i: (bi, 0, 0, 0))
    tspec = pl.BlockSpec((tb, DK_C // 2), lambda bi, i: (i, 0))
    return pl.pallas_call(
        functools.partial(_mixer_c_kernel, L=L, CB=CB),
        grid=(b, nt),
        in_specs=[pl.BlockSpec((1, tb, qk_w), lambda bi, i: (bi, i, 0)),
                  pl.BlockSpec((1, tb, qk_w), lambda bi, i: (bi, i, 1)),
                  pl.BlockSpec((1, tb, v_w), lambda bi, i: (bi, i, 1)),
                  pl.BlockSpec((1, tb, v_w), lambda bi, i: (bi, i, 2)),
                  tspec, tspec, sspec],
        out_specs=[pl.BlockSpec((1, tb, MIX_C), lambda bi, i: (bi, i, 0)), sspec],
        out_shape=[jax.ShapeDtypeStruct((b, t, MIX_C), BF16),
                   jax.ShapeDtypeStruct((b, H_C, DK_C, DV_C), F32)],
        compiler_params=_cparams(("parallel", "arbitrary")),
        name=name,
    )(proj, proj, proj, proj, cos, sin, s0)


def _out_route_kernel(*refs, has_alias):
    if has_alias:
        refs = refs[:9] + refs[10:]
    (mixed_ref, wout_ref, h_ref, g1_ref, gain_ref, sc_ref, sh_ref, wrt_ref, rb_ref,
     hnew_ref, hx_ref, bkt_ref, aux_ref) = refs
    bb, tt, m = mixed_ref.shape
    d = h_ref.shape[-1]
    rows = bb * tt
    y = _dot(mixed_ref[...].reshape(rows, m), wout_ref[...])
    h = h_ref[...] + g1_ref[...] * y.reshape(bb, tt, d)
    hnew_ref[...] = h
    hn = _rms(h) * gain_ref[...]
    hn = (hn * (1.0 + sc_ref[...]) + sh_ref[...]).reshape(rows, d)
    for j in range(HX_S):
        hx_ref[:, j, :] = hn[:, j * LANES:(j + 1) * LANES]

    logits = _mm3(wrt_ref[...], hn, NT)
    score = _sigmoid(logits)
    sel = score + rb_ref[...]

    def row(a, r):
        return a[r:r + 1, :]

    gscore = []
    for g in range(N_GROUPS):
        a, b, c, e = (row(sel, EPG * g + i) for i in range(EPG))
        hi_ab, lo_ab, hi_ce, lo_ce = jnp.maximum(a, b), jnp.minimum(a, b), jnp.maximum(c, e), jnp.minimum(c, e)
        top1 = jnp.maximum(hi_ab, hi_ce)
        top2 = jnp.maximum(jnp.maximum(lo_ab, lo_ce), jnp.minimum(hi_ab, hi_ce))
        gscore.append(top1 + top2)
    best = jnp.zeros((1, rows), I32)
    bestv = gscore[0]
    for g in range(1, N_GROUPS):
        upd = gscore[g] > bestv
        best = jnp.where(upd, g, best)
        bestv = jnp.where(upd, gscore[g], bestv)

    def pick(a, i):
        out = row(a, i)
        for g in range(1, N_GROUPS):
            out = jnp.where(best == g, row(a, EPG * g + i), out)
        return out

    vsel = [pick(sel, i) for i in range(EPG)]
    vsc = [pick(score, i) for i in range(EPG)]
    i1 = jnp.zeros((1, rows), I32)
    v1 = vsel[0]
    for i in range(1, EPG):
        upd = vsel[i] > v1
        i1 = jnp.where(upd, i, i1)
        v1 = jnp.where(upd, vsel[i], v1)
    i2 = jnp.full((1, rows), -1, I32)
    v2 = jnp.full((1, rows), -jnp.inf, F32)
    for i in range(EPG):
        cand = jnp.where(i1 == i, -jnp.inf, vsel[i])
        upd = cand > v2
        i2 = jnp.where(upd, i, i2)
        v2 = jnp.where(upd, cand, v2)
    s1 = jnp.zeros((1, rows), F32)
    s2 = jnp.zeros((1, rows), F32)
    for i in range(EPG):
        s1 = jnp.where(i1 == i, vsc[i], s1)
        s2 = jnp.where(i2 == i, vsc[i], s2)
    den = s1 + s2
    w1 = s1 / den
    w2 = s2 / den
    first_lo = i1 < i2
    lo = jnp.minimum(i1, i2)
    hi = jnp.maximum(i1, i2)
    wlo = jnp.where(first_lo, w1, w2)
    whi = jnp.where(first_lo, w2, w1)
    pair = jnp.where(lo == 0, hi - 1, jnp.where(lo == 1, hi + 1, 5))
    bkt_ref[...] = (best * 6 + pair).reshape(1, 1, rows)

    aux_ref[...] = jnp.concatenate([wlo, whi, jnp.zeros((6, rows), F32)], axis=0)


def _out_route_call(mixed, wout, h, g1, gain, sc, sh, wrt, rb, hx_buf, ntot, row_block_off, *, bb, tt, name):
    b, t, m = mixed.shape
    d = h.shape[-1]
    rows = bb * tt
    nti = t // tt
    has_alias = hx_buf is not None
    xspec = pl.BlockSpec((bb, tt, d), lambda bi, i: (bi, i, 0))
    mspec = pl.BlockSpec((bb, 1, d), lambda bi, i: (bi, 0, 0))
    in_specs = [pl.BlockSpec((bb, tt, m), lambda bi, i: (bi, i, 0)),
                pl.BlockSpec((m, d), lambda bi, i: (0, 0)),
                xspec, mspec,
                pl.BlockSpec((1, 1, d), lambda bi, i: (0, 0, 0)), mspec, mspec,
                pl.BlockSpec((N_EXPERTS, d), lambda bi, i: (0, 0)),
                pl.BlockSpec((N_EXPERTS, 1), lambda bi, i: (0, 0))]
    args = [mixed, wout, h, g1, gain.reshape(1, 1, d), sc, sh, wrt, rb]
    if has_alias:
        in_specs.append(pl.BlockSpec(memory_space=pl.ANY))
        args.append(hx_buf)
    return pl.pallas_call(
        functools.partial(_out_route_kernel, has_alias=has_alias),
        grid=(b // bb, nti),
        in_specs=in_specs,
        out_specs=[xspec,
                   pl.BlockSpec((rows, HX_S, LANES), lambda bi, i: (row_block_off + bi * nti + i, 0, 0)),
                   pl.BlockSpec((1, 1, rows), lambda bi, i: (bi * nti + i, 0, 0)),
                   pl.BlockSpec((8, rows), lambda bi, i: (0, bi * nti + i))],
        out_shape=[jax.ShapeDtypeStruct((b, t, d), F32),
                   jax.ShapeDtypeStruct((ntot, HX_S, LANES), F32),
                   jax.ShapeDtypeStruct(((b // bb) * nti, 1, rows), I32),
                   jax.ShapeDtypeStruct((8, b * t), F32)],
        input_output_aliases={9: 1} if has_alias else {},
        compiler_params=_cparams(("parallel", "parallel")),
        name=name,
    )(*args)


def _moe_kernel(ea_ref, eb_ref, chg_ref, src_ref, hx_hbm, aux_ref, wga_ref, wgb_ref, wua_ref, wub_ref, wda_ref,
                wdb_ref, ys_ref, xbuf, wg_s, wu_s, wd_s, sem):
    t = pl.program_id(0)
    nt = pl.num_programs(0)
    slot = lax.rem(t, 2)

    def row_copy(tile, r, s):
        row = src_ref[tile * MOE_TM + r]
        return pltpu.make_async_copy(hx_hbm.at[row], xbuf.at[s, r], sem.at[s])

    def start_rows(tile, s):
        for r in range(MOE_TM):
            row_copy(tile, r, s).start()

    def wait_rows(tile, s):
        for r in range(MOE_TM):
            row_copy(tile, r, s).wait()

    @pl.when(t == 0)
    def _():
        start_rows(0, 0)

    wait_rows(t, slot)
    nxt = jnp.minimum(t + 1, nt - 1)
    start_rows(nxt, 1 - slot)

    @pl.when(chg_ref[t] > 0)
    def _():
        rc = 128
        for i in range(D_MODEL // rc):
            wg_s[0, i * rc:(i + 1) * rc, :] = wga_ref[0, 0, i * rc:(i + 1) * rc, :].astype(BF16)
            wg_s[1, i * rc:(i + 1) * rc, :] = wgb_ref[0, 0, i * rc:(i + 1) * rc, :].astype(BF16)
            wu_s[0, i * rc:(i + 1) * rc, :] = wua_ref[0, 0, i * rc:(i + 1) * rc, :].astype(BF16)
            wu_s[1, i * rc:(i + 1) * rc, :] = wub_ref[0, 0, i * rc:(i + 1) * rc, :].astype(BF16)
        for i in range(D_FF // rc):
            wd_s[0, i * rc:(i + 1) * rc, :] = wda_ref[0, 0, i * rc:(i + 1) * rc, :].astype(BF16)
            wd_s[1, i * rc:(i + 1) * rc, :] = wdb_ref[0, 0, i * rc:(i + 1) * rc, :].astype(BF16)

    x = jnp.concatenate([xbuf[slot, :, j, :] for j in range(HX_S)], axis=-1).astype(BF16)
    ri = lax.broadcasted_iota(I32, (MOE_TM, MOE_TM), 0)
    ci = lax.broadcasted_iota(I32, (MOE_TM, MOE_TM), 1)
    wcol = _mm_mask_l((ri == ci).astype(BF16), aux_ref[...], NT)
    wlo = wcol[:, 0:1]
    whi = wcol[:, 1:2]
    act_a = (_silu(_dot(x, wg_s[0])) * _dot(x, wu_s[0]) * wlo).astype(BF16)
    act_b = (_silu(_dot(x, wg_s[1])) * _dot(x, wu_s[1]) * whi).astype(BF16)
    ys_ref[...] = _dot(act_a, wd_s[0]) + _dot(act_b, wd_s[1])

    @pl.when(t == nt - 1)
    def _():
        wait_rows(nxt, 1 - slot)


def _moe_call(ea, eb, chg, src, hx, aux_sorted, wg, wu, wd, layer):
    p = src.shape[0]
    n_tiles = p // MOE_TM
    wa_in = pl.BlockSpec((1, 1, D_MODEL, D_FF), lambda t, ea, eb, ch, sr: (layer, ea[t], 0, 0))
    wb_in = pl.BlockSpec((1, 1, D_MODEL, D_FF), lambda t, ea, eb, ch, sr: (layer, eb[t], 0, 0))
    wa_out = pl.BlockSpec((1, 1, D_FF, D_MODEL), lambda t, ea, eb, ch, sr: (layer, ea[t], 0, 0))
    wb_out = pl.BlockSpec((1, 1, D_FF, D_MODEL), lambda t, ea, eb, ch, sr: (layer, eb[t], 0, 0))
    return pl.pallas_call(
        _moe_kernel,
        grid_spec=pltpu.PrefetchScalarGridSpec(
            num_scalar_prefetch=4, grid=(n_tiles,),
            in_specs=[pl.BlockSpec(memory_space=pl.ANY),
                      pl.BlockSpec((8, MOE_TM), lambda t, ea, eb, ch, sr: (0, t)),
                      wa_in, wb_in, wa_in, wb_in, wa_out, wb_out],
            out_specs=pl.BlockSpec((MOE_TM, D_MODEL), lambda t, ea, eb, ch, sr: (t, 0)),
            scratch_shapes=[pltpu.VMEM((2, MOE_TM, HX_S, LANES), F32),
                            pltpu.VMEM((2, D_MODEL, D_FF), BF16), pltpu.VMEM((2, D_MODEL, D_FF), BF16),
                            pltpu.VMEM((2, D_FF, D_MODEL), BF16),
                            pltpu.SemaphoreType.DMA((2,))]),
        out_shape=jax.ShapeDtypeStruct((p, D_MODEL), F32),
        compiler_params=_cparams(("arbitrary",)),
        name="moe",
    )(ea, eb, chg, src, hx, aux_sorted, wg, wg, wu, wu, wd, wd)


_PAIRS = [(0, 1), (0, 2), (0, 3), (1, 2), (1, 3), (2, 3)]
_EA_TABLE = np.array([g * EPG + p[0] for g in range(N_GROUPS) for p in _PAIRS], np.int32)
_EB_TABLE = np.array([g * EPG + p[1] for g in range(N_GROUPS) for p in _PAIRS], np.int32)


def _moe(hx, aux, bkt, wg, wu, wd, layer):
    n = hx.shape[0]
    tm = MOE_TM
    ar = jnp.arange(N_BUCKETS, dtype=I32)
    oh = (bkt[:, None] == ar[None, :]).astype(I32)
    blk = LANES
    nb = -(-n // blk)
    ohb = jnp.pad(oh, ((0, nb * blk - n), (0, 0))).reshape(nb, blk, N_BUCKETS)
    tril = jnp.tril(jnp.ones((blk, blk), F32))
    inner = jnp.einsum('ij,bjk->bik', tril, ohb.astype(F32))
    tot = inner[:, -1, :]
    outer = jnp.cumsum(tot, axis=0) - tot
    cs = (inner + outer[:, None, :]).reshape(nb * blk, N_BUCKETS)[:n].astype(I32)
    rank = jnp.sum(oh * cs, axis=1) - 1
    counts = cs[-1]
    padded = ((counts + tm - 1) // tm) * tm
    ends = jnp.cumsum(padded)
    offs = ends - padded
    pos = jnp.sum(oh * offs[None, :], axis=1) + rank
    n_tiles = -(-n // tm) + N_BUCKETS
    p = n_tiles * tm
    src = jnp.zeros((p,), I32).at[pos].set(jnp.arange(n, dtype=I32))
    tstart = jnp.arange(n_tiles, dtype=I32) * tm
    tb = jnp.sum((ends[None, :] <= tstart[:, None]).astype(I32), axis=1)
    n_used = ends[-1] // tm
    valid = (jnp.arange(n_tiles, dtype=I32) < n_used).astype(I32)
    last_b = tb[jnp.maximum(n_used - 1, 0)]
    tb = jnp.clip(jnp.where(valid > 0, tb, last_b), 0, N_BUCKETS - 1)
    ea = jnp.asarray(_EA_TABLE)[tb]
    eb = jnp.asarray(_EB_TABLE)[tb]
    chg = jnp.concatenate([jnp.ones((1,), I32), (tb[1:] != tb[:-1]).astype(I32)])
    ys = _moe_call(ea, eb, chg, src, hx, jnp.take(aux, src, axis=1), wg, wu, wd, layer)
    return jnp.take(ys, pos, axis=0)


def _final_kernel(h_ref, y_ref, g_ref, gain_ref, o_ref):
    bb, tt, d = h_ref.shape
    h = h_ref[...] + g_ref[...] * y_ref[...].reshape(bb, tt, d)
    o_ref[...] = _rms(h) * gain_ref[...]


def _final_call(h, y2d, off, g2, gain, *, bb, tt, name):
    b, t, d = h.shape
    nti = t // tt
    rows = bb * tt
    xspec = pl.BlockSpec((bb, tt, d), lambda bi, i: (bi, i, 0))
    return pl.pallas_call(
        _final_kernel,
        grid=(b // bb, nti),
        in_specs=[xspec, pl.BlockSpec((rows, d), lambda bi, i: (off + bi * nti + i, 0)),
                  pl.BlockSpec((bb, 1, d), lambda bi, i: (bi, 0, 0)),
                  pl.BlockSpec((1, 1, d), lambda bi, i: (0, 0, 0))],
        out_specs=xspec,
        out_shape=jax.ShapeDtypeStruct((b, t, d), F32),
        compiler_params=_cparams(("parallel", "parallel")),
        name=name,
    )(h, y2d, g2, gain.reshape(1, 1, d))


def _group_cfg(b, t):
    if t >= 512:
        return dict(bb=1, tt=512, L=CHUNK, CB=4)
    return dict(bb=b, tt=t, L=min(CHUNK, t), CB=1)


def _rope_tables(pos0, t):
    half = DK_C // 2
    inv = jnp.power(ROPE_BASE, -jnp.linspace(0.0, 1.0, half, dtype=F32))
    ang = (pos0 + jnp.arange(t, dtype=F32))[:, None] * inv[None, :]
    return jnp.cos(ang), jnp.sin(ang)


def kernel(x_prompt, x_sample, c_prompt, c_sample, state_a_conv, state_a_rec, state_b_c, state_b_n, state_b_m,
           state_c_rec, w_mod, b_mod, norm_mix, norm_ffn, w_in_ab, conv_a, a_log, dt_bias, norm_a, gate_bias_b,
           norm_b, w_out_ab, w_in_c, w_out_c, w_router, router_bias, w_gate, w_up, w_down, norm_final):
    d = D_MODEL
    bp, tp, _ = x_prompt.shape
    bs, ts, _ = x_sample.shape
    n_ab, n_c = w_in_ab.shape[0], w_in_c.shape[0]

    mod_all = _mod_call(jnp.concatenate([c_prompt, c_sample], axis=0), w_mod, b_mod)

    o = np.cumsum((0,) + AB_SIZES)
    w_ab_main = jnp.concatenate([w_in_ab[:, :, o[0]:o[1]].astype(BF16), w_in_ab[:, :, o[3]:o[7]].astype(BF16),
                                 w_in_ab[:, :, o[8]:o[9]].astype(BF16)], axis=-1)
    w_ab_gate = jnp.concatenate([w_in_ab[:, :, o[1]:o[3]], w_in_ab[:, :, o[7]:o[8]]], axis=-1)
    w_ab_gate_pad = jnp.pad(w_ab_gate, ((0, 0), (0, 0), (0, LANES - N_GATE_COLS)))
    w_ab_gate_t = jnp.swapaxes(w_ab_gate, 1, 2)
    w_c = w_in_c.astype(BF16)
    w_out_ab_b = w_out_ab.astype(BF16)
    w_out_c_b = w_out_c.astype(BF16)
    wrt = w_router.T
    rb = router_bias.reshape(N_EXPERTS, 1)
    zeros4 = jnp.zeros((n_ab, H_A), F32)
    gpar_row0 = jnp.concatenate([a_log, jnp.zeros((n_ab, LANES - H_A), F32)], axis=1)
    gpar_row1 = jnp.concatenate([dt_bias, zeros4, gate_bias_b, jnp.zeros((n_ab, LANES - 16), F32)], axis=1)
    gpar = jnp.concatenate([gpar_row0[:, None], gpar_row1[:, None], jnp.zeros((n_ab, 6, LANES), F32)], axis=1)
    gpart = jnp.swapaxes(jnp.concatenate([gpar[:, :2, :N_GATE_COLS], jnp.zeros((n_ab, LANES - 2, N_GATE_COLS), F32)],
                                         axis=1), 1, 2)

    groups = []
    zf = lambda *s: jnp.zeros(s, F32)
    groups.append(dict(
        x=x_prompt, b=bp, t=tp, pos0=0.0, mod=mod_all[:, :bp], row0=0,
        conv=zf(n_ab, bp, CONV_W - 1, QKV_A), sa=zf(n_ab, bp, H_A, DK_A, DV_A), cb=zf(n_ab, bp, H_B, DQK_B, DV_B),
        nb=zf(n_ab, bp, H_B, DQK_B), mb=zf(n_ab, bp, H_B), sc=zf(n_c, bp, H_C, DK_C, DV_C)))
    groups.append(dict(
        x=x_sample, b=bs, t=ts, pos0=float(PAST_LEN), mod=mod_all[:, bp:], row0=bp * tp,
        conv=state_a_conv, sa=state_a_rec, cb=state_b_c, nb=state_b_n, mb=state_b_m, sc=state_c_rec))
    n_tot = bp * tp + bs * ts
    for g in groups:
        g.update(_group_cfg(g["b"], g["t"]))
        g["h"] = g["x"]
        g["prev"] = None
        g["new_ab"], g["new_c"] = [], []
        g["cos"], g["sin"] = _rope_tables(g["pos0"], g["t"])
        g["roff"] = g["row0"] // (g["bb"] * g["tt"])

    for layer in range(DEPTH):
        li = layer // 2
        if layer == 0:
            hx_buf = jnp.zeros((n_tot, HX_S, LANES), F32)
        bkts, auxs = [], []
        for gi, g in enumerate(groups):
            b, t, bb, tt = g["b"], g["t"], g["bb"], g["tt"]
            mods = [m.reshape(b, 1, d) for m in jnp.split(g["mod"][layer], N_MOD, axis=-1)]
            sh1, sc1, g1, sh2, sc2, g2 = mods
            tag = f"l{layer}g{gi}"
            if layer % 2 == 0:
                outs = _ln_mm_call(g["h"], g["prev"], norm_mix[layer], sc1, sh1, w_ab_main[li],
                                   (w_ab_gate_pad[li], w_ab_gate_t[li]), bb=bb, tt=tt, tn=512, out_dtype=F32,
                                   name="ln_ab_" + tag)
                if g["prev"] is not None:
                    g["h"], outs = outs[0], outs[1:]
                proj, gates, gates_t = outs
                gates_t = gates_t.reshape(N_GATE_COLS, b, t).transpose(1, 0, 2)
                mixed, conv_n, sa_n, cb_n, nb_n, mb_n = _mixer_ab_call(
                    proj, gates, gates_t, g["conv"][li], g["sa"][li], g["cb"][li],
                    g["nb"][li].reshape(b, H_B, 1, DQK_B), g["mb"][li].reshape(b, H_B, 1, 1),
                    conv_a[li], gpar[li], gpart[li], norm_a[li].reshape(1, DV_A), norm_b[li].reshape(1, DV_B),
                    L=g["L"], CB=g["CB"], name="mixer_ab_" + tag)
                g["new_ab"].append((conv_n, sa_n, cb_n, nb_n.reshape(b, H_B, DQK_B), mb_n.reshape(b, H_B)))
                wout = w_out_ab_b[li]
            else:
                outs = _ln_mm_call(g["h"], g["prev"], norm_mix[layer], sc1, sh1, w_c[li], None,
                                   bb=bb, tt=tt, tn=512, out_dtype=BF16, name="ln_c_" + tag)
                if g["prev"] is not None:
                    g["h"], outs = outs[0], outs[1:]
                (proj,) = outs
                mixed, sc_n = _mixer_c_call(proj, g["cos"], g["sin"], g["sc"][li], L=g["L"], CB=g["CB"],
                                            name="mixer_c_" + tag)
                g["new_c"].append(sc_n)
                wout = w_out_c_b[li]
            g["h"], hx_buf, bkt, aux = _out_route_call(mixed, wout, g["h"], g1, norm_ffn[layer], sc2, sh2, wrt, rb,
                                                       hx_buf, n_tot, g["roff"], bb=bb, tt=tt,
                                                       name="out_route_" + tag)
            bkts.append(bkt.reshape(-1))
            auxs.append(aux)
            g["g2"] = g2
        y_moe = _moe(hx_buf, jnp.concatenate(auxs, axis=1), jnp.concatenate(bkts),
                     w_gate, w_up, w_down, layer)
        for g in groups:
            g["prev"] = (y_moe, g["roff"], g["g2"])

    outs = []
    for gi, g in enumerate(groups):
        y2d, off, g2 = g["prev"]
        y = _final_call(g["h"], y2d, off, g2, norm_final, bb=g["bb"], tt=g["tt"], name=f"final_g{gi}")
        ab = [jnp.stack(s) for s in zip(*g["new_ab"])]
        outs.append((y, ab[0], ab[1], ab[2], ab[3], ab[4], jnp.stack(g["new_c"])))
    p, s = outs
    return (p[0], s[0], p[1], p[2], p[3], p[4], p[5], p[6], s[1], s[2], s[3], s[4], s[5], s[6])
```

```python
import functools

import numpy as np
import jax
import jax.numpy as jnp
from jax import lax
from jax.experimental import pallas as pl
from jax.experimental.pallas import tpu as pltpu

F32 = jnp.float32
BF16 = jnp.bfloat16
I32 = jnp.int32

D_MODEL = 1024
DEPTH = 2
CHUNK = 64
H_A, DK_A, DV_A, CONV_W = 4, 128, 128, 4
QKV_A = H_A * (2 * DK_A + DV_A)
H_B, DQK_B, DV_B = 4, 64, 128
H_C, DK_C, DV_C = 4, 256, 512
ROPE_BASE = 10000.0
PAST_LEN = 4096
AB_SIZES = (QKV_A, H_A, H_A, H_A * DV_A, H_B * DQK_B, H_B * DQK_B, H_B * DV_B, 2 * H_B, H_B * DV_B)
MIX_AB = H_A * DV_A + H_B * DV_B
MIX_C = H_C * DV_C
IN_C = 2 * H_C * DK_C + 2 * H_C * DV_C
N_EXPERTS, N_GROUPS, EPG, D_FF = 16, 4, 4, 512
N_MOD = 6
EPS = 1e-6

LANES = 128
AB_MAIN = 3584
N_GATE_COLS = 16
HX_W = D_MODEL + LANES
N_BUCKETS = N_GROUPS * 6
MOE_TM = 256
VMEM_LIMIT = 48 * 1024 * 1024

NN = ((1,), (0,))
NT = ((1,), (1,))
TN = ((0,), (0,))


def _dot(a, b, dims=NN):
    return lax.dot_general(a, b, (dims, ((), ())), preferred_element_type=F32)


def _mmb(a, b, dims=NN):
    return _dot(a.astype(BF16), b.astype(BF16), dims)


def _split2(x):
    hi = x.astype(BF16)
    lo = (x - hi.astype(F32)).astype(BF16)
    return hi, lo


def _split3(x):
    hi = x.astype(BF16)
    r = x - hi.astype(F32)
    mid = r.astype(BF16)
    lo = (r - mid.astype(F32)).astype(BF16)
    return hi, mid, lo


def _mm3(a, b, dims=NN):
    ah, al = _split2(a)
    bh, bl = _split2(b)
    return _dot(ah, bh, dims) + (_dot(ah, bl, dims) + _dot(al, bh, dims))


def _mm_mask_l(mask_bf16, x, dims=NN):
    h, m, l = _split3(x)
    return _dot(mask_bf16, h, dims) + (_dot(mask_bf16, m, dims) + _dot(mask_bf16, l, dims))


def _mm_mask_r(x, mask_bf16):
    h, m, l = _split3(x)
    return _dot(h, mask_bf16) + (_dot(m, mask_bf16) + _dot(l, mask_bf16))


def _sigmoid(x):
    return 1.0 / (1.0 + jnp.exp(-x))


def _silu(x):
    return x * _sigmoid(x)


def _softplus(x):
    return jnp.maximum(x, 0.0) + jnp.log(1.0 + jnp.exp(-jnp.abs(x)))


def _rms(x, eps=EPS):
    return x * lax.rsqrt(jnp.mean(x * x, axis=-1, keepdims=True) + eps)


def _cparams(sem):
    return pltpu.CompilerParams(dimension_semantics=sem, vmem_limit_bytes=VMEM_LIMIT)


def _mod_kernel(c_ref, w_ref, b_ref, o_ref):
    c = c_ref[...]
    o_ref[0] = _mm3(_silu(c), w_ref[0]) + b_ref[0]


def _mod_call(c_all, w_mod, b_mod):
    bt = c_all.shape[0]
    e = w_mod.shape[-1]
    tn = 1024
    return pl.pallas_call(
        _mod_kernel,
        grid=(DEPTH, e // tn),
        in_specs=[pl.BlockSpec((bt, D_MODEL), lambda l, j: (0, 0)),
                  pl.BlockSpec((1, D_MODEL, tn), lambda l, j: (l, 0, j)),
                  pl.BlockSpec((1, 1, tn), lambda l, j: (l, 0, j))],
        out_specs=pl.BlockSpec((1, bt, tn), lambda l, j: (l, 0, j)),
        out_shape=jax.ShapeDtypeStruct((DEPTH, bt, e), F32),
        compiler_params=_cparams(("parallel", "parallel")),
        name="mod",
    )(c_all, w_mod, b_mod.reshape(DEPTH, 1, e))


def _ln_mm_kernel(*refs, has_prev, has_gates, tn):
    it = iter(refs)
    x_ref = next(it)
    if has_prev:
        yp_ref, gp_ref = next(it), next(it)
    gain_ref, sc_ref, sh_ref, w_ref = next(it), next(it), next(it), next(it)
    if has_gates:
        wg_ref, wgt_ref = next(it), next(it)
    if has_prev:
        h_ref = next(it)
    proj_ref = next(it)
    if has_gates:
        g_ref, gt_ref = next(it), next(it)
    hn_s = next(it)

    bb, tt, d = x_ref.shape
    rows = bb * tt
    x = x_ref[...]
    if has_prev:
        x = x + gp_ref[...] * yp_ref[...].reshape(bb, tt, d)
        h_ref[...] = x
    hn = _rms(x) * gain_ref[...]
    hn = hn * (1.0 + sc_ref[...]) + sh_ref[...]
    hn2 = hn.reshape(rows, d)
    hn_s[...] = hn2.astype(BF16)
    if has_gates:
        g_ref[...] = _mm3(hn2, wg_ref[...]).reshape(bb, tt, LANES)
        gt_ref[...] = _mm3(wgt_ref[...], hn2, NT)
    for j in range(proj_ref.shape[-1] // tn):
        cols = slice(j * tn, (j + 1) * tn)
        proj_ref[:, :, cols] = _dot(hn_s[...], w_ref[:, cols]).reshape(bb, tt, tn).astype(proj_ref.dtype)


def _ln_mm_call(x, prev, gain, sc, sh, w, gates_w, *, bb, tt, tn, out_dtype, name):
    b, t, d = x.shape
    e = w.shape[1]
    rows = bb * tt
    nti = t // tt
    grid = (b // bb, nti)
    has_prev = prev is not None
    has_gates = gates_w is not None
    xspec = pl.BlockSpec((bb, tt, d), lambda bi, i: (bi, i, 0))
    mspec = pl.BlockSpec((bb, 1, d), lambda bi, i: (bi, 0, 0))
    in_specs, args = [xspec], [x]
    if has_prev:
        y2d, off, gp = prev
        in_specs += [pl.BlockSpec((rows, d), lambda bi, i: (off + bi * nti + i, 0)), mspec]
        args += [y2d, gp]
    in_specs += [pl.BlockSpec((1, 1, d), lambda bi, i: (0, 0, 0)), mspec, mspec,
                 pl.BlockSpec((d, e), lambda bi, i: (0, 0), pipeline_mode=pl.Buffered(1))]
    args += [gain.reshape(1, 1, d), sc, sh, w]
    if has_gates:
        wg, wgt = gates_w
        in_specs += [pl.BlockSpec((d, LANES), lambda bi, i: (0, 0)),
                     pl.BlockSpec((N_GATE_COLS, d), lambda bi, i: (0, 0))]
        args += [wg, wgt]
    out_specs, out_shape = [], []
    if has_prev:
        out_specs.append(xspec)
        out_shape.append(jax.ShapeDtypeStruct((b, t, d), F32))
    out_specs.append(pl.BlockSpec((bb, tt, e), lambda bi, i: (bi, i, 0)))
    out_shape.append(jax.ShapeDtypeStruct((b, t, e), out_dtype))
    if has_gates:
        out_specs += [pl.BlockSpec((bb, tt, LANES), lambda bi, i: (bi, i, 0)),
                      pl.BlockSpec((N_GATE_COLS, rows), lambda bi, i: (0, bi * nti + i))]
        out_shape += [jax.ShapeDtypeStruct((b, t, LANES), F32),
                      jax.ShapeDtypeStruct((N_GATE_COLS, b * t), F32)]
    return pl.pallas_call(
        functools.partial(_ln_mm_kernel, has_prev=has_prev, has_gates=has_gates, tn=tn),
        grid=grid, in_specs=in_specs, out_specs=out_specs, out_shape=out_shape,
        scratch_shapes=[pltpu.VMEM((rows, d), BF16)],
        compiler_params=_cparams(("parallel", "parallel")),
        name=name,
    )(*args)


def _tri_inv_all(a_list, eye, length, mm):
    xs = [-a for a in a_list]
    ps = [eye + x for x in xs]
    n = 2
    while n < length:
        xs = [mm(x, x) for x in xs]
        ps = [p + mm(p, x) for p, x in zip(ps, xs)]
        n *= 2
    return ps


def _mixer_ab_kernel(qkv_ref, z_ref, qkb_ref, vb_ref, ob_ref, g_ref, gt_ref,
                     conv0_ref, sa0_ref, cb0_ref, nb0_ref, mb0_ref,
                     convw_ref, gpar_ref, gpart_ref, na_ref, nbn_ref,
                     mixed_ref, conv_o, sa_o, cb_o, nb_o, mb_o,
                     xbuf, *, L, CB):
    TB = L * CB

    @pl.when(pl.program_id(1) == 0)
    def _():
        xbuf[5:8, :] = conv0_ref[0]
        sa_o[...] = sa0_ref[...]
        cb_o[...] = cb0_ref[...]
        nb_o[...] = nb0_ref[...]
        mb_o[...] = mb0_ref[...]

    xbuf[8:8 + TB, :] = qkv_ref[0]
    conv_o[0] = xbuf[5 + TB:8 + TB, :]

    ii = lax.broadcasted_iota(I32, (L, L), 0)
    jj = lax.broadcasted_iota(I32, (L, L), 1)
    tri = ii >= jj
    strict = ii > jj
    eye = (ii == jj).astype(F32)
    tril_b = tri.astype(BF16)
    triu_b = (ii <= jj).astype(BF16)
    lane = lax.broadcasted_iota(I32, (L, LANES), 1)
    rowi = lax.broadcasted_iota(I32, (N_GATE_COLS, L), 0)
    gpar = gpar_ref[...]
    gpart = gpart_ref[...]
    nea = -jnp.exp(gpar[0:1, :])
    neat = -jnp.exp(gpart[:, 0:1])
    norm_a = na_ref[...]
    norm_b = nbn_ref[...]

    mm = _mmb
    mm_inv = _mm3

    def conv_tile(r0, col0):
        acc = None
        for j in range(CONV_W):
            term = convw_ref[j:j + 1, col0:col0 + DK_A] * xbuf[5 + j + r0:5 + j + r0 + L, col0:col0 + DK_A]
            acc = term if acc is None else acc + term
        return _silu(acc)

    gates = []
    for c in range(CB):
        r0 = c * L
        pre = g_ref[0, r0:r0 + L, :] + gpar[1:2, :]
        csrc = jnp.where(lane < 4, nea * _softplus(pre),
                         jnp.where(lane >= 12, -_softplus(-pre), 0.0))
        csrc = jnp.where(lane < N_GATE_COLS, csrc, 0.0)
        gcum = _mm_mask_l(tril_b, csrc)
        pret = gt_ref[0, :, r0:r0 + L] + gpart[:, 1:2]
        csrct = jnp.where(rowi < 4, neat * _softplus(pret),
                          jnp.where(rowi >= 12, -_softplus(-pret), 0.0))
        gcumt = _mm_mask_r(csrct, triu_b)
        gates.append((pre, pret, gcum, gcumt, _sigmoid(pre)))

    ga = []
    a_list = []
    for c in range(CB):
        r0 = c * L
        pre, pret, gcum, gcumt, beta_all = gates[c]
        for h in range(H_A):
            q = conv_tile(r0, h * DK_A)
            k = conv_tile(r0, H_A * DK_A + h * DK_A)
            v = conv_tile(r0, 2 * H_A * DK_A + h * DV_A)
            q = q * lax.rsqrt(jnp.sum(q * q, axis=-1, keepdims=True) + EPS) * (DK_A ** -0.5)
            k = k * lax.rsqrt(jnp.sum(k * k, axis=-1, keepdims=True) + EPS)
            gc_c = gcum[:, h:h + 1]
            gc_r = gcumt[h:h + 1, :]
            decay = jnp.where(tri, jnp.exp(jnp.where(tri, gc_c - gc_r, 0.0)), 0.0)
            beta_c = beta_all[:, 4 + h:5 + h]
            kb = k * beta_c
            a_list.append(jnp.where(strict, mm_inv(kb, k, NT) * decay, 0.0))
            eg = jnp.exp(gc_c)
            gl = gc_c[L - 1:L, :]
            attn = jnp.where(tri, mm(q, k, NT) * decay, 0.0)
            ga.append(dict(kbeg=kb * eg, vb=v * beta_c, qeg=q * eg, attn=attn,
                           kdec=k * jnp.exp(gl - gc_c), sdec=jnp.exp(gl)))
    tinv = _tri_inv_all(a_list, eye, L, mm_inv)
    for d, t in zip(ga, tinv):
        d["w"] = mm_inv(t, d.pop("kbeg"))
        d["u"] = mm_inv(t, d.pop("vb"))

    gb = []
    for c in range(CB):
        r0 = c * L
        pre, pret, gcum, gcumt, _ = gates[c]
        for h in range(H_B):
            q = qkb_ref[0, r0:r0 + L, h * DQK_B:(h + 1) * DQK_B] * (DQK_B ** -0.5)
            k = qkb_ref[0, r0:r0 + L, H_B * DQK_B + h * DQK_B:H_B * DQK_B + (h + 1) * DQK_B]
            b_c = gcum[:, 12 + h:13 + h]
            b_r = gcumt[12 + h:13 + h, :]
            li_r = pret[8 + h:9 + h, :]
            dm = jnp.where(tri, b_c - b_r + li_r, -jnp.inf)
            gb.append(dict(q=q, k=k, qk=mm(q, k, NT), dm=dm, dmax=jnp.max(dm, axis=-1, keepdims=True),
                           b_c=b_c, li_c=pre[:, 8 + h:9 + h]))

    for c in range(CB):
        r0 = c * L
        for h in range(H_A):
            d = ga[c * H_A + h]
            s = sa_o[0, h]
            v_new = d["u"] - mm(d["w"], s)
            o = mm(d["qeg"], s) + mm(d["attn"], v_new)
            sa_o[0, h] = s * d["sdec"] + mm(d["kdec"], v_new, TN)
            zg = z_ref[0, r0:r0 + L, h * DV_A:(h + 1) * DV_A]
            o = _rms(o) * norm_a * _silu(zg)
            mixed_ref[0, r0:r0 + L, h * DV_A:(h + 1) * DV_A] = o.astype(mixed_ref.dtype)
        for h in range(H_B):
            d = gb[c * H_B + h]
            q, k, b_c = d["q"], d["k"], d["b_c"]
            v = vb_ref[0, r0:r0 + L, h * DV_B:(h + 1) * DV_B]
            m_prev = mb_o[0, h]
            a0 = b_c + m_prev
            m_t = jnp.maximum(a0, d["dmax"])
            w0 = jnp.exp(a0 - m_t)
            sm = d["qk"] * jnp.exp(d["dm"] - m_t)
            cst = cb_o[0, h]
            nst = nb_o[0, h]
            num = w0 * mm(q, cst) + mm(sm, v)
            den = w0 * jnp.sum(q * nst, axis=-1, keepdims=True) + jnp.sum(sm, axis=-1, keepdims=True)
            hh = num / jnp.maximum(jnp.abs(den), jnp.exp(-m_t))
            m_new = m_t[L - 1:L, :]
            wk = jnp.exp(b_c[L - 1:L, :] - b_c + d["li_c"] - m_new)
            w0l = w0[L - 1:L, :]
            kw = k * wk
            cb_o[0, h] = w0l * cst + mm(kw, v, TN)
            nb_o[0, h] = w0l * nst + jnp.sum(kw, axis=0, keepdims=True)
            mb_o[0, h] = m_new
            og = ob_ref[0, r0:r0 + L, h * DV_B:(h + 1) * DV_B]
            o = _rms(hh) * norm_b * _sigmoid(og)
            c0 = H_A * DV_A + h * DV_B
            mixed_ref[0, r0:r0 + L, c0:c0 + DV_B] = o.astype(mixed_ref.dtype)

    xbuf[5:8, :] = xbuf[5 + TB:8 + TB, :]


def _mixer_ab_call(proj, gates, gates_t, conv0, sa0, cb0, nb0, mb0, convw, gpar, gpart, norm_a, norm_b, *, L, CB, name):
    b, t, _ = proj.shape
    tb = L * CB
    nt = t // tb

    def col(width, idx):
        return pl.BlockSpec((1, tb, width), lambda bi, i: (bi, i, idx))

    def const(shape):
        return pl.BlockSpec(shape, lambda bi, i: (0,) * len(shape))

    def state(shape):
        return pl.BlockSpec((1,) + shape, lambda bi, i: (bi,) + (0,) * len(shape))

    in_specs = [col(QKV_A, 0), col(512, 3), col(512, 4), col(512, 5), col(512, 6),
                col(LANES, 0), pl.BlockSpec((1, N_GATE_COLS, tb), lambda bi, i: (bi, 0, i)),
                state((CONV_W - 1, QKV_A)), state((H_A, DK_A, DV_A)), state((H_B, DQK_B, DV_B)),
                state((H_B, 1, DQK_B)), state((H_B, 1, 1)),
                const((CONV_W, QKV_A)), const((8, LANES)), const((N_GATE_COLS, LANES)),
                const((1, DV_A)), const((1, DV_B))]
    out_specs = [pl.BlockSpec((1, tb, MIX_AB), lambda bi, i: (bi, i, 0)),
                 state((CONV_W - 1, QKV_A)), state((H_A, DK_A, DV_A)), state((H_B, DQK_B, DV_B)),
                 state((H_B, 1, DQK_B)), state((H_B, 1, 1))]
    out_shape = [jax.ShapeDtypeStruct((b, t, MIX_AB), BF16),
                 jax.ShapeDtypeStruct((b, CONV_W - 1, QKV_A), F32),
                 jax.ShapeDtypeStruct((b, H_A, DK_A, DV_A), F32),
                 jax.ShapeDtypeStruct((b, H_B, DQK_B, DV_B), F32),
                 jax.ShapeDtypeStruct((b, H_B, 1, DQK_B), F32),
                 jax.ShapeDtypeStruct((b, H_B, 1, 1), F32)]
    return pl.pallas_call(
        functools.partial(_mixer_ab_kernel, L=L, CB=CB),
        grid=(b, nt), in_specs=in_specs, out_specs=out_specs, out_shape=out_shape,
        scratch_shapes=[pltpu.VMEM((8 + tb, QKV_A), F32)],
        compiler_params=_cparams(("parallel", "arbitrary")),
        name=name,
    )(proj, proj, proj, proj, proj, gates, gates_t, conv0, sa0, cb0, nb0, mb0, convw, gpar, gpart, norm_a, norm_b)


def _mixer_c_kernel(q_ref, k_ref, v_ref, g_ref, cos_ref, sin_ref, s0_ref, mixed_ref, s_o, *, L, CB):
    @pl.when(pl.program_id(1) == 0)
    def _():
        s_o[...] = s0_ref[...]

    ii = lax.broadcasted_iota(I32, (L, L), 0)
    jj = lax.broadcasted_iota(I32, (L, L), 1)
    rel = (ii - jj).astype(F32)
    idx = lax.broadcasted_iota(I32, (L, 1), 0).astype(F32)
    half = DK_C // 2

    def rope(x, cos, sin):
        x1, x2 = x[:, :half], x[:, half:]
        return jnp.concatenate([x1 * cos - x2 * sin, x1 * sin + x2 * cos], axis=-1)

    for c in range(CB):
        r0 = c * L
        cos = cos_ref[r0:r0 + L, :]
        sin = sin_ref[r0:r0 + L, :]
        for h in range(H_C):
            lg = float(np.log1p(-np.exp2(-5.0 - h)))
            intra = jnp.where(rel >= 0, jnp.exp(lg * jnp.maximum(rel, 0.0)), 0.0)
            q_decay = jnp.exp(lg * (idx + 1.0))
            k_decay = jnp.exp(lg * (L - 1.0 - idx))
            s_decay = float(np.exp(lg * L))
            q = rope(q_ref[0, r0:r0 + L, h * DK_C:(h + 1) * DK_C].astype(F32), cos, sin)
            k = rope(k_ref[0, r0:r0 + L, h * DK_C:(h + 1) * DK_C].astype(F32), cos, sin) * (DK_C ** -0.5)
            v = v_ref[0, r0:r0 + L, h * DV_C:(h + 1) * DV_C]
            s = s_o[0, h]
            o = _mmb(_mmb(q, k, NT) * intra, v) + _mmb(q * q_decay, s)
            s_o[0, h] = s_decay * s + _mmb(k * k_decay, v, TN)
            gg = g_ref[0, r0:r0 + L, h * DV_C:(h + 1) * DV_C].astype(F32)
            o = _rms(o) * _silu(gg)
            mixed_ref[0, r0:r0 + L, h * DV_C:(h + 1) * DV_C] = o.astype(mixed_ref.dtype)


def _mixer_c_call(proj, cos, sin, s0, *, L, CB, name):
    b, t, _ = proj.shape
    tb = L * CB
    nt = t // tb
    qk_w, v_w = H_C * DK_C, H_C * DV_C
    sspec = pl.BlockSpec((1, H_C, DK_C, DV_C), lambda bi, i: (bi, 0, 0, 0))
    tspec = pl.BlockSpec((tb, DK_C // 2), lambda bi, i: (i, 0))
    return pl.pallas_call(
        functools.partial(_mixer_c_kernel, L=L, CB=CB),
        grid=(b, nt),
        in_specs=[pl.BlockSpec((1, tb, qk_w), lambda bi, i: (bi, i, 0)),
                  pl.BlockSpec((1, tb, qk_w), lambda bi, i: (bi, i, 1)),
                  pl.BlockSpec((1, tb, v_w), lambda bi, i: (bi, i, 1)),
                  pl.BlockSpec((1, tb, v_w), lambda bi, i: (bi, i, 2)),
                  tspec, tspec, sspec],
        out_specs=[pl.BlockSpec((1, tb, MIX_C), lambda bi, i: (bi, i, 0)), sspec],
        out_shape=[jax.ShapeDtypeStruct((b, t, MIX_C), BF16),
                   jax.ShapeDtypeStruct((b, H_C, DK_C, DV_C), F32)],
        compiler_params=_cparams(("parallel", "arbitrary")),
        name=name,
    )(proj, proj, proj, proj, cos, sin, s0)


def _out_route_kernel(mixed_ref, wout_ref, h_ref, g1_ref, gain_ref, sc_ref, sh_ref, wrt_ref, rb_ref, hx_in_ref,
                      hnew_ref, hx_ref, bkt_ref):
    del hx_in_ref
    bb, tt, m = mixed_ref.shape
    d = h_ref.shape[-1]
    rows = bb * tt
    y = _dot(mixed_ref[...].reshape(rows, m), wout_ref[...])
    h = h_ref[...] + g1_ref[...] * y.reshape(bb, tt, d)
    hnew_ref[...] = h
    hn = _rms(h) * gain_ref[...]
    hn = (hn * (1.0 + sc_ref[...]) + sh_ref[...]).reshape(rows, d)
    hx_ref[:, 0:d] = hn

    logits = _mm3(wrt_ref[...], hn, NT)
    score = _sigmoid(logits)
    sel = score + rb_ref[...]

    def row(a, r):
        return a[r:r + 1, :]

    gscore = []
    for g in range(N_GROUPS):
        a, b, c, e = (row(sel, EPG * g + i) for i in range(EPG))
        hi_ab, lo_ab, hi_ce, lo_ce = jnp.maximum(a, b), jnp.minimum(a, b), jnp.maximum(c, e), jnp.minimum(c, e)
        top1 = jnp.maximum(hi_ab, hi_ce)
        top2 = jnp.maximum(jnp.maximum(lo_ab, lo_ce), jnp.minimum(hi_ab, hi_ce))
        gscore.append(top1 + top2)
    best = jnp.zeros((1, rows), I32)
    bestv = gscore[0]
    for g in range(1, N_GROUPS):
        upd = gscore[g] > bestv
        best = jnp.where(upd, g, best)
        bestv = jnp.where(upd, gscore[g], bestv)

    def pick(a, i):
        out = row(a, i)
        for g in range(1, N_GROUPS):
            out = jnp.where(best == g, row(a, EPG * g + i), out)
        return out

    vsel = [pick(sel, i) for i in range(EPG)]
    vsc = [pick(score, i) for i in range(EPG)]
    i1 = jnp.zeros((1, rows), I32)
    v1 = vsel[0]
    for i in range(1, EPG):
        upd = vsel[i] > v1
        i1 = jnp.where(upd, i, i1)
        v1 = jnp.where(upd, vsel[i], v1)
    i2 = jnp.full((1, rows), -1, I32)
    v2 = jnp.full((1, rows), -jnp.inf, F32)
    for i in range(EPG):
        cand = jnp.where(i1 == i, -jnp.inf, vsel[i])
        upd = cand > v2
        i2 = jnp.where(upd, i, i2)
        v2 = jnp.where(upd, cand, v2)
    s1 = jnp.zeros((1, rows), F32)
    s2 = jnp.zeros((1, rows), F32)
    for i in range(EPG):
        s1 = jnp.where(i1 == i, vsc[i], s1)
        s2 = jnp.where(i2 == i, vsc[i], s2)
    den = s1 + s2
    w1 = s1 / den
    w2 = s2 / den
    first_lo = i1 < i2
    lo = jnp.minimum(i1, i2)
    hi = jnp.maximum(i1, i2)
    wlo = jnp.where(first_lo, w1, w2)
    whi = jnp.where(first_lo, w2, w1)
    pair = jnp.where(lo == 0, hi - 1, jnp.where(lo == 1, hi + 1, 5))
    bkt_ref[...] = (best * 6 + pair).reshape(1, 1, rows)

    aux = jnp.concatenate([wlo, whi, jnp.zeros((LANES - 2, rows), F32)], axis=0)
    hx_ref[:, d:d + LANES] = aux.T


def _out_route_call(mixed, wout, h, g1, gain, sc, sh, wrt, rb, hx_buf, row_block_off, *, bb, tt, name):
    b, t, m = mixed.shape
    d = h.shape[-1]
    rows = bb * tt
    nti = t // tt
    ntot = hx_buf.shape[0]
    xspec = pl.BlockSpec((bb, tt, d), lambda bi, i: (bi, i, 0))
    mspec = pl.BlockSpec((bb, 1, d), lambda bi, i: (bi, 0, 0))
    return pl.pallas_call(
        _out_route_kernel,
        grid=(b // bb, nti),
        in_specs=[pl.BlockSpec((bb, tt, m), lambda bi, i: (bi, i, 0)),
                  pl.BlockSpec((m, d), lambda bi, i: (0, 0)),
                  xspec, mspec,
                  pl.BlockSpec((1, 1, d), lambda bi, i: (0, 0, 0)), mspec, mspec,
                  pl.BlockSpec((N_EXPERTS, d), lambda bi, i: (0, 0)),
                  pl.BlockSpec((N_EXPERTS, 1), lambda bi, i: (0, 0)),
                  pl.BlockSpec(memory_space=pl.ANY)],
        out_specs=[xspec,
                   pl.BlockSpec((rows, HX_W), lambda bi, i: (row_block_off + bi * nti + i, 0)),
                   pl.BlockSpec((1, 1, rows), lambda bi, i: (bi * nti + i, 0, 0))],
        out_shape=[jax.ShapeDtypeStruct((b, t, d), F32),
                   jax.ShapeDtypeStruct((ntot, HX_W), F32),
                   jax.ShapeDtypeStruct(((b // bb) * nti, 1, rows), I32)],
        input_output_aliases={9: 1},
        compiler_params=_cparams(("parallel", "parallel")),
        name=name,
    )(mixed, wout, h, g1, gain.reshape(1, 1, d), sc, sh, wrt, rb, hx_buf)


def _moe_kernel(ea_ref, eb_ref, chg_ref, src_ref, hx_hbm, wga_ref, wgb_ref, wua_ref, wub_ref, wda_ref, wdb_ref,
                ys_ref, xbuf, wg_s, wu_s, wd_s, sem):
    t = pl.program_id(0)
    nt = pl.num_programs(0)
    slot = lax.rem(t, 2)

    def row_copy(tile, r, s):
        row = src_ref[tile * MOE_TM + r]
        return pltpu.make_async_copy(hx_hbm.at[pl.ds(row, 1), :], xbuf.at[s, pl.ds(r, 1), :], sem.at[s])

    def start_rows(tile, s):
        for r in range(MOE_TM):
            row_copy(tile, r, s).start(priority=r % 2)

    def wait_rows(tile, s):
        for r in range(MOE_TM):
            row_copy(tile, r, s).wait()

    @pl.when(t == 0)
    def _():
        start_rows(0, 0)

    wait_rows(t, slot)
    nxt = jnp.minimum(t + 1, nt - 1)
    start_rows(nxt, 1 - slot)

    @pl.when(chg_ref[t] > 0)
    def _():
        rc = 128
        for i in range(D_MODEL // rc):
            wg_s[0, i * rc:(i + 1) * rc, :] = wga_ref[0, 0, i * rc:(i + 1) * rc, :].astype(BF16)
            wg_s[1, i * rc:(i + 1) * rc, :] = wgb_ref[0, 0, i * rc:(i + 1) * rc, :].astype(BF16)
            wu_s[0, i * rc:(i + 1) * rc, :] = wua_ref[0, 0, i * rc:(i + 1) * rc, :].astype(BF16)
            wu_s[1, i * rc:(i + 1) * rc, :] = wub_ref[0, 0, i * rc:(i + 1) * rc, :].astype(BF16)
        for i in range(D_FF // rc):
            wd_s[0, i * rc:(i + 1) * rc, :] = wda_ref[0, 0, i * rc:(i + 1) * rc, :].astype(BF16)
            wd_s[1, i * rc:(i + 1) * rc, :] = wdb_ref[0, 0, i * rc:(i + 1) * rc, :].astype(BF16)

    x = xbuf[slot, :, 0:D_MODEL].astype(BF16)
    wlo = xbuf[slot, :, D_MODEL:D_MODEL + 1]
    whi = xbuf[slot, :, D_MODEL + 1:D_MODEL + 2]
    act_a = (_silu(_dot(x, wg_s[0])) * _dot(x, wu_s[0]) * wlo).astype(BF16)
    act_b = (_silu(_dot(x, wg_s[1])) * _dot(x, wu_s[1]) * whi).astype(BF16)
    ys_ref[...] = _dot(act_a, wd_s[0]) + _dot(act_b, wd_s[1])

    @pl.when(t == nt - 1)
    def _():
        wait_rows(nxt, 1 - slot)


def _moe_call(ea, eb, chg, src, hx, wg, wu, wd, layer):
    p = src.shape[0]
    n_tiles = p // MOE_TM
    wa_in = pl.BlockSpec((1, 1, D_MODEL, D_FF), lambda t, ea, eb, ch, sr: (layer, ea[t], 0, 0))
    wb_in = pl.BlockSpec((1, 1, D_MODEL, D_FF), lambda t, ea, eb, ch, sr: (layer, eb[t], 0, 0))
    wa_out = pl.BlockSpec((1, 1, D_FF, D_MODEL), lambda t, ea, eb, ch, sr: (layer, ea[t], 0, 0))
    wb_out = pl.BlockSpec((1, 1, D_FF, D_MODEL), lambda t, ea, eb, ch, sr: (layer, eb[t], 0, 0))
    return pl.pallas_call(
        _moe_kernel,
        grid_spec=pltpu.PrefetchScalarGridSpec(
            num_scalar_prefetch=4, grid=(n_tiles,),
            in_specs=[pl.BlockSpec(memory_space=pl.ANY), wa_in, wb_in, wa_in, wb_in, wa_out, wb_out],
            out_specs=pl.BlockSpec((MOE_TM, D_MODEL), lambda t, ea, eb, ch, sr: (t, 0)),
            scratch_shapes=[pltpu.VMEM((2, MOE_TM, HX_W), F32),
                            pltpu.VMEM((2, D_MODEL, D_FF), BF16), pltpu.VMEM((2, D_MODEL, D_FF), BF16),
                            pltpu.VMEM((2, D_FF, D_MODEL), BF16),
                            pltpu.SemaphoreType.DMA((2,))]),
        out_shape=jax.ShapeDtypeStruct((p, D_MODEL), F32),
        compiler_params=_cparams(("arbitrary",)),
        name="moe",
    )(ea, eb, chg, src, hx, wg, wg, wu, wu, wd, wd)


_PAIRS = [(0, 1), (0, 2), (0, 3), (1, 2), (1, 3), (2, 3)]
_EA_TABLE = np.array([g * EPG + p[0] for g in range(N_GROUPS) for p in _PAIRS], np.int32)
_EB_TABLE = np.array([g * EPG + p[1] for g in range(N_GROUPS) for p in _PAIRS], np.int32)


def _moe(hx, bkt, wg, wu, wd, layer):
    n = hx.shape[0]
    tm = MOE_TM
    ar = jnp.arange(N_BUCKETS, dtype=I32)
    oh = (bkt[:, None] == ar[None, :]).astype(I32)
    blk = LANES
    nb = -(-n // blk)
    ohb = jnp.pad(oh, ((0, nb * blk - n), (0, 0))).reshape(nb, blk, N_BUCKETS)
    tril = jnp.tril(jnp.ones((blk, blk), F32))
    inner = jnp.einsum('ij,bjk->bik', tril, ohb.astype(F32))
    tot = inner[:, -1, :]
    outer = jnp.cumsum(tot, axis=0) - tot
    cs = (inner + outer[:, None, :]).reshape(nb * blk, N_BUCKETS)[:n].astype(I32)
    rank = jnp.sum(oh * cs, axis=1) - 1
    counts = cs[-1]
    padded = ((counts + tm - 1) // tm) * tm
    ends = jnp.cumsum(padded)
    offs = ends - padded
    pos = jnp.sum(oh * offs[None, :], axis=1) + rank
    n_tiles = -(-n // tm) + N_BUCKETS
    p = n_tiles * tm
    src = jnp.zeros((p,), I32).at[pos].set(jnp.arange(n, dtype=I32))
    tstart = jnp.arange(n_tiles, dtype=I32) * tm
    tb = jnp.sum((ends[None, :] <= tstart[:, None]).astype(I32), axis=1)
    n_used = ends[-1] // tm
    valid = (jnp.arange(n_tiles, dtype=I32) < n_used).astype(I32)
    last_b = tb[jnp.maximum(n_used - 1, 0)]
    tb = jnp.clip(jnp.where(valid > 0, tb, last_b), 0, N_BUCKETS - 1)
    ea = jnp.asarray(_EA_TABLE)[tb]
    eb = jnp.asarray(_EB_TABLE)[tb]
    chg = jnp.concatenate([jnp.ones((1,), I32), (tb[1:] != tb[:-1]).astype(I32)])
    ys = _moe_call(ea, eb, chg, src, hx, wg, wu, wd, layer)
    return jnp.take(ys, pos, axis=0)


def _final_kernel(h_ref, y_ref, g_ref, gain_ref, o_ref):
    bb, tt, d = h_ref.shape
    h = h_ref[...] + g_ref[...] * y_ref[...].reshape(bb, tt, d)
    o_ref[...] = _rms(h) * gain_ref[...]


def _final_call(h, y2d, off, g2, gain, *, bb, tt, name):
    b, t, d = h.shape
    nti = t // tt
    rows = bb * tt
    xspec = pl.BlockSpec((bb, tt, d), lambda bi, i: (bi, i, 0))
    return pl.pallas_call(
        _final_kernel,
        grid=(b // bb, nti),
        in_specs=[xspec, pl.BlockSpec((rows, d), lambda bi, i: (off + bi * nti + i, 0)),
                  pl.BlockSpec((bb, 1, d), lambda bi, i: (bi, 0, 0)),
                  pl.BlockSpec((1, 1, d), lambda bi, i: (0, 0, 0))],
        out_specs=xspec,
        out_shape=jax.ShapeDtypeStruct((b, t, d), F32),
        compiler_params=_cparams(("parallel", "parallel")),
        name=name,
    )(h, y2d, g2, gain.reshape(1, 1, d))


def _group_cfg(b, t):
    if t >= 512:
        return dict(bb=1, tt=512, L=CHUNK, CB=4)
    return dict(bb=b, tt=t, L=min(CHUNK, t), CB=1)


def _rope_tables(pos0, t):
    half = DK_C // 2
    inv = jnp.power(ROPE_BASE, -jnp.linspace(0.0, 1.0, half, dtype=F32))
    ang = (pos0 + jnp.arange(t, dtype=F32))[:, None] * inv[None, :]
    return jnp.cos(ang), jnp.sin(ang)


def kernel(x_prompt, x_sample, c_prompt, c_sample, state_a_conv, state_a_rec, state_b_c, state_b_n, state_b_m,
           state_c_rec, w_mod, b_mod, norm_mix, norm_ffn, w_in_ab, conv_a, a_log, dt_bias, norm_a, gate_bias_b,
           norm_b, w_out_ab, w_in_c, w_out_c, w_router, router_bias, w_gate, w_up, w_down, norm_final):
    d = D_MODEL
    bp, tp, _ = x_prompt.shape
    bs, ts, _ = x_sample.shape
    n_ab, n_c = w_in_ab.shape[0], w_in_c.shape[0]

    mod_all = _mod_call(jnp.concatenate([c_prompt, c_sample], axis=0), w_mod, b_mod)

    o = np.cumsum((0,) + AB_SIZES)
    w_ab_main = jnp.concatenate([w_in_ab[:, :, o[0]:o[1]].astype(BF16), w_in_ab[:, :, o[3]:o[7]].astype(BF16),
                                 w_in_ab[:, :, o[8]:o[9]].astype(BF16)], axis=-1)
    w_ab_gate = jnp.concatenate([w_in_ab[:, :, o[1]:o[3]], w_in_ab[:, :, o[7]:o[8]]], axis=-1)
    w_ab_gate_pad = jnp.pad(w_ab_gate, ((0, 0), (0, 0), (0, LANES - N_GATE_COLS)))
    w_ab_gate_t = jnp.swapaxes(w_ab_gate, 1, 2)
    w_c = w_in_c.astype(BF16)
    w_out_ab_b = w_out_ab.astype(BF16)
    w_out_c_b = w_out_c.astype(BF16)
    wrt = w_router.T
    rb = router_bias.reshape(N_EXPERTS, 1)
    zeros4 = jnp.zeros((n_ab, H_A), F32)
    gpar_row0 = jnp.concatenate([a_log, jnp.zeros((n_ab, LANES - H_A), F32)], axis=1)
    gpar_row1 = jnp.concatenate([dt_bias, zeros4, gate_bias_b, jnp.zeros((n_ab, LANES - 16), F32)], axis=1)
    gpar = jnp.concatenate([gpar_row0[:, None], gpar_row1[:, None], jnp.zeros((n_ab, 6, LANES), F32)], axis=1)
    gpart = jnp.swapaxes(jnp.concatenate([gpar[:, :2, :N_GATE_COLS], jnp.zeros((n_ab, LANES - 2, N_GATE_COLS), F32)],
                                         axis=1), 1, 2)

    groups = []
    zf = lambda *s: jnp.zeros(s, F32)
    groups.append(dict(
        x=x_prompt, b=bp, t=tp, pos0=0.0, mod=mod_all[:, :bp], row0=0,
        conv=zf(n_ab, bp, CONV_W - 1, QKV_A), sa=zf(n_ab, bp, H_A, DK_A, DV_A), cb=zf(n_ab, bp, H_B, DQK_B, DV_B),
        nb=zf(n_ab, bp, H_B, DQK_B), mb=zf(n_ab, bp, H_B), sc=zf(n_c, bp, H_C, DK_C, DV_C)))
    groups.append(dict(
        x=x_sample, b=bs, t=ts, pos0=float(PAST_LEN), mod=mod_all[:, bp:], row0=bp * tp,
        conv=state_a_conv, sa=state_a_rec, cb=state_b_c, nb=state_b_n, mb=state_b_m, sc=state_c_rec))
    n_tot = bp * tp + bs * ts
    for g in groups:
        g.update(_group_cfg(g["b"], g["t"]))
        g["h"] = g["x"]
        g["prev"] = None
        g["new_ab"], g["new_c"] = [], []
        g["cos"], g["sin"] = _rope_tables(g["pos0"], g["t"])
        g["roff"] = g["row0"] // (g["bb"] * g["tt"])

    for layer in range(DEPTH):
        li = layer // 2
        if layer == 0:
            hx_buf = jnp.zeros((n_tot, HX_W), F32)
        bkts = []
        for gi, g in enumerate(groups):
            b, t, bb, tt = g["b"], g["t"], g["bb"], g["tt"]
            mods = [m.reshape(b, 1, d) for m in jnp.split(g["mod"][layer], N_MOD, axis=-1)]
            sh1, sc1, g1, sh2, sc2, g2 = mods
            tag = f"l{layer}g{gi}"
            if layer % 2 == 0:
                outs = _ln_mm_call(g["h"], g["prev"], norm_mix[layer], sc1, sh1, w_ab_main[li],
                                   (w_ab_gate_pad[li], w_ab_gate_t[li]), bb=bb, tt=tt, tn=512, out_dtype=F32,
                                   name="ln_ab_" + tag)
                if g["prev"] is not None:
                    g["h"], outs = outs[0], outs[1:]
                proj, gates, gates_t = outs
                gates_t = gates_t.reshape(N_GATE_COLS, b, t).transpose(1, 0, 2)
                mixed, conv_n, sa_n, cb_n, nb_n, mb_n = _mixer_ab_call(
                    proj, gates, gates_t, g["conv"][li], g["sa"][li], g["cb"][li],
                    g["nb"][li].reshape(b, H_B, 1, DQK_B), g["mb"][li].reshape(b, H_B, 1, 1),
                    conv_a[li], gpar[li], gpart[li], norm_a[li].reshape(1, DV_A), norm_b[li].reshape(1, DV_B),
                    L=g["L"], CB=g["CB"], name="mixer_ab_" + tag)
                g["new_ab"].append((conv_n, sa_n, cb_n, nb_n.reshape(b, H_B, DQK_B), mb_n.reshape(b, H_B)))
                wout = w_out_ab_b[li]
            else:
                outs = _ln_mm_call(g["h"], g["prev"], norm_mix[layer], sc1, sh1, w_c[li], None,
                                   bb=bb, tt=tt, tn=512, out_dtype=BF16, name="ln_c_" + tag)
                if g["prev"] is not None:
                    g["h"], outs = outs[0], outs[1:]
                (proj,) = outs
                mixed, sc_n = _mixer_c_call(proj, g["cos"], g["sin"], g["sc"][li], L=g["L"], CB=g["CB"],
                                            name="mixer_c_" + tag)
                g["new_c"].append(sc_n)
                wout = w_out_c_b[li]
            g["h"], hx_buf, bkt = _out_route_call(mixed, wout, g["h"], g1, norm_ffn[layer], sc2, sh2, wrt, rb,
                                                  hx_buf, g["roff"], bb=bb, tt=tt, name="out_route_" + tag)
            bkts.append(bkt.reshape(-1))
            g["g2"] = g2
        y_moe = _moe(hx_buf, jnp.concatenate(bkts), w_gate, w_up, w_down, layer)
        for g in groups:
            g["prev"] = (y_moe, g["roff"], g["g2"])

    outs = []
    for gi, g in enumerate(groups):
        y2d, off, g2 = g["prev"]
        y = _final_call(g["h"], y2d, off, g2, norm_final, bb=g["bb"], tt=g["tt"], name=f"final_g{gi}")
        ab = [jnp.stack(s) for s in zip(*g["new_ab"])]
        outs.append((y, ab[0], ab[1], ab[2], ab[3], ab[4], jnp.stack(g["new_c"])))
    p, s = outs
    return (p[0], s[0], p[1], p[2], p[3], p[4], p[5], p[6], s[1], s[2], s[3], s[4], s[5], s[6])
```

```python
import functools

import numpy as np
import jax
import jax.numpy as jnp
from jax import lax
from jax.experimental import pallas as pl
from jax.experimental.pallas import tpu as pltpu

F32 = jnp.float32
BF16 = jnp.bfloat16
I32 = jnp.int32

D_MODEL = 1024
DEPTH = 2
CHUNK = 64
H_A, DK_A, DV_A, CONV_W = 4, 128, 128, 4
QKV_A = H_A * (2 * DK_A + DV_A)
H_B, DQK_B, DV_B = 4, 64, 128
H_C, DK_C, DV_C = 4, 256, 512
ROPE_BASE = 10000.0
PAST_LEN = 4096
AB_SIZES = (QKV_A, H_A, H_A, H_A * DV_A, H_B * DQK_B, H_B * DQK_B, H_B * DV_B, 2 * H_B, H_B * DV_B)
MIX_AB = H_A * DV_A + H_B * DV_B
MIX_C = H_C * DV_C
IN_C = 2 * H_C * DK_C + 2 * H_C * DV_C
N_EXPERTS, N_GROUPS, EPG, D_FF = 16, 4, 4, 512
N_MOD = 6
EPS = 1e-6

LANES = 128
AB_MAIN = 3584
N_GATE_COLS = 16
HX_W = D_MODEL + LANES
N_BUCKETS = N_GROUPS * 6
MOE_TM = 256
VMEM_LIMIT = 48 * 1024 * 1024

NN = ((1,), (0,))
NT = ((1,), (1,))
TN = ((0,), (0,))


def _dot(a, b, dims=NN):
    return lax.dot_general(a, b, (dims, ((), ())), preferred_element_type=F32)


def _mmb(a, b, dims=NN):
    return _dot(a.astype(BF16), b.astype(BF16), dims)


def _split2(x):
    hi = x.astype(BF16)
    lo = (x - hi.astype(F32)).astype(BF16)
    return hi, lo


def _split3(x):
    hi = x.astype(BF16)
    r = x - hi.astype(F32)
    mid = r.astype(BF16)
    lo = (r - mid.astype(F32)).astype(BF16)
    return hi, mid, lo


def _mm3(a, b, dims=NN):
    ah, al = _split2(a)
    bh, bl = _split2(b)
    return _dot(ah, bh, dims) + (_dot(ah, bl, dims) + _dot(al, bh, dims))


def _mm_mask_l(mask_bf16, x, dims=NN):
    h, m, l = _split3(x)
    return _dot(mask_bf16, h, dims) + (_dot(mask_bf16, m, dims) + _dot(mask_bf16, l, dims))


def _mm_mask_r(x, mask_bf16):
    h, m, l = _split3(x)
    return _dot(h, mask_bf16) + (_dot(m, mask_bf16) + _dot(l, mask_bf16))


def _sigmoid(x):
    return 1.0 / (1.0 + jnp.exp(-x))


def _silu(x):
    return x * _sigmoid(x)


def _softplus(x):
    return jnp.maximum(x, 0.0) + jnp.log(1.0 + jnp.exp(-jnp.abs(x)))


def _rms(x, eps=EPS):
    return x * lax.rsqrt(jnp.mean(x * x, axis=-1, keepdims=True) + eps)


def _cparams(sem):
    return pltpu.CompilerParams(dimension_semantics=sem, vmem_limit_bytes=VMEM_LIMIT)


def _mod_kernel(c_ref, w_ref, b_ref, o_ref):
    c = c_ref[...]
    o_ref[0] = _mm3(_silu(c), w_ref[0]) + b_ref[0]


def _mod_call(c_all, w_mod, b_mod):
    bt = c_all.shape[0]
    e = w_mod.shape[-1]
    tn = 1024
    return pl.pallas_call(
        _mod_kernel,
        grid=(DEPTH, e // tn),
        in_specs=[pl.BlockSpec((bt, D_MODEL), lambda l, j: (0, 0)),
                  pl.BlockSpec((1, D_MODEL, tn), lambda l, j: (l, 0, j)),
                  pl.BlockSpec((1, 1, tn), lambda l, j: (l, 0, j))],
        out_specs=pl.BlockSpec((1, bt, tn), lambda l, j: (l, 0, j)),
        out_shape=jax.ShapeDtypeStruct((DEPTH, bt, e), F32),
        compiler_params=_cparams(("parallel", "parallel")),
        name="mod",
    )(c_all, w_mod, b_mod.reshape(DEPTH, 1, e))


def _ln_mm_kernel(*refs, has_prev, has_gates, tn):
    it = iter(refs)
    x_ref = next(it)
    if has_prev:
        yp_ref, gp_ref = next(it), next(it)
    gain_ref, sc_ref, sh_ref, w_ref = next(it), next(it), next(it), next(it)
    if has_gates:
        wg_ref, wgt_ref = next(it), next(it)
    if has_prev:
        h_ref = next(it)
    proj_ref = next(it)
    if has_gates:
        g_ref, gt_ref = next(it), next(it)
    hn_s = next(it)

    bb, tt, d = x_ref.shape
    rows = bb * tt
    x = x_ref[...]
    if has_prev:
        x = x + gp_ref[...] * yp_ref[...].reshape(bb, tt, d)
        h_ref[...] = x
    hn = _rms(x) * gain_ref[...]
    hn = hn * (1.0 + sc_ref[...]) + sh_ref[...]
    hn2 = hn.reshape(rows, d)
    hn_s[...] = hn2.astype(BF16)
    if has_gates:
        g_ref[...] = _mm3(hn2, wg_ref[...]).reshape(bb, tt, LANES)
        gt_ref[...] = _mm3(wgt_ref[...], hn2, NT)
    for j in range(proj_ref.shape[-1] // tn):
        cols = slice(j * tn, (j + 1) * tn)
        proj_ref[:, :, cols] = _dot(hn_s[...], w_ref[:, cols]).reshape(bb, tt, tn).astype(proj_ref.dtype)


def _ln_mm_call(x, prev, gain, sc, sh, w, gates_w, *, bb, tt, tn, out_dtype, name):
    b, t, d = x.shape
    e = w.shape[1]
    rows = bb * tt
    nti = t // tt
    grid = (b // bb, nti)
    has_prev = prev is not None
    has_gates = gates_w is not None
    xspec = pl.BlockSpec((bb, tt, d), lambda bi, i: (bi, i, 0))
    mspec = pl.BlockSpec((bb, 1, d), lambda bi, i: (bi, 0, 0))
    in_specs, args = [xspec], [x]
    if has_prev:
        y2d, off, gp = prev
        in_specs += [pl.BlockSpec((rows, d), lambda bi, i: (off + bi * nti + i, 0)), mspec]
        args += [y2d, gp]
    in_specs += [pl.BlockSpec((1, 1, d), lambda bi, i: (0, 0, 0)), mspec, mspec,
                 pl.BlockSpec((d, e), lambda bi, i: (0, 0), pipeline_mode=pl.Buffered(1))]
    args += [gain.reshape(1, 1, d), sc, sh, w]
    if has_gates:
        wg, wgt = gates_w
        in_specs += [pl.BlockSpec((d, LANES), lambda bi, i: (0, 0)),
                     pl.BlockSpec((N_GATE_COLS, d), lambda bi, i: (0, 0))]
        args += [wg, wgt]
    out_specs, out_shape = [], []
    if has_prev:
        out_specs.append(xspec)
        out_shape.append(jax.ShapeDtypeStruct((b, t, d), F32))
    out_specs.append(pl.BlockSpec((bb, tt, e), lambda bi, i: (bi, i, 0)))
    out_shape.append(jax.ShapeDtypeStruct((b, t, e), out_dtype))
    if has_gates:
        out_specs += [pl.BlockSpec((bb, tt, LANES), lambda bi, i: (bi, i, 0)),
                      pl.BlockSpec((N_GATE_COLS, rows), lambda bi, i: (0, bi * nti + i))]
        out_shape += [jax.ShapeDtypeStruct((b, t, LANES), F32),
                      jax.ShapeDtypeStruct((N_GATE_COLS, b * t), F32)]
    return pl.pallas_call(
        functools.partial(_ln_mm_kernel, has_prev=has_prev, has_gates=has_gates, tn=tn),
        grid=grid, in_specs=in_specs, out_specs=out_specs, out_shape=out_shape,
        scratch_shapes=[pltpu.VMEM((rows, d), BF16)],
        compiler_params=_cparams(("parallel", "parallel")),
        name=name,
    )(*args)


def _tri_inv_all(a_list, eye, length, mm):
    xs = [-a for a in a_list]
    ps = [eye + x for x in xs]
    n = 2
    while n < length:
        xs = [mm(x, x) for x in xs]
        ps = [p + mm(p, x) for p, x in zip(ps, xs)]
        n *= 2
    return ps


def _mixer_ab_kernel(qkv_ref, z_ref, qkb_ref, vb_ref, ob_ref, g_ref, gt_ref,
                     conv0_ref, sa0_ref, cb0_ref, nb0_ref, mb0_ref,
                     convw_ref, gpar_ref, gpart_ref, na_ref, nbn_ref,
                     mixed_ref, conv_o, sa_o, cb_o, nb_o, mb_o,
                     xbuf, *, L, CB):
    TB = L * CB

    @pl.when(pl.program_id(1) == 0)
    def _():
        xbuf[5:8, :] = conv0_ref[0]
        sa_o[...] = sa0_ref[...]
        cb_o[...] = cb0_ref[...]
        nb_o[...] = nb0_ref[...]
        mb_o[...] = mb0_ref[...]

    xbuf[8:8 + TB, :] = qkv_ref[0]
    conv_o[0] = xbuf[5 + TB:8 + TB, :]

    ii = lax.broadcasted_iota(I32, (L, L), 0)
    jj = lax.broadcasted_iota(I32, (L, L), 1)
    tri = ii >= jj
    strict = ii > jj
    eye = (ii == jj).astype(F32)
    tril_b = tri.astype(BF16)
    triu_b = (ii <= jj).astype(BF16)
    lane = lax.broadcasted_iota(I32, (L, LANES), 1)
    rowi = lax.broadcasted_iota(I32, (N_GATE_COLS, L), 0)
    gpar = gpar_ref[...]
    gpart = gpart_ref[...]
    nea = -jnp.exp(gpar[0:1, :])
    neat = -jnp.exp(gpart[:, 0:1])
    norm_a = na_ref[...]
    norm_b = nbn_ref[...]

    mm = _mmb
    mm_inv = _mm3

    def conv_tile(r0, col0):
        acc = None
        for j in range(CONV_W):
            term = convw_ref[j:j + 1, col0:col0 + DK_A] * xbuf[5 + j + r0:5 + j + r0 + L, col0:col0 + DK_A]
            acc = term if acc is None else acc + term
        return _silu(acc)

    gates = []
    for c in range(CB):
        r0 = c * L
        pre = g_ref[0, r0:r0 + L, :] + gpar[1:2, :]
        csrc = jnp.where(lane < 4, nea * _softplus(pre),
                         jnp.where(lane >= 12, -_softplus(-pre), 0.0))
        csrc = jnp.where(lane < N_GATE_COLS, csrc, 0.0)
        gcum = _mm_mask_l(tril_b, csrc)
        pret = gt_ref[0, :, r0:r0 + L] + gpart[:, 1:2]
        csrct = jnp.where(rowi < 4, neat * _softplus(pret),
                          jnp.where(rowi >= 12, -_softplus(-pret), 0.0))
        gcumt = _mm_mask_r(csrct, triu_b)
        gates.append((pre, pret, gcum, gcumt, _sigmoid(pre)))

    ga = []
    a_list = []
    for c in range(CB):
        r0 = c * L
        pre, pret, gcum, gcumt, beta_all = gates[c]
        for h in range(H_A):
            q = conv_tile(r0, h * DK_A)
            k = conv_tile(r0, H_A * DK_A + h * DK_A)
            v = conv_tile(r0, 2 * H_A * DK_A + h * DV_A)
            q = q * lax.rsqrt(jnp.sum(q * q, axis=-1, keepdims=True) + EPS) * (DK_A ** -0.5)
            k = k * lax.rsqrt(jnp.sum(k * k, axis=-1, keepdims=True) + EPS)
            gc_c = gcum[:, h:h + 1]
            gc_r = gcumt[h:h + 1, :]
            decay = jnp.where(tri, jnp.exp(jnp.where(tri, gc_c - gc_r, 0.0)), 0.0)
            beta_c = beta_all[:, 4 + h:5 + h]
            kb = k * beta_c
            a_list.append(jnp.where(strict, mm_inv(kb, k, NT) * decay, 0.0))
            eg = jnp.exp(gc_c)
            gl = gc_c[L - 1:L, :]
            attn = jnp.where(tri, mm(q, k, NT) * decay, 0.0)
            ga.append(dict(kbeg=kb * eg, vb=v * beta_c, qeg=q * eg, attn=attn,
                           kdec=k * jnp.exp(gl - gc_c), sdec=jnp.exp(gl)))
    tinv = _tri_inv_all(a_list, eye, L, mm_inv)
    for d, t in zip(ga, tinv):
        d["w"] = mm_inv(t, d.pop("kbeg"))
        d["u"] = mm_inv(t, d.pop("vb"))

    gb = []
    for c in range(CB):
        r0 = c * L
        pre, pret, gcum, gcumt, _ = gates[c]
        for h in range(H_B):
            q = qkb_ref[0, r0:r0 + L, h * DQK_B:(h + 1) * DQK_B] * (DQK_B ** -0.5)
            k = qkb_ref[0, r0:r0 + L, H_B * DQK_B + h * DQK_B:H_B * DQK_B + (h + 1) * DQK_B]
            b_c = gcum[:, 12 + h:13 + h]
            b_r = gcumt[12 + h:13 + h, :]
            li_r = pret[8 + h:9 + h, :]
            dm = jnp.where(tri, b_c - b_r + li_r, -jnp.inf)
            gb.append(dict(q=q, k=k, qk=mm(q, k, NT), dm=dm, dmax=jnp.max(dm, axis=-1, keepdims=True),
                           b_c=b_c, li_c=pre[:, 8 + h:9 + h]))

    for c in range(CB):
        r0 = c * L
        for h in range(H_A):
            d = ga[c * H_A + h]
            s = sa_o[0, h]
            v_new = d["u"] - mm(d["w"], s)
            o = mm(d["qeg"], s) + mm(d["attn"], v_new)
            sa_o[0, h] = s * d["sdec"] + mm(d["kdec"], v_new, TN)
            zg = z_ref[0, r0:r0 + L, h * DV_A:(h + 1) * DV_A]
            o = _rms(o) * norm_a * _silu(zg)
            mixed_ref[0, r0:r0 + L, h * DV_A:(h + 1) * DV_A] = o.astype(mixed_ref.dtype)
        for h in range(H_B):
            d = gb[c * H_B + h]
            q, k, b_c = d["q"], d["k"], d["b_c"]
            v = vb_ref[0, r0:r0 + L, h * DV_B:(h + 1) * DV_B]
            m_prev = mb_o[0, h]
            a0 = b_c + m_prev
            m_t = jnp.maximum(a0, d["dmax"])
            w0 = jnp.exp(a0 - m_t)
            sm = d["qk"] * jnp.exp(d["dm"] - m_t)
            cst = cb_o[0, h]
            nst = nb_o[0, h]
            num = w0 * mm(q, cst) + mm(sm, v)
            den = w0 * jnp.sum(q * nst, axis=-1, keepdims=True) + jnp.sum(sm, axis=-1, keepdims=True)
            hh = num / jnp.maximum(jnp.abs(den), jnp.exp(-m_t))
            m_new = m_t[L - 1:L, :]
            wk = jnp.exp(b_c[L - 1:L, :] - b_c + d["li_c"] - m_new)
            w0l = w0[L - 1:L, :]
            kw = k * wk
            cb_o[0, h] = w0l * cst + mm(kw, v, TN)
            nb_o[0, h] = w0l * nst + jnp.sum(kw, axis=0, keepdims=True)
            mb_o[0, h] = m_new
            og = ob_ref[0, r0:r0 + L, h * DV_B:(h + 1) * DV_B]
            o = _rms(hh) * norm_b * _sigmoid(og)
            c0 = H_A * DV_A + h * DV_B
            mixed_ref[0, r0:r0 + L, c0:c0 + DV_B] = o.astype(mixed_ref.dtype)

    xbuf[5:8, :] = xbuf[5 + TB:8 + TB, :]


def _mixer_ab_call(proj, gates, gates_t, conv0, sa0, cb0, nb0, mb0, convw, gpar, gpart, norm_a, norm_b, *, L, CB, name):
    b, t, _ = proj.shape
    tb = L * CB
    nt = t // tb

    def col(width, idx):
        return pl.BlockSpec((1, tb, width), lambda bi, i: (bi, i, idx))

    def const(shape):
        return pl.BlockSpec(shape, lambda bi, i: (0,) * len(shape))

    def state(shape):
        return pl.BlockSpec((1,) + shape, lambda bi, i: (bi,) + (0,) * len(shape))

    in_specs = [col(QKV_A, 0), col(512, 3), col(512, 4), col(512, 5), col(512, 6),
                col(LANES, 0), pl.BlockSpec((1, N_GATE_COLS, tb), lambda bi, i: (bi, 0, i)),
                state((CONV_W - 1, QKV_A)), state((H_A, DK_A, DV_A)), state((H_B, DQK_B, DV_B)),
                state((H_B, 1, DQK_B)), state((H_B, 1, 1)),
                const((CONV_W, QKV_A)), const((8, LANES)), const((N_GATE_COLS, LANES)),
                const((1, DV_A)), const((1, DV_B))]
    out_specs = [pl.BlockSpec((1, tb, MIX_AB), lambda bi, i: (bi, i, 0)),
                 state((CONV_W - 1, QKV_A)), state((H_A, DK_A, DV_A)), state((H_B, DQK_B, DV_B)),
                 state((H_B, 1, DQK_B)), state((H_B, 1, 1))]
    out_shape = [jax.ShapeDtypeStruct((b, t, MIX_AB), BF16),
                 jax.ShapeDtypeStruct((b, CONV_W - 1, QKV_A), F32),
                 jax.ShapeDtypeStruct((b, H_A, DK_A, DV_A), F32),
                 jax.ShapeDtypeStruct((b, H_B, DQK_B, DV_B), F32),
                 jax.ShapeDtypeStruct((b, H_B, 1, DQK_B), F32),
                 jax.ShapeDtypeStruct((b, H_B, 1, 1), F32)]
    return pl.pallas_call(
        functools.partial(_mixer_ab_kernel, L=L, CB=CB),
        grid=(b, nt), in_specs=in_specs, out_specs=out_specs, out_shape=out_shape,
        scratch_shapes=[pltpu.VMEM((8 + tb, QKV_A), F32)],
        compiler_params=_cparams(("parallel", "arbitrary")),
        name=name,
    )(proj, proj, proj, proj, proj, gates, gates_t, conv0, sa0, cb0, nb0, mb0, convw, gpar, gpart, norm_a, norm_b)


def _mixer_c_kernel(q_ref, k_ref, v_ref, g_ref, cos_ref, sin_ref, s0_ref, mixed_ref, s_o, *, L, CB):
    @pl.when(pl.program_id(1) == 0)
    def _():
        s_o[...] = s0_ref[...]

    ii = lax.broadcasted_iota(I32, (L, L), 0)
    jj = lax.broadcasted_iota(I32, (L, L), 1)
    rel = (ii - jj).astype(F32)
    idx = lax.broadcasted_iota(I32, (L, 1), 0).astype(F32)
    half = DK_C // 2

    def rope(x, cos, sin):
        x1, x2 = x[:, :half], x[:, half:]
        return jnp.concatenate([x1 * cos - x2 * sin, x1 * sin + x2 * cos], axis=-1)

    for c in range(CB):
        r0 = c * L
        cos = cos_ref[r0:r0 + L, :]
        sin = sin_ref[r0:r0 + L, :]
        for h in range(H_C):
            lg = float(np.log1p(-np.exp2(-5.0 - h)))
            intra = jnp.where(rel >= 0, jnp.exp(lg * jnp.maximum(rel, 0.0)), 0.0)
            q_decay = jnp.exp(lg * (idx + 1.0))
            k_decay = jnp.exp(lg * (L - 1.0 - idx))
            s_decay = float(np.exp(lg * L))
            q = rope(q_ref[0, r0:r0 + L, h * DK_C:(h + 1) * DK_C].astype(F32), cos, sin)
            k = rope(k_ref[0, r0:r0 + L, h * DK_C:(h + 1) * DK_C].astype(F32), cos, sin) * (DK_C ** -0.5)
            v = v_ref[0, r0:r0 + L, h * DV_C:(h + 1) * DV_C]
            s = s_o[0, h]
            o = _mmb(_mmb(q, k, NT) * intra, v) + _mmb(q * q_decay, s)
            s_o[0, h] = s_decay * s + _mmb(k * k_decay, v, TN)
            gg = g_ref[0, r0:r0 + L, h * DV_C:(h + 1) * DV_C].astype(F32)
            o = _rms(o) * _silu(gg)
            mixed_ref[0, r0:r0 + L, h * DV_C:(h + 1) * DV_C] = o.astype(mixed_ref.dtype)


def _mixer_c_call(proj, cos, sin, s0, *, L, CB, name):
    b, t, _ = proj.shape
    tb = L * CB
    nt = t // tb
    qk_w, v_w = H_C * DK_C, H_C * DV_C
    sspec = pl.BlockSpec((1, H_C, DK_C, DV_C), lambda bi, i: (bi, 0, 0, 0))
    tspec = pl.BlockSpec((tb, DK_C // 2), lambda bi, i: (i, 0))
    return pl.pallas_call(
        functools.partial(_mixer_c_kernel, L=L, CB=CB),
        grid=(b, nt),
        in_specs=[pl.BlockSpec((1, tb, qk_w), lambda bi, i: (bi, i, 0)),
                  pl.BlockSpec((1, tb, qk_w), lambda bi, i: (bi, i, 1)),
                  pl.BlockSpec((1, tb, v_w), lambda bi, i: (bi, i, 1)),
                  pl.BlockSpec((1, tb, v_w), lambda bi, i: (bi, i, 2)),
                  tspec, tspec, sspec],
        out_specs=[pl.BlockSpec((1, tb, MIX_C), lambda bi, i: (bi, i, 0)), sspec],
        out_shape=[jax.ShapeDtypeStruct((b, t, MIX_C), BF16),
                   jax.ShapeDtypeStruct((b, H_C, DK_C, DV_C), F32)],
        compiler_params=_cparams(("parallel", "arbitrary")),
        name=name,
    )(proj, proj, proj, proj, cos, sin, s0)


def _out_route_kernel(mixed_ref, wout_ref, h_ref, g1_ref, gain_ref, sc_ref, sh_ref, wrt_ref, rb_ref, hx_in_ref,
                      hnew_ref, hx_ref, bkt_ref):
    del hx_in_ref
    bb, tt, m = mixed_ref.shape
    d = h_ref.shape[-1]
    rows = bb * tt
    y = _dot(mixed_ref[...].reshape(rows, m), wout_ref[...])
    h = h_ref[...] + g1_ref[...] * y.reshape(bb, tt, d)
    hnew_ref[...] = h
    hn = _rms(h) * gain_ref[...]
    hn = (hn * (1.0 + sc_ref[...]) + sh_ref[...]).reshape(rows, d)
    hx_ref[:, 0:d] = hn

    logits = _mm3(wrt_ref[...], hn, NT)
    score = _sigmoid(logits)
    sel = score + rb_ref[...]

    def row(a, r):
        return a[r:r + 1, :]

    gscore = []
    for g in range(N_GROUPS):
        a, b, c, e = (row(sel, EPG * g + i) for i in range(EPG))
        hi_ab, lo_ab, hi_ce, lo_ce = jnp.maximum(a, b), jnp.minimum(a, b), jnp.maximum(c, e), jnp.minimum(c, e)
        top1 = jnp.maximum(hi_ab, hi_ce)
        top2 = jnp.maximum(jnp.maximum(lo_ab, lo_ce), jnp.minimum(hi_ab, hi_ce))
        gscore.append(top1 + top2)
    best = jnp.zeros((1, rows), I32)
    bestv = gscore[0]
    for g in range(1, N_GROUPS):
        upd = gscore[g] > bestv
        best = jnp.where(upd, g, best)
        bestv = jnp.where(upd, gscore[g], bestv)

    def pick(a, i):
        out = row(a, i)
        for g in range(1, N_GROUPS):
            out = jnp.where(best == g, row(a, EPG * g + i), out)
        return out

    vsel = [pick(sel, i) for i in range(EPG)]
    vsc = [pick(score, i) for i in range(EPG)]
    i1 = jnp.zeros((1, rows), I32)
    v1 = vsel[0]
    for i in range(1, EPG):
        upd = vsel[i] > v1
        i1 = jnp.where(upd, i, i1)
        v1 = jnp.where(upd, vsel[i], v1)
    i2 = jnp.full((1, rows), -1, I32)
    v2 = jnp.full((1, rows), -jnp.inf, F32)
    for i in range(EPG):
        cand = jnp.where(i1 == i, -jnp.inf, vsel[i])
        upd = cand > v2
        i2 = jnp.where(upd, i, i2)
        v2 = jnp.where(upd, cand, v2)
    s1 = jnp.zeros((1, rows), F32)
    s2 = jnp.zeros((1, rows), F32)
    for i in range(EPG):
        s1 = jnp.where(i1 == i, vsc[i], s1)
        s2 = jnp.where(i2 == i, vsc[i], s2)
    den = s1 + s2
    w1 = s1 / den
    w2 = s2 / den
    first_lo = i1 < i2
    lo = jnp.minimum(i1, i2)
    hi = jnp.maximum(i1, i2)
    wlo = jnp.where(first_lo, w1, w2)
    whi = jnp.where(first_lo, w2, w1)
    pair = jnp.where(lo == 0, hi - 1, jnp.where(lo == 1, hi + 1, 5))
    bkt_ref[...] = (best * 6 + pair).reshape(1, 1, rows)

    aux = jnp.concatenate([wlo, whi, jnp.zeros((LANES - 2, rows), F32)], axis=0)
    hx_ref[:, d:d + LANES] = aux.T


def _out_route_call(mixed, wout, h, g1, gain, sc, sh, wrt, rb, hx_buf, row_block_off, *, bb, tt, name):
    b, t, m = mixed.shape
    d = h.shape[-1]
    rows = bb * tt
    nti = t // tt
    ntot = hx_buf.shape[0]
    xspec = pl.BlockSpec((bb, tt, d), lambda bi, i: (bi, i, 0))
    mspec = pl.BlockSpec((bb, 1, d), lambda bi, i: (bi, 0, 0))
    return pl.pallas_call(
        _out_route_kernel,
        grid=(b // bb, nti),
        in_specs=[pl.BlockSpec((bb, tt, m), lambda bi, i: (bi, i, 0)),
                  pl.BlockSpec((m, d), lambda bi, i: (0, 0)),
                  xspec, mspec,
                  pl.BlockSpec((1, 1, d), lambda bi, i: (0, 0, 0)), mspec, mspec,
                  pl.BlockSpec((N_EXPERTS, d), lambda bi, i: (0, 0)),
                  pl.BlockSpec((N_EXPERTS, 1), lambda bi, i: (0, 0)),
                  pl.BlockSpec(memory_space=pl.ANY)],
        out_specs=[xspec,
                   pl.BlockSpec((rows, HX_W), lambda bi, i: (row_block_off + bi * nti + i, 0)),
                   pl.BlockSpec((1, 1, rows), lambda bi, i: (bi * nti + i, 0, 0))],
        out_shape=[jax.ShapeDtypeStruct((b, t, d), F32),
                   jax.ShapeDtypeStruct((ntot, HX_W), F32),
                   jax.ShapeDtypeStruct(((b // bb) * nti, 1, rows), I32)],
        input_output_aliases={9: 1},
        compiler_params=_cparams(("parallel", "parallel")),
        name=name,
    )(mixed, wout, h, g1, gain.reshape(1, 1, d), sc, sh, wrt, rb, hx_buf)


def _dispatch_kernel(pos_ref, hx_ref, xs_in, xs_out, sem):
    del xs_in
    i = pl.program_id(0)
    rows = hx_ref.shape[0]

    def row_copy(r):
        p = pos_ref[i * rows + r]
        return pltpu.make_async_copy(hx_ref.at[pl.ds(r, 1), :], xs_out.at[pl.ds(p, 1), :], sem)

    for r in range(rows):
        row_copy(r).start(priority=r % 2)
    for r in range(rows):
        row_copy(r).wait()


def _dispatch_call(pos, hx, xs_buf, rows):
    n = hx.shape[0]
    return pl.pallas_call(
        _dispatch_kernel,
        grid_spec=pltpu.PrefetchScalarGridSpec(
            num_scalar_prefetch=1, grid=(n // rows,),
            in_specs=[pl.BlockSpec((rows, HX_W), lambda i, ps: (i, 0)), pl.BlockSpec(memory_space=pl.ANY)],
            out_specs=pl.BlockSpec(memory_space=pl.ANY),
            scratch_shapes=[pltpu.SemaphoreType.DMA(())]),
        out_shape=jax.ShapeDtypeStruct(xs_buf.shape, F32),
        input_output_aliases={2: 0},
        compiler_params=_cparams(("arbitrary",)),
        name="dispatch",
    )(pos, hx, xs_buf)


def _moe_kernel(ea_ref, eb_ref, chg_ref, valid_ref, xs_ref, wga_ref, wgb_ref, wua_ref, wub_ref, wda_ref, wdb_ref,
                ys_ref, wg_s, wu_s, wd_s):
    t = pl.program_id(0)

    @pl.when(chg_ref[t] > 0)
    def _():
        rc = 128
        for i in range(D_MODEL // rc):
            wg_s[0, i * rc:(i + 1) * rc, :] = wga_ref[0, 0, i * rc:(i + 1) * rc, :].astype(BF16)
            wg_s[1, i * rc:(i + 1) * rc, :] = wgb_ref[0, 0, i * rc:(i + 1) * rc, :].astype(BF16)
            wu_s[0, i * rc:(i + 1) * rc, :] = wua_ref[0, 0, i * rc:(i + 1) * rc, :].astype(BF16)
            wu_s[1, i * rc:(i + 1) * rc, :] = wub_ref[0, 0, i * rc:(i + 1) * rc, :].astype(BF16)
        for i in range(D_FF // rc):
            wd_s[0, i * rc:(i + 1) * rc, :] = wda_ref[0, 0, i * rc:(i + 1) * rc, :].astype(BF16)
            wd_s[1, i * rc:(i + 1) * rc, :] = wdb_ref[0, 0, i * rc:(i + 1) * rc, :].astype(BF16)

    @pl.when(valid_ref[t] > 0)
    def _():
        x = xs_ref[:, 0:D_MODEL].astype(BF16)
        wlo = xs_ref[:, D_MODEL:D_MODEL + 1]
        whi = xs_ref[:, D_MODEL + 1:D_MODEL + 2]
        act_a = (_silu(_dot(x, wg_s[0])) * _dot(x, wu_s[0]) * wlo).astype(BF16)
        act_b = (_silu(_dot(x, wg_s[1])) * _dot(x, wu_s[1]) * whi).astype(BF16)
        ys_ref[...] = _dot(act_a, wd_s[0]) + _dot(act_b, wd_s[1])

    @pl.when(valid_ref[t] == 0)
    def _():
        ys_ref[...] = jnp.zeros_like(ys_ref)


def _moe_call(ea, eb, chg, valid, xs, wg, wu, wd, layer):
    p = xs.shape[0]
    n_tiles = p // MOE_TM
    wa_in = pl.BlockSpec((1, 1, D_MODEL, D_FF), lambda t, ea, eb, ch, sr: (layer, ea[t], 0, 0))
    wb_in = pl.BlockSpec((1, 1, D_MODEL, D_FF), lambda t, ea, eb, ch, sr: (layer, eb[t], 0, 0))
    wa_out = pl.BlockSpec((1, 1, D_FF, D_MODEL), lambda t, ea, eb, ch, sr: (layer, ea[t], 0, 0))
    wb_out = pl.BlockSpec((1, 1, D_FF, D_MODEL), lambda t, ea, eb, ch, sr: (layer, eb[t], 0, 0))
    return pl.pallas_call(
        _moe_kernel,
        grid_spec=pltpu.PrefetchScalarGridSpec(
            num_scalar_prefetch=4, grid=(n_tiles,),
            in_specs=[pl.BlockSpec((MOE_TM, HX_W), lambda t, ea, eb, ch, sr: (t, 0)),
                      wa_in, wb_in, wa_in, wb_in, wa_out, wb_out],
            out_specs=pl.BlockSpec((MOE_TM, D_MODEL), lambda t, ea, eb, ch, sr: (t, 0)),
            scratch_shapes=[pltpu.VMEM((2, D_MODEL, D_FF), BF16), pltpu.VMEM((2, D_MODEL, D_FF), BF16),
                            pltpu.VMEM((2, D_FF, D_MODEL), BF16)]),
        out_shape=jax.ShapeDtypeStruct((p, D_MODEL), F32),
        compiler_params=_cparams(("arbitrary",)),
        name="moe",
    )(ea, eb, chg, valid, xs, wg, wg, wu, wu, wd, wd)


_PAIRS = [(0, 1), (0, 2), (0, 3), (1, 2), (1, 3), (2, 3)]
_EA_TABLE = np.array([g * EPG + p[0] for g in range(N_GROUPS) for p in _PAIRS], np.int32)
_EB_TABLE = np.array([g * EPG + p[1] for g in range(N_GROUPS) for p in _PAIRS], np.int32)


def _moe(hx, bkt, xs_buf, wg, wu, wd, layer):
    n = hx.shape[0]
    tm = MOE_TM
    ar = jnp.arange(N_BUCKETS, dtype=I32)
    oh = (bkt[:, None] == ar[None, :]).astype(I32)
    blk = LANES
    nb = -(-n // blk)
    ohb = jnp.pad(oh, ((0, nb * blk - n), (0, 0))).reshape(nb, blk, N_BUCKETS)
    tril = jnp.tril(jnp.ones((blk, blk), F32))
    inner = jnp.einsum('ij,bjk->bik', tril, ohb.astype(F32))
    tot = inner[:, -1, :]
    outer = jnp.cumsum(tot, axis=0) - tot
    cs = (inner + outer[:, None, :]).reshape(nb * blk, N_BUCKETS)[:n].astype(I32)
    rank = jnp.sum(oh * cs, axis=1) - 1
    counts = cs[-1]
    padded = ((counts + tm - 1) // tm) * tm
    ends = jnp.cumsum(padded)
    offs = ends - padded
    pos = jnp.sum(oh * offs[None, :], axis=1) + rank
    n_tiles = -(-n // tm) + N_BUCKETS
    p = n_tiles * tm
    if xs_buf is None:
        xs_buf = jnp.zeros((p, HX_W), F32)
    tstart = jnp.arange(n_tiles, dtype=I32) * tm
    tb = jnp.sum((ends[None, :] <= tstart[:, None]).astype(I32), axis=1)
    n_used = ends[-1] // tm
    valid = (jnp.arange(n_tiles, dtype=I32) < n_used).astype(I32)
    last_b = tb[jnp.maximum(n_used - 1, 0)]
    tb = jnp.clip(jnp.where(valid > 0, tb, last_b), 0, N_BUCKETS - 1)
    ea = jnp.asarray(_EA_TABLE)[tb]
    eb = jnp.asarray(_EB_TABLE)[tb]
    chg = jnp.concatenate([jnp.ones((1,), I32), (tb[1:] != tb[:-1]).astype(I32)])
    xs = _dispatch_call(pos, hx, xs_buf, tm)
    ys = _moe_call(ea, eb, chg, valid, xs, wg, wu, wd, layer)
    return jnp.take(ys, pos, axis=0), xs


def _final_kernel(h_ref, y_ref, g_ref, gain_ref, o_ref):
    bb, tt, d = h_ref.shape
    h = h_ref[...] + g_ref[...] * y_ref[...].reshape(bb, tt, d)
    o_ref[...] = _rms(h) * gain_ref[...]


def _final_call(h, y2d, off, g2, gain, *, bb, tt, name):
    b, t, d = h.shape
    nti = t // tt
    rows = bb * tt
    xspec = pl.BlockSpec((bb, tt, d), lambda bi, i: (bi, i, 0))
    return pl.pallas_call(
        _final_kernel,
        grid=(b // bb, nti),
        in_specs=[xspec, pl.BlockSpec((rows, d), lambda bi, i: (off + bi * nti + i, 0)),
                  pl.BlockSpec((bb, 1, d), lambda bi, i: (bi, 0, 0)),
                  pl.BlockSpec((1, 1, d), lambda bi, i: (0, 0, 0))],
        out_specs=xspec,
        out_shape=jax.ShapeDtypeStruct((b, t, d), F32),
        compiler_params=_cparams(("parallel", "parallel")),
        name=name,
    )(h, y2d, g2, gain.reshape(1, 1, d))


def _group_cfg(b, t):
    if t >= 512:
        return dict(bb=1, tt=512, L=CHUNK, CB=4)
    return dict(bb=b, tt=t, L=min(CHUNK, t), CB=1)


def _rope_tables(pos0, t):
    half = DK_C // 2
    inv = jnp.power(ROPE_BASE, -jnp.linspace(0.0, 1.0, half, dtype=F32))
    ang = (pos0 + jnp.arange(t, dtype=F32))[:, None] * inv[None, :]
    return jnp.cos(ang), jnp.sin(ang)


def kernel(x_prompt, x_sample, c_prompt, c_sample, state_a_conv, state_a_rec, state_b_c, state_b_n, state_b_m,
           state_c_rec, w_mod, b_mod, norm_mix, norm_ffn, w_in_ab, conv_a, a_log, dt_bias, norm_a, gate_bias_b,
           norm_b, w_out_ab, w_in_c, w_out_c, w_router, router_bias, w_gate, w_up, w_down, norm_final):
    d = D_MODEL
    bp, tp, _ = x_prompt.shape
    bs, ts, _ = x_sample.shape
    n_ab, n_c = w_in_ab.shape[0], w_in_c.shape[0]

    mod_all = _mod_call(jnp.concatenate([c_prompt, c_sample], axis=0), w_mod, b_mod)

    o = np.cumsum((0,) + AB_SIZES)
    w_ab_main = jnp.concatenate([w_in_ab[:, :, o[0]:o[1]].astype(BF16), w_in_ab[:, :, o[3]:o[7]].astype(BF16),
                                 w_in_ab[:, :, o[8]:o[9]].astype(BF16)], axis=-1)
    w_ab_gate = jnp.concatenate([w_in_ab[:, :, o[1]:o[3]], w_in_ab[:, :, o[7]:o[8]]], axis=-1)
    w_ab_gate_pad = jnp.pad(w_ab_gate, ((0, 0), (0, 0), (0, LANES - N_GATE_COLS)))
    w_ab_gate_t = jnp.swapaxes(w_ab_gate, 1, 2)
    w_c = w_in_c.astype(BF16)
    w_out_ab_b = w_out_ab.astype(BF16)
    w_out_c_b = w_out_c.astype(BF16)
    wrt = w_router.T
    rb = router_bias.reshape(N_EXPERTS, 1)
    zeros4 = jnp.zeros((n_ab, H_A), F32)
    gpar_row0 = jnp.concatenate([a_log, jnp.zeros((n_ab, LANES - H_A), F32)], axis=1)
    gpar_row1 = jnp.concatenate([dt_bias, zeros4, gate_bias_b, jnp.zeros((n_ab, LANES - 16), F32)], axis=1)
    gpar = jnp.concatenate([gpar_row0[:, None], gpar_row1[:, None], jnp.zeros((n_ab, 6, LANES), F32)], axis=1)
    gpart = jnp.swapaxes(jnp.concatenate([gpar[:, :2, :N_GATE_COLS], jnp.zeros((n_ab, LANES - 2, N_GATE_COLS), F32)],
                                         axis=1), 1, 2)

    groups = []
    zf = lambda *s: jnp.zeros(s, F32)
    groups.append(dict(
        x=x_prompt, b=bp, t=tp, pos0=0.0, mod=mod_all[:, :bp], row0=0,
        conv=zf(n_ab, bp, CONV_W - 1, QKV_A), sa=zf(n_ab, bp, H_A, DK_A, DV_A), cb=zf(n_ab, bp, H_B, DQK_B, DV_B),
        nb=zf(n_ab, bp, H_B, DQK_B), mb=zf(n_ab, bp, H_B), sc=zf(n_c, bp, H_C, DK_C, DV_C)))
    groups.append(dict(
        x=x_sample, b=bs, t=ts, pos0=float(PAST_LEN), mod=mod_all[:, bp:], row0=bp * tp,
        conv=state_a_conv, sa=state_a_rec, cb=state_b_c, nb=state_b_n, mb=state_b_m, sc=state_c_rec))
    n_tot = bp * tp + bs * ts
    for g in groups:
        g.update(_group_cfg(g["b"], g["t"]))
        g["h"] = g["x"]
        g["prev"] = None
        g["new_ab"], g["new_c"] = [], []
        g["cos"], g["sin"] = _rope_tables(g["pos0"], g["t"])
        g["roff"] = g["row0"] // (g["bb"] * g["tt"])

    xs_buf = None
    for layer in range(DEPTH):
        li = layer // 2
        if layer == 0:
            hx_buf = jnp.zeros((n_tot, HX_W), F32)
        bkts = []
        for gi, g in enumerate(groups):
            b, t, bb, tt = g["b"], g["t"], g["bb"], g["tt"]
            mods = [m.reshape(b, 1, d) for m in jnp.split(g["mod"][layer], N_MOD, axis=-1)]
            sh1, sc1, g1, sh2, sc2, g2 = mods
            tag = f"l{layer}g{gi}"
            if layer % 2 == 0:
                outs = _ln_mm_call(g["h"], g["prev"], norm_mix[layer], sc1, sh1, w_ab_main[li],
                                   (w_ab_gate_pad[li], w_ab_gate_t[li]), bb=bb, tt=tt, tn=512, out_dtype=F32,
                                   name="ln_ab_" + tag)
                if g["prev"] is not None:
                    g["h"], outs = outs[0], outs[1:]
                proj, gates, gates_t = outs
                gates_t = gates_t.reshape(N_GATE_COLS, b, t).transpose(1, 0, 2)
                mixed, conv_n, sa_n, cb_n, nb_n, mb_n = _mixer_ab_call(
                    proj, gates, gates_t, g["conv"][li], g["sa"][li], g["cb"][li],
                    g["nb"][li].reshape(b, H_B, 1, DQK_B), g["mb"][li].reshape(b, H_B, 1, 1),
                    conv_a[li], gpar[li], gpart[li], norm_a[li].reshape(1, DV_A), norm_b[li].reshape(1, DV_B),
                    L=g["L"], CB=g["CB"], name="mixer_ab_" + tag)
                g["new_ab"].append((conv_n, sa_n, cb_n, nb_n.reshape(b, H_B, DQK_B), mb_n.reshape(b, H_B)))
                wout = w_out_ab_b[li]
            else:
                outs = _ln_mm_call(g["h"], g["prev"], norm_mix[layer], sc1, sh1, w_c[li], None,
                                   bb=bb, tt=tt, tn=512, out_dtype=BF16, name="ln_c_" + tag)
                if g["prev"] is not None:
                    g["h"], outs = outs[0], outs[1:]
                (proj,) = outs
                mixed, sc_n = _mixer_c_call(proj, g["cos"], g["sin"], g["sc"][li], L=g["L"], CB=g["CB"],
                                            name="mixer_c_" + tag)
                g["new_c"].append(sc_n)
                wout = w_out_c_b[li]
            g["h"], hx_buf, bkt = _out_route_call(mixed, wout, g["h"], g1, norm_ffn[layer], sc2, sh2, wrt, rb,
                                                  hx_buf, g["roff"], bb=bb, tt=tt, name="out_route_" + tag)
            bkts.append(bkt.reshape(-1))
            g["g2"] = g2
        y_moe, xs_buf = _moe(hx_buf, jnp.concatenate(bkts), xs_buf, w_gate, w_up, w_down, layer)
        for g in groups:
            g["prev"] = (y_moe, g["roff"], g["g2"])

    outs = []
    for gi, g in enumerate(groups):
        y2d, off, g2 = g["prev"]
        y = _final_call(g["h"], y2d, off, g2, norm_final, bb=g["bb"], tt=g["tt"], name=f"final_g{gi}")
        ab = [jnp.stack(s) for s in zip(*g["new_ab"])]
        outs.append((y, ab[0], ab[1], ab[2], ab[3], ab[4], jnp.stack(g["new_c"])))
    p, s = outs
    return (p[0], s[0], p[1], p[2], p[3], p[4], p[5], p[6], s[1], s[2], s[3], s[4], s[5], s[6])
```

```python
import functools

import numpy as np
import jax
import jax.numpy as jnp
from jax import lax
from jax.experimental import pallas as pl
from jax.experimental.pallas import tpu as pltpu

F32 = jnp.float32
BF16 = jnp.bfloat16
I32 = jnp.int32

D_MODEL = 1024
DEPTH = 2
CHUNK = 64
H_A, DK_A, DV_A, CONV_W = 4, 128, 128, 4
QKV_A = H_A * (2 * DK_A + DV_A)
H_B, DQK_B, DV_B = 4, 64, 128
H_C, DK_C, DV_C = 4, 256, 512
ROPE_BASE = 10000.0
PAST_LEN = 4096
AB_SIZES = (QKV_A, H_A, H_A, H_A * DV_A, H_B * DQK_B, H_B * DQK_B, H_B * DV_B, 2 * H_B, H_B * DV_B)
MIX_AB = H_A * DV_A + H_B * DV_B
MIX_C = H_C * DV_C
IN_C = 2 * H_C * DK_C + 2 * H_C * DV_C
N_EXPERTS, N_GROUPS, EPG, D_FF = 16, 4, 4, 512
N_MOD = 6
EPS = 1e-6

LANES = 128
AB_MAIN = 3584
N_GATE_COLS = 16
HX_W = D_MODEL + LANES
N_BUCKETS = N_GROUPS * 6
MOE_TM = 256
VMEM_LIMIT = 48 * 1024 * 1024

NN = ((1,), (0,))
NT = ((1,), (1,))
TN = ((0,), (0,))


def _dot(a, b, dims=NN):
    return lax.dot_general(a, b, (dims, ((), ())), preferred_element_type=F32)


def _mmb(a, b, dims=NN):
    return _dot(a.astype(BF16), b.astype(BF16), dims)


def _split2(x):
    hi = x.astype(BF16)
    lo = (x - hi.astype(F32)).astype(BF16)
    return hi, lo


def _split3(x):
    hi = x.astype(BF16)
    r = x - hi.astype(F32)
    mid = r.astype(BF16)
    lo = (r - mid.astype(F32)).astype(BF16)
    return hi, mid, lo


def _mm3(a, b, dims=NN):
    ah, al = _split2(a)
    bh, bl = _split2(b)
    return _dot(ah, bh, dims) + (_dot(ah, bl, dims) + _dot(al, bh, dims))


def _mm_mask_l(mask_bf16, x, dims=NN):
    h, m, l = _split3(x)
    return _dot(mask_bf16, h, dims) + (_dot(mask_bf16, m, dims) + _dot(mask_bf16, l, dims))


def _mm_mask_r(x, mask_bf16):
    h, m, l = _split3(x)
    return _dot(h, mask_bf16) + (_dot(m, mask_bf16) + _dot(l, mask_bf16))


def _sigmoid(x):
    return 1.0 / (1.0 + jnp.exp(-x))


def _silu(x):
    return x * _sigmoid(x)


def _softplus(x):
    return jnp.maximum(x, 0.0) + jnp.log(1.0 + jnp.exp(-jnp.abs(x)))


def _rms(x, eps=EPS):
    return x * lax.rsqrt(jnp.mean(x * x, axis=-1, keepdims=True) + eps)


def _cparams(sem):
    return pltpu.CompilerParams(dimension_semantics=sem, vmem_limit_bytes=VMEM_LIMIT)


def _mod_kernel(c_ref, w_ref, b_ref, o_ref):
    c = c_ref[...]
    o_ref[0] = _mm3(_silu(c), w_ref[0]) + b_ref[0]


def _mod_call(c_all, w_mod, b_mod):
    bt = c_all.shape[0]
    e = w_mod.shape[-1]
    tn = 1024
    return pl.pallas_call(
        _mod_kernel,
        grid=(DEPTH, e // tn),
        in_specs=[pl.BlockSpec((bt, D_MODEL), lambda l, j: (0, 0)),
                  pl.BlockSpec((1, D_MODEL, tn), lambda l, j: (l, 0, j)),
                  pl.BlockSpec((1, 1, tn), lambda l, j: (l, 0, j))],
        out_specs=pl.BlockSpec((1, bt, tn), lambda l, j: (l, 0, j)),
        out_shape=jax.ShapeDtypeStruct((DEPTH, bt, e), F32),
        compiler_params=_cparams(("parallel", "parallel")),
        name="mod",
    )(c_all, w_mod, b_mod.reshape(DEPTH, 1, e))


def _ln_mm_kernel(*refs, has_prev, has_gates, tn):
    it = iter(refs)
    x_ref = next(it)
    if has_prev:
        yp_ref, gp_ref = next(it), next(it)
    gain_ref, sc_ref, sh_ref, w_ref = next(it), next(it), next(it), next(it)
    if has_gates:
        wg_ref, wgt_ref = next(it), next(it)
    if has_prev:
        h_ref = next(it)
    proj_ref = next(it)
    if has_gates:
        g_ref, gt_ref = next(it), next(it)
    hn_s = next(it)

    bb, tt, d = x_ref.shape
    rows = bb * tt
    x = x_ref[...]
    if has_prev:
        x = x + gp_ref[...] * yp_ref[...].reshape(bb, tt, d)
        h_ref[...] = x
    hn = _rms(x) * gain_ref[...]
    hn = hn * (1.0 + sc_ref[...]) + sh_ref[...]
    hn2 = hn.reshape(rows, d)
    hn_s[...] = hn2.astype(BF16)
    if has_gates:
        g_ref[...] = _mm3(hn2, wg_ref[...]).reshape(bb, tt, LANES)
        gt_ref[...] = _mm3(wgt_ref[...], hn2, NT)
    for j in range(proj_ref.shape[-1] // tn):
        cols = slice(j * tn, (j + 1) * tn)
        proj_ref[:, :, cols] = _dot(hn_s[...], w_ref[:, cols]).reshape(bb, tt, tn).astype(proj_ref.dtype)


def _ln_mm_call(x, prev, gain, sc, sh, w, gates_w, *, bb, tt, tn, out_dtype, name):
    b, t, d = x.shape
    e = w.shape[1]
    rows = bb * tt
    nti = t // tt
    grid = (b // bb, nti)
    has_prev = prev is not None
    has_gates = gates_w is not None
    xspec = pl.BlockSpec((bb, tt, d), lambda bi, i: (bi, i, 0))
    mspec = pl.BlockSpec((bb, 1, d), lambda bi, i: (bi, 0, 0))
    in_specs, args = [xspec], [x]
    if has_prev:
        y2d, off, gp = prev
        in_specs += [pl.BlockSpec((rows, d), lambda bi, i: (off + bi * nti + i, 0)), mspec]
        args += [y2d, gp]
    in_specs += [pl.BlockSpec((1, 1, d), lambda bi, i: (0, 0, 0)), mspec, mspec,
                 pl.BlockSpec((d, e), lambda bi, i: (0, 0), pipeline_mode=pl.Buffered(1))]
    args += [gain.reshape(1, 1, d), sc, sh, w]
    if has_gates:
        wg, wgt = gates_w
        in_specs += [pl.BlockSpec((d, LANES), lambda bi, i: (0, 0)),
                     pl.BlockSpec((N_GATE_COLS, d), lambda bi, i: (0, 0))]
        args += [wg, wgt]
    out_specs, out_shape = [], []
    if has_prev:
        out_specs.append(xspec)
        out_shape.append(jax.ShapeDtypeStruct((b, t, d), F32))
    out_specs.append(pl.BlockSpec((bb, tt, e), lambda bi, i: (bi, i, 0)))
    out_shape.append(jax.ShapeDtypeStruct((b, t, e), out_dtype))
    if has_gates:
        out_specs += [pl.BlockSpec((bb, tt, LANES), lambda bi, i: (bi, i, 0)),
                      pl.BlockSpec((N_GATE_COLS, rows), lambda bi, i: (0, bi * nti + i))]
        out_shape += [jax.ShapeDtypeStruct((b, t, LANES), F32),
                      jax.ShapeDtypeStruct((N_GATE_COLS, b * t), F32)]
    return pl.pallas_call(
        functools.partial(_ln_mm_kernel, has_prev=has_prev, has_gates=has_gates, tn=tn),
        grid=grid, in_specs=in_specs, out_specs=out_specs, out_shape=out_shape,
        scratch_shapes=[pltpu.VMEM((rows, d), BF16)],
        compiler_params=_cparams(("parallel", "parallel")),
        name=name,
    )(*args)


def _tri_inv_all(a_list, eye, length, mm):
    xs = [-a for a in a_list]
    ps = [eye + x for x in xs]
    n = 2
    while n < length:
        xs = [mm(x, x) for x in xs]
        ps = [p + mm(p, x) for p, x in zip(ps, xs)]
        n *= 2
    return ps


def _mixer_ab_kernel(qkv_ref, z_ref, qkb_ref, vb_ref, ob_ref, g_ref, gt_ref,
                     conv0_ref, sa0_ref, cb0_ref, nb0_ref, mb0_ref,
                     convw_ref, gpar_ref, gpart_ref, na_ref, nbn_ref,
                     mixed_ref, conv_o, sa_o, cb_o, nb_o, mb_o,
                     xbuf, *, L, CB):
    TB = L * CB

    @pl.when(pl.program_id(1) == 0)
    def _():
        xbuf[5:8, :] = conv0_ref[0]
        sa_o[...] = sa0_ref[...]
        cb_o[...] = cb0_ref[...]
        nb_o[...] = nb0_ref[...]
        mb_o[...] = mb0_ref[...]

    xbuf[8:8 + TB, :] = qkv_ref[0]
    conv_o[0] = xbuf[5 + TB:8 + TB, :]

    ii = lax.broadcasted_iota(I32, (L, L), 0)
    jj = lax.broadcasted_iota(I32, (L, L), 1)
    tri = ii >= jj
    strict = ii > jj
    eye = (ii == jj).astype(F32)
    tril_b = tri.astype(BF16)
    triu_b = (ii <= jj).astype(BF16)
    lane = lax.broadcasted_iota(I32, (L, LANES), 1)
    rowi = lax.broadcasted_iota(I32, (N_GATE_COLS, L), 0)
    gpar = gpar_ref[...]
    gpart = gpart_ref[...]
    nea = -jnp.exp(gpar[0:1, :])
    neat = -jnp.exp(gpart[:, 0:1])
    norm_a = na_ref[...]
    norm_b = nbn_ref[...]

    mm = _mmb
    mm_inv = _mm3

    def conv_tile(r0, col0):
        acc = None
        for j in range(CONV_W):
            term = convw_ref[j:j + 1, col0:col0 + DK_A] * xbuf[5 + j + r0:5 + j + r0 + L, col0:col0 + DK_A]
            acc = term if acc is None else acc + term
        return _silu(acc)

    gates = []
    for c in range(CB):
        r0 = c * L
        pre = g_ref[0, r0:r0 + L, :] + gpar[1:2, :]
        csrc = jnp.where(lane < 4, nea * _softplus(pre),
                         jnp.where(lane >= 12, -_softplus(-pre), 0.0))
        csrc = jnp.where(lane < N_GATE_COLS, csrc, 0.0)
        gcum = _mm_mask_l(tril_b, csrc)
        pret = gt_ref[0, :, r0:r0 + L] + gpart[:, 1:2]
        csrct = jnp.where(rowi < 4, neat * _softplus(pret),
                          jnp.where(rowi >= 12, -_softplus(-pret), 0.0))
        gcumt = _mm_mask_r(csrct, triu_b)
        gates.append((pre, pret, gcum, gcumt, _sigmoid(pre)))

    ga = []
    a_list = []
    for c in range(CB):
        r0 = c * L
        pre, pret, gcum, gcumt, beta_all = gates[c]
        for h in range(H_A):
            q = conv_tile(r0, h * DK_A)
            k = conv_tile(r0, H_A * DK_A + h * DK_A)
            v = conv_tile(r0, 2 * H_A * DK_A + h * DV_A)
            q = q * lax.rsqrt(jnp.sum(q * q, axis=-1, keepdims=True) + EPS) * (DK_A ** -0.5)
            k = k * lax.rsqrt(jnp.sum(k * k, axis=-1, keepdims=True) + EPS)
            gc_c = gcum[:, h:h + 1]
            gc_r = gcumt[h:h + 1, :]
            decay = jnp.where(tri, jnp.exp(jnp.where(tri, gc_c - gc_r, 0.0)), 0.0)
            beta_c = beta_all[:, 4 + h:5 + h]
            kb = k * beta_c
            a_list.append(jnp.where(strict, mm_inv(kb, k, NT) * decay, 0.0))
            eg = jnp.exp(gc_c)
            gl = gc_c[L - 1:L, :]
            attn = jnp.where(tri, mm(q, k, NT) * decay, 0.0)
            ga.append(dict(kbeg=kb * eg, vb=v * beta_c, qeg=q * eg, attn=attn,
                           kdec=k * jnp.exp(gl - gc_c), sdec=jnp.exp(gl)))
    tinv = _tri_inv_all(a_list, eye, L, mm_inv)
    for d, t in zip(ga, tinv):
        d["w"] = mm_inv(t, d.pop("kbeg"))
        d["u"] = mm_inv(t, d.pop("vb"))

    gb = []
    for c in range(CB):
        r0 = c * L
        pre, pret, gcum, gcumt, _ = gates[c]
        for h in range(H_B):
            q = qkb_ref[0, r0:r0 + L, h * DQK_B:(h + 1) * DQK_B] * (DQK_B ** -0.5)
            k = qkb_ref[0, r0:r0 + L, H_B * DQK_B + h * DQK_B:H_B * DQK_B + (h + 1) * DQK_B]
            b_c = gcum[:, 12 + h:13 + h]
            b_r = gcumt[12 + h:13 + h, :]
            li_r = pret[8 + h:9 + h, :]
            dm = jnp.where(tri, b_c - b_r + li_r, -jnp.inf)
            gb.append(dict(q=q, k=k, qk=mm(q, k, NT), dm=dm, dmax=jnp.max(dm, axis=-1, keepdims=True),
                           b_c=b_c, li_c=pre[:, 8 + h:9 + h]))

    for c in range(CB):
        r0 = c * L
        for h in range(H_A):
            d = ga[c * H_A + h]
            s = sa_o[0, h]
            v_new = d["u"] - mm(d["w"], s)
            o = mm(d["qeg"], s) + mm(d["attn"], v_new)
            sa_o[0, h] = s * d["sdec"] + mm(d["kdec"], v_new, TN)
            zg = z_ref[0, r0:r0 + L, h * DV_A:(h + 1) * DV_A]
            o = _rms(o) * norm_a * _silu(zg)
            mixed_ref[0, r0:r0 + L, h * DV_A:(h + 1) * DV_A] = o.astype(mixed_ref.dtype)
        for h in range(H_B):
            d = gb[c * H_B + h]
            q, k, b_c = d["q"], d["k"], d["b_c"]
            v = vb_ref[0, r0:r0 + L, h * DV_B:(h + 1) * DV_B]
            m_prev = mb_o[0, h]
            a0 = b_c + m_prev
            m_t = jnp.maximum(a0, d["dmax"])
            w0 = jnp.exp(a0 - m_t)
            sm = d["qk"] * jnp.exp(d["dm"] - m_t)
            cst = cb_o[0, h]
            nst = nb_o[0, h]
            num = w0 * mm(q, cst) + mm(sm, v)
            den = w0 * jnp.sum(q * nst, axis=-1, keepdims=True) + jnp.sum(sm, axis=-1, keepdims=True)
            hh = num / jnp.maximum(jnp.abs(den), jnp.exp(-m_t))
            m_new = m_t[L - 1:L, :]
            wk = jnp.exp(b_c[L - 1:L, :] - b_c + d["li_c"] - m_new)
            w0l = w0[L - 1:L, :]
            kw = k * wk
            cb_o[0, h] = w0l * cst + mm(kw, v, TN)
            nb_o[0, h] = w0l * nst + jnp.sum(kw, axis=0, keepdims=True)
            mb_o[0, h] = m_new
            og = ob_ref[0, r0:r0 + L, h * DV_B:(h + 1) * DV_B]
            o = _rms(hh) * norm_b * _sigmoid(og)
            c0 = H_A * DV_A + h * DV_B
            mixed_ref[0, r0:r0 + L, c0:c0 + DV_B] = o.astype(mixed_ref.dtype)

    xbuf[5:8, :] = xbuf[5 + TB:8 + TB, :]


def _mixer_ab_call(proj, gates, gates_t, conv0, sa0, cb0, nb0, mb0, convw, gpar, gpart, norm_a, norm_b, *, L, CB, name):
    b, t, _ = proj.shape
    tb = L * CB
    nt = t // tb

    def col(width, idx):
        return pl.BlockSpec((1, tb, width), lambda bi, i: (bi, i, idx))

    def const(shape):
        return pl.BlockSpec(shape, lambda bi, i: (0,) * len(shape))

    def state(shape):
        return pl.BlockSpec((1,) + shape, lambda bi, i: (bi,) + (0,) * len(shape))

    in_specs = [col(QKV_A, 0), col(512, 3), col(512, 4), col(512, 5), col(512, 6),
                col(LANES, 0), pl.BlockSpec((1, N_GATE_COLS, tb), lambda bi, i: (bi, 0, i)),
                state((CONV_W - 1, QKV_A)), state((H_A, DK_A, DV_A)), state((H_B, DQK_B, DV_B)),
                state((H_B, 1, DQK_B)), state((H_B, 1, 1)),
                const((CONV_W, QKV_A)), const((8, LANES)), const((N_GATE_COLS, LANES)),
                const((1, DV_A)), const((1, DV_B))]
    out_specs = [pl.BlockSpec((1, tb, MIX_AB), lambda bi, i: (bi, i, 0)),
                 state((CONV_W - 1, QKV_A)), state((H_A, DK_A, DV_A)), state((H_B, DQK_B, DV_B)),
                 state((H_B, 1, DQK_B)), state((H_B, 1, 1))]
    out_shape = [jax.ShapeDtypeStruct((b, t, MIX_AB), BF16),
                 jax.ShapeDtypeStruct((b, CONV_W - 1, QKV_A), F32),
                 jax.ShapeDtypeStruct((b, H_A, DK_A, DV_A), F32),
                 jax.ShapeDtypeStruct((b, H_B, DQK_B, DV_B), F32),
                 jax.ShapeDtypeStruct((b, H_B, 1, DQK_B), F32),
                 jax.ShapeDtypeStruct((b, H_B, 1, 1), F32)]
    return pl.pallas_call(
        functools.partial(_mixer_ab_kernel, L=L, CB=CB),
        grid=(b, nt), in_specs=in_specs, out_specs=out_specs, out_shape=out_shape,
        scratch_shapes=[pltpu.VMEM((8 + tb, QKV_A), F32)],
        compiler_params=_cparams(("parallel", "arbitrary")),
        name=name,
    )(proj, proj, proj, proj, proj, gates, gates_t, conv0, sa0, cb0, nb0, mb0, convw, gpar, gpart, norm_a, norm_b)


def _mixer_c_kernel(q_ref, k_ref, v_ref, g_ref, cos_ref, sin_ref, s0_ref, mixed_ref, s_o, *, L, CB):
    @pl.when(pl.program_id(1) == 0)
    def _():
        s_o[...] = s0_ref[...]

    ii = lax.broadcasted_iota(I32, (L, L), 0)
    jj = lax.broadcasted_iota(I32, (L, L), 1)
    rel = (ii - jj).astype(F32)
    idx = lax.broadcasted_iota(I32, (L, 1), 0).astype(F32)
    half = DK_C // 2

    def rope(x, cos, sin):
        x1, x2 = x[:, :half], x[:, half:]
        return jnp.concatenate([x1 * cos - x2 * sin, x1 * sin + x2 * cos], axis=-1)

    for c in range(CB):
        r0 = c * L
        cos = cos_ref[r0:r0 + L, :]
        sin = sin_ref[r0:r0 + L, :]
        for h in range(H_C):
            lg = float(np.log1p(-np.exp2(-5.0 - h)))
            intra = jnp.where(rel >= 0, jnp.exp(lg * jnp.maximum(rel, 0.0)), 0.0)
            q_decay = jnp.exp(lg * (idx + 1.0))
            k_decay = jnp.exp(lg * (L - 1.0 - idx))
            s_decay = float(np.exp(lg * L))
            q = rope(q_ref[0, r0:r0 + L, h * DK_C:(h + 1) * DK_C].astype(F32), cos, sin)
            k = rope(k_ref[0, r0:r0 + L, h * DK_C:(h + 1) * DK_C].astype(F32), cos, sin) * (DK_C ** -0.5)
            v = v_ref[0, r0:r0 + L, h * DV_C:(h + 1) * DV_C]
            s = s_o[0, h]
            o = _mmb(_mmb(q, k, NT) * intra, v) + _mmb(q * q_decay, s)
            s_o[0, h] = s_decay * s + _mmb(k * k_decay, v, TN)
            gg = g_ref[0, r0:r0 + L, h * DV_C:(h + 1) * DV_C].astype(F32)
            o = _rms(o) * _silu(gg)
            mixed_ref[0, r0:r0 + L, h * DV_C:(h + 1) * DV_C] = o.astype(mixed_ref.dtype)


def _mixer_c_call(proj, cos, sin, s0, *, L, CB, name):
    b, t, _ = proj.shape
    tb = L * CB
    nt = t // tb
    qk_w, v_w = H_C * DK_C, H_C * DV_C
    sspec = pl.BlockSpec((1, H_C, DK_C, DV_C), lambda bi, i: (bi, 0, 0, 0))
    tspec = pl.BlockSpec((tb, DK_C // 2), lambda bi, i: (i, 0))
    return pl.pallas_call(
        functools.partial(_mixer_c_kernel, L=L, CB=CB),
        grid=(b, nt),
        in_specs=[pl.BlockSpec((1, tb, qk_w), lambda bi, i: (bi, i, 0)),
                  pl.BlockSpec((1, tb, qk_w), lambda bi, i: (bi, i, 1)),
                  pl.BlockSpec((1, tb, v_w), lambda bi, i: (bi, i, 1)),
                  pl.BlockSpec((1, tb, v_w), lambda bi, i: (bi, i, 2)),
                  tspec, tspec, sspec],
        out_specs=[pl.BlockSpec((1, tb, MIX_C), lambda bi, i: (bi, i, 0)), sspec],
        out_shape=[jax.ShapeDtypeStruct((b, t, MIX_C), BF16),
                   jax.ShapeDtypeStruct((b, H_C, DK_C, DV_C), F32)],
        compiler_params=_cparams(("parallel", "arbitrary")),
        name=name,
    )(proj, proj, proj, proj, cos, sin, s0)


def _out_route_kernel(mixed_ref, wout_ref, h_ref, g1_ref, gain_ref, sc_ref, sh_ref, wrt_ref, rb_ref, hx_in_ref,
                      hnew_ref, hx_ref, bkt_ref):
    del hx_in_ref
    bb, tt, m = mixed_ref.shape
    d = h_ref.shape[-1]
    rows = bb * tt
    y = _dot(mixed_ref[...].reshape(rows, m), wout_ref[...])
    h = h_ref[...] + g1_ref[...] * y.reshape(bb, tt, d)
    hnew_ref[...] = h
    hn = _rms(h) * gain_ref[...]
    hn = (hn * (1.0 + sc_ref[...]) + sh_ref[...]).reshape(rows, d)
    hx_ref[:, 0:d] = hn

    logits = _mm3(wrt_ref[...], hn, NT)
    score = _sigmoid(logits)
    sel = score + rb_ref[...]

    def row(a, r):
        return a[r:r + 1, :]

    gscore = []
    for g in range(N_GROUPS):
        a, b, c, e = (row(sel, EPG * g + i) for i in range(EPG))
        hi_ab, lo_ab, hi_ce, lo_ce = jnp.maximum(a, b), jnp.minimum(a, b), jnp.maximum(c, e), jnp.minimum(c, e)
        top1 = jnp.maximum(hi_ab, hi_ce)
        top2 = jnp.maximum(jnp.maximum(lo_ab, lo_ce), jnp.minimum(hi_ab, hi_ce))
        gscore.append(top1 + top2)
    best = jnp.zeros((1, rows), I32)
    bestv = gscore[0]
    for g in range(1, N_GROUPS):
        upd = gscore[g] > bestv
        best = jnp.where(upd, g, best)
        bestv = jnp.where(upd, gscore[g], bestv)

    def pick(a, i):
        out = row(a, i)
        for g in range(1, N_GROUPS):
            out = jnp.where(best == g, row(a, EPG * g + i), out)
        return out

    vsel = [pick(sel, i) for i in range(EPG)]
    vsc = [pick(score, i) for i in range(EPG)]
    i1 = jnp.zeros((1, rows), I32)
    v1 = vsel[0]
    for i in range(1, EPG):
        upd = vsel[i] > v1
        i1 = jnp.where(upd, i, i1)
        v1 = jnp.where(upd, vsel[i], v1)
    i2 = jnp.full((1, rows), -1, I32)
    v2 = jnp.full((1, rows), -jnp.inf, F32)
    for i in range(EPG):
        cand = jnp.where(i1 == i, -jnp.inf, vsel[i])
        upd = cand > v2
        i2 = jnp.where(upd, i, i2)
        v2 = jnp.where(upd, cand, v2)
    s1 = jnp.zeros((1, rows), F32)
    s2 = jnp.zeros((1, rows), F32)
    for i in range(EPG):
        s1 = jnp.where(i1 == i, vsc[i], s1)
        s2 = jnp.where(i2 == i, vsc[i], s2)
    den = s1 + s2
    w1 = s1 / den
    w2 = s2 / den
    first_lo = i1 < i2
    lo = jnp.minimum(i1, i2)
    hi = jnp.maximum(i1, i2)
    wlo = jnp.where(first_lo, w1, w2)
    whi = jnp.where(first_lo, w2, w1)
    pair = jnp.where(lo == 0, hi - 1, jnp.where(lo == 1, hi + 1, 5))
    bkt_ref[...] = (best * 6 + pair).reshape(1, 1, rows)

    aux = jnp.concatenate([wlo, whi, jnp.zeros((LANES - 2, rows), F32)], axis=0)
    hx_ref[:, d:d + LANES] = aux.T


def _out_route_call(mixed, wout, h, g1, gain, sc, sh, wrt, rb, hx_buf, row_block_off, *, bb, tt, name):
    b, t, m = mixed.shape
    d = h.shape[-1]
    rows = bb * tt
    nti = t // tt
    ntot = hx_buf.shape[0]
    xspec = pl.BlockSpec((bb, tt, d), lambda bi, i: (bi, i, 0))
    mspec = pl.BlockSpec((bb, 1, d), lambda bi, i: (bi, 0, 0))
    return pl.pallas_call(
        _out_route_kernel,
        grid=(b // bb, nti),
        in_specs=[pl.BlockSpec((bb, tt, m), lambda bi, i: (bi, i, 0)),
                  pl.BlockSpec((m, d), lambda bi, i: (0, 0)),
                  xspec, mspec,
                  pl.BlockSpec((1, 1, d), lambda bi, i: (0, 0, 0)), mspec, mspec,
                  pl.BlockSpec((N_EXPERTS, d), lambda bi, i: (0, 0)),
                  pl.BlockSpec((N_EXPERTS, 1), lambda bi, i: (0, 0)),
                  pl.BlockSpec(memory_space=pl.ANY)],
        out_specs=[xspec,
                   pl.BlockSpec((rows, HX_W), lambda bi, i: (row_block_off + bi * nti + i, 0)),
                   pl.BlockSpec((1, 1, rows), lambda bi, i: (bi * nti + i, 0, 0))],
        out_shape=[jax.ShapeDtypeStruct((b, t, d), F32),
                   jax.ShapeDtypeStruct((ntot, HX_W), F32),
                   jax.ShapeDtypeStruct(((b // bb) * nti, 1, rows), I32)],
        input_output_aliases={9: 1},
        compiler_params=_cparams(("parallel", "parallel")),
        name=name,
    )(mixed, wout, h, g1, gain.reshape(1, 1, d), sc, sh, wrt, rb, hx_buf)


def _dispatch_kernel(pos_ref, hx_ref, xs_in, xs_out, sem):
    del xs_in
    i = pl.program_id(0)
    rows = hx_ref.shape[0]

    def row_copy(r):
        p = pos_ref[i * rows + r]
        return pltpu.make_async_copy(hx_ref.at[pl.ds(r, 1), :], xs_out.at[pl.ds(p, 1), :], sem)

    for r in range(rows):
        row_copy(r).start(priority=r % 2)
    for r in range(rows):
        row_copy(r).wait()


def _dispatch_call(pos, hx, xs_buf, rows):
    n = hx.shape[0]
    return pl.pallas_call(
        _dispatch_kernel,
        grid_spec=pltpu.PrefetchScalarGridSpec(
            num_scalar_prefetch=1, grid=(n // rows,),
            in_specs=[pl.BlockSpec((rows, HX_W), lambda i, ps: (i, 0)), pl.BlockSpec(memory_space=pl.ANY)],
            out_specs=pl.BlockSpec(memory_space=pl.ANY),
            scratch_shapes=[pltpu.SemaphoreType.DMA(())]),
        out_shape=jax.ShapeDtypeStruct(xs_buf.shape, F32),
        input_output_aliases={2: 0},
        compiler_params=_cparams(("arbitrary",)),
        name="dispatch",
    )(pos, hx, xs_buf)


def _moe_kernel(ea_ref, eb_ref, chg_ref, valid_ref, xs_ref, wga_ref, wgb_ref, wua_ref, wub_ref, wda_ref, wdb_ref,
                ys_ref, wg_s, wu_s, wd_s):
    t = pl.program_id(0)

    @pl.when(chg_ref[t] > 0)
    def _():
        rc = 128
        for i in range(D_MODEL // rc):
            wg_s[0, i * rc:(i + 1) * rc, :] = wga_ref[0, 0, i * rc:(i + 1) * rc, :].astype(BF16)
            wg_s[1, i * rc:(i + 1) * rc, :] = wgb_ref[0, 0, i * rc:(i + 1) * rc, :].astype(BF16)
            wu_s[0, i * rc:(i + 1) * rc, :] = wua_ref[0, 0, i * rc:(i + 1) * rc, :].astype(BF16)
            wu_s[1, i * rc:(i + 1) * rc, :] = wub_ref[0, 0, i * rc:(i + 1) * rc, :].astype(BF16)
        for i in range(D_FF // rc):
            wd_s[0, i * rc:(i + 1) * rc, :] = wda_ref[0, 0, i * rc:(i + 1) * rc, :].astype(BF16)
            wd_s[1, i * rc:(i + 1) * rc, :] = wdb_ref[0, 0, i * rc:(i + 1) * rc, :].astype(BF16)

    @pl.when(valid_ref[t] > 0)
    def _():
        x = xs_ref[:, 0:D_MODEL].astype(BF16)
        wlo = xs_ref[:, D_MODEL:D_MODEL + 1]
        whi = xs_ref[:, D_MODEL + 1:D_MODEL + 2]
        act_a = (_silu(_dot(x, wg_s[0])) * _dot(x, wu_s[0]) * wlo).astype(BF16)
        act_b = (_silu(_dot(x, wg_s[1])) * _dot(x, wu_s[1]) * whi).astype(BF16)
        ys_ref[...] = _dot(act_a, wd_s[0]) + _dot(act_b, wd_s[1])

    @pl.when(valid_ref[t] == 0)
    def _():
        ys_ref[...] = jnp.zeros_like(ys_ref)


def _moe_call(ea, eb, chg, valid, xs, wg, wu, wd, layer):
    p = xs.shape[0]
    n_tiles = p // MOE_TM
    wa_in = pl.BlockSpec((1, 1, D_MODEL, D_FF), lambda t, ea, eb, ch, sr: (layer, ea[t], 0, 0))
    wb_in = pl.BlockSpec((1, 1, D_MODEL, D_FF), lambda t, ea, eb, ch, sr: (layer, eb[t], 0, 0))
    wa_out = pl.BlockSpec((1, 1, D_FF, D_MODEL), lambda t, ea, eb, ch, sr: (layer, ea[t], 0, 0))
    wb_out = pl.BlockSpec((1, 1, D_FF, D_MODEL), lambda t, ea, eb, ch, sr: (layer, eb[t], 0, 0))
    return pl.pallas_call(
        _moe_kernel,
        grid_spec=pltpu.PrefetchScalarGridSpec(
            num_scalar_prefetch=4, grid=(n_tiles,),
            in_specs=[pl.BlockSpec((MOE_TM, HX_W), lambda t, ea, eb, ch, sr: (t, 0)),
                      wa_in, wb_in, wa_in, wb_in, wa_out, wb_out],
            out_specs=pl.BlockSpec((MOE_TM, D_MODEL), lambda t, ea, eb, ch, sr: (t, 0)),
            scratch_shapes=[pltpu.VMEM((2, D_MODEL, D_FF), BF16), pltpu.VMEM((2, D_MODEL, D_FF), BF16),
                            pltpu.VMEM((2, D_FF, D_MODEL), BF16)]),
        out_shape=jax.ShapeDtypeStruct((p, D_MODEL), F32),
        compiler_params=_cparams(("arbitrary",)),
        name="moe",
    )(ea, eb, chg, valid, xs, wg, wg, wu, wu, wd, wd)


_PAIRS = [(0, 1), (0, 2), (0, 3), (1, 2), (1, 3), (2, 3)]
_EA_TABLE = np.array([g * EPG + p[0] for g in range(N_GROUPS) for p in _PAIRS], np.int32)
_EB_TABLE = np.array([g * EPG + p[1] for g in range(N_GROUPS) for p in _PAIRS], np.int32)


def _moe(hx, bkt, xs_buf, wg, wu, wd, layer):
    n = hx.shape[0]
    tm = MOE_TM
    ar = jnp.arange(N_BUCKETS, dtype=I32)
    oh = (bkt[:, None] == ar[None, :]).astype(I32)
    blk = LANES
    nb = -(-n // blk)
    ohb = jnp.pad(oh, ((0, nb * blk - n), (0, 0))).reshape(nb, blk, N_BUCKETS)
    tril = jnp.tril(jnp.ones((blk, blk), F32))
    inner = jnp.einsum('ij,bjk->bik', tril, ohb.astype(F32))
    tot = inner[:, -1, :]
    outer = jnp.cumsum(tot, axis=0) - tot
    cs = (inner + outer[:, None, :]).reshape(nb * blk, N_BUCKETS)[:n].astype(I32)
    rank = jnp.sum(oh * cs, axis=1) - 1
    counts = cs[-1]
    padded = ((counts + tm - 1) // tm) * tm
    ends = jnp.cumsum(padded)
    offs = ends - padded
    pos = jnp.sum(oh * offs[None, :], axis=1) + rank
    n_tiles = -(-n // tm) + N_BUCKETS
    p = n_tiles * tm
    if xs_buf is None:
        xs_buf = jnp.zeros((p, HX_W), F32)
    tstart = jnp.arange(n_tiles, dtype=I32) * tm
    tb = jnp.sum((ends[None, :] <= tstart[:, None]).astype(I32), axis=1)
    n_used = ends[-1] // tm
    valid = (jnp.arange(n_tiles, dtype=I32) < n_used).astype(I32)
    last_b = tb[jnp.maximum(n_used - 1, 0)]
    tb = jnp.clip(jnp.where(valid > 0, tb, last_b), 0, N_BUCKETS - 1)
    ea = jnp.asarray(_EA_TABLE)[tb]
    eb = jnp.asarray(_EB_TABLE)[tb]
    chg = jnp.concatenate([jnp.ones((1,), I32), (tb[1:] != tb[:-1]).astype(I32)])
    xs = _dispatch_call(pos, hx, xs_buf, tm)
    ys = _moe_call(ea, eb, chg, valid, xs, wg, wu, wd, layer)
    return jnp.take(ys, pos, axis=0), xs


def _final_kernel(h_ref, y_ref, g_ref, gain_ref, o_ref):
    bb, tt, d = h_ref.shape
    h = h_ref[...] + g_ref[...] * y_ref[...].reshape(bb, tt, d)
    o_ref[...] = _rms(h) * gain_ref[...]


def _final_call(h, y2d, off, g2, gain, *, bb, tt, name):
    b, t, d = h.shape
    nti = t // tt
    rows = bb * tt
    xspec = pl.BlockSpec((bb, tt, d), lambda bi, i: (bi, i, 0))
    return pl.pallas_call(
        _final_kernel,
        grid=(b // bb, nti),
        in_specs=[xspec, pl.BlockSpec((rows, d), lambda bi, i: (off + bi * nti + i, 0)),
                  pl.BlockSpec((bb, 1, d), lambda bi, i: (bi, 0, 0)),
                  pl.BlockSpec((1, 1, d), lambda bi, i: (0, 0, 0))],
        out_specs=xspec,
        out_shape=jax.ShapeDtypeStruct((b, t, d), F32),
        compiler_params=_cparams(("parallel", "parallel")),
        name=name,
    )(h, y2d, g2, gain.reshape(1, 1, d))


def _group_cfg(b, t):
    if t >= 512:
        return dict(bb=1, tt=512, L=CHUNK, CB=4, Lc=4 * CHUNK, CBc=1)
    return dict(bb=b, tt=t, L=min(CHUNK, t), CB=1, Lc=min(CHUNK, t), CBc=1)


def _rope_tables(pos0, t):
    half = DK_C // 2
    inv = jnp.power(ROPE_BASE, -jnp.linspace(0.0, 1.0, half, dtype=F32))
    ang = (pos0 + jnp.arange(t, dtype=F32))[:, None] * inv[None, :]
    return jnp.cos(ang), jnp.sin(ang)


def kernel(x_prompt, x_sample, c_prompt, c_sample, state_a_conv, state_a_rec, state_b_c, state_b_n, state_b_m,
           state_c_rec, w_mod, b_mod, norm_mix, norm_ffn, w_in_ab, conv_a, a_log, dt_bias, norm_a, gate_bias_b,
           norm_b, w_out_ab, w_in_c, w_out_c, w_router, router_bias, w_gate, w_up, w_down, norm_final):
    d = D_MODEL
    bp, tp, _ = x_prompt.shape
    bs, ts, _ = x_sample.shape
    n_ab, n_c = w_in_ab.shape[0], w_in_c.shape[0]

    mod_all = _mod_call(jnp.concatenate([c_prompt, c_sample], axis=0), w_mod, b_mod)

    o = np.cumsum((0,) + AB_SIZES)
    w_ab_main = jnp.concatenate([w_in_ab[:, :, o[0]:o[1]].astype(BF16), w_in_ab[:, :, o[3]:o[7]].astype(BF16),
                                 w_in_ab[:, :, o[8]:o[9]].astype(BF16)], axis=-1)
    w_ab_gate = jnp.concatenate([w_in_ab[:, :, o[1]:o[3]], w_in_ab[:, :, o[7]:o[8]]], axis=-1)
    w_ab_gate_pad = jnp.pad(w_ab_gate, ((0, 0), (0, 0), (0, LANES - N_GATE_COLS)))
    w_ab_gate_t = jnp.swapaxes(w_ab_gate, 1, 2)
    w_c = w_in_c.astype(BF16)
    w_out_ab_b = w_out_ab.astype(BF16)
    w_out_c_b = w_out_c.astype(BF16)
    wrt = w_router.T
    rb = router_bias.reshape(N_EXPERTS, 1)
    zeros4 = jnp.zeros((n_ab, H_A), F32)
    gpar_row0 = jnp.concatenate([a_log, jnp.zeros((n_ab, LANES - H_A), F32)], axis=1)
    gpar_row1 = jnp.concatenate([dt_bias, zeros4, gate_bias_b, jnp.zeros((n_ab, LANES - 16), F32)], axis=1)
    gpar = jnp.concatenate([gpar_row0[:, None], gpar_row1[:, None], jnp.zeros((n_ab, 6, LANES), F32)], axis=1)
    gpart = jnp.swapaxes(jnp.concatenate([gpar[:, :2, :N_GATE_COLS], jnp.zeros((n_ab, LANES - 2, N_GATE_COLS), F32)],
                                         axis=1), 1, 2)

    groups = []
    zf = lambda *s: jnp.zeros(s, F32)
    groups.append(dict(
        x=x_prompt, b=bp, t=tp, pos0=0.0, mod=mod_all[:, :bp], row0=0,
        conv=zf(n_ab, bp, CONV_W - 1, QKV_A), sa=zf(n_ab, bp, H_A, DK_A, DV_A), cb=zf(n_ab, bp, H_B, DQK_B, DV_B),
        nb=zf(n_ab, bp, H_B, DQK_B), mb=zf(n_ab, bp, H_B), sc=zf(n_c, bp, H_C, DK_C, DV_C)))
    groups.append(dict(
        x=x_sample, b=bs, t=ts, pos0=float(PAST_LEN), mod=mod_all[:, bp:], row0=bp * tp,
        conv=state_a_conv, sa=state_a_rec, cb=state_b_c, nb=state_b_n, mb=state_b_m, sc=state_c_rec))
    n_tot = bp * tp + bs * ts
    for g in groups:
        g.update(_group_cfg(g["b"], g["t"]))
        g["h"] = g["x"]
        g["prev"] = None
        g["new_ab"], g["new_c"] = [], []
        g["cos"], g["sin"] = _rope_tables(g["pos0"], g["t"])
        g["roff"] = g["row0"] // (g["bb"] * g["tt"])

    xs_buf = None
    for layer in range(DEPTH):
        li = layer // 2
        if layer == 0:
            hx_buf = jnp.zeros((n_tot, HX_W), F32)
        bkts = []
        for gi, g in enumerate(groups):
            b, t, bb, tt = g["b"], g["t"], g["bb"], g["tt"]
            mods = [m.reshape(b, 1, d) for m in jnp.split(g["mod"][layer], N_MOD, axis=-1)]
            sh1, sc1, g1, sh2, sc2, g2 = mods
            tag = f"l{layer}g{gi}"
            if layer % 2 == 0:
                outs = _ln_mm_call(g["h"], g["prev"], norm_mix[layer], sc1, sh1, w_ab_main[li],
                                   (w_ab_gate_pad[li], w_ab_gate_t[li]), bb=bb, tt=tt, tn=512, out_dtype=F32,
                                   name="ln_ab_" + tag)
                if g["prev"] is not None:
                    g["h"], outs = outs[0], outs[1:]
                proj, gates, gates_t = outs
                gates_t = gates_t.reshape(N_GATE_COLS, b, t).transpose(1, 0, 2)
                mixed, conv_n, sa_n, cb_n, nb_n, mb_n = _mixer_ab_call(
                    proj, gates, gates_t, g["conv"][li], g["sa"][li], g["cb"][li],
                    g["nb"][li].reshape(b, H_B, 1, DQK_B), g["mb"][li].reshape(b, H_B, 1, 1),
                    conv_a[li], gpar[li], gpart[li], norm_a[li].reshape(1, DV_A), norm_b[li].reshape(1, DV_B),
                    L=g["L"], CB=g["CB"], name="mixer_ab_" + tag)
                g["new_ab"].append((conv_n, sa_n, cb_n, nb_n.reshape(b, H_B, DQK_B), mb_n.reshape(b, H_B)))
                wout = w_out_ab_b[li]
            else:
                outs = _ln_mm_call(g["h"], g["prev"], norm_mix[layer], sc1, sh1, w_c[li], None,
                                   bb=bb, tt=tt, tn=512, out_dtype=BF16, name="ln_c_" + tag)
                if g["prev"] is not None:
                    g["h"], outs = outs[0], outs[1:]
                (proj,) = outs
                mixed, sc_n = _mixer_c_call(proj, g["cos"], g["sin"], g["sc"][li], L=g["Lc"], CB=g["CBc"],
                                            name="mixer_c_" + tag)
                g["new_c"].append(sc_n)
                wout = w_out_c_b[li]
            g["h"], hx_buf, bkt = _out_route_call(mixed, wout, g["h"], g1, norm_ffn[layer], sc2, sh2, wrt, rb,
                                                  hx_buf, g["roff"], bb=bb, tt=tt, name="out_route_" + tag)
            bkts.append(bkt.reshape(-1))
            g["g2"] = g2
        y_moe, xs_buf = _moe(hx_buf, jnp.concatenate(bkts), xs_buf, w_gate, w_up, w_down, layer)
        for g in groups:
            g["prev"] = (y_moe, g["roff"], g["g2"])

    outs = []
    for gi, g in enumerate(groups):
        y2d, off, g2 = g["prev"]
        y = _final_call(g["h"], y2d, off, g2, norm_final, bb=g["bb"], tt=g["tt"], name=f"final_g{gi}")
        ab = [jnp.stack(s) for s in zip(*g["new_ab"])]
        outs.append((y, ab[0], ab[1], ab[2], ab[3], ab[4], jnp.stack(g["new_c"])))
    p, s = outs
    return (p[0], s[0], p[1], p[2], p[3], p[4], p[5], p[6], s[1], s[2], s[3], s[4], s[5], s[6])
```

```python
import functools

import numpy as np
import jax
import jax.numpy as jnp
from jax import lax
from jax.experimental import pallas as pl
from jax.experimental.pallas import tpu as pltpu

F32 = jnp.float32
BF16 = jnp.bfloat16
I32 = jnp.int32

D_MODEL = 1024
DEPTH = 2
CHUNK = 64
H_A, DK_A, DV_A, CONV_W = 4, 128, 128, 4
QKV_A = H_A * (2 * DK_A + DV_A)
H_B, DQK_B, DV_B = 4, 64, 128
H_C, DK_C, DV_C = 4, 256, 512
ROPE_BASE = 10000.0
PAST_LEN = 4096
AB_SIZES = (QKV_A, H_A, H_A, H_A * DV_A, H_B * DQK_B, H_B * DQK_B, H_B * DV_B, 2 * H_B, H_B * DV_B)
MIX_AB = H_A * DV_A + H_B * DV_B
MIX_C = H_C * DV_C
IN_C = 2 * H_C * DK_C + 2 * H_C * DV_C
N_EXPERTS, N_GROUPS, EPG, D_FF = 16, 4, 4, 512
N_MOD = 6
EPS = 1e-6

LANES = 128
AB_MAIN = 3584
N_GATE_COLS = 16
HX_W = D_MODEL + LANES
N_BUCKETS = N_GROUPS * 6
MOE_TM = 256
VMEM_LIMIT = 48 * 1024 * 1024

NN = ((1,), (0,))
NT = ((1,), (1,))
TN = ((0,), (0,))


def _dot(a, b, dims=NN):
    return lax.dot_general(a, b, (dims, ((), ())), preferred_element_type=F32)


def _mmb(a, b, dims=NN):
    return _dot(a.astype(BF16), b.astype(BF16), dims)


def _split2(x):
    hi = x.astype(BF16)
    lo = (x - hi.astype(F32)).astype(BF16)
    return hi, lo


def _split3(x):
    hi = x.astype(BF16)
    r = x - hi.astype(F32)
    mid = r.astype(BF16)
    lo = (r - mid.astype(F32)).astype(BF16)
    return hi, mid, lo


def _mm3(a, b, dims=NN):
    ah, al = _split2(a)
    bh, bl = _split2(b)
    return _dot(ah, bh, dims) + (_dot(ah, bl, dims) + _dot(al, bh, dims))


def _mm_mask_l(mask_bf16, x, dims=NN):
    h, m, l = _split3(x)
    return _dot(mask_bf16, h, dims) + (_dot(mask_bf16, m, dims) + _dot(mask_bf16, l, dims))


def _mm_mask_r(x, mask_bf16):
    h, m, l = _split3(x)
    return _dot(h, mask_bf16) + (_dot(m, mask_bf16) + _dot(l, mask_bf16))


def _sigmoid(x):
    return 1.0 / (1.0 + jnp.exp(-x))


def _silu(x):
    return x * _sigmoid(x)


def _softplus(x):
    return jnp.maximum(x, 0.0) + jnp.log(1.0 + jnp.exp(-jnp.abs(x)))


def _rms(x, eps=EPS):
    return x * lax.rsqrt(jnp.mean(x * x, axis=-1, keepdims=True) + eps)


def _cparams(sem):
    return pltpu.CompilerParams(dimension_semantics=sem, vmem_limit_bytes=VMEM_LIMIT)


def _mod_kernel(c_ref, w_ref, b_ref, o_ref):
    c = c_ref[...]
    o_ref[0] = _mm3(_silu(c), w_ref[0]) + b_ref[0]


def _mod_call(c_all, w_mod, b_mod):
    bt = c_all.shape[0]
    e = w_mod.shape[-1]
    tn = 1024
    return pl.pallas_call(
        _mod_kernel,
        grid=(DEPTH, e // tn),
        in_specs=[pl.BlockSpec((bt, D_MODEL), lambda l, j: (0, 0)),
                  pl.BlockSpec((1, D_MODEL, tn), lambda l, j: (l, 0, j)),
                  pl.BlockSpec((1, 1, tn), lambda l, j: (l, 0, j))],
        out_specs=pl.BlockSpec((1, bt, tn), lambda l, j: (l, 0, j)),
        out_shape=jax.ShapeDtypeStruct((DEPTH, bt, e), F32),
        compiler_params=_cparams(("parallel", "parallel")),
        name="mod",
    )(c_all, w_mod, b_mod.reshape(DEPTH, 1, e))


def _ln_mm_kernel(*refs, has_prev, has_gates, tn):
    it = iter(refs)
    x_ref = next(it)
    if has_prev:
        yp_ref, gp_ref = next(it), next(it)
    gain_ref, sc_ref, sh_ref, w_ref = next(it), next(it), next(it), next(it)
    if has_gates:
        wg_ref, wgt_ref = next(it), next(it)
    if has_prev:
        h_ref = next(it)
    proj_ref = next(it)
    if has_gates:
        g_ref, gt_ref = next(it), next(it)
    hn_s = next(it)

    bb, tt, d = x_ref.shape
    rows = bb * tt
    x = x_ref[...]
    if has_prev:
        x = x + gp_ref[...] * yp_ref[...].reshape(bb, tt, d)
        h_ref[...] = x
    hn = _rms(x) * gain_ref[...]
    hn = hn * (1.0 + sc_ref[...]) + sh_ref[...]
    hn2 = hn.reshape(rows, d)
    hn_s[...] = hn2.astype(BF16)
    if has_gates:
        g_ref[...] = _mm3(hn2, wg_ref[...]).reshape(bb, tt, LANES)
        gt_ref[...] = _mm3(wgt_ref[...], hn2, NT)
    for j in range(proj_ref.shape[-1] // tn):
        cols = slice(j * tn, (j + 1) * tn)
        proj_ref[:, :, cols] = _dot(hn_s[...], w_ref[:, cols]).reshape(bb, tt, tn).astype(proj_ref.dtype)


def _ln_mm_call(x, prev, gain, sc, sh, w, gates_w, *, bb, tt, tn, out_dtype, name):
    b, t, d = x.shape
    e = w.shape[1]
    rows = bb * tt
    nti = t // tt
    grid = (b // bb, nti)
    has_prev = prev is not None
    has_gates = gates_w is not None
    xspec = pl.BlockSpec((bb, tt, d), lambda bi, i: (bi, i, 0))
    mspec = pl.BlockSpec((bb, 1, d), lambda bi, i: (bi, 0, 0))
    in_specs, args = [xspec], [x]
    if has_prev:
        y2d, off, gp = prev
        in_specs += [pl.BlockSpec((rows, d), lambda bi, i: (off + bi * nti + i, 0)), mspec]
        args += [y2d, gp]
    in_specs += [pl.BlockSpec((1, 1, d), lambda bi, i: (0, 0, 0)), mspec, mspec,
                 pl.BlockSpec((d, e), lambda bi, i: (0, 0), pipeline_mode=pl.Buffered(1))]
    args += [gain.reshape(1, 1, d), sc, sh, w]
    if has_gates:
        wg, wgt = gates_w
        in_specs += [pl.BlockSpec((d, LANES), lambda bi, i: (0, 0)),
                     pl.BlockSpec((N_GATE_COLS, d), lambda bi, i: (0, 0))]
        args += [wg, wgt]
    out_specs, out_shape = [], []
    if has_prev:
        out_specs.append(xspec)
        out_shape.append(jax.ShapeDtypeStruct((b, t, d), F32))
    out_specs.append(pl.BlockSpec((bb, tt, e), lambda bi, i: (bi, i, 0)))
    out_shape.append(jax.ShapeDtypeStruct((b, t, e), out_dtype))
    if has_gates:
        out_specs += [pl.BlockSpec((bb, tt, LANES), lambda bi, i: (bi, i, 0)),
                      pl.BlockSpec((N_GATE_COLS, rows), lambda bi, i: (0, bi * nti + i))]
        out_shape += [jax.ShapeDtypeStruct((b, t, LANES), F32),
                      jax.ShapeDtypeStruct((N_GATE_COLS, b * t), F32)]
    return pl.pallas_call(
        functools.partial(_ln_mm_kernel, has_prev=has_prev, has_gates=has_gates, tn=tn),
        grid=grid, in_specs=in_specs, out_specs=out_specs, out_shape=out_shape,
        scratch_shapes=[pltpu.VMEM((rows, d), BF16)],
        compiler_params=_cparams(("parallel", "parallel")),
        name=name,
    )(*args)


def _tri_inv_all(a_list, eye, length, mm):
    xs = [-a for a in a_list]
    ps = [eye + x for x in xs]
    n = 2
    while n < length:
        xs = [mm(x, x) for x in xs]
        ps = [p + mm(p, x) for p, x in zip(ps, xs)]
        n *= 2
    return ps


def _mixer_ab_kernel(qkv_ref, z_ref, qkb_ref, vb_ref, ob_ref, g_ref, gt_ref,
                     conv0_ref, sa0_ref, cb0_ref, nb0_ref, mb0_ref,
                     convw_ref, gpar_ref, gpart_ref, na_ref, nbn_ref,
                     mixed_ref, conv_o, sa_o, cb_o, nb_o, mb_o,
                     xbuf, *, L, CB):
    TB = L * CB

    @pl.when(pl.program_id(1) == 0)
    def _():
        xbuf[5:8, :] = conv0_ref[0]
        sa_o[...] = sa0_ref[...]
        cb_o[...] = cb0_ref[...]
        nb_o[...] = nb0_ref[...]
        mb_o[...] = mb0_ref[...]

    xbuf[8:8 + TB, :] = qkv_ref[0]
    conv_o[0] = xbuf[5 + TB:8 + TB, :]

    ii = lax.broadcasted_iota(I32, (L, L), 0)
    jj = lax.broadcasted_iota(I32, (L, L), 1)
    tri = ii >= jj
    strict = ii > jj
    eye = (ii == jj).astype(F32)
    tril_b = tri.astype(BF16)
    triu_b = (ii <= jj).astype(BF16)
    lane = lax.broadcasted_iota(I32, (L, LANES), 1)
    rowi = lax.broadcasted_iota(I32, (N_GATE_COLS, L), 0)
    gpar = gpar_ref[...]
    gpart = gpart_ref[...]
    nea = -jnp.exp(gpar[0:1, :])
    neat = -jnp.exp(gpart[:, 0:1])
    norm_a = na_ref[...]
    norm_b = nbn_ref[...]

    mm = _mmb
    mm_inv = _mm3

    def conv_tile(r0, col0):
        acc = None
        for j in range(CONV_W):
            term = convw_ref[j:j + 1, col0:col0 + DK_A] * xbuf[5 + j + r0:5 + j + r0 + L, col0:col0 + DK_A]
            acc = term if acc is None else acc + term
        return _silu(acc)

    chunks = range(CB)
    items = [(c, h) for c in chunks for h in range(H_A)]
    rs = [slice(c * L, (c + 1) * L) for c in chunks]

    pre = [g_ref[0, r, :] + gpar[1:2, :] for r in rs]
    csrc = [jnp.where(lane < 4, nea * _softplus(p), jnp.where(lane >= 12, -_softplus(-p), 0.0)) for p in pre]
    gcum = [_mm_mask_l(tril_b, jnp.where(lane < N_GATE_COLS, x, 0.0)) for x in csrc]
    pret = [gt_ref[0, :, r] + gpart[:, 1:2] for r in rs]
    csrct = [jnp.where(rowi < 4, neat * _softplus(p), jnp.where(rowi >= 12, -_softplus(-p), 0.0)) for p in pret]
    gcumt = [_mm_mask_r(x, triu_b) for x in csrct]
    beta_all = [_sigmoid(p) for p in pre]

    q = [conv_tile(c * L, h * DK_A) for c, h in items]
    k = [conv_tile(c * L, H_A * DK_A + h * DK_A) for c, h in items]
    v = [conv_tile(c * L, 2 * H_A * DK_A + h * DV_A) for c, h in items]
    q = [x * lax.rsqrt(jnp.sum(x * x, axis=-1, keepdims=True) + EPS) * (DK_A ** -0.5) for x in q]
    k = [x * lax.rsqrt(jnp.sum(x * x, axis=-1, keepdims=True) + EPS) for x in k]
    gc_c = [gcum[c][:, h:h + 1] for c, h in items]
    gc_r = [gcumt[c][h:h + 1, :] for c, h in items]
    decay = [jnp.where(tri, jnp.exp(jnp.where(tri, a - b, 0.0)), 0.0) for a, b in zip(gc_c, gc_r)]
    beta_c = [beta_all[c][:, 4 + h:5 + h] for c, h in items]
    kb = [x * b for x, b in zip(k, beta_c)]
    a_list = [jnp.where(strict, mm_inv(x, y, NT) * d, 0.0) for x, y, d in zip(kb, k, decay)]
    attn = [jnp.where(tri, mm(x, y, NT) * d, 0.0) for x, y, d in zip(q, k, decay)]
    eg = [jnp.exp(x) for x in gc_c]
    gl = [x[L - 1:L, :] for x in gc_c]
    rhs = [jnp.concatenate([x * e, y * b], axis=-1) for x, e, y, b in zip(kb, eg, v, beta_c)]
    qeg = [x * e for x, e in zip(q, eg)]
    kdec = [x * jnp.exp(g - gc) for x, g, gc in zip(k, gl, gc_c)]
    sdec = [jnp.exp(g) for g in gl]
    tinv = _tri_inv_all(a_list, eye, L, mm)
    sol = [mm(t, r) for t, r in zip(tinv, rhs)]
    resid = [r - (s + mm_inv(a, s)) for r, s, a in zip(rhs, sol, a_list)]
    sol = [s + mm(t, r) for s, t, r in zip(sol, tinv, resid)]

    heads = range(H_B)
    qb = [qkb_ref[0, rs[c], h * DQK_B:(h + 1) * DQK_B] * (DQK_B ** -0.5) for c, h in items]
    kbb = [qkb_ref[0, rs[c], H_B * DQK_B + h * DQK_B:H_B * DQK_B + (h + 1) * DQK_B] for c, h in items]
    b_c = [gcum[c][:, 12 + h:13 + h] for c, h in items]
    dm = [jnp.where(tri, gcum[c][:, 12 + h:13 + h] - gcumt[c][12 + h:13 + h, :] + pret[c][8 + h:9 + h, :], -jnp.inf)
          for c, h in items]
    dmax = [jnp.max(x, axis=-1, keepdims=True) for x in dm]
    qkm = [mm(x, y, NT) for x, y in zip(qb, kbb)]
    li_c = [pre[c][:, 8 + h:9 + h] for c, h in items]

    for c in chunks:
        ia = [c * H_A + h for h in heads]
        s = [sa_o[0, h] for h in heads]
        v_new = [sol[i][:, DK_A:] - mm(sol[i][:, :DK_A], s[h]) for h, i in enumerate(ia)]
        m_prev = [mb_o[0, h] for h in heads]
        a0 = [b_c[i] + m_prev[h] for h, i in enumerate(ia)]
        m_t = [jnp.maximum(a0[h], dmax[i]) for h, i in enumerate(ia)]
        w0 = [jnp.exp(a0[h] - m_t[h]) for h in heads]
        sm = [qkm[i] * jnp.exp(dm[i] - m_t[h]) for h, i in enumerate(ia)]
        cst = [cb_o[0, h] for h in heads]
        nst = [nb_o[0, h] for h in heads]
        vb = [vb_ref[0, rs[c], h * DV_B:(h + 1) * DV_B] for h in heads]
        o_a = [mm(qeg[i], s[h]) + mm(attn[i], v_new[h]) for h, i in enumerate(ia)]
        for h, i in enumerate(ia):
            sa_o[0, h] = s[h] * sdec[i] + mm(kdec[i], v_new[h], TN)
        num = [w0[h] * mm(qb[i], cst[h]) + mm(sm[h], vb[h]) for h, i in enumerate(ia)]
        den = [w0[h] * jnp.sum(qb[i] * nst[h], axis=-1, keepdims=True) + jnp.sum(sm[h], axis=-1, keepdims=True)
               for h, i in enumerate(ia)]
        hh = [num[h] / jnp.maximum(jnp.abs(den[h]), jnp.exp(-m_t[h])) for h in heads]
        m_new = [m_t[h][L - 1:L, :] for h in heads]
        kw = [kbb[i] * jnp.exp(b_c[i][L - 1:L, :] - b_c[i] + li_c[i] - m_new[h]) for h, i in enumerate(ia)]
        for h in heads:
            w0l = w0[h][L - 1:L, :]
            cb_o[0, h] = w0l * cst[h] + mm(kw[h], vb[h], TN)
            nb_o[0, h] = w0l * nst[h] + jnp.sum(kw[h], axis=0, keepdims=True)
            mb_o[0, h] = m_new[h]
        for h in heads:
            zg = z_ref[0, rs[c], h * DV_A:(h + 1) * DV_A]
            mixed_ref[0, rs[c], h * DV_A:(h + 1) * DV_A] = (_rms(o_a[h]) * norm_a * _silu(zg)).astype(mixed_ref.dtype)
            og = ob_ref[0, rs[c], h * DV_B:(h + 1) * DV_B]
            c0 = H_A * DV_A + h * DV_B
            mixed_ref[0, rs[c], c0:c0 + DV_B] = (_rms(hh[h]) * norm_b * _sigmoid(og)).astype(mixed_ref.dtype)

    xbuf[5:8, :] = xbuf[5 + TB:8 + TB, :]


def _mixer_ab_call(proj, gates, gates_t, conv0, sa0, cb0, nb0, mb0, convw, gpar, gpart, norm_a, norm_b, *, L, CB, name):
    b, t, _ = proj.shape
    tb = L * CB
    nt = t // tb

    def col(width, idx):
        return pl.BlockSpec((1, tb, width), lambda bi, i: (bi, i, idx))

    def const(shape):
        return pl.BlockSpec(shape, lambda bi, i: (0,) * len(shape))

    def state(shape):
        return pl.BlockSpec((1,) + shape, lambda bi, i: (bi,) + (0,) * len(shape))

    in_specs = [col(QKV_A, 0), col(512, 3), col(512, 4), col(512, 5), col(512, 6),
                col(LANES, 0), pl.BlockSpec((1, N_GATE_COLS, tb), lambda bi, i: (bi, 0, i)),
                state((CONV_W - 1, QKV_A)), state((H_A, DK_A, DV_A)), state((H_B, DQK_B, DV_B)),
                state((H_B, 1, DQK_B)), state((H_B, 1, 1)),
                const((CONV_W, QKV_A)), const((8, LANES)), const((N_GATE_COLS, LANES)),
                const((1, DV_A)), const((1, DV_B))]
    out_specs = [pl.BlockSpec((1, tb, MIX_AB), lambda bi, i: (bi, i, 0)),
                 state((CONV_W - 1, QKV_A)), state((H_A, DK_A, DV_A)), state((H_B, DQK_B, DV_B)),
                 state((H_B, 1, DQK_B)), state((H_B, 1, 1))]
    out_shape = [jax.ShapeDtypeStruct((b, t, MIX_AB), BF16),
                 jax.ShapeDtypeStruct((b, CONV_W - 1, QKV_A), F32),
                 jax.ShapeDtypeStruct((b, H_A, DK_A, DV_A), F32),
                 jax.ShapeDtypeStruct((b, H_B, DQK_B, DV_B), F32),
                 jax.ShapeDtypeStruct((b, H_B, 1, DQK_B), F32),
                 jax.ShapeDtypeStruct((b, H_B, 1, 1), F32)]
    return pl.pallas_call(
        functools.partial(_mixer_ab_kernel, L=L, CB=CB),
        grid=(b, nt), in_specs=in_specs, out_specs=out_specs, out_shape=out_shape,
        scratch_shapes=[pltpu.VMEM((8 + tb, QKV_A), F32)],
        compiler_params=_cparams(("parallel", "arbitrary")),
        name=name,
    )(proj, proj, proj, proj, proj, gates, gates_t, conv0, sa0, cb0, nb0, mb0, convw, gpar, gpart, norm_a, norm_b)


def _mixer_c_kernel(q_ref, k_ref, v_ref, g_ref, cos_ref, sin_ref, s0_ref, mixed_ref, s_o, *, L, CB):
    @pl.when(pl.program_id(1) == 0)
    def _():
        s_o[...] = s0_ref[...]

    ii = lax.broadcasted_iota(I32, (L, L), 0)
    jj = lax.broadcasted_iota(I32, (L, L), 1)
    rel = (ii - jj).astype(F32)
    idx = lax.broadcasted_iota(I32, (L, 1), 0).astype(F32)
    half = DK_C // 2

    def rope(x, cos, sin):
        x1, x2 = x[:, :half], x[:, half:]
        return jnp.concatenate([x1 * cos - x2 * sin, x1 * sin + x2 * cos], axis=-1)

    for c in range(CB):
        r0 = c * L
        cos = cos_ref[r0:r0 + L, :]
        sin = sin_ref[r0:r0 + L, :]
        for h in range(H_C):
            lg = float(np.log1p(-np.exp2(-5.0 - h)))
            intra = jnp.where(rel >= 0, jnp.exp(lg * jnp.maximum(rel, 0.0)), 0.0)
            q_decay = jnp.exp(lg * (idx + 1.0))
            k_decay = jnp.exp(lg * (L - 1.0 - idx))
            s_decay = float(np.exp(lg * L))
            q = rope(q_ref[0, r0:r0 + L, h * DK_C:(h + 1) * DK_C].astype(F32), cos, sin)
            k = rope(k_ref[0, r0:r0 + L, h * DK_C:(h + 1) * DK_C].astype(F32), cos, sin) * (DK_C ** -0.5)
            v = v_ref[0, r0:r0 + L, h * DV_C:(h + 1) * DV_C]
            s = s_o[0, h]
            o = _mmb(_mmb(q, k, NT) * intra, v) + _mmb(q * q_decay, s)
            s_o[0, h] = s_decay * s + _mmb(k * k_decay, v, TN)
            gg = g_ref[0, r0:r0 + L, h * DV_C:(h + 1) * DV_C].astype(F32)
            o = _rms(o) * _silu(gg)
            mixed_ref[0, r0:r0 + L, h * DV_C:(h + 1) * DV_C] = o.astype(mixed_ref.dtype)


def _mixer_c_call(proj, cos, sin, s0, *, L, CB, name):
    b, t, _ = proj.shape
    tb = L * CB
    nt = t // tb
    qk_w, v_w = H_C * DK_C, H_C * DV_C
    sspec = pl.BlockSpec((1, H_C, DK_C, DV_C), lambda bi, i: (bi, 0, 0, 0))
    tspec = pl.BlockSpec((tb, DK_C // 2), lambda bi, i: (i, 0))
    return pl.pallas_call(
        functools.partial(_mixer_c_kernel, L=L, CB=CB),
        grid=(b, nt),
        in_specs=[pl.BlockSpec((1, tb, qk_w), lambda bi, i: (bi, i, 0)),
                  pl.BlockSpec((1, tb, qk_w), lambda bi, i: (bi, i, 1)),
                  pl.BlockSpec((1, tb, v_w), lambda bi, i: (bi, i, 1)),
                  pl.BlockSpec((1, tb, v_w), lambda bi, i: (bi, i, 2)),
                  tspec, tspec, sspec],
        out_specs=[pl.BlockSpec((1, tb, MIX_C), lambda bi, i: (bi, i, 0)), sspec],
        out_shape=[jax.ShapeDtypeStruct((b, t, MIX_C), BF16),
                   jax.ShapeDtypeStruct((b, H_C, DK_C, DV_C), F32)],
        compiler_params=_cparams(("parallel", "arbitrary")),
        name=name,
    )(proj, proj, proj, proj, cos, sin, s0)


def _out_route_kernel(mixed_ref, wout_ref, h_ref, g1_ref, gain_ref, sc_ref, sh_ref, wrt_ref, rb_ref, hx_in_ref,
                      hnew_ref, hx_ref, bkt_ref):
    del hx_in_ref
    bb, tt, m = mixed_ref.shape
    d = h_ref.shape[-1]
    rows = bb * tt
    y = _dot(mixed_ref[...].reshape(rows, m), wout_ref[...])
    h = h_ref[...] + g1_ref[...] * y.reshape(bb, tt, d)
    hnew_ref[...] = h
    hn = _rms(h) * gain_ref[...]
    hn = (hn * (1.0 + sc_ref[...]) + sh_ref[...]).reshape(rows, d)
    hx_ref[:, 0:d] = hn

    logits = _mm3(wrt_ref[...], hn, NT)
    score = _sigmoid(logits)
    sel = score + rb_ref[...]

    def row(a, r):
        return a[r:r + 1, :]

    gscore = []
    for g in range(N_GROUPS):
        a, b, c, e = (row(sel, EPG * g + i) for i in range(EPG))
        hi_ab, lo_ab, hi_ce, lo_ce = jnp.maximum(a, b), jnp.minimum(a, b), jnp.maximum(c, e), jnp.minimum(c, e)
        top1 = jnp.maximum(hi_ab, hi_ce)
        top2 = jnp.maximum(jnp.maximum(lo_ab, lo_ce), jnp.minimum(hi_ab, hi_ce))
        gscore.append(top1 + top2)
    best = jnp.zeros((1, rows), I32)
    bestv = gscore[0]
    for g in range(1, N_GROUPS):
        upd = gscore[g] > bestv
        best = jnp.where(upd, g, best)
        bestv = jnp.where(upd, gscore[g], bestv)

    def pick(a, i):
        out = row(a, i)
        for g in range(1, N_GROUPS):
            out = jnp.where(best == g, row(a, EPG * g + i), out)
        return out

    vsel = [pick(sel, i) for i in range(EPG)]
    vsc = [pick(score, i) for i in range(EPG)]
    i1 = jnp.zeros((1, rows), I32)
    v1 = vsel[0]
    for i in range(1, EPG):
        upd = vsel[i] > v1
        i1 = jnp.where(upd, i, i1)
        v1 = jnp.where(upd, vsel[i], v1)
    i2 = jnp.full((1, rows), -1, I32)
    v2 = jnp.full((1, rows), -jnp.inf, F32)
    for i in range(EPG):
        cand = jnp.where(i1 == i, -jnp.inf, vsel[i])
        upd = cand > v2
        i2 = jnp.where(upd, i, i2)
        v2 = jnp.where(upd, cand, v2)
    s1 = jnp.zeros((1, rows), F32)
    s2 = jnp.zeros((1, rows), F32)
    for i in range(EPG):
        s1 = jnp.where(i1 == i, vsc[i], s1)
        s2 = jnp.where(i2 == i, vsc[i], s2)
    den = s1 + s2
    w1 = s1 / den
    w2 = s2 / den
    first_lo = i1 < i2
    lo = jnp.minimum(i1, i2)
    hi = jnp.maximum(i1, i2)
    wlo = jnp.where(first_lo, w1, w2)
    whi = jnp.where(first_lo, w2, w1)
    pair = jnp.where(lo == 0, hi - 1, jnp.where(lo == 1, 6 - hi, 5))
    bkt_ref[...] = (best * 6 + pair).reshape(1, 1, rows)
    wa = jnp.where(pair == 5, whi, wlo)
    wb = jnp.where(pair == 5, wlo, whi)

    aux = jnp.concatenate([wa, wb, jnp.zeros((LANES - 2, rows), F32)], axis=0)
    hx_ref[:, d:d + LANES] = aux.T


def _out_route_call(mixed, wout, h, g1, gain, sc, sh, wrt, rb, hx_buf, row_block_off, *, bb, tt, name):
    b, t, m = mixed.shape
    d = h.shape[-1]
    rows = bb * tt
    nti = t // tt
    ntot = hx_buf.shape[0]
    xspec = pl.BlockSpec((bb, tt, d), lambda bi, i: (bi, i, 0))
    mspec = pl.BlockSpec((bb, 1, d), lambda bi, i: (bi, 0, 0))
    return pl.pallas_call(
        _out_route_kernel,
        grid=(b // bb, nti),
        in_specs=[pl.BlockSpec((bb, tt, m), lambda bi, i: (bi, i, 0)),
                  pl.BlockSpec((m, d), lambda bi, i: (0, 0)),
                  xspec, mspec,
                  pl.BlockSpec((1, 1, d), lambda bi, i: (0, 0, 0)), mspec, mspec,
                  pl.BlockSpec((N_EXPERTS, d), lambda bi, i: (0, 0)),
                  pl.BlockSpec((N_EXPERTS, 1), lambda bi, i: (0, 0)),
                  pl.BlockSpec(memory_space=pl.ANY)],
        out_specs=[xspec,
                   pl.BlockSpec((rows, HX_W), lambda bi, i: (row_block_off + bi * nti + i, 0)),
                   pl.BlockSpec((1, 1, rows), lambda bi, i: (bi * nti + i, 0, 0))],
        out_shape=[jax.ShapeDtypeStruct((b, t, d), F32),
                   jax.ShapeDtypeStruct((ntot, HX_W), F32),
                   jax.ShapeDtypeStruct(((b // bb) * nti, 1, rows), I32)],
        input_output_aliases={9: 1},
        compiler_params=_cparams(("parallel", "parallel")),
        name=name,
    )(mixed, wout, h, g1, gain.reshape(1, 1, d), sc, sh, wrt, rb, hx_buf)


def _dispatch_kernel(pos_ref, hx_ref, xs_in, xs_out, sem):
    del xs_in
    i = pl.program_id(0)
    rows = hx_ref.shape[0]

    def row_copy(r):
        p = pos_ref[i * rows + r]
        return pltpu.make_async_copy(hx_ref.at[pl.ds(r, 1), :], xs_out.at[pl.ds(p, 1), :], sem)

    for r in range(rows):
        row_copy(r).start(priority=r % 2)
    for r in range(rows):
        row_copy(r).wait()


def _dispatch_call(pos, hx, xs_buf, rows):
    n = hx.shape[0]
    return pl.pallas_call(
        _dispatch_kernel,
        grid_spec=pltpu.PrefetchScalarGridSpec(
            num_scalar_prefetch=1, grid=(n // rows,),
            in_specs=[pl.BlockSpec((rows, HX_W), lambda i, ps: (i, 0)), pl.BlockSpec(memory_space=pl.ANY)],
            out_specs=pl.BlockSpec(memory_space=pl.ANY),
            scratch_shapes=[pltpu.SemaphoreType.DMA(())]),
        out_shape=jax.ShapeDtypeStruct(xs_buf.shape, F32),
        input_output_aliases={2: 0},
        compiler_params=_cparams(("arbitrary",)),
        name="dispatch",
    )(pos, hx, xs_buf)


def _moe_kernel(ea_ref, eb_ref, chg_ref, valid_ref, xs_ref, wga_ref, wgb_ref, wua_ref, wub_ref, wda_ref, wdb_ref,
                ys_ref, wg_s, wu_s, wd_s):
    t = pl.program_id(0)

    def recast(slot, wg_ref, wu_ref, wd_ref):
        rc = 128
        for i in range(D_MODEL // rc):
            wg_s[slot, i * rc:(i + 1) * rc, :] = wg_ref[0, 0, i * rc:(i + 1) * rc, :].astype(BF16)
            wu_s[slot, i * rc:(i + 1) * rc, :] = wu_ref[0, 0, i * rc:(i + 1) * rc, :].astype(BF16)
        for i in range(D_FF // rc):
            wd_s[slot, i * rc:(i + 1) * rc, :] = wd_ref[0, 0, i * rc:(i + 1) * rc, :].astype(BF16)

    @pl.when(chg_ref[0, t] > 0)
    def _():
        recast(0, wga_ref, wua_ref, wda_ref)

    @pl.when(chg_ref[1, t] > 0)
    def _():
        recast(1, wgb_ref, wub_ref, wdb_ref)

    @pl.when(valid_ref[t] > 0)
    def _():
        x = xs_ref[:, 0:D_MODEL].astype(BF16)
        w_a = xs_ref[:, D_MODEL:D_MODEL + 1]
        w_b = xs_ref[:, D_MODEL + 1:D_MODEL + 2]
        act_a = (_silu(_dot(x, wg_s[0])) * _dot(x, wu_s[0]) * w_a).astype(BF16)
        act_b = (_silu(_dot(x, wg_s[1])) * _dot(x, wu_s[1]) * w_b).astype(BF16)
        ys_ref[...] = _dot(act_a, wd_s[0]) + _dot(act_b, wd_s[1])

    @pl.when(valid_ref[t] == 0)
    def _():
        ys_ref[...] = jnp.zeros_like(ys_ref)


def _moe_call(ea, eb, chg, valid, xs, wg, wu, wd, layer):
    p = xs.shape[0]
    n_tiles = p // MOE_TM
    wa_in = pl.BlockSpec((1, 1, D_MODEL, D_FF), lambda t, ea, eb, ch, sr: (layer, ea[t], 0, 0))
    wb_in = pl.BlockSpec((1, 1, D_MODEL, D_FF), lambda t, ea, eb, ch, sr: (layer, eb[t], 0, 0))
    wa_out = pl.BlockSpec((1, 1, D_FF, D_MODEL), lambda t, ea, eb, ch, sr: (layer, ea[t], 0, 0))
    wb_out = pl.BlockSpec((1, 1, D_FF, D_MODEL), lambda t, ea, eb, ch, sr: (layer, eb[t], 0, 0))
    return pl.pallas_call(
        _moe_kernel,
        grid_spec=pltpu.PrefetchScalarGridSpec(
            num_scalar_prefetch=4, grid=(n_tiles,),
            in_specs=[pl.BlockSpec((MOE_TM, HX_W), lambda t, ea, eb, ch, sr: (t, 0)),
                      wa_in, wb_in, wa_in, wb_in, wa_out, wb_out],
            out_specs=pl.BlockSpec((MOE_TM, D_MODEL), lambda t, ea, eb, ch, sr: (t, 0)),
            scratch_shapes=[pltpu.VMEM((2, D_MODEL, D_FF), BF16), pltpu.VMEM((2, D_MODEL, D_FF), BF16),
                            pltpu.VMEM((2, D_FF, D_MODEL), BF16)]),
        out_shape=jax.ShapeDtypeStruct((p, D_MODEL), F32),
        compiler_params=_cparams(("arbitrary",)),
        name="moe",
    )(ea, eb, chg, valid, xs, wg, wg, wu, wu, wd, wd)


_PAIRS = [(0, 1), (0, 2), (0, 3), (1, 3), (1, 2), (3, 2)]
_EA_TABLE = np.array([g * EPG + p[0] for g in range(N_GROUPS) for p in _PAIRS], np.int32)
_EB_TABLE = np.array([g * EPG + p[1] for g in range(N_GROUPS) for p in _PAIRS], np.int32)


def _moe(hx, bkt, xs_buf, wg, wu, wd, layer):
    n = hx.shape[0]
    tm = MOE_TM
    ar = jnp.arange(N_BUCKETS, dtype=I32)
    blk = LANES
    nb = -(-n // blk)
    bkt_p = jnp.pad(bkt, (0, nb * blk - n), constant_values=N_BUCKETS)
    oh = (ar[:, None] == bkt_p[None, :]).astype(F32)
    triu = jnp.triu(jnp.ones((blk, blk), F32))
    inner = jnp.einsum('kbj,ji->kbi', oh.reshape(N_BUCKETS, nb, blk), triu)
    tot = inner[:, :, -1]
    outer = jnp.cumsum(tot, axis=1) - tot
    cs = (inner + outer[:, :, None]).reshape(N_BUCKETS, nb * blk)
    counts = (outer[:, -1] + tot[:, -1]).astype(I32)
    padded = ((counts + tm - 1) // tm) * tm
    ends = jnp.cumsum(padded)
    offs = ends - padded
    pos = jnp.sum(oh * (cs - 1.0 + offs.astype(F32)[:, None]), axis=0).astype(I32)[:n]
    n_tiles = -(-n // tm) + N_BUCKETS
    p = n_tiles * tm
    if xs_buf is None:
        xs_buf = jnp.zeros((p, HX_W), F32)
    tstart = jnp.arange(n_tiles, dtype=I32) * tm
    tb = jnp.sum((ends[None, :] <= tstart[:, None]).astype(I32), axis=1)
    n_used = ends[-1] // tm
    valid = (jnp.arange(n_tiles, dtype=I32) < n_used).astype(I32)
    last_b = tb[jnp.maximum(n_used - 1, 0)]
    tb = jnp.clip(jnp.where(valid > 0, tb, last_b), 0, N_BUCKETS - 1)
    ea = jnp.asarray(_EA_TABLE)[tb]
    eb = jnp.asarray(_EB_TABLE)[tb]
    first = jnp.ones((1,), I32)
    chg = jnp.stack([jnp.concatenate([first, (ea[1:] != ea[:-1]).astype(I32)]),
                     jnp.concatenate([first, (eb[1:] != eb[:-1]).astype(I32)])])
    xs = _dispatch_call(pos, hx, xs_buf, tm)
    ys = _moe_call(ea, eb, chg, valid, xs, wg, wu, wd, layer)
    return jnp.take(ys, pos, axis=0), xs


def _final_kernel(h_ref, y_ref, g_ref, gain_ref, o_ref):
    bb, tt, d = h_ref.shape
    h = h_ref[...] + g_ref[...] * y_ref[...].reshape(bb, tt, d)
    o_ref[...] = _rms(h) * gain_ref[...]


def _final_call(h, y2d, off, g2, gain, *, bb, tt, name):
    b, t, d = h.shape
    nti = t // tt
    rows = bb * tt
    xspec = pl.BlockSpec((bb, tt, d), lambda bi, i: (bi, i, 0))
    return pl.pallas_call(
        _final_kernel,
        grid=(b // bb, nti),
        in_specs=[xspec, pl.BlockSpec((rows, d), lambda bi, i: (off + bi * nti + i, 0)),
                  pl.BlockSpec((bb, 1, d), lambda bi, i: (bi, 0, 0)),
                  pl.BlockSpec((1, 1, d), lambda bi, i: (0, 0, 0))],
        out_specs=xspec,
        out_shape=jax.ShapeDtypeStruct((b, t, d), F32),
        compiler_params=_cparams(("parallel", "parallel")),
        name=name,
    )(h, y2d, g2, gain.reshape(1, 1, d))


def _group_cfg(b, t):
    if t >= 512:
        return dict(bb=1, tt=512, L=CHUNK, CB=4, Lc=4 * CHUNK, CBc=1)
    return dict(bb=b, tt=t, L=min(CHUNK, t), CB=1, Lc=min(CHUNK, t), CBc=1)


def _rope_tables(pos0, t):
    half = DK_C // 2
    inv = jnp.power(ROPE_BASE, -jnp.linspace(0.0, 1.0, half, dtype=F32))
    ang = (pos0 + jnp.arange(t, dtype=F32))[:, None] * inv[None, :]
    return jnp.cos(ang), jnp.sin(ang)


def kernel(x_prompt, x_sample, c_prompt, c_sample, state_a_conv, state_a_rec, state_b_c, state_b_n, state_b_m,
           state_c_rec, w_mod, b_mod, norm_mix, norm_ffn, w_in_ab, conv_a, a_log, dt_bias, norm_a, gate_bias_b,
           norm_b, w_out_ab, w_in_c, w_out_c, w_router, router_bias, w_gate, w_up, w_down, norm_final):
    d = D_MODEL
    bp, tp, _ = x_prompt.shape
    bs, ts, _ = x_sample.shape
    n_ab, n_c = w_in_ab.shape[0], w_in_c.shape[0]

    mod_all = _mod_call(jnp.concatenate([c_prompt, c_sample], axis=0), w_mod, b_mod)

    o = np.cumsum((0,) + AB_SIZES)
    w_ab_main = jnp.concatenate([w_in_ab[:, :, o[0]:o[1]].astype(BF16), w_in_ab[:, :, o[3]:o[7]].astype(BF16),
                                 w_in_ab[:, :, o[8]:o[9]].astype(BF16)], axis=-1)
    w_ab_gate = jnp.concatenate([w_in_ab[:, :, o[1]:o[3]], w_in_ab[:, :, o[7]:o[8]]], axis=-1)
    w_ab_gate_pad = jnp.pad(w_ab_gate, ((0, 0), (0, 0), (0, LANES - N_GATE_COLS)))
    w_ab_gate_t = jnp.swapaxes(w_ab_gate, 1, 2)
    w_c = w_in_c.astype(BF16)
    w_out_ab_b = w_out_ab.astype(BF16)
    w_out_c_b = w_out_c.astype(BF16)
    wrt = w_router.T
    rb = router_bias.reshape(N_EXPERTS, 1)
    zeros4 = jnp.zeros((n_ab, H_A), F32)
    gpar_row0 = jnp.concatenate([a_log, jnp.zeros((n_ab, LANES - H_A), F32)], axis=1)
    gpar_row1 = jnp.concatenate([dt_bias, zeros4, gate_bias_b, jnp.zeros((n_ab, LANES - 16), F32)], axis=1)
    gpar = jnp.concatenate([gpar_row0[:, None], gpar_row1[:, None], jnp.zeros((n_ab, 6, LANES), F32)], axis=1)
    gpart = jnp.swapaxes(jnp.concatenate([gpar[:, :2, :N_GATE_COLS], jnp.zeros((n_ab, LANES - 2, N_GATE_COLS), F32)],
                                         axis=1), 1, 2)

    groups = []
    zf = lambda *s: jnp.zeros(s, F32)
    groups.append(dict(
        x=x_prompt, b=bp, t=tp, pos0=0.0, mod=mod_all[:, :bp], row0=0,
        conv=zf(n_ab, bp, CONV_W - 1, QKV_A), sa=zf(n_ab, bp, H_A, DK_A, DV_A), cb=zf(n_ab, bp, H_B, DQK_B, DV_B),
        nb=zf(n_ab, bp, H_B, DQK_B), mb=zf(n_ab, bp, H_B), sc=zf(n_c, bp, H_C, DK_C, DV_C)))
    groups.append(dict(
        x=x_sample, b=bs, t=ts, pos0=float(PAST_LEN), mod=mod_all[:, bp:], row0=bp * tp,
        conv=state_a_conv, sa=state_a_rec, cb=state_b_c, nb=state_b_n, mb=state_b_m, sc=state_c_rec))
    n_tot = bp * tp + bs * ts
    for g in groups:
        g.update(_group_cfg(g["b"], g["t"]))
        g["h"] = g["x"]
        g["prev"] = None
        g["new_ab"], g["new_c"] = [], []
        g["cos"], g["sin"] = _rope_tables(g["pos0"], g["t"])
        g["roff"] = g["row0"] // (g["bb"] * g["tt"])

    xs_buf = None
    for layer in range(DEPTH):
        li = layer // 2
        if layer == 0:
            hx_buf = jnp.zeros((n_tot, HX_W), F32)
        bkts = []
        for gi, g in enumerate(groups):
            b, t, bb, tt = g["b"], g["t"], g["bb"], g["tt"]
            mods = [m.reshape(b, 1, d) for m in jnp.split(g["mod"][layer], N_MOD, axis=-1)]
            sh1, sc1, g1, sh2, sc2, g2 = mods
            tag = f"l{layer}g{gi}"
            if layer % 2 == 0:
                outs = _ln_mm_call(g["h"], g["prev"], norm_mix[layer], sc1, sh1, w_ab_main[li],
                                   (w_ab_gate_pad[li], w_ab_gate_t[li]), bb=bb, tt=tt, tn=512, out_dtype=F32,
                                   name="ln_ab_" + tag)
                if g["prev"] is not None:
                    g["h"], outs = outs[0], outs[1:]
                proj, gates, gates_t = outs
                gates_t = gates_t.reshape(N_GATE_COLS, b, t).transpose(1, 0, 2)
                mixed, conv_n, sa_n, cb_n, nb_n, mb_n = _mixer_ab_call(
                    proj, gates, gates_t, g["conv"][li], g["sa"][li], g["cb"][li],
                    g["nb"][li].reshape(b, H_B, 1, DQK_B), g["mb"][li].reshape(b, H_B, 1, 1),
                    conv_a[li], gpar[li], gpart[li], norm_a[li].reshape(1, DV_A), norm_b[li].reshape(1, DV_B),
                    L=g["L"], CB=g["CB"], name="mixer_ab_" + tag)
                g["new_ab"].append((conv_n, sa_n, cb_n, nb_n.reshape(b, H_B, DQK_B), mb_n.reshape(b, H_B)))
                wout = w_out_ab_b[li]
            else:
                outs = _ln_mm_call(g["h"], g["prev"], norm_mix[layer], sc1, sh1, w_c[li], None,
                                   bb=bb, tt=tt, tn=512, out_dtype=BF16, name="ln_c_" + tag)
                if g["prev"] is not None:
                    g["h"], outs = outs[0], outs[1:]
                (proj,) = outs
                mixed, sc_n = _mixer_c_call(proj, g["cos"], g["sin"], g["sc"][li], L=g["Lc"], CB=g["CBc"],
                                            name="mixer_c_" + tag)
                g["new_c"].append(sc_n)
                wout = w_out_c_b[li]
            g["h"], hx_buf, bkt = _out_route_call(mixed, wout, g["h"], g1, norm_ffn[layer], sc2, sh2, wrt, rb,
                                                  hx_buf, g["roff"], bb=bb, tt=tt, name="out_route_" + tag)
            bkts.append(bkt.reshape(-1))
            g["g2"] = g2
        y_moe, xs_buf = _moe(hx_buf, jnp.concatenate(bkts), xs_buf, w_gate, w_up, w_down, layer)
        for g in groups:
            g["prev"] = (y_moe, g["roff"], g["g2"])

    outs = []
    for gi, g in enumerate(groups):
        y2d, off, g2 = g["prev"]
        y = _final_call(g["h"], y2d, off, g2, norm_final, bb=g["bb"], tt=g["tt"], name=f"final_g{gi}")
        ab = [jnp.stack(s) for s in zip(*g["new_ab"])]
        outs.append((y, ab[0], ab[1], ab[2], ab[3], ab[4], jnp.stack(g["new_c"])))
    p, s = outs
    return (p[0], s[0], p[1], p[2], p[3], p[4], p[5], p[6], s[1], s[2], s[3], s[4], s[5], s[6])
```

```python
import functools

import numpy as np
import jax
import jax.numpy as jnp
from jax import lax
from jax.experimental import pallas as pl
from jax.experimental.pallas import tpu as pltpu

F32 = jnp.float32
BF16 = jnp.bfloat16
I32 = jnp.int32

D_MODEL = 1024
DEPTH = 2
CHUNK = 64
H_A, DK_A, DV_A, CONV_W = 4, 128, 128, 4
QKV_A = H_A * (2 * DK_A + DV_A)
H_B, DQK_B, DV_B = 4, 64, 128
H_C, DK_C, DV_C = 4, 256, 512
ROPE_BASE = 10000.0
PAST_LEN = 4096
AB_SIZES = (QKV_A, H_A, H_A, H_A * DV_A, H_B * DQK_B, H_B * DQK_B, H_B * DV_B, 2 * H_B, H_B * DV_B)
MIX_AB = H_A * DV_A + H_B * DV_B
MIX_C = H_C * DV_C
IN_C = 2 * H_C * DK_C + 2 * H_C * DV_C
N_EXPERTS, N_GROUPS, EPG, D_FF = 16, 4, 4, 512
N_MOD = 6
EPS = 1e-6

LANES = 128
AB_MAIN = 3584
N_GATE_COLS = 16
HX_W = D_MODEL + LANES
N_BUCKETS = N_GROUPS * 6
MOE_TM = 256
VMEM_LIMIT = 48 * 1024 * 1024

NN = ((1,), (0,))
NT = ((1,), (1,))
TN = ((0,), (0,))


def _dot(a, b, dims=NN):
    return lax.dot_general(a, b, (dims, ((), ())), preferred_element_type=F32)


def _mmb(a, b, dims=NN):
    return _dot(a.astype(BF16), b.astype(BF16), dims)


def _split2(x):
    hi = x.astype(BF16)
    lo = (x - hi.astype(F32)).astype(BF16)
    return hi, lo


def _split3(x):
    hi = x.astype(BF16)
    r = x - hi.astype(F32)
    mid = r.astype(BF16)
    lo = (r - mid.astype(F32)).astype(BF16)
    return hi, mid, lo


def _mm3(a, b, dims=NN):
    ah, al = _split2(a)
    bh, bl = _split2(b)
    return _dot(ah, bh, dims) + (_dot(ah, bl, dims) + _dot(al, bh, dims))


def _mm_mask_l(mask_bf16, x, dims=NN):
    h, m, l = _split3(x)
    return _dot(mask_bf16, h, dims) + (_dot(mask_bf16, m, dims) + _dot(mask_bf16, l, dims))


def _mm_mask_r(x, mask_bf16):
    h, m, l = _split3(x)
    return _dot(h, mask_bf16) + (_dot(m, mask_bf16) + _dot(l, mask_bf16))


def _sigmoid(x):
    return 1.0 / (1.0 + jnp.exp(-x))


def _silu(x):
    return x * _sigmoid(x)


def _softplus(x):
    return jnp.maximum(x, 0.0) + jnp.log(1.0 + jnp.exp(-jnp.abs(x)))


def _rms(x, eps=EPS):
    return x * lax.rsqrt(jnp.mean(x * x, axis=-1, keepdims=True) + eps)


def _cparams(sem):
    return pltpu.CompilerParams(dimension_semantics=sem, vmem_limit_bytes=VMEM_LIMIT)


def _mod_kernel(c_ref, w_ref, b_ref, o_ref):
    c = c_ref[...]
    o_ref[0] = _mm3(_silu(c), w_ref[0]) + b_ref[0]


def _mod_call(c_all, w_mod, b_mod):
    bt = c_all.shape[0]
    e = w_mod.shape[-1]
    tn = 1024
    return pl.pallas_call(
        _mod_kernel,
        grid=(DEPTH, e // tn),
        in_specs=[pl.BlockSpec((bt, D_MODEL), lambda l, j: (0, 0)),
                  pl.BlockSpec((1, D_MODEL, tn), lambda l, j: (l, 0, j)),
                  pl.BlockSpec((1, 1, tn), lambda l, j: (l, 0, j))],
        out_specs=pl.BlockSpec((1, bt, tn), lambda l, j: (l, 0, j)),
        out_shape=jax.ShapeDtypeStruct((DEPTH, bt, e), F32),
        compiler_params=_cparams(("parallel", "parallel")),
        name="mod",
    )(c_all, w_mod, b_mod.reshape(DEPTH, 1, e))


def _ln_mm_kernel(*refs, has_prev, has_gates, tn):
    it = iter(refs)
    x_ref = next(it)
    if has_prev:
        yp_ref, gp_ref = next(it), next(it)
    gain_ref, sc_ref, sh_ref, w_ref = next(it), next(it), next(it), next(it)
    if has_gates:
        wg_ref, wgt_ref = next(it), next(it)
    if has_prev:
        h_ref = next(it)
    proj_ref = next(it)
    if has_gates:
        g_ref, gt_ref = next(it), next(it)
    hn_s = next(it)

    bb, tt, d = x_ref.shape
    rows = bb * tt
    x = x_ref[...]
    if has_prev:
        x = x + gp_ref[...] * yp_ref[...].reshape(bb, tt, d)
        h_ref[...] = x
    hn = _rms(x) * gain_ref[...]
    hn = hn * (1.0 + sc_ref[...]) + sh_ref[...]
    hn2 = hn.reshape(rows, d)
    hn_s[...] = hn2.astype(BF16)
    if has_gates:
        g_ref[...] = _mm3(hn2, wg_ref[...]).reshape(bb, tt, LANES)
        gt_ref[...] = _mm3(wgt_ref[...], hn2, NT)
    for j in range(proj_ref.shape[-1] // tn):
        cols = slice(j * tn, (j + 1) * tn)
        proj_ref[:, :, cols] = _dot(hn_s[...], w_ref[:, cols]).reshape(bb, tt, tn).astype(proj_ref.dtype)


def _ln_mm_call(x, prev, gain, sc, sh, w, gates_w, *, bb, tt, tn, out_dtype, name):
    b, t, d = x.shape
    e = w.shape[1]
    rows = bb * tt
    nti = t // tt
    grid = (b // bb, nti)
    has_prev = prev is not None
    has_gates = gates_w is not None
    xspec = pl.BlockSpec((bb, tt, d), lambda bi, i: (bi, i, 0))
    mspec = pl.BlockSpec((bb, 1, d), lambda bi, i: (bi, 0, 0))
    in_specs, args = [xspec], [x]
    if has_prev:
        y2d, off, gp = prev
        in_specs += [pl.BlockSpec((rows, d), lambda bi, i: (off + bi * nti + i, 0)), mspec]
        args += [y2d, gp]
    in_specs += [pl.BlockSpec((1, 1, d), lambda bi, i: (0, 0, 0)), mspec, mspec,
                 pl.BlockSpec((d, e), lambda bi, i: (0, 0), pipeline_mode=pl.Buffered(1))]
    args += [gain.reshape(1, 1, d), sc, sh, w]
    if has_gates:
        wg, wgt = gates_w
        in_specs += [pl.BlockSpec((d, LANES), lambda bi, i: (0, 0)),
                     pl.BlockSpec((N_GATE_COLS, d), lambda bi, i: (0, 0))]
        args += [wg, wgt]
    out_specs, out_shape = [], []
    if has_prev:
        out_specs.append(xspec)
        out_shape.append(jax.ShapeDtypeStruct((b, t, d), F32))
    out_specs.append(pl.BlockSpec((bb, tt, e), lambda bi, i: (bi, i, 0)))
    out_shape.append(jax.ShapeDtypeStruct((b, t, e), out_dtype))
    if has_gates:
        out_specs += [pl.BlockSpec((bb, tt, LANES), lambda bi, i: (bi, i, 0)),
                      pl.BlockSpec((N_GATE_COLS, rows), lambda bi, i: (0, bi * nti + i))]
        out_shape += [jax.ShapeDtypeStruct((b, t, LANES), F32),
                      jax.ShapeDtypeStruct((N_GATE_COLS, b * t), F32)]
    return pl.pallas_call(
        functools.partial(_ln_mm_kernel, has_prev=has_prev, has_gates=has_gates, tn=tn),
        grid=grid, in_specs=in_specs, out_specs=out_specs, out_shape=out_shape,
        scratch_shapes=[pltpu.VMEM((rows, d), BF16)],
        compiler_params=_cparams(("parallel", "parallel")),
        name=name,
    )(*args)


def _tri_inv_all(a_list, eye, length, mm):
    xs = [-a for a in a_list]
    ps = [eye + x for x in xs]
    n = 2
    while n < length:
        xs = [mm(x, x) for x in xs]
        ps = [p + mm(p, x) for p, x in zip(ps, xs)]
        n *= 2
    return ps


def _mixer_ab_kernel(qkv_ref, z_ref, qkb_ref, vb_ref, ob_ref, g_ref, gt_ref,
                     conv0_ref, sa0_ref, cb0_ref, nb0_ref, mb0_ref,
                     convw_ref, gpar_ref, gpart_ref, na_ref, nbn_ref,
                     mixed_ref, conv_o, sa_o, cb_o, nb_o, mb_o,
                     xbuf, *, L, CB):
    TB = L * CB

    @pl.when(pl.program_id(1) == 0)
    def _():
        xbuf[5:8, :] = conv0_ref[0]
        sa_o[...] = sa0_ref[...]
        cb_o[...] = cb0_ref[...]
        nb_o[...] = nb0_ref[...]
        mb_o[...] = mb0_ref[...]

    xbuf[8:8 + TB, :] = qkv_ref[0]
    conv_o[0] = xbuf[5 + TB:8 + TB, :]

    ii = lax.broadcasted_iota(I32, (L, L), 0)
    jj = lax.broadcasted_iota(I32, (L, L), 1)
    tri = ii >= jj
    strict = ii > jj
    eye = (ii == jj).astype(F32)
    tril_b = tri.astype(BF16)
    triu_b = (ii <= jj).astype(BF16)
    lane = lax.broadcasted_iota(I32, (L, LANES), 1)
    rowi = lax.broadcasted_iota(I32, (N_GATE_COLS, L), 0)
    gpar = gpar_ref[...]
    gpart = gpart_ref[...]
    nea = -jnp.exp(gpar[0:1, :])
    neat = -jnp.exp(gpart[:, 0:1])
    norm_a = na_ref[...]
    norm_b = nbn_ref[...]

    mm = _mmb
    mm_inv = _mm3

    def conv_tile(r0, col0):
        acc = None
        for j in range(CONV_W):
            term = convw_ref[j:j + 1, col0:col0 + DK_A] * xbuf[5 + j + r0:5 + j + r0 + L, col0:col0 + DK_A]
            acc = term if acc is None else acc + term
        return _silu(acc)

    chunks = range(CB)
    items = [(c, h) for c in chunks for h in range(H_A)]
    rs = [slice(c * L, (c + 1) * L) for c in chunks]

    pre = [g_ref[0, r, :] + gpar[1:2, :] for r in rs]
    csrc = [jnp.where(lane < 4, nea * _softplus(p), jnp.where(lane >= 12, -_softplus(-p), 0.0)) for p in pre]
    gcum = [_mm_mask_l(tril_b, jnp.where(lane < N_GATE_COLS, x, 0.0)) for x in csrc]
    pret = [gt_ref[0, :, r] + gpart[:, 1:2] for r in rs]
    csrct = [jnp.where(rowi < 4, neat * _softplus(p), jnp.where(rowi >= 12, -_softplus(-p), 0.0)) for p in pret]
    gcumt = [_mm_mask_r(x, triu_b) for x in csrct]
    beta_all = [_sigmoid(p) for p in pre]

    q = [conv_tile(c * L, h * DK_A) for c, h in items]
    k = [conv_tile(c * L, H_A * DK_A + h * DK_A) for c, h in items]
    v = [conv_tile(c * L, 2 * H_A * DK_A + h * DV_A) for c, h in items]
    q = [x * lax.rsqrt(jnp.sum(x * x, axis=-1, keepdims=True) + EPS) * (DK_A ** -0.5) for x in q]
    k = [x * lax.rsqrt(jnp.sum(x * x, axis=-1, keepdims=True) + EPS) for x in k]
    gc_c = [gcum[c][:, h:h + 1] for c, h in items]
    gc_r = [gcumt[c][h:h + 1, :] for c, h in items]
    decay = [jnp.where(tri, jnp.exp(jnp.where(tri, a - b, 0.0)), 0.0) for a, b in zip(gc_c, gc_r)]
    beta_c = [beta_all[c][:, 4 + h:5 + h] for c, h in items]
    kb = [x * b for x, b in zip(k, beta_c)]
    a_list = [jnp.where(strict, mm_inv(x, y, NT) * d, 0.0) for x, y, d in zip(kb, k, decay)]
    attn = [jnp.where(tri, mm(x, y, NT) * d, 0.0) for x, y, d in zip(q, k, decay)]
    eg = [jnp.exp(x) for x in gc_c]
    gl = [x[L - 1:L, :] for x in gc_c]
    rhs = [jnp.concatenate([x * e, y * b], axis=-1) for x, e, y, b in zip(kb, eg, v, beta_c)]
    qeg = [x * e for x, e in zip(q, eg)]
    kdec = [x * jnp.exp(g - gc) for x, g, gc in zip(k, gl, gc_c)]
    sdec = [jnp.exp(g) for g in gl]
    tinv = _tri_inv_all(a_list, eye, L, mm)
    sol = [mm(t, r) for t, r in zip(tinv, rhs)]
    resid = [r - (s + mm_inv(a, s)) for r, s, a in zip(rhs, sol, a_list)]
    sol = [s + mm(t, r) for s, t, r in zip(sol, tinv, resid)]

    heads = range(H_B)
    qb = [qkb_ref[0, rs[c], h * DQK_B:(h + 1) * DQK_B] * (DQK_B ** -0.5) for c, h in items]
    kbb = [qkb_ref[0, rs[c], H_B * DQK_B + h * DQK_B:H_B * DQK_B + (h + 1) * DQK_B] for c, h in items]
    b_c = [gcum[c][:, 12 + h:13 + h] for c, h in items]
    dm = [jnp.where(tri, gcum[c][:, 12 + h:13 + h] - gcumt[c][12 + h:13 + h, :] + pret[c][8 + h:9 + h, :], -jnp.inf)
          for c, h in items]
    dmax = [jnp.max(x, axis=-1, keepdims=True) for x in dm]
    qkm = [mm(x, y, NT) for x, y in zip(qb, kbb)]
    li_c = [pre[c][:, 8 + h:9 + h] for c, h in items]

    for c in chunks:
        ia = [c * H_A + h for h in heads]
        s = [sa_o[0, h] for h in heads]
        v_new = [sol[i][:, DK_A:] - mm(sol[i][:, :DK_A], s[h]) for h, i in enumerate(ia)]
        m_prev = [mb_o[0, h] for h in heads]
        a0 = [b_c[i] + m_prev[h] for h, i in enumerate(ia)]
        m_t = [jnp.maximum(a0[h], dmax[i]) for h, i in enumerate(ia)]
        w0 = [jnp.exp(a0[h] - m_t[h]) for h in heads]
        sm = [qkm[i] * jnp.exp(dm[i] - m_t[h]) for h, i in enumerate(ia)]
        cst = [cb_o[0, h] for h in heads]
        nst = [nb_o[0, h] for h in heads]
        vb = [vb_ref[0, rs[c], h * DV_B:(h + 1) * DV_B] for h in heads]
        o_a = [mm(qeg[i], s[h]) + mm(attn[i], v_new[h]) for h, i in enumerate(ia)]
        for h, i in enumerate(ia):
            sa_o[0, h] = s[h] * sdec[i] + mm(kdec[i], v_new[h], TN)
        num = [w0[h] * mm(qb[i], cst[h]) + mm(sm[h], vb[h]) for h, i in enumerate(ia)]
        den = [w0[h] * jnp.sum(qb[i] * nst[h], axis=-1, keepdims=True) + jnp.sum(sm[h], axis=-1, keepdims=True)
               for h, i in enumerate(ia)]
        hh = [num[h] / jnp.maximum(jnp.abs(den[h]), jnp.exp(-m_t[h])) for h in heads]
        m_new = [m_t[h][L - 1:L, :] for h in heads]
        kw = [kbb[i] * jnp.exp(b_c[i][L - 1:L, :] - b_c[i] + li_c[i] - m_new[h]) for h, i in enumerate(ia)]
        for h in heads:
            w0l = w0[h][L - 1:L, :]
            cb_o[0, h] = w0l * cst[h] + mm(kw[h], vb[h], TN)
            nb_o[0, h] = w0l * nst[h] + jnp.sum(kw[h], axis=0, keepdims=True)
            mb_o[0, h] = m_new[h]
        for h in heads:
            zg = z_ref[0, rs[c], h * DV_A:(h + 1) * DV_A]
            mixed_ref[0, rs[c], h * DV_A:(h + 1) * DV_A] = (_rms(o_a[h]) * norm_a * _silu(zg)).astype(mixed_ref.dtype)
            og = ob_ref[0, rs[c], h * DV_B:(h + 1) * DV_B]
            c0 = H_A * DV_A + h * DV_B
            mixed_ref[0, rs[c], c0:c0 + DV_B] = (_rms(hh[h]) * norm_b * _sigmoid(og)).astype(mixed_ref.dtype)

    xbuf[5:8, :] = xbuf[5 + TB:8 + TB, :]


def _mixer_ab_call(proj, gates, gates_t, conv0, sa0, cb0, nb0, mb0, convw, gpar, gpart, norm_a, norm_b, *, L, CB, name):
    b, t, _ = proj.shape
    tb = L * CB
    nt = t // tb

    def col(width, idx):
        return pl.BlockSpec((1, tb, width), lambda bi, i: (bi, i, idx))

    def const(shape):
        return pl.BlockSpec(shape, lambda bi, i: (0,) * len(shape))

    def state(shape):
        return pl.BlockSpec((1,) + shape, lambda bi, i: (bi,) + (0,) * len(shape))

    in_specs = [col(QKV_A, 0), col(512, 3), col(512, 4), col(512, 5), col(512, 6),
                col(LANES, 0), pl.BlockSpec((1, N_GATE_COLS, tb), lambda bi, i: (bi, 0, i)),
                state((CONV_W - 1, QKV_A)), state((H_A, DK_A, DV_A)), state((H_B, DQK_B, DV_B)),
                state((H_B, 1, DQK_B)), state((H_B, 1, 1)),
                const((CONV_W, QKV_A)), const((8, LANES)), const((N_GATE_COLS, LANES)),
                const((1, DV_A)), const((1, DV_B))]
    out_specs = [pl.BlockSpec((1, tb, MIX_AB), lambda bi, i: (bi, i, 0)),
                 state((CONV_W - 1, QKV_A)), state((H_A, DK_A, DV_A)), state((H_B, DQK_B, DV_B)),
                 state((H_B, 1, DQK_B)), state((H_B, 1, 1))]
    out_shape = [jax.ShapeDtypeStruct((b, t, MIX_AB), BF16),
                 jax.ShapeDtypeStruct((b, CONV_W - 1, QKV_A), F32),
                 jax.ShapeDtypeStruct((b, H_A, DK_A, DV_A), F32),
                 jax.ShapeDtypeStruct((b, H_B, DQK_B, DV_B), F32),
                 jax.ShapeDtypeStruct((b, H_B, 1, DQK_B), F32),
                 jax.ShapeDtypeStruct((b, H_B, 1, 1), F32)]
    return pl.pallas_call(
        functools.partial(_mixer_ab_kernel, L=L, CB=CB),
        grid=(b, nt), in_specs=in_specs, out_specs=out_specs, out_shape=out_shape,
        scratch_shapes=[pltpu.VMEM((8 + tb, QKV_A), F32)],
        compiler_params=_cparams(("parallel", "arbitrary")),
        name=name,
    )(proj, proj, proj, proj, proj, gates, gates_t, conv0, sa0, cb0, nb0, mb0, convw, gpar, gpart, norm_a, norm_b)


def _mixer_c_kernel(q_ref, k_ref, v_ref, g_ref, cos_ref, sin_ref, s0_ref, mixed_ref, s_o, *, L, CB):
    @pl.when(pl.program_id(1) == 0)
    def _():
        s_o[...] = s0_ref[...]

    ii = lax.broadcasted_iota(I32, (L, L), 0)
    jj = lax.broadcasted_iota(I32, (L, L), 1)
    rel = (ii - jj).astype(F32)
    idx = lax.broadcasted_iota(I32, (L, 1), 0).astype(F32)
    half = DK_C // 2

    def rope(x, cos, sin):
        x1, x2 = x[:, :half], x[:, half:]
        return jnp.concatenate([x1 * cos - x2 * sin, x1 * sin + x2 * cos], axis=-1)

    for c in range(CB):
        r0 = c * L
        cos = cos_ref[r0:r0 + L, :]
        sin = sin_ref[r0:r0 + L, :]
        for h in range(H_C):
            lg = float(np.log1p(-np.exp2(-5.0 - h)))
            intra = jnp.where(rel >= 0, jnp.exp(lg * jnp.maximum(rel, 0.0)), 0.0)
            q_decay = jnp.exp(lg * (idx + 1.0))
            k_decay = jnp.exp(lg * (L - 1.0 - idx))
            s_decay = float(np.exp(lg * L))
            q = rope(q_ref[0, r0:r0 + L, h * DK_C:(h + 1) * DK_C].astype(F32), cos, sin)
            k = rope(k_ref[0, r0:r0 + L, h * DK_C:(h + 1) * DK_C].astype(F32), cos, sin) * (DK_C ** -0.5)
            v = v_ref[0, r0:r0 + L, h * DV_C:(h + 1) * DV_C]
            s = s_o[0, h]
            o = _mmb(_mmb(q, k, NT) * intra, v) + _mmb(q * q_decay, s)
            s_o[0, h] = s_decay * s + _mmb(k * k_decay, v, TN)
            gg = g_ref[0, r0:r0 + L, h * DV_C:(h + 1) * DV_C].astype(F32)
            o = _rms(o) * _silu(gg)
            mixed_ref[0, r0:r0 + L, h * DV_C:(h + 1) * DV_C] = o.astype(mixed_ref.dtype)


def _mixer_c_call(proj, cos, sin, s0, *, L, CB, name):
    b, t, _ = proj.shape
    tb = L * CB
    nt = t // tb
    qk_w, v_w = H_C * DK_C, H_C * DV_C
    sspec = pl.BlockSpec((1, H_C, DK_C, DV_C), lambda bi, i: (bi, 0, 0, 0))
    tspec = pl.BlockSpec((tb, DK_C // 2), lambda bi, i: (i, 0))
    return pl.pallas_call(
        functools.partial(_mixer_c_kernel, L=L, CB=CB),
        grid=(b, nt),
        in_specs=[pl.BlockSpec((1, tb, qk_w), lambda bi, i: (bi, i, 0)),
                  pl.BlockSpec((1, tb, qk_w), lambda bi, i: (bi, i, 1)),
                  pl.BlockSpec((1, tb, v_w), lambda bi, i: (bi, i, 1)),
                  pl.BlockSpec((1, tb, v_w), lambda bi, i: (bi, i, 2)),
                  tspec, tspec, sspec],
        out_specs=[pl.BlockSpec((1, tb, MIX_C), lambda bi, i: (bi, i, 0)), sspec],
        out_shape=[jax.ShapeDtypeStruct((b, t, MIX_C), BF16),
                   jax.ShapeDtypeStruct((b, H_C, DK_C, DV_C), F32)],
        compiler_params=_cparams(("parallel", "arbitrary")),
        name=name,
    )(proj, proj, proj, proj, cos, sin, s0)


def _out_route_kernel(mixed_ref, wout_ref, h_ref, g1_ref, gain_ref, sc_ref, sh_ref, wrt_ref, rb_ref, hx_in_ref,
                      hnew_ref, hx_ref, bkt_ref):
    del hx_in_ref
    bb, tt, m = mixed_ref.shape
    d = h_ref.shape[-1]
    rows = bb * tt
    y = _dot(mixed_ref[...].reshape(rows, m), wout_ref[...])
    h = h_ref[...] + g1_ref[...] * y.reshape(bb, tt, d)
    hnew_ref[...] = h
    hn = _rms(h) * gain_ref[...]
    hn = (hn * (1.0 + sc_ref[...]) + sh_ref[...]).reshape(rows, d)
    hx_ref[:, 0:d] = hn

    logits = _mm3(wrt_ref[...], hn, NT)
    score = _sigmoid(logits)
    sel = score + rb_ref[...]

    def row(a, r):
        return a[r:r + 1, :]

    gscore = []
    for g in range(N_GROUPS):
        a, b, c, e = (row(sel, EPG * g + i) for i in range(EPG))
        hi_ab, lo_ab, hi_ce, lo_ce = jnp.maximum(a, b), jnp.minimum(a, b), jnp.maximum(c, e), jnp.minimum(c, e)
        top1 = jnp.maximum(hi_ab, hi_ce)
        top2 = jnp.maximum(jnp.maximum(lo_ab, lo_ce), jnp.minimum(hi_ab, hi_ce))
        gscore.append(top1 + top2)
    best = jnp.zeros((1, rows), I32)
    bestv = gscore[0]
    for g in range(1, N_GROUPS):
        upd = gscore[g] > bestv
        best = jnp.where(upd, g, best)
        bestv = jnp.where(upd, gscore[g], bestv)

    def pick(a, i):
        out = row(a, i)
        for g in range(1, N_GROUPS):
            out = jnp.where(best == g, row(a, EPG * g + i), out)
        return out

    vsel = [pick(sel, i) for i in range(EPG)]
    vsc = [pick(score, i) for i in range(EPG)]
    i1 = jnp.zeros((1, rows), I32)
    v1 = vsel[0]
    for i in range(1, EPG):
        upd = vsel[i] > v1
        i1 = jnp.where(upd, i, i1)
        v1 = jnp.where(upd, vsel[i], v1)
    i2 = jnp.full((1, rows), -1, I32)
    v2 = jnp.full((1, rows), -jnp.inf, F32)
    for i in range(EPG):
        cand = jnp.where(i1 == i, -jnp.inf, vsel[i])
        upd = cand > v2
        i2 = jnp.where(upd, i, i2)
        v2 = jnp.where(upd, cand, v2)
    s1 = jnp.zeros((1, rows), F32)
    s2 = jnp.zeros((1, rows), F32)
    for i in range(EPG):
        s1 = jnp.where(i1 == i, vsc[i], s1)
        s2 = jnp.where(i2 == i, vsc[i], s2)
    den = s1 + s2
    w1 = s1 / den
    w2 = s2 / den
    first_lo = i1 < i2
    lo = jnp.minimum(i1, i2)
    hi = jnp.maximum(i1, i2)
    wlo = jnp.where(first_lo, w1, w2)
    whi = jnp.where(first_lo, w2, w1)
    pair = jnp.where(lo == 0, hi - 1, jnp.where(lo == 1, 6 - hi, 5))
    bkt_ref[...] = (best * 6 + pair).reshape(1, 1, rows)
    wa = jnp.where(pair == 5, whi, wlo)
    wb = jnp.where(pair == 5, wlo, whi)

    aux = jnp.concatenate([wa, wb, jnp.zeros((LANES - 2, rows), F32)], axis=0)
    hx_ref[:, d:d + LANES] = aux.T


def _out_route_call(mixed, wout, h, g1, gain, sc, sh, wrt, rb, hx_buf, row_block_off, *, bb, tt, name):
    b, t, m = mixed.shape
    d = h.shape[-1]
    rows = bb * tt
    nti = t // tt
    ntot = hx_buf.shape[0]
    xspec = pl.BlockSpec((bb, tt, d), lambda bi, i: (bi, i, 0))
    mspec = pl.BlockSpec((bb, 1, d), lambda bi, i: (bi, 0, 0))
    return pl.pallas_call(
        _out_route_kernel,
        grid=(b // bb, nti),
        in_specs=[pl.BlockSpec((bb, tt, m), lambda bi, i: (bi, i, 0)),
                  pl.BlockSpec((m, d), lambda bi, i: (0, 0)),
                  xspec, mspec,
                  pl.BlockSpec((1, 1, d), lambda bi, i: (0, 0, 0)), mspec, mspec,
                  pl.BlockSpec((N_EXPERTS, d), lambda bi, i: (0, 0)),
                  pl.BlockSpec((N_EXPERTS, 1), lambda bi, i: (0, 0)),
                  pl.BlockSpec(memory_space=pl.ANY)],
        out_specs=[xspec,
                   pl.BlockSpec((rows, HX_W), lambda bi, i: (row_block_off + bi * nti + i, 0)),
                   pl.BlockSpec((1, 1, rows), lambda bi, i: (bi * nti + i, 0, 0))],
        out_shape=[jax.ShapeDtypeStruct((b, t, d), F32),
                   jax.ShapeDtypeStruct((ntot, HX_W), F32),
                   jax.ShapeDtypeStruct(((b // bb) * nti, 1, rows), I32)],
        input_output_aliases={9: 1},
        compiler_params=_cparams(("parallel", "parallel")),
        name=name,
    )(mixed, wout, h, g1, gain.reshape(1, 1, d), sc, sh, wrt, rb, hx_buf)


def _dispatch_kernel(pos_ref, hx_ref, xs_in, xs_out, sem):
    del xs_in
    i = pl.program_id(0)
    rows = hx_ref.shape[0]

    def row_copy(r):
        p = pos_ref[i * rows + r]
        return pltpu.make_async_copy(hx_ref.at[pl.ds(r, 1), :], xs_out.at[pl.ds(p, 1), :], sem)

    for r in range(rows):
        row_copy(r).start(priority=r % 2)
    for r in range(rows):
        row_copy(r).wait()


def _dispatch_call(pos, hx, xs_buf, rows):
    n = hx.shape[0]
    return pl.pallas_call(
        _dispatch_kernel,
        grid_spec=pltpu.PrefetchScalarGridSpec(
            num_scalar_prefetch=1, grid=(n // rows,),
            in_specs=[pl.BlockSpec((rows, HX_W), lambda i, ps: (i, 0)), pl.BlockSpec(memory_space=pl.ANY)],
            out_specs=pl.BlockSpec(memory_space=pl.ANY),
            scratch_shapes=[pltpu.SemaphoreType.DMA(())]),
        out_shape=jax.ShapeDtypeStruct(xs_buf.shape, F32),
        input_output_aliases={2: 0},
        compiler_params=_cparams(("arbitrary",)),
        name="dispatch",
    )(pos, hx, xs_buf)


def _moe_kernel(exp_ref, chg_ref, nxt_ref, valid_ref, xs_ref, wg_hbm, wu_hbm, wd_hbm,
                ys_ref, wg_f, wu_f, wd_f, wg_s, wu_s, wd_s, sem, *, layer):
    t = pl.program_id(0)

    def fetch(slot, e):
        return (pltpu.make_async_copy(wg_hbm.at[layer, e], wg_f.at[slot], sem.at[slot]),
                pltpu.make_async_copy(wu_hbm.at[layer, e], wu_f.at[slot], sem.at[slot]),
                pltpu.make_async_copy(wd_hbm.at[layer, e], wd_f.at[slot], sem.at[slot]))

    def recast(slot):
        rc = 128
        for i in range(D_MODEL // rc):
            wg_s[slot, i * rc:(i + 1) * rc, :] = wg_f[slot, i * rc:(i + 1) * rc, :].astype(BF16)
            wu_s[slot, i * rc:(i + 1) * rc, :] = wu_f[slot, i * rc:(i + 1) * rc, :].astype(BF16)
        for i in range(D_FF // rc):
            wd_s[slot, i * rc:(i + 1) * rc, :] = wd_f[slot, i * rc:(i + 1) * rc, :].astype(BF16)

    for slot in range(2):
        @pl.when(t == 0)
        def _():
            for cp in fetch(slot, exp_ref[slot, 0]):
                cp.start()

        @pl.when(chg_ref[slot, t] > 0)
        def _():
            for cp in fetch(slot, exp_ref[slot, t]):
                cp.wait()
            recast(slot)

            @pl.when(nxt_ref[slot, t] >= 0)
            def _():
                for cp in fetch(slot, nxt_ref[slot, t]):
                    cp.start()

    @pl.when(valid_ref[t] > 0)
    def _():
        x = xs_ref[:, 0:D_MODEL].astype(BF16)
        w_a = xs_ref[:, D_MODEL:D_MODEL + 1]
        w_b = xs_ref[:, D_MODEL + 1:D_MODEL + 2]
        act_a = (_silu(_dot(x, wg_s[0])) * _dot(x, wu_s[0]) * w_a).astype(BF16)
        act_b = (_silu(_dot(x, wg_s[1])) * _dot(x, wu_s[1]) * w_b).astype(BF16)
        ys_ref[...] = _dot(act_a, wd_s[0]) + _dot(act_b, wd_s[1])

    @pl.when(valid_ref[t] == 0)
    def _():
        ys_ref[...] = jnp.zeros_like(ys_ref)


def _moe_call(experts, chg, nxt, valid, xs, wg, wu, wd, layer):
    p = xs.shape[0]
    n_tiles = p // MOE_TM
    return pl.pallas_call(
        functools.partial(_moe_kernel, layer=layer),
        grid_spec=pltpu.PrefetchScalarGridSpec(
            num_scalar_prefetch=4, grid=(n_tiles,),
            in_specs=[pl.BlockSpec((MOE_TM, HX_W), lambda t, ex, ch, nx, va: (t, 0)),
                      pl.BlockSpec(memory_space=pl.ANY), pl.BlockSpec(memory_space=pl.ANY),
                      pl.BlockSpec(memory_space=pl.ANY)],
            out_specs=pl.BlockSpec((MOE_TM, D_MODEL), lambda t, ex, ch, nx, va: (t, 0)),
            scratch_shapes=[pltpu.VMEM((2, D_MODEL, D_FF), F32), pltpu.VMEM((2, D_MODEL, D_FF), F32),
                            pltpu.VMEM((2, D_FF, D_MODEL), F32),
                            pltpu.VMEM((2, D_MODEL, D_FF), BF16), pltpu.VMEM((2, D_MODEL, D_FF), BF16),
                            pltpu.VMEM((2, D_FF, D_MODEL), BF16),
                            pltpu.SemaphoreType.DMA((2,))]),
        out_shape=jax.ShapeDtypeStruct((p, D_MODEL), F32),
        compiler_params=_cparams(("arbitrary",)),
        name="moe",
    )(experts, chg, nxt, valid, xs, wg, wu, wd)


_PAIRS = [(0, 1), (0, 2), (0, 3), (1, 3), (1, 2), (3, 2)]
_EA_TABLE = np.array([g * EPG + p[0] for g in range(N_GROUPS) for p in _PAIRS], np.int32)
_EB_TABLE = np.array([g * EPG + p[1] for g in range(N_GROUPS) for p in _PAIRS], np.int32)


def _moe(hx, bkt, xs_buf, wg, wu, wd, layer):
    n = hx.shape[0]
    tm = MOE_TM
    ar = jnp.arange(N_BUCKETS, dtype=I32)
    blk = LANES
    nb = -(-n // blk)
    bkt_p = jnp.pad(bkt, (0, nb * blk - n), constant_values=N_BUCKETS)
    oh = (ar[:, None] == bkt_p[None, :]).astype(F32)
    triu = jnp.triu(jnp.ones((blk, blk), F32))
    inner = jnp.einsum('kbj,ji->kbi', oh.reshape(N_BUCKETS, nb, blk), triu)
    tot = inner[:, :, -1]
    outer = jnp.cumsum(tot, axis=1) - tot
    cs = (inner + outer[:, :, None]).reshape(N_BUCKETS, nb * blk)
    counts = (outer[:, -1] + tot[:, -1]).astype(I32)
    padded = ((counts + tm - 1) // tm) * tm
    ends = jnp.cumsum(padded)
    offs = ends - padded
    pos = jnp.sum(oh * (cs - 1.0 + offs.astype(F32)[:, None]), axis=0).astype(I32)[:n]
    n_tiles = -(-n // tm) + N_BUCKETS
    p = n_tiles * tm
    if xs_buf is None:
        xs_buf = jnp.zeros((p, HX_W), F32)
    tstart = jnp.arange(n_tiles, dtype=I32) * tm
    tb = jnp.sum((ends[None, :] <= tstart[:, None]).astype(I32), axis=1)
    n_used = ends[-1] // tm
    valid = (jnp.arange(n_tiles, dtype=I32) < n_used).astype(I32)
    last_b = tb[jnp.maximum(n_used - 1, 0)]
    tb = jnp.clip(jnp.where(valid > 0, tb, last_b), 0, N_BUCKETS - 1)
    ea = jnp.asarray(_EA_TABLE)[tb]
    eb = jnp.asarray(_EB_TABLE)[tb]
    experts = jnp.stack([ea, eb])
    chg = jnp.concatenate([jnp.ones((2, 1), I32), (experts[:, 1:] != experts[:, :-1]).astype(I32)], axis=1)
    tidx = jnp.arange(n_tiles, dtype=I32)
    chg_at = jnp.where(chg > 0, tidx[None, :], n_tiles)
    nxt_at = jnp.concatenate([lax.cummin(chg_at, axis=1, reverse=True)[:, 1:], jnp.full((2, 1), n_tiles, I32)], axis=1)
    nxt = jnp.where(nxt_at < n_tiles,
                    jnp.take_along_axis(experts, jnp.minimum(nxt_at, n_tiles - 1), axis=1), -1)
    xs = _dispatch_call(pos, hx, xs_buf, tm)
    ys = _moe_call(experts, chg, nxt, valid, xs, wg, wu, wd, layer)
    return ys.at[pos].get(mode="promise_in_bounds", unique_indices=True), xs


def _final_kernel(h_ref, y_ref, g_ref, gain_ref, o_ref):
    bb, tt, d = h_ref.shape
    h = h_ref[...] + g_ref[...] * y_ref[...].reshape(bb, tt, d)
    o_ref[...] = _rms(h) * gain_ref[...]


def _final_call(h, y2d, off, g2, gain, *, bb, tt, name):
    b, t, d = h.shape
    nti = t // tt
    rows = bb * tt
    xspec = pl.BlockSpec((bb, tt, d), lambda bi, i: (bi, i, 0))
    return pl.pallas_call(
        _final_kernel,
        grid=(b // bb, nti),
        in_specs=[xspec, pl.BlockSpec((rows, d), lambda bi, i: (off + bi * nti + i, 0)),
                  pl.BlockSpec((bb, 1, d), lambda bi, i: (bi, 0, 0)),
                  pl.BlockSpec((1, 1, d), lambda bi, i: (0, 0, 0))],
        out_specs=xspec,
        out_shape=jax.ShapeDtypeStruct((b, t, d), F32),
        compiler_params=_cparams(("parallel", "parallel")),
        name=name,
    )(h, y2d, g2, gain.reshape(1, 1, d))


def _group_cfg(b, t):
    if t >= 512:
        return dict(bb=1, tt=512, L=CHUNK, CB=4, Lc=4 * CHUNK, CBc=1)
    return dict(bb=b, tt=t, L=min(CHUNK, t), CB=1, Lc=min(CHUNK, t), CBc=1)


def _rope_tables(pos0, t):
    half = DK_C // 2
    inv = jnp.power(ROPE_BASE, -jnp.linspace(0.0, 1.0, half, dtype=F32))
    ang = (pos0 + jnp.arange(t, dtype=F32))[:, None] * inv[None, :]
    return jnp.cos(ang), jnp.sin(ang)


def kernel(x_prompt, x_sample, c_prompt, c_sample, state_a_conv, state_a_rec, state_b_c, state_b_n, state_b_m,
           state_c_rec, w_mod, b_mod, norm_mix, norm_ffn, w_in_ab, conv_a, a_log, dt_bias, norm_a, gate_bias_b,
           norm_b, w_out_ab, w_in_c, w_out_c, w_router, router_bias, w_gate, w_up, w_down, norm_final):
    d = D_MODEL
    bp, tp, _ = x_prompt.shape
    bs, ts, _ = x_sample.shape
    n_ab, n_c = w_in_ab.shape[0], w_in_c.shape[0]

    mod_all = _mod_call(jnp.concatenate([c_prompt, c_sample], axis=0), w_mod, b_mod)

    o = np.cumsum((0,) + AB_SIZES)
    w_ab_main = jnp.concatenate([w_in_ab[:, :, o[0]:o[1]].astype(BF16), w_in_ab[:, :, o[3]:o[7]].astype(BF16),
                                 w_in_ab[:, :, o[8]:o[9]].astype(BF16)], axis=-1)
    w_ab_gate = jnp.concatenate([w_in_ab[:, :, o[1]:o[3]], w_in_ab[:, :, o[7]:o[8]]], axis=-1)
    w_ab_gate_pad = jnp.pad(w_ab_gate, ((0, 0), (0, 0), (0, LANES - N_GATE_COLS)))
    w_ab_gate_t = jnp.swapaxes(w_ab_gate, 1, 2)
    w_c = w_in_c.astype(BF16)
    w_out_ab_b = w_out_ab.astype(BF16)
    w_out_c_b = w_out_c.astype(BF16)
    wrt = w_router.T
    rb = router_bias.reshape(N_EXPERTS, 1)
    zeros4 = jnp.zeros((n_ab, H_A), F32)
    gpar_row0 = jnp.concatenate([a_log, jnp.zeros((n_ab, LANES - H_A), F32)], axis=1)
    gpar_row1 = jnp.concatenate([dt_bias, zeros4, gate_bias_b, jnp.zeros((n_ab, LANES - 16), F32)], axis=1)
    gpar = jnp.concatenate([gpar_row0[:, None], gpar_row1[:, None], jnp.zeros((n_ab, 6, LANES), F32)], axis=1)
    gpart = jnp.swapaxes(jnp.concatenate([gpar[:, :2, :N_GATE_COLS], jnp.zeros((n_ab, LANES - 2, N_GATE_COLS), F32)],
                                         axis=1), 1, 2)

    groups = []
    zf = lambda *s: jnp.zeros(s, F32)
    groups.append(dict(
        x=x_prompt, b=bp, t=tp, pos0=0.0, mod=mod_all[:, :bp], row0=0,
        conv=zf(n_ab, bp, CONV_W - 1, QKV_A), sa=zf(n_ab, bp, H_A, DK_A, DV_A), cb=zf(n_ab, bp, H_B, DQK_B, DV_B),
        nb=zf(n_ab, bp, H_B, DQK_B), mb=zf(n_ab, bp, H_B), sc=zf(n_c, bp, H_C, DK_C, DV_C)))
    groups.append(dict(
        x=x_sample, b=bs, t=ts, pos0=float(PAST_LEN), mod=mod_all[:, bp:], row0=bp * tp,
        conv=state_a_conv, sa=state_a_rec, cb=state_b_c, nb=state_b_n, mb=state_b_m, sc=state_c_rec))
    n_tot = bp * tp + bs * ts
    for g in groups:
        g.update(_group_cfg(g["b"], g["t"]))
        g["h"] = g["x"]
        g["prev"] = None
        g["new_ab"], g["new_c"] = [], []
        g["cos"], g["sin"] = _rope_tables(g["pos0"], g["t"])
        g["roff"] = g["row0"] // (g["bb"] * g["tt"])

    xs_buf = None
    for layer in range(DEPTH):
        li = layer // 2
        if layer == 0:
            hx_buf = jnp.zeros((n_tot, HX_W), F32)
        bkts = []
        for gi, g in enumerate(groups):
            b, t, bb, tt = g["b"], g["t"], g["bb"], g["tt"]
            mods = [m.reshape(b, 1, d) for m in jnp.split(g["mod"][layer], N_MOD, axis=-1)]
            sh1, sc1, g1, sh2, sc2, g2 = mods
            tag = f"l{layer}g{gi}"
            if layer % 2 == 0:
                outs = _ln_mm_call(g["h"], g["prev"], norm_mix[layer], sc1, sh1, w_ab_main[li],
                                   (w_ab_gate_pad[li], w_ab_gate_t[li]), bb=bb, tt=tt, tn=512, out_dtype=F32,
                                   name="ln_ab_" + tag)
                if g["prev"] is not None:
                    g["h"], outs = outs[0], outs[1:]
                proj, gates, gates_t = outs
                gates_t = gates_t.reshape(N_GATE_COLS, b, t).transpose(1, 0, 2)
                mixed, conv_n, sa_n, cb_n, nb_n, mb_n = _mixer_ab_call(
                    proj, gates, gates_t, g["conv"][li], g["sa"][li], g["cb"][li],
                    g["nb"][li].reshape(b, H_B, 1, DQK_B), g["mb"][li].reshape(b, H_B, 1, 1),
                    conv_a[li], gpar[li], gpart[li], norm_a[li].reshape(1, DV_A), norm_b[li].reshape(1, DV_B),
                    L=g["L"], CB=g["CB"], name="mixer_ab_" + tag)
                g["new_ab"].append((conv_n, sa_n, cb_n, nb_n.reshape(b, H_B, DQK_B), mb_n.reshape(b, H_B)))
                wout = w_out_ab_b[li]
            else:
                outs = _ln_mm_call(g["h"], g["prev"], norm_mix[layer], sc1, sh1, w_c[li], None,
                                   bb=bb, tt=tt, tn=512, out_dtype=BF16, name="ln_c_" + tag)
                if g["prev"] is not None:
                    g["h"], outs = outs[0], outs[1:]
                (proj,) = outs
                mixed, sc_n = _mixer_c_call(proj, g["cos"], g["sin"], g["sc"][li], L=g["Lc"], CB=g["CBc"],
                                            name="mixer_c_" + tag)
                g["new_c"].append(sc_n)
                wout = w_out_c_b[li]
            g["h"], hx_buf, bkt = _out_route_call(mixed, wout, g["h"], g1, norm_ffn[layer], sc2, sh2, wrt, rb,
                                                  hx_buf, g["roff"], bb=bb, tt=tt, name="out_route_" + tag)
            bkts.append(bkt.reshape(-1))
            g["g2"] = g2
        y_moe, xs_buf = _moe(hx_buf, jnp.concatenate(bkts), xs_buf, w_gate, w_up, w_down, layer)
        for g in groups:
            g["prev"] = (y_moe, g["roff"], g["g2"])

    outs = []
    for gi, g in enumerate(groups):
        y2d, off, g2 = g["prev"]
        y = _final_call(g["h"], y2d, off, g2, norm_final, bb=g["bb"], tt=g["tt"], name=f"final_g{gi}")
        ab = [jnp.stack(s) for s in zip(*g["new_ab"])]
        outs.append((y, ab[0], ab[1], ab[2], ab[3], ab[4], jnp.stack(g["new_c"])))
    p, s = outs
    return (p[0], s[0], p[1], p[2], p[3], p[4], p[5], p[6], s[1], s[2], s[3], s[4], s[5], s[6])
```

```python
import functools

import numpy as np
import jax
import jax.numpy as jnp
from jax import lax
from jax.experimental import pallas as pl
from jax.experimental.pallas import tpu as pltpu

F32 = jnp.float32
BF16 = jnp.bfloat16
I32 = jnp.int32

D_MODEL = 1024
DEPTH = 2
CHUNK = 64
H_A, DK_A, DV_A, CONV_W = 4, 128, 128, 4
QKV_A = H_A * (2 * DK_A + DV_A)
H_B, DQK_B, DV_B = 4, 64, 128
H_C, DK_C, DV_C = 4, 256, 512
ROPE_BASE = 10000.0
PAST_LEN = 4096
AB_SIZES = (QKV_A, H_A, H_A, H_A * DV_A, H_B * DQK_B, H_B * DQK_B, H_B * DV_B, 2 * H_B, H_B * DV_B)
MIX_AB = H_A * DV_A + H_B * DV_B
MIX_C = H_C * DV_C
IN_C = 2 * H_C * DK_C + 2 * H_C * DV_C
N_EXPERTS, N_GROUPS, EPG, D_FF = 16, 4, 4, 512
N_MOD = 6
EPS = 1e-6

LANES = 128
AB_MAIN = 3584
N_GATE_COLS = 16
HX_W = D_MODEL + LANES
N_BUCKETS = N_GROUPS * 6
MOE_TM = 256
VMEM_LIMIT = 48 * 1024 * 1024

NN = ((1,), (0,))
NT = ((1,), (1,))
TN = ((0,), (0,))


def _dot(a, b, dims=NN):
    return lax.dot_general(a, b, (dims, ((), ())), preferred_element_type=F32)


def _mmb(a, b, dims=NN):
    return _dot(a.astype(BF16), b.astype(BF16), dims)


def _split2(x):
    hi = x.astype(BF16)
    lo = (x - hi.astype(F32)).astype(BF16)
    return hi, lo


def _split3(x):
    hi = x.astype(BF16)
    r = x - hi.astype(F32)
    mid = r.astype(BF16)
    lo = (r - mid.astype(F32)).astype(BF16)
    return hi, mid, lo


def _mm3(a, b, dims=NN):
    ah, al = _split2(a)
    bh, bl = _split2(b)
    return _dot(ah, bh, dims) + (_dot(ah, bl, dims) + _dot(al, bh, dims))


def _mm_mask_l(mask_bf16, x, dims=NN):
    h, m, l = _split3(x)
    return _dot(mask_bf16, h, dims) + (_dot(mask_bf16, m, dims) + _dot(mask_bf16, l, dims))


def _mm_mask_r(x, mask_bf16):
    h, m, l = _split3(x)
    return _dot(h, mask_bf16) + (_dot(m, mask_bf16) + _dot(l, mask_bf16))


def _sigmoid(x):
    return 1.0 / (1.0 + jnp.exp(-x))


def _silu(x):
    return x * _sigmoid(x)


def _softplus(x):
    return jnp.maximum(x, 0.0) + jnp.log(1.0 + jnp.exp(-jnp.abs(x)))


def _rms(x, eps=EPS):
    return x * lax.rsqrt(jnp.mean(x * x, axis=-1, keepdims=True) + eps)


def _cparams(sem):
    return pltpu.CompilerParams(dimension_semantics=sem, vmem_limit_bytes=VMEM_LIMIT)


def _mod_kernel(c_ref, w_ref, b_ref, o_ref):
    c = c_ref[...]
    o_ref[0] = _mm3(_silu(c), w_ref[0]) + b_ref[0]


def _mod_call(c_all, w_mod, b_mod):
    bt = c_all.shape[0]
    e = w_mod.shape[-1]
    tn = 1024
    return pl.pallas_call(
        _mod_kernel,
        grid=(DEPTH, e // tn),
        in_specs=[pl.BlockSpec((bt, D_MODEL), lambda l, j: (0, 0)),
                  pl.BlockSpec((1, D_MODEL, tn), lambda l, j: (l, 0, j)),
                  pl.BlockSpec((1, 1, tn), lambda l, j: (l, 0, j))],
        out_specs=pl.BlockSpec((1, bt, tn), lambda l, j: (l, 0, j)),
        out_shape=jax.ShapeDtypeStruct((DEPTH, bt, e), F32),
        compiler_params=_cparams(("parallel", "parallel")),
        name="mod",
    )(c_all, w_mod, b_mod.reshape(DEPTH, 1, e))


def _ln_mm_kernel(*refs, has_prev, has_gates, tn):
    it = iter(refs)
    x_ref = next(it)
    if has_prev:
        yp_ref, gp_ref = next(it), next(it)
    gain_ref, sc_ref, sh_ref, w_ref = next(it), next(it), next(it), next(it)
    if has_gates:
        wg_ref = next(it)
    if has_prev:
        h_ref = next(it)
    proj_ref = next(it)
    if has_gates:
        g_ref, gt_ref = next(it), next(it)
    hn_s = next(it)

    bb, tt, d = x_ref.shape
    rows = bb * tt
    x = x_ref[...]
    if has_prev:
        x = x + gp_ref[...] * yp_ref[...].reshape(bb, tt, d)
        h_ref[...] = x
    hn = _rms(x) * gain_ref[...]
    hn = hn * (1.0 + sc_ref[...]) + sh_ref[...]
    hn2 = hn.reshape(rows, d)
    hn_s[...] = hn2.astype(BF16)
    if has_gates:
        gates = _mm3(hn2, wg_ref[...])
        g_ref[...] = gates.reshape(bb, tt, LANES)
        gt_ref[...] = gates.T[0:N_GATE_COLS, :]
    for j in range(proj_ref.shape[-1] // tn):
        cols = slice(j * tn, (j + 1) * tn)
        proj_ref[:, :, cols] = _dot(hn_s[...], w_ref[:, cols]).reshape(bb, tt, tn).astype(proj_ref.dtype)


def _ln_mm_call(x, prev, gain, sc, sh, w, gates_w, *, bb, tt, tn, out_dtype, name):
    b, t, d = x.shape
    e = w.shape[1]
    rows = bb * tt
    nti = t // tt
    grid = (b // bb, nti)
    has_prev = prev is not None
    has_gates = gates_w is not None
    xspec = pl.BlockSpec((bb, tt, d), lambda bi, i: (bi, i, 0))
    mspec = pl.BlockSpec((bb, 1, d), lambda bi, i: (bi, 0, 0))
    in_specs, args = [xspec], [x]
    if has_prev:
        y2d, off, gp = prev
        in_specs += [pl.BlockSpec((rows, d), lambda bi, i: (off + bi * nti + i, 0)), mspec]
        args += [y2d, gp]
    in_specs += [pl.BlockSpec((1, 1, d), lambda bi, i: (0, 0, 0)), mspec, mspec,
                 pl.BlockSpec((d, e), lambda bi, i: (0, 0), pipeline_mode=pl.Buffered(1))]
    args += [gain.reshape(1, 1, d), sc, sh, w]
    if has_gates:
        in_specs.append(pl.BlockSpec((d, LANES), lambda bi, i: (0, 0)))
        args.append(gates_w)
    out_specs, out_shape = [], []
    if has_prev:
        out_specs.append(xspec)
        out_shape.append(jax.ShapeDtypeStruct((b, t, d), F32))
    out_specs.append(pl.BlockSpec((bb, tt, e), lambda bi, i: (bi, i, 0)))
    out_shape.append(jax.ShapeDtypeStruct((b, t, e), out_dtype))
    if has_gates:
        out_specs += [pl.BlockSpec((bb, tt, LANES), lambda bi, i: (bi, i, 0)),
                      pl.BlockSpec((N_GATE_COLS, rows), lambda bi, i: (0, bi * nti + i))]
        out_shape += [jax.ShapeDtypeStruct((b, t, LANES), F32),
                      jax.ShapeDtypeStruct((N_GATE_COLS, b * t), F32)]
    return pl.pallas_call(
        functools.partial(_ln_mm_kernel, has_prev=has_prev, has_gates=has_gates, tn=tn),
        grid=grid, in_specs=in_specs, out_specs=out_specs, out_shape=out_shape,
        scratch_shapes=[pltpu.VMEM((rows, d), BF16)],
        compiler_params=_cparams(("parallel", "parallel")),
        name=name,
    )(*args)


def _tri_inv_all(a_list, eye, length, mm):
    xs = [-a for a in a_list]
    ps = [eye + x for x in xs]
    n = 2
    while n < length:
        xs = [mm(x, x) for x in xs]
        ps = [p + mm(p, x) for p, x in zip(ps, xs)]
        n *= 2
    return ps


def _mixer_ab_kernel(qkv_ref, z_ref, qkb_ref, vb_ref, ob_ref, g_ref, gt_ref,
                     conv0_ref, sa0_ref, cb0_ref, nb0_ref, mb0_ref,
                     convw_ref, gpar_ref, gpart_ref, na_ref, nbn_ref,
                     mixed_ref, conv_o, sa_o, cb_o, nb_o, mb_o,
                     xbuf, *, L, CB):
    TB = L * CB

    @pl.when(pl.program_id(1) == 0)
    def _():
        xbuf[5:8, :] = conv0_ref[0]
        sa_o[...] = sa0_ref[...]
        cb_o[...] = cb0_ref[...]
        nb_o[...] = nb0_ref[...]
        mb_o[...] = mb0_ref[...]

    xbuf[8:8 + TB, :] = qkv_ref[0]
    conv_o[0] = xbuf[5 + TB:8 + TB, :]

    ii = lax.broadcasted_iota(I32, (L, L), 0)
    jj = lax.broadcasted_iota(I32, (L, L), 1)
    tri = ii >= jj
    strict = ii > jj
    eye = (ii == jj).astype(F32)
    tril_b = tri.astype(BF16)
    triu_b = (ii <= jj).astype(BF16)
    lane = lax.broadcasted_iota(I32, (L, LANES), 1)
    rowi = lax.broadcasted_iota(I32, (N_GATE_COLS, L), 0)
    gpar = gpar_ref[...]
    gpart = gpart_ref[...]
    nea = -jnp.exp(gpar[0:1, :])
    neat = -jnp.exp(gpart[:, 0:1])
    norm_a = na_ref[...]
    norm_b = nbn_ref[...]

    mm = _mmb
    mm_inv = _mm3

    def conv_tile(r0, col0):
        acc = None
        for j in range(CONV_W):
            term = convw_ref[j:j + 1, col0:col0 + DK_A] * xbuf[5 + j + r0:5 + j + r0 + L, col0:col0 + DK_A]
            acc = term if acc is None else acc + term
        return _silu(acc)

    chunks = range(CB)
    items = [(c, h) for c in chunks for h in range(H_A)]
    rs = [slice(c * L, (c + 1) * L) for c in chunks]

    pre = [g_ref[0, r, :] + gpar[1:2, :] for r in rs]
    csrc = [jnp.where(lane < 4, nea * _softplus(p), jnp.where(lane >= 12, -_softplus(-p), 0.0)) for p in pre]
    gcum = [_mm_mask_l(tril_b, jnp.where(lane < N_GATE_COLS, x, 0.0)) for x in csrc]
    pret = [gt_ref[0, :, r] + gpart[:, 1:2] for r in rs]
    csrct = [jnp.where(rowi < 4, neat * _softplus(p), jnp.where(rowi >= 12, -_softplus(-p), 0.0)) for p in pret]
    gcumt = [_mm_mask_r(x, triu_b) for x in csrct]
    beta_all = [_sigmoid(p) for p in pre]

    q = [conv_tile(c * L, h * DK_A) for c, h in items]
    k = [conv_tile(c * L, H_A * DK_A + h * DK_A) for c, h in items]
    v = [conv_tile(c * L, 2 * H_A * DK_A + h * DV_A) for c, h in items]
    q = [x * lax.rsqrt(jnp.sum(x * x, axis=-1, keepdims=True) + EPS) * (DK_A ** -0.5) for x in q]
    k = [x * lax.rsqrt(jnp.sum(x * x, axis=-1, keepdims=True) + EPS) for x in k]
    gc_c = [gcum[c][:, h:h + 1] for c, h in items]
    gc_r = [gcumt[c][h:h + 1, :] for c, h in items]
    decay = [jnp.where(tri, jnp.exp(jnp.where(tri, a - b, 0.0)), 0.0) for a, b in zip(gc_c, gc_r)]
    beta_c = [beta_all[c][:, 4 + h:5 + h] for c, h in items]
    kb = [x * b for x, b in zip(k, beta_c)]
    a_list = [jnp.where(strict, mm_inv(x, y, NT) * d, 0.0) for x, y, d in zip(kb, k, decay)]
    attn = [jnp.where(tri, mm(x, y, NT) * d, 0.0) for x, y, d in zip(q, k, decay)]
    eg = [jnp.exp(x) for x in gc_c]
    gl = [x[L - 1:L, :] for x in gc_c]
    rhs = [jnp.concatenate([x * e, y * b], axis=-1) for x, e, y, b in zip(kb, eg, v, beta_c)]
    qeg = [x * e for x, e in zip(q, eg)]
    kdec = [x * jnp.exp(g - gc) for x, g, gc in zip(k, gl, gc_c)]
    sdec = [jnp.exp(g) for g in gl]
    tinv = _tri_inv_all(a_list, eye, L, mm)
    sol = [mm(t, r) for t, r in zip(tinv, rhs)]
    resid = [r - (s + mm_inv(a, s)) for r, s, a in zip(rhs, sol, a_list)]
    sol = [s + mm(t, r) for s, t, r in zip(sol, tinv, resid)]

    heads = range(H_B)
    qb = [qkb_ref[0, rs[c], h * DQK_B:(h + 1) * DQK_B] * (DQK_B ** -0.5) for c, h in items]
    kbb = [qkb_ref[0, rs[c], H_B * DQK_B + h * DQK_B:H_B * DQK_B + (h + 1) * DQK_B] for c, h in items]
    b_c = [gcum[c][:, 12 + h:13 + h] for c, h in items]
    dm = [jnp.where(tri, gcum[c][:, 12 + h:13 + h] - gcumt[c][12 + h:13 + h, :] + pret[c][8 + h:9 + h, :], -jnp.inf)
          for c, h in items]
    dmax = [jnp.max(x, axis=-1, keepdims=True) for x in dm]
    qkm = [mm(x, y, NT) for x, y in zip(qb, kbb)]
    li_c = [pre[c][:, 8 + h:9 + h] for c, h in items]

    for c in chunks:
        ia = [c * H_A + h for h in heads]
        s = [sa_o[0, h] for h in heads]
        v_new = [sol[i][:, DK_A:] - mm(sol[i][:, :DK_A], s[h]) for h, i in enumerate(ia)]
        m_prev = [mb_o[0, h] for h in heads]
        a0 = [b_c[i] + m_prev[h] for h, i in enumerate(ia)]
        m_t = [jnp.maximum(a0[h], dmax[i]) for h, i in enumerate(ia)]
        w0 = [jnp.exp(a0[h] - m_t[h]) for h in heads]
        sm = [qkm[i] * jnp.exp(dm[i] - m_t[h]) for h, i in enumerate(ia)]
        cst = [cb_o[0, h] for h in heads]
        nst = [nb_o[0, h] for h in heads]
        vb = [vb_ref[0, rs[c], h * DV_B:(h + 1) * DV_B] for h in heads]
        o_a = [mm(qeg[i], s[h]) + mm(attn[i], v_new[h]) for h, i in enumerate(ia)]
        for h, i in enumerate(ia):
            sa_o[0, h] = s[h] * sdec[i] + mm(kdec[i], v_new[h], TN)
        num = [w0[h] * mm(qb[i], cst[h]) + mm(sm[h], vb[h]) for h, i in enumerate(ia)]
        den = [w0[h] * jnp.sum(qb[i] * nst[h], axis=-1, keepdims=True) + jnp.sum(sm[h], axis=-1, keepdims=True)
               for h, i in enumerate(ia)]
        hh = [num[h] / jnp.maximum(jnp.abs(den[h]), jnp.exp(-m_t[h])) for h in heads]
        m_new = [m_t[h][L - 1:L, :] for h in heads]
        kw = [kbb[i] * jnp.exp(b_c[i][L - 1:L, :] - b_c[i] + li_c[i] - m_new[h]) for h, i in enumerate(ia)]
        for h in heads:
            w0l = w0[h][L - 1:L, :]
            cb_o[0, h] = w0l * cst[h] + mm(kw[h], vb[h], TN)
            nb_o[0, h] = w0l * nst[h] + jnp.sum(kw[h], axis=0, keepdims=True)
            mb_o[0, h] = m_new[h]
        for h in heads:
            zg = z_ref[0, rs[c], h * DV_A:(h + 1) * DV_A]
            mixed_ref[0, rs[c], h * DV_A:(h + 1) * DV_A] = (_rms(o_a[h]) * norm_a * _silu(zg)).astype(mixed_ref.dtype)
            og = ob_ref[0, rs[c], h * DV_B:(h + 1) * DV_B]
            c0 = H_A * DV_A + h * DV_B
            mixed_ref[0, rs[c], c0:c0 + DV_B] = (_rms(hh[h]) * norm_b * _sigmoid(og)).astype(mixed_ref.dtype)

    xbuf[5:8, :] = xbuf[5 + TB:8 + TB, :]


def _mixer_ab_call(proj, gates, gates_t, conv0, sa0, cb0, nb0, mb0, convw, gpar, gpart, norm_a, norm_b, *, L, CB, name):
    b, t, _ = proj.shape
    tb = L * CB
    nt = t // tb

    def col(width, idx):
        return pl.BlockSpec((1, tb, width), lambda bi, i: (bi, i, idx))

    def const(shape):
        return pl.BlockSpec(shape, lambda bi, i: (0,) * len(shape))

    def state(shape):
        return pl.BlockSpec((1,) + shape, lambda bi, i: (bi,) + (0,) * len(shape))

    in_specs = [col(QKV_A, 0), col(512, 3), col(512, 4), col(512, 5), col(512, 6),
                col(LANES, 0), pl.BlockSpec((1, N_GATE_COLS, tb), lambda bi, i: (bi, 0, i)),
                state((CONV_W - 1, QKV_A)), state((H_A, DK_A, DV_A)), state((H_B, DQK_B, DV_B)),
                state((H_B, 1, DQK_B)), state((H_B, 1, 1)),
                const((CONV_W, QKV_A)), const((8, LANES)), const((N_GATE_COLS, LANES)),
                const((1, DV_A)), const((1, DV_B))]
    out_specs = [pl.BlockSpec((1, tb, MIX_AB), lambda bi, i: (bi, i, 0)),
                 state((CONV_W - 1, QKV_A)), state((H_A, DK_A, DV_A)), state((H_B, DQK_B, DV_B)),
                 state((H_B, 1, DQK_B)), state((H_B, 1, 1))]
    out_shape = [jax.ShapeDtypeStruct((b, t, MIX_AB), BF16),
                 jax.ShapeDtypeStruct((b, CONV_W - 1, QKV_A), F32),
                 jax.ShapeDtypeStruct((b, H_A, DK_A, DV_A), F32),
                 jax.ShapeDtypeStruct((b, H_B, DQK_B, DV_B), F32),
                 jax.ShapeDtypeStruct((b, H_B, 1, DQK_B), F32),
                 jax.ShapeDtypeStruct((b, H_B, 1, 1), F32)]
    return pl.pallas_call(
        functools.partial(_mixer_ab_kernel, L=L, CB=CB),
        grid=(b, nt), in_specs=in_specs, out_specs=out_specs, out_shape=out_shape,
        scratch_shapes=[pltpu.VMEM((8 + tb, QKV_A), F32)],
        compiler_params=_cparams(("parallel", "arbitrary")),
        name=name,
    )(proj, proj, proj, proj, proj, gates, gates_t, conv0, sa0, cb0, nb0, mb0, convw, gpar, gpart, norm_a, norm_b)


def _mixer_c_kernel(q_ref, k_ref, v_ref, g_ref, cos_ref, sin_ref, s0_ref, mixed_ref, s_o, *, L, CB):
    @pl.when(pl.program_id(1) == 0)
    def _():
        s_o[...] = s0_ref[...]

    ii = lax.broadcasted_iota(I32, (L, L), 0)
    jj = lax.broadcasted_iota(I32, (L, L), 1)
    rel = (ii - jj).astype(F32)
    idx = lax.broadcasted_iota(I32, (L, 1), 0).astype(F32)
    half = DK_C // 2

    def rope(x, cos, sin):
        x1, x2 = x[:, :half], x[:, half:]
        return jnp.concatenate([x1 * cos - x2 * sin, x1 * sin + x2 * cos], axis=-1)

    for c in range(CB):
        r0 = c * L
        cos = cos_ref[r0:r0 + L, :]
        sin = sin_ref[r0:r0 + L, :]
        for h in range(H_C):
            lg = float(np.log1p(-np.exp2(-5.0 - h)))
            intra = jnp.where(rel >= 0, jnp.exp(lg * jnp.maximum(rel, 0.0)), 0.0)
            q_decay = jnp.exp(lg * (idx + 1.0))
            k_decay = jnp.exp(lg * (L - 1.0 - idx))
            s_decay = float(np.exp(lg * L))
            q = rope(q_ref[0, r0:r0 + L, h * DK_C:(h + 1) * DK_C].astype(F32), cos, sin)
            k = rope(k_ref[0, r0:r0 + L, h * DK_C:(h + 1) * DK_C].astype(F32), cos, sin) * (DK_C ** -0.5)
            v = v_ref[0, r0:r0 + L, h * DV_C:(h + 1) * DV_C]
            s = s_o[0, h]
            o = _mmb(_mmb(q, k, NT) * intra, v) + _mmb(q * q_decay, s)
            s_o[0, h] = s_decay * s + _mmb(k * k_decay, v, TN)
            gg = g_ref[0, r0:r0 + L, h * DV_C:(h + 1) * DV_C].astype(F32)
            o = _rms(o) * _silu(gg)
            mixed_ref[0, r0:r0 + L, h * DV_C:(h + 1) * DV_C] = o.astype(mixed_ref.dtype)


def _mixer_c_call(proj, cos, sin, s0, *, L, CB, name):
    b, t, _ = proj.shape
    tb = L * CB
    nt = t // tb
    qk_w, v_w = H_C * DK_C, H_C * DV_C
    sspec = pl.BlockSpec((1, H_C, DK_C, DV_C), lambda bi, i: (bi, 0, 0, 0))
    tspec = pl.BlockSpec((tb, DK_C // 2), lambda bi, i: (i, 0))
    return pl.pallas_call(
        functools.partial(_mixer_c_kernel, L=L, CB=CB),
        grid=(b, nt),
        in_specs=[pl.BlockSpec((1, tb, qk_w), lambda bi, i: (bi, i, 0)),
                  pl.BlockSpec((1, tb, qk_w), lambda bi, i: (bi, i, 1)),
                  pl.BlockSpec((1, tb, v_w), lambda bi, i: (bi, i, 1)),
                  pl.BlockSpec((1, tb, v_w), lambda bi, i: (bi, i, 2)),
                  tspec, tspec, sspec],
        out_specs=[pl.BlockSpec((1, tb, MIX_C), lambda bi, i: (bi, i, 0)), sspec],
        out_shape=[jax.ShapeDtypeStruct((b, t, MIX_C), BF16),
                   jax.ShapeDtypeStruct((b, H_C, DK_C, DV_C), F32)],
        compiler_params=_cparams(("parallel", "arbitrary")),
        name=name,
    )(proj, proj, proj, proj, cos, sin, s0)


def _out_route_kernel(mixed_ref, wout_ref, h_ref, g1_ref, gain_ref, sc_ref, sh_ref, wrt_ref, rb_ref, hx_in_ref,
                      hnew_ref, hx_ref, bkt_ref):
    del hx_in_ref
    bb, tt, m = mixed_ref.shape
    d = h_ref.shape[-1]
    rows = bb * tt
    y = _dot(mixed_ref[...].reshape(rows, m), wout_ref[...])
    h = h_ref[...] + g1_ref[...] * y.reshape(bb, tt, d)
    hnew_ref[...] = h
    hn = _rms(h) * gain_ref[...]
    hn = (hn * (1.0 + sc_ref[...]) + sh_ref[...]).reshape(rows, d)
    hx_ref[:, 0:d] = hn

    logits = _mm3(wrt_ref[...], hn, NT)
    score = _sigmoid(logits)
    sel = score + rb_ref[...]

    def row(a, r):
        return a[r:r + 1, :]

    gscore = []
    for g in range(N_GROUPS):
        a, b, c, e = (row(sel, EPG * g + i) for i in range(EPG))
        hi_ab, lo_ab, hi_ce, lo_ce = jnp.maximum(a, b), jnp.minimum(a, b), jnp.maximum(c, e), jnp.minimum(c, e)
        top1 = jnp.maximum(hi_ab, hi_ce)
        top2 = jnp.maximum(jnp.maximum(lo_ab, lo_ce), jnp.minimum(hi_ab, hi_ce))
        gscore.append(top1 + top2)
    best = jnp.zeros((1, rows), I32)
    bestv = gscore[0]
    for g in range(1, N_GROUPS):
        upd = gscore[g] > bestv
        best = jnp.where(upd, g, best)
        bestv = jnp.where(upd, gscore[g], bestv)

    def pick(a, i):
        out = row(a, i)
        for g in range(1, N_GROUPS):
            out = jnp.where(best == g, row(a, EPG * g + i), out)
        return out

    vsel = [pick(sel, i) for i in range(EPG)]
    vsc = [pick(score, i) for i in range(EPG)]
    i1 = jnp.zeros((1, rows), I32)
    v1 = vsel[0]
    for i in range(1, EPG):
        upd = vsel[i] > v1
        i1 = jnp.where(upd, i, i1)
        v1 = jnp.where(upd, vsel[i], v1)
    i2 = jnp.full((1, rows), -1, I32)
    v2 = jnp.full((1, rows), -jnp.inf, F32)
    for i in range(EPG):
        cand = jnp.where(i1 == i, -jnp.inf, vsel[i])
        upd = cand > v2
        i2 = jnp.where(upd, i, i2)
        v2 = jnp.where(upd, cand, v2)
    s1 = jnp.zeros((1, rows), F32)
    s2 = jnp.zeros((1, rows), F32)
    for i in range(EPG):
        s1 = jnp.where(i1 == i, vsc[i], s1)
        s2 = jnp.where(i2 == i, vsc[i], s2)
    den = s1 + s2
    w1 = s1 / den
    w2 = s2 / den
    first_lo = i1 < i2
    lo = jnp.minimum(i1, i2)
    hi = jnp.maximum(i1, i2)
    wlo = jnp.where(first_lo, w1, w2)
    whi = jnp.where(first_lo, w2, w1)
    pair = jnp.where(lo == 0, hi - 1, jnp.where(lo == 1, 6 - hi, 5))
    bkt_ref[...] = (best * 6 + pair).reshape(1, 1, rows)
    wa = jnp.where(pair == 5, whi, wlo)
    wb = jnp.where(pair == 5, wlo, whi)

    aux = jnp.concatenate([wa, wb, jnp.zeros((LANES - 2, rows), F32)], axis=0)
    hx_ref[:, d:d + LANES] = aux.T


def _out_route_call(mixed, wout, h, g1, gain, sc, sh, wrt, rb, hx_buf, row_block_off, *, bb, tt, name):
    b, t, m = mixed.shape
    d = h.shape[-1]
    rows = bb * tt
    nti = t // tt
    ntot = hx_buf.shape[0]
    xspec = pl.BlockSpec((bb, tt, d), lambda bi, i: (bi, i, 0))
    mspec = pl.BlockSpec((bb, 1, d), lambda bi, i: (bi, 0, 0))
    return pl.pallas_call(
        _out_route_kernel,
        grid=(b // bb, nti),
        in_specs=[pl.BlockSpec((bb, tt, m), lambda bi, i: (bi, i, 0)),
                  pl.BlockSpec((m, d), lambda bi, i: (0, 0)),
                  xspec, mspec,
                  pl.BlockSpec((1, 1, d), lambda bi, i: (0, 0, 0)), mspec, mspec,
                  pl.BlockSpec((N_EXPERTS, d), lambda bi, i: (0, 0)),
                  pl.BlockSpec((N_EXPERTS, 1), lambda bi, i: (0, 0)),
                  pl.BlockSpec(memory_space=pl.ANY)],
        out_specs=[xspec,
                   pl.BlockSpec((rows, HX_W), lambda bi, i: (row_block_off + bi * nti + i, 0)),
                   pl.BlockSpec((1, 1, rows), lambda bi, i: (bi * nti + i, 0, 0))],
        out_shape=[jax.ShapeDtypeStruct((b, t, d), F32),
                   jax.ShapeDtypeStruct((ntot, HX_W), F32),
                   jax.ShapeDtypeStruct(((b // bb) * nti, 1, rows), I32)],
        input_output_aliases={9: 1},
        compiler_params=_cparams(("parallel", "parallel")),
        name=name,
    )(mixed, wout, h, g1, gain.reshape(1, 1, d), sc, sh, wrt, rb, hx_buf)


def _dispatch_kernel(pos_ref, hx_ref, xs_in, xs_out, sem):
    del xs_in
    i = pl.program_id(0)
    rows = hx_ref.shape[0]

    def row_copy(r):
        p = pos_ref[i * rows + r]
        return pltpu.make_async_copy(hx_ref.at[pl.ds(r, 1), :], xs_out.at[pl.ds(p, 1), :], sem)

    for r in range(rows):
        row_copy(r).start(priority=r % 2)
    for r in range(rows):
        row_copy(r).wait()


def _dispatch_call(pos, hx, xs_buf, rows):
    n = hx.shape[0]
    return pl.pallas_call(
        _dispatch_kernel,
        grid_spec=pltpu.PrefetchScalarGridSpec(
            num_scalar_prefetch=1, grid=(n // rows,),
            in_specs=[pl.BlockSpec((rows, HX_W), lambda i, ps: (i, 0)), pl.BlockSpec(memory_space=pl.ANY)],
            out_specs=pl.BlockSpec(memory_space=pl.ANY),
            scratch_shapes=[pltpu.SemaphoreType.DMA(())]),
        out_shape=jax.ShapeDtypeStruct(xs_buf.shape, F32),
        input_output_aliases={2: 0},
        compiler_params=_cparams(("arbitrary",)),
        name="dispatch",
    )(pos, hx, xs_buf)


def _moe_kernel(exp_ref, chg_ref, nxt_ref, valid_ref, xs_ref, wg_hbm, wu_hbm, wd_hbm,
                ys_ref, wg_f, wu_f, wd_f, wg_s, wu_s, wd_s, sem, *, layer):
    t = pl.program_id(0)

    def fetch(slot, e):
        return (pltpu.make_async_copy(wg_hbm.at[layer, e], wg_f.at[slot], sem.at[slot]),
                pltpu.make_async_copy(wu_hbm.at[layer, e], wu_f.at[slot], sem.at[slot]),
                pltpu.make_async_copy(wd_hbm.at[layer, e], wd_f.at[slot], sem.at[slot]))

    def recast(slot):
        rc = 128
        for i in range(D_MODEL // rc):
            wg_s[slot, i * rc:(i + 1) * rc, :] = wg_f[slot, i * rc:(i + 1) * rc, :].astype(BF16)
            wu_s[slot, i * rc:(i + 1) * rc, :] = wu_f[slot, i * rc:(i + 1) * rc, :].astype(BF16)
        for i in range(D_FF // rc):
            wd_s[slot, i * rc:(i + 1) * rc, :] = wd_f[slot, i * rc:(i + 1) * rc, :].astype(BF16)

    for slot in range(2):
        @pl.when(t == 0)
        def _():
            for cp in fetch(slot, exp_ref[slot, 0]):
                cp.start()

        @pl.when(chg_ref[slot, t] > 0)
        def _():
            for cp in fetch(slot, exp_ref[slot, t]):
                cp.wait()
            recast(slot)

            @pl.when(nxt_ref[slot, t] >= 0)
            def _():
                for cp in fetch(slot, nxt_ref[slot, t]):
                    cp.start()

    @pl.when(valid_ref[t] > 0)
    def _():
        x = xs_ref[:, 0:D_MODEL].astype(BF16)
        w_a = xs_ref[:, D_MODEL:D_MODEL + 1]
        w_b = xs_ref[:, D_MODEL + 1:D_MODEL + 2]
        act_a = (_silu(_dot(x, wg_s[0])) * _dot(x, wu_s[0]) * w_a).astype(BF16)
        act_b = (_silu(_dot(x, wg_s[1])) * _dot(x, wu_s[1]) * w_b).astype(BF16)
        ys_ref[...] = _dot(act_a, wd_s[0]) + _dot(act_b, wd_s[1])

    @pl.when(valid_ref[t] == 0)
    def _():
        ys_ref[...] = jnp.zeros_like(ys_ref)


def _moe_call(experts, chg, nxt, valid, xs, wg, wu, wd, layer):
    p = xs.shape[0]
    n_tiles = p // MOE_TM
    return pl.pallas_call(
        functools.partial(_moe_kernel, layer=layer),
        grid_spec=pltpu.PrefetchScalarGridSpec(
            num_scalar_prefetch=4, grid=(n_tiles,),
            in_specs=[pl.BlockSpec((MOE_TM, HX_W), lambda t, ex, ch, nx, va: (t, 0)),
                      pl.BlockSpec(memory_space=pl.ANY), pl.BlockSpec(memory_space=pl.ANY),
                      pl.BlockSpec(memory_space=pl.ANY)],
            out_specs=pl.BlockSpec((MOE_TM, D_MODEL), lambda t, ex, ch, nx, va: (t, 0)),
            scratch_shapes=[pltpu.VMEM((2, D_MODEL, D_FF), F32), pltpu.VMEM((2, D_MODEL, D_FF), F32),
                            pltpu.VMEM((2, D_FF, D_MODEL), F32),
                            pltpu.VMEM((2, D_MODEL, D_FF), BF16), pltpu.VMEM((2, D_MODEL, D_FF), BF16),
                            pltpu.VMEM((2, D_FF, D_MODEL), BF16),
                            pltpu.SemaphoreType.DMA((2,))]),
        out_shape=jax.ShapeDtypeStruct((p, D_MODEL), F32),
        compiler_params=_cparams(("arbitrary",)),
        name="moe",
    )(experts, chg, nxt, valid, xs, wg, wu, wd)


_PAIRS = [(0, 1), (0, 2), (0, 3), (1, 3), (1, 2), (3, 2)]
_EA_TABLE = np.array([g * EPG + p[0] for g in range(N_GROUPS) for p in _PAIRS], np.int32)
_EB_TABLE = np.array([g * EPG + p[1] for g in range(N_GROUPS) for p in _PAIRS], np.int32)


def _moe(hx, bkt, xs_buf, wg, wu, wd, layer):
    n = hx.shape[0]
    tm = MOE_TM
    ar = jnp.arange(N_BUCKETS, dtype=I32)
    blk = LANES
    nb = -(-n // blk)
    bkt_p = jnp.pad(bkt, (0, nb * blk - n), constant_values=N_BUCKETS)
    oh = (ar[:, None] == bkt_p[None, :]).astype(F32)
    triu = jnp.triu(jnp.ones((blk, blk), F32))
    inner = jnp.einsum('kbj,ji->kbi', oh.reshape(N_BUCKETS, nb, blk), triu)
    tot = inner[:, :, -1]
    outer = jnp.cumsum(tot, axis=1) - tot
    cs = (inner + outer[:, :, None]).reshape(N_BUCKETS, nb * blk)
    counts = (outer[:, -1] + tot[:, -1]).astype(I32)
    padded = ((counts + tm - 1) // tm) * tm
    ends = jnp.cumsum(padded)
    offs = ends - padded
    pos = jnp.sum(oh * (cs - 1.0 + offs.astype(F32)[:, None]), axis=0).astype(I32)[:n]
    n_tiles = -(-n // tm) + N_BUCKETS
    p = n_tiles * tm
    if xs_buf is None:
        xs_buf = jnp.zeros((p, HX_W), F32)
    tstart = jnp.arange(n_tiles, dtype=I32) * tm
    tb = jnp.sum((ends[None, :] <= tstart[:, None]).astype(I32), axis=1)
    n_used = ends[-1] // tm
    valid = (jnp.arange(n_tiles, dtype=I32) < n_used).astype(I32)
    last_b = tb[jnp.maximum(n_used - 1, 0)]
    tb = jnp.clip(jnp.where(valid > 0, tb, last_b), 0, N_BUCKETS - 1)
    ea = jnp.asarray(_EA_TABLE)[tb]
    eb = jnp.asarray(_EB_TABLE)[tb]
    experts = jnp.stack([ea, eb])
    chg = jnp.concatenate([jnp.ones((2, 1), I32), (experts[:, 1:] != experts[:, :-1]).astype(I32)], axis=1)
    tidx = jnp.arange(n_tiles, dtype=I32)
    chg_at = jnp.where(chg > 0, tidx[None, :], n_tiles)
    nxt_at = jnp.concatenate([lax.cummin(chg_at, axis=1, reverse=True)[:, 1:], jnp.full((2, 1), n_tiles, I32)], axis=1)
    nxt = jnp.where(nxt_at < n_tiles,
                    jnp.take_along_axis(experts, jnp.minimum(nxt_at, n_tiles - 1), axis=1), -1)
    xs = _dispatch_call(pos, hx, xs_buf, tm)
    ys = _moe_call(experts, chg, nxt, valid, xs, wg, wu, wd, layer)
    return ys.at[pos].get(mode="promise_in_bounds", unique_indices=True), xs


def _final_kernel(h_ref, y_ref, g_ref, gain_ref, o_ref):
    bb, tt, d = h_ref.shape
    h = h_ref[...] + g_ref[...] * y_ref[...].reshape(bb, tt, d)
    o_ref[...] = _rms(h) * gain_ref[...]


def _final_call(h, y2d, off, g2, gain, *, bb, tt, name):
    b, t, d = h.shape
    nti = t // tt
    rows = bb * tt
    xspec = pl.BlockSpec((bb, tt, d), lambda bi, i: (bi, i, 0))
    return pl.pallas_call(
        _final_kernel,
        grid=(b // bb, nti),
        in_specs=[xspec, pl.BlockSpec((rows, d), lambda bi, i: (off + bi * nti + i, 0)),
                  pl.BlockSpec((bb, 1, d), lambda bi, i: (bi, 0, 0)),
                  pl.BlockSpec((1, 1, d), lambda bi, i: (0, 0, 0))],
        out_specs=xspec,
        out_shape=jax.ShapeDtypeStruct((b, t, d), F32),
        compiler_params=_cparams(("parallel", "parallel")),
        name=name,
    )(h, y2d, g2, gain.reshape(1, 1, d))


def _group_cfg(b, t):
    if t >= 512:
        return dict(bb=1, tt=512, tt_out=1024, L=CHUNK, CB=4, Lc=4 * CHUNK, CBc=1)
    return dict(bb=b, tt=t, tt_out=t, L=min(CHUNK, t), CB=1, Lc=min(CHUNK, t), CBc=1)


def _rope_tables(pos0, t):
    half = DK_C // 2
    inv = jnp.power(ROPE_BASE, -jnp.linspace(0.0, 1.0, half, dtype=F32))
    ang = (pos0 + jnp.arange(t, dtype=F32))[:, None] * inv[None, :]
    return jnp.cos(ang), jnp.sin(ang)


def kernel(x_prompt, x_sample, c_prompt, c_sample, state_a_conv, state_a_rec, state_b_c, state_b_n, state_b_m,
           state_c_rec, w_mod, b_mod, norm_mix, norm_ffn, w_in_ab, conv_a, a_log, dt_bias, norm_a, gate_bias_b,
           norm_b, w_out_ab, w_in_c, w_out_c, w_router, router_bias, w_gate, w_up, w_down, norm_final):
    d = D_MODEL
    bp, tp, _ = x_prompt.shape
    bs, ts, _ = x_sample.shape
    n_ab, n_c = w_in_ab.shape[0], w_in_c.shape[0]

    mod_all = _mod_call(jnp.concatenate([c_prompt, c_sample], axis=0), w_mod, b_mod)

    o = np.cumsum((0,) + AB_SIZES)
    w_ab_main = jnp.concatenate([w_in_ab[:, :, o[0]:o[1]].astype(BF16), w_in_ab[:, :, o[3]:o[7]].astype(BF16),
                                 w_in_ab[:, :, o[8]:o[9]].astype(BF16)], axis=-1)
    w_ab_gate = jnp.concatenate([w_in_ab[:, :, o[1]:o[3]], w_in_ab[:, :, o[7]:o[8]]], axis=-1)
    w_ab_gate_pad = jnp.pad(w_ab_gate, ((0, 0), (0, 0), (0, LANES - N_GATE_COLS)))
    w_c = w_in_c.astype(BF16)
    w_out_ab_b = w_out_ab.astype(BF16)
    w_out_c_b = w_out_c.astype(BF16)
    wrt = w_router.T
    rb = router_bias.reshape(N_EXPERTS, 1)
    zeros4 = jnp.zeros((n_ab, H_A), F32)
    gpar_row0 = jnp.concatenate([a_log, jnp.zeros((n_ab, LANES - H_A), F32)], axis=1)
    gpar_row1 = jnp.concatenate([dt_bias, zeros4, gate_bias_b, jnp.zeros((n_ab, LANES - 16), F32)], axis=1)
    gpar = jnp.concatenate([gpar_row0[:, None], gpar_row1[:, None], jnp.zeros((n_ab, 6, LANES), F32)], axis=1)
    gpart = jnp.swapaxes(jnp.concatenate([gpar[:, :2, :N_GATE_COLS], jnp.zeros((n_ab, LANES - 2, N_GATE_COLS), F32)],
                                         axis=1), 1, 2)

    groups = []
    zf = lambda *s: jnp.zeros(s, F32)
    groups.append(dict(
        x=x_prompt, b=bp, t=tp, pos0=0.0, mod=mod_all[:, :bp], row0=0,
        conv=zf(n_ab, bp, CONV_W - 1, QKV_A), sa=zf(n_ab, bp, H_A, DK_A, DV_A), cb=zf(n_ab, bp, H_B, DQK_B, DV_B),
        nb=zf(n_ab, bp, H_B, DQK_B), mb=zf(n_ab, bp, H_B), sc=zf(n_c, bp, H_C, DK_C, DV_C)))
    groups.append(dict(
        x=x_sample, b=bs, t=ts, pos0=float(PAST_LEN), mod=mod_all[:, bp:], row0=bp * tp,
        conv=state_a_conv, sa=state_a_rec, cb=state_b_c, nb=state_b_n, mb=state_b_m, sc=state_c_rec))
    n_tot = bp * tp + bs * ts
    for g in groups:
        g.update(_group_cfg(g["b"], g["t"]))
        g["h"] = g["x"]
        g["prev"] = None
        g["new_ab"], g["new_c"] = [], []
        g["cos"], g["sin"] = _rope_tables(g["pos0"], g["t"])
        g["roff"] = g["row0"] // (g["bb"] * g["tt"])

    xs_buf = None
    for layer in range(DEPTH):
        li = layer // 2
        if layer == 0:
            hx_buf = jnp.zeros((n_tot, HX_W), F32)
        bkts = []
        for gi, g in enumerate(groups):
            b, t, bb, tt = g["b"], g["t"], g["bb"], g["tt"]
            mods = [m.reshape(b, 1, d) for m in jnp.split(g["mod"][layer], N_MOD, axis=-1)]
            sh1, sc1, g1, sh2, sc2, g2 = mods
            tag = f"l{layer}g{gi}"
            if layer % 2 == 0:
                outs = _ln_mm_call(g["h"], g["prev"], norm_mix[layer], sc1, sh1, w_ab_main[li],
                                   w_ab_gate_pad[li], bb=bb, tt=tt, tn=512, out_dtype=F32,
                                   name="ln_ab_" + tag)
                if g["prev"] is not None:
                    g["h"], outs = outs[0], outs[1:]
                proj, gates, gates_t = outs
                gates_t = gates_t.reshape(N_GATE_COLS, b, t).transpose(1, 0, 2)
                mixed, conv_n, sa_n, cb_n, nb_n, mb_n = _mixer_ab_call(
                    proj, gates, gates_t, g["conv"][li], g["sa"][li], g["cb"][li],
                    g["nb"][li].reshape(b, H_B, 1, DQK_B), g["mb"][li].reshape(b, H_B, 1, 1),
                    conv_a[li], gpar[li], gpart[li], norm_a[li].reshape(1, DV_A), norm_b[li].reshape(1, DV_B),
                    L=g["L"], CB=g["CB"], name="mixer_ab_" + tag)
                g["new_ab"].append((conv_n, sa_n, cb_n, nb_n.reshape(b, H_B, DQK_B), mb_n.reshape(b, H_B)))
                wout = w_out_ab_b[li]
            else:
                outs = _ln_mm_call(g["h"], g["prev"], norm_mix[layer], sc1, sh1, w_c[li], None,
                                   bb=bb, tt=tt, tn=512, out_dtype=BF16, name="ln_c_" + tag)
                if g["prev"] is not None:
                    g["h"], outs = outs[0], outs[1:]
                (proj,) = outs
                mixed, sc_n = _mixer_c_call(proj, g["cos"], g["sin"], g["sc"][li], L=g["Lc"], CB=g["CBc"],
                                            name="mixer_c_" + tag)
                g["new_c"].append(sc_n)
                wout = w_out_c_b[li]
            tto = g["tt_out"]
            g["h"], hx_buf, bkt = _out_route_call(mixed, wout, g["h"], g1, norm_ffn[layer], sc2, sh2, wrt, rb,
                                                  hx_buf, g["row0"] // (bb * tto), bb=bb, tt=tto,
                                                  name="out_route_" + tag)
            bkts.append(bkt.reshape(-1))
            g["g2"] = g2
        y_moe, xs_buf = _moe(hx_buf, jnp.concatenate(bkts), xs_buf, w_gate, w_up, w_down, layer)
        for g in groups:
            g["prev"] = (y_moe, g["roff"], g["g2"])

    outs = []
    for gi, g in enumerate(groups):
        y2d, off, g2 = g["prev"]
        y = _final_call(g["h"], y2d, off, g2, norm_final, bb=g["bb"], tt=g["tt"], name=f"final_g{gi}")
        ab = [jnp.stack(s) for s in zip(*g["new_ab"])]
        outs.append((y, ab[0], ab[1], ab[2], ab[3], ab[4], jnp.stack(g["new_c"])))
    p, s = outs
    return (p[0], s[0], p[1], p[2], p[3], p[4], p[5], p[6], s[1], s[2], s[3], s[4], s[5], s[6])
```

```python
import functools

import numpy as np
import jax
import jax.numpy as jnp
from jax import lax
from jax.experimental import pallas as pl
from jax.experimental.pallas import tpu as pltpu

F32 = jnp.float32
BF16 = jnp.bfloat16
I32 = jnp.int32

D_MODEL = 1024
DEPTH = 2
CHUNK = 64
H_A, DK_A, DV_A, CONV_W = 4, 128, 128, 4
QKV_A = H_A * (2 * DK_A + DV_A)
H_B, DQK_B, DV_B = 4, 64, 128
H_C, DK_C, DV_C = 4, 256, 512
ROPE_BASE = 10000.0
PAST_LEN = 4096
AB_SIZES = (QKV_A, H_A, H_A, H_A * DV_A, H_B * DQK_B, H_B * DQK_B, H_B * DV_B, 2 * H_B, H_B * DV_B)
MIX_AB = H_A * DV_A + H_B * DV_B
MIX_C = H_C * DV_C
IN_C = 2 * H_C * DK_C + 2 * H_C * DV_C
N_EXPERTS, N_GROUPS, EPG, D_FF = 16, 4, 4, 512
N_MOD = 6
EPS = 1e-6

LANES = 128
AB_MAIN = 3584
N_GATE_COLS = 16
HX_W = D_MODEL + LANES
N_BUCKETS = N_GROUPS * 6
MOE_TM = 256
VMEM_LIMIT = 48 * 1024 * 1024

NN = ((1,), (0,))
NT = ((1,), (1,))
TN = ((0,), (0,))


def _dot(a, b, dims=NN):
    return lax.dot_general(a, b, (dims, ((), ())), preferred_element_type=F32)


def _mmb(a, b, dims=NN):
    return _dot(a.astype(BF16), b.astype(BF16), dims)


def _split2(x):
    hi = x.astype(BF16)
    lo = (x - hi.astype(F32)).astype(BF16)
    return hi, lo


def _split3(x):
    hi = x.astype(BF16)
    r = x - hi.astype(F32)
    mid = r.astype(BF16)
    lo = (r - mid.astype(F32)).astype(BF16)
    return hi, mid, lo


def _mm3(a, b, dims=NN):
    ah, al = _split2(a)
    bh, bl = _split2(b)
    return _dot(ah, bh, dims) + (_dot(ah, bl, dims) + _dot(al, bh, dims))


def _mm_mask_l(mask_bf16, x, dims=NN):
    h, m, l = _split3(x)
    return _dot(mask_bf16, h, dims) + (_dot(mask_bf16, m, dims) + _dot(mask_bf16, l, dims))


def _mm_mask_r(x, mask_bf16):
    h, m, l = _split3(x)
    return _dot(h, mask_bf16) + (_dot(m, mask_bf16) + _dot(l, mask_bf16))


def _sigmoid(x):
    return 1.0 / (1.0 + jnp.exp(-x))


def _silu(x):
    return x * _sigmoid(x)


def _softplus(x):
    return jnp.maximum(x, 0.0) + jnp.log(1.0 + jnp.exp(-jnp.abs(x)))


def _rms(x, eps=EPS):
    return x * lax.rsqrt(jnp.mean(x * x, axis=-1, keepdims=True) + eps)


def _cparams(sem):
    return pltpu.CompilerParams(dimension_semantics=sem, vmem_limit_bytes=VMEM_LIMIT)


def _mod_kernel(c_ref, w_ref, b_ref, o_ref):
    c = c_ref[...]
    o_ref[0] = _mm3(_silu(c), w_ref[0]) + b_ref[0]


def _mod_call(c_all, w_mod, b_mod):
    bt = c_all.shape[0]
    e = w_mod.shape[-1]
    tn = 1024
    return pl.pallas_call(
        _mod_kernel,
        grid=(DEPTH, e // tn),
        in_specs=[pl.BlockSpec((bt, D_MODEL), lambda l, j: (0, 0)),
                  pl.BlockSpec((1, D_MODEL, tn), lambda l, j: (l, 0, j)),
                  pl.BlockSpec((1, 1, tn), lambda l, j: (l, 0, j))],
        out_specs=pl.BlockSpec((1, bt, tn), lambda l, j: (l, 0, j)),
        out_shape=jax.ShapeDtypeStruct((DEPTH, bt, e), F32),
        compiler_params=_cparams(("parallel", "parallel")),
        name="mod",
    )(c_all, w_mod, b_mod.reshape(DEPTH, 1, e))


def _ln_mm_kernel(*refs, has_prev, has_gates, tn):
    it = iter(refs)
    x_ref = next(it)
    if has_prev:
        yp_ref, gp_ref = next(it), next(it)
    gain_ref, sc_ref, sh_ref, w_ref = next(it), next(it), next(it), next(it)
    if has_gates:
        wg_ref, conv0_ref, convw_ref = next(it), next(it), next(it)
    if has_prev:
        h_ref = next(it)
    proj_ref = next(it)
    if has_gates:
        g_ref, gt_ref, qkv_ref, conv_o = next(it), next(it), next(it), next(it)
    hn_s = next(it)
    if has_gates:
        tail_s = next(it)

        @pl.when(pl.program_id(1) == 0)
        def _():
            tail_s[...] = conv0_ref[...]

    bb, tt, d = x_ref.shape
    rows = bb * tt
    x = x_ref[...]
    if has_prev:
        x = x + gp_ref[...] * yp_ref[...].reshape(bb, tt, d)
        h_ref[...] = x
    hn = _rms(x) * gain_ref[...]
    hn = hn * (1.0 + sc_ref[...]) + sh_ref[...]
    hn2 = hn.reshape(rows, d)
    hn_s[...] = hn2.astype(BF16)
    if has_gates:
        gates = _mm3(hn2, wg_ref[...])
        g_ref[...] = gates.reshape(bb, tt, LANES)
        gt_ref[...] = gates.T[0:N_GATE_COLS, :]
    n_conv = QKV_A // tn if has_gates else 0
    n_tiles = w_ref.shape[-1] // tn

    def project(j):
        return _dot(hn_s[...], w_ref[:, j * tn:(j + 1) * tn]).reshape(bb, tt, tn)

    def finish(j, y):
        cols = slice(j * tn, (j + 1) * tn)
        if j < n_conv:
            xp = jnp.concatenate([tail_s[:, :, cols], y], axis=1)
            acc = None
            for k in range(CONV_W):
                term = convw_ref[k:k + 1, cols] * xp[:, k:k + tt, :]
                acc = term if acc is None else acc + term
            qkv_ref[:, :, cols] = _silu(acc).astype(qkv_ref.dtype)
            tail = y[:, tt - (CONV_W - 1):tt, :]
            tail_s[:, :, cols] = tail
            conv_o[:, :, cols] = tail
        else:
            pcols = slice((j - n_conv) * tn, (j - n_conv + 1) * tn)
            proj_ref[:, :, pcols] = y.astype(proj_ref.dtype)

    y = project(0)
    for j in range(n_tiles):
        y_next = project(j + 1) if j + 1 < n_tiles else None
        finish(j, y)
        y = y_next


def _ln_mm_call(x, prev, gain, sc, sh, w, gates_w, *, bb, tt, tn, out_dtype, name):
    b, t, d = x.shape
    e = w.shape[1]
    rows = bb * tt
    nti = t // tt
    grid = (b // bb, nti)
    has_prev = prev is not None
    has_gates = gates_w is not None
    xspec = pl.BlockSpec((bb, tt, d), lambda bi, i: (bi, i, 0))
    mspec = pl.BlockSpec((bb, 1, d), lambda bi, i: (bi, 0, 0))
    in_specs, args = [xspec], [x]
    if has_prev:
        y2d, off, gp = prev
        in_specs += [pl.BlockSpec((rows, d), lambda bi, i: (off + bi * nti + i, 0)), mspec]
        args += [y2d, gp]
    in_specs += [pl.BlockSpec((1, 1, d), lambda bi, i: (0, 0, 0)), mspec, mspec,
                 pl.BlockSpec((d, e), lambda bi, i: (0, 0), pipeline_mode=pl.Buffered(1))]
    args += [gain.reshape(1, 1, d), sc, sh, w]
    tail_spec = pl.BlockSpec((bb, CONV_W - 1, QKV_A), lambda bi, i: (bi, 0, 0))
    if has_gates:
        wg, conv0, convw = gates_w
        in_specs += [pl.BlockSpec((d, LANES), lambda bi, i: (0, 0)), tail_spec,
                     pl.BlockSpec((CONV_W, QKV_A), lambda bi, i: (0, 0))]
        args += [wg, conv0, convw]
    e_proj = e - QKV_A if has_gates else e
    out_specs, out_shape = [], []
    if has_prev:
        out_specs.append(xspec)
        out_shape.append(jax.ShapeDtypeStruct((b, t, d), F32))
    out_specs.append(pl.BlockSpec((bb, tt, e_proj), lambda bi, i: (bi, i, 0)))
    out_shape.append(jax.ShapeDtypeStruct((b, t, e_proj), out_dtype))
    scratch = [pltpu.VMEM((rows, d), BF16)]
    if has_gates:
        out_specs += [pl.BlockSpec((bb, tt, LANES), lambda bi, i: (bi, i, 0)),
                      pl.BlockSpec((N_GATE_COLS, rows), lambda bi, i: (0, bi * nti + i)),
                      pl.BlockSpec((bb, tt, QKV_A), lambda bi, i: (bi, i, 0)), tail_spec]
        out_shape += [jax.ShapeDtypeStruct((b, t, LANES), F32),
                      jax.ShapeDtypeStruct((N_GATE_COLS, b * t), F32),
                      jax.ShapeDtypeStruct((b, t, QKV_A), BF16),
                      jax.ShapeDtypeStruct((b, CONV_W - 1, QKV_A), F32)]
        scratch.append(pltpu.VMEM((bb, CONV_W - 1, QKV_A), F32))
    return pl.pallas_call(
        functools.partial(_ln_mm_kernel, has_prev=has_prev, has_gates=has_gates, tn=tn),
        grid=grid, in_specs=in_specs, out_specs=out_specs, out_shape=out_shape,
        scratch_shapes=scratch,
        compiler_params=_cparams(("parallel", "arbitrary" if has_gates else "parallel")),
        name=name,
    )(*args)


def _tri_inv_all(a_list, eye, length, mm):
    xs = [-a for a in a_list]
    ps = [eye + x for x in xs]
    n = 2
    while n < length:
        xs = [mm(x, x) for x in xs]
        ps = [p + mm(p, x) for p, x in zip(ps, xs)]
        n *= 2
    return ps


def _mixer_ab_kernel(qkv_ref, z_ref, qkb_ref, vb_ref, ob_ref, g_ref, gt_ref,
                     sa0_ref, cb0_ref, nb0_ref, mb0_ref,
                     gpar_ref, gpart_ref, na_ref, nbn_ref,
                     mixed_ref, sa_o, cb_o, nb_o, mb_o, *, L, CB):
    @pl.when(pl.program_id(1) == 0)
    def _():
        sa_o[...] = sa0_ref[...]
        cb_o[...] = cb0_ref[...]
        nb_o[...] = nb0_ref[...]
        mb_o[...] = mb0_ref[...]

    ii = lax.broadcasted_iota(I32, (L, L), 0)
    jj = lax.broadcasted_iota(I32, (L, L), 1)
    tri = ii >= jj
    strict = ii > jj
    eye = (ii == jj).astype(F32)
    tril_b = tri.astype(BF16)
    triu_b = (ii <= jj).astype(BF16)
    lane = lax.broadcasted_iota(I32, (L, LANES), 1)
    rowi = lax.broadcasted_iota(I32, (N_GATE_COLS, L), 0)
    gpar = gpar_ref[...]
    gpart = gpart_ref[...]
    nea = -jnp.exp(gpar[0:1, :])
    neat = -jnp.exp(gpart[:, 0:1])
    norm_a = na_ref[...]
    norm_b = nbn_ref[...]

    mm = _mmb
    mm_inv = _mm3

    def qkv_tile(r0, col0):
        return qkv_ref[0, r0:r0 + L, col0:col0 + DK_A].astype(F32)

    chunks = range(CB)
    items = [(c, h) for c in chunks for h in range(H_A)]
    rs = [slice(c * L, (c + 1) * L) for c in chunks]

    pre = [g_ref[0, r, :] + gpar[1:2, :] for r in rs]
    csrc = [jnp.where(lane < 4, nea * _softplus(p), jnp.where(lane >= 12, -_softplus(-p), 0.0)) for p in pre]
    gcum = [_mm_mask_l(tril_b, jnp.where(lane < N_GATE_COLS, x, 0.0)) for x in csrc]
    pret = [gt_ref[0, :, r] + gpart[:, 1:2] for r in rs]
    csrct = [jnp.where(rowi < 4, neat * _softplus(p), jnp.where(rowi >= 12, -_softplus(-p), 0.0)) for p in pret]
    gcumt = [_mm_mask_r(x, triu_b) for x in csrct]
    beta_all = [_sigmoid(p) for p in pre]

    q = [qkv_tile(c * L, h * DK_A) for c, h in items]
    k = [qkv_tile(c * L, H_A * DK_A + h * DK_A) for c, h in items]
    v = [qkv_tile(c * L, 2 * H_A * DK_A + h * DV_A) for c, h in items]
    q = [x * lax.rsqrt(jnp.sum(x * x, axis=-1, keepdims=True) + EPS) * (DK_A ** -0.5) for x in q]
    k = [x * lax.rsqrt(jnp.sum(x * x, axis=-1, keepdims=True) + EPS) for x in k]
    gc_c = [gcum[c][:, h:h + 1] for c, h in items]
    gc_r = [gcumt[c][h:h + 1, :] for c, h in items]
    decay = [jnp.where(tri, jnp.exp(jnp.where(tri, a - b, 0.0)), 0.0) for a, b in zip(gc_c, gc_r)]
    beta_c = [beta_all[c][:, 4 + h:5 + h] for c, h in items]
    kb = [x * b for x, b in zip(k, beta_c)]
    a_list = [jnp.where(strict, mm_inv(x, y, NT) * d, 0.0) for x, y, d in zip(kb, k, decay)]
    attn = [jnp.where(tri, mm(x, y, NT) * d, 0.0) for x, y, d in zip(q, k, decay)]
    eg = [jnp.exp(x) for x in gc_c]
    gl = [x[L - 1:L, :] for x in gc_c]
    rhs = [jnp.concatenate([x * e, y * b], axis=-1) for x, e, y, b in zip(kb, eg, v, beta_c)]
    qeg = [x * e for x, e in zip(q, eg)]
    kdec = [x * jnp.exp(g - gc) for x, g, gc in zip(k, gl, gc_c)]
    sdec = [jnp.exp(g) for g in gl]
    tinv = _tri_inv_all(a_list, eye, L, mm)
    sol = [mm(t, r) for t, r in zip(tinv, rhs)]
    resid = [r - (s + mm_inv(a, s)) for r, s, a in zip(rhs, sol, a_list)]
    sol = [s + mm(t, r) for s, t, r in zip(sol, tinv, resid)]

    heads = range(H_B)
    qb = [qkb_ref[0, rs[c], h * DQK_B:(h + 1) * DQK_B] * (DQK_B ** -0.5) for c, h in items]
    kbb = [qkb_ref[0, rs[c], H_B * DQK_B + h * DQK_B:H_B * DQK_B + (h + 1) * DQK_B] for c, h in items]
    b_c = [gcum[c][:, 12 + h:13 + h] for c, h in items]
    dm = [jnp.where(tri, gcum[c][:, 12 + h:13 + h] - gcumt[c][12 + h:13 + h, :] + pret[c][8 + h:9 + h, :], -jnp.inf)
          for c, h in items]
    dmax = [jnp.max(x, axis=-1, keepdims=True) for x in dm]
    qkm = [mm(x, y, NT) for x, y in zip(qb, kbb)]
    li_c = [pre[c][:, 8 + h:9 + h] for c, h in items]

    for c in chunks:
        ia = [c * H_A + h for h in heads]
        s = [sa_o[0, h] for h in heads]
        v_new = [sol[i][:, DK_A:] - mm(sol[i][:, :DK_A], s[h]) for h, i in enumerate(ia)]
        m_prev = [mb_o[0, h] for h in heads]
        a0 = [b_c[i] + m_prev[h] for h, i in enumerate(ia)]
        m_t = [jnp.maximum(a0[h], dmax[i]) for h, i in enumerate(ia)]
        w0 = [jnp.exp(a0[h] - m_t[h]) for h in heads]
        sm = [qkm[i] * jnp.exp(dm[i] - m_t[h]) for h, i in enumerate(ia)]
        cst = [cb_o[0, h] for h in heads]
        nst = [nb_o[0, h] for h in heads]
        vb = [vb_ref[0, rs[c], h * DV_B:(h + 1) * DV_B] for h in heads]
        o_a = [mm(qeg[i], s[h]) + mm(attn[i], v_new[h]) for h, i in enumerate(ia)]
        for h, i in enumerate(ia):
            sa_o[0, h] = s[h] * sdec[i] + mm(kdec[i], v_new[h], TN)
        num = [w0[h] * mm(qb[i], cst[h]) + mm(sm[h], vb[h]) for h, i in enumerate(ia)]
        den = [w0[h] * jnp.sum(qb[i] * nst[h], axis=-1, keepdims=True) + jnp.sum(sm[h], axis=-1, keepdims=True)
               for h, i in enumerate(ia)]
        hh = [num[h] / jnp.maximum(jnp.abs(den[h]), jnp.exp(-m_t[h])) for h in heads]
        m_new = [m_t[h][L - 1:L, :] for h in heads]
        kw = [kbb[i] * jnp.exp(b_c[i][L - 1:L, :] - b_c[i] + li_c[i] - m_new[h]) for h, i in enumerate(ia)]
        for h in heads:
            w0l = w0[h][L - 1:L, :]
            cb_o[0, h] = w0l * cst[h] + mm(kw[h], vb[h], TN)
            nb_o[0, h] = w0l * nst[h] + jnp.sum(kw[h], axis=0, keepdims=True)
            mb_o[0, h] = m_new[h]
        for h in heads:
            zg = z_ref[0, rs[c], h * DV_A:(h + 1) * DV_A]
            mixed_ref[0, rs[c], h * DV_A:(h + 1) * DV_A] = (_rms(o_a[h]) * norm_a * _silu(zg)).astype(mixed_ref.dtype)
            og = ob_ref[0, rs[c], h * DV_B:(h + 1) * DV_B]
            c0 = H_A * DV_A + h * DV_B
            mixed_ref[0, rs[c], c0:c0 + DV_B] = (_rms(hh[h]) * norm_b * _sigmoid(og)).astype(mixed_ref.dtype)


def _mixer_ab_call(qkv, proj, gates, gates_t, sa0, cb0, nb0, mb0, gpar, gpart, norm_a, norm_b, *, L, CB, name):
    b, t, _ = proj.shape
    tb = L * CB
    nt = t // tb

    def col(width, idx):
        return pl.BlockSpec((1, tb, width), lambda bi, i: (bi, i, idx))

    def const(shape):
        return pl.BlockSpec(shape, lambda bi, i: (0,) * len(shape))

    def state(shape):
        return pl.BlockSpec((1,) + shape, lambda bi, i: (bi,) + (0,) * len(shape))

    in_specs = [col(QKV_A, 0), col(512, 0), col(512, 1), col(512, 2), col(512, 3),
                col(LANES, 0), pl.BlockSpec((1, N_GATE_COLS, tb), lambda bi, i: (bi, 0, i)),
                state((H_A, DK_A, DV_A)), state((H_B, DQK_B, DV_B)),
                state((H_B, 1, DQK_B)), state((H_B, 1, 1)),
                const((8, LANES)), const((N_GATE_COLS, LANES)),
                const((1, DV_A)), const((1, DV_B))]
    out_specs = [pl.BlockSpec((1, tb, MIX_AB), lambda bi, i: (bi, i, 0)),
                 state((H_A, DK_A, DV_A)), state((H_B, DQK_B, DV_B)),
                 state((H_B, 1, DQK_B)), state((H_B, 1, 1))]
    out_shape = [jax.ShapeDtypeStruct((b, t, MIX_AB), BF16),
                 jax.ShapeDtypeStruct((b, H_A, DK_A, DV_A), F32),
                 jax.ShapeDtypeStruct((b, H_B, DQK_B, DV_B), F32),
                 jax.ShapeDtypeStruct((b, H_B, 1, DQK_B), F32),
                 jax.ShapeDtypeStruct((b, H_B, 1, 1), F32)]
    return pl.pallas_call(
        functools.partial(_mixer_ab_kernel, L=L, CB=CB),
        grid=(b, nt), in_specs=in_specs, out_specs=out_specs, out_shape=out_shape,
        compiler_params=_cparams(("parallel", "arbitrary")),
        name=name,
    )(qkv, proj, proj, proj, proj, gates, gates_t, sa0, cb0, nb0, mb0, gpar, gpart, norm_a, norm_b)


def _mixer_c_kernel(q_ref, k_ref, v_ref, g_ref, cos_ref, sin_ref, s0_ref, mixed_ref, s_o, *, L, CB):
    @pl.when(pl.program_id(1) == 0)
    def _():
        s_o[...] = s0_ref[...]

    ii = lax.broadcasted_iota(I32, (L, L), 0)
    jj = lax.broadcasted_iota(I32, (L, L), 1)
    rel = (ii - jj).astype(F32)
    idx = lax.broadcasted_iota(I32, (L, 1), 0).astype(F32)
    half = DK_C // 2

    def rope(x, cos, sin):
        x1, x2 = x[:, :half], x[:, half:]
        return jnp.concatenate([x1 * cos - x2 * sin, x1 * sin + x2 * cos], axis=-1)

    for c in range(CB):
        r0 = c * L
        cos = cos_ref[r0:r0 + L, :]
        sin = sin_ref[r0:r0 + L, :]
        for h in range(H_C):
            lg = float(np.log1p(-np.exp2(-5.0 - h)))
            intra = jnp.where(rel >= 0, jnp.exp(lg * jnp.maximum(rel, 0.0)), 0.0)
            q_decay = jnp.exp(lg * (idx + 1.0))
            k_decay = jnp.exp(lg * (L - 1.0 - idx))
            s_decay = float(np.exp(lg * L))
            q = rope(q_ref[0, r0:r0 + L, h * DK_C:(h + 1) * DK_C].astype(F32), cos, sin)
            k = rope(k_ref[0, r0:r0 + L, h * DK_C:(h + 1) * DK_C].astype(F32), cos, sin) * (DK_C ** -0.5)
            v = v_ref[0, r0:r0 + L, h * DV_C:(h + 1) * DV_C]
            s = s_o[0, h]
            o = _mmb(_mmb(q, k, NT) * intra, v) + _mmb(q * q_decay, s)
            s_o[0, h] = s_decay * s + _mmb(k * k_decay, v, TN)
            gg = g_ref[0, r0:r0 + L, h * DV_C:(h + 1) * DV_C].astype(F32)
            o = _rms(o) * _silu(gg)
            mixed_ref[0, r0:r0 + L, h * DV_C:(h + 1) * DV_C] = o.astype(mixed_ref.dtype)


def _mixer_c_call(proj, cos, sin, s0, *, L, CB, name):
    b, t, _ = proj.shape
    tb = L * CB
    nt = t // tb
    qk_w, v_w = H_C * DK_C, H_C * DV_C
    sspec = pl.BlockSpec((1, H_C, DK_C, DV_C), lambda bi, i: (bi, 0, 0, 0))
    tspec = pl.BlockSpec((tb, DK_C // 2), lambda bi, i: (i, 0))
    return pl.pallas_call(
        functools.partial(_mixer_c_kernel, L=L, CB=CB),
        grid=(b, nt),
        in_specs=[pl.BlockSpec((1, tb, qk_w), lambda bi, i: (bi, i, 0)),
                  pl.BlockSpec((1, tb, qk_w), lambda bi, i: (bi, i, 1)),
                  pl.BlockSpec((1, tb, v_w), lambda bi, i: (bi, i, 1)),
                  pl.BlockSpec((1, tb, v_w), lambda bi, i: (bi, i, 2)),
                  tspec, tspec, sspec],
        out_specs=[pl.BlockSpec((1, tb, MIX_C), lambda bi, i: (bi, i, 0)), sspec],
        out_shape=[jax.ShapeDtypeStruct((b, t, MIX_C), BF16),
                   jax.ShapeDtypeStruct((b, H_C, DK_C, DV_C), F32)],
        compiler_params=_cparams(("parallel", "arbitrary")),
        name=name,
    )(proj, proj, proj, proj, cos, sin, s0)


def _out_route_kernel(mixed_ref, wout_ref, h_ref, g1_ref, gain_ref, sc_ref, sh_ref, wrt_ref, rb_ref, hx_in_ref,
                      hnew_ref, hx_ref, bkt_ref):
    del hx_in_ref
    bb, tt, m = mixed_ref.shape
    d = h_ref.shape[-1]
    rows = bb * tt
    y = _dot(mixed_ref[...].reshape(rows, m), wout_ref[...])
    h = h_ref[...] + g1_ref[...] * y.reshape(bb, tt, d)
    hnew_ref[...] = h
    hn = _rms(h) * gain_ref[...]
    hn = (hn * (1.0 + sc_ref[...]) + sh_ref[...]).reshape(rows, d)
    hx_ref[:, 0:d] = hn

    logits = _mm3(wrt_ref[...], hn, NT)
    score = _sigmoid(logits)
    sel = score + rb_ref[...]

    def row(a, r):
        return a[r:r + 1, :]

    gscore = []
    for g in range(N_GROUPS):
        a, b, c, e = (row(sel, EPG * g + i) for i in range(EPG))
        hi_ab, lo_ab, hi_ce, lo_ce = jnp.maximum(a, b), jnp.minimum(a, b), jnp.maximum(c, e), jnp.minimum(c, e)
        top1 = jnp.maximum(hi_ab, hi_ce)
        top2 = jnp.maximum(jnp.maximum(lo_ab, lo_ce), jnp.minimum(hi_ab, hi_ce))
        gscore.append(top1 + top2)
    best = jnp.zeros((1, rows), I32)
    bestv = gscore[0]
    for g in range(1, N_GROUPS):
        upd = gscore[g] > bestv
        best = jnp.where(upd, g, best)
        bestv = jnp.where(upd, gscore[g], bestv)

    def pick(a, i):
        out = row(a, i)
        for g in range(1, N_GROUPS):
            out = jnp.where(best == g, row(a, EPG * g + i), out)
        return out

    vsel = [pick(sel, i) for i in range(EPG)]
    vsc = [pick(score, i) for i in range(EPG)]
    i1 = jnp.zeros((1, rows), I32)
    v1 = vsel[0]
    for i in range(1, EPG):
        upd = vsel[i] > v1
        i1 = jnp.where(upd, i, i1)
        v1 = jnp.where(upd, vsel[i], v1)
    i2 = jnp.full((1, rows), -1, I32)
    v2 = jnp.full((1, rows), -jnp.inf, F32)
    for i in range(EPG):
        cand = jnp.where(i1 == i, -jnp.inf, vsel[i])
        upd = cand > v2
        i2 = jnp.where(upd, i, i2)
        v2 = jnp.where(upd, cand, v2)
    s1 = jnp.zeros((1, rows), F32)
    s2 = jnp.zeros((1, rows), F32)
    for i in range(EPG):
        s1 = jnp.where(i1 == i, vsc[i], s1)
        s2 = jnp.where(i2 == i, vsc[i], s2)
    den = s1 + s2
    w1 = s1 / den
    w2 = s2 / den
    first_lo = i1 < i2
    lo = jnp.minimum(i1, i2)
    hi = jnp.maximum(i1, i2)
    wlo = jnp.where(first_lo, w1, w2)
    whi = jnp.where(first_lo, w2, w1)
    pair = jnp.where(lo == 0, hi - 1, jnp.where(lo == 1, 6 - hi, 5))
    bkt_ref[...] = (best * 6 + pair).reshape(1, 1, rows)
    wa = jnp.where(pair == 5, whi, wlo)
    wb = jnp.where(pair == 5, wlo, whi)

    aux = jnp.concatenate([wa, wb, jnp.zeros((LANES - 2, rows), F32)], axis=0)
    hx_ref[:, d:d + LANES] = aux.T


def _out_route_call(mixed, wout, h, g1, gain, sc, sh, wrt, rb, hx_buf, row_block_off, *, bb, tt, name):
    b, t, m = mixed.shape
    d = h.shape[-1]
    rows = bb * tt
    nti = t // tt
    ntot = hx_buf.shape[0]
    xspec = pl.BlockSpec((bb, tt, d), lambda bi, i: (bi, i, 0))
    mspec = pl.BlockSpec((bb, 1, d), lambda bi, i: (bi, 0, 0))
    return pl.pallas_call(
        _out_route_kernel,
        grid=(b // bb, nti),
        in_specs=[pl.BlockSpec((bb, tt, m), lambda bi, i: (bi, i, 0)),
                  pl.BlockSpec((m, d), lambda bi, i: (0, 0)),
                  xspec, mspec,
                  pl.BlockSpec((1, 1, d), lambda bi, i: (0, 0, 0)), mspec, mspec,
                  pl.BlockSpec((N_EXPERTS, d), lambda bi, i: (0, 0)),
                  pl.BlockSpec((N_EXPERTS, 1), lambda bi, i: (0, 0)),
                  pl.BlockSpec(memory_space=pl.ANY)],
        out_specs=[xspec,
                   pl.BlockSpec((rows, HX_W), lambda bi, i: (row_block_off + bi * nti + i, 0)),
                   pl.BlockSpec((1, 1, rows), lambda bi, i: (bi * nti + i, 0, 0))],
        out_shape=[jax.ShapeDtypeStruct((b, t, d), F32),
                   jax.ShapeDtypeStruct((ntot, HX_W), F32),
                   jax.ShapeDtypeStruct(((b // bb) * nti, 1, rows), I32)],
        input_output_aliases={9: 1},
        compiler_params=_cparams(("parallel", "parallel")),
        name=name,
    )(mixed, wout, h, g1, gain.reshape(1, 1, d), sc, sh, wrt, rb, hx_buf)


def _dispatch_kernel(pos_ref, hx_ref, xs_in, xs_out, sem):
    del xs_in
    i = pl.program_id(0)
    rows = hx_ref.shape[0]

    def row_copy(r):
        p = pos_ref[i * rows + r]
        return pltpu.make_async_copy(hx_ref.at[pl.ds(r, 1), :], xs_out.at[pl.ds(p, 1), :], sem)

    for r in range(rows):
        row_copy(r).start(priority=r % 2)
    for r in range(rows):
        row_copy(r).wait()


def _dispatch_call(pos, hx, xs_buf, rows):
    n = hx.shape[0]
    return pl.pallas_call(
        _dispatch_kernel,
        grid_spec=pltpu.PrefetchScalarGridSpec(
            num_scalar_prefetch=1, grid=(n // rows,),
            in_specs=[pl.BlockSpec((rows, HX_W), lambda i, ps: (i, 0)), pl.BlockSpec(memory_space=pl.ANY)],
            out_specs=pl.BlockSpec(memory_space=pl.ANY),
            scratch_shapes=[pltpu.SemaphoreType.DMA(())]),
        out_shape=jax.ShapeDtypeStruct(xs_buf.shape, F32),
        input_output_aliases={2: 0},
        compiler_params=_cparams(("arbitrary",)),
        name="dispatch",
    )(pos, hx, xs_buf)


def _moe_kernel(exp_ref, chg_ref, nxt_ref, valid_ref, xs_ref, wg_hbm, wu_hbm, wd_hbm,
                ys_ref, wg_f, wu_f, wd_f, wg_s, wu_s, wd_s, sem, *, layer):
    t = pl.program_id(0)

    def fetch(slot, e):
        return (pltpu.make_async_copy(wg_hbm.at[layer, e], wg_f.at[slot], sem.at[slot]),
                pltpu.make_async_copy(wu_hbm.at[layer, e], wu_f.at[slot], sem.at[slot]),
                pltpu.make_async_copy(wd_hbm.at[layer, e], wd_f.at[slot], sem.at[slot]))

    def recast(slot):
        rc = 128
        for i in range(D_MODEL // rc):
            wg_s[slot, i * rc:(i + 1) * rc, :] = wg_f[slot, i * rc:(i + 1) * rc, :].astype(BF16)
            wu_s[slot, i * rc:(i + 1) * rc, :] = wu_f[slot, i * rc:(i + 1) * rc, :].astype(BF16)
        for i in range(D_FF // rc):
            wd_s[slot, i * rc:(i + 1) * rc, :] = wd_f[slot, i * rc:(i + 1) * rc, :].astype(BF16)

    for slot in range(2):
        @pl.when(t == 0)
        def _():
            for cp in fetch(slot, exp_ref[slot, 0]):
                cp.start()

        @pl.when(chg_ref[slot, t] > 0)
        def _():
            for cp in fetch(slot, exp_ref[slot, t]):
                cp.wait()
            recast(slot)

            @pl.when(nxt_ref[slot, t] >= 0)
            def _():
                for cp in fetch(slot, nxt_ref[slot, t]):
                    cp.start()

    @pl.when(valid_ref[t] > 0)
    def _():
        x = xs_ref[:, 0:D_MODEL].astype(BF16)
        w_a = xs_ref[:, D_MODEL:D_MODEL + 1]
        w_b = xs_ref[:, D_MODEL + 1:D_MODEL + 2]
        act_a = (_silu(_dot(x, wg_s[0])) * _dot(x, wu_s[0]) * w_a).astype(BF16)
        act_b = (_silu(_dot(x, wg_s[1])) * _dot(x, wu_s[1]) * w_b).astype(BF16)
        ys_ref[...] = _dot(act_a, wd_s[0]) + _dot(act_b, wd_s[1])

    @pl.when(valid_ref[t] == 0)
    def _():
        ys_ref[...] = jnp.zeros_like(ys_ref)


def _moe_call(experts, chg, nxt, valid, xs, wg, wu, wd, layer):
    p = xs.shape[0]
    n_tiles = p // MOE_TM
    return pl.pallas_call(
        functools.partial(_moe_kernel, layer=layer),
        grid_spec=pltpu.PrefetchScalarGridSpec(
            num_scalar_prefetch=4, grid=(n_tiles,),
            in_specs=[pl.BlockSpec((MOE_TM, HX_W), lambda t, ex, ch, nx, va: (t, 0)),
                      pl.BlockSpec(memory_space=pl.ANY), pl.BlockSpec(memory_space=pl.ANY),
                      pl.BlockSpec(memory_space=pl.ANY)],
            out_specs=pl.BlockSpec((MOE_TM, D_MODEL), lambda t, ex, ch, nx, va: (t, 0)),
            scratch_shapes=[pltpu.VMEM((2, D_MODEL, D_FF), F32), pltpu.VMEM((2, D_MODEL, D_FF), F32),
                            pltpu.VMEM((2, D_FF, D_MODEL), F32),
                            pltpu.VMEM((2, D_MODEL, D_FF), BF16), pltpu.VMEM((2, D_MODEL, D_FF), BF16),
                            pltpu.VMEM((2, D_FF, D_MODEL), BF16),
                            pltpu.SemaphoreType.DMA((2,))]),
        out_shape=jax.ShapeDtypeStruct((p, D_MODEL), F32),
        compiler_params=_cparams(("arbitrary",)),
        name="moe",
    )(experts, chg, nxt, valid, xs, wg, wu, wd)


_PAIRS = [(0, 1), (0, 2), (0, 3), (1, 3), (1, 2), (3, 2)]
_EA_TABLE = np.array([g * EPG + p[0] for g in range(N_GROUPS) for p in _PAIRS], np.int32)
_EB_TABLE = np.array([g * EPG + p[1] for g in range(N_GROUPS) for p in _PAIRS], np.int32)


def _moe(hx, bkt, xs_buf, wg, wu, wd, layer):
    n = hx.shape[0]
    tm = MOE_TM
    ar = jnp.arange(N_BUCKETS, dtype=I32)
    blk = LANES
    nb = -(-n // blk)
    bkt_p = jnp.pad(bkt, (0, nb * blk - n), constant_values=N_BUCKETS)
    oh = (ar[:, None] == bkt_p[None, :]).astype(F32)
    triu = jnp.triu(jnp.ones((blk, blk), F32))
    inner = jnp.einsum('kbj,ji->kbi', oh.reshape(N_BUCKETS, nb, blk), triu)
    tot = inner[:, :, -1]
    outer = jnp.cumsum(tot, axis=1) - tot
    cs = (inner + outer[:, :, None]).reshape(N_BUCKETS, nb * blk)
    counts = (outer[:, -1] + tot[:, -1]).astype(I32)
    padded = ((counts + tm - 1) // tm) * tm
    ends = jnp.cumsum(padded)
    offs = ends - padded
    pos = jnp.sum(oh * (cs - 1.0 + offs.astype(F32)[:, None]), axis=0).astype(I32)[:n]
    n_tiles = -(-n // tm) + N_BUCKETS
    p = n_tiles * tm
    if xs_buf is None:
        xs_buf = jnp.zeros((p, HX_W), F32)
    tstart = jnp.arange(n_tiles, dtype=I32) * tm
    tb = jnp.sum((ends[None, :] <= tstart[:, None]).astype(I32), axis=1)
    n_used = ends[-1] // tm
    valid = (jnp.arange(n_tiles, dtype=I32) < n_used).astype(I32)
    last_b = tb[jnp.maximum(n_used - 1, 0)]
    tb = jnp.clip(jnp.where(valid > 0, tb, last_b), 0, N_BUCKETS - 1)
    ea = jnp.asarray(_EA_TABLE)[tb]
    eb = jnp.asarray(_EB_TABLE)[tb]
    experts = jnp.stack([ea, eb])
    chg = jnp.concatenate([jnp.ones((2, 1), I32), (experts[:, 1:] != experts[:, :-1]).astype(I32)], axis=1)
    tidx = jnp.arange(n_tiles, dtype=I32)
    chg_at = jnp.where(chg > 0, tidx[None, :], n_tiles)
    nxt_at = jnp.concatenate([lax.cummin(chg_at, axis=1, reverse=True)[:, 1:], jnp.full((2, 1), n_tiles, I32)], axis=1)
    nxt = jnp.where(nxt_at < n_tiles,
                    jnp.take_along_axis(experts, jnp.minimum(nxt_at, n_tiles - 1), axis=1), -1)
    xs = _dispatch_call(pos, hx, xs_buf, tm)
    ys = _moe_call(experts, chg, nxt, valid, xs, wg, wu, wd, layer)
    return ys.at[pos].get(mode="promise_in_bounds", unique_indices=True), xs


def _final_kernel(h_ref, y_ref, g_ref, gain_ref, o_ref):
    bb, tt, d = h_ref.shape
    h = h_ref[...] + g_ref[...] * y_ref[...].reshape(bb, tt, d)
    o_ref[...] = _rms(h) * gain_ref[...]


def _final_call(h, y2d, off, g2, gain, *, bb, tt, name):
    b, t, d = h.shape
    nti = t // tt
    rows = bb * tt
    xspec = pl.BlockSpec((bb, tt, d), lambda bi, i: (bi, i, 0))
    return pl.pallas_call(
        _final_kernel,
        grid=(b // bb, nti),
        in_specs=[xspec, pl.BlockSpec((rows, d), lambda bi, i: (off + bi * nti + i, 0)),
                  pl.BlockSpec((bb, 1, d), lambda bi, i: (bi, 0, 0)),
                  pl.BlockSpec((1, 1, d), lambda bi, i: (0, 0, 0))],
        out_specs=xspec,
        out_shape=jax.ShapeDtypeStruct((b, t, d), F32),
        compiler_params=_cparams(("parallel", "parallel")),
        name=name,
    )(h, y2d, g2, gain.reshape(1, 1, d))


def _group_cfg(b, t):
    if t >= 512:
        return dict(bb=1, tt=512, tt_out=1024, L=CHUNK, CB=4, Lc=4 * CHUNK, CBc=1)
    return dict(bb=b, tt=t, tt_out=t, L=min(CHUNK, t), CB=1, Lc=min(CHUNK, t), CBc=1)


def _rope_tables(pos0, t):
    half = DK_C // 2
    inv = jnp.power(ROPE_BASE, -jnp.linspace(0.0, 1.0, half, dtype=F32))
    ang = (pos0 + jnp.arange(t, dtype=F32))[:, None] * inv[None, :]
    return jnp.cos(ang), jnp.sin(ang)


def kernel(x_prompt, x_sample, c_prompt, c_sample, state_a_conv, state_a_rec, state_b_c, state_b_n, state_b_m,
           state_c_rec, w_mod, b_mod, norm_mix, norm_ffn, w_in_ab, conv_a, a_log, dt_bias, norm_a, gate_bias_b,
           norm_b, w_out_ab, w_in_c, w_out_c, w_router, router_bias, w_gate, w_up, w_down, norm_final):
    d = D_MODEL
    bp, tp, _ = x_prompt.shape
    bs, ts, _ = x_sample.shape
    n_ab, n_c = w_in_ab.shape[0], w_in_c.shape[0]

    mod_all = _mod_call(jnp.concatenate([c_prompt, c_sample], axis=0), w_mod, b_mod)

    o = np.cumsum((0,) + AB_SIZES)
    w_ab_main = jnp.concatenate([w_in_ab[:, :, o[0]:o[1]].astype(BF16), w_in_ab[:, :, o[3]:o[7]].astype(BF16),
                                 w_in_ab[:, :, o[8]:o[9]].astype(BF16)], axis=-1)
    w_ab_gate = jnp.concatenate([w_in_ab[:, :, o[1]:o[3]], w_in_ab[:, :, o[7]:o[8]]], axis=-1)
    w_ab_gate_pad = jnp.pad(w_ab_gate, ((0, 0), (0, 0), (0, LANES - N_GATE_COLS)))
    w_c = w_in_c.astype(BF16)
    w_out_ab_b = w_out_ab.astype(BF16)
    w_out_c_b = w_out_c.astype(BF16)
    wrt = w_router.T
    rb = router_bias.reshape(N_EXPERTS, 1)
    zeros4 = jnp.zeros((n_ab, H_A), F32)
    gpar_row0 = jnp.concatenate([a_log, jnp.zeros((n_ab, LANES - H_A), F32)], axis=1)
    gpar_row1 = jnp.concatenate([dt_bias, zeros4, gate_bias_b, jnp.zeros((n_ab, LANES - 16), F32)], axis=1)
    gpar = jnp.concatenate([gpar_row0[:, None], gpar_row1[:, None], jnp.zeros((n_ab, 6, LANES), F32)], axis=1)
    gpart = jnp.swapaxes(jnp.concatenate([gpar[:, :2, :N_GATE_COLS], jnp.zeros((n_ab, LANES - 2, N_GATE_COLS), F32)],
                                         axis=1), 1, 2)

    groups = []
    zf = lambda *s: jnp.zeros(s, F32)
    groups.append(dict(
        x=x_prompt, b=bp, t=tp, pos0=0.0, mod=mod_all[:, :bp], row0=0,
        conv=zf(n_ab, bp, CONV_W - 1, QKV_A), sa=zf(n_ab, bp, H_A, DK_A, DV_A), cb=zf(n_ab, bp, H_B, DQK_B, DV_B),
        nb=zf(n_ab, bp, H_B, DQK_B), mb=zf(n_ab, bp, H_B), sc=zf(n_c, bp, H_C, DK_C, DV_C)))
    groups.append(dict(
        x=x_sample, b=bs, t=ts, pos0=float(PAST_LEN), mod=mod_all[:, bp:], row0=bp * tp,
        conv=state_a_conv, sa=state_a_rec, cb=state_b_c, nb=state_b_n, mb=state_b_m, sc=state_c_rec))
    n_tot = bp * tp + bs * ts
    for g in groups:
        g.update(_group_cfg(g["b"], g["t"]))
        g["h"] = g["x"]
        g["prev"] = None
        g["new_ab"], g["new_c"] = [], []
        g["cos"], g["sin"] = _rope_tables(g["pos0"], g["t"])
        g["roff"] = g["row0"] // (g["bb"] * g["tt"])

    xs_buf = None
    for layer in range(DEPTH):
        li = layer // 2
        if layer == 0:
            hx_buf = jnp.zeros((n_tot, HX_W), F32)
        bkts = []
        for gi, g in enumerate(groups):
            b, t, bb, tt = g["b"], g["t"], g["bb"], g["tt"]
            mods = [m.reshape(b, 1, d) for m in jnp.split(g["mod"][layer], N_MOD, axis=-1)]
            sh1, sc1, g1, sh2, sc2, g2 = mods
            tag = f"l{layer}g{gi}"
            if layer % 2 == 0:
                outs = _ln_mm_call(g["h"], g["prev"], norm_mix[layer], sc1, sh1, w_ab_main[li],
                                   (w_ab_gate_pad[li], g["conv"][li], conv_a[li]), bb=bb, tt=tt, tn=512,
                                   out_dtype=F32, name="ln_ab_" + tag)
                if g["prev"] is not None:
                    g["h"], outs = outs[0], outs[1:]
                proj, gates, gates_t, qkv, conv_n = outs
                gates_t = gates_t.reshape(N_GATE_COLS, b, t).transpose(1, 0, 2)
                mixed, sa_n, cb_n, nb_n, mb_n = _mixer_ab_call(
                    qkv, proj, gates, gates_t, g["sa"][li], g["cb"][li],
                    g["nb"][li].reshape(b, H_B, 1, DQK_B), g["mb"][li].reshape(b, H_B, 1, 1),
                    gpar[li], gpart[li], norm_a[li].reshape(1, DV_A), norm_b[li].reshape(1, DV_B),
                    L=g["L"], CB=g["CB"], name="mixer_ab_" + tag)
                g["new_ab"].append((conv_n, sa_n, cb_n, nb_n.reshape(b, H_B, DQK_B), mb_n.reshape(b, H_B)))
                wout = w_out_ab_b[li]
            else:
                outs = _ln_mm_call(g["h"], g["prev"], norm_mix[layer], sc1, sh1, w_c[li], None,
                                   bb=bb, tt=tt, tn=512, out_dtype=BF16, name="ln_c_" + tag)
                if g["prev"] is not None:
                    g["h"], outs = outs[0], outs[1:]
                (proj,) = outs
                mixed, sc_n = _mixer_c_call(proj, g["cos"], g["sin"], g["sc"][li], L=g["Lc"], CB=g["CBc"],
                                            name="mixer_c_" + tag)
                g["new_c"].append(sc_n)
                wout = w_out_c_b[li]
            tto = g["tt_out"]
            g["h"], hx_buf, bkt = _out_route_call(mixed, wout, g["h"], g1, norm_ffn[layer], sc2, sh2, wrt, rb,
                                                  hx_buf, g["row0"] // (bb * tto), bb=bb, tt=tto,
                                                  name="out_route_" + tag)
            bkts.append(bkt.reshape(-1))
            g["g2"] = g2
        y_moe, xs_buf = _moe(hx_buf, jnp.concatenate(bkts), xs_buf, w_gate, w_up, w_down, layer)
        for g in groups:
            g["prev"] = (y_moe, g["roff"], g["g2"])

    outs = []
    for gi, g in enumerate(groups):
        y2d, off, g2 = g["prev"]
        y = _final_call(g["h"], y2d, off, g2, norm_final, bb=g["bb"], tt=g["tt"], name=f"final_g{gi}")
        ab = [jnp.stack(s) for s in zip(*g["new_ab"])]
        outs.append((y, ab[0], ab[1], ab[2], ab[3], ab[4], jnp.stack(g["new_c"])))
    p, s = outs
    return (p[0], s[0], p[1], p[2], p[3], p[4], p[5], p[6], s[1], s[2], s[3], s[4], s[5], s[6])
```

```python
import functools

import numpy as np
import jax
import jax.numpy as jnp
from jax import lax
from jax.experimental import pallas as pl
from jax.experimental.pallas import tpu as pltpu

F32 = jnp.float32
BF16 = jnp.bfloat16
I32 = jnp.int32

D_MODEL = 1024
DEPTH = 2
CHUNK = 64
H_A, DK_A, DV_A, CONV_W = 4, 128, 128, 4
QKV_A = H_A * (2 * DK_A + DV_A)
H_B, DQK_B, DV_B = 4, 64, 128
H_C, DK_C, DV_C = 4, 256, 512
ROPE_BASE = 10000.0
PAST_LEN = 4096
AB_SIZES = (QKV_A, H_A, H_A, H_A * DV_A, H_B * DQK_B, H_B * DQK_B, H_B * DV_B, 2 * H_B, H_B * DV_B)
MIX_AB = H_A * DV_A + H_B * DV_B
MIX_C = H_C * DV_C
IN_C = 2 * H_C * DK_C + 2 * H_C * DV_C
N_EXPERTS, N_GROUPS, EPG, D_FF = 16, 4, 4, 512
N_MOD = 6
EPS = 1e-6

LANES = 128
AB_MAIN = 3584
N_GATE_COLS = 16
HX_W = D_MODEL + LANES
N_BUCKETS = N_GROUPS * 6
MOE_TM = 256
VMEM_LIMIT = 48 * 1024 * 1024

NN = ((1,), (0,))
NT = ((1,), (1,))
TN = ((0,), (0,))


def _dot(a, b, dims=NN):
    return lax.dot_general(a, b, (dims, ((), ())), preferred_element_type=F32)


def _mmb(a, b, dims=NN):
    return _dot(a.astype(BF16), b.astype(BF16), dims)


def _split2(x):
    hi = x.astype(BF16)
    lo = (x - hi.astype(F32)).astype(BF16)
    return hi, lo


def _split3(x):
    hi = x.astype(BF16)
    r = x - hi.astype(F32)
    mid = r.astype(BF16)
    lo = (r - mid.astype(F32)).astype(BF16)
    return hi, mid, lo


def _mm3(a, b, dims=NN):
    ah, al = _split2(a)
    bh, bl = _split2(b)
    return _dot(ah, bh, dims) + (_dot(ah, bl, dims) + _dot(al, bh, dims))


def _mm_mask_l(mask_bf16, x, dims=NN):
    h, m, l = _split3(x)
    return _dot(mask_bf16, h, dims) + (_dot(mask_bf16, m, dims) + _dot(mask_bf16, l, dims))


def _mm_mask_r(x, mask_bf16):
    h, m, l = _split3(x)
    return _dot(h, mask_bf16) + (_dot(m, mask_bf16) + _dot(l, mask_bf16))


def _sigmoid(x):
    return 1.0 / (1.0 + jnp.exp(-x))


def _silu(x):
    return x * _sigmoid(x)


def _softplus(x):
    return jnp.maximum(x, 0.0) + jnp.log(1.0 + jnp.exp(-jnp.abs(x)))


def _rms(x, eps=EPS):
    return x * lax.rsqrt(jnp.mean(x * x, axis=-1, keepdims=True) + eps)


def _cparams(sem):
    return pltpu.CompilerParams(dimension_semantics=sem, vmem_limit_bytes=VMEM_LIMIT)


def _mod_kernel(c_ref, w_ref, b_ref, o_ref):
    c = c_ref[...]
    o_ref[0] = _mm3(_silu(c), w_ref[0]) + b_ref[0]


def _mod_call(c_all, w_mod, b_mod):
    bt = c_all.shape[0]
    e = w_mod.shape[-1]
    tn = 1024
    return pl.pallas_call(
        _mod_kernel,
        grid=(DEPTH, e // tn),
        in_specs=[pl.BlockSpec((bt, D_MODEL), lambda l, j: (0, 0)),
                  pl.BlockSpec((1, D_MODEL, tn), lambda l, j: (l, 0, j)),
                  pl.BlockSpec((1, 1, tn), lambda l, j: (l, 0, j))],
        out_specs=pl.BlockSpec((1, bt, tn), lambda l, j: (l, 0, j)),
        out_shape=jax.ShapeDtypeStruct((DEPTH, bt, e), F32),
        compiler_params=_cparams(("parallel", "parallel")),
        name="mod",
    )(c_all, w_mod, b_mod.reshape(DEPTH, 1, e))


def _ln_mm_kernel(*refs, has_prev, has_gates, tn):
    it = iter(refs)
    x_ref = next(it)
    if has_prev:
        yp_ref, gp_ref = next(it), next(it)
    gain_ref, sc_ref, sh_ref, w_ref = next(it), next(it), next(it), next(it)
    if has_gates:
        wg_ref, conv0_ref, convw_ref = next(it), next(it), next(it)
    if has_prev:
        h_ref = next(it)
    proj_ref = next(it)
    if has_gates:
        g_ref, gt_ref, qkv_ref, conv_o = next(it), next(it), next(it), next(it)
    hn_s = next(it)
    if has_gates:
        tail_s = next(it)

        @pl.when(pl.program_id(1) == 0)
        def _():
            tail_s[...] = conv0_ref[...]

    bb, tt, d = x_ref.shape
    rows = bb * tt
    x = x_ref[...]
    if has_prev:
        x = x + gp_ref[...] * yp_ref[...].reshape(bb, tt, d)
        h_ref[...] = x
    hn = _rms(x) * gain_ref[...]
    hn = hn * (1.0 + sc_ref[...]) + sh_ref[...]
    hn2 = hn.reshape(rows, d)
    hn_s[...] = hn2.astype(BF16)
    if has_gates:
        gates = _mm3(hn2, wg_ref[...])
        g_ref[...] = gates.reshape(bb, tt, LANES)
        gt_ref[...] = gates.T[0:N_GATE_COLS, :]
    n_conv = QKV_A // tn if has_gates else 0
    n_tiles = w_ref.shape[-1] // tn

    def project(j):
        return _dot(hn_s[...], w_ref[:, j * tn:(j + 1) * tn]).reshape(bb, tt, tn)

    def finish(j, y):
        cols = slice(j * tn, (j + 1) * tn)
        if j < n_conv:
            xp = jnp.concatenate([tail_s[:, :, cols], y], axis=1)
            acc = None
            for k in range(CONV_W):
                term = convw_ref[k:k + 1, cols] * xp[:, k:k + tt, :]
                acc = term if acc is None else acc + term
            qkv_ref[:, :, cols] = _silu(acc).astype(qkv_ref.dtype)
            tail = y[:, tt - (CONV_W - 1):tt, :]
            tail_s[:, :, cols] = tail
            conv_o[:, :, cols] = tail
        else:
            pcols = slice((j - n_conv) * tn, (j - n_conv + 1) * tn)
            proj_ref[:, :, pcols] = y.astype(proj_ref.dtype)

    y = project(0)
    for j in range(n_tiles):
        y_next = project(j + 1) if j + 1 < n_tiles else None
        finish(j, y)
        y = y_next


def _ln_mm_call(x, prev, gain, sc, sh, w, gates_w, *, bb, tt, tn, out_dtype, name):
    b, t, d = x.shape
    e = w.shape[1]
    rows = bb * tt
    nti = t // tt
    grid = (b // bb, nti)
    has_prev = prev is not None
    has_gates = gates_w is not None
    xspec = pl.BlockSpec((bb, tt, d), lambda bi, i: (bi, i, 0))
    mspec = pl.BlockSpec((bb, 1, d), lambda bi, i: (bi, 0, 0))
    in_specs, args = [xspec], [x]
    if has_prev:
        y2d, off, gp = prev
        in_specs += [pl.BlockSpec((rows, d), lambda bi, i: (off + bi * nti + i, 0)), mspec]
        args += [y2d, gp]
    in_specs += [pl.BlockSpec((1, 1, d), lambda bi, i: (0, 0, 0)), mspec, mspec,
                 pl.BlockSpec((d, e), lambda bi, i: (0, 0), pipeline_mode=pl.Buffered(1))]
    args += [gain.reshape(1, 1, d), sc, sh, w]
    tail_spec = pl.BlockSpec((bb, CONV_W - 1, QKV_A), lambda bi, i: (bi, 0, 0))
    if has_gates:
        wg, conv0, convw = gates_w
        in_specs += [pl.BlockSpec((d, LANES), lambda bi, i: (0, 0)), tail_spec,
                     pl.BlockSpec((CONV_W, QKV_A), lambda bi, i: (0, 0))]
        args += [wg, conv0, convw]
    e_proj = e - QKV_A if has_gates else e
    out_specs, out_shape = [], []
    if has_prev:
        out_specs.append(xspec)
        out_shape.append(jax.ShapeDtypeStruct((b, t, d), F32))
    out_specs.append(pl.BlockSpec((bb, tt, e_proj), lambda bi, i: (bi, i, 0)))
    out_shape.append(jax.ShapeDtypeStruct((b, t, e_proj), out_dtype))
    scratch = [pltpu.VMEM((rows, d), BF16)]
    if has_gates:
        out_specs += [pl.BlockSpec((bb, tt, LANES), lambda bi, i: (bi, i, 0)),
                      pl.BlockSpec((N_GATE_COLS, rows), lambda bi, i: (0, bi * nti + i)),
                      pl.BlockSpec((bb, tt, QKV_A), lambda bi, i: (bi, i, 0)), tail_spec]
        out_shape += [jax.ShapeDtypeStruct((b, t, LANES), F32),
                      jax.ShapeDtypeStruct((N_GATE_COLS, b * t), F32),
                      jax.ShapeDtypeStruct((b, t, QKV_A), BF16),
                      jax.ShapeDtypeStruct((b, CONV_W - 1, QKV_A), F32)]
        scratch.append(pltpu.VMEM((bb, CONV_W - 1, QKV_A), F32))
    return pl.pallas_call(
        functools.partial(_ln_mm_kernel, has_prev=has_prev, has_gates=has_gates, tn=tn),
        grid=grid, in_specs=in_specs, out_specs=out_specs, out_shape=out_shape,
        scratch_shapes=scratch,
        compiler_params=_cparams(("parallel", "arbitrary" if has_gates else "parallel")),
        name=name,
    )(*args)


def _tri_inv_all(a_list, eye, length, mm):
    xs = [-a for a in a_list]
    ps = [eye + x for x in xs]
    n = 2
    while n < length:
        xs = [mm(x, x) for x in xs]
        ps = [p + mm(p, x) for p, x in zip(ps, xs)]
        n *= 2
    return ps


def _mixer_ab_kernel(qkv_ref, z_ref, qkb_ref, vb_ref, ob_ref, g_ref, gt_ref,
                     sa0_ref, cb0_ref, nb0_ref, mb0_ref,
                     gpar_ref, gpart_ref, na_ref, nbn_ref,
                     mixed_ref, sa_o, cb_o, nb_o, mb_o, *, L, CB):
    @pl.when(pl.program_id(1) == 0)
    def _():
        sa_o[...] = sa0_ref[...]
        cb_o[...] = cb0_ref[...]
        nb_o[...] = nb0_ref[...]
        mb_o[...] = mb0_ref[...]

    ii = lax.broadcasted_iota(I32, (L, L), 0)
    jj = lax.broadcasted_iota(I32, (L, L), 1)
    tri = ii >= jj
    strict = ii > jj
    eye = (ii == jj).astype(F32)
    tril_b = tri.astype(BF16)
    triu_b = (ii <= jj).astype(BF16)
    lane = lax.broadcasted_iota(I32, (L, LANES), 1)
    rowi = lax.broadcasted_iota(I32, (N_GATE_COLS, L), 0)
    gpar = gpar_ref[...]
    gpart = gpart_ref[...]
    nea = -jnp.exp(gpar[0:1, :])
    neat = -jnp.exp(gpart[:, 0:1])
    norm_a = na_ref[...]
    norm_b = nbn_ref[...]

    mm = _mmb
    mm_inv = _mm3

    def qkv_tile(r0, col0):
        return qkv_ref[0, r0:r0 + L, col0:col0 + DK_A].astype(F32)

    chunks = range(CB)
    items = [(c, h) for c in chunks for h in range(H_A)]
    rs = [slice(c * L, (c + 1) * L) for c in chunks]

    pre = [g_ref[0, r, :] + gpar[1:2, :] for r in rs]
    csrc = [jnp.where(lane < 4, nea * _softplus(p), jnp.where(lane >= 12, -_softplus(-p), 0.0)) for p in pre]
    gcum = [_mm_mask_l(tril_b, jnp.where(lane < N_GATE_COLS, x, 0.0)) for x in csrc]
    pret = [gt_ref[0, :, r] + gpart[:, 1:2] for r in rs]
    csrct = [jnp.where(rowi < 4, neat * _softplus(p), jnp.where(rowi >= 12, -_softplus(-p), 0.0)) for p in pret]
    gcumt = [_mm_mask_r(x, triu_b) for x in csrct]
    beta_all = [_sigmoid(p) for p in pre]

    q = [qkv_tile(c * L, h * DK_A) for c, h in items]
    k = [qkv_tile(c * L, H_A * DK_A + h * DK_A) for c, h in items]
    v = [qkv_tile(c * L, 2 * H_A * DK_A + h * DV_A) for c, h in items]
    q = [x * lax.rsqrt(jnp.sum(x * x, axis=-1, keepdims=True) + EPS) * (DK_A ** -0.5) for x in q]
    k = [x * lax.rsqrt(jnp.sum(x * x, axis=-1, keepdims=True) + EPS) for x in k]
    gc_c = [gcum[c][:, h:h + 1] for c, h in items]
    gc_r = [gcumt[c][h:h + 1, :] for c, h in items]
    decay = [jnp.where(tri, jnp.exp(jnp.where(tri, a - b, 0.0)), 0.0) for a, b in zip(gc_c, gc_r)]
    beta_c = [beta_all[c][:, 4 + h:5 + h] for c, h in items]
    kb = [x * b for x, b in zip(k, beta_c)]
    a_list = [jnp.where(strict, mm_inv(x, y, NT) * d, 0.0) for x, y, d in zip(kb, k, decay)]
    attn = [jnp.where(tri, mm(x, y, NT) * d, 0.0) for x, y, d in zip(q, k, decay)]
    eg = [jnp.exp(x) for x in gc_c]
    gl = [x[L - 1:L, :] for x in gc_c]
    rhs = [jnp.concatenate([x * e, y * b], axis=-1) for x, e, y, b in zip(kb, eg, v, beta_c)]
    qeg = [x * e for x, e in zip(q, eg)]
    kdec = [x * jnp.exp(g - gc) for x, g, gc in zip(k, gl, gc_c)]
    sdec = [jnp.exp(g) for g in gl]
    tinv = _tri_inv_all(a_list, eye, L, mm)
    sol = [mm(t, r) for t, r in zip(tinv, rhs)]
    resid = [r - (s + mm_inv(a, s)) for r, s, a in zip(rhs, sol, a_list)]
    sol = [s + mm(t, r) for s, t, r in zip(sol, tinv, resid)]

    heads = range(H_B)
    qb = [qkb_ref[0, rs[c], h * DQK_B:(h + 1) * DQK_B] * (DQK_B ** -0.5) for c, h in items]
    kbb = [qkb_ref[0, rs[c], H_B * DQK_B + h * DQK_B:H_B * DQK_B + (h + 1) * DQK_B] for c, h in items]
    b_c = [gcum[c][:, 12 + h:13 + h] for c, h in items]
    dm = [jnp.where(tri, gcum[c][:, 12 + h:13 + h] - gcumt[c][12 + h:13 + h, :] + pret[c][8 + h:9 + h, :], -jnp.inf)
          for c, h in items]
    dmax = [jnp.max(x, axis=-1, keepdims=True) for x in dm]
    qkm = [mm(x, y, NT) for x, y in zip(qb, kbb)]
    li_c = [pre[c][:, 8 + h:9 + h] for c, h in items]

    for c in chunks:
        ia = [c * H_A + h for h in heads]
        s = [sa_o[0, h] for h in heads]
        v_new = [sol[i][:, DK_A:] - mm(sol[i][:, :DK_A], s[h]) for h, i in enumerate(ia)]
        m_prev = [mb_o[0, h] for h in heads]
        a0 = [b_c[i] + m_prev[h] for h, i in enumerate(ia)]
        m_t = [jnp.maximum(a0[h], dmax[i]) for h, i in enumerate(ia)]
        w0 = [jnp.exp(a0[h] - m_t[h]) for h in heads]
        sm = [qkm[i] * jnp.exp(dm[i] - m_t[h]) for h, i in enumerate(ia)]
        cst = [cb_o[0, h] for h in heads]
        nst = [nb_o[0, h] for h in heads]
        vb = [vb_ref[0, rs[c], h * DV_B:(h + 1) * DV_B] for h in heads]
        o_a = [mm(qeg[i], s[h]) + mm(attn[i], v_new[h]) for h, i in enumerate(ia)]
        for h, i in enumerate(ia):
            sa_o[0, h] = s[h] * sdec[i] + mm(kdec[i], v_new[h], TN)
        num = [w0[h] * mm(qb[i], cst[h]) + mm(sm[h], vb[h]) for h, i in enumerate(ia)]
        den = [w0[h] * jnp.sum(qb[i] * nst[h], axis=-1, keepdims=True) + jnp.sum(sm[h], axis=-1, keepdims=True)
               for h, i in enumerate(ia)]
        hh = [num[h] / jnp.maximum(jnp.abs(den[h]), jnp.exp(-m_t[h])) for h in heads]
        m_new = [m_t[h][L - 1:L, :] for h in heads]
        kw = [kbb[i] * jnp.exp(b_c[i][L - 1:L, :] - b_c[i] + li_c[i] - m_new[h]) for h, i in enumerate(ia)]
        for h in heads:
            w0l = w0[h][L - 1:L, :]
            cb_o[0, h] = w0l * cst[h] + mm(kw[h], vb[h], TN)
            nb_o[0, h] = w0l * nst[h] + jnp.sum(kw[h], axis=0, keepdims=True)
            mb_o[0, h] = m_new[h]
        for h in heads:
            zg = z_ref[0, rs[c], h * DV_A:(h + 1) * DV_A]
            mixed_ref[0, rs[c], h * DV_A:(h + 1) * DV_A] = (_rms(o_a[h]) * norm_a * _silu(zg)).astype(mixed_ref.dtype)
            og = ob_ref[0, rs[c], h * DV_B:(h + 1) * DV_B]
            c0 = H_A * DV_A + h * DV_B
            mixed_ref[0, rs[c], c0:c0 + DV_B] = (_rms(hh[h]) * norm_b * _sigmoid(og)).astype(mixed_ref.dtype)


def _mixer_ab_call(qkv, proj, gates, gates_t, sa0, cb0, nb0, mb0, gpar, gpart, norm_a, norm_b, *, L, CB, name):
    b, t, _ = proj.shape
    tb = L * CB
    nt = t // tb

    def col(width, idx):
        return pl.BlockSpec((1, tb, width), lambda bi, i: (bi, i, idx))

    def const(shape):
        return pl.BlockSpec(shape, lambda bi, i: (0,) * len(shape))

    def state(shape):
        return pl.BlockSpec((1,) + shape, lambda bi, i: (bi,) + (0,) * len(shape))

    in_specs = [col(QKV_A, 0), col(512, 0), col(512, 1), col(512, 2), col(512, 3),
                col(LANES, 0), pl.BlockSpec((1, N_GATE_COLS, tb), lambda bi, i: (bi, 0, i)),
                state((H_A, DK_A, DV_A)), state((H_B, DQK_B, DV_B)),
                state((H_B, 1, DQK_B)), state((H_B, 1, 1)),
                const((8, LANES)), const((N_GATE_COLS, LANES)),
                const((1, DV_A)), const((1, DV_B))]
    out_specs = [pl.BlockSpec((1, tb, MIX_AB), lambda bi, i: (bi, i, 0)),
                 state((H_A, DK_A, DV_A)), state((H_B, DQK_B, DV_B)),
                 state((H_B, 1, DQK_B)), state((H_B, 1, 1))]
    out_shape = [jax.ShapeDtypeStruct((b, t, MIX_AB), BF16),
                 jax.ShapeDtypeStruct((b, H_A, DK_A, DV_A), F32),
                 jax.ShapeDtypeStruct((b, H_B, DQK_B, DV_B), F32),
                 jax.ShapeDtypeStruct((b, H_B, 1, DQK_B), F32),
                 jax.ShapeDtypeStruct((b, H_B, 1, 1), F32)]
    return pl.pallas_call(
        functools.partial(_mixer_ab_kernel, L=L, CB=CB),
        grid=(b, nt), in_specs=in_specs, out_specs=out_specs, out_shape=out_shape,
        compiler_params=_cparams(("parallel", "arbitrary")),
        name=name,
    )(qkv, proj, proj, proj, proj, gates, gates_t, sa0, cb0, nb0, mb0, gpar, gpart, norm_a, norm_b)


def _mixer_c_kernel(q_ref, k_ref, v_ref, g_ref, cos_ref, sin_ref, s0_ref, mixed_ref, s_o, *, L, CB):
    @pl.when(pl.program_id(1) == 0)
    def _():
        s_o[...] = s0_ref[...]

    ii = lax.broadcasted_iota(I32, (L, L), 0)
    jj = lax.broadcasted_iota(I32, (L, L), 1)
    rel = (ii - jj).astype(F32)
    idx = lax.broadcasted_iota(I32, (L, 1), 0).astype(F32)
    half = DK_C // 2

    def rope(x, cos, sin):
        x1, x2 = x[:, :half], x[:, half:]
        return jnp.concatenate([x1 * cos - x2 * sin, x1 * sin + x2 * cos], axis=-1)

    for c in range(CB):
        r0 = c * L
        cos = cos_ref[r0:r0 + L, :]
        sin = sin_ref[r0:r0 + L, :]
        for h in range(H_C):
            lg = float(np.log1p(-np.exp2(-5.0 - h)))
            intra = jnp.where(rel >= 0, jnp.exp(lg * jnp.maximum(rel, 0.0)), 0.0)
            q_decay = jnp.exp(lg * (idx + 1.0))
            k_decay = jnp.exp(lg * (L - 1.0 - idx))
            s_decay = float(np.exp(lg * L))
            q = rope(q_ref[0, r0:r0 + L, h * DK_C:(h + 1) * DK_C].astype(F32), cos, sin)
            k = rope(k_ref[0, r0:r0 + L, h * DK_C:(h + 1) * DK_C].astype(F32), cos, sin) * (DK_C ** -0.5)
            v = v_ref[0, r0:r0 + L, h * DV_C:(h + 1) * DV_C]
            s = s_o[0, h]
            o = _mmb(_mmb(q, k, NT) * intra, v) + _mmb(q * q_decay, s)
            s_o[0, h] = s_decay * s + _mmb(k * k_decay, v, TN)
            gg = g_ref[0, r0:r0 + L, h * DV_C:(h + 1) * DV_C].astype(F32)
            o = _rms(o) * _silu(gg)
            mixed_ref[0, r0:r0 + L, h * DV_C:(h + 1) * DV_C] = o.astype(mixed_ref.dtype)


def _mixer_c_call(proj, cos, sin, s0, *, L, CB, name):
    b, t, _ = proj.shape
    tb = L * CB
    nt = t // tb
    qk_w, v_w = H_C * DK_C, H_C * DV_C
    sspec = pl.BlockSpec((1, H_C, DK_C, DV_C), lambda bi, i: (bi, 0, 0, 0))
    tspec = pl.BlockSpec((tb, DK_C // 2), lambda bi, i: (i, 0))
    return pl.pallas_call(
        functools.partial(_mixer_c_kernel, L=L, CB=CB),
        grid=(b, nt),
        in_specs=[pl.BlockSpec((1, tb, qk_w), lambda bi, i: (bi, i, 0)),
                  pl.BlockSpec((1, tb, qk_w), lambda bi, i: (bi, i, 1)),
                  pl.BlockSpec((1, tb, v_w), lambda bi, i: (bi, i, 1)),
                  pl.BlockSpec((1, tb, v_w), lambda bi, i: (bi, i, 2)),
                  tspec, tspec, sspec],
        out_specs=[pl.BlockSpec((1, tb, MIX_C), lambda bi, i: (bi, i, 0)), sspec],
        out_shape=[jax.ShapeDtypeStruct((b, t, MIX_C), BF16),
                   jax.ShapeDtypeStruct((b, H_C, DK_C, DV_C), F32)],
        compiler_params=_cparams(("parallel", "arbitrary")),
        name=name,
    )(proj, proj, proj, proj, cos, sin, s0)


def _out_route_kernel(mixed_ref, wout_ref, h_ref, g1_ref, gain_ref, sc_ref, sh_ref, wrt_ref, rb_ref,
                      hnew_ref, hx_ref, bkt_ref):
    bb, tt, m = mixed_ref.shape
    d = h_ref.shape[-1]
    rows = bb * tt
    y = _dot(mixed_ref[...].reshape(rows, m), wout_ref[...])
    h = h_ref[...] + g1_ref[...] * y.reshape(bb, tt, d)
    hnew_ref[...] = h
    hn = _rms(h) * gain_ref[...]
    hn = (hn * (1.0 + sc_ref[...]) + sh_ref[...]).reshape(rows, d)
    hx_ref[:, 0:d] = hn

    logits = _mm3(wrt_ref[...], hn, NT)
    score = _sigmoid(logits)
    sel = score + rb_ref[...]

    def row(a, r):
        return a[r:r + 1, :]

    gscore = []
    for g in range(N_GROUPS):
        a, b, c, e = (row(sel, EPG * g + i) for i in range(EPG))
        hi_ab, lo_ab, hi_ce, lo_ce = jnp.maximum(a, b), jnp.minimum(a, b), jnp.maximum(c, e), jnp.minimum(c, e)
        top1 = jnp.maximum(hi_ab, hi_ce)
        top2 = jnp.maximum(jnp.maximum(lo_ab, lo_ce), jnp.minimum(hi_ab, hi_ce))
        gscore.append(top1 + top2)
    best = jnp.zeros((1, rows), I32)
    bestv = gscore[0]
    for g in range(1, N_GROUPS):
        upd = gscore[g] > bestv
        best = jnp.where(upd, g, best)
        bestv = jnp.where(upd, gscore[g], bestv)

    def pick(a, i):
        out = row(a, i)
        for g in range(1, N_GROUPS):
            out = jnp.where(best == g, row(a, EPG * g + i), out)
        return out

    vsel = [pick(sel, i) for i in range(EPG)]
    vsc = [pick(score, i) for i in range(EPG)]
    i1 = jnp.zeros((1, rows), I32)
    v1 = vsel[0]
    for i in range(1, EPG):
        upd = vsel[i] > v1
        i1 = jnp.where(upd, i, i1)
        v1 = jnp.where(upd, vsel[i], v1)
    i2 = jnp.full((1, rows), -1, I32)
    v2 = jnp.full((1, rows), -jnp.inf, F32)
    for i in range(EPG):
        cand = jnp.where(i1 == i, -jnp.inf, vsel[i])
        upd = cand > v2
        i2 = jnp.where(upd, i, i2)
        v2 = jnp.where(upd, cand, v2)
    s1 = jnp.zeros((1, rows), F32)
    s2 = jnp.zeros((1, rows), F32)
    for i in range(EPG):
        s1 = jnp.where(i1 == i, vsc[i], s1)
        s2 = jnp.where(i2 == i, vsc[i], s2)
    den = s1 + s2
    w1 = s1 / den
    w2 = s2 / den
    first_lo = i1 < i2
    lo = jnp.minimum(i1, i2)
    hi = jnp.maximum(i1, i2)
    wlo = jnp.where(first_lo, w1, w2)
    whi = jnp.where(first_lo, w2, w1)
    pair = jnp.where(lo == 0, hi - 1, jnp.where(lo == 1, 6 - hi, 5))
    bkt_ref[...] = (best * 6 + pair).reshape(1, 1, rows)
    wa = jnp.where(pair == 5, whi, wlo)
    wb = jnp.where(pair == 5, wlo, whi)

    aux = jnp.concatenate([wa, wb, jnp.zeros((LANES - 2, rows), F32)], axis=0)
    hx_ref[:, d:d + LANES] = aux.T


def _out_route_call(mixed, wout, h, g1, gain, sc, sh, wrt, rb, *, bb, tt, name):
    b, t, m = mixed.shape
    d = h.shape[-1]
    rows = bb * tt
    nti = t // tt
    xspec = pl.BlockSpec((bb, tt, d), lambda bi, i: (bi, i, 0))
    mspec = pl.BlockSpec((bb, 1, d), lambda bi, i: (bi, 0, 0))
    return pl.pallas_call(
        _out_route_kernel,
        grid=(b // bb, nti),
        in_specs=[pl.BlockSpec((bb, tt, m), lambda bi, i: (bi, i, 0)),
                  pl.BlockSpec((m, d), lambda bi, i: (0, 0)),
                  xspec, mspec,
                  pl.BlockSpec((1, 1, d), lambda bi, i: (0, 0, 0)), mspec, mspec,
                  pl.BlockSpec((N_EXPERTS, d), lambda bi, i: (0, 0)),
                  pl.BlockSpec((N_EXPERTS, 1), lambda bi, i: (0, 0))],
        out_specs=[xspec,
                   pl.BlockSpec((rows, HX_W), lambda bi, i: (bi * nti + i, 0)),
                   pl.BlockSpec((1, 1, rows), lambda bi, i: (bi * nti + i, 0, 0))],
        out_shape=[jax.ShapeDtypeStruct((b, t, d), F32),
                   jax.ShapeDtypeStruct((b * t, HX_W), F32),
                   jax.ShapeDtypeStruct(((b // bb) * nti, 1, rows), I32)],
        compiler_params=_cparams(("parallel", "parallel")),
        name=name,
    )(mixed, wout, h, g1, gain.reshape(1, 1, d), sc, sh, wrt, rb)


def _dispatch_kernel(pos_ref, hxp_ref, hxs_ref, xs_in, xs_out, sem, *, np_tiles):
    del xs_in
    i = pl.program_id(0)
    rows = hxp_ref.shape[0]

    def scatter(hx_ref):
        def row_copy(r):
            p = pos_ref[i * rows + r]
            return pltpu.make_async_copy(hx_ref.at[pl.ds(r, 1), :], xs_out.at[pl.ds(p, 1), :], sem)

        for r in range(rows):
            row_copy(r).start(priority=r % 2)
        for r in range(rows):
            row_copy(r).wait()

    @pl.when(i < np_tiles)
    def _():
        scatter(hxp_ref)

    @pl.when(i >= np_tiles)
    def _():
        scatter(hxs_ref)


def _dispatch_call(pos, hx_p, hx_s, xs_buf, rows):
    np_tiles, ns_tiles = hx_p.shape[0] // rows, hx_s.shape[0] // rows
    return pl.pallas_call(
        functools.partial(_dispatch_kernel, np_tiles=np_tiles),
        grid_spec=pltpu.PrefetchScalarGridSpec(
            num_scalar_prefetch=1, grid=(np_tiles + ns_tiles,),
            in_specs=[pl.BlockSpec((rows, HX_W), lambda i, ps: (jnp.minimum(i, np_tiles - 1), 0)),
                      pl.BlockSpec((rows, HX_W), lambda i, ps: (jnp.maximum(i - np_tiles, 0), 0)),
                      pl.BlockSpec(memory_space=pl.ANY)],
            out_specs=pl.BlockSpec(memory_space=pl.ANY),
            scratch_shapes=[pltpu.SemaphoreType.DMA(())]),
        out_shape=jax.ShapeDtypeStruct(xs_buf.shape, F32),
        input_output_aliases={3: 0},
        compiler_params=_cparams(("arbitrary",)),
        name="dispatch",
    )(pos, hx_p, hx_s, xs_buf)


def _moe_kernel(exp_ref, chg_ref, nxt_ref, valid_ref, xs_ref, wg_hbm, wu_hbm, wd_hbm,
                ys_ref, wg_f, wu_f, wd_f, wgu_s, wd_s, sem, *, layer):
    t = pl.program_id(0)

    def fetch(slot, e):
        return (pltpu.make_async_copy(wg_hbm.at[layer, e], wg_f.at[slot], sem.at[slot]),
                pltpu.make_async_copy(wu_hbm.at[layer, e], wu_f.at[slot], sem.at[slot]),
                pltpu.make_async_copy(wd_hbm.at[layer, e], wd_f.at[slot], sem.at[slot]))

    def recast(slot):
        rc = 128
        for i in range(D_MODEL // rc):
            rows = slice(i * rc, (i + 1) * rc)
            wgu_s[rows, slot * D_FF:(slot + 1) * D_FF] = wg_f[slot, rows, :].astype(BF16)
            wgu_s[rows, (2 + slot) * D_FF:(3 + slot) * D_FF] = wu_f[slot, rows, :].astype(BF16)
        for i in range(D_FF // rc):
            wd_s[slot * D_FF + i * rc:slot * D_FF + (i + 1) * rc, :] = wd_f[slot, i * rc:(i + 1) * rc, :].astype(BF16)

    for slot in range(2):
        @pl.when(t == 0)
        def _():
            for cp in fetch(slot, exp_ref[slot, 0]):
                cp.start()

        @pl.when(chg_ref[slot, t] > 0)
        def _():
            for cp in fetch(slot, exp_ref[slot, t]):
                cp.wait()
            recast(slot)

            @pl.when(nxt_ref[slot, t] >= 0)
            def _():
                for cp in fetch(slot, nxt_ref[slot, t]):
                    cp.start()

    @pl.when(valid_ref[t] > 0)
    def _():
        x = xs_ref[:, 0:D_MODEL].astype(BF16)
        w_a = xs_ref[:, D_MODEL:D_MODEL + 1]
        w_b = xs_ref[:, D_MODEL + 1:D_MODEL + 2]
        gu = _dot(x, wgu_s[...])
        lane = lax.broadcasted_iota(I32, (MOE_TM, 2 * D_FF), 1)
        act = _silu(gu[:, :2 * D_FF]) * gu[:, 2 * D_FF:] * jnp.where(lane < D_FF, w_a, w_b)
        ys_ref[...] = _dot(act.astype(BF16), wd_s[...])

    @pl.when(valid_ref[t] == 0)
    def _():
        ys_ref[...] = jnp.zeros_like(ys_ref)


def _moe_call(experts, chg, nxt, valid, xs, wg, wu, wd, layer):
    p = xs.shape[0]
    n_tiles = p // MOE_TM
    return pl.pallas_call(
        functools.partial(_moe_kernel, layer=layer),
        grid_spec=pltpu.PrefetchScalarGridSpec(
            num_scalar_prefetch=4, grid=(n_tiles,),
            in_specs=[pl.BlockSpec((MOE_TM, HX_W), lambda t, ex, ch, nx, va: (t, 0)),
                      pl.BlockSpec(memory_space=pl.ANY), pl.BlockSpec(memory_space=pl.ANY),
                      pl.BlockSpec(memory_space=pl.ANY)],
            out_specs=pl.BlockSpec((MOE_TM, D_MODEL), lambda t, ex, ch, nx, va: (t, 0)),
            scratch_shapes=[pltpu.VMEM((2, D_MODEL, D_FF), F32), pltpu.VMEM((2, D_MODEL, D_FF), F32),
                            pltpu.VMEM((2, D_FF, D_MODEL), F32),
                            pltpu.VMEM((D_MODEL, 4 * D_FF), BF16), pltpu.VMEM((2 * D_FF, D_MODEL), BF16),
                            pltpu.SemaphoreType.DMA((2,))]),
        out_shape=jax.ShapeDtypeStruct((p, D_MODEL), F32),
        compiler_params=_cparams(("arbitrary",)),
        name="moe",
    )(experts, chg, nxt, valid, xs, wg, wu, wd)


_PAIRS = [(0, 1), (0, 2), (0, 3), (1, 3), (1, 2), (3, 2)]
_EA_TABLE = np.array([g * EPG + p[0] for g in range(N_GROUPS) for p in _PAIRS], np.int32)
_EB_TABLE = np.array([g * EPG + p[1] for g in range(N_GROUPS) for p in _PAIRS], np.int32)


def _moe(hx_p, hx_s, bkt, xs_buf, wg, wu, wd, layer):
    n = hx_p.shape[0] + hx_s.shape[0]
    tm = MOE_TM
    ar = jnp.arange(N_BUCKETS, dtype=I32)
    blk = LANES
    nb = -(-n // blk)
    bkt_p = jnp.pad(bkt, (0, nb * blk - n), constant_values=N_BUCKETS)
    oh = (ar[:, None] == bkt_p[None, :]).astype(F32)
    triu = jnp.triu(jnp.ones((blk, blk), F32))
    inner = jnp.einsum('kbj,ji->kbi', oh.reshape(N_BUCKETS, nb, blk), triu)
    tot = inner[:, :, -1]
    outer = jnp.cumsum(tot, axis=1) - tot
    cs = (inner + outer[:, :, None]).reshape(N_BUCKETS, nb * blk)
    counts = (outer[:, -1] + tot[:, -1]).astype(I32)
    padded = ((counts + tm - 1) // tm) * tm
    ends = jnp.cumsum(padded)
    offs = ends - padded
    pos = jnp.sum(oh * (cs - 1.0 + offs.astype(F32)[:, None]), axis=0).astype(I32)[:n]
    n_tiles = -(-n // tm) + N_BUCKETS
    p = n_tiles * tm
    if xs_buf is None:
        xs_buf = jnp.zeros((p, HX_W), F32)
    tstart = jnp.arange(n_tiles, dtype=I32) * tm
    tb = jnp.sum((ends[None, :] <= tstart[:, None]).astype(I32), axis=1)
    n_used = ends[-1] // tm
    valid = (jnp.arange(n_tiles, dtype=I32) < n_used).astype(I32)
    last_b = tb[jnp.maximum(n_used - 1, 0)]
    tb = jnp.clip(jnp.where(valid > 0, tb, last_b), 0, N_BUCKETS - 1)
    ea = jnp.asarray(_EA_TABLE)[tb]
    eb = jnp.asarray(_EB_TABLE)[tb]
    experts = jnp.stack([ea, eb])
    chg = jnp.concatenate([jnp.ones((2, 1), I32), (experts[:, 1:] != experts[:, :-1]).astype(I32)], axis=1)
    tidx = jnp.arange(n_tiles, dtype=I32)
    chg_at = jnp.where(chg > 0, tidx[None, :], n_tiles)
    nxt_at = jnp.concatenate([lax.cummin(chg_at, axis=1, reverse=True)[:, 1:], jnp.full((2, 1), n_tiles, I32)], axis=1)
    nxt = jnp.where(nxt_at < n_tiles,
                    jnp.take_along_axis(experts, jnp.minimum(nxt_at, n_tiles - 1), axis=1), -1)
    xs = _dispatch_call(pos, hx_p, hx_s, xs_buf, tm)
    ys = _moe_call(experts, chg, nxt, valid, xs, wg, wu, wd, layer)
    return ys.at[pos].get(mode="promise_in_bounds", unique_indices=True), xs


def _final_kernel(h_ref, y_ref, g_ref, gain_ref, o_ref):
    bb, tt, d = h_ref.shape
    h = h_ref[...] + g_ref[...] * y_ref[...].reshape(bb, tt, d)
    o_ref[...] = _rms(h) * gain_ref[...]


def _final_call(h, y2d, off, g2, gain, *, bb, tt, name):
    b, t, d = h.shape
    nti = t // tt
    rows = bb * tt
    xspec = pl.BlockSpec((bb, tt, d), lambda bi, i: (bi, i, 0))
    return pl.pallas_call(
        _final_kernel,
        grid=(b // bb, nti),
        in_specs=[xspec, pl.BlockSpec((rows, d), lambda bi, i: (off + bi * nti + i, 0)),
                  pl.BlockSpec((bb, 1, d), lambda bi, i: (bi, 0, 0)),
                  pl.BlockSpec((1, 1, d), lambda bi, i: (0, 0, 0))],
        out_specs=xspec,
        out_shape=jax.ShapeDtypeStruct((b, t, d), F32),
        compiler_params=_cparams(("parallel", "parallel")),
        name=name,
    )(h, y2d, g2, gain.reshape(1, 1, d))


def _group_cfg(b, t):
    if t >= 512:
        return dict(bb=1, tt=512, tt_out=1024, L=CHUNK, CB=4, Lc=4 * CHUNK, CBc=1)
    return dict(bb=b, tt=t, tt_out=t, L=min(CHUNK, t), CB=1, Lc=min(CHUNK, t), CBc=1)


def _rope_tables(pos0, t):
    half = DK_C // 2
    inv = np.power(np.float64(ROPE_BASE), -np.linspace(0.0, 1.0, half, dtype=np.float64))
    ang = (np.float64(pos0) + np.arange(t, dtype=np.float64))[:, None] * inv[None, :]
    return jnp.asarray(np.cos(ang), F32), jnp.asarray(np.sin(ang), F32)


def kernel(x_prompt, x_sample, c_prompt, c_sample, state_a_conv, state_a_rec, state_b_c, state_b_n, state_b_m,
           state_c_rec, w_mod, b_mod, norm_mix, norm_ffn, w_in_ab, conv_a, a_log, dt_bias, norm_a, gate_bias_b,
           norm_b, w_out_ab, w_in_c, w_out_c, w_router, router_bias, w_gate, w_up, w_down, norm_final):
    d = D_MODEL
    bp, tp, _ = x_prompt.shape
    bs, ts, _ = x_sample.shape
    n_ab, n_c = w_in_ab.shape[0], w_in_c.shape[0]

    mod_all = _mod_call(jnp.concatenate([c_prompt, c_sample], axis=0), w_mod, b_mod)

    o = np.cumsum((0,) + AB_SIZES)
    w_ab_main = jnp.concatenate([w_in_ab[:, :, o[0]:o[1]].astype(BF16), w_in_ab[:, :, o[3]:o[7]].astype(BF16),
                                 w_in_ab[:, :, o[8]:o[9]].astype(BF16)], axis=-1)
    w_ab_gate = jnp.concatenate([w_in_ab[:, :, o[1]:o[3]], w_in_ab[:, :, o[7]:o[8]]], axis=-1)
    w_ab_gate_pad = jnp.pad(w_ab_gate, ((0, 0), (0, 0), (0, LANES - N_GATE_COLS)))
    w_c = w_in_c.astype(BF16)
    w_out_ab_b = w_out_ab.astype(BF16)
    w_out_c_b = w_out_c.astype(BF16)
    wrt = w_router.T
    rb = router_bias.reshape(N_EXPERTS, 1)
    zeros4 = jnp.zeros((n_ab, H_A), F32)
    gpar_row0 = jnp.concatenate([a_log, jnp.zeros((n_ab, LANES - H_A), F32)], axis=1)
    gpar_row1 = jnp.concatenate([dt_bias, zeros4, gate_bias_b, jnp.zeros((n_ab, LANES - 16), F32)], axis=1)
    gpar = jnp.concatenate([gpar_row0[:, None], gpar_row1[:, None], jnp.zeros((n_ab, 6, LANES), F32)], axis=1)
    gpart = jnp.swapaxes(jnp.concatenate([gpar[:, :2, :N_GATE_COLS], jnp.zeros((n_ab, LANES - 2, N_GATE_COLS), F32)],
                                         axis=1), 1, 2)

    groups = []
    zf = lambda *s: jnp.zeros(s, F32)
    groups.append(dict(
        x=x_prompt, b=bp, t=tp, pos0=0.0, mod=mod_all[:, :bp], row0=0,
        conv=zf(n_ab, bp, CONV_W - 1, QKV_A), sa=zf(n_ab, bp, H_A, DK_A, DV_A), cb=zf(n_ab, bp, H_B, DQK_B, DV_B),
        nb=zf(n_ab, bp, H_B, DQK_B), mb=zf(n_ab, bp, H_B), sc=zf(n_c, bp, H_C, DK_C, DV_C)))
    groups.append(dict(
        x=x_sample, b=bs, t=ts, pos0=float(PAST_LEN), mod=mod_all[:, bp:], row0=bp * tp,
        conv=state_a_conv, sa=state_a_rec, cb=state_b_c, nb=state_b_n, mb=state_b_m, sc=state_c_rec))
    n_tot = bp * tp + bs * ts
    for g in groups:
        g.update(_group_cfg(g["b"], g["t"]))
        g["h"] = g["x"]
        g["prev"] = None
        g["new_ab"], g["new_c"] = [], []
        g["cos"], g["sin"] = _rope_tables(g["pos0"], g["t"])
        g["roff"] = g["row0"] // (g["bb"] * g["tt"])

    xs_buf = None
    for layer in range(DEPTH):
        li = layer // 2
        bkts, hxs = [], []
        for gi, g in enumerate(groups):
            b, t, bb, tt = g["b"], g["t"], g["bb"], g["tt"]
            mods = [m.reshape(b, 1, d) for m in jnp.split(g["mod"][layer], N_MOD, axis=-1)]
            sh1, sc1, g1, sh2, sc2, g2 = mods
            tag = f"l{layer}g{gi}"
            if layer % 2 == 0:
                outs = _ln_mm_call(g["h"], g["prev"], norm_mix[layer], sc1, sh1, w_ab_main[li],
                                   (w_ab_gate_pad[li], g["conv"][li], conv_a[li]), bb=bb, tt=tt, tn=512,
                                   out_dtype=F32, name="ln_ab_" + tag)
                if g["prev"] is not None:
                    g["h"], outs = outs[0], outs[1:]
                proj, gates, gates_t, qkv, conv_n = outs
                gates_t = gates_t.reshape(N_GATE_COLS, b, t).transpose(1, 0, 2)
                mixed, sa_n, cb_n, nb_n, mb_n = _mixer_ab_call(
                    qkv, proj, gates, gates_t, g["sa"][li], g["cb"][li],
                    g["nb"][li].reshape(b, H_B, 1, DQK_B), g["mb"][li].reshape(b, H_B, 1, 1),
                    gpar[li], gpart[li], norm_a[li].reshape(1, DV_A), norm_b[li].reshape(1, DV_B),
                    L=g["L"], CB=g["CB"], name="mixer_ab_" + tag)
                g["new_ab"].append((conv_n, sa_n, cb_n, nb_n.reshape(b, H_B, DQK_B), mb_n.reshape(b, H_B)))
                wout = w_out_ab_b[li]
            else:
                outs = _ln_mm_call(g["h"], g["prev"], norm_mix[layer], sc1, sh1, w_c[li], None,
                                   bb=bb, tt=tt, tn=512, out_dtype=BF16, name="ln_c_" + tag)
                if g["prev"] is not None:
                    g["h"], outs = outs[0], outs[1:]
                (proj,) = outs
                mixed, sc_n = _mixer_c_call(proj, g["cos"], g["sin"], g["sc"][li], L=g["Lc"], CB=g["CBc"],
                                            name="mixer_c_" + tag)
                g["new_c"].append(sc_n)
                wout = w_out_c_b[li]
            tto = g["tt_out"]
            g["h"], hx, bkt = _out_route_call(mixed, wout, g["h"], g1, norm_ffn[layer], sc2, sh2, wrt, rb,
                                              bb=bb, tt=tto, name="out_route_" + tag)
            bkts.append(bkt.reshape(-1))
            hxs.append(hx)
            g["g2"] = g2
        y_moe, xs_buf = _moe(hxs[0], hxs[1], jnp.concatenate(bkts), xs_buf, w_gate, w_up, w_down, layer)
        for g in groups:
            g["prev"] = (y_moe, g["roff"], g["g2"])

    outs = []
    for gi, g in enumerate(groups):
        y2d, off, g2 = g["prev"]
        y = _final_call(g["h"], y2d, off, g2, norm_final, bb=g["bb"], tt=g["tt"], name=f"final_g{gi}")
        ab = [jnp.stack(s) for s in zip(*g["new_ab"])]
        outs.append((y, ab[0], ab[1], ab[2], ab[3], ab[4], jnp.stack(g["new_c"])))
    p, s = outs
    return (p[0], s[0], p[1], p[2], p[3], p[4], p[5], p[6], s[1], s[2], s[3], s[4], s[5], s[6])
```

```python
import functools

import numpy as np
import jax
import jax.numpy as jnp
from jax import lax
from jax.experimental import pallas as pl
from jax.experimental.pallas import tpu as pltpu

F32 = jnp.float32
BF16 = jnp.bfloat16
I32 = jnp.int32

D_MODEL = 1024
DEPTH = 2
CHUNK = 64
H_A, DK_A, DV_A, CONV_W = 4, 128, 128, 4
QKV_A = H_A * (2 * DK_A + DV_A)
H_B, DQK_B, DV_B = 4, 64, 128
H_C, DK_C, DV_C = 4, 256, 512
ROPE_BASE = 10000.0
PAST_LEN = 4096
AB_SIZES = (QKV_A, H_A, H_A, H_A * DV_A, H_B * DQK_B, H_B * DQK_B, H_B * DV_B, 2 * H_B, H_B * DV_B)
MIX_AB = H_A * DV_A + H_B * DV_B
MIX_C = H_C * DV_C
IN_C = 2 * H_C * DK_C + 2 * H_C * DV_C
N_EXPERTS, N_GROUPS, EPG, D_FF = 16, 4, 4, 512
N_MOD = 6
EPS = 1e-6

LANES = 128
AB_MAIN = 3584
N_GATE_COLS = 16
HX_W = D_MODEL + LANES
N_BUCKETS = N_GROUPS * 6
MOE_TM = 256
VMEM_LIMIT = 48 * 1024 * 1024

NN = ((1,), (0,))
NT = ((1,), (1,))
TN = ((0,), (0,))


def _dot(a, b, dims=NN):
    return lax.dot_general(a, b, (dims, ((), ())), preferred_element_type=F32)


def _mmb(a, b, dims=NN):
    return _dot(a.astype(BF16), b.astype(BF16), dims)


def _split2(x):
    hi = x.astype(BF16)
    lo = (x - hi.astype(F32)).astype(BF16)
    return hi, lo


def _split3(x):
    hi = x.astype(BF16)
    r = x - hi.astype(F32)
    mid = r.astype(BF16)
    lo = (r - mid.astype(F32)).astype(BF16)
    return hi, mid, lo


def _mm3(a, b, dims=NN):
    ah, al = _split2(a)
    bh, bl = _split2(b)
    return _dot(ah, bh, dims) + (_dot(ah, bl, dims) + _dot(al, bh, dims))


def _mm_mask_l(mask_bf16, x, dims=NN):
    h, m, l = _split3(x)
    return _dot(mask_bf16, h, dims) + (_dot(mask_bf16, m, dims) + _dot(mask_bf16, l, dims))


def _mm_mask_r(x, mask_bf16):
    h, m, l = _split3(x)
    return _dot(h, mask_bf16) + (_dot(m, mask_bf16) + _dot(l, mask_bf16))


def _sigmoid(x):
    return 1.0 / (1.0 + jnp.exp(-x))


def _silu(x):
    return x * _sigmoid(x)


def _softplus(x):
    return jnp.maximum(x, 0.0) + jnp.log(1.0 + jnp.exp(-jnp.abs(x)))


def _rms(x, eps=EPS):
    return x * lax.rsqrt(jnp.mean(x * x, axis=-1, keepdims=True) + eps)


def _cparams(sem):
    return pltpu.CompilerParams(dimension_semantics=sem, vmem_limit_bytes=VMEM_LIMIT)


def _mod_kernel(c_ref, w_ref, b_ref, o_ref):
    c = c_ref[...]
    o_ref[0] = _mm3(_silu(c), w_ref[0]) + b_ref[0]


def _mod_call(c_all, w_mod, b_mod):
    bt = c_all.shape[0]
    e = w_mod.shape[-1]
    tn = 1024
    return pl.pallas_call(
        _mod_kernel,
        grid=(DEPTH, e // tn),
        in_specs=[pl.BlockSpec((bt, D_MODEL), lambda l, j: (0, 0)),
                  pl.BlockSpec((1, D_MODEL, tn), lambda l, j: (l, 0, j)),
                  pl.BlockSpec((1, 1, tn), lambda l, j: (l, 0, j))],
        out_specs=pl.BlockSpec((1, bt, tn), lambda l, j: (l, 0, j)),
        out_shape=jax.ShapeDtypeStruct((DEPTH, bt, e), F32),
        compiler_params=_cparams(("parallel", "parallel")),
        name="mod",
    )(c_all, w_mod, b_mod.reshape(DEPTH, 1, e))


def _ln_mm_kernel(*refs, has_prev, has_gates, tn):
    it = iter(refs)
    x_ref = next(it)
    if has_prev:
        yp_ref, gp_ref = next(it), next(it)
    gain_ref, sc_ref, sh_ref, w_ref = next(it), next(it), next(it), next(it)
    if has_gates:
        wg_ref, conv0_ref, convw_ref = next(it), next(it), next(it)
    if has_prev:
        h_ref = next(it)
    proj_ref = next(it)
    if has_gates:
        g_ref, gt_ref, qkv_ref, conv_o = next(it), next(it), next(it), next(it)
    hn_s = next(it)
    if has_gates:
        tail_s = next(it)

        @pl.when(pl.program_id(1) == 0)
        def _():
            tail_s[...] = conv0_ref[...]

    bb, tt, d = x_ref.shape
    rows = bb * tt
    x = x_ref[...]
    if has_prev:
        x = x + gp_ref[...] * yp_ref[...].reshape(bb, tt, d)
        h_ref[...] = x
    hn = _rms(x) * gain_ref[...]
    hn = hn * (1.0 + sc_ref[...]) + sh_ref[...]
    hn2 = hn.reshape(rows, d)
    hn_s[...] = hn2.astype(BF16)
    if has_gates:
        gates = _mm3(hn2, wg_ref[...])
        g_ref[...] = gates.reshape(bb, tt, LANES)
        gt_ref[...] = gates.T[0:N_GATE_COLS, :]
    n_conv = QKV_A // tn if has_gates else 0
    n_tiles = w_ref.shape[-1] // tn

    def project(j):
        return _dot(hn_s[...], w_ref[:, j * tn:(j + 1) * tn]).reshape(bb, tt, tn)

    def finish(j, y):
        cols = slice(j * tn, (j + 1) * tn)
        if j < n_conv:
            xp = jnp.concatenate([tail_s[:, :, cols], y], axis=1)
            acc = None
            for k in range(CONV_W):
                term = convw_ref[k:k + 1, cols] * xp[:, k:k + tt, :]
                acc = term if acc is None else acc + term
            qkv_ref[:, :, cols] = _silu(acc).astype(qkv_ref.dtype)
            tail = y[:, tt - (CONV_W - 1):tt, :]
            tail_s[:, :, cols] = tail
            conv_o[:, :, cols] = tail
        else:
            pcols = slice((j - n_conv) * tn, (j - n_conv + 1) * tn)
            proj_ref[:, :, pcols] = y.astype(proj_ref.dtype)

    y = project(0)
    for j in range(n_tiles):
        y_next = project(j + 1) if j + 1 < n_tiles else None
        finish(j, y)
        y = y_next


def _ln_mm_call(x, prev, gain, sc, sh, w, gates_w, *, bb, tt, tn, out_dtype, name):
    b, t, d = x.shape
    e = w.shape[1]
    rows = bb * tt
    nti = t // tt
    grid = (b // bb, nti)
    has_prev = prev is not None
    has_gates = gates_w is not None
    xspec = pl.BlockSpec((bb, tt, d), lambda bi, i: (bi, i, 0))
    mspec = pl.BlockSpec((bb, 1, d), lambda bi, i: (bi, 0, 0))
    in_specs, args = [xspec], [x]
    if has_prev:
        y2d, off, gp = prev
        in_specs += [pl.BlockSpec((rows, d), lambda bi, i: (off + bi * nti + i, 0)), mspec]
        args += [y2d, gp]
    in_specs += [pl.BlockSpec((1, 1, d), lambda bi, i: (0, 0, 0)), mspec, mspec,
                 pl.BlockSpec((d, e), lambda bi, i: (0, 0), pipeline_mode=pl.Buffered(1))]
    args += [gain.reshape(1, 1, d), sc, sh, w]
    tail_spec = pl.BlockSpec((bb, CONV_W - 1, QKV_A), lambda bi, i: (bi, 0, 0))
    if has_gates:
        wg, conv0, convw = gates_w
        in_specs += [pl.BlockSpec((d, LANES), lambda bi, i: (0, 0)), tail_spec,
                     pl.BlockSpec((CONV_W, QKV_A), lambda bi, i: (0, 0))]
        args += [wg, conv0, convw]
    e_proj = e - QKV_A if has_gates else e
    out_specs, out_shape = [], []
    if has_prev:
        out_specs.append(xspec)
        out_shape.append(jax.ShapeDtypeStruct((b, t, d), F32))
    out_specs.append(pl.BlockSpec((bb, tt, e_proj), lambda bi, i: (bi, i, 0)))
    out_shape.append(jax.ShapeDtypeStruct((b, t, e_proj), out_dtype))
    scratch = [pltpu.VMEM((rows, d), BF16)]
    if has_gates:
        out_specs += [pl.BlockSpec((bb, tt, LANES), lambda bi, i: (bi, i, 0)),
                      pl.BlockSpec((N_GATE_COLS, rows), lambda bi, i: (0, bi * nti + i)),
                      pl.BlockSpec((bb, tt, QKV_A), lambda bi, i: (bi, i, 0)), tail_spec]
        out_shape += [jax.ShapeDtypeStruct((b, t, LANES), F32),
                      jax.ShapeDtypeStruct((N_GATE_COLS, b * t), F32),
                      jax.ShapeDtypeStruct((b, t, QKV_A), BF16),
                      jax.ShapeDtypeStruct((b, CONV_W - 1, QKV_A), F32)]
        scratch.append(pltpu.VMEM((bb, CONV_W - 1, QKV_A), F32))
    return pl.pallas_call(
        functools.partial(_ln_mm_kernel, has_prev=has_prev, has_gates=has_gates, tn=tn),
        grid=grid, in_specs=in_specs, out_specs=out_specs, out_shape=out_shape,
        scratch_shapes=scratch,
        compiler_params=_cparams(("parallel", "arbitrary" if has_gates else "parallel")),
        name=name,
    )(*args)


def _tri_inv_all(a_list, eye, length, mm):
    xs = [-a for a in a_list]
    ps = [eye + x for x in xs]
    n = 2
    while n < length:
        xs = [mm(x, x) for x in xs]
        ps = [p + mm(p, x) for p, x in zip(ps, xs)]
        n *= 2
    return ps


def _mixer_ab_kernel(qkv_ref, z_ref, qkb_ref, vb_ref, ob_ref, g_ref, gt_ref,
                     sa0_ref, cb0_ref, nb0_ref, mb0_ref,
                     gpar_ref, gpart_ref, na_ref, nbn_ref,
                     mixed_ref, sa_o, cb_o, nb_o, mb_o, *, L, CB):
    @pl.when(pl.program_id(1) == 0)
    def _():
        sa_o[...] = sa0_ref[...]
        cb_o[...] = cb0_ref[...]
        nb_o[...] = nb0_ref[...]
        mb_o[...] = mb0_ref[...]

    ii = lax.broadcasted_iota(I32, (L, L), 0)
    jj = lax.broadcasted_iota(I32, (L, L), 1)
    tri = ii >= jj
    strict = ii > jj
    eye = (ii == jj).astype(F32)
    tril_b = tri.astype(BF16)
    triu_b = (ii <= jj).astype(BF16)
    lane = lax.broadcasted_iota(I32, (L, LANES), 1)
    rowi = lax.broadcasted_iota(I32, (N_GATE_COLS, L), 0)
    gpar = gpar_ref[...]
    gpart = gpart_ref[...]
    nea = -jnp.exp(gpar[0:1, :])
    neat = -jnp.exp(gpart[:, 0:1])
    norm_a = na_ref[...]
    norm_b = nbn_ref[...]

    mm = _mmb
    mm_inv = _mm3

    def qkv_tile(r0, col0):
        return qkv_ref[0, r0:r0 + L, col0:col0 + DK_A].astype(F32)

    chunks = range(CB)
    items = [(c, h) for c in chunks for h in range(H_A)]
    rs = [slice(c * L, (c + 1) * L) for c in chunks]

    pre = [g_ref[0, r, :] + gpar[1:2, :] for r in rs]
    csrc = [jnp.where(lane < 4, nea * _softplus(p), jnp.where(lane >= 12, -_softplus(-p), 0.0)) for p in pre]
    gcum = [_mm_mask_l(tril_b, jnp.where(lane < N_GATE_COLS, x, 0.0)) for x in csrc]
    pret = [gt_ref[0, :, r] + gpart[:, 1:2] for r in rs]
    csrct = [jnp.where(rowi < 4, neat * _softplus(p), jnp.where(rowi >= 12, -_softplus(-p), 0.0)) for p in pret]
    gcumt = [_mm_mask_r(x, triu_b) for x in csrct]
    beta_all = [_sigmoid(p) for p in pre]

    q = [qkv_tile(c * L, h * DK_A) for c, h in items]
    k = [qkv_tile(c * L, H_A * DK_A + h * DK_A) for c, h in items]
    v = [qkv_tile(c * L, 2 * H_A * DK_A + h * DV_A) for c, h in items]
    q = [x * lax.rsqrt(jnp.sum(x * x, axis=-1, keepdims=True) + EPS) * (DK_A ** -0.5) for x in q]
    k = [x * lax.rsqrt(jnp.sum(x * x, axis=-1, keepdims=True) + EPS) for x in k]
    gc_c = [gcum[c][:, h:h + 1] for c, h in items]
    gc_r = [gcumt[c][h:h + 1, :] for c, h in items]
    decay = [jnp.where(tri, jnp.exp(jnp.where(tri, a - b, 0.0)), 0.0) for a, b in zip(gc_c, gc_r)]
    beta_c = [beta_all[c][:, 4 + h:5 + h] for c, h in items]
    kb = [x * b for x, b in zip(k, beta_c)]
    a_list = [jnp.where(strict, mm(x, y, NT) * d, 0.0) for x, y, d in zip(kb, k, decay)]
    attn = [jnp.where(tri, mm(x, y, NT) * d, 0.0) for x, y, d in zip(q, k, decay)]
    eg = [jnp.exp(x) for x in gc_c]
    gl = [x[L - 1:L, :] for x in gc_c]
    rhs = [jnp.concatenate([x * e, y * b], axis=-1) for x, e, y, b in zip(kb, eg, v, beta_c)]
    qeg = [x * e for x, e in zip(q, eg)]
    kdec = [x * jnp.exp(g - gc) for x, g, gc in zip(k, gl, gc_c)]
    sdec = [jnp.exp(g) for g in gl]
    tinv = _tri_inv_all(a_list, eye, L, mm)
    sol = [mm(t, r) for t, r in zip(tinv, rhs)]
    resid = [r - (s + mm_inv(a, s)) for r, s, a in zip(rhs, sol, a_list)]
    sol = [s + mm(t, r) for s, t, r in zip(sol, tinv, resid)]

    heads = range(H_B)
    qb = [qkb_ref[0, rs[c], h * DQK_B:(h + 1) * DQK_B] * (DQK_B ** -0.5) for c, h in items]
    kbb = [qkb_ref[0, rs[c], H_B * DQK_B + h * DQK_B:H_B * DQK_B + (h + 1) * DQK_B] for c, h in items]
    b_c = [gcum[c][:, 12 + h:13 + h] for c, h in items]
    dm = [jnp.where(tri, gcum[c][:, 12 + h:13 + h] - gcumt[c][12 + h:13 + h, :] + pret[c][8 + h:9 + h, :], -jnp.inf)
          for c, h in items]
    dmax = [jnp.max(x, axis=-1, keepdims=True) for x in dm]
    qkm = [mm(x, y, NT) for x, y in zip(qb, kbb)]
    li_c = [pre[c][:, 8 + h:9 + h] for c, h in items]

    for c in chunks:
        ia = [c * H_A + h for h in heads]
        s = [sa_o[0, h] for h in heads]
        v_new = [sol[i][:, DK_A:] - mm(sol[i][:, :DK_A], s[h]) for h, i in enumerate(ia)]
        m_prev = [mb_o[0, h] for h in heads]
        a0 = [b_c[i] + m_prev[h] for h, i in enumerate(ia)]
        m_t = [jnp.maximum(a0[h], dmax[i]) for h, i in enumerate(ia)]
        w0 = [jnp.exp(a0[h] - m_t[h]) for h in heads]
        sm = [qkm[i] * jnp.exp(dm[i] - m_t[h]) for h, i in enumerate(ia)]
        cst = [cb_o[0, h] for h in heads]
        nst = [nb_o[0, h] for h in heads]
        vb = [vb_ref[0, rs[c], h * DV_B:(h + 1) * DV_B] for h in heads]
        o_a = [mm(qeg[i], s[h]) + mm(attn[i], v_new[h]) for h, i in enumerate(ia)]
        for h, i in enumerate(ia):
            sa_o[0, h] = s[h] * sdec[i] + mm(kdec[i], v_new[h], TN)
        num = [w0[h] * mm(qb[i], cst[h]) + mm(sm[h], vb[h]) for h, i in enumerate(ia)]
        den = [w0[h] * jnp.sum(qb[i] * nst[h], axis=-1, keepdims=True) + jnp.sum(sm[h], axis=-1, keepdims=True)
               for h, i in enumerate(ia)]
        hh = [num[h] / jnp.maximum(jnp.abs(den[h]), jnp.exp(-m_t[h])) for h in heads]
        m_new = [m_t[h][L - 1:L, :] for h in heads]
        kw = [kbb[i] * jnp.exp(b_c[i][L - 1:L, :] - b_c[i] + li_c[i] - m_new[h]) for h, i in enumerate(ia)]
        for h in heads:
            w0l = w0[h][L - 1:L, :]
            cb_o[0, h] = w0l * cst[h] + mm(kw[h], vb[h], TN)
            nb_o[0, h] = w0l * nst[h] + jnp.sum(kw[h], axis=0, keepdims=True)
            mb_o[0, h] = m_new[h]
        for h in heads:
            zg = z_ref[0, rs[c], h * DV_A:(h + 1) * DV_A]
            mixed_ref[0, rs[c], h * DV_A:(h + 1) * DV_A] = (_rms(o_a[h]) * norm_a * _silu(zg)).astype(mixed_ref.dtype)
            og = ob_ref[0, rs[c], h * DV_B:(h + 1) * DV_B]
            c0 = H_A * DV_A + h * DV_B
            mixed_ref[0, rs[c], c0:c0 + DV_B] = (_rms(hh[h]) * norm_b * _sigmoid(og)).astype(mixed_ref.dtype)


def _mixer_ab_call(qkv, proj, gates, gates_t, sa0, cb0, nb0, mb0, gpar, gpart, norm_a, norm_b, *, L, CB, name):
    b, t, _ = proj.shape
    tb = L * CB
    nt = t // tb

    def col(width, idx):
        return pl.BlockSpec((1, tb, width), lambda bi, i: (bi, i, idx))

    def const(shape):
        return pl.BlockSpec(shape, lambda bi, i: (0,) * len(shape))

    def state(shape):
        return pl.BlockSpec((1,) + shape, lambda bi, i: (bi,) + (0,) * len(shape))

    in_specs = [col(QKV_A, 0), col(512, 0), col(512, 1), col(512, 2), col(512, 3),
                col(LANES, 0), pl.BlockSpec((1, N_GATE_COLS, tb), lambda bi, i: (bi, 0, i)),
                state((H_A, DK_A, DV_A)), state((H_B, DQK_B, DV_B)),
                state((H_B, 1, DQK_B)), state((H_B, 1, 1)),
                const((8, LANES)), const((N_GATE_COLS, LANES)),
                const((1, DV_A)), const((1, DV_B))]
    out_specs = [pl.BlockSpec((1, tb, MIX_AB), lambda bi, i: (bi, i, 0)),
                 state((H_A, DK_A, DV_A)), state((H_B, DQK_B, DV_B)),
                 state((H_B, 1, DQK_B)), state((H_B, 1, 1))]
    out_shape = [jax.ShapeDtypeStruct((b, t, MIX_AB), BF16),
                 jax.ShapeDtypeStruct((b, H_A, DK_A, DV_A), F32),
                 jax.ShapeDtypeStruct((b, H_B, DQK_B, DV_B), F32),
                 jax.ShapeDtypeStruct((b, H_B, 1, DQK_B), F32),
                 jax.ShapeDtypeStruct((b, H_B, 1, 1), F32)]
    return pl.pallas_call(
        functools.partial(_mixer_ab_kernel, L=L, CB=CB),
        grid=(b, nt), in_specs=in_specs, out_specs=out_specs, out_shape=out_shape,
        compiler_params=_cparams(("parallel", "arbitrary")),
        name=name,
    )(qkv, proj, proj, proj, proj, gates, gates_t, sa0, cb0, nb0, mb0, gpar, gpart, norm_a, norm_b)


def _mixer_c_kernel(q_ref, k_ref, v_ref, g_ref, cos_ref, sin_ref, s0_ref, mixed_ref, s_o, intra_s, *, L, CB):
    log_gamma = [float(np.log1p(-np.exp2(-5.0 - h))) for h in range(H_C)]

    @pl.when(pl.program_id(1) == 0)
    def _():
        s_o[...] = s0_ref[...]
        ii = lax.broadcasted_iota(I32, (L, L), 0)
        jj = lax.broadcasted_iota(I32, (L, L), 1)
        rel = (ii - jj).astype(F32)
        for h in range(H_C):
            intra_s[h] = jnp.where(rel >= 0, jnp.exp(log_gamma[h] * jnp.maximum(rel, 0.0)), 0.0)

    idx = lax.broadcasted_iota(I32, (L, 1), 0).astype(F32)
    half = DK_C // 2

    def rope(x, cos, sin):
        x1, x2 = x[:, :half], x[:, half:]
        return jnp.concatenate([x1 * cos - x2 * sin, x1 * sin + x2 * cos], axis=-1)

    for c in range(CB):
        r0 = c * L
        cos = cos_ref[r0:r0 + L, :]
        sin = sin_ref[r0:r0 + L, :]
        for h in range(H_C):
            lg = log_gamma[h]
            intra = intra_s[h]
            q_decay = jnp.exp(lg * (idx + 1.0))
            k_decay = jnp.exp(lg * (L - 1.0 - idx))
            s_decay = float(np.exp(lg * L))
            q = rope(q_ref[0, r0:r0 + L, h * DK_C:(h + 1) * DK_C].astype(F32), cos, sin)
            k = rope(k_ref[0, r0:r0 + L, h * DK_C:(h + 1) * DK_C].astype(F32), cos, sin) * (DK_C ** -0.5)
            v = v_ref[0, r0:r0 + L, h * DV_C:(h + 1) * DV_C]
            s = s_o[0, h]
            o = _mmb(_mmb(q, k, NT) * intra, v) + _mmb(q * q_decay, s)
            s_o[0, h] = s_decay * s + _mmb(k * k_decay, v, TN)
            gg = g_ref[0, r0:r0 + L, h * DV_C:(h + 1) * DV_C].astype(F32)
            o = _rms(o) * _silu(gg)
            mixed_ref[0, r0:r0 + L, h * DV_C:(h + 1) * DV_C] = o.astype(mixed_ref.dtype)


def _mixer_c_call(proj, cos, sin, s0, *, L, CB, name):
    b, t, _ = proj.shape
    tb = L * CB
    nt = t // tb
    qk_w, v_w = H_C * DK_C, H_C * DV_C
    sspec = pl.BlockSpec((1, H_C, DK_C, DV_C), lambda bi, i: (bi, 0, 0, 0))
    tspec = pl.BlockSpec((tb, DK_C // 2), lambda bi, i: (i, 0))
    return pl.pallas_call(
        functools.partial(_mixer_c_kernel, L=L, CB=CB),
        grid=(b, nt),
        in_specs=[pl.BlockSpec((1, tb, qk_w), lambda bi, i: (bi, i, 0)),
                  pl.BlockSpec((1, tb, qk_w), lambda bi, i: (bi, i, 1)),
                  pl.BlockSpec((1, tb, v_w), lambda bi, i: (bi, i, 1)),
                  pl.BlockSpec((1, tb, v_w), lambda bi, i: (bi, i, 2)),
                  tspec, tspec, sspec],
        out_specs=[pl.BlockSpec((1, tb, MIX_C), lambda bi, i: (bi, i, 0)), sspec],
        out_shape=[jax.ShapeDtypeStruct((b, t, MIX_C), BF16),
                   jax.ShapeDtypeStruct((b, H_C, DK_C, DV_C), F32)],
        scratch_shapes=[pltpu.VMEM((H_C, L, L), F32)],
        compiler_params=_cparams(("parallel", "arbitrary")),
        name=name,
    )(proj, proj, proj, proj, cos, sin, s0)


def _out_route_kernel(mixed_ref, wout_ref, h_ref, g1_ref, gain_ref, sc_ref, sh_ref, wrt_ref, rb_ref,
                      hnew_ref, hx_ref, bkt_ref):
    bb, tt, m = mixed_ref.shape
    d = h_ref.shape[-1]
    rows = bb * tt
    y = _dot(mixed_ref[...].reshape(rows, m), wout_ref[...])
    h = h_ref[...] + g1_ref[...] * y.reshape(bb, tt, d)
    hnew_ref[...] = h
    hn = _rms(h) * gain_ref[...]
    hn = (hn * (1.0 + sc_ref[...]) + sh_ref[...]).reshape(rows, d)
    hx_ref[:, 0:d] = hn

    logits = _mm3(wrt_ref[...], hn, NT)
    score = _sigmoid(logits)
    sel = score + rb_ref[...]

    def row(a, r):
        return a[r:r + 1, :]

    gscore = []
    for g in range(N_GROUPS):
        a, b, c, e = (row(sel, EPG * g + i) for i in range(EPG))
        hi_ab, lo_ab, hi_ce, lo_ce = jnp.maximum(a, b), jnp.minimum(a, b), jnp.maximum(c, e), jnp.minimum(c, e)
        top1 = jnp.maximum(hi_ab, hi_ce)
        top2 = jnp.maximum(jnp.maximum(lo_ab, lo_ce), jnp.minimum(hi_ab, hi_ce))
        gscore.append(top1 + top2)
    best = jnp.zeros((1, rows), I32)
    bestv = gscore[0]
    for g in range(1, N_GROUPS):
        upd = gscore[g] > bestv
        best = jnp.where(upd, g, best)
        bestv = jnp.where(upd, gscore[g], bestv)

    def pick(a, i):
        out = row(a, i)
        for g in range(1, N_GROUPS):
            out = jnp.where(best == g, row(a, EPG * g + i), out)
        return out

    vsel = [pick(sel, i) for i in range(EPG)]
    vsc = [pick(score, i) for i in range(EPG)]
    i1 = jnp.zeros((1, rows), I32)
    v1 = vsel[0]
    for i in range(1, EPG):
        upd = vsel[i] > v1
        i1 = jnp.where(upd, i, i1)
        v1 = jnp.where(upd, vsel[i], v1)
    i2 = jnp.full((1, rows), -1, I32)
    v2 = jnp.full((1, rows), -jnp.inf, F32)
    for i in range(EPG):
        cand = jnp.where(i1 == i, -jnp.inf, vsel[i])
        upd = cand > v2
        i2 = jnp.where(upd, i, i2)
        v2 = jnp.where(upd, cand, v2)
    s1 = jnp.zeros((1, rows), F32)
    s2 = jnp.zeros((1, rows), F32)
    for i in range(EPG):
        s1 = jnp.where(i1 == i, vsc[i], s1)
        s2 = jnp.where(i2 == i, vsc[i], s2)
    den = s1 + s2
    w1 = s1 / den
    w2 = s2 / den
    first_lo = i1 < i2
    lo = jnp.minimum(i1, i2)
    hi = jnp.maximum(i1, i2)
    wlo = jnp.where(first_lo, w1, w2)
    whi = jnp.where(first_lo, w2, w1)
    pair = jnp.where(lo == 0, hi - 1, jnp.where(lo == 1, 6 - hi, 5))
    bkt_ref[...] = (best * 6 + pair).reshape(1, 1, rows)
    wa = jnp.where(pair == 5, whi, wlo)
    wb = jnp.where(pair == 5, wlo, whi)

    aux = jnp.concatenate([wa, wb, jnp.zeros((LANES - 2, rows), F32)], axis=0)
    hx_ref[:, d:d + LANES] = aux.T


def _out_route_call(mixed, wout, h, g1, gain, sc, sh, wrt, rb, *, bb, tt, name):
    b, t, m = mixed.shape
    d = h.shape[-1]
    rows = bb * tt
    nti = t // tt
    xspec = pl.BlockSpec((bb, tt, d), lambda bi, i: (bi, i, 0))
    mspec = pl.BlockSpec((bb, 1, d), lambda bi, i: (bi, 0, 0))
    return pl.pallas_call(
        _out_route_kernel,
        grid=(b // bb, nti),
        in_specs=[pl.BlockSpec((bb, tt, m), lambda bi, i: (bi, i, 0)),
                  pl.BlockSpec((m, d), lambda bi, i: (0, 0)),
                  xspec, mspec,
                  pl.BlockSpec((1, 1, d), lambda bi, i: (0, 0, 0)), mspec, mspec,
                  pl.BlockSpec((N_EXPERTS, d), lambda bi, i: (0, 0)),
                  pl.BlockSpec((N_EXPERTS, 1), lambda bi, i: (0, 0))],
        out_specs=[xspec,
                   pl.BlockSpec((rows, HX_W), lambda bi, i: (bi * nti + i, 0)),
                   pl.BlockSpec((1, 1, rows), lambda bi, i: (bi * nti + i, 0, 0))],
        out_shape=[jax.ShapeDtypeStruct((b, t, d), F32),
                   jax.ShapeDtypeStruct((b * t, HX_W), F32),
                   jax.ShapeDtypeStruct(((b // bb) * nti, 1, rows), I32)],
        compiler_params=_cparams(("parallel", "parallel")),
        name=name,
    )(mixed, wout, h, g1, gain.reshape(1, 1, d), sc, sh, wrt, rb)


def _dispatch_kernel(pos_ref, hxp_ref, hxs_ref, xs_in, xs_out, sem, *, np_tiles):
    del xs_in
    i = pl.program_id(0)
    rows = hxp_ref.shape[0]

    def scatter(hx_ref):
        def row_copy(r):
            p = pos_ref[i * rows + r]
            return pltpu.make_async_copy(hx_ref.at[pl.ds(r, 1), :], xs_out.at[pl.ds(p, 1), :], sem)

        for r in range(rows):
            row_copy(r).start(priority=r % 2)
        for r in range(rows):
            row_copy(r).wait()

    @pl.when(i < np_tiles)
    def _():
        scatter(hxp_ref)

    @pl.when(i >= np_tiles)
    def _():
        scatter(hxs_ref)


def _dispatch_call(pos, hx_p, hx_s, xs_buf, rows):
    np_tiles, ns_tiles = hx_p.shape[0] // rows, hx_s.shape[0] // rows
    return pl.pallas_call(
        functools.partial(_dispatch_kernel, np_tiles=np_tiles),
        grid_spec=pltpu.PrefetchScalarGridSpec(
            num_scalar_prefetch=1, grid=(np_tiles + ns_tiles,),
            in_specs=[pl.BlockSpec((rows, HX_W), lambda i, ps: (jnp.minimum(i, np_tiles - 1), 0)),
                      pl.BlockSpec((rows, HX_W), lambda i, ps: (jnp.maximum(i - np_tiles, 0), 0)),
                      pl.BlockSpec(memory_space=pl.ANY)],
            out_specs=pl.BlockSpec(memory_space=pl.ANY),
            scratch_shapes=[pltpu.SemaphoreType.DMA(())]),
        out_shape=jax.ShapeDtypeStruct(xs_buf.shape, F32),
        input_output_aliases={3: 0},
        compiler_params=_cparams(("arbitrary",)),
        name="dispatch",
    )(pos, hx_p, hx_s, xs_buf)


def _moe_kernel(exp_ref, chg_ref, nxt_ref, valid_ref, xs_ref, wg_hbm, wu_hbm, wd_hbm,
                ys_ref, wg_f, wu_f, wd_f, wgu_s, wd_s, sem, *, layer):
    t = pl.program_id(0)

    def fetch(slot, e):
        return (pltpu.make_async_copy(wg_hbm.at[layer, e], wg_f.at[slot], sem.at[slot]),
                pltpu.make_async_copy(wu_hbm.at[layer, e], wu_f.at[slot], sem.at[slot]),
                pltpu.make_async_copy(wd_hbm.at[layer, e], wd_f.at[slot], sem.at[slot]))

    def recast(slot):
        rc = 128
        for i in range(D_MODEL // rc):
            rows = slice(i * rc, (i + 1) * rc)
            wgu_s[rows, slot * D_FF:(slot + 1) * D_FF] = wg_f[slot, rows, :].astype(BF16)
            wgu_s[rows, (2 + slot) * D_FF:(3 + slot) * D_FF] = wu_f[slot, rows, :].astype(BF16)
        for i in range(D_FF // rc):
            wd_s[slot * D_FF + i * rc:slot * D_FF + (i + 1) * rc, :] = wd_f[slot, i * rc:(i + 1) * rc, :].astype(BF16)

    for slot in range(2):
        @pl.when(t == 0)
        def _():
            for cp in fetch(slot, exp_ref[slot, 0]):
                cp.start()

        @pl.when(chg_ref[slot, t] > 0)
        def _():
            for cp in fetch(slot, exp_ref[slot, t]):
                cp.wait()
            recast(slot)

            @pl.when(nxt_ref[slot, t] >= 0)
            def _():
                for cp in fetch(slot, nxt_ref[slot, t]):
                    cp.start()

    @pl.when(valid_ref[t] > 0)
    def _():
        x = xs_ref[:, 0:D_MODEL].astype(BF16)
        w_a = xs_ref[:, D_MODEL:D_MODEL + 1]
        w_b = xs_ref[:, D_MODEL + 1:D_MODEL + 2]
        gu = _dot(x, wgu_s[...])
        lane = lax.broadcasted_iota(I32, (MOE_TM, 2 * D_FF), 1)
        act = _silu(gu[:, :2 * D_FF]) * gu[:, 2 * D_FF:] * jnp.where(lane < D_FF, w_a, w_b)
        ys_ref[...] = _dot(act.astype(BF16), wd_s[...])

    @pl.when(valid_ref[t] == 0)
    def _():
        ys_ref[...] = jnp.zeros_like(ys_ref)


def _moe_call(experts, chg, nxt, valid, xs, wg, wu, wd, layer):
    p = xs.shape[0]
    n_tiles = p // MOE_TM
    return pl.pallas_call(
        functools.partial(_moe_kernel, layer=layer),
        grid_spec=pltpu.PrefetchScalarGridSpec(
            num_scalar_prefetch=4, grid=(n_tiles,),
            in_specs=[pl.BlockSpec((MOE_TM, HX_W), lambda t, ex, ch, nx, va: (t, 0)),
                      pl.BlockSpec(memory_space=pl.ANY), pl.BlockSpec(memory_space=pl.ANY),
                      pl.BlockSpec(memory_space=pl.ANY)],
            out_specs=pl.BlockSpec((MOE_TM, D_MODEL), lambda t, ex, ch, nx, va: (t, 0)),
            scratch_shapes=[pltpu.VMEM((2, D_MODEL, D_FF), F32), pltpu.VMEM((2, D_MODEL, D_FF), F32),
                            pltpu.VMEM((2, D_FF, D_MODEL), F32),
                            pltpu.VMEM((D_MODEL, 4 * D_FF), BF16), pltpu.VMEM((2 * D_FF, D_MODEL), BF16),
                            pltpu.SemaphoreType.DMA((2,))]),
        out_shape=jax.ShapeDtypeStruct((p, D_MODEL), F32),
        compiler_params=_cparams(("arbitrary",)),
        name="moe",
    )(experts, chg, nxt, valid, xs, wg, wu, wd)


_PAIRS = [(0, 1), (0, 2), (0, 3), (1, 3), (1, 2), (3, 2)]
_EA_TABLE = np.array([g * EPG + p[0] for g in range(N_GROUPS) for p in _PAIRS], np.int32)
_EB_TABLE = np.array([g * EPG + p[1] for g in range(N_GROUPS) for p in _PAIRS], np.int32)


def _moe(hx_p, hx_s, bkt, xs_buf, wg, wu, wd, layer):
    n = hx_p.shape[0] + hx_s.shape[0]
    tm = MOE_TM
    ar = jnp.arange(N_BUCKETS, dtype=I32)
    blk = LANES
    nb = -(-n // blk)
    bkt_p = jnp.pad(bkt, (0, nb * blk - n), constant_values=N_BUCKETS)
    oh = (ar[:, None] == bkt_p[None, :]).astype(F32)
    triu = jnp.triu(jnp.ones((blk, blk), F32))
    inner = jnp.einsum('kbj,ji->kbi', oh.reshape(N_BUCKETS, nb, blk), triu)
    tot = inner[:, :, -1]
    outer = jnp.cumsum(tot, axis=1) - tot
    cs = (inner + outer[:, :, None]).reshape(N_BUCKETS, nb * blk)
    counts = (outer[:, -1] + tot[:, -1]).astype(I32)
    padded = ((counts + tm - 1) // tm) * tm
    ends = jnp.cumsum(padded)
    offs = ends - padded
    pos = jnp.sum(oh * (cs - 1.0 + offs.astype(F32)[:, None]), axis=0).astype(I32)[:n]
    n_tiles = -(-n // tm) + N_BUCKETS
    p = n_tiles * tm
    if xs_buf is None:
        xs_buf = jnp.zeros((p, HX_W), F32)
    tstart = jnp.arange(n_tiles, dtype=I32) * tm
    tb = jnp.sum((ends[None, :] <= tstart[:, None]).astype(I32), axis=1)
    n_used = ends[-1] // tm
    valid = (jnp.arange(n_tiles, dtype=I32) < n_used).astype(I32)
    last_b = tb[jnp.maximum(n_used - 1, 0)]
    tb = jnp.clip(jnp.where(valid > 0, tb, last_b), 0, N_BUCKETS - 1)
    ea = jnp.asarray(_EA_TABLE)[tb]
    eb = jnp.asarray(_EB_TABLE)[tb]
    experts = jnp.stack([ea, eb])
    chg = jnp.concatenate([jnp.ones((2, 1), I32), (experts[:, 1:] != experts[:, :-1]).astype(I32)], axis=1)
    tidx = jnp.arange(n_tiles, dtype=I32)
    chg_at = jnp.where(chg > 0, tidx[None, :], n_tiles)
    nxt_at = jnp.concatenate([lax.cummin(chg_at, axis=1, reverse=True)[:, 1:], jnp.full((2, 1), n_tiles, I32)], axis=1)
    nxt = jnp.where(nxt_at < n_tiles,
                    jnp.take_along_axis(experts, jnp.minimum(nxt_at, n_tiles - 1), axis=1), -1)
    xs = _dispatch_call(pos, hx_p, hx_s, xs_buf, tm)
    ys = _moe_call(experts, chg, nxt, valid, xs, wg, wu, wd, layer)
    return ys.at[pos].get(mode="promise_in_bounds", unique_indices=True), xs


def _final_kernel(h_ref, y_ref, g_ref, gain_ref, o_ref):
    bb, tt, d = h_ref.shape
    h = h_ref[...] + g_ref[...] * y_ref[...].reshape(bb, tt, d)
    o_ref[...] = _rms(h) * gain_ref[...]


def _final_call(h, y2d, off, g2, gain, *, bb, tt, name):
    b, t, d = h.shape
    nti = t // tt
    rows = bb * tt
    xspec = pl.BlockSpec((bb, tt, d), lambda bi, i: (bi, i, 0))
    return pl.pallas_call(
        _final_kernel,
        grid=(b // bb, nti),
        in_specs=[xspec, pl.BlockSpec((rows, d), lambda bi, i: (off + bi * nti + i, 0)),
                  pl.BlockSpec((bb, 1, d), lambda bi, i: (bi, 0, 0)),
                  pl.BlockSpec((1, 1, d), lambda bi, i: (0, 0, 0))],
        out_specs=xspec,
        out_shape=jax.ShapeDtypeStruct((b, t, d), F32),
        compiler_params=_cparams(("parallel", "parallel")),
        name=name,
    )(h, y2d, g2, gain.reshape(1, 1, d))


def _group_cfg(b, t):
    if t >= 512:
        return dict(bb=1, tt=512, tt_out=1024, L=CHUNK, CB=4, Lc=4 * CHUNK, CBc=1)
    return dict(bb=b, tt=t, tt_out=t, L=min(CHUNK, t), CB=1, Lc=min(CHUNK, t), CBc=1)


def _rope_tables(pos0, t):
    half = DK_C // 2
    inv = np.power(np.float64(ROPE_BASE), -np.linspace(0.0, 1.0, half, dtype=np.float64))
    ang = (np.float64(pos0) + np.arange(t, dtype=np.float64))[:, None] * inv[None, :]
    return jnp.asarray(np.cos(ang), F32), jnp.asarray(np.sin(ang), F32)


def kernel(x_prompt, x_sample, c_prompt, c_sample, state_a_conv, state_a_rec, state_b_c, state_b_n, state_b_m,
           state_c_rec, w_mod, b_mod, norm_mix, norm_ffn, w_in_ab, conv_a, a_log, dt_bias, norm_a, gate_bias_b,
           norm_b, w_out_ab, w_in_c, w_out_c, w_router, router_bias, w_gate, w_up, w_down, norm_final):
    d = D_MODEL
    bp, tp, _ = x_prompt.shape
    bs, ts, _ = x_sample.shape
    n_ab, n_c = w_in_ab.shape[0], w_in_c.shape[0]

    mod_all = _mod_call(jnp.concatenate([c_prompt, c_sample], axis=0), w_mod, b_mod)

    o = np.cumsum((0,) + AB_SIZES)
    w_ab_main = jnp.concatenate([w_in_ab[:, :, o[0]:o[1]].astype(BF16), w_in_ab[:, :, o[3]:o[7]].astype(BF16),
                                 w_in_ab[:, :, o[8]:o[9]].astype(BF16)], axis=-1)
    w_ab_gate = jnp.concatenate([w_in_ab[:, :, o[1]:o[3]], w_in_ab[:, :, o[7]:o[8]]], axis=-1)
    w_ab_gate_pad = jnp.pad(w_ab_gate, ((0, 0), (0, 0), (0, LANES - N_GATE_COLS)))
    w_c = w_in_c.astype(BF16)
    w_out_ab_b = w_out_ab.astype(BF16)
    w_out_c_b = w_out_c.astype(BF16)
    wrt = w_router.T
    rb = router_bias.reshape(N_EXPERTS, 1)
    zeros4 = jnp.zeros((n_ab, H_A), F32)
    gpar_row0 = jnp.concatenate([a_log, jnp.zeros((n_ab, LANES - H_A), F32)], axis=1)
    gpar_row1 = jnp.concatenate([dt_bias, zeros4, gate_bias_b, jnp.zeros((n_ab, LANES - 16), F32)], axis=1)
    gpar = jnp.concatenate([gpar_row0[:, None], gpar_row1[:, None], jnp.zeros((n_ab, 6, LANES), F32)], axis=1)
    gpart = jnp.swapaxes(jnp.concatenate([gpar[:, :2, :N_GATE_COLS], jnp.zeros((n_ab, LANES - 2, N_GATE_COLS), F32)],
                                         axis=1), 1, 2)

    groups = []
    zf = lambda *s: jnp.zeros(s, F32)
    groups.append(dict(
        x=x_prompt, b=bp, t=tp, pos0=0.0, mod=mod_all[:, :bp], row0=0,
        conv=zf(n_ab, bp, CONV_W - 1, QKV_A), sa=zf(n_ab, bp, H_A, DK_A, DV_A), cb=zf(n_ab, bp, H_B, DQK_B, DV_B),
        nb=zf(n_ab, bp, H_B, DQK_B), mb=zf(n_ab, bp, H_B), sc=zf(n_c, bp, H_C, DK_C, DV_C)))
    groups.append(dict(
        x=x_sample, b=bs, t=ts, pos0=float(PAST_LEN), mod=mod_all[:, bp:], row0=bp * tp,
        conv=state_a_conv, sa=state_a_rec, cb=state_b_c, nb=state_b_n, mb=state_b_m, sc=state_c_rec))
    n_tot = bp * tp + bs * ts
    for g in groups:
        g.update(_group_cfg(g["b"], g["t"]))
        g["h"] = g["x"]
        g["prev"] = None
        g["new_ab"], g["new_c"] = [], []
        g["cos"], g["sin"] = _rope_tables(g["pos0"], g["t"])
        g["roff"] = g["row0"] // (g["bb"] * g["tt"])

    xs_buf = None
    for layer in range(DEPTH):
        li = layer // 2
        bkts, hxs = [], []
        for gi, g in enumerate(groups):
            b, t, bb, tt = g["b"], g["t"], g["bb"], g["tt"]
            mods = [m.reshape(b, 1, d) for m in jnp.split(g["mod"][layer], N_MOD, axis=-1)]
            sh1, sc1, g1, sh2, sc2, g2 = mods
            tag = f"l{layer}g{gi}"
            if layer % 2 == 0:
                outs = _ln_mm_call(g["h"], g["prev"], norm_mix[layer], sc1, sh1, w_ab_main[li],
                                   (w_ab_gate_pad[li], g["conv"][li], conv_a[li]), bb=bb, tt=tt, tn=512,
                                   out_dtype=F32, name="ln_ab_" + tag)
                if g["prev"] is not None:
                    g["h"], outs = outs[0], outs[1:]
                proj, gates, gates_t, qkv, conv_n = outs
                gates_t = gates_t.reshape(N_GATE_COLS, b, t).transpose(1, 0, 2)
                mixed, sa_n, cb_n, nb_n, mb_n = _mixer_ab_call(
                    qkv, proj, gates, gates_t, g["sa"][li], g["cb"][li],
                    g["nb"][li].reshape(b, H_B, 1, DQK_B), g["mb"][li].reshape(b, H_B, 1, 1),
                    gpar[li], gpart[li], norm_a[li].reshape(1, DV_A), norm_b[li].reshape(1, DV_B),
                    L=g["L"], CB=g["CB"], name="mixer_ab_" + tag)
                g["new_ab"].append((conv_n, sa_n, cb_n, nb_n.reshape(b, H_B, DQK_B), mb_n.reshape(b, H_B)))
                wout = w_out_ab_b[li]
            else:
                outs = _ln_mm_call(g["h"], g["prev"], norm_mix[layer], sc1, sh1, w_c[li], None,
                                   bb=bb, tt=tt, tn=512, out_dtype=BF16, name="ln_c_" + tag)
                if g["prev"] is not None:
                    g["h"], outs = outs[0], outs[1:]
                (proj,) = outs
                mixed, sc_n = _mixer_c_call(proj, g["cos"], g["sin"], g["sc"][li], L=g["Lc"], CB=g["CBc"],
                                            name="mixer_c_" + tag)
                g["new_c"].append(sc_n)
                wout = w_out_c_b[li]
            tto = g["tt_out"]
            g["h"], hx, bkt = _out_route_call(mixed, wout, g["h"], g1, norm_ffn[layer], sc2, sh2, wrt, rb,
                                              bb=bb, tt=tto, name="out_route_" + tag)
            bkts.append(bkt.reshape(-1))
            hxs.append(hx)
            g["g2"] = g2
        y_moe, xs_buf = _moe(hxs[0], hxs[1], jnp.concatenate(bkts), xs_buf, w_gate, w_up, w_down, layer)
        for g in groups:
            g["prev"] = (y_moe, g["roff"], g["g2"])

    outs = []
    for gi, g in enumerate(groups):
        y2d, off, g2 = g["prev"]
        y = _final_call(g["h"], y2d, off, g2, norm_final, bb=g["bb"], tt=g["tt"], name=f"final_g{gi}")
        ab = [jnp.stack(s) for s in zip(*g["new_ab"])]
        outs.append((y, ab[0], ab[1], ab[2], ab[3], ab[4], jnp.stack(g["new_c"])))
    p, s = outs
    return (p[0], s[0], p[1], p[2], p[3], p[4], p[5], p[6], s[1], s[2], s[3], s[4], s[5], s[6])
```

```python
import functools

import numpy as np
import jax
import jax.numpy as jnp
from jax import lax
from jax.experimental import pallas as pl
from jax.experimental.pallas import tpu as pltpu

F32 = jnp.float32
BF16 = jnp.bfloat16
I32 = jnp.int32

D_MODEL = 1024
DEPTH = 2
CHUNK = 64
H_A, DK_A, DV_A, CONV_W = 4, 128, 128, 4
QKV_A = H_A * (2 * DK_A + DV_A)
H_B, DQK_B, DV_B = 4, 64, 128
H_C, DK_C, DV_C = 4, 256, 512
ROPE_BASE = 10000.0
PAST_LEN = 4096
AB_SIZES = (QKV_A, H_A, H_A, H_A * DV_A, H_B * DQK_B, H_B * DQK_B, H_B * DV_B, 2 * H_B, H_B * DV_B)
MIX_AB = H_A * DV_A + H_B * DV_B
MIX_C = H_C * DV_C
IN_C = 2 * H_C * DK_C + 2 * H_C * DV_C
N_EXPERTS, N_GROUPS, EPG, D_FF = 16, 4, 4, 512
N_MOD = 6
EPS = 1e-6

LANES = 128
AB_MAIN = 3584
N_GATE_COLS = 16
HX_W = D_MODEL + LANES
N_BUCKETS = N_GROUPS * 6
MOE_TM = 256
VMEM_LIMIT = 48 * 1024 * 1024

NN = ((1,), (0,))
NT = ((1,), (1,))
TN = ((0,), (0,))


def _dot(a, b, dims=NN):
    return lax.dot_general(a, b, (dims, ((), ())), preferred_element_type=F32)


def _mmb(a, b, dims=NN):
    return _dot(a.astype(BF16), b.astype(BF16), dims)


def _split2(x):
    hi = x.astype(BF16)
    lo = (x - hi.astype(F32)).astype(BF16)
    return hi, lo


def _split3(x):
    hi = x.astype(BF16)
    r = x - hi.astype(F32)
    mid = r.astype(BF16)
    lo = (r - mid.astype(F32)).astype(BF16)
    return hi, mid, lo


def _mm3(a, b, dims=NN):
    ah, al = _split2(a)
    bh, bl = _split2(b)
    return _dot(ah, bh, dims) + (_dot(ah, bl, dims) + _dot(al, bh, dims))


def _mm_mask_l(mask_bf16, x, dims=NN):
    h, m, l = _split3(x)
    return _dot(mask_bf16, h, dims) + (_dot(mask_bf16, m, dims) + _dot(mask_bf16, l, dims))


def _mm_mask_r(x, mask_bf16):
    h, m, l = _split3(x)
    return _dot(h, mask_bf16) + (_dot(m, mask_bf16) + _dot(l, mask_bf16))


def _sigmoid(x):
    return 1.0 / (1.0 + jnp.exp(-x))


def _silu(x):
    return x * _sigmoid(x)


def _softplus(x):
    return jnp.maximum(x, 0.0) + jnp.log(1.0 + jnp.exp(-jnp.abs(x)))


def _rms(x, eps=EPS):
    return x * lax.rsqrt(jnp.mean(x * x, axis=-1, keepdims=True) + eps)


def _cparams(sem):
    return pltpu.CompilerParams(dimension_semantics=sem, vmem_limit_bytes=VMEM_LIMIT)


def _mod_kernel(c_ref, w_ref, b_ref, o_ref):
    c = c_ref[...]
    o_ref[0] = _mm3(_silu(c), w_ref[0]) + b_ref[0]


def _mod_call(c_all, w_mod, b_mod):
    bt = c_all.shape[0]
    e = w_mod.shape[-1]
    tn = 1024
    return pl.pallas_call(
        _mod_kernel,
        grid=(DEPTH, e // tn),
        in_specs=[pl.BlockSpec((bt, D_MODEL), lambda l, j: (0, 0)),
                  pl.BlockSpec((1, D_MODEL, tn), lambda l, j: (l, 0, j)),
                  pl.BlockSpec((1, 1, tn), lambda l, j: (l, 0, j))],
        out_specs=pl.BlockSpec((1, bt, tn), lambda l, j: (l, 0, j)),
        out_shape=jax.ShapeDtypeStruct((DEPTH, bt, e), F32),
        compiler_params=_cparams(("parallel", "parallel")),
        name="mod",
    )(c_all, w_mod, b_mod.reshape(DEPTH, 1, e))


def _ln_mm_kernel(*refs, has_prev, has_gates, tn):
    it = iter(refs)
    x_ref = next(it)
    if has_prev:
        yp_ref, gp_ref = next(it), next(it)
    gain_ref, sc_ref, sh_ref, w_ref = next(it), next(it), next(it), next(it)
    if has_gates:
        wg_ref, conv0_ref, convw_ref = next(it), next(it), next(it)
    if has_prev:
        h_ref = next(it)
    proj_ref = next(it)
    if has_gates:
        g_ref, gt_ref, qkv_ref, conv_o = next(it), next(it), next(it), next(it)
    hn_s = next(it)
    if has_gates:
        tail_s = next(it)

        @pl.when(pl.program_id(1) == 0)
        def _():
            tail_s[...] = conv0_ref[...]

    bb, tt, d = x_ref.shape
    rows = bb * tt
    x = x_ref[...]
    if has_prev:
        x = x + gp_ref[...] * yp_ref[...].reshape(bb, tt, d)
        h_ref[...] = x
    hn = _rms(x) * gain_ref[...]
    hn = hn * (1.0 + sc_ref[...]) + sh_ref[...]
    hn2 = hn.reshape(rows, d)
    hn_s[...] = hn2.astype(BF16)
    if has_gates:
        gates = _mm3(hn2, wg_ref[...])
        g_ref[...] = gates.reshape(bb, tt, LANES)
        gt_ref[...] = gates.T[0:N_GATE_COLS, :]
    n_conv = QKV_A // tn if has_gates else 0
    n_tiles = w_ref.shape[-1] // tn

    def project(j):
        return _dot(hn_s[...], w_ref[:, j * tn:(j + 1) * tn]).reshape(bb, tt, tn)

    def finish(j, y):
        cols = slice(j * tn, (j + 1) * tn)
        if j < n_conv:
            xp = jnp.concatenate([tail_s[:, :, cols], y], axis=1)
            acc = None
            for k in range(CONV_W):
                term = convw_ref[k:k + 1, cols] * xp[:, k:k + tt, :]
                acc = term if acc is None else acc + term
            qkv_ref[:, :, cols] = _silu(acc).astype(qkv_ref.dtype)
            tail = y[:, tt - (CONV_W - 1):tt, :]
            tail_s[:, :, cols] = tail
            conv_o[:, :, cols] = tail
        else:
            pcols = slice((j - n_conv) * tn, (j - n_conv + 1) * tn)
            proj_ref[:, :, pcols] = y.astype(proj_ref.dtype)

    y = project(0)
    for j in range(n_tiles):
        y_next = project(j + 1) if j + 1 < n_tiles else None
        finish(j, y)
        y = y_next


def _ln_mm_call(x, prev, gain, sc, sh, w, gates_w, *, bb, tt, tn, out_dtype, name):
    b, t, d = x.shape
    e = w.shape[1]
    rows = bb * tt
    nti = t // tt
    grid = (b // bb, nti)
    has_prev = prev is not None
    has_gates = gates_w is not None
    xspec = pl.BlockSpec((bb, tt, d), lambda bi, i: (bi, i, 0))
    mspec = pl.BlockSpec((bb, 1, d), lambda bi, i: (bi, 0, 0))
    in_specs, args = [xspec], [x]
    if has_prev:
        y2d, off, gp = prev
        in_specs += [pl.BlockSpec((rows, d), lambda bi, i: (off + bi * nti + i, 0)), mspec]
        args += [y2d, gp]
    in_specs += [pl.BlockSpec((1, 1, d), lambda bi, i: (0, 0, 0)), mspec, mspec,
                 pl.BlockSpec((d, e), lambda bi, i: (0, 0), pipeline_mode=pl.Buffered(1))]
    args += [gain.reshape(1, 1, d), sc, sh, w]
    tail_spec = pl.BlockSpec((bb, CONV_W - 1, QKV_A), lambda bi, i: (bi, 0, 0))
    if has_gates:
        wg, conv0, convw = gates_w
        in_specs += [pl.BlockSpec((d, LANES), lambda bi, i: (0, 0)), tail_spec,
                     pl.BlockSpec((CONV_W, QKV_A), lambda bi, i: (0, 0))]
        args += [wg, conv0, convw]
    e_proj = e - QKV_A if has_gates else e
    out_specs, out_shape = [], []
    if has_prev:
        out_specs.append(xspec)
        out_shape.append(jax.ShapeDtypeStruct((b, t, d), F32))
    out_specs.append(pl.BlockSpec((bb, tt, e_proj), lambda bi, i: (bi, i, 0)))
    out_shape.append(jax.ShapeDtypeStruct((b, t, e_proj), out_dtype))
    scratch = [pltpu.VMEM((rows, d), BF16)]
    if has_gates:
        out_specs += [pl.BlockSpec((bb, tt, LANES), lambda bi, i: (bi, i, 0)),
                      pl.BlockSpec((N_GATE_COLS, rows), lambda bi, i: (0, bi * nti + i)),
                      pl.BlockSpec((bb, tt, QKV_A), lambda bi, i: (bi, i, 0)), tail_spec]
        out_shape += [jax.ShapeDtypeStruct((b, t, LANES), F32),
                      jax.ShapeDtypeStruct((N_GATE_COLS, b * t), F32),
                      jax.ShapeDtypeStruct((b, t, QKV_A), BF16),
                      jax.ShapeDtypeStruct((b, CONV_W - 1, QKV_A), F32)]
        scratch.append(pltpu.VMEM((bb, CONV_W - 1, QKV_A), F32))
    return pl.pallas_call(
        functools.partial(_ln_mm_kernel, has_prev=has_prev, has_gates=has_gates, tn=tn),
        grid=grid, in_specs=in_specs, out_specs=out_specs, out_shape=out_shape,
        scratch_shapes=scratch,
        compiler_params=_cparams(("parallel", "arbitrary" if has_gates else "parallel")),
        name=name,
    )(*args)


def _tri_inv_all(a_list, eye, length, mm):
    xs = [-a for a in a_list]
    ps = [eye + x for x in xs]
    n = 2
    while n < length:
        xs = [mm(x, x) for x in xs]
        ps = [p + mm(p, x) for p, x in zip(ps, xs)]
        n *= 2
    return ps


def _mixer_ab_kernel(qkv_ref, z_ref, qkb_ref, vb_ref, ob_ref, g_ref, gt_ref,
                     sa0_ref, cb0_ref, nb0_ref, mb0_ref,
                     gpar_ref, gpart_ref, na_ref, nbn_ref,
                     mixed_ref, sa_o, cb_o, nb_o, mb_o, *, L, CB):
    @pl.when(pl.program_id(1) == 0)
    def _():
        sa_o[...] = sa0_ref[...]
        cb_o[...] = cb0_ref[...]
        nb_o[...] = nb0_ref[...]
        mb_o[...] = mb0_ref[...]

    ii = lax.broadcasted_iota(I32, (L, L), 0)
    jj = lax.broadcasted_iota(I32, (L, L), 1)
    tri = ii >= jj
    strict = ii > jj
    eye = (ii == jj).astype(F32)
    tril_b = tri.astype(BF16)
    triu_b = (ii <= jj).astype(BF16)
    rowi = lax.broadcasted_iota(I32, (N_GATE_COLS, L), 0)
    gpar = gpar_ref[...]
    gpart = gpart_ref[...]
    neat = -jnp.exp(gpart[:, 0:1])
    norm_a = na_ref[...]
    norm_b = nbn_ref[...]

    mm = _mmb

    def qkv_tile(r0, col0):
        return qkv_ref[0, r0:r0 + L, col0:col0 + DK_A].astype(F32)

    chunks = range(CB)
    items = [(c, h) for c in chunks for h in range(H_A)]
    rs = [slice(c * L, (c + 1) * L) for c in chunks]

    pre = [g_ref[0, r, :] + gpar[1:2, :] for r in rs]
    pret = [gt_ref[0, :, r] + gpart[:, 1:2] for r in rs]
    csrct = [jnp.where(rowi < 4, neat * _softplus(p), jnp.where(rowi >= 12, -_softplus(-p), 0.0)) for p in pret]
    gcumt = [_mm_mask_r(x, triu_b) for x in csrct]
    gcum = [_mm_mask_l(tril_b, x, NT) for x in csrct]
    beta_all = [_sigmoid(p) for p in pre]

    q = [qkv_tile(c * L, h * DK_A) for c, h in items]
    k = [qkv_tile(c * L, H_A * DK_A + h * DK_A) for c, h in items]
    v = [qkv_tile(c * L, 2 * H_A * DK_A + h * DV_A) for c, h in items]
    q = [x * lax.rsqrt(jnp.sum(x * x, axis=-1, keepdims=True) + EPS) * (DK_A ** -0.5) for x in q]
    k = [x * lax.rsqrt(jnp.sum(x * x, axis=-1, keepdims=True) + EPS) for x in k]
    gc_c = [gcum[c][:, h:h + 1] for c, h in items]
    gc_r = [gcumt[c][h:h + 1, :] for c, h in items]
    decay = [jnp.where(tri, jnp.exp(jnp.where(tri, a - b, 0.0)), 0.0) for a, b in zip(gc_c, gc_r)]
    beta_c = [beta_all[c][:, 4 + h:5 + h] for c, h in items]
    kb = [x * b for x, b in zip(k, beta_c)]
    a_list = [jnp.where(strict, mm(x, y, NT) * d, 0.0) for x, y, d in zip(kb, k, decay)]
    attn = [jnp.where(tri, mm(x, y, NT) * d, 0.0) for x, y, d in zip(q, k, decay)]
    eg = [jnp.exp(x) for x in gc_c]
    gl = [x[L - 1:L, :] for x in gc_c]
    rhs = [jnp.concatenate([x * e, y * b], axis=-1) for x, e, y, b in zip(kb, eg, v, beta_c)]
    qeg = [x * e for x, e in zip(q, eg)]
    kdec = [x * jnp.exp(g - gc) for x, g, gc in zip(k, gl, gc_c)]
    sdec = [jnp.exp(g) for g in gl]
    a_list = [a.astype(BF16) for a in a_list]
    tinv = _tri_inv_all([a.astype(F32) for a in a_list], eye, L, mm)
    sol = [mm(t, r) for t, r in zip(tinv, rhs)]
    sol_hl = [_split2(s) for s in sol]
    resid = [r - (s + (_dot(a, hi) + _dot(a, lo))) for r, s, a, (hi, lo) in zip(rhs, sol, a_list, sol_hl)]
    sol = [s + mm(t, r) for s, t, r in zip(sol, tinv, resid)]

    heads = range(H_B)
    qb = [qkb_ref[0, rs[c], h * DQK_B:(h + 1) * DQK_B] * (DQK_B ** -0.5) for c, h in items]
    kbb = [qkb_ref[0, rs[c], H_B * DQK_B + h * DQK_B:H_B * DQK_B + (h + 1) * DQK_B] for c, h in items]
    b_c = [gcum[c][:, 12 + h:13 + h] for c, h in items]
    dm = [jnp.where(tri, gcum[c][:, 12 + h:13 + h] - gcumt[c][12 + h:13 + h, :] + pret[c][8 + h:9 + h, :], -jnp.inf)
          for c, h in items]
    dmax = [jnp.max(x, axis=-1, keepdims=True) for x in dm]
    qkm = [mm(x, y, NT) for x, y in zip(qb, kbb)]
    li_c = [pre[c][:, 8 + h:9 + h] for c, h in items]

    for c in chunks:
        ia = [c * H_A + h for h in heads]
        s = [sa_o[0, h] for h in heads]
        v_new = [sol[i][:, DK_A:] - mm(sol[i][:, :DK_A], s[h]) for h, i in enumerate(ia)]
        m_prev = [mb_o[0, h] for h in heads]
        a0 = [b_c[i] + m_prev[h] for h, i in enumerate(ia)]
        m_t = [jnp.maximum(a0[h], dmax[i]) for h, i in enumerate(ia)]
        w0 = [jnp.exp(a0[h] - m_t[h]) for h in heads]
        sm = [qkm[i] * jnp.exp(dm[i] - m_t[h]) for h, i in enumerate(ia)]
        cst = [cb_o[0, h] for h in heads]
        nst = [nb_o[0, h] for h in heads]
        vb = [vb_ref[0, rs[c], h * DV_B:(h + 1) * DV_B] for h in heads]
        o_a = [mm(qeg[i], s[h]) + mm(attn[i], v_new[h]) for h, i in enumerate(ia)]
        for h, i in enumerate(ia):
            sa_o[0, h] = s[h] * sdec[i] + mm(kdec[i], v_new[h], TN)
        num = [w0[h] * mm(qb[i], cst[h]) + mm(sm[h], vb[h]) for h, i in enumerate(ia)]
        den = [w0[h] * jnp.sum(qb[i] * nst[h], axis=-1, keepdims=True) + jnp.sum(sm[h], axis=-1, keepdims=True)
               for h, i in enumerate(ia)]
        hh = [num[h] / jnp.maximum(jnp.abs(den[h]), jnp.exp(-m_t[h])) for h in heads]
        m_new = [m_t[h][L - 1:L, :] for h in heads]
        kw = [kbb[i] * jnp.exp(b_c[i][L - 1:L, :] - b_c[i] + li_c[i] - m_new[h]) for h, i in enumerate(ia)]
        for h in heads:
            w0l = w0[h][L - 1:L, :]
            cb_o[0, h] = w0l * cst[h] + mm(kw[h], vb[h], TN)
            nb_o[0, h] = w0l * nst[h] + jnp.sum(kw[h], axis=0, keepdims=True)
            mb_o[0, h] = m_new[h]
        for h in heads:
            zg = z_ref[0, rs[c], h * DV_A:(h + 1) * DV_A]
            mixed_ref[0, rs[c], h * DV_A:(h + 1) * DV_A] = (_rms(o_a[h]) * norm_a * _silu(zg)).astype(mixed_ref.dtype)
            og = ob_ref[0, rs[c], h * DV_B:(h + 1) * DV_B]
            c0 = H_A * DV_A + h * DV_B
            mixed_ref[0, rs[c], c0:c0 + DV_B] = (_rms(hh[h]) * norm_b * _sigmoid(og)).astype(mixed_ref.dtype)


def _mixer_ab_call(qkv, proj, gates, gates_t, sa0, cb0, nb0, mb0, gpar, gpart, norm_a, norm_b, *, L, CB, name):
    b, t, _ = proj.shape
    tb = L * CB
    nt = t // tb

    def col(width, idx):
        return pl.BlockSpec((1, tb, width), lambda bi, i: (bi, i, idx))

    def const(shape):
        return pl.BlockSpec(shape, lambda bi, i: (0,) * len(shape))

    def state(shape):
        return pl.BlockSpec((1,) + shape, lambda bi, i: (bi,) + (0,) * len(shape))

    in_specs = [col(QKV_A, 0), col(512, 0), col(512, 1), col(512, 2), col(512, 3),
                col(LANES, 0), pl.BlockSpec((1, N_GATE_COLS, tb), lambda bi, i: (bi, 0, i)),
                state((H_A, DK_A, DV_A)), state((H_B, DQK_B, DV_B)),
                state((H_B, 1, DQK_B)), state((H_B, 1, 1)),
                const((8, LANES)), const((N_GATE_COLS, LANES)),
                const((1, DV_A)), const((1, DV_B))]
    out_specs = [pl.BlockSpec((1, tb, MIX_AB), lambda bi, i: (bi, i, 0)),
                 state((H_A, DK_A, DV_A)), state((H_B, DQK_B, DV_B)),
                 state((H_B, 1, DQK_B)), state((H_B, 1, 1))]
    out_shape = [jax.ShapeDtypeStruct((b, t, MIX_AB), BF16),
                 jax.ShapeDtypeStruct((b, H_A, DK_A, DV_A), F32),
                 jax.ShapeDtypeStruct((b, H_B, DQK_B, DV_B), F32),
                 jax.ShapeDtypeStruct((b, H_B, 1, DQK_B), F32),
                 jax.ShapeDtypeStruct((b, H_B, 1, 1), F32)]
    return pl.pallas_call(
        functools.partial(_mixer_ab_kernel, L=L, CB=CB),
        grid=(b, nt), in_specs=in_specs, out_specs=out_specs, out_shape=out_shape,
        compiler_params=_cparams(("parallel", "arbitrary")),
        name=name,
    )(qkv, proj, proj, proj, proj, gates, gates_t, sa0, cb0, nb0, mb0, gpar, gpart, norm_a, norm_b)


def _mixer_c_kernel(q_ref, k_ref, v_ref, g_ref, cos_ref, sin_ref, s0_ref, mixed_ref, s_o, intra_s, *, L, CB):
    log_gamma = [float(np.log1p(-np.exp2(-5.0 - h))) for h in range(H_C)]

    @pl.when(pl.program_id(1) == 0)
    def _():
        s_o[...] = s0_ref[...]
        ii = lax.broadcasted_iota(I32, (L, L), 0)
        jj = lax.broadcasted_iota(I32, (L, L), 1)
        rel = (ii - jj).astype(F32)
        for h in range(H_C):
            intra_s[h] = jnp.where(rel >= 0, jnp.exp(log_gamma[h] * jnp.maximum(rel, 0.0)), 0.0)

    idx = lax.broadcasted_iota(I32, (L, 1), 0).astype(F32)
    half = DK_C // 2

    def rope(x, cos, sin):
        x1, x2 = x[:, :half], x[:, half:]
        return jnp.concatenate([x1 * cos - x2 * sin, x1 * sin + x2 * cos], axis=-1)

    for c in range(CB):
        r0 = c * L
        cos = cos_ref[r0:r0 + L, :]
        sin = sin_ref[r0:r0 + L, :]
        for h in range(H_C):
            lg = log_gamma[h]
            intra = intra_s[h]
            q_decay = jnp.exp(lg * (idx + 1.0))
            k_decay = jnp.exp(lg * (L - 1.0 - idx))
            s_decay = float(np.exp(lg * L))
            q = rope(q_ref[0, r0:r0 + L, h * DK_C:(h + 1) * DK_C].astype(F32), cos, sin)
            k = rope(k_ref[0, r0:r0 + L, h * DK_C:(h + 1) * DK_C].astype(F32), cos, sin) * (DK_C ** -0.5)
            v = v_ref[0, r0:r0 + L, h * DV_C:(h + 1) * DV_C]
            s = s_o[0, h]
            o = _mmb(_mmb(q, k, NT) * intra, v) + _mmb(q * q_decay, s)
            s_o[0, h] = s_decay * s + _mmb(k * k_decay, v, TN)
            gg = g_ref[0, r0:r0 + L, h * DV_C:(h + 1) * DV_C].astype(F32)
            o = _rms(o) * _silu(gg)
            mixed_ref[0, r0:r0 + L, h * DV_C:(h + 1) * DV_C] = o.astype(mixed_ref.dtype)


def _mixer_c_call(proj, cos, sin, s0, *, L, CB, name):
    b, t, _ = proj.shape
    tb = L * CB
    nt = t // tb
    qk_w, v_w = H_C * DK_C, H_C * DV_C
    sspec = pl.BlockSpec((1, H_C, DK_C, DV_C), lambda bi, i: (bi, 0, 0, 0))
    tspec = pl.BlockSpec((tb, DK_C // 2), lambda bi, i: (i, 0))
    return pl.pallas_call(
        functools.partial(_mixer_c_kernel, L=L, CB=CB),
        grid=(b, nt),
        in_specs=[pl.BlockSpec((1, tb, qk_w), lambda bi, i: (bi, i, 0)),
                  pl.BlockSpec((1, tb, qk_w), lambda bi, i: (bi, i, 1)),
                  pl.BlockSpec((1, tb, v_w), lambda bi, i: (bi, i, 1)),
                  pl.BlockSpec((1, tb, v_w), lambda bi, i: (bi, i, 2)),
                  tspec, tspec, sspec],
        out_specs=[pl.BlockSpec((1, tb, MIX_C), lambda bi, i: (bi, i, 0)), sspec],
        out_shape=[jax.ShapeDtypeStruct((b, t, MIX_C), BF16),
                   jax.ShapeDtypeStruct((b, H_C, DK_C, DV_C), F32)],
        scratch_shapes=[pltpu.VMEM((H_C, L, L), F32)],
        compiler_params=_cparams(("parallel", "arbitrary")),
        name=name,
    )(proj, proj, proj, proj, cos, sin, s0)


def _out_route_kernel(mixed_ref, wout_ref, h_ref, g1_ref, gain_ref, sc_ref, sh_ref, wrt_ref, rb_ref,
                      hnew_ref, hx_ref, bkt_ref):
    bb, tt, m = mixed_ref.shape
    d = h_ref.shape[-1]
    rows = bb * tt
    y = _dot(mixed_ref[...].reshape(rows, m), wout_ref[...])
    h = h_ref[...] + g1_ref[...] * y.reshape(bb, tt, d)
    hnew_ref[...] = h
    hn = _rms(h) * gain_ref[...]
    hn = (hn * (1.0 + sc_ref[...]) + sh_ref[...]).reshape(rows, d)
    hx_ref[:, 0:d] = hn

    logits = _mm3(wrt_ref[...], hn, NT)
    score = _sigmoid(logits)
    sel = score + rb_ref[...]

    def row(a, r):
        return a[r:r + 1, :]

    gscore = []
    for g in range(N_GROUPS):
        a, b, c, e = (row(sel, EPG * g + i) for i in range(EPG))
        hi_ab, lo_ab, hi_ce, lo_ce = jnp.maximum(a, b), jnp.minimum(a, b), jnp.maximum(c, e), jnp.minimum(c, e)
        top1 = jnp.maximum(hi_ab, hi_ce)
        top2 = jnp.maximum(jnp.maximum(lo_ab, lo_ce), jnp.minimum(hi_ab, hi_ce))
        gscore.append(top1 + top2)
    best = jnp.zeros((1, rows), I32)
    bestv = gscore[0]
    for g in range(1, N_GROUPS):
        upd = gscore[g] > bestv
        best = jnp.where(upd, g, best)
        bestv = jnp.where(upd, gscore[g], bestv)

    def pick(a, i):
        out = row(a, i)
        for g in range(1, N_GROUPS):
            out = jnp.where(best == g, row(a, EPG * g + i), out)
        return out

    vsel = [pick(sel, i) for i in range(EPG)]
    vsc = [pick(score, i) for i in range(EPG)]
    i1 = jnp.zeros((1, rows), I32)
    v1 = vsel[0]
    for i in range(1, EPG):
        upd = vsel[i] > v1
        i1 = jnp.where(upd, i, i1)
        v1 = jnp.where(upd, vsel[i], v1)
    i2 = jnp.full((1, rows), -1, I32)
    v2 = jnp.full((1, rows), -jnp.inf, F32)
    for i in range(EPG):
        cand = jnp.where(i1 == i, -jnp.inf, vsel[i])
        upd = cand > v2
        i2 = jnp.where(upd, i, i2)
        v2 = jnp.where(upd, cand, v2)
    s1 = jnp.zeros((1, rows), F32)
    s2 = jnp.zeros((1, rows), F32)
    for i in range(EPG):
        s1 = jnp.where(i1 == i, vsc[i], s1)
        s2 = jnp.where(i2 == i, vsc[i], s2)
    den = s1 + s2
    w1 = s1 / den
    w2 = s2 / den
    first_lo = i1 < i2
    lo = jnp.minimum(i1, i2)
    hi = jnp.maximum(i1, i2)
    wlo = jnp.where(first_lo, w1, w2)
    whi = jnp.where(first_lo, w2, w1)
    pair = jnp.where(lo == 0, hi - 1, jnp.where(lo == 1, 6 - hi, 5))
    bkt_ref[...] = (best * 6 + pair).reshape(1, 1, rows)
    wa = jnp.where(pair == 5, whi, wlo)
    wb = jnp.where(pair == 5, wlo, whi)

    aux = jnp.concatenate([wa, wb, jnp.zeros((LANES - 2, rows), F32)], axis=0)
    hx_ref[:, d:d + LANES] = aux.T


def _out_route_call(mixed, wout, h, g1, gain, sc, sh, wrt, rb, *, bb, tt, name):
    b, t, m = mixed.shape
    d = h.shape[-1]
    rows = bb * tt
    nti = t // tt
    xspec = pl.BlockSpec((bb, tt, d), lambda bi, i: (bi, i, 0))
    mspec = pl.BlockSpec((bb, 1, d), lambda bi, i: (bi, 0, 0))
    return pl.pallas_call(
        _out_route_kernel,
        grid=(b // bb, nti),
        in_specs=[pl.BlockSpec((bb, tt, m), lambda bi, i: (bi, i, 0)),
                  pl.BlockSpec((m, d), lambda bi, i: (0, 0)),
                  xspec, mspec,
                  pl.BlockSpec((1, 1, d), lambda bi, i: (0, 0, 0)), mspec, mspec,
                  pl.BlockSpec((N_EXPERTS, d), lambda bi, i: (0, 0)),
                  pl.BlockSpec((N_EXPERTS, 1), lambda bi, i: (0, 0))],
        out_specs=[xspec,
                   pl.BlockSpec((rows, HX_W), lambda bi, i: (bi * nti + i, 0)),
                   pl.BlockSpec((1, 1, rows), lambda bi, i: (bi * nti + i, 0, 0))],
        out_shape=[jax.ShapeDtypeStruct((b, t, d), F32),
                   jax.ShapeDtypeStruct((b * t, HX_W), F32),
                   jax.ShapeDtypeStruct(((b // bb) * nti, 1, rows), I32)],
        compiler_params=_cparams(("parallel", "parallel")),
        name=name,
    )(mixed, wout, h, g1, gain.reshape(1, 1, d), sc, sh, wrt, rb)


def _dispatch_kernel(pos_ref, hxp_ref, hxs_ref, xs_in, xs_out, sem, *, np_tiles):
    del xs_in
    i = pl.program_id(0)
    rows = hxp_ref.shape[0]

    def scatter(hx_ref):
        def row_copy(r):
            p = pos_ref[i * rows + r]
            return pltpu.make_async_copy(hx_ref.at[pl.ds(r, 1), :], xs_out.at[pl.ds(p, 1), :], sem)

        for r in range(rows):
            row_copy(r).start(priority=r % 2)
        for r in range(rows):
            row_copy(r).wait()

    @pl.when(i < np_tiles)
    def _():
        scatter(hxp_ref)

    @pl.when(i >= np_tiles)
    def _():
        scatter(hxs_ref)


def _dispatch_call(pos, hx_p, hx_s, xs_buf, rows):
    np_tiles, ns_tiles = hx_p.shape[0] // rows, hx_s.shape[0] // rows
    return pl.pallas_call(
        functools.partial(_dispatch_kernel, np_tiles=np_tiles),
        grid_spec=pltpu.PrefetchScalarGridSpec(
            num_scalar_prefetch=1, grid=(np_tiles + ns_tiles,),
            in_specs=[pl.BlockSpec((rows, HX_W), lambda i, ps: (jnp.minimum(i, np_tiles - 1), 0)),
                      pl.BlockSpec((rows, HX_W), lambda i, ps: (jnp.maximum(i - np_tiles, 0), 0)),
                      pl.BlockSpec(memory_space=pl.ANY)],
            out_specs=pl.BlockSpec(memory_space=pl.ANY),
            scratch_shapes=[pltpu.SemaphoreType.DMA(())]),
        out_shape=jax.ShapeDtypeStruct(xs_buf.shape, F32),
        input_output_aliases={3: 0},
        compiler_params=_cparams(("arbitrary",)),
        name="dispatch",
    )(pos, hx_p, hx_s, xs_buf)


def _moe_kernel(exp_ref, chg_ref, nxt_ref, valid_ref, xs_ref, wg_hbm, wu_hbm, wd_hbm,
                ys_ref, wg_f, wu_f, wd_f, wgu_s, wd_s, sem, *, layer):
    t = pl.program_id(0)

    def fetch(slot, e):
        return (pltpu.make_async_copy(wg_hbm.at[layer, e], wg_f.at[slot], sem.at[slot]),
                pltpu.make_async_copy(wu_hbm.at[layer, e], wu_f.at[slot], sem.at[slot]),
                pltpu.make_async_copy(wd_hbm.at[layer, e], wd_f.at[slot], sem.at[slot]))

    def recast(slot):
        rc = 128
        for i in range(D_MODEL // rc):
            rows = slice(i * rc, (i + 1) * rc)
            wgu_s[rows, slot * D_FF:(slot + 1) * D_FF] = wg_f[slot, rows, :].astype(BF16)
            wgu_s[rows, (2 + slot) * D_FF:(3 + slot) * D_FF] = wu_f[slot, rows, :].astype(BF16)
        for i in range(D_FF // rc):
            wd_s[slot * D_FF + i * rc:slot * D_FF + (i + 1) * rc, :] = wd_f[slot, i * rc:(i + 1) * rc, :].astype(BF16)

    for slot in range(2):
        @pl.when(t == 0)
        def _():
            for cp in fetch(slot, exp_ref[slot, 0]):
                cp.start()

        @pl.when(chg_ref[slot, t] > 0)
        def _():
            for cp in fetch(slot, exp_ref[slot, t]):
                cp.wait()
            recast(slot)

            @pl.when(nxt_ref[slot, t] >= 0)
            def _():
                for cp in fetch(slot, nxt_ref[slot, t]):
                    cp.start()

    @pl.when(valid_ref[t] > 0)
    def _():
        x = xs_ref[:, 0:D_MODEL].astype(BF16)
        w_a = xs_ref[:, D_MODEL:D_MODEL + 1]
        w_b = xs_ref[:, D_MODEL + 1:D_MODEL + 2]
        gu = _dot(x, wgu_s[...])
        lane = lax.broadcasted_iota(I32, (MOE_TM, 2 * D_FF), 1)
        act = _silu(gu[:, :2 * D_FF]) * gu[:, 2 * D_FF:] * jnp.where(lane < D_FF, w_a, w_b)
        ys_ref[...] = _dot(act.astype(BF16), wd_s[...])

    @pl.when(valid_ref[t] == 0)
    def _():
        ys_ref[...] = jnp.zeros_like(ys_ref)


def _moe_call(experts, chg, nxt, valid, xs, wg, wu, wd, layer):
    p = xs.shape[0]
    n_tiles = p // MOE_TM
    return pl.pallas_call(
        functools.partial(_moe_kernel, layer=layer),
        grid_spec=pltpu.PrefetchScalarGridSpec(
            num_scalar_prefetch=4, grid=(n_tiles,),
            in_specs=[pl.BlockSpec((MOE_TM, HX_W), lambda t, ex, ch, nx, va: (t, 0)),
                      pl.BlockSpec(memory_space=pl.ANY), pl.BlockSpec(memory_space=pl.ANY),
                      pl.BlockSpec(memory_space=pl.ANY)],
            out_specs=pl.BlockSpec((MOE_TM, D_MODEL), lambda t, ex, ch, nx, va: (t, 0)),
            scratch_shapes=[pltpu.VMEM((2, D_MODEL, D_FF), F32), pltpu.VMEM((2, D_MODEL, D_FF), F32),
                            pltpu.VMEM((2, D_FF, D_MODEL), F32),
                            pltpu.VMEM((D_MODEL, 4 * D_FF), BF16), pltpu.VMEM((2 * D_FF, D_MODEL), BF16),
                            pltpu.SemaphoreType.DMA((2,))]),
        out_shape=jax.ShapeDtypeStruct((p, D_MODEL), F32),
        compiler_params=_cparams(("arbitrary",)),
        name="moe",
    )(experts, chg, nxt, valid, xs, wg, wu, wd)


_PAIRS = [(0, 1), (0, 2), (0, 3), (1, 3), (1, 2), (3, 2)]
_EA_TABLE = np.array([g * EPG + p[0] for g in range(N_GROUPS) for p in _PAIRS], np.int32)
_EB_TABLE = np.array([g * EPG + p[1] for g in range(N_GROUPS) for p in _PAIRS], np.int32)


def _moe(hx_p, hx_s, bkt, xs_buf, wg, wu, wd, layer):
    n = hx_p.shape[0] + hx_s.shape[0]
    tm = MOE_TM
    ar = jnp.arange(N_BUCKETS, dtype=I32)
    blk = LANES
    nb = -(-n // blk)
    bkt_p = jnp.pad(bkt, (0, nb * blk - n), constant_values=N_BUCKETS)
    oh = (ar[:, None] == bkt_p[None, :]).astype(F32)
    triu = jnp.triu(jnp.ones((blk, blk), F32))
    inner = jnp.einsum('kbj,ji->kbi', oh.reshape(N_BUCKETS, nb, blk), triu)
    tot = inner[:, :, -1]
    outer = jnp.cumsum(tot, axis=1) - tot
    cs = (inner + outer[:, :, None]).reshape(N_BUCKETS, nb * blk)
    counts = (outer[:, -1] + tot[:, -1]).astype(I32)
    padded = ((counts + tm - 1) // tm) * tm
    ends = jnp.cumsum(padded)
    offs = ends - padded
    pos = jnp.sum(oh * (cs - 1.0 + offs.astype(F32)[:, None]), axis=0).astype(I32)[:n]
    n_tiles = -(-n // tm) + N_BUCKETS
    p = n_tiles * tm
    if xs_buf is None:
        xs_buf = jnp.zeros((p, HX_W), F32)
    tstart = jnp.arange(n_tiles, dtype=I32) * tm
    tb = jnp.sum((ends[None, :] <= tstart[:, None]).astype(I32), axis=1)
    n_used = ends[-1] // tm
    valid = (jnp.arange(n_tiles, dtype=I32) < n_used).astype(I32)
    last_b = tb[jnp.maximum(n_used - 1, 0)]
    tb = jnp.clip(jnp.where(valid > 0, tb, last_b), 0, N_BUCKETS - 1)
    ea = jnp.asarray(_EA_TABLE)[tb]
    eb = jnp.asarray(_EB_TABLE)[tb]
    experts = jnp.stack([ea, eb])
    chg = jnp.concatenate([jnp.ones((2, 1), I32), (experts[:, 1:] != experts[:, :-1]).astype(I32)], axis=1)
    tidx = jnp.arange(n_tiles, dtype=I32)
    chg_at = jnp.where(chg > 0, tidx[None, :], n_tiles)
    nxt_at = jnp.concatenate([lax.cummin(chg_at, axis=1, reverse=True)[:, 1:], jnp.full((2, 1), n_tiles, I32)], axis=1)
    nxt = jnp.where(nxt_at < n_tiles,
                    jnp.take_along_axis(experts, jnp.minimum(nxt_at, n_tiles - 1), axis=1), -1)
    xs = _dispatch_call(pos, hx_p, hx_s, xs_buf, tm)
    ys = _moe_call(experts, chg, nxt, valid, xs, wg, wu, wd, layer)
    return ys.at[pos].get(mode="promise_in_bounds", unique_indices=True), xs


def _final_kernel(h_ref, y_ref, g_ref, gain_ref, o_ref):
    bb, tt, d = h_ref.shape
    h = h_ref[...] + g_ref[...] * y_ref[...].reshape(bb, tt, d)
    o_ref[...] = _rms(h) * gain_ref[...]


def _final_call(h, y2d, off, g2, gain, *, bb, tt, name):
    b, t, d = h.shape
    nti = t // tt
    rows = bb * tt
    xspec = pl.BlockSpec((bb, tt, d), lambda bi, i: (bi, i, 0))
    return pl.pallas_call(
        _final_kernel,
        grid=(b // bb, nti),
        in_specs=[xspec, pl.BlockSpec((rows, d), lambda bi, i: (off + bi * nti + i, 0)),
                  pl.BlockSpec((bb, 1, d), lambda bi, i: (bi, 0, 0)),
                  pl.BlockSpec((1, 1, d), lambda bi, i: (0, 0, 0))],
        out_specs=xspec,
        out_shape=jax.ShapeDtypeStruct((b, t, d), F32),
        compiler_params=_cparams(("parallel", "parallel")),
        name=name,
    )(h, y2d, g2, gain.reshape(1, 1, d))


def _group_cfg(b, t):
    if t >= 512:
        return dict(bb=1, tt=512, tt_out=1024, L=CHUNK, CB=4, Lc=4 * CHUNK, CBc=1)
    return dict(bb=b, tt=t, tt_out=t, L=min(CHUNK, t), CB=1, Lc=min(CHUNK, t), CBc=1)


def _rope_tables(pos0, t):
    half = DK_C // 2
    inv = np.power(np.float64(ROPE_BASE), -np.linspace(0.0, 1.0, half, dtype=np.float64))
    ang = (np.float64(pos0) + np.arange(t, dtype=np.float64))[:, None] * inv[None, :]
    return jnp.asarray(np.cos(ang), F32), jnp.asarray(np.sin(ang), F32)


def kernel(x_prompt, x_sample, c_prompt, c_sample, state_a_conv, state_a_rec, state_b_c, state_b_n, state_b_m,
           state_c_rec, w_mod, b_mod, norm_mix, norm_ffn, w_in_ab, conv_a, a_log, dt_bias, norm_a, gate_bias_b,
           norm_b, w_out_ab, w_in_c, w_out_c, w_router, router_bias, w_gate, w_up, w_down, norm_final):
    d = D_MODEL
    bp, tp, _ = x_prompt.shape
    bs, ts, _ = x_sample.shape
    n_ab, n_c = w_in_ab.shape[0], w_in_c.shape[0]

    mod_all = _mod_call(jnp.concatenate([c_prompt, c_sample], axis=0), w_mod, b_mod)

    o = np.cumsum((0,) + AB_SIZES)
    w_ab_main = jnp.concatenate([w_in_ab[:, :, o[0]:o[1]].astype(BF16), w_in_ab[:, :, o[3]:o[7]].astype(BF16),
                                 w_in_ab[:, :, o[8]:o[9]].astype(BF16)], axis=-1)
    w_ab_gate = jnp.concatenate([w_in_ab[:, :, o[1]:o[3]], w_in_ab[:, :, o[7]:o[8]]], axis=-1)
    w_ab_gate_pad = jnp.pad(w_ab_gate, ((0, 0), (0, 0), (0, LANES - N_GATE_COLS)))
    w_c = w_in_c.astype(BF16)
    w_out_ab_b = w_out_ab.astype(BF16)
    w_out_c_b = w_out_c.astype(BF16)
    wrt = w_router.T
    rb = router_bias.reshape(N_EXPERTS, 1)
    zeros4 = jnp.zeros((n_ab, H_A), F32)
    gpar_row0 = jnp.concatenate([a_log, jnp.zeros((n_ab, LANES - H_A), F32)], axis=1)
    gpar_row1 = jnp.concatenate([dt_bias, zeros4, gate_bias_b, jnp.zeros((n_ab, LANES - 16), F32)], axis=1)
    gpar = jnp.concatenate([gpar_row0[:, None], gpar_row1[:, None], jnp.zeros((n_ab, 6, LANES), F32)], axis=1)
    gpart = jnp.swapaxes(jnp.concatenate([gpar[:, :2, :N_GATE_COLS], jnp.zeros((n_ab, LANES - 2, N_GATE_COLS), F32)],
                                         axis=1), 1, 2)

    groups = []
    zf = lambda *s: jnp.zeros(s, F32)
    groups.append(dict(
        x=x_prompt, b=bp, t=tp, pos0=0.0, mod=mod_all[:, :bp], row0=0,
        conv=zf(n_ab, bp, CONV_W - 1, QKV_A), sa=zf(n_ab, bp, H_A, DK_A, DV_A), cb=zf(n_ab, bp, H_B, DQK_B, DV_B),
        nb=zf(n_ab, bp, H_B, DQK_B), mb=zf(n_ab, bp, H_B), sc=zf(n_c, bp, H_C, DK_C, DV_C)))
    groups.append(dict(
        x=x_sample, b=bs, t=ts, pos0=float(PAST_LEN), mod=mod_all[:, bp:], row0=bp * tp,
        conv=state_a_conv, sa=state_a_rec, cb=state_b_c, nb=state_b_n, mb=state_b_m, sc=state_c_rec))
    n_tot = bp * tp + bs * ts
    for g in groups:
        g.update(_group_cfg(g["b"], g["t"]))
        g["h"] = g["x"]
        g["prev"] = None
        g["new_ab"], g["new_c"] = [], []
        g["cos"], g["sin"] = _rope_tables(g["pos0"], g["t"])
        g["roff"] = g["row0"] // (g["bb"] * g["tt"])

    xs_buf = None
    for layer in range(DEPTH):
        li = layer // 2
        bkts, hxs = [], []
        for gi, g in enumerate(groups):
            b, t, bb, tt = g["b"], g["t"], g["bb"], g["tt"]
            mods = [m.reshape(b, 1, d) for m in jnp.split(g["mod"][layer], N_MOD, axis=-1)]
            sh1, sc1, g1, sh2, sc2, g2 = mods
            tag = f"l{layer}g{gi}"
            if layer % 2 == 0:
                outs = _ln_mm_call(g["h"], g["prev"], norm_mix[layer], sc1, sh1, w_ab_main[li],
                                   (w_ab_gate_pad[li], g["conv"][li], conv_a[li]), bb=bb, tt=tt, tn=512,
                                   out_dtype=F32, name="ln_ab_" + tag)
                if g["prev"] is not None:
                    g["h"], outs = outs[0], outs[1:]
                proj, gates, gates_t, qkv, conv_n = outs
                gates_t = gates_t.reshape(N_GATE_COLS, b, t).transpose(1, 0, 2)
                mixed, sa_n, cb_n, nb_n, mb_n = _mixer_ab_call(
                    qkv, proj, gates, gates_t, g["sa"][li], g["cb"][li],
                    g["nb"][li].reshape(b, H_B, 1, DQK_B), g["mb"][li].reshape(b, H_B, 1, 1),
                    gpar[li], gpart[li], norm_a[li].reshape(1, DV_A), norm_b[li].reshape(1, DV_B),
                    L=g["L"], CB=g["CB"], name="mixer_ab_" + tag)
                g["new_ab"].append((conv_n, sa_n, cb_n, nb_n.reshape(b, H_B, DQK_B), mb_n.reshape(b, H_B)))
                wout = w_out_ab_b[li]
            else:
                outs = _ln_mm_call(g["h"], g["prev"], norm_mix[layer], sc1, sh1, w_c[li], None,
                                   bb=bb, tt=tt, tn=512, out_dtype=BF16, name="ln_c_" + tag)
                if g["prev"] is not None:
                    g["h"], outs = outs[0], outs[1:]
                (proj,) = outs
                mixed, sc_n = _mixer_c_call(proj, g["cos"], g["sin"], g["sc"][li], L=g["Lc"], CB=g["CBc"],
                                            name="mixer_c_" + tag)
                g["new_c"].append(sc_n)
                wout = w_out_c_b[li]
            tto = g["tt_out"]
            g["h"], hx, bkt = _out_route_call(mixed, wout, g["h"], g1, norm_ffn[layer], sc2, sh2, wrt, rb,
                                              bb=bb, tt=tto, name="out_route_" + tag)
            bkts.append(bkt.reshape(-1))
            hxs.append(hx)
            g["g2"] = g2
        y_moe, xs_buf = _moe(hxs[0], hxs[1], jnp.concatenate(bkts), xs_buf, w_gate, w_up, w_down, layer)
        for g in groups:
            g["prev"] = (y_moe, g["roff"], g["g2"])

    outs = []
    for gi, g in enumerate(groups):
        y2d, off, g2 = g["prev"]
        y = _final_call(g["h"], y2d, off, g2, norm_final, bb=g["bb"], tt=g["tt"], name=f"final_g{gi}")
        ab = [jnp.stack(s) for s in zip(*g["new_ab"])]
        outs.append((y, ab[0], ab[1], ab[2], ab[3], ab[4], jnp.stack(g["new_c"])))
    p, s = outs
    return (p[0], s[0], p[1], p[2], p[3], p[4], p[5], p[6], s[1], s[2], s[3], s[4], s[5], s[6])
```

```python
import functools

import numpy as np
import jax
import jax.numpy as jnp
from jax import lax
from jax.experimental import pallas as pl
from jax.experimental.pallas import tpu as pltpu

F32 = jnp.float32
BF16 = jnp.bfloat16
I32 = jnp.int32

D_MODEL = 1024
DEPTH = 2
CHUNK = 64
H_A, DK_A, DV_A, CONV_W = 4, 128, 128, 4
QKV_A = H_A * (2 * DK_A + DV_A)
H_B, DQK_B, DV_B = 4, 64, 128
H_C, DK_C, DV_C = 4, 256, 512
ROPE_BASE = 10000.0
PAST_LEN = 4096
AB_SIZES = (QKV_A, H_A, H_A, H_A * DV_A, H_B * DQK_B, H_B * DQK_B, H_B * DV_B, 2 * H_B, H_B * DV_B)
MIX_AB = H_A * DV_A + H_B * DV_B
MIX_C = H_C * DV_C
IN_C = 2 * H_C * DK_C + 2 * H_C * DV_C
N_EXPERTS, N_GROUPS, EPG, D_FF = 16, 4, 4, 512
N_MOD = 6
EPS = 1e-6

LANES = 128
AB_MAIN = 3584
N_GATE_COLS = 16
HX_W = D_MODEL + LANES
N_BUCKETS = N_GROUPS * 6
MOE_TM = 256
DISPATCH_ROWS = 256
VMEM_LIMIT = 48 * 1024 * 1024

NN = ((1,), (0,))
NT = ((1,), (1,))
TN = ((0,), (0,))


def _dot(a, b, dims=NN):
    return lax.dot_general(a, b, (dims, ((), ())), preferred_element_type=F32)


def _mmb(a, b, dims=NN):
    return _dot(a.astype(BF16), b.astype(BF16), dims)


def _split2(x):
    hi = x.astype(BF16)
    lo = (x - hi.astype(F32)).astype(BF16)
    return hi, lo


def _split3(x):
    hi = x.astype(BF16)
    r = x - hi.astype(F32)
    mid = r.astype(BF16)
    lo = (r - mid.astype(F32)).astype(BF16)
    return hi, mid, lo


def _mm3(a, b, dims=NN):
    ah, al = _split2(a)
    bh, bl = _split2(b)
    return _dot(ah, bh, dims) + (_dot(ah, bl, dims) + _dot(al, bh, dims))


def _mm_mask_l(mask_bf16, x, dims=NN):
    h, m, l = _split3(x)
    return _dot(mask_bf16, h, dims) + (_dot(mask_bf16, m, dims) + _dot(mask_bf16, l, dims))


def _mm_mask_r(x, mask_bf16):
    h, m, l = _split3(x)
    return _dot(h, mask_bf16) + (_dot(m, mask_bf16) + _dot(l, mask_bf16))


def _sigmoid(x):
    return 1.0 / (1.0 + jnp.exp(-x))


def _silu(x):
    return x * _sigmoid(x)


def _softplus(x):
    return jnp.maximum(x, 0.0) + jnp.log(1.0 + jnp.exp(-jnp.abs(x)))


def _rms(x, eps=EPS):
    return x * lax.rsqrt(jnp.mean(x * x, axis=-1, keepdims=True) + eps)


def _cparams(sem):
    return pltpu.CompilerParams(dimension_semantics=sem, vmem_limit_bytes=VMEM_LIMIT)


def _mod_kernel(c_ref, w_ref, b_ref, o_ref):
    c = c_ref[...]
    o_ref[0] = _mm3(_silu(c), w_ref[0]) + b_ref[0]


def _mod_call(c_all, w_mod, b_mod):
    bt = c_all.shape[0]
    e = w_mod.shape[-1]
    tn = 1024
    return pl.pallas_call(
        _mod_kernel,
        grid=(DEPTH, e // tn),
        in_specs=[pl.BlockSpec((bt, D_MODEL), lambda l, j: (0, 0)),
                  pl.BlockSpec((1, D_MODEL, tn), lambda l, j: (l, 0, j)),
                  pl.BlockSpec((1, 1, tn), lambda l, j: (l, 0, j))],
        out_specs=pl.BlockSpec((1, bt, tn), lambda l, j: (l, 0, j)),
        out_shape=jax.ShapeDtypeStruct((DEPTH, bt, e), F32),
        compiler_params=_cparams(("parallel", "parallel")),
        name="mod",
    )(c_all, w_mod, b_mod.reshape(DEPTH, 1, e))


def _ln_mm_kernel(*refs, has_prev, has_gates, tn):
    it = iter(refs)
    x_ref = next(it)
    if has_prev:
        yp_ref, gp_ref = next(it), next(it)
    gain_ref, sc_ref, sh_ref, w_ref = next(it), next(it), next(it), next(it)
    if has_gates:
        wg_ref, conv0_ref, convw_ref = next(it), next(it), next(it)
    if has_prev:
        h_ref = next(it)
    proj_ref = next(it)
    if has_gates:
        g_ref, gt_ref, qkv_ref, conv_o = next(it), next(it), next(it), next(it)
    hn_s = next(it)
    if has_gates:
        tail_s = next(it)

        @pl.when(pl.program_id(1) == 0)
        def _():
            tail_s[...] = conv0_ref[...]

    bb, tt, d = x_ref.shape
    rows = bb * tt
    x = x_ref[...]
    if has_prev:
        x = x + gp_ref[...] * yp_ref[...].reshape(bb, tt, d)
        h_ref[...] = x
    hn = _rms(x) * gain_ref[...]
    hn = hn * (1.0 + sc_ref[...]) + sh_ref[...]
    hn2 = hn.reshape(rows, d)
    hn_s[...] = hn2.astype(BF16)
    if has_gates:
        gates = _mm3(hn2, wg_ref[...])
        g_ref[...] = gates.reshape(bb, tt, LANES)
        gt_ref[...] = gates.T[0:N_GATE_COLS, :]
    n_conv = QKV_A // tn if has_gates else 0
    n_tiles = w_ref.shape[-1] // tn

    def project(j):
        return _dot(hn_s[...], w_ref[:, j * tn:(j + 1) * tn]).reshape(bb, tt, tn)

    def finish(j, y):
        cols = slice(j * tn, (j + 1) * tn)
        if j < n_conv:
            xp = jnp.concatenate([tail_s[:, :, cols], y], axis=1)
            acc = None
            for k in range(CONV_W):
                term = convw_ref[k:k + 1, cols] * xp[:, k:k + tt, :]
                acc = term if acc is None else acc + term
            qkv_ref[:, :, cols] = _silu(acc).astype(qkv_ref.dtype)
            tail = y[:, tt - (CONV_W - 1):tt, :]
            tail_s[:, :, cols] = tail
            conv_o[:, :, cols] = tail
        else:
            pcols = slice((j - n_conv) * tn, (j - n_conv + 1) * tn)
            proj_ref[:, :, pcols] = y.astype(proj_ref.dtype)

    y = project(0)
    for j in range(n_tiles):
        y_next = project(j + 1) if j + 1 < n_tiles else None
        finish(j, y)
        y = y_next


def _ln_mm_call(x, prev, gain, sc, sh, w, gates_w, *, bb, tt, tn, out_dtype, name):
    b, t, d = x.shape
    e = w.shape[1]
    rows = bb * tt
    nti = t // tt
    grid = (b // bb, nti)
    has_prev = prev is not None
    has_gates = gates_w is not None
    xspec = pl.BlockSpec((bb, tt, d), lambda bi, i: (bi, i, 0))
    mspec = pl.BlockSpec((bb, 1, d), lambda bi, i: (bi, 0, 0))
    in_specs, args = [xspec], [x]
    if has_prev:
        y2d, off, gp = prev
        in_specs += [pl.BlockSpec((rows, d), lambda bi, i: (off + bi * nti + i, 0)), mspec]
        args += [y2d, gp]
    in_specs += [pl.BlockSpec((1, 1, d), lambda bi, i: (0, 0, 0)), mspec, mspec,
                 pl.BlockSpec((d, e), lambda bi, i: (0, 0), pipeline_mode=pl.Buffered(1))]
    args += [gain.reshape(1, 1, d), sc, sh, w]
    tail_spec = pl.BlockSpec((bb, CONV_W - 1, QKV_A), lambda bi, i: (bi, 0, 0))
    if has_gates:
        wg, conv0, convw = gates_w
        in_specs += [pl.BlockSpec((d, LANES), lambda bi, i: (0, 0)), tail_spec,
                     pl.BlockSpec((CONV_W, QKV_A), lambda bi, i: (0, 0))]
        args += [wg, conv0, convw]
    e_proj = e - QKV_A if has_gates else e
    out_specs, out_shape = [], []
    if has_prev:
        out_specs.append(xspec)
        out_shape.append(jax.ShapeDtypeStruct((b, t, d), F32))
    out_specs.append(pl.BlockSpec((bb, tt, e_proj), lambda bi, i: (bi, i, 0)))
    out_shape.append(jax.ShapeDtypeStruct((b, t, e_proj), out_dtype))
    scratch = [pltpu.VMEM((rows, d), BF16)]
    if has_gates:
        out_specs += [pl.BlockSpec((bb, tt, LANES), lambda bi, i: (bi, i, 0)),
                      pl.BlockSpec((N_GATE_COLS, rows), lambda bi, i: (0, bi * nti + i)),
                      pl.BlockSpec((bb, tt, QKV_A), lambda bi, i: (bi, i, 0)), tail_spec]
        out_shape += [jax.ShapeDtypeStruct((b, t, LANES), F32),
                      jax.ShapeDtypeStruct((N_GATE_COLS, b * t), F32),
                      jax.ShapeDtypeStruct((b, t, QKV_A), BF16),
                      jax.ShapeDtypeStruct((b, CONV_W - 1, QKV_A), F32)]
        scratch.append(pltpu.VMEM((bb, CONV_W - 1, QKV_A), F32))
    return pl.pallas_call(
        functools.partial(_ln_mm_kernel, has_prev=has_prev, has_gates=has_gates, tn=tn),
        grid=grid, in_specs=in_specs, out_specs=out_specs, out_shape=out_shape,
        scratch_shapes=scratch,
        compiler_params=_cparams(("parallel", "arbitrary" if has_gates else "parallel")),
        name=name,
    )(*args)


def _tri_inv_all(a_list, eye, length, mm):
    xs = [-a for a in a_list]
    ps = [eye + x for x in xs]
    n = 2
    while n < length:
        xs = [mm(x, x) for x in xs]
        ps = [p + mm(p, x) for p, x in zip(ps, xs)]
        n *= 2
    return ps


def _mixer_ab_kernel(qkv_ref, z_ref, qkb_ref, vb_ref, ob_ref, g_ref, gt_ref,
                     sa0_ref, cb0_ref, nb0_ref, mb0_ref,
                     gpar_ref, gpart_ref, na_ref, nbn_ref,
                     mixed_ref, sa_o, cb_o, nb_o, mb_o, *, L, CB):
    @pl.when(pl.program_id(1) == 0)
    def _():
        sa_o[...] = sa0_ref[...]
        cb_o[...] = cb0_ref[...]
        nb_o[...] = nb0_ref[...]
        mb_o[...] = mb0_ref[...]

    ii = lax.broadcasted_iota(I32, (L, L), 0)
    jj = lax.broadcasted_iota(I32, (L, L), 1)
    tri = ii >= jj
    strict = ii > jj
    eye = (ii == jj).astype(F32)
    tril_b = tri.astype(BF16)
    triu_b = (ii <= jj).astype(BF16)
    rowi = lax.broadcasted_iota(I32, (N_GATE_COLS, L), 0)
    gpar = gpar_ref[...]
    gpart = gpart_ref[...]
    neat = -jnp.exp(gpart[:, 0:1])
    norm_a = na_ref[...]
    norm_b = nbn_ref[...]

    mm = _mmb

    def qkv_tile(r0, col0):
        return qkv_ref[0, r0:r0 + L, col0:col0 + DK_A].astype(F32)

    chunks = range(CB)
    items = [(c, h) for c in chunks for h in range(H_A)]
    rs = [slice(c * L, (c + 1) * L) for c in chunks]

    pre = [g_ref[0, r, :] + gpar[1:2, :] for r in rs]
    pret = [gt_ref[0, :, r] + gpart[:, 1:2] for r in rs]
    csrct = [jnp.where(rowi < 4, neat * _softplus(p), jnp.where(rowi >= 12, -_softplus(-p), 0.0)) for p in pret]
    gcumt = [_mm_mask_r(x, triu_b) for x in csrct]
    gcum = [_mm_mask_l(tril_b, x, NT) for x in csrct]
    beta_all = [_sigmoid(p) for p in pre]

    q = [qkv_tile(c * L, h * DK_A) for c, h in items]
    k = [qkv_tile(c * L, H_A * DK_A + h * DK_A) for c, h in items]
    v = [qkv_tile(c * L, 2 * H_A * DK_A + h * DV_A) for c, h in items]
    q = [x * lax.rsqrt(jnp.sum(x * x, axis=-1, keepdims=True) + EPS) * (DK_A ** -0.5) for x in q]
    k = [x * lax.rsqrt(jnp.sum(x * x, axis=-1, keepdims=True) + EPS) for x in k]
    gc_c = [gcum[c][:, h:h + 1] for c, h in items]
    gc_r = [gcumt[c][h:h + 1, :] for c, h in items]
    decay = [jnp.where(tri, jnp.exp(jnp.where(tri, a - b, 0.0)), 0.0) for a, b in zip(gc_c, gc_r)]
    beta_c = [beta_all[c][:, 4 + h:5 + h] for c, h in items]
    kb = [x * b for x, b in zip(k, beta_c)]
    a_list = [jnp.where(strict, mm(x, y, NT) * d, 0.0) for x, y, d in zip(kb, k, decay)]
    attn = [jnp.where(tri, mm(x, y, NT) * d, 0.0) for x, y, d in zip(q, k, decay)]
    eg = [jnp.exp(x) for x in gc_c]
    gl = [x[L - 1:L, :] for x in gc_c]
    rhs = [jnp.concatenate([x * e, y * b], axis=-1) for x, e, y, b in zip(kb, eg, v, beta_c)]
    qeg = [x * e for x, e in zip(q, eg)]
    kdec = [x * jnp.exp(g - gc) for x, g, gc in zip(k, gl, gc_c)]
    sdec = [jnp.exp(g) for g in gl]
    a_list = [a.astype(BF16) for a in a_list]
    tinv = _tri_inv_all([a.astype(F32) for a in a_list], eye, L, mm)
    sol = [mm(t, r) for t, r in zip(tinv, rhs)]
    sol_hl = [_split2(s) for s in sol]
    resid = [r - (s + (_dot(a, hi) + _dot(a, lo))) for r, s, a, (hi, lo) in zip(rhs, sol, a_list, sol_hl)]
    sol = [s + mm(t, r) for s, t, r in zip(sol, tinv, resid)]

    heads = range(H_B)
    qb = [qkb_ref[0, rs[c], h * DQK_B:(h + 1) * DQK_B] * (DQK_B ** -0.5) for c, h in items]
    kbb = [qkb_ref[0, rs[c], H_B * DQK_B + h * DQK_B:H_B * DQK_B + (h + 1) * DQK_B] for c, h in items]
    b_c = [gcum[c][:, 12 + h:13 + h] for c, h in items]
    dm = [jnp.where(tri, gcum[c][:, 12 + h:13 + h] - gcumt[c][12 + h:13 + h, :] + pret[c][8 + h:9 + h, :], -jnp.inf)
          for c, h in items]
    dmax = [jnp.max(x, axis=-1, keepdims=True) for x in dm]
    qkm = [mm(x, y, NT) for x, y in zip(qb, kbb)]
    li_c = [pre[c][:, 8 + h:9 + h] for c, h in items]

    for c in chunks:
        ia = [c * H_A + h for h in heads]
        s = [sa_o[0, h] for h in heads]
        v_new = [sol[i][:, DK_A:] - mm(sol[i][:, :DK_A], s[h]) for h, i in enumerate(ia)]
        m_prev = [mb_o[0, h] for h in heads]
        a0 = [b_c[i] + m_prev[h] for h, i in enumerate(ia)]
        m_t = [jnp.maximum(a0[h], dmax[i]) for h, i in enumerate(ia)]
        w0 = [jnp.exp(a0[h] - m_t[h]) for h in heads]
        sm = [qkm[i] * jnp.exp(dm[i] - m_t[h]) for h, i in enumerate(ia)]
        cst = [cb_o[0, h] for h in heads]
        nst = [nb_o[0, h] for h in heads]
        vb = [vb_ref[0, rs[c], h * DV_B:(h + 1) * DV_B] for h in heads]
        o_a = [mm(qeg[i], s[h]) + mm(attn[i], v_new[h]) for h, i in enumerate(ia)]
        for h, i in enumerate(ia):
            sa_o[0, h] = s[h] * sdec[i] + mm(kdec[i], v_new[h], TN)
        num = [w0[h] * mm(qb[i], cst[h]) + mm(sm[h], vb[h]) for h, i in enumerate(ia)]
        den = [w0[h] * jnp.sum(qb[i] * nst[h], axis=-1, keepdims=True) + jnp.sum(sm[h], axis=-1, keepdims=True)
               for h, i in enumerate(ia)]
        hh = [num[h] / jnp.maximum(jnp.abs(den[h]), jnp.exp(-m_t[h])) for h in heads]
        m_new = [m_t[h][L - 1:L, :] for h in heads]
        kw = [kbb[i] * jnp.exp(b_c[i][L - 1:L, :] - b_c[i] + li_c[i] - m_new[h]) for h, i in enumerate(ia)]
        for h in heads:
            w0l = w0[h][L - 1:L, :]
            cb_o[0, h] = w0l * cst[h] + mm(kw[h], vb[h], TN)
            nb_o[0, h] = w0l * nst[h] + jnp.sum(kw[h], axis=0, keepdims=True)
            mb_o[0, h] = m_new[h]
        for h in heads:
            zg = z_ref[0, rs[c], h * DV_A:(h + 1) * DV_A]
            mixed_ref[0, rs[c], h * DV_A:(h + 1) * DV_A] = (_rms(o_a[h]) * norm_a * _silu(zg)).astype(mixed_ref.dtype)
            og = ob_ref[0, rs[c], h * DV_B:(h + 1) * DV_B]
            c0 = H_A * DV_A + h * DV_B
            mixed_ref[0, rs[c], c0:c0 + DV_B] = (_rms(hh[h]) * norm_b * _sigmoid(og)).astype(mixed_ref.dtype)


def _mixer_ab_call(qkv, proj, gates, gates_t, sa0, cb0, nb0, mb0, gpar, gpart, norm_a, norm_b, *, L, CB, name):
    b, t, _ = proj.shape
    tb = L * CB
    nt = t // tb

    def col(width, idx):
        return pl.BlockSpec((1, tb, width), lambda bi, i: (bi, i, idx))

    def const(shape):
        return pl.BlockSpec(shape, lambda bi, i: (0,) * len(shape))

    def state(shape):
        return pl.BlockSpec((1,) + shape, lambda bi, i: (bi,) + (0,) * len(shape))

    in_specs = [col(QKV_A, 0), col(512, 0), col(512, 1), col(512, 2), col(512, 3),
                col(LANES, 0), pl.BlockSpec((1, N_GATE_COLS, tb), lambda bi, i: (bi, 0, i)),
                state((H_A, DK_A, DV_A)), state((H_B, DQK_B, DV_B)),
                state((H_B, 1, DQK_B)), state((H_B, 1, 1)),
                const((8, LANES)), const((N_GATE_COLS, LANES)),
                const((1, DV_A)), const((1, DV_B))]
    out_specs = [pl.BlockSpec((1, tb, MIX_AB), lambda bi, i: (bi, i, 0)),
                 state((H_A, DK_A, DV_A)), state((H_B, DQK_B, DV_B)),
                 state((H_B, 1, DQK_B)), state((H_B, 1, 1))]
    out_shape = [jax.ShapeDtypeStruct((b, t, MIX_AB), BF16),
                 jax.ShapeDtypeStruct((b, H_A, DK_A, DV_A), F32),
                 jax.ShapeDtypeStruct((b, H_B, DQK_B, DV_B), F32),
                 jax.ShapeDtypeStruct((b, H_B, 1, DQK_B), F32),
                 jax.ShapeDtypeStruct((b, H_B, 1, 1), F32)]
    return pl.pallas_call(
        functools.partial(_mixer_ab_kernel, L=L, CB=CB),
        grid=(b, nt), in_specs=in_specs, out_specs=out_specs, out_shape=out_shape,
        compiler_params=_cparams(("parallel", "arbitrary")),
        name=name,
    )(qkv, proj, proj, proj, proj, gates, gates_t, sa0, cb0, nb0, mb0, gpar, gpart, norm_a, norm_b)


def _mixer_c_kernel(q_ref, k_ref, v_ref, g_ref, cos_ref, sin_ref, s0_ref, mixed_ref, s_o, intra_s, *, L, CB):
    log_gamma = [float(np.log1p(-np.exp2(-5.0 - h))) for h in range(H_C)]

    @pl.when(pl.program_id(1) == 0)
    def _():
        s_o[...] = s0_ref[...]
        ii = lax.broadcasted_iota(I32, (L, L), 0)
        jj = lax.broadcasted_iota(I32, (L, L), 1)
        rel = (ii - jj).astype(F32)
        for h in range(H_C):
            intra_s[h] = jnp.where(rel >= 0, jnp.exp(log_gamma[h] * jnp.maximum(rel, 0.0)), 0.0)

    idx = lax.broadcasted_iota(I32, (L, 1), 0).astype(F32)
    half = DK_C // 2

    def rope(x, cos, sin):
        x1, x2 = x[:, :half], x[:, half:]
        return jnp.concatenate([x1 * cos - x2 * sin, x1 * sin + x2 * cos], axis=-1)

    for c in range(CB):
        r0 = c * L
        cos = cos_ref[r0:r0 + L, :]
        sin = sin_ref[r0:r0 + L, :]
        for h in range(H_C):
            lg = log_gamma[h]
            intra = intra_s[h]
            q_decay = jnp.exp(lg * (idx + 1.0))
            k_decay = jnp.exp(lg * (L - 1.0 - idx))
            s_decay = float(np.exp(lg * L))
            q = rope(q_ref[0, r0:r0 + L, h * DK_C:(h + 1) * DK_C].astype(F32), cos, sin)
            k = rope(k_ref[0, r0:r0 + L, h * DK_C:(h + 1) * DK_C].astype(F32), cos, sin) * (DK_C ** -0.5)
            v = v_ref[0, r0:r0 + L, h * DV_C:(h + 1) * DV_C]
            s = s_o[0, h]
            o = _mmb(_mmb(q, k, NT) * intra, v) + _mmb(q * q_decay, s)
            s_o[0, h] = s_decay * s + _mmb(k * k_decay, v, TN)
            gg = g_ref[0, r0:r0 + L, h * DV_C:(h + 1) * DV_C].astype(F32)
            o = _rms(o) * _silu(gg)
            mixed_ref[0, r0:r0 + L, h * DV_C:(h + 1) * DV_C] = o.astype(mixed_ref.dtype)


def _mixer_c_call(proj, cos, sin, s0, *, L, CB, name):
    b, t, _ = proj.shape
    tb = L * CB
    nt = t // tb
    qk_w, v_w = H_C * DK_C, H_C * DV_C
    sspec = pl.BlockSpec((1, H_C, DK_C, DV_C), lambda bi, i: (bi, 0, 0, 0))
    tspec = pl.BlockSpec((tb, DK_C // 2), lambda bi, i: (i, 0))
    return pl.pallas_call(
        functools.partial(_mixer_c_kernel, L=L, CB=CB),
        grid=(b, nt),
        in_specs=[pl.BlockSpec((1, tb, qk_w), lambda bi, i: (bi, i, 0)),
                  pl.BlockSpec((1, tb, qk_w), lambda bi, i: (bi, i, 1)),
                  pl.BlockSpec((1, tb, v_w), lambda bi, i: (bi, i, 1)),
                  pl.BlockSpec((1, tb, v_w), lambda bi, i: (bi, i, 2)),
                  tspec, tspec, sspec],
        out_specs=[pl.BlockSpec((1, tb, MIX_C), lambda bi, i: (bi, i, 0)), sspec],
        out_shape=[jax.ShapeDtypeStruct((b, t, MIX_C), BF16),
                   jax.ShapeDtypeStruct((b, H_C, DK_C, DV_C), F32)],
        scratch_shapes=[pltpu.VMEM((H_C, L, L), F32)],
        compiler_params=_cparams(("parallel", "arbitrary")),
        name=name,
    )(proj, proj, proj, proj, cos, sin, s0)


def _out_route_kernel(mixed_ref, wout_ref, h_ref, g1_ref, gain_ref, sc_ref, sh_ref, wrt_ref, rb_ref,
                      hnew_ref, hx_ref, bkt_ref):
    bb, tt, m = mixed_ref.shape
    d = h_ref.shape[-1]
    rows = bb * tt
    y = _dot(mixed_ref[...].reshape(rows, m), wout_ref[...])
    h = h_ref[...] + g1_ref[...] * y.reshape(bb, tt, d)
    hnew_ref[...] = h
    hn = _rms(h) * gain_ref[...]
    hn = (hn * (1.0 + sc_ref[...]) + sh_ref[...]).reshape(rows, d)
    hx_ref[:, 0:d] = hn

    logits = _mm3(wrt_ref[...], hn, NT)
    score = _sigmoid(logits)
    sel = score + rb_ref[...]

    def row(a, r):
        return a[r:r + 1, :]

    gscore = []
    for g in range(N_GROUPS):
        a, b, c, e = (row(sel, EPG * g + i) for i in range(EPG))
        hi_ab, lo_ab, hi_ce, lo_ce = jnp.maximum(a, b), jnp.minimum(a, b), jnp.maximum(c, e), jnp.minimum(c, e)
        top1 = jnp.maximum(hi_ab, hi_ce)
        top2 = jnp.maximum(jnp.maximum(lo_ab, lo_ce), jnp.minimum(hi_ab, hi_ce))
        gscore.append(top1 + top2)
    best = jnp.zeros((1, rows), I32)
    bestv = gscore[0]
    for g in range(1, N_GROUPS):
        upd = gscore[g] > bestv
        best = jnp.where(upd, g, best)
        bestv = jnp.where(upd, gscore[g], bestv)

    def pick(a, i):
        out = row(a, i)
        for g in range(1, N_GROUPS):
            out = jnp.where(best == g, row(a, EPG * g + i), out)
        return out

    vsel = [pick(sel, i) for i in range(EPG)]
    vsc = [pick(score, i) for i in range(EPG)]
    i1 = jnp.zeros((1, rows), I32)
    v1 = vsel[0]
    for i in range(1, EPG):
        upd = vsel[i] > v1
        i1 = jnp.where(upd, i, i1)
        v1 = jnp.where(upd, vsel[i], v1)
    i2 = jnp.full((1, rows), -1, I32)
    v2 = jnp.full((1, rows), -jnp.inf, F32)
    for i in range(EPG):
        cand = jnp.where(i1 == i, -jnp.inf, vsel[i])
        upd = cand > v2
        i2 = jnp.where(upd, i, i2)
        v2 = jnp.where(upd, cand, v2)
    s1 = jnp.zeros((1, rows), F32)
    s2 = jnp.zeros((1, rows), F32)
    for i in range(EPG):
        s1 = jnp.where(i1 == i, vsc[i], s1)
        s2 = jnp.where(i2 == i, vsc[i], s2)
    den = s1 + s2
    w1 = s1 / den
    w2 = s2 / den
    first_lo = i1 < i2
    lo = jnp.minimum(i1, i2)
    hi = jnp.maximum(i1, i2)
    wlo = jnp.where(first_lo, w1, w2)
    whi = jnp.where(first_lo, w2, w1)
    pair = jnp.where(lo == 0, hi - 1, jnp.where(lo == 1, 6 - hi, 5))
    bkt_ref[...] = (best * 6 + pair).reshape(1, 1, rows)
    wa = jnp.where(pair == 5, whi, wlo)
    wb = jnp.where(pair == 5, wlo, whi)

    aux = jnp.concatenate([wa, wb, jnp.zeros((LANES - 2, rows), F32)], axis=0)
    hx_ref[:, d:d + LANES] = aux.T


def _out_route_call(mixed, wout, h, g1, gain, sc, sh, wrt, rb, *, bb, tt, name):
    b, t, m = mixed.shape
    d = h.shape[-1]
    rows = bb * tt
    nti = t // tt
    xspec = pl.BlockSpec((bb, tt, d), lambda bi, i: (bi, i, 0))
    mspec = pl.BlockSpec((bb, 1, d), lambda bi, i: (bi, 0, 0))
    return pl.pallas_call(
        _out_route_kernel,
        grid=(b // bb, nti),
        in_specs=[pl.BlockSpec((bb, tt, m), lambda bi, i: (bi, i, 0)),
                  pl.BlockSpec((m, d), lambda bi, i: (0, 0)),
                  xspec, mspec,
                  pl.BlockSpec((1, 1, d), lambda bi, i: (0, 0, 0)), mspec, mspec,
                  pl.BlockSpec((N_EXPERTS, d), lambda bi, i: (0, 0)),
                  pl.BlockSpec((N_EXPERTS, 1), lambda bi, i: (0, 0))],
        out_specs=[xspec,
                   pl.BlockSpec((rows, HX_W), lambda bi, i: (bi * nti + i, 0)),
                   pl.BlockSpec((1, 1, rows), lambda bi, i: (bi * nti + i, 0, 0))],
        out_shape=[jax.ShapeDtypeStruct((b, t, d), F32),
                   jax.ShapeDtypeStruct((b * t, HX_W), F32),
                   jax.ShapeDtypeStruct(((b // bb) * nti, 1, rows), I32)],
        compiler_params=_cparams(("parallel", "parallel")),
        name=name,
    )(mixed, wout, h, g1, gain.reshape(1, 1, d), sc, sh, wrt, rb)


DISPATCH_SLOTS = 3


def _dispatch_kernel(pos_ref, hxp_hbm, hxs_hbm, xs_in, xs_out, buf, in_sem, out_sem, *, np_tiles, n_steps, rows):
    del xs_in
    i = pl.program_id(0)

    def load(tile, slot, start):
        def go(src, t):
            cp = pltpu.make_async_copy(src.at[pl.ds(t * rows, rows), :], buf.at[slot], in_sem.at[slot])
            cp.start() if start else cp.wait()

        if not start:
            go(hxp_hbm, 0)
            return

        @pl.when(tile < np_tiles)
        def _():
            go(hxp_hbm, tile)

        @pl.when(tile >= np_tiles)
        def _():
            go(hxs_hbm, tile - np_tiles)

    def scatter(tile, slot, start):
        for r in range(rows):
            p = pos_ref[tile * rows + r] if start else 0
            cp = pltpu.make_async_copy(buf.at[slot, pl.ds(r, 1), :], xs_out.at[pl.ds(p, 1), :], out_sem.at[slot])
            cp.start(priority=r % 2) if start else cp.wait()

    slot = lax.rem(i, DISPATCH_SLOTS)
    nxt_slot = lax.rem(i + 1, DISPATCH_SLOTS)

    @pl.when(i == 0)
    def _():
        load(i, slot, True)

    load(i, slot, False)

    @pl.when(i + 1 < n_steps)
    def _():
        @pl.when(i >= 2)
        def _():
            scatter(i - 2, nxt_slot, False)
        load(i + 1, nxt_slot, True)

    scatter(i, slot, True)

    @pl.when(i == n_steps - 1)
    def _():
        for back in range(min(DISPATCH_SLOTS, n_steps) - 1, -1, -1):
            scatter(i - back, lax.rem(i - back, DISPATCH_SLOTS), False)


def _dispatch_call(pos, hx_p, hx_s, xs_buf, rows):
    np_tiles, ns_tiles = hx_p.shape[0] // rows, hx_s.shape[0] // rows
    n_steps = np_tiles + ns_tiles
    return pl.pallas_call(
        functools.partial(_dispatch_kernel, np_tiles=np_tiles, n_steps=n_steps, rows=rows),
        grid_spec=pltpu.PrefetchScalarGridSpec(
            num_scalar_prefetch=1, grid=(n_steps,),
            in_specs=[pl.BlockSpec(memory_space=pl.ANY), pl.BlockSpec(memory_space=pl.ANY),
                      pl.BlockSpec(memory_space=pl.ANY)],
            out_specs=pl.BlockSpec(memory_space=pl.ANY),
            scratch_shapes=[pltpu.VMEM((DISPATCH_SLOTS, rows, HX_W), F32),
                            pltpu.SemaphoreType.DMA((DISPATCH_SLOTS,)),
                            pltpu.SemaphoreType.DMA((DISPATCH_SLOTS,))]),
        out_shape=jax.ShapeDtypeStruct(xs_buf.shape, F32),
        input_output_aliases={3: 0},
        compiler_params=_cparams(("arbitrary",)),
        name="dispatch",
    )(pos, hx_p, hx_s, xs_buf)


def _moe_kernel(exp_ref, chg_ref, nxt_ref, valid_ref, xs_ref, wg_hbm, wu_hbm, wd_hbm,
                ys_ref, wg_f, wu_f, wd_f, wgu_s, wd_s, sem, *, layer):
    t = pl.program_id(0)

    def fetch(slot, e):
        return (pltpu.make_async_copy(wg_hbm.at[layer, e], wg_f.at[slot], sem.at[slot]),
                pltpu.make_async_copy(wu_hbm.at[layer, e], wu_f.at[slot], sem.at[slot]),
                pltpu.make_async_copy(wd_hbm.at[layer, e], wd_f.at[slot], sem.at[slot]))

    def recast(slot):
        rc = 128
        for i in range(D_MODEL // rc):
            rows = slice(i * rc, (i + 1) * rc)
            wgu_s[rows, slot * D_FF:(slot + 1) * D_FF] = wg_f[slot, rows, :].astype(BF16)
            wgu_s[rows, (2 + slot) * D_FF:(3 + slot) * D_FF] = wu_f[slot, rows, :].astype(BF16)
        for i in range(D_FF // rc):
            wd_s[slot * D_FF + i * rc:slot * D_FF + (i + 1) * rc, :] = wd_f[slot, i * rc:(i + 1) * rc, :].astype(BF16)

    for slot in range(2):
        @pl.when(t == 0)
        def _():
            for cp in fetch(slot, exp_ref[slot, 0]):
                cp.start()

        @pl.when(chg_ref[slot, t] > 0)
        def _():
            for cp in fetch(slot, exp_ref[slot, t]):
                cp.wait()
            recast(slot)

            @pl.when(nxt_ref[slot, t] >= 0)
            def _():
                for cp in fetch(slot, nxt_ref[slot, t]):
                    cp.start()

    @pl.when(valid_ref[t] > 0)
    def _():
        x = xs_ref[:, 0:D_MODEL].astype(BF16)
        w_a = xs_ref[:, D_MODEL:D_MODEL + 1]
        w_b = xs_ref[:, D_MODEL + 1:D_MODEL + 2]
        gu = _dot(x, wgu_s[...])
        lane = lax.broadcasted_iota(I32, (MOE_TM, 2 * D_FF), 1)
        act = _silu(gu[:, :2 * D_FF]) * gu[:, 2 * D_FF:] * jnp.where(lane < D_FF, w_a, w_b)
        ys_ref[...] = _dot(act.astype(BF16), wd_s[...])

    @pl.when(valid_ref[t] == 0)
    def _():
        ys_ref[...] = jnp.zeros_like(ys_ref)


def _moe_call(experts, chg, nxt, valid, xs, wg, wu, wd, layer):
    p = xs.shape[0]
    n_tiles = p // MOE_TM
    return pl.pallas_call(
        functools.partial(_moe_kernel, layer=layer),
        grid_spec=pltpu.PrefetchScalarGridSpec(
            num_scalar_prefetch=4, grid=(n_tiles,),
            in_specs=[pl.BlockSpec((MOE_TM, HX_W), lambda t, ex, ch, nx, va: (t, 0)),
                      pl.BlockSpec(memory_space=pl.ANY), pl.BlockSpec(memory_space=pl.ANY),
                      pl.BlockSpec(memory_space=pl.ANY)],
            out_specs=pl.BlockSpec((MOE_TM, D_MODEL), lambda t, ex, ch, nx, va: (t, 0)),
            scratch_shapes=[pltpu.VMEM((2, D_MODEL, D_FF), F32), pltpu.VMEM((2, D_MODEL, D_FF), F32),
                            pltpu.VMEM((2, D_FF, D_MODEL), F32),
                            pltpu.VMEM((D_MODEL, 4 * D_FF), BF16), pltpu.VMEM((2 * D_FF, D_MODEL), BF16),
                            pltpu.SemaphoreType.DMA((2,))]),
        out_shape=jax.ShapeDtypeStruct((p, D_MODEL), F32),
        compiler_params=_cparams(("arbitrary",)),
        name="moe",
    )(experts, chg, nxt, valid, xs, wg, wu, wd)


_PAIRS = [(0, 1), (0, 2), (0, 3), (1, 3), (1, 2), (3, 2)]
_EA_TABLE = np.array([g * EPG + p[0] for g in range(N_GROUPS) for p in _PAIRS], np.int32)
_EB_TABLE = np.array([g * EPG + p[1] for g in range(N_GROUPS) for p in _PAIRS], np.int32)


def _moe(hx_p, hx_s, bkt, xs_buf, wg, wu, wd, layer):
    n = hx_p.shape[0] + hx_s.shape[0]
    tm = MOE_TM
    ar = jnp.arange(N_BUCKETS, dtype=I32)
    blk = LANES
    nb = -(-n // blk)
    bkt_p = jnp.pad(bkt, (0, nb * blk - n), constant_values=N_BUCKETS)
    oh = (ar[:, None] == bkt_p[None, :]).astype(F32)
    triu = jnp.triu(jnp.ones((blk, blk), F32))
    inner = jnp.einsum('kbj,ji->kbi', oh.reshape(N_BUCKETS, nb, blk), triu)
    tot = inner[:, :, -1]
    outer = jnp.cumsum(tot, axis=1) - tot
    cs = (inner + outer[:, :, None]).reshape(N_BUCKETS, nb * blk)
    counts = (outer[:, -1] + tot[:, -1]).astype(I32)
    padded = ((counts + tm - 1) // tm) * tm
    ends = jnp.cumsum(padded)
    offs = ends - padded
    pos = jnp.sum(oh * (cs - 1.0 + offs.astype(F32)[:, None]), axis=0).astype(I32)[:n]
    n_tiles = -(-n // tm) + N_BUCKETS
    p = n_tiles * tm
    if xs_buf is None:
        xs_buf = jnp.zeros((p, HX_W), F32)
    tstart = jnp.arange(n_tiles, dtype=I32) * tm
    tb = jnp.sum((ends[None, :] <= tstart[:, None]).astype(I32), axis=1)
    n_used = ends[-1] // tm
    valid = (jnp.arange(n_tiles, dtype=I32) < n_used).astype(I32)
    last_b = tb[jnp.maximum(n_used - 1, 0)]
    tb = jnp.clip(jnp.where(valid > 0, tb, last_b), 0, N_BUCKETS - 1)
    ea = jnp.asarray(_EA_TABLE)[tb]
    eb = jnp.asarray(_EB_TABLE)[tb]
    experts = jnp.stack([ea, eb])
    chg = jnp.concatenate([jnp.ones((2, 1), I32), (experts[:, 1:] != experts[:, :-1]).astype(I32)], axis=1)
    tidx = jnp.arange(n_tiles, dtype=I32)
    chg_at = jnp.where(chg > 0, tidx[None, :], n_tiles)
    nxt_at = jnp.concatenate([lax.cummin(chg_at, axis=1, reverse=True)[:, 1:], jnp.full((2, 1), n_tiles, I32)], axis=1)
    nxt = jnp.where(nxt_at < n_tiles,
                    jnp.take_along_axis(experts, jnp.minimum(nxt_at, n_tiles - 1), axis=1), -1)
    xs = _dispatch_call(pos, hx_p, hx_s, xs_buf, DISPATCH_ROWS)
    ys = _moe_call(experts, chg, nxt, valid, xs, wg, wu, wd, layer)
    return ys.at[pos].get(mode="promise_in_bounds", unique_indices=True), xs


def _final_kernel(h_ref, y_ref, g_ref, gain_ref, o_ref):
    bb, tt, d = h_ref.shape
    h = h_ref[...] + g_ref[...] * y_ref[...].reshape(bb, tt, d)
    o_ref[...] = _rms(h) * gain_ref[...]


def _final_call(h, y2d, off, g2, gain, *, bb, tt, name):
    b, t, d = h.shape
    nti = t // tt
    rows = bb * tt
    xspec = pl.BlockSpec((bb, tt, d), lambda bi, i: (bi, i, 0))
    return pl.pallas_call(
        _final_kernel,
        grid=(b // bb, nti),
        in_specs=[xspec, pl.BlockSpec((rows, d), lambda bi, i: (off + bi * nti + i, 0)),
                  pl.BlockSpec((bb, 1, d), lambda bi, i: (bi, 0, 0)),
                  pl.BlockSpec((1, 1, d), lambda bi, i: (0, 0, 0))],
        out_specs=xspec,
        out_shape=jax.ShapeDtypeStruct((b, t, d), F32),
        compiler_params=_cparams(("parallel", "parallel")),
        name=name,
    )(h, y2d, g2, gain.reshape(1, 1, d))


def _group_cfg(b, t):
    if t >= 512:
        return dict(bb=1, tt=512, tt_out=1024, L=CHUNK, CB=4, Lc=4 * CHUNK, CBc=1)
    return dict(bb=b, tt=t, tt_out=t, L=min(CHUNK, t), CB=1, Lc=min(CHUNK, t), CBc=1)


def _rope_tables(pos0, t):
    half = DK_C // 2
    inv = np.power(np.float64(ROPE_BASE), -np.linspace(0.0, 1.0, half, dtype=np.float64))
    ang = (np.float64(pos0) + np.arange(t, dtype=np.float64))[:, None] * inv[None, :]
    return jnp.asarray(np.cos(ang), F32), jnp.asarray(np.sin(ang), F32)


def kernel(x_prompt, x_sample, c_prompt, c_sample, state_a_conv, state_a_rec, state_b_c, state_b_n, state_b_m,
           state_c_rec, w_mod, b_mod, norm_mix, norm_ffn, w_in_ab, conv_a, a_log, dt_bias, norm_a, gate_bias_b,
           norm_b, w_out_ab, w_in_c, w_out_c, w_router, router_bias, w_gate, w_up, w_down, norm_final):
    d = D_MODEL
    bp, tp, _ = x_prompt.shape
    bs, ts, _ = x_sample.shape
    n_ab, n_c = w_in_ab.shape[0], w_in_c.shape[0]

    mod_all = _mod_call(jnp.concatenate([c_prompt, c_sample], axis=0), w_mod, b_mod)

    o = np.cumsum((0,) + AB_SIZES)
    w_ab_main = jnp.concatenate([w_in_ab[:, :, o[0]:o[1]].astype(BF16), w_in_ab[:, :, o[3]:o[7]].astype(BF16),
                                 w_in_ab[:, :, o[8]:o[9]].astype(BF16)], axis=-1)
    w_ab_gate = jnp.concatenate([w_in_ab[:, :, o[1]:o[3]], w_in_ab[:, :, o[7]:o[8]]], axis=-1)
    w_ab_gate_pad = jnp.pad(w_ab_gate, ((0, 0), (0, 0), (0, LANES - N_GATE_COLS)))
    w_c = w_in_c.astype(BF16)
    w_out_ab_b = w_out_ab.astype(BF16)
    w_out_c_b = w_out_c.astype(BF16)
    wrt = w_router.T
    rb = router_bias.reshape(N_EXPERTS, 1)
    zeros4 = jnp.zeros((n_ab, H_A), F32)
    gpar_row0 = jnp.concatenate([a_log, jnp.zeros((n_ab, LANES - H_A), F32)], axis=1)
    gpar_row1 = jnp.concatenate([dt_bias, zeros4, gate_bias_b, jnp.zeros((n_ab, LANES - 16), F32)], axis=1)
    gpar = jnp.concatenate([gpar_row0[:, None], gpar_row1[:, None], jnp.zeros((n_ab, 6, LANES), F32)], axis=1)
    gpart = jnp.swapaxes(jnp.concatenate([gpar[:, :2, :N_GATE_COLS], jnp.zeros((n_ab, LANES - 2, N_GATE_COLS), F32)],
                                         axis=1), 1, 2)

    groups = []
    zf = lambda *s: jnp.zeros(s, F32)
    groups.append(dict(
        x=x_prompt, b=bp, t=tp, pos0=0.0, mod=mod_all[:, :bp], row0=0,
        conv=zf(n_ab, bp, CONV_W - 1, QKV_A), sa=zf(n_ab, bp, H_A, DK_A, DV_A), cb=zf(n_ab, bp, H_B, DQK_B, DV_B),
        nb=zf(n_ab, bp, H_B, DQK_B), mb=zf(n_ab, bp, H_B), sc=zf(n_c, bp, H_C, DK_C, DV_C)))
    groups.append(dict(
        x=x_sample, b=bs, t=ts, pos0=float(PAST_LEN), mod=mod_all[:, bp:], row0=bp * tp,
        conv=state_a_conv, sa=state_a_rec, cb=state_b_c, nb=state_b_n, mb=state_b_m, sc=state_c_rec))
    n_tot = bp * tp + bs * ts
    for g in groups:
        g.update(_group_cfg(g["b"], g["t"]))
        g["h"] = g["x"]
        g["prev"] = None
        g["new_ab"], g["new_c"] = [], []
        g["cos"], g["sin"] = _rope_tables(g["pos0"], g["t"])
        g["roff"] = g["row0"] // (g["bb"] * g["tt"])

    xs_buf = None
    for layer in range(DEPTH):
        li = layer // 2
        bkts, hxs = [], []
        for gi, g in enumerate(groups):
            b, t, bb, tt = g["b"], g["t"], g["bb"], g["tt"]
            mods = [m.reshape(b, 1, d) for m in jnp.split(g["mod"][layer], N_MOD, axis=-1)]
            sh1, sc1, g1, sh2, sc2, g2 = mods
            tag = f"l{layer}g{gi}"
            if layer % 2 == 0:
                outs = _ln_mm_call(g["h"], g["prev"], norm_mix[layer], sc1, sh1, w_ab_main[li],
                                   (w_ab_gate_pad[li], g["conv"][li], conv_a[li]), bb=bb, tt=tt, tn=512,
                                   out_dtype=F32, name="ln_ab_" + tag)
                if g["prev"] is not None:
                    g["h"], outs = outs[0], outs[1:]
                proj, gates, gates_t, qkv, conv_n = outs
                gates_t = gates_t.reshape(N_GATE_COLS, b, t).transpose(1, 0, 2)
                mixed, sa_n, cb_n, nb_n, mb_n = _mixer_ab_call(
                    qkv, proj, gates, gates_t, g["sa"][li], g["cb"][li],
                    g["nb"][li].reshape(b, H_B, 1, DQK_B), g["mb"][li].reshape(b, H_B, 1, 1),
                    gpar[li], gpart[li], norm_a[li].reshape(1, DV_A), norm_b[li].reshape(1, DV_B),
                    L=g["L"], CB=g["CB"], name="mixer_ab_" + tag)
                g["new_ab"].append((conv_n, sa_n, cb_n, nb_n.reshape(b, H_B, DQK_B), mb_n.reshape(b, H_B)))
                wout = w_out_ab_b[li]
            else:
                outs = _ln_mm_call(g["h"], g["prev"], norm_mix[layer], sc1, sh1, w_c[li], None,
                                   bb=bb, tt=tt, tn=512, out_dtype=BF16, name="ln_c_" + tag)
                if g["prev"] is not None:
                    g["h"], outs = outs[0], outs[1:]
                (proj,) = outs
                mixed, sc_n = _mixer_c_call(proj, g["cos"], g["sin"], g["sc"][li], L=g["Lc"], CB=g["CBc"],
                                            name="mixer_c_" + tag)
                g["new_c"].append(sc_n)
                wout = w_out_c_b[li]
            tto = g["tt_out"]
            g["h"], hx, bkt = _out_route_call(mixed, wout, g["h"], g1, norm_ffn[layer], sc2, sh2, wrt, rb,
                                              bb=bb, tt=tto, name="out_route_" + tag)
            bkts.append(bkt.reshape(-1))
            hxs.append(hx)
            g["g2"] = g2
        y_moe, xs_buf = _moe(hxs[0], hxs[1], jnp.concatenate(bkts), xs_buf, w_gate, w_up, w_down, layer)
        for g in groups:
            g["prev"] = (y_moe, g["roff"], g["g2"])

    outs = []
    for gi, g in enumerate(groups):
        y2d, off, g2 = g["prev"]
        y = _final_call(g["h"], y2d, off, g2, norm_final, bb=g["bb"], tt=g["tt"], name=f"final_g{gi}")
        ab = [jnp.stack(s) for s in zip(*g["new_ab"])]
        outs.append((y, ab[0], ab[1], ab[2], ab[3], ab[4], jnp.stack(g["new_c"])))
    p, s = outs
    return (p[0], s[0], p[1], p[2], p[3], p[4], p[5], p[6], s[1], s[2], s[3], s[4], s[5], s[6])
```

```python
import functools

import numpy as np
import jax
import jax.numpy as jnp
from jax import lax
from jax.experimental import pallas as pl
from jax.experimental.pallas import tpu as pltpu

F32 = jnp.float32
BF16 = jnp.bfloat16
I32 = jnp.int32

D_MODEL = 1024
DEPTH = 2
CHUNK = 64
H_A, DK_A, DV_A, CONV_W = 4, 128, 128, 4
QKV_A = H_A * (2 * DK_A + DV_A)
H_B, DQK_B, DV_B = 4, 64, 128
H_C, DK_C, DV_C = 4, 256, 512
ROPE_BASE = 10000.0
PAST_LEN = 4096
AB_SIZES = (QKV_A, H_A, H_A, H_A * DV_A, H_B * DQK_B, H_B * DQK_B, H_B * DV_B, 2 * H_B, H_B * DV_B)
MIX_AB = H_A * DV_A + H_B * DV_B
MIX_C = H_C * DV_C
IN_C = 2 * H_C * DK_C + 2 * H_C * DV_C
N_EXPERTS, N_GROUPS, EPG, D_FF = 16, 4, 4, 512
N_MOD = 6
EPS = 1e-6

LANES = 128
AB_MAIN = 3584
N_GATE_COLS = 16
HX_W = D_MODEL + LANES
N_BUCKETS = N_GROUPS * 6
MOE_TM = 256
DISPATCH_ROWS = 256
VMEM_LIMIT = 48 * 1024 * 1024

NN = ((1,), (0,))
NT = ((1,), (1,))
TN = ((0,), (0,))


def _dot(a, b, dims=NN):
    return lax.dot_general(a, b, (dims, ((), ())), preferred_element_type=F32)


def _mmb(a, b, dims=NN):
    return _dot(a.astype(BF16), b.astype(BF16), dims)


def _split2(x):
    hi = x.astype(BF16)
    lo = (x - hi.astype(F32)).astype(BF16)
    return hi, lo


def _split3(x):
    hi = x.astype(BF16)
    r = x - hi.astype(F32)
    mid = r.astype(BF16)
    lo = (r - mid.astype(F32)).astype(BF16)
    return hi, mid, lo


def _mm3(a, b, dims=NN):
    ah, al = _split2(a)
    bh, bl = _split2(b)
    return _dot(ah, bh, dims) + (_dot(ah, bl, dims) + _dot(al, bh, dims))


def _mm_mask_l(mask_bf16, x, dims=NN):
    h, m, l = _split3(x)
    return _dot(mask_bf16, h, dims) + (_dot(mask_bf16, m, dims) + _dot(mask_bf16, l, dims))


def _mm_mask_r(x, mask_bf16):
    h, m, l = _split3(x)
    return _dot(h, mask_bf16) + (_dot(m, mask_bf16) + _dot(l, mask_bf16))


def _sigmoid(x):
    return 1.0 / (1.0 + jnp.exp(-x))


def _silu(x):
    return x * _sigmoid(x)


def _softplus(x):
    return jnp.maximum(x, 0.0) + jnp.log(1.0 + jnp.exp(-jnp.abs(x)))


def _rms(x, eps=EPS):
    return x * lax.rsqrt(jnp.mean(x * x, axis=-1, keepdims=True) + eps)


def _cparams(sem):
    return pltpu.CompilerParams(dimension_semantics=sem, vmem_limit_bytes=VMEM_LIMIT)


def _cast_kernel(w_ref, o_ref):
    o_ref[...] = w_ref[...].astype(o_ref.dtype)


def _cast_call(w, name):
    n, r, c = w.shape
    rb = 128
    spec = pl.BlockSpec((1, rb, c), lambda i, j: (i, j, 0))
    return pl.pallas_call(
        _cast_kernel, grid=(n, r // rb), in_specs=[spec], out_specs=spec,
        out_shape=jax.ShapeDtypeStruct(w.shape, BF16),
        compiler_params=_cparams(("parallel", "parallel")), name=name,
    )(w)


def _mod_kernel(c_ref, w_ref, b_ref, o_ref):
    c = c_ref[...]
    o_ref[0] = _mm3(_silu(c), w_ref[0]) + b_ref[0]


def _mod_call(c_all, w_mod, b_mod):
    bt = c_all.shape[0]
    e = w_mod.shape[-1]
    tn = 1024
    return pl.pallas_call(
        _mod_kernel,
        grid=(DEPTH, e // tn),
        in_specs=[pl.BlockSpec((bt, D_MODEL), lambda l, j: (0, 0)),
                  pl.BlockSpec((1, D_MODEL, tn), lambda l, j: (l, 0, j)),
                  pl.BlockSpec((1, 1, tn), lambda l, j: (l, 0, j))],
        out_specs=pl.BlockSpec((1, bt, tn), lambda l, j: (l, 0, j)),
        out_shape=jax.ShapeDtypeStruct((DEPTH, bt, e), F32),
        compiler_params=_cparams(("parallel", "parallel")),
        name="mod",
    )(c_all, w_mod, b_mod.reshape(DEPTH, 1, e))


def _ln_mm_kernel(*refs, has_prev, has_gates, tn):
    it = iter(refs)
    x_ref = next(it)
    if has_prev:
        yp_ref, gp_ref = next(it), next(it)
    gain_ref, sc_ref, sh_ref, w_ref = next(it), next(it), next(it), next(it)
    if has_gates:
        wg_ref, conv0_ref, convw_ref = next(it), next(it), next(it)
    if has_prev:
        h_ref = next(it)
    proj_ref = next(it)
    if has_gates:
        g_ref, gt_ref, qkv_ref, conv_o = next(it), next(it), next(it), next(it)
    hn_s = next(it)
    if has_gates:
        tail_s = next(it)

        @pl.when(pl.program_id(1) == 0)
        def _():
            tail_s[...] = conv0_ref[...]

    bb, tt, d = x_ref.shape
    rows = bb * tt
    x = x_ref[...]
    if has_prev:
        x = x + gp_ref[...] * yp_ref[...].reshape(bb, tt, d)
        h_ref[...] = x
    hn = _rms(x) * gain_ref[...]
    hn = hn * (1.0 + sc_ref[...]) + sh_ref[...]
    hn2 = hn.reshape(rows, d)
    hn_s[...] = hn2.astype(BF16)
    if has_gates:
        gates = _mm3(hn2, wg_ref[...])
        g_ref[...] = gates.reshape(bb, tt, LANES)
        gt_ref[...] = gates.T[0:N_GATE_COLS, :]
    n_conv = QKV_A // tn if has_gates else 0
    n_tiles = w_ref.shape[-1] // tn

    def project(j):
        return _dot(hn_s[...], w_ref[:, j * tn:(j + 1) * tn]).reshape(bb, tt, tn)

    def finish(j, y):
        cols = slice(j * tn, (j + 1) * tn)
        if j < n_conv:
            xp = jnp.concatenate([tail_s[:, :, cols], y], axis=1)
            acc = None
            for k in range(CONV_W):
                term = convw_ref[k:k + 1, cols] * xp[:, k:k + tt, :]
                acc = term if acc is None else acc + term
            qkv_ref[:, :, cols] = _silu(acc).astype(qkv_ref.dtype)
            tail = y[:, tt - (CONV_W - 1):tt, :]
            tail_s[:, :, cols] = tail
            conv_o[:, :, cols] = tail
        else:
            pcols = slice((j - n_conv) * tn, (j - n_conv + 1) * tn)
            proj_ref[:, :, pcols] = y.astype(proj_ref.dtype)

    y = project(0)
    for j in range(n_tiles):
        y_next = project(j + 1) if j + 1 < n_tiles else None
        finish(j, y)
        y = y_next


def _ln_mm_call(x, prev, gain, sc, sh, w, gates_w, *, bb, tt, tn, out_dtype, name):
    b, t, d = x.shape
    e = w.shape[1]
    rows = bb * tt
    nti = t // tt
    grid = (b // bb, nti)
    has_prev = prev is not None
    has_gates = gates_w is not None
    xspec = pl.BlockSpec((bb, tt, d), lambda bi, i: (bi, i, 0))
    mspec = pl.BlockSpec((bb, 1, d), lambda bi, i: (bi, 0, 0))
    in_specs, args = [xspec], [x]
    if has_prev:
        y2d, off, gp = prev
        in_specs += [pl.BlockSpec((rows, d), lambda bi, i: (off + bi * nti + i, 0)), mspec]
        args += [y2d, gp]
    in_specs += [pl.BlockSpec((1, 1, d), lambda bi, i: (0, 0, 0)), mspec, mspec,
                 pl.BlockSpec((d, e), lambda bi, i: (0, 0), pipeline_mode=pl.Buffered(1))]
    args += [gain.reshape(1, 1, d), sc, sh, w]
    tail_spec = pl.BlockSpec((bb, CONV_W - 1, QKV_A), lambda bi, i: (bi, 0, 0))
    if has_gates:
        wg, conv0, convw = gates_w
        in_specs += [pl.BlockSpec((d, LANES), lambda bi, i: (0, 0)), tail_spec,
                     pl.BlockSpec((CONV_W, QKV_A), lambda bi, i: (0, 0))]
        args += [wg, conv0, convw]
    e_proj = e - QKV_A if has_gates else e
    out_specs, out_shape = [], []
    if has_prev:
        out_specs.append(xspec)
        out_shape.append(jax.ShapeDtypeStruct((b, t, d), F32))
    out_specs.append(pl.BlockSpec((bb, tt, e_proj), lambda bi, i: (bi, i, 0)))
    out_shape.append(jax.ShapeDtypeStruct((b, t, e_proj), out_dtype))
    scratch = [pltpu.VMEM((rows, d), BF16)]
    if has_gates:
        out_specs += [pl.BlockSpec((bb, tt, LANES), lambda bi, i: (bi, i, 0)),
                      pl.BlockSpec((N_GATE_COLS, rows), lambda bi, i: (0, bi * nti + i)),
                      pl.BlockSpec((bb, tt, QKV_A), lambda bi, i: (bi, i, 0)), tail_spec]
        out_shape += [jax.ShapeDtypeStruct((b, t, LANES), F32),
                      jax.ShapeDtypeStruct((N_GATE_COLS, b * t), F32),
                      jax.ShapeDtypeStruct((b, t, QKV_A), BF16),
                      jax.ShapeDtypeStruct((b, CONV_W - 1, QKV_A), F32)]
        scratch.append(pltpu.VMEM((bb, CONV_W - 1, QKV_A), F32))
    return pl.pallas_call(
        functools.partial(_ln_mm_kernel, has_prev=has_prev, has_gates=has_gates, tn=tn),
        grid=grid, in_specs=in_specs, out_specs=out_specs, out_shape=out_shape,
        scratch_shapes=scratch,
        compiler_params=_cparams(("parallel", "arbitrary" if has_gates else "parallel")),
        name=name,
    )(*args)


def _tri_inv_all(a_list, eye, length, mm):
    xs = [-a for a in a_list]
    ps = [eye + x for x in xs]
    n = 2
    while n < length:
        xs = [mm(x, x) for x in xs]
        ps = [p + mm(p, x) for p, x in zip(ps, xs)]
        n *= 2
    return ps


def _mixer_ab_kernel(qkv_ref, z_ref, qkb_ref, vb_ref, ob_ref, g_ref, gt_ref,
                     sa0_ref, cb0_ref, nb0_ref, mb0_ref,
                     gpar_ref, gpart_ref, na_ref, nbn_ref,
                     mixed_ref, sa_o, cb_o, nb_o, mb_o, *, L, CB):
    @pl.when(pl.program_id(1) == 0)
    def _():
        sa_o[...] = sa0_ref[...]
        cb_o[...] = cb0_ref[...]
        nb_o[...] = nb0_ref[...]
        mb_o[...] = mb0_ref[...]

    ii = lax.broadcasted_iota(I32, (L, L), 0)
    jj = lax.broadcasted_iota(I32, (L, L), 1)
    tri = ii >= jj
    strict = ii > jj
    eye = (ii == jj).astype(F32)
    tril_b = tri.astype(BF16)
    triu_b = (ii <= jj).astype(BF16)
    rowi = lax.broadcasted_iota(I32, (N_GATE_COLS, L), 0)
    gpar = gpar_ref[...]
    gpart = gpart_ref[...]
    neat = -jnp.exp(gpart[:, 0:1])
    norm_a = na_ref[...]
    norm_b = nbn_ref[...]

    mm = _mmb

    def qkv_tile(r0, col0):
        return qkv_ref[0, r0:r0 + L, col0:col0 + DK_A].astype(F32)

    chunks = range(CB)
    items = [(c, h) for c in chunks for h in range(H_A)]
    rs = [slice(c * L, (c + 1) * L) for c in chunks]

    pre = [g_ref[0, r, :] + gpar[1:2, :] for r in rs]
    pret = [gt_ref[0, :, r] + gpart[:, 1:2] for r in rs]
    csrct = [jnp.where(rowi < 4, neat * _softplus(p), jnp.where(rowi >= 12, -_softplus(-p), 0.0)) for p in pret]
    gcumt = [_mm_mask_r(x, triu_b) for x in csrct]
    gcum = [_mm_mask_l(tril_b, x, NT) for x in csrct]
    beta_all = [_sigmoid(p) for p in pre]

    q = [qkv_tile(c * L, h * DK_A) for c, h in items]
    k = [qkv_tile(c * L, H_A * DK_A + h * DK_A) for c, h in items]
    v = [qkv_tile(c * L, 2 * H_A * DK_A + h * DV_A) for c, h in items]
    q = [x * lax.rsqrt(jnp.sum(x * x, axis=-1, keepdims=True) + EPS) * (DK_A ** -0.5) for x in q]
    k = [x * lax.rsqrt(jnp.sum(x * x, axis=-1, keepdims=True) + EPS) for x in k]
    gc_c = [gcum[c][:, h:h + 1] for c, h in items]
    gc_r = [gcumt[c][h:h + 1, :] for c, h in items]
    decay = [jnp.where(tri, jnp.exp(jnp.where(tri, a - b, 0.0)), 0.0) for a, b in zip(gc_c, gc_r)]
    beta_c = [beta_all[c][:, 4 + h:5 + h] for c, h in items]
    kb = [x * b for x, b in zip(k, beta_c)]
    a_list = [jnp.where(strict, mm(x, y, NT) * d, 0.0) for x, y, d in zip(kb, k, decay)]
    attn = [jnp.where(tri, mm(x, y, NT) * d, 0.0) for x, y, d in zip(q, k, decay)]
    eg = [jnp.exp(x) for x in gc_c]
    gl = [x[L - 1:L, :] for x in gc_c]
    rhs = [jnp.concatenate([x * e, y * b], axis=-1) for x, e, y, b in zip(kb, eg, v, beta_c)]
    qeg = [x * e for x, e in zip(q, eg)]
    kdec = [x * jnp.exp(g - gc) for x, g, gc in zip(k, gl, gc_c)]
    sdec = [jnp.exp(g) for g in gl]
    a_list = [a.astype(BF16) for a in a_list]
    tinv = _tri_inv_all([a.astype(F32) for a in a_list], eye, L, mm)
    sol = [mm(t, r) for t, r in zip(tinv, rhs)]
    sol_hl = [_split2(s) for s in sol]
    resid = [r - (s + (_dot(a, hi) + _dot(a, lo))) for r, s, a, (hi, lo) in zip(rhs, sol, a_list, sol_hl)]
    sol = [s + mm(t, r) for s, t, r in zip(sol, tinv, resid)]

    heads = range(H_B)
    qb = [qkb_ref[0, rs[c], h * DQK_B:(h + 1) * DQK_B] * (DQK_B ** -0.5) for c, h in items]
    kbb = [qkb_ref[0, rs[c], H_B * DQK_B + h * DQK_B:H_B * DQK_B + (h + 1) * DQK_B] for c, h in items]
    b_c = [gcum[c][:, 12 + h:13 + h] for c, h in items]
    dm = [jnp.where(tri, gcum[c][:, 12 + h:13 + h] - gcumt[c][12 + h:13 + h, :] + pret[c][8 + h:9 + h, :], -jnp.inf)
          for c, h in items]
    dmax = [jnp.max(x, axis=-1, keepdims=True) for x in dm]
    qkm = [mm(x, y, NT) for x, y in zip(qb, kbb)]
    li_c = [pre[c][:, 8 + h:9 + h] for c, h in items]

    for c in chunks:
        ia = [c * H_A + h for h in heads]
        s = [sa_o[0, h] for h in heads]
        v_new = [sol[i][:, DK_A:] - mm(sol[i][:, :DK_A], s[h]) for h, i in enumerate(ia)]
        m_prev = [mb_o[0, h] for h in heads]
        a0 = [b_c[i] + m_prev[h] for h, i in enumerate(ia)]
        m_t = [jnp.maximum(a0[h], dmax[i]) for h, i in enumerate(ia)]
        w0 = [jnp.exp(a0[h] - m_t[h]) for h in heads]
        sm = [qkm[i] * jnp.exp(dm[i] - m_t[h]) for h, i in enumerate(ia)]
        cst = [cb_o[0, h] for h in heads]
        nst = [nb_o[0, h] for h in heads]
        vb = [vb_ref[0, rs[c], h * DV_B:(h + 1) * DV_B] for h in heads]
        o_a = [mm(qeg[i], s[h]) + mm(attn[i], v_new[h]) for h, i in enumerate(ia)]
        for h, i in enumerate(ia):
            sa_o[0, h] = s[h] * sdec[i] + mm(kdec[i], v_new[h], TN)
        num = [w0[h] * mm(qb[i], cst[h]) + mm(sm[h], vb[h]) for h, i in enumerate(ia)]
        den = [w0[h] * jnp.sum(qb[i] * nst[h], axis=-1, keepdims=True) + jnp.sum(sm[h], axis=-1, keepdims=True)
               for h, i in enumerate(ia)]
        hh = [num[h] / jnp.maximum(jnp.abs(den[h]), jnp.exp(-m_t[h])) for h in heads]
        m_new = [m_t[h][L - 1:L, :] for h in heads]
        kw = [kbb[i] * jnp.exp(b_c[i][L - 1:L, :] - b_c[i] + li_c[i] - m_new[h]) for h, i in enumerate(ia)]
        for h in heads:
            w0l = w0[h][L - 1:L, :]
            cb_o[0, h] = w0l * cst[h] + mm(kw[h], vb[h], TN)
            nb_o[0, h] = w0l * nst[h] + jnp.sum(kw[h], axis=0, keepdims=True)
            mb_o[0, h] = m_new[h]
        for h in heads:
            zg = z_ref[0, rs[c], h * DV_A:(h + 1) * DV_A]
            mixed_ref[0, rs[c], h * DV_A:(h + 1) * DV_A] = (_rms(o_a[h]) * norm_a * _silu(zg)).astype(mixed_ref.dtype)
            og = ob_ref[0, rs[c], h * DV_B:(h + 1) * DV_B]
            c0 = H_A * DV_A + h * DV_B
            mixed_ref[0, rs[c], c0:c0 + DV_B] = (_rms(hh[h]) * norm_b * _sigmoid(og)).astype(mixed_ref.dtype)


def _mixer_ab_call(qkv, proj, gates, gates_t, sa0, cb0, nb0, mb0, gpar, gpart, norm_a, norm_b, *, L, CB, name):
    b, t, _ = proj.shape
    tb = L * CB
    nt = t // tb

    def col(width, idx):
        return pl.BlockSpec((1, tb, width), lambda bi, i: (bi, i, idx))

    def const(shape):
        return pl.BlockSpec(shape, lambda bi, i: (0,) * len(shape))

    def state(shape):
        return pl.BlockSpec((1,) + shape, lambda bi, i: (bi,) + (0,) * len(shape))

    in_specs = [col(QKV_A, 0), col(512, 0), col(512, 1), col(512, 2), col(512, 3),
                col(LANES, 0), pl.BlockSpec((1, N_GATE_COLS, tb), lambda bi, i: (bi, 0, i)),
                state((H_A, DK_A, DV_A)), state((H_B, DQK_B, DV_B)),
                state((H_B, 1, DQK_B)), state((H_B, 1, 1)),
                const((8, LANES)), const((N_GATE_COLS, LANES)),
                const((1, DV_A)), const((1, DV_B))]
    out_specs = [pl.BlockSpec((1, tb, MIX_AB), lambda bi, i: (bi, i, 0)),
                 state((H_A, DK_A, DV_A)), state((H_B, DQK_B, DV_B)),
                 state((H_B, 1, DQK_B)), state((H_B, 1, 1))]
    out_shape = [jax.ShapeDtypeStruct((b, t, MIX_AB), BF16),
                 jax.ShapeDtypeStruct((b, H_A, DK_A, DV_A), F32),
                 jax.ShapeDtypeStruct((b, H_B, DQK_B, DV_B), F32),
                 jax.ShapeDtypeStruct((b, H_B, 1, DQK_B), F32),
                 jax.ShapeDtypeStruct((b, H_B, 1, 1), F32)]
    return pl.pallas_call(
        functools.partial(_mixer_ab_kernel, L=L, CB=CB),
        grid=(b, nt), in_specs=in_specs, out_specs=out_specs, out_shape=out_shape,
        compiler_params=_cparams(("parallel", "arbitrary")),
        name=name,
    )(qkv, proj, proj, proj, proj, gates, gates_t, sa0, cb0, nb0, mb0, gpar, gpart, norm_a, norm_b)


def _mixer_c_kernel(q_ref, k_ref, v_ref, g_ref, cos_ref, sin_ref, s0_ref, mixed_ref, s_o, intra_s, *, L, CB):
    log_gamma = [float(np.log1p(-np.exp2(-5.0 - h))) for h in range(H_C)]

    @pl.when(pl.program_id(1) == 0)
    def _():
        s_o[...] = s0_ref[...]
        ii = lax.broadcasted_iota(I32, (L, L), 0)
        jj = lax.broadcasted_iota(I32, (L, L), 1)
        rel = (ii - jj).astype(F32)
        for h in range(H_C):
            intra_s[h] = jnp.where(rel >= 0, jnp.exp(log_gamma[h] * jnp.maximum(rel, 0.0)), 0.0)

    idx = lax.broadcasted_iota(I32, (L, 1), 0).astype(F32)
    half = DK_C // 2

    def rope(x, cos, sin):
        x1, x2 = x[:, :half], x[:, half:]
        return jnp.concatenate([x1 * cos - x2 * sin, x1 * sin + x2 * cos], axis=-1)

    for c in range(CB):
        r0 = c * L
        cos = cos_ref[r0:r0 + L, :]
        sin = sin_ref[r0:r0 + L, :]
        for h in range(H_C):
            lg = log_gamma[h]
            intra = intra_s[h]
            q_decay = jnp.exp(lg * (idx + 1.0))
            k_decay = jnp.exp(lg * (L - 1.0 - idx))
            s_decay = float(np.exp(lg * L))
            q = rope(q_ref[0, r0:r0 + L, h * DK_C:(h + 1) * DK_C].astype(F32), cos, sin)
            k = rope(k_ref[0, r0:r0 + L, h * DK_C:(h + 1) * DK_C].astype(F32), cos, sin) * (DK_C ** -0.5)
            v = v_ref[0, r0:r0 + L, h * DV_C:(h + 1) * DV_C]
            s = s_o[0, h]
            o = _mmb(_mmb(q, k, NT) * intra, v) + _mmb(q * q_decay, s)
            s_o[0, h] = s_decay * s + _mmb(k * k_decay, v, TN)
            gg = g_ref[0, r0:r0 + L, h * DV_C:(h + 1) * DV_C].astype(F32)
            o = _rms(o) * _silu(gg)
            mixed_ref[0, r0:r0 + L, h * DV_C:(h + 1) * DV_C] = o.astype(mixed_ref.dtype)


def _mixer_c_call(proj, cos, sin, s0, *, L, CB, name):
    b, t, _ = proj.shape
    tb = L * CB
    nt = t // tb
    qk_w, v_w = H_C * DK_C, H_C * DV_C
    sspec = pl.BlockSpec((1, H_C, DK_C, DV_C), lambda bi, i: (bi, 0, 0, 0))
    tspec = pl.BlockSpec((tb, DK_C // 2), lambda bi, i: (i, 0))
    return pl.pallas_call(
        functools.partial(_mixer_c_kernel, L=L, CB=CB),
        grid=(b, nt),
        in_specs=[pl.BlockSpec((1, tb, qk_w), lambda bi, i: (bi, i, 0)),
                  pl.BlockSpec((1, tb, qk_w), lambda bi, i: (bi, i, 1)),
                  pl.BlockSpec((1, tb, v_w), lambda bi, i: (bi, i, 1)),
                  pl.BlockSpec((1, tb, v_w), lambda bi, i: (bi, i, 2)),
                  tspec, tspec, sspec],
        out_specs=[pl.BlockSpec((1, tb, MIX_C), lambda bi, i: (bi, i, 0)), sspec],
        out_shape=[jax.ShapeDtypeStruct((b, t, MIX_C), BF16),
                   jax.ShapeDtypeStruct((b, H_C, DK_C, DV_C), F32)],
        scratch_shapes=[pltpu.VMEM((H_C, L, L), F32)],
        compiler_params=_cparams(("parallel", "arbitrary")),
        name=name,
    )(proj, proj, proj, proj, cos, sin, s0)


def _out_route_kernel(mixed_ref, wout_ref, h_ref, g1_ref, gain_ref, sc_ref, sh_ref, wrt_ref, rb_ref,
                      hnew_ref, hx_ref, bkt_ref):
    bb, tt, m = mixed_ref.shape
    d = h_ref.shape[-1]
    rows = bb * tt
    y = _dot(mixed_ref[...].reshape(rows, m), wout_ref[...])
    h = h_ref[...] + g1_ref[...] * y.reshape(bb, tt, d)
    hnew_ref[...] = h
    hn = _rms(h) * gain_ref[...]
    hn = (hn * (1.0 + sc_ref[...]) + sh_ref[...]).reshape(rows, d)
    hx_ref[:, 0:d] = hn

    logits = _mm3(wrt_ref[...], hn, NT)
    score = _sigmoid(logits)
    sel = score + rb_ref[...]

    def row(a, r):
        return a[r:r + 1, :]

    gscore = []
    for g in range(N_GROUPS):
        a, b, c, e = (row(sel, EPG * g + i) for i in range(EPG))
        hi_ab, lo_ab, hi_ce, lo_ce = jnp.maximum(a, b), jnp.minimum(a, b), jnp.maximum(c, e), jnp.minimum(c, e)
        top1 = jnp.maximum(hi_ab, hi_ce)
        top2 = jnp.maximum(jnp.maximum(lo_ab, lo_ce), jnp.minimum(hi_ab, hi_ce))
        gscore.append(top1 + top2)
    best = jnp.zeros((1, rows), I32)
    bestv = gscore[0]
    for g in range(1, N_GROUPS):
        upd = gscore[g] > bestv
        best = jnp.where(upd, g, best)
        bestv = jnp.where(upd, gscore[g], bestv)

    def pick(a, i):
        out = row(a, i)
        for g in range(1, N_GROUPS):
            out = jnp.where(best == g, row(a, EPG * g + i), out)
        return out

    vsel = [pick(sel, i) for i in range(EPG)]
    vsc = [pick(score, i) for i in range(EPG)]
    i1 = jnp.zeros((1, rows), I32)
    v1 = vsel[0]
    for i in range(1, EPG):
        upd = vsel[i] > v1
        i1 = jnp.where(upd, i, i1)
        v1 = jnp.where(upd, vsel[i], v1)
    i2 = jnp.full((1, rows), -1, I32)
    v2 = jnp.full((1, rows), -jnp.inf, F32)
    for i in range(EPG):
        cand = jnp.where(i1 == i, -jnp.inf, vsel[i])
        upd = cand > v2
        i2 = jnp.where(upd, i, i2)
        v2 = jnp.where(upd, cand, v2)
    s1 = jnp.zeros((1, rows), F32)
    s2 = jnp.zeros((1, rows), F32)
    for i in range(EPG):
        s1 = jnp.where(i1 == i, vsc[i], s1)
        s2 = jnp.where(i2 == i, vsc[i], s2)
    den = s1 + s2
    w1 = s1 / den
    w2 = s2 / den
    first_lo = i1 < i2
    lo = jnp.minimum(i1, i2)
    hi = jnp.maximum(i1, i2)
    wlo = jnp.where(first_lo, w1, w2)
    whi = jnp.where(first_lo, w2, w1)
    pair = jnp.where(lo == 0, hi - 1, jnp.where(lo == 1, 6 - hi, 5))
    bkt_ref[...] = (best * 6 + pair).reshape(1, 1, rows)
    wa = jnp.where(pair == 5, whi, wlo)
    wb = jnp.where(pair == 5, wlo, whi)

    aux = jnp.concatenate([wa, wb, jnp.zeros((LANES - 2, rows), F32)], axis=0)
    hx_ref[:, d:d + LANES] = aux.T


def _out_route_call(mixed, wout, h, g1, gain, sc, sh, wrt, rb, *, bb, tt, name):
    b, t, m = mixed.shape
    d = h.shape[-1]
    rows = bb * tt
    nti = t // tt
    xspec = pl.BlockSpec((bb, tt, d), lambda bi, i: (bi, i, 0))
    mspec = pl.BlockSpec((bb, 1, d), lambda bi, i: (bi, 0, 0))
    return pl.pallas_call(
        _out_route_kernel,
        grid=(b // bb, nti),
        in_specs=[pl.BlockSpec((bb, tt, m), lambda bi, i: (bi, i, 0)),
                  pl.BlockSpec((m, d), lambda bi, i: (0, 0)),
                  xspec, mspec,
                  pl.BlockSpec((1, 1, d), lambda bi, i: (0, 0, 0)), mspec, mspec,
                  pl.BlockSpec((N_EXPERTS, d), lambda bi, i: (0, 0)),
                  pl.BlockSpec((N_EXPERTS, 1), lambda bi, i: (0, 0))],
        out_specs=[xspec,
                   pl.BlockSpec((rows, HX_W), lambda bi, i: (bi * nti + i, 0)),
                   pl.BlockSpec((1, 1, rows), lambda bi, i: (bi * nti + i, 0, 0))],
        out_shape=[jax.ShapeDtypeStruct((b, t, d), F32),
                   jax.ShapeDtypeStruct((b * t, HX_W), F32),
                   jax.ShapeDtypeStruct(((b // bb) * nti, 1, rows), I32)],
        compiler_params=_cparams(("parallel", "parallel")),
        name=name,
    )(mixed, wout, h, g1, gain.reshape(1, 1, d), sc, sh, wrt, rb)


def _dispatch_kernel(pos_ref, hxp_ref, hxs_ref, xs_in, xs_out, sem, *, np_tiles):
    del xs_in
    i = pl.program_id(0)
    rows = hxp_ref.shape[0]

    def scatter(hx_ref):
        def row_copy(r):
            p = pos_ref[i * rows + r]
            return pltpu.make_async_copy(hx_ref.at[pl.ds(r, 1), :], xs_out.at[pl.ds(p, 1), :], sem)

        for r in range(rows):
            row_copy(r).start(priority=r % 2)
        for r in range(rows):
            row_copy(r).wait()

    @pl.when(i < np_tiles)
    def _():
        scatter(hxp_ref)

    @pl.when(i >= np_tiles)
    def _():
        scatter(hxs_ref)


def _dispatch_call(pos, hx_p, hx_s, xs_buf, rows):
    np_tiles, ns_tiles = hx_p.shape[0] // rows, hx_s.shape[0] // rows
    return pl.pallas_call(
        functools.partial(_dispatch_kernel, np_tiles=np_tiles),
        grid_spec=pltpu.PrefetchScalarGridSpec(
            num_scalar_prefetch=1, grid=(np_tiles + ns_tiles,),
            in_specs=[pl.BlockSpec((rows, HX_W), lambda i, ps: (jnp.minimum(i, np_tiles - 1), 0)),
                      pl.BlockSpec((rows, HX_W), lambda i, ps: (jnp.maximum(i - np_tiles, 0), 0)),
                      pl.BlockSpec(memory_space=pl.ANY)],
            out_specs=pl.BlockSpec(memory_space=pl.ANY),
            scratch_shapes=[pltpu.SemaphoreType.DMA(())]),
        out_shape=jax.ShapeDtypeStruct(xs_buf.shape, F32),
        input_output_aliases={3: 0},
        compiler_params=_cparams(("arbitrary",)),
        name="dispatch",
    )(pos, hx_p, hx_s, xs_buf)


def _moe_kernel(exp_ref, chg_ref, nxt_ref, valid_ref, xs_ref, wg_hbm, wu_hbm, wd_hbm,
                ys_ref, wg_f, wu_f, wd_f, wgu_s, wd_s, sem, *, layer):
    t = pl.program_id(0)

    def fetch(slot, e):
        return (pltpu.make_async_copy(wg_hbm.at[layer, e], wg_f.at[slot], sem.at[slot]),
                pltpu.make_async_copy(wu_hbm.at[layer, e], wu_f.at[slot], sem.at[slot]),
                pltpu.make_async_copy(wd_hbm.at[layer, e], wd_f.at[slot], sem.at[slot]))

    def recast(slot):
        rc = 128
        for i in range(D_MODEL // rc):
            rows = slice(i * rc, (i + 1) * rc)
            wgu_s[rows, slot * D_FF:(slot + 1) * D_FF] = wg_f[slot, rows, :].astype(BF16)
            wgu_s[rows, (2 + slot) * D_FF:(3 + slot) * D_FF] = wu_f[slot, rows, :].astype(BF16)
        for i in range(D_FF // rc):
            wd_s[slot * D_FF + i * rc:slot * D_FF + (i + 1) * rc, :] = wd_f[slot, i * rc:(i + 1) * rc, :].astype(BF16)

    for slot in range(2):
        @pl.when(t == 0)
        def _():
            for cp in fetch(slot, exp_ref[slot, 0]):
                cp.start()

        @pl.when(chg_ref[slot, t] > 0)
        def _():
            for cp in fetch(slot, exp_ref[slot, t]):
                cp.wait()
            recast(slot)

            @pl.when(nxt_ref[slot, t] >= 0)
            def _():
                for cp in fetch(slot, nxt_ref[slot, t]):
                    cp.start()

    @pl.when(valid_ref[t] > 0)
    def _():
        x = xs_ref[:, 0:D_MODEL].astype(BF16)
        w_a = xs_ref[:, D_MODEL:D_MODEL + 1]
        w_b = xs_ref[:, D_MODEL + 1:D_MODEL + 2]
        gu = _dot(x, wgu_s[...])
        lane = lax.broadcasted_iota(I32, (MOE_TM, 2 * D_FF), 1)
        act = _silu(gu[:, :2 * D_FF]) * gu[:, 2 * D_FF:] * jnp.where(lane < D_FF, w_a, w_b)
        ys_ref[...] = _dot(act.astype(BF16), wd_s[...])

    @pl.when(valid_ref[t] == 0)
    def _():
        ys_ref[...] = jnp.zeros_like(ys_ref)


def _moe_call(experts, chg, nxt, valid, xs, wg, wu, wd, layer):
    p = xs.shape[0]
    n_tiles = p // MOE_TM
    return pl.pallas_call(
        functools.partial(_moe_kernel, layer=layer),
        grid_spec=pltpu.PrefetchScalarGridSpec(
            num_scalar_prefetch=4, grid=(n_tiles,),
            in_specs=[pl.BlockSpec((MOE_TM, HX_W), lambda t, ex, ch, nx, va: (t, 0)),
                      pl.BlockSpec(memory_space=pl.ANY), pl.BlockSpec(memory_space=pl.ANY),
                      pl.BlockSpec(memory_space=pl.ANY)],
            out_specs=pl.BlockSpec((MOE_TM, D_MODEL), lambda t, ex, ch, nx, va: (t, 0)),
            scratch_shapes=[pltpu.VMEM((2, D_MODEL, D_FF), F32), pltpu.VMEM((2, D_MODEL, D_FF), F32),
                            pltpu.VMEM((2, D_FF, D_MODEL), F32),
                            pltpu.VMEM((D_MODEL, 4 * D_FF), BF16), pltpu.VMEM((2 * D_FF, D_MODEL), BF16),
                            pltpu.SemaphoreType.DMA((2,))]),
        out_shape=jax.ShapeDtypeStruct((p, D_MODEL), F32),
        compiler_params=_cparams(("arbitrary",)),
        name="moe",
    )(experts, chg, nxt, valid, xs, wg, wu, wd)


_PAIRS = [(0, 1), (0, 2), (0, 3), (1, 3), (1, 2), (3, 2)]
_EA_TABLE = np.array([g * EPG + p[0] for g in range(N_GROUPS) for p in _PAIRS], np.int32)
_EB_TABLE = np.array([g * EPG + p[1] for g in range(N_GROUPS) for p in _PAIRS], np.int32)


def _moe(hx_p, hx_s, bkt, xs_buf, wg, wu, wd, layer):
    n = hx_p.shape[0] + hx_s.shape[0]
    tm = MOE_TM
    ar = jnp.arange(N_BUCKETS, dtype=I32)
    blk = LANES
    nb = -(-n // blk)
    bkt_p = jnp.pad(bkt, (0, nb * blk - n), constant_values=N_BUCKETS)
    oh = (ar[:, None] == bkt_p[None, :]).astype(F32)
    triu = jnp.triu(jnp.ones((blk, blk), F32))
    inner = jnp.einsum('kbj,ji->kbi', oh.reshape(N_BUCKETS, nb, blk), triu)
    tot = inner[:, :, -1]
    outer = jnp.cumsum(tot, axis=1) - tot
    cs = (inner + outer[:, :, None]).reshape(N_BUCKETS, nb * blk)
    counts = (outer[:, -1] + tot[:, -1]).astype(I32)
    padded = ((counts + tm - 1) // tm) * tm
    ends = jnp.cumsum(padded)
    offs = ends - padded
    pos = jnp.sum(oh * (cs - 1.0 + offs.astype(F32)[:, None]), axis=0).astype(I32)[:n]
    n_tiles = -(-n // tm) + N_BUCKETS
    p = n_tiles * tm
    if xs_buf is None:
        xs_buf = jnp.zeros((p, HX_W), F32)
    tstart = jnp.arange(n_tiles, dtype=I32) * tm
    tb = jnp.sum((ends[None, :] <= tstart[:, None]).astype(I32), axis=1)
    n_used = ends[-1] // tm
    valid = (jnp.arange(n_tiles, dtype=I32) < n_used).astype(I32)
    last_b = tb[jnp.maximum(n_used - 1, 0)]
    tb = jnp.clip(jnp.where(valid > 0, tb, last_b), 0, N_BUCKETS - 1)
    ea = jnp.asarray(_EA_TABLE)[tb]
    eb = jnp.asarray(_EB_TABLE)[tb]
    experts = jnp.stack([ea, eb])
    chg = jnp.concatenate([jnp.ones((2, 1), I32), (experts[:, 1:] != experts[:, :-1]).astype(I32)], axis=1)
    tidx = jnp.arange(n_tiles, dtype=I32)
    chg_at = jnp.where(chg > 0, tidx[None, :], n_tiles)
    nxt_at = jnp.concatenate([lax.cummin(chg_at, axis=1, reverse=True)[:, 1:], jnp.full((2, 1), n_tiles, I32)], axis=1)
    nxt = jnp.where(nxt_at < n_tiles,
                    jnp.take_along_axis(experts, jnp.minimum(nxt_at, n_tiles - 1), axis=1), -1)
    xs = _dispatch_call(pos, hx_p, hx_s, xs_buf, DISPATCH_ROWS)
    ys = _moe_call(experts, chg, nxt, valid, xs, wg, wu, wd, layer)
    return ys.at[pos].get(mode="promise_in_bounds", unique_indices=True), xs


def _final_kernel(h_ref, y_ref, g_ref, gain_ref, o_ref):
    bb, tt, d = h_ref.shape
    h = h_ref[...] + g_ref[...] * y_ref[...].reshape(bb, tt, d)
    o_ref[...] = _rms(h) * gain_ref[...]


def _final_call(h, y2d, off, g2, gain, *, bb, tt, name):
    b, t, d = h.shape
    nti = t // tt
    rows = bb * tt
    xspec = pl.BlockSpec((bb, tt, d), lambda bi, i: (bi, i, 0))
    return pl.pallas_call(
        _final_kernel,
        grid=(b // bb, nti),
        in_specs=[xspec, pl.BlockSpec((rows, d), lambda bi, i: (off + bi * nti + i, 0)),
                  pl.BlockSpec((bb, 1, d), lambda bi, i: (bi, 0, 0)),
                  pl.BlockSpec((1, 1, d), lambda bi, i: (0, 0, 0))],
        out_specs=xspec,
        out_shape=jax.ShapeDtypeStruct((b, t, d), F32),
        compiler_params=_cparams(("parallel", "parallel")),
        name=name,
    )(h, y2d, g2, gain.reshape(1, 1, d))


def _group_cfg(b, t):
    if t >= 512:
        return dict(bb=1, tt=512, tt_out=1024, L=CHUNK, CB=4, Lc=4 * CHUNK, CBc=1)
    return dict(bb=b, tt=t, tt_out=t, L=min(CHUNK, t), CB=1, Lc=min(CHUNK, t), CBc=1)


def _rope_tables(pos0, t):
    half = DK_C // 2
    inv = np.power(np.float64(ROPE_BASE), -np.linspace(0.0, 1.0, half, dtype=np.float64))
    ang = (np.float64(pos0) + np.arange(t, dtype=np.float64))[:, None] * inv[None, :]
    return jnp.asarray(np.cos(ang), F32), jnp.asarray(np.sin(ang), F32)


def kernel(x_prompt, x_sample, c_prompt, c_sample, state_a_conv, state_a_rec, state_b_c, state_b_n, state_b_m,
           state_c_rec, w_mod, b_mod, norm_mix, norm_ffn, w_in_ab, conv_a, a_log, dt_bias, norm_a, gate_bias_b,
           norm_b, w_out_ab, w_in_c, w_out_c, w_router, router_bias, w_gate, w_up, w_down, norm_final):
    d = D_MODEL
    bp, tp, _ = x_prompt.shape
    bs, ts, _ = x_sample.shape
    n_ab, n_c = w_in_ab.shape[0], w_in_c.shape[0]

    mod_all = _mod_call(jnp.concatenate([c_prompt, c_sample], axis=0), w_mod, b_mod)

    o = np.cumsum((0,) + AB_SIZES)
    w_ab_b = _cast_call(w_in_ab, "cast_w_in_ab")
    w_ab_main = jnp.concatenate([w_ab_b[:, :, o[0]:o[1]], w_ab_b[:, :, o[3]:o[7]], w_ab_b[:, :, o[8]:o[9]]],
                                axis=-1)
    w_ab_gate = jnp.concatenate([w_in_ab[:, :, o[1]:o[3]], w_in_ab[:, :, o[7]:o[8]]], axis=-1)
    w_ab_gate_pad = jnp.pad(w_ab_gate, ((0, 0), (0, 0), (0, LANES - N_GATE_COLS)))
    w_c = _cast_call(w_in_c, "cast_w_in_c")
    w_out_ab_b = _cast_call(w_out_ab, "cast_w_out_ab")
    w_out_c_b = _cast_call(w_out_c, "cast_w_out_c")
    wrt = w_router.T
    rb = router_bias.reshape(N_EXPERTS, 1)
    zeros4 = jnp.zeros((n_ab, H_A), F32)
    gpar_row0 = jnp.concatenate([a_log, jnp.zeros((n_ab, LANES - H_A), F32)], axis=1)
    gpar_row1 = jnp.concatenate([dt_bias, zeros4, gate_bias_b, jnp.zeros((n_ab, LANES - 16), F32)], axis=1)
    gpar = jnp.concatenate([gpar_row0[:, None], gpar_row1[:, None], jnp.zeros((n_ab, 6, LANES), F32)], axis=1)
    gpart = jnp.swapaxes(jnp.concatenate([gpar[:, :2, :N_GATE_COLS], jnp.zeros((n_ab, LANES - 2, N_GATE_COLS), F32)],
                                         axis=1), 1, 2)

    groups = []
    zf = lambda *s: jnp.zeros(s, F32)
    groups.append(dict(
        x=x_prompt, b=bp, t=tp, pos0=0.0, mod=mod_all[:, :bp], row0=0,
        conv=zf(n_ab, bp, CONV_W - 1, QKV_A), sa=zf(n_ab, bp, H_A, DK_A, DV_A), cb=zf(n_ab, bp, H_B, DQK_B, DV_B),
        nb=zf(n_ab, bp, H_B, DQK_B), mb=zf(n_ab, bp, H_B), sc=zf(n_c, bp, H_C, DK_C, DV_C)))
    groups.append(dict(
        x=x_sample, b=bs, t=ts, pos0=float(PAST_LEN), mod=mod_all[:, bp:], row0=bp * tp,
        conv=state_a_conv, sa=state_a_rec, cb=state_b_c, nb=state_b_n, mb=state_b_m, sc=state_c_rec))
    n_tot = bp * tp + bs * ts
    for g in groups:
        g.update(_group_cfg(g["b"], g["t"]))
        g["h"] = g["x"]
        g["prev"] = None
        g["new_ab"], g["new_c"] = [], []
        g["cos"], g["sin"] = _rope_tables(g["pos0"], g["t"])
        g["roff"] = g["row0"] // (g["bb"] * g["tt"])

    xs_buf = None
    for layer in range(DEPTH):
        li = layer // 2
        bkts, hxs = [], []
        for gi, g in enumerate(groups):
            b, t, bb, tt = g["b"], g["t"], g["bb"], g["tt"]
            mods = [m.reshape(b, 1, d) for m in jnp.split(g["mod"][layer], N_MOD, axis=-1)]
            sh1, sc1, g1, sh2, sc2, g2 = mods
            tag = f"l{layer}g{gi}"
            if layer % 2 == 0:
                outs = _ln_mm_call(g["h"], g["prev"], norm_mix[layer], sc1, sh1, w_ab_main[li],
                                   (w_ab_gate_pad[li], g["conv"][li], conv_a[li]), bb=bb, tt=tt, tn=512,
                                   out_dtype=F32, name="ln_ab_" + tag)
                if g["prev"] is not None:
                    g["h"], outs = outs[0], outs[1:]
                proj, gates, gates_t, qkv, conv_n = outs
                gates_t = gates_t.reshape(N_GATE_COLS, b, t).transpose(1, 0, 2)
                mixed, sa_n, cb_n, nb_n, mb_n = _mixer_ab_call(
                    qkv, proj, gates, gates_t, g["sa"][li], g["cb"][li],
                    g["nb"][li].reshape(b, H_B, 1, DQK_B), g["mb"][li].reshape(b, H_B, 1, 1),
                    gpar[li], gpart[li], norm_a[li].reshape(1, DV_A), norm_b[li].reshape(1, DV_B),
                    L=g["L"], CB=g["CB"], name="mixer_ab_" + tag)
                g["new_ab"].append((conv_n, sa_n, cb_n, nb_n.reshape(b, H_B, DQK_B), mb_n.reshape(b, H_B)))
                wout = w_out_ab_b[li]
            else:
                outs = _ln_mm_call(g["h"], g["prev"], norm_mix[layer], sc1, sh1, w_c[li], None,
                                   bb=bb, tt=tt, tn=512, out_dtype=BF16, name="ln_c_" + tag)
                if g["prev"] is not None:
                    g["h"], outs = outs[0], outs[1:]
                (proj,) = outs
                mixed, sc_n = _mixer_c_call(proj, g["cos"], g["sin"], g["sc"][li], L=g["Lc"], CB=g["CBc"],
                                            name="mixer_c_" + tag)
                g["new_c"].append(sc_n)
                wout = w_out_c_b[li]
            tto = g["tt_out"]
            g["h"], hx, bkt = _out_route_call(mixed, wout, g["h"], g1, norm_ffn[layer], sc2, sh2, wrt, rb,
                                              bb=bb, tt=tto, name="out_route_" + tag)
            bkts.append(bkt.reshape(-1))
            hxs.append(hx)
            g["g2"] = g2
        y_moe, xs_buf = _moe(hxs[0], hxs[1], jnp.concatenate(bkts), xs_buf, w_gate, w_up, w_down, layer)
        for g in groups:
            g["prev"] = (y_moe, g["roff"], g["g2"])

    outs = []
    for gi, g in enumerate(groups):
        y2d, off, g2 = g["prev"]
        y = _final_call(g["h"], y2d, off, g2, norm_final, bb=g["bb"], tt=g["tt"], name=f"final_g{gi}")
        ab = [jnp.stack(s) for s in zip(*g["new_ab"])]
        outs.append((y, ab[0], ab[1], ab[2], ab[3], ab[4], jnp.stack(g["new_c"])))
    p, s = outs
    return (p[0], s[0], p[1], p[2], p[3], p[4], p[5], p[6], s[1], s[2], s[3], s[4], s[5], s[6])
```

```python
import functools

import numpy as np
import jax
import jax.numpy as jnp
from jax import lax
from jax.experimental import pallas as pl
from jax.experimental.pallas import tpu as pltpu

F32 = jnp.float32
BF16 = jnp.bfloat16
I32 = jnp.int32

D_MODEL = 1024
DEPTH = 2
CHUNK = 64
H_A, DK_A, DV_A, CONV_W = 4, 128, 128, 4
QKV_A = H_A * (2 * DK_A + DV_A)
H_B, DQK_B, DV_B = 4, 64, 128
H_C, DK_C, DV_C = 4, 256, 512
ROPE_BASE = 10000.0
PAST_LEN = 4096
AB_SIZES = (QKV_A, H_A, H_A, H_A * DV_A, H_B * DQK_B, H_B * DQK_B, H_B * DV_B, 2 * H_B, H_B * DV_B)
MIX_AB = H_A * DV_A + H_B * DV_B
MIX_C = H_C * DV_C
N_EXPERTS, N_GROUPS, EPG, D_FF = 16, 4, 4, 512
N_MOD = 6
EPS = 1e-6

LANES = 128
N_GATE_COLS = 16
HX_W = D_MODEL + LANES
N_BUCKETS = N_GROUPS * 6
MOE_TM = 256
VMEM_LIMIT = 48 * 1024 * 1024

NN = ((1,), (0,))
NT = ((1,), (1,))
TN = ((0,), (0,))


def _dot(a, b, dims=NN):
    return lax.dot_general(a, b, (dims, ((), ())), preferred_element_type=F32)


def _mmb(a, b, dims=NN):
    return _dot(a.astype(BF16), b.astype(BF16), dims)


def _split2(x):
    hi = x.astype(BF16)
    lo = (x - hi.astype(F32)).astype(BF16)
    return hi, lo


def _split3(x):
    hi = x.astype(BF16)
    r = x - hi.astype(F32)
    mid = r.astype(BF16)
    lo = (r - mid.astype(F32)).astype(BF16)
    return hi, mid, lo


def _mm3(a, b, dims=NN):
    ah, al = _split2(a)
    bh, bl = _split2(b)
    return _dot(ah, bh, dims) + (_dot(ah, bl, dims) + _dot(al, bh, dims))


def _mm_mask_l(mask_bf16, x, dims=NN):
    h, m, l = _split3(x)
    return _dot(mask_bf16, h, dims) + (_dot(mask_bf16, m, dims) + _dot(mask_bf16, l, dims))


def _mm_mask_r(x, mask_bf16):
    h, m, l = _split3(x)
    return _dot(h, mask_bf16) + (_dot(m, mask_bf16) + _dot(l, mask_bf16))


def _sigmoid(x):
    return 1.0 / (1.0 + jnp.exp(-x))


def _silu(x):
    return x * _sigmoid(x)


def _softplus(x):
    return jnp.maximum(x, 0.0) + jnp.log(1.0 + jnp.exp(-jnp.abs(x)))


def _rms(x, eps=EPS):
    return x * lax.rsqrt(jnp.mean(x * x, axis=-1, keepdims=True) + eps)


def _cparams(sem):
    return pltpu.CompilerParams(dimension_semantics=sem, vmem_limit_bytes=VMEM_LIMIT)


def _mod_kernel(c_ref, w_ref, b_ref, o_ref):
    c = c_ref[...]
    o_ref[0] = _mm3(_silu(c), w_ref[0]) + b_ref[0]


def _mod_call(c_all, w_mod, b_mod):
    bt = c_all.shape[0]
    e = w_mod.shape[-1]
    tn = 1024
    return pl.pallas_call(
        _mod_kernel,
        grid=(DEPTH, e // tn),
        in_specs=[pl.BlockSpec((bt, D_MODEL), lambda l, j: (0, 0)),
                  pl.BlockSpec((1, D_MODEL, tn), lambda l, j: (l, 0, j)),
                  pl.BlockSpec((1, 1, tn), lambda l, j: (l, 0, j))],
        out_specs=pl.BlockSpec((1, bt, tn), lambda l, j: (l, 0, j)),
        out_shape=jax.ShapeDtypeStruct((DEPTH, bt, e), F32),
        compiler_params=_cparams(("parallel", "parallel")),
        name="mod",
    )(c_all, w_mod, b_mod.reshape(DEPTH, 1, e))


def _ln_mm_kernel(*refs, has_prev, has_gates, tn):
    it = iter(refs)
    x_ref = next(it)
    if has_prev:
        yp_ref, gp_ref = next(it), next(it)
    gain_ref, sc_ref, sh_ref, w_ref = next(it), next(it), next(it), next(it)
    if has_gates:
        wg_ref, conv0_ref, convw_ref = next(it), next(it), next(it)
    if has_prev:
        h_ref = next(it)
    proj_ref = next(it)
    if has_gates:
        g_ref, gt_ref, qkv_ref, conv_o = next(it), next(it), next(it), next(it)
    hn_s = next(it)
    if has_gates:
        tail_s = next(it)

        @pl.when(pl.program_id(1) == 0)
        def _():
            tail_s[...] = conv0_ref[...]

    bb, tt, d = x_ref.shape
    rows = bb * tt
    x = x_ref[...]
    if has_prev:
        x = x + gp_ref[...] * yp_ref[...].reshape(bb, tt, d)
        h_ref[...] = x
    hn = _rms(x) * gain_ref[...]
    hn = hn * (1.0 + sc_ref[...]) + sh_ref[...]
    hn2 = hn.reshape(rows, d)
    hn_s[...] = hn2.astype(BF16)
    if has_gates:
        gates = _mm3(hn2, wg_ref[...])
        g_ref[...] = gates.reshape(bb, tt, LANES)
        gt_ref[...] = gates.T[0:N_GATE_COLS, :]
    n_conv = QKV_A // tn if has_gates else 0
    n_tiles = w_ref.shape[-1] // tn

    def project(j):
        return _dot(hn_s[...], w_ref[:, j * tn:(j + 1) * tn]).reshape(bb, tt, tn)

    def finish(j, y):
        cols = slice(j * tn, (j + 1) * tn)
        if j < n_conv:
            prev = tail_s[:, :, cols]
            acc = convw_ref[CONV_W - 1:CONV_W, cols] * y
            for lag in range(1, CONV_W):
                shifted = jnp.concatenate([prev[:, CONV_W - 1 - lag:, :], y[:, :tt - lag, :]], axis=1)
                acc = acc + convw_ref[CONV_W - 1 - lag:CONV_W - lag, cols] * shifted
            qkv_ref[:, :, cols] = _silu(acc).astype(qkv_ref.dtype)
            tail = y[:, tt - (CONV_W - 1):tt, :]
            tail_s[:, :, cols] = tail
            conv_o[:, :, cols] = tail
        else:
            pcols = slice((j - n_conv) * tn, (j - n_conv + 1) * tn)
            proj_ref[:, :, pcols] = y.astype(proj_ref.dtype)

    y = project(0)
    for j in range(n_tiles):
        y_next = project(j + 1) if j + 1 < n_tiles else None
        finish(j, y)
        y = y_next


def _ln_mm_call(x, prev, gain, sc, sh, w, gates_w, *, bb, tt, tn, out_dtype, name):
    b, t, d = x.shape
    e = w.shape[1]
    rows = bb * tt
    nti = t // tt
    grid = (b // bb, nti)
    has_prev = prev is not None
    has_gates = gates_w is not None
    xspec = pl.BlockSpec((bb, tt, d), lambda bi, i: (bi, i, 0))
    mspec = pl.BlockSpec((bb, 1, d), lambda bi, i: (bi, 0, 0))
    in_specs, args = [xspec], [x]
    if has_prev:
        y2d, off, gp = prev
        in_specs += [pl.BlockSpec((rows, d), lambda bi, i: (off + bi * nti + i, 0)), mspec]
        args += [y2d, gp]
    in_specs += [pl.BlockSpec((1, 1, d), lambda bi, i: (0, 0, 0)), mspec, mspec,
                 pl.BlockSpec((d, e), lambda bi, i: (0, 0), pipeline_mode=pl.Buffered(1))]
    args += [gain.reshape(1, 1, d), sc, sh, w]
    tail_spec = pl.BlockSpec((bb, CONV_W - 1, QKV_A), lambda bi, i: (bi, 0, 0))
    if has_gates:
        wg, conv0, convw = gates_w
        in_specs += [pl.BlockSpec((d, LANES), lambda bi, i: (0, 0)), tail_spec,
                     pl.BlockSpec((CONV_W, QKV_A), lambda bi, i: (0, 0))]
        args += [wg, conv0, convw]
    e_proj = e - QKV_A if has_gates else e
    out_specs, out_shape = [], []
    if has_prev:
        out_specs.append(xspec)
        out_shape.append(jax.ShapeDtypeStruct((b, t, d), F32))
    out_specs.append(pl.BlockSpec((bb, tt, e_proj), lambda bi, i: (bi, i, 0)))
    out_shape.append(jax.ShapeDtypeStruct((b, t, e_proj), out_dtype))
    scratch = [pltpu.VMEM((rows, d), BF16)]
    if has_gates:
        out_specs += [pl.BlockSpec((bb, tt, LANES), lambda bi, i: (bi, i, 0)),
                      pl.BlockSpec((N_GATE_COLS, rows), lambda bi, i: (0, bi * nti + i)),
                      pl.BlockSpec((bb, tt, QKV_A), lambda bi, i: (bi, i, 0)), tail_spec]
        out_shape += [jax.ShapeDtypeStruct((b, t, LANES), F32),
                      jax.ShapeDtypeStruct((N_GATE_COLS, b * t), F32),
                      jax.ShapeDtypeStruct((b, t, QKV_A), BF16),
                      jax.ShapeDtypeStruct((b, CONV_W - 1, QKV_A), F32)]
        scratch.append(pltpu.VMEM((bb, CONV_W - 1, QKV_A), F32))
    return pl.pallas_call(
        functools.partial(_ln_mm_kernel, has_prev=has_prev, has_gates=has_gates, tn=tn),
        grid=grid, in_specs=in_specs, out_specs=out_specs, out_shape=out_shape,
        scratch_shapes=scratch,
        compiler_params=_cparams(("parallel", "arbitrary" if has_gates else "parallel")),
        name=name,
    )(*args)


def _tri_inv_all(a_list, eye, length, mm):
    xs = [-a for a in a_list]
    ps = [eye + x for x in xs]
    n = 2
    while n < length:
        xs = [mm(x, x) for x in xs]
        ps = [p + mm(p, x) for p, x in zip(ps, xs)]
        n *= 2
    return ps


def _mixer_ab_kernel(qkv_ref, z_ref, qkb_ref, vb_ref, ob_ref, g_ref, gt_ref,
                     sa0_ref, cb0_ref, nb0_ref, mb0_ref,
                     gpar_ref, gpart_ref, na_ref, nbn_ref,
                     mixed_ref, sa_o, cb_o, nb_o, mb_o, *, L, CB):
    @pl.when(pl.program_id(1) == 0)
    def _():
        sa_o[...] = sa0_ref[...]
        cb_o[...] = cb0_ref[...]
        nb_o[...] = nb0_ref[...]
        mb_o[...] = mb0_ref[...]

    ii = lax.broadcasted_iota(I32, (L, L), 0)
    jj = lax.broadcasted_iota(I32, (L, L), 1)
    tri = ii >= jj
    strict = ii > jj
    eye = (ii == jj).astype(F32)
    tril_b = tri.astype(BF16)
    triu_b = (ii <= jj).astype(BF16)
    rowi = lax.broadcasted_iota(I32, (N_GATE_COLS, L), 0)
    gpar = gpar_ref[...]
    gpart = gpart_ref[...]
    neat = -jnp.exp(gpart[:, 0:1])
    norm_a = na_ref[...]
    norm_b = nbn_ref[...]

    mm = _mmb

    def qkv_tile(r0, col0):
        return qkv_ref[0, r0:r0 + L, col0:col0 + DK_A].astype(F32)

    chunks = range(CB)
    items = [(c, h) for c in chunks for h in range(H_A)]
    rs = [slice(c * L, (c + 1) * L) for c in chunks]

    pre = [g_ref[0, r, :] + gpar[1:2, :] for r in rs]
    pret = [gt_ref[0, :, r] + gpart[:, 1:2] for r in rs]
    csrct = [jnp.where(rowi < 4, neat * _softplus(p), jnp.where(rowi >= 12, -_softplus(-p), 0.0)) for p in pret]
    gcumt = [_mm_mask_r(x, triu_b) for x in csrct]
    gcum = [_mm_mask_l(tril_b, x, NT) for x in csrct]
    beta_all = [_sigmoid(p) for p in pre]

    q = [qkv_tile(c * L, h * DK_A) for c, h in items]
    k = [qkv_tile(c * L, H_A * DK_A + h * DK_A) for c, h in items]
    v = [qkv_tile(c * L, 2 * H_A * DK_A + h * DV_A) for c, h in items]
    q = [x * lax.rsqrt(jnp.sum(x * x, axis=-1, keepdims=True) + EPS) * (DK_A ** -0.5) for x in q]
    k = [x * lax.rsqrt(jnp.sum(x * x, axis=-1, keepdims=True) + EPS) for x in k]
    gc_c = [gcum[c][:, h:h + 1] for c, h in items]
    gc_r = [gcumt[c][h:h + 1, :] for c, h in items]
    decay = [jnp.where(tri, jnp.exp(jnp.where(tri, a - b, 0.0)), 0.0) for a, b in zip(gc_c, gc_r)]
    beta_c = [beta_all[c][:, 4 + h:5 + h] for c, h in items]
    kb = [x * b for x, b in zip(k, beta_c)]
    a_list = [jnp.where(strict, mm(x, y, NT) * d, 0.0) for x, y, d in zip(kb, k, decay)]
    attn = [jnp.where(tri, mm(x, y, NT) * d, 0.0) for x, y, d in zip(q, k, decay)]
    eg = [jnp.exp(x) for x in gc_c]
    gl = [x[L - 1:L, :] for x in gc_c]
    rhs = [jnp.concatenate([x * e, y * b], axis=-1) for x, e, y, b in zip(kb, eg, v, beta_c)]
    qeg = [x * e for x, e in zip(q, eg)]
    kdec = [x * jnp.exp(g - gc) for x, g, gc in zip(k, gl, gc_c)]
    sdec = [jnp.exp(g) for g in gl]
    a_list = [a.astype(BF16) for a in a_list]
    tinv = _tri_inv_all([a.astype(F32) for a in a_list], eye, L, mm)
    sol = [mm(t, r) for t, r in zip(tinv, rhs)]
    sol_hl = [_split2(s) for s in sol]
    resid = [r - (s + (_dot(a, hi) + _dot(a, lo))) for r, s, a, (hi, lo) in zip(rhs, sol, a_list, sol_hl)]
    sol = [s + mm(t, r) for s, t, r in zip(sol, tinv, resid)]

    heads = range(H_B)
    qb = [qkb_ref[0, rs[c], h * DQK_B:(h + 1) * DQK_B] * (DQK_B ** -0.5) for c, h in items]
    kbb = [qkb_ref[0, rs[c], H_B * DQK_B + h * DQK_B:H_B * DQK_B + (h + 1) * DQK_B] for c, h in items]
    b_c = [gcum[c][:, 12 + h:13 + h] for c, h in items]
    dm = [jnp.where(tri, gcum[c][:, 12 + h:13 + h] - gcumt[c][12 + h:13 + h, :] + pret[c][8 + h:9 + h, :], -jnp.inf)
          for c, h in items]
    dmax = [jnp.max(x, axis=-1, keepdims=True) for x in dm]
    qkm = [mm(x, y, NT) for x, y in zip(qb, kbb)]
    li_c = [pre[c][:, 8 + h:9 + h] for c, h in items]

    for c in chunks:
        ia = [c * H_A + h for h in heads]
        s = [sa_o[0, h] for h in heads]
        v_new = [sol[i][:, DK_A:] - mm(sol[i][:, :DK_A], s[h]) for h, i in enumerate(ia)]
        m_prev = [mb_o[0, h] for h in heads]
        a0 = [b_c[i] + m_prev[h] for h, i in enumerate(ia)]
        m_t = [jnp.maximum(a0[h], dmax[i]) for h, i in enumerate(ia)]
        w0 = [jnp.exp(a0[h] - m_t[h]) for h in heads]
        sm = [qkm[i] * jnp.exp(dm[i] - m_t[h]) for h, i in enumerate(ia)]
        cst = [cb_o[0, h] for h in heads]
        nst = [nb_o[0, h] for h in heads]
        vb = [vb_ref[0, rs[c], h * DV_B:(h + 1) * DV_B] for h in heads]
        o_a = [mm(qeg[i], s[h]) + mm(attn[i], v_new[h]) for h, i in enumerate(ia)]
        for h, i in enumerate(ia):
            sa_o[0, h] = s[h] * sdec[i] + mm(kdec[i], v_new[h], TN)
        num = [w0[h] * mm(qb[i], cst[h]) + mm(sm[h], vb[h]) for h, i in enumerate(ia)]
        den = [w0[h] * jnp.sum(qb[i] * nst[h], axis=-1, keepdims=True) + jnp.sum(sm[h], axis=-1, keepdims=True)
               for h, i in enumerate(ia)]
        hh = [num[h] / jnp.maximum(jnp.abs(den[h]), jnp.exp(-m_t[h])) for h in heads]
        m_new = [m_t[h][L - 1:L, :] for h in heads]
        kw = [kbb[i] * jnp.exp(b_c[i][L - 1:L, :] - b_c[i] + li_c[i] - m_new[h]) for h, i in enumerate(ia)]
        for h in heads:
            w0l = w0[h][L - 1:L, :]
            cb_o[0, h] = w0l * cst[h] + mm(kw[h], vb[h], TN)
            nb_o[0, h] = w0l * nst[h] + jnp.sum(kw[h], axis=0, keepdims=True)
            mb_o[0, h] = m_new[h]
        for h in heads:
            zg = z_ref[0, rs[c], h * DV_A:(h + 1) * DV_A]
            mixed_ref[0, rs[c], h * DV_A:(h + 1) * DV_A] = (_rms(o_a[h]) * norm_a * _silu(zg)).astype(mixed_ref.dtype)
            og = ob_ref[0, rs[c], h * DV_B:(h + 1) * DV_B]
            c0 = H_A * DV_A + h * DV_B
            mixed_ref[0, rs[c], c0:c0 + DV_B] = (_rms(hh[h]) * norm_b * _sigmoid(og)).astype(mixed_ref.dtype)


def _mixer_ab_call(qkv, proj, gates, gates_t, sa0, cb0, nb0, mb0, gpar, gpart, norm_a, norm_b, *, L, CB, name):
    b, t, _ = proj.shape
    tb = L * CB
    nt = t // tb

    def col(width, idx):
        return pl.BlockSpec((1, tb, width), lambda bi, i: (bi, i, idx))

    def const(shape):
        return pl.BlockSpec(shape, lambda bi, i: (0,) * len(shape))

    def state(shape):
        return pl.BlockSpec((1,) + shape, lambda bi, i: (bi,) + (0,) * len(shape))

    in_specs = [col(QKV_A, 0), col(512, 0), col(512, 1), col(512, 2), col(512, 3),
                col(LANES, 0), pl.BlockSpec((1, N_GATE_COLS, tb), lambda bi, i: (bi, 0, i)),
                state((H_A, DK_A, DV_A)), state((H_B, DQK_B, DV_B)),
                state((H_B, 1, DQK_B)), state((H_B, 1, 1)),
                const((8, LANES)), const((N_GATE_COLS, LANES)),
                const((1, DV_A)), const((1, DV_B))]
    out_specs = [pl.BlockSpec((1, tb, MIX_AB), lambda bi, i: (bi, i, 0)),
                 state((H_A, DK_A, DV_A)), state((H_B, DQK_B, DV_B)),
                 state((H_B, 1, DQK_B)), state((H_B, 1, 1))]
    out_shape = [jax.ShapeDtypeStruct((b, t, MIX_AB), BF16),
                 jax.ShapeDtypeStruct((b, H_A, DK_A, DV_A), F32),
                 jax.ShapeDtypeStruct((b, H_B, DQK_B, DV_B), F32),
                 jax.ShapeDtypeStruct((b, H_B, 1, DQK_B), F32),
                 jax.ShapeDtypeStruct((b, H_B, 1, 1), F32)]
    return pl.pallas_call(
        functools.partial(_mixer_ab_kernel, L=L, CB=CB),
        grid=(b, nt), in_specs=in_specs, out_specs=out_specs, out_shape=out_shape,
        compiler_params=_cparams(("parallel", "arbitrary")),
        name=name,
    )(qkv, proj, proj, proj, proj, gates, gates_t, sa0, cb0, nb0, mb0, gpar, gpart, norm_a, norm_b)


def _mixer_c_kernel(q_ref, k_ref, v_ref, g_ref, cos_ref, sin_ref, s0_ref, mixed_ref, s_o, intra_s, *, L, CB):
    log_gamma = [float(np.log1p(-np.exp2(-5.0 - h))) for h in range(H_C)]

    @pl.when(pl.program_id(1) == 0)
    def _():
        s_o[...] = s0_ref[...]
        ii = lax.broadcasted_iota(I32, (L, L), 0)
        jj = lax.broadcasted_iota(I32, (L, L), 1)
        rel = (ii - jj).astype(F32)
        for h in range(H_C):
            intra_s[h] = jnp.where(rel >= 0, jnp.exp(log_gamma[h] * jnp.maximum(rel, 0.0)), 0.0)

    idx = lax.broadcasted_iota(I32, (L, 1), 0).astype(F32)
    half = DK_C // 2

    def rope(x, cos, sin):
        x1, x2 = x[:, :half], x[:, half:]
        return jnp.concatenate([x1 * cos - x2 * sin, x1 * sin + x2 * cos], axis=-1)

    for c in range(CB):
        r0 = c * L
        cos = cos_ref[r0:r0 + L, :]
        sin = sin_ref[r0:r0 + L, :]
        for h in range(H_C):
            lg = log_gamma[h]
            intra = intra_s[h]
            q_decay = jnp.exp(lg * (idx + 1.0))
            k_decay = jnp.exp(lg * (L - 1.0 - idx))
            s_decay = float(np.exp(lg * L))
            q = rope(q_ref[0, r0:r0 + L, h * DK_C:(h + 1) * DK_C].astype(F32), cos, sin)
            k = rope(k_ref[0, r0:r0 + L, h * DK_C:(h + 1) * DK_C].astype(F32), cos, sin) * (DK_C ** -0.5)
            v = v_ref[0, r0:r0 + L, h * DV_C:(h + 1) * DV_C]
            s = s_o[0, h]
            o = _mmb(_mmb(q, k, NT) * intra, v) + _mmb(q * q_decay, s)
            s_o[0, h] = s_decay * s + _mmb(k * k_decay, v, TN)
            gg = g_ref[0, r0:r0 + L, h * DV_C:(h + 1) * DV_C].astype(F32)
            o = _rms(o) * _silu(gg)
            mixed_ref[0, r0:r0 + L, h * DV_C:(h + 1) * DV_C] = o.astype(mixed_ref.dtype)


def _mixer_c_call(proj, cos, sin, s0, *, L, CB, name):
    b, t, _ = proj.shape
    tb = L * CB
    nt = t // tb
    qk_w, v_w = H_C * DK_C, H_C * DV_C
    sspec = pl.BlockSpec((1, H_C, DK_C, DV_C), lambda bi, i: (bi, 0, 0, 0))
    tspec = pl.BlockSpec((tb, DK_C // 2), lambda bi, i: (i, 0))
    return pl.pallas_call(
        functools.partial(_mixer_c_kernel, L=L, CB=CB),
        grid=(b, nt),
        in_specs=[pl.BlockSpec((1, tb, qk_w), lambda bi, i: (bi, i, 0)),
                  pl.BlockSpec((1, tb, qk_w), lambda bi, i: (bi, i, 1)),
                  pl.BlockSpec((1, tb, v_w), lambda bi, i: (bi, i, 1)),
                  pl.BlockSpec((1, tb, v_w), lambda bi, i: (bi, i, 2)),
                  tspec, tspec, sspec],
        out_specs=[pl.BlockSpec((1, tb, MIX_C), lambda bi, i: (bi, i, 0)), sspec],
        out_shape=[jax.ShapeDtypeStruct((b, t, MIX_C), BF16),
                   jax.ShapeDtypeStruct((b, H_C, DK_C, DV_C), F32)],
        scratch_shapes=[pltpu.VMEM((H_C, L, L), F32)],
        compiler_params=_cparams(("parallel", "arbitrary")),
        name=name,
    )(proj, proj, proj, proj, cos, sin, s0)


def _out_route_kernel(mixed_ref, wout_ref, h_ref, g1_ref, gain_ref, sc_ref, sh_ref, wrt_ref, rb_ref,
                      hnew_ref, hx_ref, bkt_ref):
    bb, tt, m = mixed_ref.shape
    d = h_ref.shape[-1]
    rows = bb * tt
    y = _dot(mixed_ref[...].reshape(rows, m), wout_ref[...])
    h = h_ref[...] + g1_ref[...] * y.reshape(bb, tt, d)
    hnew_ref[...] = h
    hn = _rms(h) * gain_ref[...]
    hn = (hn * (1.0 + sc_ref[...]) + sh_ref[...]).reshape(rows, d)
    hx_ref[:, 0:d] = hn

    logits = _mm3(wrt_ref[...], hn, NT)
    score = _sigmoid(logits)
    sel = score + rb_ref[...]

    def row(a, r):
        return a[r:r + 1, :]

    gscore = []
    for g in range(N_GROUPS):
        a, b, c, e = (row(sel, EPG * g + i) for i in range(EPG))
        hi_ab, lo_ab, hi_ce, lo_ce = jnp.maximum(a, b), jnp.minimum(a, b), jnp.maximum(c, e), jnp.minimum(c, e)
        top1 = jnp.maximum(hi_ab, hi_ce)
        top2 = jnp.maximum(jnp.maximum(lo_ab, lo_ce), jnp.minimum(hi_ab, hi_ce))
        gscore.append(top1 + top2)
    best = jnp.zeros((1, rows), I32)
    bestv = gscore[0]
    for g in range(1, N_GROUPS):
        upd = gscore[g] > bestv
        best = jnp.where(upd, g, best)
        bestv = jnp.where(upd, gscore[g], bestv)

    def pick(a, i):
        out = row(a, i)
        for g in range(1, N_GROUPS):
            out = jnp.where(best == g, row(a, EPG * g + i), out)
        return out

    vsel = [pick(sel, i) for i in range(EPG)]
    vsc = [pick(score, i) for i in range(EPG)]
    i1 = jnp.zeros((1, rows), I32)
    v1 = vsel[0]
    for i in range(1, EPG):
        upd = vsel[i] > v1
        i1 = jnp.where(upd, i, i1)
        v1 = jnp.where(upd, vsel[i], v1)
    i2 = jnp.full((1, rows), -1, I32)
    v2 = jnp.full((1, rows), -jnp.inf, F32)
    for i in range(EPG):
        cand = jnp.where(i1 == i, -jnp.inf, vsel[i])
        upd = cand > v2
        i2 = jnp.where(upd, i, i2)
        v2 = jnp.where(upd, cand, v2)
    s1 = jnp.zeros((1, rows), F32)
    s2 = jnp.zeros((1, rows), F32)
    for i in range(EPG):
        s1 = jnp.where(i1 == i, vsc[i], s1)
        s2 = jnp.where(i2 == i, vsc[i], s2)
    den = s1 + s2
    w1 = s1 / den
    w2 = s2 / den
    first_lo = i1 < i2
    lo = jnp.minimum(i1, i2)
    hi = jnp.maximum(i1, i2)
    wlo = jnp.where(first_lo, w1, w2)
    whi = jnp.where(first_lo, w2, w1)
    pair = jnp.where(lo == 0, hi - 1, jnp.where(lo == 1, 6 - hi, 5))
    bkt_ref[...] = (best * 6 + pair).reshape(1, 1, rows)
    wa = jnp.where(pair == 5, whi, wlo)
    wb = jnp.where(pair == 5, wlo, whi)

    aux = jnp.concatenate([wa, wb, jnp.zeros((LANES - 2, rows), F32)], axis=0)
    hx_ref[:, d:d + LANES] = aux.T


def _out_route_call(mixed, wout, h, g1, gain, sc, sh, wrt, rb, *, bb, tt, name):
    b, t, m = mixed.shape
    d = h.shape[-1]
    rows = bb * tt
    nti = t // tt
    xspec = pl.BlockSpec((bb, tt, d), lambda bi, i: (bi, i, 0))
    mspec = pl.BlockSpec((bb, 1, d), lambda bi, i: (bi, 0, 0))
    return pl.pallas_call(
        _out_route_kernel,
        grid=(b // bb, nti),
        in_specs=[pl.BlockSpec((bb, tt, m), lambda bi, i: (bi, i, 0)),
                  pl.BlockSpec((m, d), lambda bi, i: (0, 0)),
                  xspec, mspec,
                  pl.BlockSpec((1, 1, d), lambda bi, i: (0, 0, 0)), mspec, mspec,
                  pl.BlockSpec((N_EXPERTS, d), lambda bi, i: (0, 0)),
                  pl.BlockSpec((N_EXPERTS, 1), lambda bi, i: (0, 0))],
        out_specs=[xspec,
                   pl.BlockSpec((rows, HX_W), lambda bi, i: (bi * nti + i, 0)),
                   pl.BlockSpec((1, 1, rows), lambda bi, i: (bi * nti + i, 0, 0))],
        out_shape=[jax.ShapeDtypeStruct((b, t, d), F32),
                   jax.ShapeDtypeStruct((b * t, HX_W), F32),
                   jax.ShapeDtypeStruct(((b // bb) * nti, 1, rows), I32)],
        compiler_params=_cparams(("parallel", "parallel")),
        name=name,
    )(mixed, wout, h, g1, gain.reshape(1, 1, d), sc, sh, wrt, rb)


def _dispatch_kernel(pos_ref, hxp_ref, hxs_ref, xs_in, xs_out, sem, *, np_tiles):
    del xs_in
    i = pl.program_id(0)
    rows = hxp_ref.shape[0]

    def scatter(hx_ref):
        def row_copy(r):
            p = pos_ref[i * rows + r]
            return pltpu.make_async_copy(hx_ref.at[pl.ds(r, 1), :], xs_out.at[pl.ds(p, 1), :], sem)

        for r in range(rows):
            row_copy(r).start(priority=r % 2)
        for r in range(rows):
            row_copy(r).wait()

    @pl.when(i < np_tiles)
    def _():
        scatter(hxp_ref)

    @pl.when(i >= np_tiles)
    def _():
        scatter(hxs_ref)


def _dispatch_call(pos, hx_p, hx_s, xs_buf, rows):
    np_tiles, ns_tiles = hx_p.shape[0] // rows, hx_s.shape[0] // rows
    return pl.pallas_call(
        functools.partial(_dispatch_kernel, np_tiles=np_tiles),
        grid_spec=pltpu.PrefetchScalarGridSpec(
            num_scalar_prefetch=1, grid=(np_tiles + ns_tiles,),
            in_specs=[pl.BlockSpec((rows, HX_W), lambda i, ps: (jnp.minimum(i, np_tiles - 1), 0)),
                      pl.BlockSpec((rows, HX_W), lambda i, ps: (jnp.maximum(i - np_tiles, 0), 0)),
                      pl.BlockSpec(memory_space=pl.ANY)],
            out_specs=pl.BlockSpec(memory_space=pl.ANY),
            scratch_shapes=[pltpu.SemaphoreType.DMA(())]),
        out_shape=jax.ShapeDtypeStruct(xs_buf.shape, F32),
        input_output_aliases={3: 0},
        compiler_params=_cparams(("arbitrary",)),
        name="dispatch",
    )(pos, hx_p, hx_s, xs_buf)


def _moe_kernel(exp_ref, chg_ref, nxt_ref, valid_ref, xs_ref, wg_hbm, wu_hbm, wd_hbm,
                ys_ref, wg_f, wu_f, wd_f, wgu_s, wd_s, sem, *, layer):
    t = pl.program_id(0)

    def fetch(slot, e):
        return (pltpu.make_async_copy(wg_hbm.at[layer, e], wg_f.at[slot], sem.at[slot]),
                pltpu.make_async_copy(wu_hbm.at[layer, e], wu_f.at[slot], sem.at[slot]),
                pltpu.make_async_copy(wd_hbm.at[layer, e], wd_f.at[slot], sem.at[slot]))

    def recast(slot):
        rc = 128
        for i in range(D_MODEL // rc):
            rows = slice(i * rc, (i + 1) * rc)
            wgu_s[rows, slot * D_FF:(slot + 1) * D_FF] = wg_f[slot, rows, :].astype(BF16)
            wgu_s[rows, (2 + slot) * D_FF:(3 + slot) * D_FF] = wu_f[slot, rows, :].astype(BF16)
        for i in range(D_FF // rc):
            wd_s[slot * D_FF + i * rc:slot * D_FF + (i + 1) * rc, :] = wd_f[slot, i * rc:(i + 1) * rc, :].astype(BF16)

    for slot in range(2):
        @pl.when(t == 0)
        def _():
            for cp in fetch(slot, exp_ref[slot, 0]):
                cp.start()

        @pl.when(chg_ref[slot, t] > 0)
        def _():
            for cp in fetch(slot, exp_ref[slot, t]):
                cp.wait()
            recast(slot)

            @pl.when(nxt_ref[slot, t] >= 0)
            def _():
                for cp in fetch(slot, nxt_ref[slot, t]):
                    cp.start()

    @pl.when(valid_ref[t] > 0)
    def _():
        x = xs_ref[:, 0:D_MODEL].astype(BF16)
        w_a = xs_ref[:, D_MODEL:D_MODEL + 1]
        w_b = xs_ref[:, D_MODEL + 1:D_MODEL + 2]
        gu = _dot(x, wgu_s[...])
        lane = lax.broadcasted_iota(I32, (MOE_TM, 2 * D_FF), 1)
        act = _silu(gu[:, :2 * D_FF]) * gu[:, 2 * D_FF:] * jnp.where(lane < D_FF, w_a, w_b)
        ys_ref[...] = _dot(act.astype(BF16), wd_s[...])

    @pl.when(valid_ref[t] == 0)
    def _():
        ys_ref[...] = jnp.zeros_like(ys_ref)


def _moe_call(experts, chg, nxt, valid, xs, wg, wu, wd, layer):
    p = xs.shape[0]
    n_tiles = p // MOE_TM
    return pl.pallas_call(
        functools.partial(_moe_kernel, layer=layer),
        grid_spec=pltpu.PrefetchScalarGridSpec(
            num_scalar_prefetch=4, grid=(n_tiles,),
            in_specs=[pl.BlockSpec((MOE_TM, HX_W), lambda t, ex, ch, nx, va: (t, 0)),
                      pl.BlockSpec(memory_space=pl.ANY), pl.BlockSpec(memory_space=pl.ANY),
                      pl.BlockSpec(memory_space=pl.ANY)],
            out_specs=pl.BlockSpec((MOE_TM, D_MODEL), lambda t, ex, ch, nx, va: (t, 0)),
            scratch_shapes=[pltpu.VMEM((2, D_MODEL, D_FF), F32), pltpu.VMEM((2, D_MODEL, D_FF), F32),
                            pltpu.VMEM((2, D_FF, D_MODEL), F32),
                            pltpu.VMEM((D_MODEL, 4 * D_FF), BF16), pltpu.VMEM((2 * D_FF, D_MODEL), BF16),
                            pltpu.SemaphoreType.DMA((2,))]),
        out_shape=jax.ShapeDtypeStruct((p, D_MODEL), F32),
        compiler_params=_cparams(("arbitrary",)),
        name="moe",
    )(experts, chg, nxt, valid, xs, wg, wu, wd)


_PAIRS = [(0, 1), (0, 2), (0, 3), (1, 3), (1, 2), (3, 2)]
_EA_TABLE = np.array([g * EPG + p[0] for g in range(N_GROUPS) for p in _PAIRS], np.int32)
_EB_TABLE = np.array([g * EPG + p[1] for g in range(N_GROUPS) for p in _PAIRS], np.int32)


def _moe(hx_p, hx_s, bkt, xs_buf, wg, wu, wd, layer):
    n = hx_p.shape[0] + hx_s.shape[0]
    tm = MOE_TM
    ar = jnp.arange(N_BUCKETS, dtype=I32)
    blk = LANES
    nb = -(-n // blk)
    bkt_p = jnp.pad(bkt, (0, nb * blk - n), constant_values=N_BUCKETS)
    oh = (ar[:, None] == bkt_p[None, :]).astype(F32)
    triu = jnp.triu(jnp.ones((blk, blk), F32))
    inner = jnp.einsum('kbj,ji->kbi', oh.reshape(N_BUCKETS, nb, blk), triu)
    tot = inner[:, :, -1]
    outer = jnp.cumsum(tot, axis=1) - tot
    cs = (inner + outer[:, :, None]).reshape(N_BUCKETS, nb * blk)
    counts = (outer[:, -1] + tot[:, -1]).astype(I32)
    padded = ((counts + tm - 1) // tm) * tm
    ends = jnp.cumsum(padded)
    offs = ends - padded
    pos = jnp.sum(oh * (cs - 1.0 + offs.astype(F32)[:, None]), axis=0).astype(I32)[:n]
    n_tiles = -(-n // tm) + N_BUCKETS
    p = n_tiles * tm
    if xs_buf is None:
        xs_buf = jnp.zeros((p, HX_W), F32)
    tstart = jnp.arange(n_tiles, dtype=I32) * tm
    tb = jnp.sum((ends[None, :] <= tstart[:, None]).astype(I32), axis=1)
    n_used = ends[-1] // tm
    valid = (jnp.arange(n_tiles, dtype=I32) < n_used).astype(I32)
    last_b = tb[jnp.maximum(n_used - 1, 0)]
    tb = jnp.clip(jnp.where(valid > 0, tb, last_b), 0, N_BUCKETS - 1)
    ea = jnp.asarray(_EA_TABLE)[tb]
    eb = jnp.asarray(_EB_TABLE)[tb]
    experts = jnp.stack([ea, eb])
    chg = jnp.concatenate([jnp.ones((2, 1), I32), (experts[:, 1:] != experts[:, :-1]).astype(I32)], axis=1)
    tidx = jnp.arange(n_tiles, dtype=I32)
    chg_at = jnp.where(chg > 0, tidx[None, :], n_tiles)
    nxt_at = jnp.concatenate([lax.cummin(chg_at, axis=1, reverse=True)[:, 1:], jnp.full((2, 1), n_tiles, I32)], axis=1)
    nxt = jnp.where(nxt_at < n_tiles,
                    jnp.take_along_axis(experts, jnp.minimum(nxt_at, n_tiles - 1), axis=1), -1)
    xs = _dispatch_call(pos, hx_p, hx_s, xs_buf, tm)
    ys = _moe_call(experts, chg, nxt, valid, xs, wg, wu, wd, layer)
    return ys.at[pos].get(mode="promise_in_bounds", unique_indices=True), xs


def _final_kernel(h_ref, y_ref, g_ref, gain_ref, o_ref):
    bb, tt, d = h_ref.shape
    h = h_ref[...] + g_ref[...] * y_ref[...].reshape(bb, tt, d)
    o_ref[...] = _rms(h) * gain_ref[...]


def _final_call(h, y2d, off, g2, gain, *, bb, tt, name):
    b, t, d = h.shape
    nti = t // tt
    rows = bb * tt
    xspec = pl.BlockSpec((bb, tt, d), lambda bi, i: (bi, i, 0))
    return pl.pallas_call(
        _final_kernel,
        grid=(b // bb, nti),
        in_specs=[xspec, pl.BlockSpec((rows, d), lambda bi, i: (off + bi * nti + i, 0)),
                  pl.BlockSpec((bb, 1, d), lambda bi, i: (bi, 0, 0)),
                  pl.BlockSpec((1, 1, d), lambda bi, i: (0, 0, 0))],
        out_specs=xspec,
        out_shape=jax.ShapeDtypeStruct((b, t, d), F32),
        compiler_params=_cparams(("parallel", "parallel")),
        name=name,
    )(h, y2d, g2, gain.reshape(1, 1, d))


def _group_cfg(b, t):
    if t >= 512:
        return dict(bb=1, tt=512, tt_out=1024, L=CHUNK, CB=4, Lc=4 * CHUNK, CBc=1)
    return dict(bb=b, tt=t, tt_out=t, L=min(CHUNK, t), CB=1, Lc=min(CHUNK, t), CBc=1)


def _rope_tables(pos0, t):
    half = DK_C // 2
    inv = np.power(np.float64(ROPE_BASE), -np.linspace(0.0, 1.0, half, dtype=np.float64))
    ang = (np.float64(pos0) + np.arange(t, dtype=np.float64))[:, None] * inv[None, :]
    return jnp.asarray(np.cos(ang), F32), jnp.asarray(np.sin(ang), F32)


def kernel(x_prompt, x_sample, c_prompt, c_sample, state_a_conv, state_a_rec, state_b_c, state_b_n, state_b_m,
           state_c_rec, w_mod, b_mod, norm_mix, norm_ffn, w_in_ab, conv_a, a_log, dt_bias, norm_a, gate_bias_b,
           norm_b, w_out_ab, w_in_c, w_out_c, w_router, router_bias, w_gate, w_up, w_down, norm_final):
    d = D_MODEL
    bp, tp, _ = x_prompt.shape
    bs, ts, _ = x_sample.shape
    n_ab, n_c = w_in_ab.shape[0], w_in_c.shape[0]

    mod_all = _mod_call(jnp.concatenate([c_prompt, c_sample], axis=0), w_mod, b_mod)

    o = np.cumsum((0,) + AB_SIZES)
    w_ab_main = jnp.concatenate([w_in_ab[:, :, o[0]:o[1]].astype(BF16), w_in_ab[:, :, o[3]:o[7]].astype(BF16),
                                 w_in_ab[:, :, o[8]:o[9]].astype(BF16)], axis=-1)
    w_ab_gate = jnp.concatenate([w_in_ab[:, :, o[1]:o[3]], w_in_ab[:, :, o[7]:o[8]]], axis=-1)
    w_ab_gate_pad = jnp.pad(w_ab_gate, ((0, 0), (0, 0), (0, LANES - N_GATE_COLS)))
    w_c = w_in_c.astype(BF16)
    w_out_ab_b = w_out_ab.astype(BF16)
    w_out_c_b = w_out_c.astype(BF16)
    wrt = w_router.T
    rb = router_bias.reshape(N_EXPERTS, 1)
    zeros4 = jnp.zeros((n_ab, H_A), F32)
    gpar_row0 = jnp.concatenate([a_log, jnp.zeros((n_ab, LANES - H_A), F32)], axis=1)
    gpar_row1 = jnp.concatenate([dt_bias, zeros4, gate_bias_b, jnp.zeros((n_ab, LANES - 16), F32)], axis=1)
    gpar = jnp.concatenate([gpar_row0[:, None], gpar_row1[:, None], jnp.zeros((n_ab, 6, LANES), F32)], axis=1)
    gpart = jnp.swapaxes(jnp.concatenate([gpar[:, :2, :N_GATE_COLS], jnp.zeros((n_ab, LANES - 2, N_GATE_COLS), F32)],
                                         axis=1), 1, 2)

    groups = []
    zf = lambda *s: jnp.zeros(s, F32)
    groups.append(dict(
        x=x_prompt, b=bp, t=tp, pos0=0.0, mod=mod_all[:, :bp], row0=0,
        conv=zf(n_ab, bp, CONV_W - 1, QKV_A), sa=zf(n_ab, bp, H_A, DK_A, DV_A), cb=zf(n_ab, bp, H_B, DQK_B, DV_B),
        nb=zf(n_ab, bp, H_B, DQK_B), mb=zf(n_ab, bp, H_B), sc=zf(n_c, bp, H_C, DK_C, DV_C)))
    groups.append(dict(
        x=x_sample, b=bs, t=ts, pos0=float(PAST_LEN), mod=mod_all[:, bp:], row0=bp * tp,
        conv=state_a_conv, sa=state_a_rec, cb=state_b_c, nb=state_b_n, mb=state_b_m, sc=state_c_rec))
    for g in groups:
        g.update(_group_cfg(g["b"], g["t"]))
        g["h"] = g["x"]
        g["prev"] = None
        g["new_ab"], g["new_c"] = [], []
        g["cos"], g["sin"] = _rope_tables(g["pos0"], g["t"])
        g["roff"] = g["row0"] // (g["bb"] * g["tt"])

    xs_buf = None
    for layer in range(DEPTH):
        li = layer // 2
        bkts, hxs = [], []
        for gi, g in enumerate(groups):
            b, t, bb, tt = g["b"], g["t"], g["bb"], g["tt"]
            mods = [m.reshape(b, 1, d) for m in jnp.split(g["mod"][layer], N_MOD, axis=-1)]
            sh1, sc1, g1, sh2, sc2, g2 = mods
            tag = f"l{layer}g{gi}"
            if layer % 2 == 0:
                outs = _ln_mm_call(g["h"], g["prev"], norm_mix[layer], sc1, sh1, w_ab_main[li],
                                   (w_ab_gate_pad[li], g["conv"][li], conv_a[li]), bb=bb, tt=tt, tn=512,
                                   out_dtype=F32, name="ln_ab_" + tag)
                if g["prev"] is not None:
                    g["h"], outs = outs[0], outs[1:]
                proj, gates, gates_t, qkv, conv_n = outs
                gates_t = gates_t.reshape(N_GATE_COLS, b, t).transpose(1, 0, 2)
                mixed, sa_n, cb_n, nb_n, mb_n = _mixer_ab_call(
                    qkv, proj, gates, gates_t, g["sa"][li], g["cb"][li],
                    g["nb"][li].reshape(b, H_B, 1, DQK_B), g["mb"][li].reshape(b, H_B, 1, 1),
                    gpar[li], gpart[li], norm_a[li].reshape(1, DV_A), norm_b[li].reshape(1, DV_B),
                    L=g["L"], CB=g["CB"], name="mixer_ab_" + tag)
                g["new_ab"].append((conv_n, sa_n, cb_n, nb_n.reshape(b, H_B, DQK_B), mb_n.reshape(b, H_B)))
                wout = w_out_ab_b[li]
            else:
                outs = _ln_mm_call(g["h"], g["prev"], norm_mix[layer], sc1, sh1, w_c[li], None,
                                   bb=bb, tt=tt, tn=512, out_dtype=BF16, name="ln_c_" + tag)
                if g["prev"] is not None:
                    g["h"], outs = outs[0], outs[1:]
                (proj,) = outs
                mixed, sc_n = _mixer_c_call(proj, g["cos"], g["sin"], g["sc"][li], L=g["Lc"], CB=g["CBc"],
                                            name="mixer_c_" + tag)
                g["new_c"].append(sc_n)
                wout = w_out_c_b[li]
            tto = g["tt_out"]
            g["h"], hx, bkt = _out_route_call(mixed, wout, g["h"], g1, norm_ffn[layer], sc2, sh2, wrt, rb,
                                              bb=bb, tt=tto, name="out_route_" + tag)
            bkts.append(bkt.reshape(-1))
            hxs.append(hx)
            g["g2"] = g2
        y_moe, xs_buf = _moe(hxs[0], hxs[1], jnp.concatenate(bkts), xs_buf, w_gate, w_up, w_down, layer)
        for g in groups:
            g["prev"] = (y_moe, g["roff"], g["g2"])

    outs = []
    for gi, g in enumerate(groups):
        y2d, off, g2 = g["prev"]
        y = _final_call(g["h"], y2d, off, g2, norm_final, bb=g["bb"], tt=g["tt"], name=f"final_g{gi}")
        ab = [jnp.stack(s) for s in zip(*g["new_ab"])]
        outs.append((y, ab[0], ab[1], ab[2], ab[3], ab[4], jnp.stack(g["new_c"])))
    p, s = outs
    return (p[0], s[0], p[1], p[2], p[3], p[4], p[5], p[6], s[1], s[2], s[3], s[4], s[5], s[6])
```

```python
import functools

import numpy as np
import jax
import jax.numpy as jnp
from jax import lax
from jax.experimental import pallas as pl
from jax.experimental.pallas import tpu as pltpu

F32 = jnp.float32
BF16 = jnp.bfloat16
I32 = jnp.int32

D_MODEL = 1024
DEPTH = 2
CHUNK = 64
H_A, DK_A, DV_A, CONV_W = 4, 128, 128, 4
QKV_A = H_A * (2 * DK_A + DV_A)
H_B, DQK_B, DV_B = 4, 64, 128
H_C, DK_C, DV_C = 4, 256, 512
ROPE_BASE = 10000.0
PAST_LEN = 4096
AB_SIZES = (QKV_A, H_A, H_A, H_A * DV_A, H_B * DQK_B, H_B * DQK_B, H_B * DV_B, 2 * H_B, H_B * DV_B)
MIX_AB = H_A * DV_A + H_B * DV_B
MIX_C = H_C * DV_C
N_EXPERTS, N_GROUPS, EPG, D_FF = 16, 4, 4, 512
N_MOD = 6
EPS = 1e-6

LANES = 128
N_GATE_COLS = 16
HX_W = D_MODEL + LANES
N_BUCKETS = N_GROUPS * 6
MOE_TM = 256
VMEM_LIMIT = 48 * 1024 * 1024

NN = ((1,), (0,))
NT = ((1,), (1,))
TN = ((0,), (0,))


def _dot(a, b, dims=NN):
    return lax.dot_general(a, b, (dims, ((), ())), preferred_element_type=F32)


def _mmb(a, b, dims=NN):
    return _dot(a.astype(BF16), b.astype(BF16), dims)


def _split2(x):
    hi = x.astype(BF16)
    lo = (x - hi.astype(F32)).astype(BF16)
    return hi, lo


def _split3(x):
    hi = x.astype(BF16)
    r = x - hi.astype(F32)
    mid = r.astype(BF16)
    lo = (r - mid.astype(F32)).astype(BF16)
    return hi, mid, lo


def _mm3(a, b, dims=NN):
    ah, al = _split2(a)
    bh, bl = _split2(b)
    return _dot(ah, bh, dims) + (_dot(ah, bl, dims) + _dot(al, bh, dims))


def _mm_mask_l(mask_bf16, x, dims=NN):
    h, m, l = _split3(x)
    return _dot(mask_bf16, h, dims) + (_dot(mask_bf16, m, dims) + _dot(mask_bf16, l, dims))


def _mm_mask_r(x, mask_bf16):
    h, m, l = _split3(x)
    return _dot(h, mask_bf16) + (_dot(m, mask_bf16) + _dot(l, mask_bf16))


def _sigmoid(x):
    return 1.0 / (1.0 + jnp.exp(-x))


def _silu(x):
    return x * _sigmoid(x)


def _softplus(x):
    return jnp.maximum(x, 0.0) + jnp.log(1.0 + jnp.exp(-jnp.abs(x)))


def _rms(x, eps=EPS):
    return x * lax.rsqrt(jnp.mean(x * x, axis=-1, keepdims=True) + eps)


def _cparams(sem):
    return pltpu.CompilerParams(dimension_semantics=sem, vmem_limit_bytes=VMEM_LIMIT)


def _mod_kernel(c_ref, w_ref, b_ref, o_ref):
    c = c_ref[...]
    o_ref[0] = _mm3(_silu(c), w_ref[0]) + b_ref[0]


def _mod_call(c_all, w_mod, b_mod):
    bt = c_all.shape[0]
    e = w_mod.shape[-1]
    tn = 1024
    return pl.pallas_call(
        _mod_kernel,
        grid=(DEPTH, e // tn),
        in_specs=[pl.BlockSpec((bt, D_MODEL), lambda l, j: (0, 0)),
                  pl.BlockSpec((1, D_MODEL, tn), lambda l, j: (l, 0, j)),
                  pl.BlockSpec((1, 1, tn), lambda l, j: (l, 0, j))],
        out_specs=pl.BlockSpec((1, bt, tn), lambda l, j: (l, 0, j)),
        out_shape=jax.ShapeDtypeStruct((DEPTH, bt, e), F32),
        compiler_params=_cparams(("parallel", "parallel")),
        name="mod",
    )(c_all, w_mod, b_mod.reshape(DEPTH, 1, e))


def _ln_mm_kernel(*refs, has_prev, has_gates, tn, zero_rows):
    it = iter(refs)
    x_ref = next(it)
    if has_prev:
        yp_ref, gp_ref = next(it), next(it)
    gain_ref, sc_ref, sh_ref, w_ref = next(it), next(it), next(it), next(it)
    if has_gates:
        wg_ref, conv0_ref, convw_ref = next(it), next(it), next(it)
    if has_prev:
        h_ref = next(it)
    proj_ref = next(it)
    if has_gates:
        g_ref, gt_ref, qkv_ref, conv_o = next(it), next(it), next(it), next(it)
    if zero_rows:
        zeros_ref = next(it)
        zeros_ref[...] = jnp.zeros_like(zeros_ref)
    hn_s = next(it)
    if has_gates:
        tail_s = next(it)

        @pl.when(pl.program_id(1) == 0)
        def _():
            tail_s[...] = conv0_ref[...]

    bb, tt, d = x_ref.shape
    rows = bb * tt
    x = x_ref[...]
    if has_prev:
        x = x + gp_ref[...] * yp_ref[...].reshape(bb, tt, d)
        h_ref[...] = x
    hn = _rms(x) * gain_ref[...]
    hn = hn * (1.0 + sc_ref[...]) + sh_ref[...]
    hn2 = hn.reshape(rows, d)
    hn_s[...] = hn2.astype(BF16)
    if has_gates:
        gates = _mm3(hn2, wg_ref[...])
        g_ref[...] = gates.reshape(bb, tt, LANES)
        gt_ref[...] = gates.T[0:N_GATE_COLS, :]
    n_conv = QKV_A // tn if has_gates else 0
    n_tiles = w_ref.shape[-1] // tn

    def project(j):
        return _dot(hn_s[...], w_ref[:, j * tn:(j + 1) * tn]).reshape(bb, tt, tn)

    def finish(j, y):
        cols = slice(j * tn, (j + 1) * tn)
        if j < n_conv:
            prev = tail_s[:, :, cols]
            acc = convw_ref[CONV_W - 1:CONV_W, cols] * y
            for lag in range(1, CONV_W):
                shifted = jnp.concatenate([prev[:, CONV_W - 1 - lag:, :], y[:, :tt - lag, :]], axis=1)
                acc = acc + convw_ref[CONV_W - 1 - lag:CONV_W - lag, cols] * shifted
            qkv_ref[:, :, cols] = _silu(acc).astype(qkv_ref.dtype)
            tail = y[:, tt - (CONV_W - 1):tt, :]
            tail_s[:, :, cols] = tail
            conv_o[:, :, cols] = tail
        else:
            pcols = slice((j - n_conv) * tn, (j - n_conv + 1) * tn)
            proj_ref[:, :, pcols] = y.astype(proj_ref.dtype)

    y = project(0)
    for j in range(n_tiles):
        y_next = project(j + 1) if j + 1 < n_tiles else None
        finish(j, y)
        y = y_next


def _ln_mm_call(x, prev, gain, sc, sh, w, gates_w, *, bb, tt, tn, out_dtype, name, zero_rows=0):
    b, t, d = x.shape
    e = w.shape[1]
    rows = bb * tt
    nti = t // tt
    grid = (b // bb, nti)
    has_prev = prev is not None
    has_gates = gates_w is not None
    xspec = pl.BlockSpec((bb, tt, d), lambda bi, i: (bi, i, 0))
    mspec = pl.BlockSpec((bb, 1, d), lambda bi, i: (bi, 0, 0))
    in_specs, args = [xspec], [x]
    if has_prev:
        y2d, off, gp = prev
        in_specs += [pl.BlockSpec((rows, d), lambda bi, i: (off + bi * nti + i, 0)), mspec]
        args += [y2d, gp]
    in_specs += [pl.BlockSpec((1, 1, d), lambda bi, i: (0, 0, 0)), mspec, mspec,
                 pl.BlockSpec((d, e), lambda bi, i: (0, 0), pipeline_mode=pl.Buffered(1))]
    args += [gain.reshape(1, 1, d), sc, sh, w]
    tail_spec = pl.BlockSpec((bb, CONV_W - 1, QKV_A), lambda bi, i: (bi, 0, 0))
    if has_gates:
        wg, conv0, convw = gates_w
        in_specs += [pl.BlockSpec((d, LANES), lambda bi, i: (0, 0)), tail_spec,
                     pl.BlockSpec((CONV_W, QKV_A), lambda bi, i: (0, 0))]
        args += [wg, conv0, convw]
    e_proj = e - QKV_A if has_gates else e
    out_specs, out_shape = [], []
    if has_prev:
        out_specs.append(xspec)
        out_shape.append(jax.ShapeDtypeStruct((b, t, d), F32))
    out_specs.append(pl.BlockSpec((bb, tt, e_proj), lambda bi, i: (bi, i, 0)))
    out_shape.append(jax.ShapeDtypeStruct((b, t, e_proj), out_dtype))
    scratch = [pltpu.VMEM((rows, d), BF16)]
    if has_gates:
        out_specs += [pl.BlockSpec((bb, tt, LANES), lambda bi, i: (bi, i, 0)),
                      pl.BlockSpec((N_GATE_COLS, rows), lambda bi, i: (0, bi * nti + i)),
                      pl.BlockSpec((bb, tt, QKV_A), lambda bi, i: (bi, i, 0)), tail_spec]
        out_shape += [jax.ShapeDtypeStruct((b, t, LANES), F32),
                      jax.ShapeDtypeStruct((N_GATE_COLS, b * t), F32),
                      jax.ShapeDtypeStruct((b, t, QKV_A), BF16),
                      jax.ShapeDtypeStruct((b, CONV_W - 1, QKV_A), F32)]
        scratch.append(pltpu.VMEM((bb, CONV_W - 1, QKV_A), F32))
    if zero_rows:
        n_steps = grid[0] * grid[1]
        assert zero_rows % (8 * n_steps) == 0
        out_specs.append(pl.BlockSpec((zero_rows // n_steps, HX_W), lambda bi, i: (bi * nti + i, 0)))
        out_shape.append(jax.ShapeDtypeStruct((zero_rows, HX_W), F32))
    return pl.pallas_call(
        functools.partial(_ln_mm_kernel, has_prev=has_prev, has_gates=has_gates, tn=tn, zero_rows=zero_rows),
        grid=grid, in_specs=in_specs, out_specs=out_specs, out_shape=out_shape,
        scratch_shapes=scratch,
        compiler_params=_cparams(("parallel", "arbitrary" if has_gates else "parallel")),
        name=name,
    )(*args)


def _tri_inv_all(a_list, eye, length, mm):
    xs = [-a for a in a_list]
    ps = [eye + x for x in xs]
    n = 2
    while n < length:
        xs = [mm(x, x) for x in xs]
        ps = [p + mm(p, x) for p, x in zip(ps, xs)]
        n *= 2
    return ps


def _mixer_ab_kernel(qkv_ref, z_ref, qkb_ref, vb_ref, ob_ref, g_ref, gt_ref,
                     sa0_ref, cb0_ref, nb0_ref, mb0_ref,
                     gpar_ref, gpart_ref, na_ref, nbn_ref,
                     mixed_ref, sa_o, cb_o, nb_o, mb_o, *, L, CB):
    @pl.when(pl.program_id(1) == 0)
    def _():
        sa_o[...] = sa0_ref[...]
        cb_o[...] = cb0_ref[...]
        nb_o[...] = nb0_ref[...]
        mb_o[...] = mb0_ref[...]

    ii = lax.broadcasted_iota(I32, (L, L), 0)
    jj = lax.broadcasted_iota(I32, (L, L), 1)
    tri = ii >= jj
    strict = ii > jj
    eye = (ii == jj).astype(F32)
    tril_b = tri.astype(BF16)
    triu_b = (ii <= jj).astype(BF16)
    rowi = lax.broadcasted_iota(I32, (N_GATE_COLS, L), 0)
    gpar = gpar_ref[...]
    gpart = gpart_ref[...]
    neat = -jnp.exp(gpart[:, 0:1])
    norm_a = na_ref[...]
    norm_b = nbn_ref[...]

    mm = _mmb

    def qkv_tile(r0, col0):
        return qkv_ref[0, r0:r0 + L, col0:col0 + DK_A].astype(F32)

    chunks = range(CB)
    items = [(c, h) for c in chunks for h in range(H_A)]
    rs = [slice(c * L, (c + 1) * L) for c in chunks]

    pre = [g_ref[0, r, :] + gpar[1:2, :] for r in rs]
    pret = [gt_ref[0, :, r] + gpart[:, 1:2] for r in rs]
    csrct = [jnp.where(rowi < 4, neat * _softplus(p), jnp.where(rowi >= 12, -_softplus(-p), 0.0)) for p in pret]
    gcumt = [_mm_mask_r(x, triu_b) for x in csrct]
    gcum = [_mm_mask_l(tril_b, x, NT) for x in csrct]
    beta_all = [_sigmoid(p) for p in pre]

    q = [qkv_tile(c * L, h * DK_A) for c, h in items]
    k = [qkv_tile(c * L, H_A * DK_A + h * DK_A) for c, h in items]
    v = [qkv_tile(c * L, 2 * H_A * DK_A + h * DV_A) for c, h in items]
    q = [x * lax.rsqrt(jnp.sum(x * x, axis=-1, keepdims=True) + EPS) * (DK_A ** -0.5) for x in q]
    k = [x * lax.rsqrt(jnp.sum(x * x, axis=-1, keepdims=True) + EPS) for x in k]
    gc_c = [gcum[c][:, h:h + 1] for c, h in items]
    gc_r = [gcumt[c][h:h + 1, :] for c, h in items]
    decay = [jnp.where(tri, jnp.exp(jnp.where(tri, a - b, 0.0)), 0.0) for a, b in zip(gc_c, gc_r)]
    beta_c = [beta_all[c][:, 4 + h:5 + h] for c, h in items]
    kb = [x * b for x, b in zip(k, beta_c)]
    a_list = [jnp.where(strict, mm(x, y, NT) * d, 0.0) for x, y, d in zip(kb, k, decay)]
    attn = [jnp.where(tri, mm(x, y, NT) * d, 0.0) for x, y, d in zip(q, k, decay)]
    eg = [jnp.exp(x) for x in gc_c]
    gl = [x[L - 1:L, :] for x in gc_c]
    rhs = [jnp.concatenate([x * e, y * b], axis=-1) for x, e, y, b in zip(kb, eg, v, beta_c)]
    qeg = [x * e for x, e in zip(q, eg)]
    kdec = [x * jnp.exp(g - gc) for x, g, gc in zip(k, gl, gc_c)]
    sdec = [jnp.exp(g) for g in gl]
    a_list = [a.astype(BF16) for a in a_list]
    tinv = _tri_inv_all([a.astype(F32) for a in a_list], eye, L, mm)
    sol = [mm(t, r) for t, r in zip(tinv, rhs)]
    sol_hl = [_split2(s) for s in sol]
    resid = [r - (s + (_dot(a, hi) + _dot(a, lo))) for r, s, a, (hi, lo) in zip(rhs, sol, a_list, sol_hl)]
    sol = [s + mm(t, r) for s, t, r in zip(sol, tinv, resid)]

    heads = range(H_B)
    qb = [qkb_ref[0, rs[c], h * DQK_B:(h + 1) * DQK_B] * (DQK_B ** -0.5) for c, h in items]
    kbb = [qkb_ref[0, rs[c], H_B * DQK_B + h * DQK_B:H_B * DQK_B + (h + 1) * DQK_B] for c, h in items]
    b_c = [gcum[c][:, 12 + h:13 + h] for c, h in items]
    dm = [jnp.where(tri, gcum[c][:, 12 + h:13 + h] - gcumt[c][12 + h:13 + h, :] + pret[c][8 + h:9 + h, :], -jnp.inf)
          for c, h in items]
    dmax = [jnp.max(x, axis=-1, keepdims=True) for x in dm]
    qkm = [mm(x, y, NT) for x, y in zip(qb, kbb)]
    li_c = [pre[c][:, 8 + h:9 + h] for c, h in items]

    for c in chunks:
        ia = [c * H_A + h for h in heads]
        s = [sa_o[0, h] for h in heads]
        v_new = [sol[i][:, DK_A:] - mm(sol[i][:, :DK_A], s[h]) for h, i in enumerate(ia)]
        m_prev = [mb_o[0, h] for h in heads]
        a0 = [b_c[i] + m_prev[h] for h, i in enumerate(ia)]
        m_t = [jnp.maximum(a0[h], dmax[i]) for h, i in enumerate(ia)]
        w0 = [jnp.exp(a0[h] - m_t[h]) for h in heads]
        sm = [qkm[i] * jnp.exp(dm[i] - m_t[h]) for h, i in enumerate(ia)]
        cst = [cb_o[0, h] for h in heads]
        nst = [nb_o[0, h] for h in heads]
        vb = [vb_ref[0, rs[c], h * DV_B:(h + 1) * DV_B] for h in heads]
        o_a = [mm(qeg[i], s[h]) + mm(attn[i], v_new[h]) for h, i in enumerate(ia)]
        for h, i in enumerate(ia):
            sa_o[0, h] = s[h] * sdec[i] + mm(kdec[i], v_new[h], TN)
        num = [w0[h] * mm(qb[i], cst[h]) + mm(sm[h], vb[h]) for h, i in enumerate(ia)]
        den = [w0[h] * jnp.sum(qb[i] * nst[h], axis=-1, keepdims=True) + jnp.sum(sm[h], axis=-1, keepdims=True)
               for h, i in enumerate(ia)]
        hh = [num[h] / jnp.maximum(jnp.abs(den[h]), jnp.exp(-m_t[h])) for h in heads]
        m_new = [m_t[h][L - 1:L, :] for h in heads]
        kw = [kbb[i] * jnp.exp(b_c[i][L - 1:L, :] - b_c[i] + li_c[i] - m_new[h]) for h, i in enumerate(ia)]
        for h in heads:
            w0l = w0[h][L - 1:L, :]
            cb_o[0, h] = w0l * cst[h] + mm(kw[h], vb[h], TN)
            nb_o[0, h] = w0l * nst[h] + jnp.sum(kw[h], axis=0, keepdims=True)
            mb_o[0, h] = m_new[h]
        for h in heads:
            zg = z_ref[0, rs[c], h * DV_A:(h + 1) * DV_A]
            mixed_ref[0, rs[c], h * DV_A:(h + 1) * DV_A] = (_rms(o_a[h]) * norm_a * _silu(zg)).astype(mixed_ref.dtype)
            og = ob_ref[0, rs[c], h * DV_B:(h + 1) * DV_B]
            c0 = H_A * DV_A + h * DV_B
            mixed_ref[0, rs[c], c0:c0 + DV_B] = (_rms(hh[h]) * norm_b * _sigmoid(og)).astype(mixed_ref.dtype)


def _mixer_ab_call(qkv, proj, gates, gates_t, sa0, cb0, nb0, mb0, gpar, gpart, norm_a, norm_b, *, L, CB, name):
    b, t, _ = proj.shape
    tb = L * CB
    nt = t // tb

    def col(width, idx):
        return pl.BlockSpec((1, tb, width), lambda bi, i: (bi, i, idx))

    def const(shape):
        return pl.BlockSpec(shape, lambda bi, i: (0,) * len(shape))

    def state(shape):
        return pl.BlockSpec((1,) + shape, lambda bi, i: (bi,) + (0,) * len(shape))

    in_specs = [col(QKV_A, 0), col(512, 0), col(512, 1), col(512, 2), col(512, 3),
                col(LANES, 0), pl.BlockSpec((1, N_GATE_COLS, tb), lambda bi, i: (bi, 0, i)),
                state((H_A, DK_A, DV_A)), state((H_B, DQK_B, DV_B)),
                state((H_B, 1, DQK_B)), state((H_B, 1, 1)),
                const((8, LANES)), const((N_GATE_COLS, LANES)),
                const((1, DV_A)), const((1, DV_B))]
    out_specs = [pl.BlockSpec((1, tb, MIX_AB), lambda bi, i: (bi, i, 0)),
                 state((H_A, DK_A, DV_A)), state((H_B, DQK_B, DV_B)),
                 state((H_B, 1, DQK_B)), state((H_B, 1, 1))]
    out_shape = [jax.ShapeDtypeStruct((b, t, MIX_AB), BF16),
                 jax.ShapeDtypeStruct((b, H_A, DK_A, DV_A), F32),
                 jax.ShapeDtypeStruct((b, H_B, DQK_B, DV_B), F32),
                 jax.ShapeDtypeStruct((b, H_B, 1, DQK_B), F32),
                 jax.ShapeDtypeStruct((b, H_B, 1, 1), F32)]
    return pl.pallas_call(
        functools.partial(_mixer_ab_kernel, L=L, CB=CB),
        grid=(b, nt), in_specs=in_specs, out_specs=out_specs, out_shape=out_shape,
        compiler_params=_cparams(("parallel", "arbitrary")),
        name=name,
    )(qkv, proj, proj, proj, proj, gates, gates_t, sa0, cb0, nb0, mb0, gpar, gpart, norm_a, norm_b)


def _mixer_c_kernel(q_ref, k_ref, v_ref, g_ref, cos_ref, sin_ref, s0_ref, mixed_ref, s_o, intra_s, *, L, CB):
    log_gamma = [float(np.log1p(-np.exp2(-5.0 - h))) for h in range(H_C)]

    @pl.when(pl.program_id(1) == 0)
    def _():
        s_o[...] = s0_ref[...]
        ii = lax.broadcasted_iota(I32, (L, L), 0)
        jj = lax.broadcasted_iota(I32, (L, L), 1)
        rel = (ii - jj).astype(F32)
        for h in range(H_C):
            intra_s[h] = jnp.where(rel >= 0, jnp.exp(log_gamma[h] * jnp.maximum(rel, 0.0)), 0.0)

    idx = lax.broadcasted_iota(I32, (L, 1), 0).astype(F32)
    half = DK_C // 2

    def rope(x, cos, sin):
        x1, x2 = x[:, :half], x[:, half:]
        return jnp.concatenate([x1 * cos - x2 * sin, x1 * sin + x2 * cos], axis=-1)

    for c in range(CB):
        r0 = c * L
        cos = cos_ref[r0:r0 + L, :]
        sin = sin_ref[r0:r0 + L, :]
        for h in range(H_C):
            lg = log_gamma[h]
            intra = intra_s[h]
            q_decay = jnp.exp(lg * (idx + 1.0))
            k_decay = jnp.exp(lg * (L - 1.0 - idx))
            s_decay = float(np.exp(lg * L))
            q = rope(q_ref[0, r0:r0 + L, h * DK_C:(h + 1) * DK_C].astype(F32), cos, sin)
            k = rope(k_ref[0, r0:r0 + L, h * DK_C:(h + 1) * DK_C].astype(F32), cos, sin) * (DK_C ** -0.5)
            v = v_ref[0, r0:r0 + L, h * DV_C:(h + 1) * DV_C]
            s = s_o[0, h]
            o = _mmb(_mmb(q, k, NT) * intra, v) + _mmb(q * q_decay, s)
            s_o[0, h] = s_decay * s + _mmb(k * k_decay, v, TN)
            gg = g_ref[0, r0:r0 + L, h * DV_C:(h + 1) * DV_C].astype(F32)
            o = _rms(o) * _silu(gg)
            mixed_ref[0, r0:r0 + L, h * DV_C:(h + 1) * DV_C] = o.astype(mixed_ref.dtype)


def _mixer_c_call(proj, cos, sin, s0, *, L, CB, name):
    b, t, _ = proj.shape
    tb = L * CB
    nt = t // tb
    qk_w, v_w = H_C * DK_C, H_C * DV_C
    sspec = pl.BlockSpec((1, H_C, DK_C, DV_C), lambda bi, i: (bi, 0, 0, 0))
    tspec = pl.BlockSpec((tb, DK_C // 2), lambda bi, i: (i, 0))
    return pl.pallas_call(
        functools.partial(_mixer_c_kernel, L=L, CB=CB),
        grid=(b, nt),
        in_specs=[pl.BlockSpec((1, tb, qk_w), lambda bi, i: (bi, i, 0)),
                  pl.BlockSpec((1, tb, qk_w), lambda bi, i: (bi, i, 1)),
                  pl.BlockSpec((1, tb, v_w), lambda bi, i: (bi, i, 1)),
                  pl.BlockSpec((1, tb, v_w), lambda bi, i: (bi, i, 2)),
                  tspec, tspec, sspec],
        out_specs=[pl.BlockSpec((1, tb, MIX_C), lambda bi, i: (bi, i, 0)), sspec],
        out_shape=[jax.ShapeDtypeStruct((b, t, MIX_C), BF16),
                   jax.ShapeDtypeStruct((b, H_C, DK_C, DV_C), F32)],
        scratch_shapes=[pltpu.VMEM((H_C, L, L), F32)],
        compiler_params=_cparams(("parallel", "arbitrary")),
        name=name,
    )(proj, proj, proj, proj, cos, sin, s0)


def _out_route_kernel(mixed_ref, wout_ref, h_ref, g1_ref, gain_ref, sc_ref, sh_ref, wrt_ref, rb_ref,
                      hnew_ref, hx_ref, bkt_ref):
    bb, tt, m = mixed_ref.shape
    d = h_ref.shape[-1]
    rows = bb * tt
    y = _dot(mixed_ref[...].reshape(rows, m), wout_ref[...])
    h = h_ref[...] + g1_ref[...] * y.reshape(bb, tt, d)
    hnew_ref[...] = h
    hn = _rms(h) * gain_ref[...]
    hn = (hn * (1.0 + sc_ref[...]) + sh_ref[...]).reshape(rows, d)
    hx_ref[:, 0:d] = hn

    logits = _mm3(wrt_ref[...], hn, NT)
    score = _sigmoid(logits)
    sel = score + rb_ref[...]

    def row(a, r):
        return a[r:r + 1, :]

    gscore = []
    for g in range(N_GROUPS):
        a, b, c, e = (row(sel, EPG * g + i) for i in range(EPG))
        hi_ab, lo_ab, hi_ce, lo_ce = jnp.maximum(a, b), jnp.minimum(a, b), jnp.maximum(c, e), jnp.minimum(c, e)
        top1 = jnp.maximum(hi_ab, hi_ce)
        top2 = jnp.maximum(jnp.maximum(lo_ab, lo_ce), jnp.minimum(hi_ab, hi_ce))
        gscore.append(top1 + top2)
    best = jnp.zeros((1, rows), I32)
    bestv = gscore[0]
    for g in range(1, N_GROUPS):
        upd = gscore[g] > bestv
        best = jnp.where(upd, g, best)
        bestv = jnp.where(upd, gscore[g], bestv)

    def pick(a, i):
        out = row(a, i)
        for g in range(1, N_GROUPS):
            out = jnp.where(best == g, row(a, EPG * g + i), out)
        return out

    vsel = [pick(sel, i) for i in range(EPG)]
    vsc = [pick(score, i) for i in range(EPG)]
    i1 = jnp.zeros((1, rows), I32)
    v1 = vsel[0]
    for i in range(1, EPG):
        upd = vsel[i] > v1
        i1 = jnp.where(upd, i, i1)
        v1 = jnp.where(upd, vsel[i], v1)
    i2 = jnp.full((1, rows), -1, I32)
    v2 = jnp.full((1, rows), -jnp.inf, F32)
    for i in range(EPG):
        cand = jnp.where(i1 == i, -jnp.inf, vsel[i])
        upd = cand > v2
        i2 = jnp.where(upd, i, i2)
        v2 = jnp.where(upd, cand, v2)
    s1 = jnp.zeros((1, rows), F32)
    s2 = jnp.zeros((1, rows), F32)
    for i in range(EPG):
        s1 = jnp.where(i1 == i, vsc[i], s1)
        s2 = jnp.where(i2 == i, vsc[i], s2)
    den = s1 + s2
    w1 = s1 / den
    w2 = s2 / den
    first_lo = i1 < i2
    lo = jnp.minimum(i1, i2)
    hi = jnp.maximum(i1, i2)
    wlo = jnp.where(first_lo, w1, w2)
    whi = jnp.where(first_lo, w2, w1)
    pair = jnp.where(lo == 0, hi - 1, jnp.where(lo == 1, 6 - hi, 5))
    bkt_ref[...] = (best * 6 + pair).reshape(1, 1, rows)
    wa = jnp.where(pair == 5, whi, wlo)
    wb = jnp.where(pair == 5, wlo, whi)

    aux = jnp.concatenate([wa, wb, jnp.zeros((LANES - 2, rows), F32)], axis=0)
    hx_ref[:, d:d + LANES] = aux.T


def _out_route_call(mixed, wout, h, g1, gain, sc, sh, wrt, rb, *, bb, tt, name):
    b, t, m = mixed.shape
    d = h.shape[-1]
    rows = bb * tt
    nti = t // tt
    xspec = pl.BlockSpec((bb, tt, d), lambda bi, i: (bi, i, 0))
    mspec = pl.BlockSpec((bb, 1, d), lambda bi, i: (bi, 0, 0))
    return pl.pallas_call(
        _out_route_kernel,
        grid=(b // bb, nti),
        in_specs=[pl.BlockSpec((bb, tt, m), lambda bi, i: (bi, i, 0)),
                  pl.BlockSpec((m, d), lambda bi, i: (0, 0)),
                  xspec, mspec,
                  pl.BlockSpec((1, 1, d), lambda bi, i: (0, 0, 0)), mspec, mspec,
                  pl.BlockSpec((N_EXPERTS, d), lambda bi, i: (0, 0)),
                  pl.BlockSpec((N_EXPERTS, 1), lambda bi, i: (0, 0))],
        out_specs=[xspec,
                   pl.BlockSpec((rows, HX_W), lambda bi, i: (bi * nti + i, 0)),
                   pl.BlockSpec((1, 1, rows), lambda bi, i: (bi * nti + i, 0, 0))],
        out_shape=[jax.ShapeDtypeStruct((b, t, d), F32),
                   jax.ShapeDtypeStruct((b * t, HX_W), F32),
                   jax.ShapeDtypeStruct(((b // bb) * nti, 1, rows), I32)],
        compiler_params=_cparams(("parallel", "parallel")),
        name=name,
    )(mixed, wout, h, g1, gain.reshape(1, 1, d), sc, sh, wrt, rb)


def _dispatch_kernel(pos_ref, hxp_ref, hxs_ref, xs_in, xs_out, sem, *, np_tiles):
    del xs_in
    i = pl.program_id(0)
    rows = hxp_ref.shape[0]

    def scatter(hx_ref):
        def row_copy(r):
            p = pos_ref[i * rows + r]
            return pltpu.make_async_copy(hx_ref.at[pl.ds(r, 1), :], xs_out.at[pl.ds(p, 1), :], sem)

        for r in range(rows):
            row_copy(r).start(priority=r % 2)
        for r in range(rows):
            row_copy(r).wait()

    @pl.when(i < np_tiles)
    def _():
        scatter(hxp_ref)

    @pl.when(i >= np_tiles)
    def _():
        scatter(hxs_ref)


def _dispatch_call(pos, hx_p, hx_s, xs_buf, rows):
    np_tiles, ns_tiles = hx_p.shape[0] // rows, hx_s.shape[0] // rows
    return pl.pallas_call(
        functools.partial(_dispatch_kernel, np_tiles=np_tiles),
        grid_spec=pltpu.PrefetchScalarGridSpec(
            num_scalar_prefetch=1, grid=(np_tiles + ns_tiles,),
            in_specs=[pl.BlockSpec((rows, HX_W), lambda i, ps: (jnp.minimum(i, np_tiles - 1), 0)),
                      pl.BlockSpec((rows, HX_W), lambda i, ps: (jnp.maximum(i - np_tiles, 0), 0)),
                      pl.BlockSpec(memory_space=pl.ANY)],
            out_specs=pl.BlockSpec(memory_space=pl.ANY),
            scratch_shapes=[pltpu.SemaphoreType.DMA(())]),
        out_shape=jax.ShapeDtypeStruct(xs_buf.shape, F32),
        input_output_aliases={3: 0},
        compiler_params=_cparams(("arbitrary",)),
        name="dispatch",
    )(pos, hx_p, hx_s, xs_buf)


def _moe_kernel(exp_ref, chg_ref, nxt_ref, valid_ref, xs_ref, wg_hbm, wu_hbm, wd_hbm,
                ys_ref, wg_f, wu_f, wd_f, wgu_s, wd_s, sem, *, layer):
    t = pl.program_id(0)

    def fetch(slot, e):
        return (pltpu.make_async_copy(wg_hbm.at[layer, e], wg_f.at[slot], sem.at[slot]),
                pltpu.make_async_copy(wu_hbm.at[layer, e], wu_f.at[slot], sem.at[slot]),
                pltpu.make_async_copy(wd_hbm.at[layer, e], wd_f.at[slot], sem.at[slot]))

    def recast(slot):
        rc = 128
        for i in range(D_MODEL // rc):
            rows = slice(i * rc, (i + 1) * rc)
            wgu_s[rows, slot * D_FF:(slot + 1) * D_FF] = wg_f[slot, rows, :].astype(BF16)
            wgu_s[rows, (2 + slot) * D_FF:(3 + slot) * D_FF] = wu_f[slot, rows, :].astype(BF16)
        for i in range(D_FF // rc):
            wd_s[slot * D_FF + i * rc:slot * D_FF + (i + 1) * rc, :] = wd_f[slot, i * rc:(i + 1) * rc, :].astype(BF16)

    for slot in range(2):
        @pl.when(t == 0)
        def _():
            for cp in fetch(slot, exp_ref[slot, 0]):
                cp.start()

        @pl.when(chg_ref[slot, t] > 0)
        def _():
            for cp in fetch(slot, exp_ref[slot, t]):
                cp.wait()
            recast(slot)

            @pl.when(nxt_ref[slot, t] >= 0)
            def _():
                for cp in fetch(slot, nxt_ref[slot, t]):
                    cp.start()

    @pl.when(valid_ref[t] > 0)
    def _():
        x = xs_ref[:, 0:D_MODEL].astype(BF16)
        w_a = xs_ref[:, D_MODEL:D_MODEL + 1]
        w_b = xs_ref[:, D_MODEL + 1:D_MODEL + 2]
        gu = _dot(x, wgu_s[...])
        lane = lax.broadcasted_iota(I32, (MOE_TM, 2 * D_FF), 1)
        act = _silu(gu[:, :2 * D_FF]) * gu[:, 2 * D_FF:] * jnp.where(lane < D_FF, w_a, w_b)
        ys_ref[...] = _dot(act.astype(BF16), wd_s[...])

    @pl.when(valid_ref[t] == 0)
    def _():
        ys_ref[...] = jnp.zeros_like(ys_ref)


def _moe_call(experts, chg, nxt, valid, xs, wg, wu, wd, layer):
    p = xs.shape[0]
    n_tiles = p // MOE_TM
    return pl.pallas_call(
        functools.partial(_moe_kernel, layer=layer),
        grid_spec=pltpu.PrefetchScalarGridSpec(
            num_scalar_prefetch=4, grid=(n_tiles,),
            in_specs=[pl.BlockSpec((MOE_TM, HX_W), lambda t, ex, ch, nx, va: (t, 0)),
                      pl.BlockSpec(memory_space=pl.ANY), pl.BlockSpec(memory_space=pl.ANY),
                      pl.BlockSpec(memory_space=pl.ANY)],
            out_specs=pl.BlockSpec((MOE_TM, D_MODEL), lambda t, ex, ch, nx, va: (t, 0)),
            scratch_shapes=[pltpu.VMEM((2, D_MODEL, D_FF), F32), pltpu.VMEM((2, D_MODEL, D_FF), F32),
                            pltpu.VMEM((2, D_FF, D_MODEL), F32),
                            pltpu.VMEM((D_MODEL, 4 * D_FF), BF16), pltpu.VMEM((2 * D_FF, D_MODEL), BF16),
                            pltpu.SemaphoreType.DMA((2,))]),
        out_shape=jax.ShapeDtypeStruct((p, D_MODEL), F32),
        compiler_params=_cparams(("arbitrary",)),
        name="moe",
    )(experts, chg, nxt, valid, xs, wg, wu, wd)


_PAIRS = [(0, 1), (0, 2), (0, 3), (1, 3), (1, 2), (3, 2)]
_EA_TABLE = np.array([g * EPG + p[0] for g in range(N_GROUPS) for p in _PAIRS], np.int32)
_EB_TABLE = np.array([g * EPG + p[1] for g in range(N_GROUPS) for p in _PAIRS], np.int32)


def _moe(hx_p, hx_s, bkt, xs_buf, wg, wu, wd, layer):
    n = hx_p.shape[0] + hx_s.shape[0]
    tm = MOE_TM
    ar = jnp.arange(N_BUCKETS, dtype=I32)
    blk = LANES
    nb = -(-n // blk)
    bkt_p = jnp.pad(bkt, (0, nb * blk - n), constant_values=N_BUCKETS)
    oh = (ar[:, None] == bkt_p[None, :]).astype(F32)
    triu = jnp.triu(jnp.ones((blk, blk), F32))
    inner = jnp.einsum('kbj,ji->kbi', oh.reshape(N_BUCKETS, nb, blk), triu)
    tot = inner[:, :, -1]
    outer = jnp.cumsum(tot, axis=1) - tot
    cs = (inner + outer[:, :, None]).reshape(N_BUCKETS, nb * blk)
    counts = (outer[:, -1] + tot[:, -1]).astype(I32)
    padded = ((counts + tm - 1) // tm) * tm
    ends = jnp.cumsum(padded)
    offs = ends - padded
    pos = jnp.sum(oh * (cs - 1.0 + offs.astype(F32)[:, None]), axis=0).astype(I32)[:n]
    n_tiles = -(-n // tm) + N_BUCKETS
    p = n_tiles * tm
    if xs_buf is None:
        xs_buf = jnp.zeros((p, HX_W), F32)
    tstart = jnp.arange(n_tiles, dtype=I32) * tm
    tb = jnp.sum((ends[None, :] <= tstart[:, None]).astype(I32), axis=1)
    n_used = ends[-1] // tm
    valid = (jnp.arange(n_tiles, dtype=I32) < n_used).astype(I32)
    last_b = tb[jnp.maximum(n_used - 1, 0)]
    tb = jnp.clip(jnp.where(valid > 0, tb, last_b), 0, N_BUCKETS - 1)
    ea = jnp.asarray(_EA_TABLE)[tb]
    eb = jnp.asarray(_EB_TABLE)[tb]
    experts = jnp.stack([ea, eb])
    chg = jnp.concatenate([jnp.ones((2, 1), I32), (experts[:, 1:] != experts[:, :-1]).astype(I32)], axis=1)
    tidx = jnp.arange(n_tiles, dtype=I32)
    chg_at = jnp.where(chg > 0, tidx[None, :], n_tiles)
    nxt_at = jnp.concatenate([lax.cummin(chg_at, axis=1, reverse=True)[:, 1:], jnp.full((2, 1), n_tiles, I32)], axis=1)
    nxt = jnp.where(nxt_at < n_tiles,
                    jnp.take_along_axis(experts, jnp.minimum(nxt_at, n_tiles - 1), axis=1), -1)
    xs = _dispatch_call(pos, hx_p, hx_s, xs_buf, tm)
    ys = _moe_call(experts, chg, nxt, valid, xs, wg, wu, wd, layer)
    return ys.at[pos].get(mode="promise_in_bounds", unique_indices=True), xs


def _final_kernel(h_ref, y_ref, g_ref, gain_ref, o_ref):
    bb, tt, d = h_ref.shape
    h = h_ref[...] + g_ref[...] * y_ref[...].reshape(bb, tt, d)
    o_ref[...] = _rms(h) * gain_ref[...]


def _final_call(h, y2d, off, g2, gain, *, bb, tt, name):
    b, t, d = h.shape
    nti = t // tt
    rows = bb * tt
    xspec = pl.BlockSpec((bb, tt, d), lambda bi, i: (bi, i, 0))
    return pl.pallas_call(
        _final_kernel,
        grid=(b // bb, nti),
        in_specs=[xspec, pl.BlockSpec((rows, d), lambda bi, i: (off + bi * nti + i, 0)),
                  pl.BlockSpec((bb, 1, d), lambda bi, i: (bi, 0, 0)),
                  pl.BlockSpec((1, 1, d), lambda bi, i: (0, 0, 0))],
        out_specs=xspec,
        out_shape=jax.ShapeDtypeStruct((b, t, d), F32),
        compiler_params=_cparams(("parallel", "parallel")),
        name=name,
    )(h, y2d, g2, gain.reshape(1, 1, d))


def _group_cfg(b, t):
    if t >= 512:
        return dict(bb=1, tt=512, tt_out=1024, L=CHUNK, CB=4, Lc=4 * CHUNK, CBc=1)
    return dict(bb=b, tt=t, tt_out=t, L=min(CHUNK, t), CB=1, Lc=min(CHUNK, t), CBc=1)


def _rope_tables(pos0, t):
    half = DK_C // 2
    inv = np.power(np.float64(ROPE_BASE), -np.linspace(0.0, 1.0, half, dtype=np.float64))
    ang = (np.float64(pos0) + np.arange(t, dtype=np.float64))[:, None] * inv[None, :]
    return jnp.asarray(np.cos(ang), F32), jnp.asarray(np.sin(ang), F32)


def kernel(x_prompt, x_sample, c_prompt, c_sample, state_a_conv, state_a_rec, state_b_c, state_b_n, state_b_m,
           state_c_rec, w_mod, b_mod, norm_mix, norm_ffn, w_in_ab, conv_a, a_log, dt_bias, norm_a, gate_bias_b,
           norm_b, w_out_ab, w_in_c, w_out_c, w_router, router_bias, w_gate, w_up, w_down, norm_final):
    d = D_MODEL
    bp, tp, _ = x_prompt.shape
    bs, ts, _ = x_sample.shape
    n_ab, n_c = w_in_ab.shape[0], w_in_c.shape[0]

    mod_all = _mod_call(jnp.concatenate([c_prompt, c_sample], axis=0), w_mod, b_mod)

    o = np.cumsum((0,) + AB_SIZES)
    w_ab_main = jnp.concatenate([w_in_ab[:, :, o[0]:o[1]].astype(BF16), w_in_ab[:, :, o[3]:o[7]].astype(BF16),
                                 w_in_ab[:, :, o[8]:o[9]].astype(BF16)], axis=-1)
    w_ab_gate = jnp.concatenate([w_in_ab[:, :, o[1]:o[3]], w_in_ab[:, :, o[7]:o[8]]], axis=-1)
    w_ab_gate_pad = jnp.pad(w_ab_gate, ((0, 0), (0, 0), (0, LANES - N_GATE_COLS)))
    w_c = w_in_c.astype(BF16)
    w_out_ab_b = w_out_ab.astype(BF16)
    w_out_c_b = w_out_c.astype(BF16)
    wrt = w_router.T
    rb = router_bias.reshape(N_EXPERTS, 1)
    zeros4 = jnp.zeros((n_ab, H_A), F32)
    gpar_row0 = jnp.concatenate([a_log, jnp.zeros((n_ab, LANES - H_A), F32)], axis=1)
    gpar_row1 = jnp.concatenate([dt_bias, zeros4, gate_bias_b, jnp.zeros((n_ab, LANES - 16), F32)], axis=1)
    gpar = jnp.concatenate([gpar_row0[:, None], gpar_row1[:, None], jnp.zeros((n_ab, 6, LANES), F32)], axis=1)
    gpart = jnp.swapaxes(jnp.concatenate([gpar[:, :2, :N_GATE_COLS], jnp.zeros((n_ab, LANES - 2, N_GATE_COLS), F32)],
                                         axis=1), 1, 2)

    groups = []
    zf = lambda *s: jnp.zeros(s, F32)
    groups.append(dict(
        x=x_prompt, b=bp, t=tp, pos0=0.0, mod=mod_all[:, :bp], row0=0,
        conv=zf(n_ab, bp, CONV_W - 1, QKV_A), sa=zf(n_ab, bp, H_A, DK_A, DV_A), cb=zf(n_ab, bp, H_B, DQK_B, DV_B),
        nb=zf(n_ab, bp, H_B, DQK_B), mb=zf(n_ab, bp, H_B), sc=zf(n_c, bp, H_C, DK_C, DV_C)))
    groups.append(dict(
        x=x_sample, b=bs, t=ts, pos0=float(PAST_LEN), mod=mod_all[:, bp:], row0=bp * tp,
        conv=state_a_conv, sa=state_a_rec, cb=state_b_c, nb=state_b_n, mb=state_b_m, sc=state_c_rec))
    for g in groups:
        g.update(_group_cfg(g["b"], g["t"]))
        g["h"] = g["x"]
        g["prev"] = None
        g["new_ab"], g["new_c"] = [], []
        g["cos"], g["sin"] = _rope_tables(g["pos0"], g["t"])
        g["roff"] = g["row0"] // (g["bb"] * g["tt"])

    xs_buf = None
    xs_rows = (-(-(bp * tp + bs * ts) // MOE_TM) + N_BUCKETS) * MOE_TM
    for layer in range(DEPTH):
        li = layer // 2
        bkts, hxs = [], []
        for gi, g in enumerate(groups):
            b, t, bb, tt = g["b"], g["t"], g["bb"], g["tt"]
            mods = [m.reshape(b, 1, d) for m in jnp.split(g["mod"][layer], N_MOD, axis=-1)]
            sh1, sc1, g1, sh2, sc2, g2 = mods
            tag = f"l{layer}g{gi}"
            if layer % 2 == 0:
                outs = _ln_mm_call(g["h"], g["prev"], norm_mix[layer], sc1, sh1, w_ab_main[li],
                                   (w_ab_gate_pad[li], g["conv"][li], conv_a[li]), bb=bb, tt=tt, tn=512,
                                   out_dtype=F32, name="ln_ab_" + tag,
                                   zero_rows=xs_rows if (xs_buf is None and gi == 0) else 0)
                if g["prev"] is not None:
                    g["h"], outs = outs[0], outs[1:]
                if xs_buf is None and gi == 0:
                    xs_buf, outs = outs[-1], outs[:-1]
                proj, gates, gates_t, qkv, conv_n = outs
                gates_t = gates_t.reshape(N_GATE_COLS, b, t).transpose(1, 0, 2)
                mixed, sa_n, cb_n, nb_n, mb_n = _mixer_ab_call(
                    qkv, proj, gates, gates_t, g["sa"][li], g["cb"][li],
                    g["nb"][li].reshape(b, H_B, 1, DQK_B), g["mb"][li].reshape(b, H_B, 1, 1),
                    gpar[li], gpart[li], norm_a[li].reshape(1, DV_A), norm_b[li].reshape(1, DV_B),
                    L=g["L"], CB=g["CB"], name="mixer_ab_" + tag)
                g["new_ab"].append((conv_n, sa_n, cb_n, nb_n.reshape(b, H_B, DQK_B), mb_n.reshape(b, H_B)))
                wout = w_out_ab_b[li]
            else:
                outs = _ln_mm_call(g["h"], g["prev"], norm_mix[layer], sc1, sh1, w_c[li], None,
                                   bb=bb, tt=tt, tn=512, out_dtype=BF16, name="ln_c_" + tag)
                if g["prev"] is not None:
                    g["h"], outs = outs[0], outs[1:]
                (proj,) = outs
                mixed, sc_n = _mixer_c_call(proj, g["cos"], g["sin"], g["sc"][li], L=g["Lc"], CB=g["CBc"],
                                            name="mixer_c_" + tag)
                g["new_c"].append(sc_n)
                wout = w_out_c_b[li]
            tto = g["tt_out"]
            g["h"], hx, bkt = _out_route_call(mixed, wout, g["h"], g1, norm_ffn[layer], sc2, sh2, wrt, rb,
                                              bb=bb, tt=tto, name="out_route_" + tag)
            bkts.append(bkt.reshape(-1))
            hxs.append(hx)
            g["g2"] = g2
        y_moe, xs_buf = _moe(hxs[0], hxs[1], jnp.concatenate(bkts), xs_buf, w_gate, w_up, w_down, layer)
        for g in groups:
            g["prev"] = (y_moe, g["roff"], g["g2"])

    outs = []
    for gi, g in enumerate(groups):
        y2d, off, g2 = g["prev"]
        y = _final_call(g["h"], y2d, off, g2, norm_final, bb=g["bb"], tt=g["tt"], name=f"final_g{gi}")
        ab = [jnp.stack(s) for s in zip(*g["new_ab"])]
        outs.append((y, ab[0], ab[1], ab[2], ab[3], ab[4], jnp.stack(g["new_c"])))
    p, s = outs
    return (p[0], s[0], p[1], p[2], p[3], p[4], p[5], p[6], s[1], s[2], s[3], s[4], s[5], s[6])
```

```python
import functools

import numpy as np
import jax
import jax.numpy as jnp
from jax import lax
from jax.experimental import pallas as pl
from jax.experimental.pallas import tpu as pltpu

F32 = jnp.float32
BF16 = jnp.bfloat16
I32 = jnp.int32

D_MODEL = 1024
DEPTH = 2
CHUNK = 64
H_A, DK_A, DV_A, CONV_W = 4, 128, 128, 4
QKV_A = H_A * (2 * DK_A + DV_A)
H_B, DQK_B, DV_B = 4, 64, 128
H_C, DK_C, DV_C = 4, 256, 512
ROPE_BASE = 10000.0
PAST_LEN = 4096
AB_SIZES = (QKV_A, H_A, H_A, H_A * DV_A, H_B * DQK_B, H_B * DQK_B, H_B * DV_B, 2 * H_B, H_B * DV_B)
MIX_AB = H_A * DV_A + H_B * DV_B
MIX_C = H_C * DV_C
N_EXPERTS, N_GROUPS, EPG, D_FF = 16, 4, 4, 512
N_MOD = 6
EPS = 1e-6

LANES = 128
N_GATE_COLS = 16
HX_W = D_MODEL + LANES
N_BUCKETS = N_GROUPS * 6
MOE_TM = 256
VMEM_LIMIT = 48 * 1024 * 1024

NN = ((1,), (0,))
NT = ((1,), (1,))
TN = ((0,), (0,))


def _dot(a, b, dims=NN):
    return lax.dot_general(a, b, (dims, ((), ())), preferred_element_type=F32)


def _mmb(a, b, dims=NN):
    return _dot(a.astype(BF16), b.astype(BF16), dims)


def _split2(x):
    hi = x.astype(BF16)
    lo = (x - hi.astype(F32)).astype(BF16)
    return hi, lo


def _split3(x):
    hi = x.astype(BF16)
    r = x - hi.astype(F32)
    mid = r.astype(BF16)
    lo = (r - mid.astype(F32)).astype(BF16)
    return hi, mid, lo


def _mm3(a, b, dims=NN):
    ah, al = _split2(a)
    bh, bl = _split2(b)
    return _dot(ah, bh, dims) + (_dot(ah, bl, dims) + _dot(al, bh, dims))


def _mm_mask_l(mask_bf16, x, dims=NN):
    h, m, l = _split3(x)
    return _dot(mask_bf16, h, dims) + (_dot(mask_bf16, m, dims) + _dot(mask_bf16, l, dims))


def _mm_mask_r(x, mask_bf16):
    h, m, l = _split3(x)
    return _dot(h, mask_bf16) + (_dot(m, mask_bf16) + _dot(l, mask_bf16))


def _sigmoid(x):
    return 1.0 / (1.0 + jnp.exp(-x))


def _silu(x):
    return x * _sigmoid(x)


def _softplus(x):
    return jnp.maximum(x, 0.0) + jnp.log(1.0 + jnp.exp(-jnp.abs(x)))


def _rms(x, eps=EPS):
    return x * lax.rsqrt(jnp.mean(x * x, axis=-1, keepdims=True) + eps)


def _cparams(sem):
    return pltpu.CompilerParams(dimension_semantics=sem, vmem_limit_bytes=VMEM_LIMIT)


def _mod_kernel(c_ref, w_ref, b_ref, o_ref):
    c = c_ref[...]
    o_ref[0] = _mm3(_silu(c), w_ref[0]) + b_ref[0]


def _mod_call(c_all, w_mod, b_mod):
    bt = c_all.shape[0]
    e = w_mod.shape[-1]
    tn = 1024
    return pl.pallas_call(
        _mod_kernel,
        grid=(DEPTH, e // tn),
        in_specs=[pl.BlockSpec((bt, D_MODEL), lambda l, j: (0, 0)),
                  pl.BlockSpec((1, D_MODEL, tn), lambda l, j: (l, 0, j)),
                  pl.BlockSpec((1, 1, tn), lambda l, j: (l, 0, j))],
        out_specs=pl.BlockSpec((1, bt, tn), lambda l, j: (l, 0, j)),
        out_shape=jax.ShapeDtypeStruct((DEPTH, bt, e), F32),
        compiler_params=_cparams(("parallel", "parallel")),
        name="mod",
    )(c_all, w_mod, b_mod.reshape(DEPTH, 1, e))


def _ln_mm_kernel(*refs, has_prev, has_gates, tn, zero_rows):
    it = iter(refs)
    x_ref = next(it)
    if has_prev:
        yp_ref, gp_ref = next(it), next(it)
    gain_ref, sc_ref, sh_ref, w_ref = next(it), next(it), next(it), next(it)
    if has_gates:
        wg_ref, conv0_ref, convw_ref = next(it), next(it), next(it)
    if has_prev:
        h_ref = next(it)
    proj_ref = next(it)
    if has_gates:
        g_ref, gt_ref, qkv_ref, conv_o = next(it), next(it), next(it), next(it)
    if zero_rows:
        zeros_ref = next(it)
        zeros_ref[...] = jnp.zeros_like(zeros_ref)
    hn_s = next(it)
    if has_gates:
        tail_s = next(it)

        @pl.when(pl.program_id(1) == 0)
        def _():
            tail_s[...] = conv0_ref[...]

    bb, tt, d = x_ref.shape
    rows = bb * tt
    x = x_ref[...]
    if has_prev:
        x = x + gp_ref[...] * yp_ref[...].reshape(bb, tt, d)
        h_ref[...] = x
    hn = _rms(x) * gain_ref[...]
    hn = hn * (1.0 + sc_ref[...]) + sh_ref[...]
    hn2 = hn.reshape(rows, d)
    hn_s[...] = hn2.astype(BF16)
    if has_gates:
        gates = _mm3(hn2, wg_ref[...])
        g_ref[...] = gates.reshape(bb, tt, LANES)
        gt_ref[...] = gates.T[0:N_GATE_COLS, :]
    n_conv = QKV_A // tn if has_gates else 0
    n_tiles = w_ref.shape[-1] // tn

    def project(j):
        return _dot(hn_s[...], w_ref[:, j * tn:(j + 1) * tn]).reshape(bb, tt, tn)

    def finish(j, y):
        cols = slice(j * tn, (j + 1) * tn)
        if j < n_conv:
            prev = tail_s[:, :, cols]
            acc = convw_ref[CONV_W - 1:CONV_W, cols] * y
            for lag in range(1, CONV_W):
                shifted = jnp.concatenate([prev[:, CONV_W - 1 - lag:, :], y[:, :tt - lag, :]], axis=1)
                acc = acc + convw_ref[CONV_W - 1 - lag:CONV_W - lag, cols] * shifted
            qkv_ref[:, :, cols] = _silu(acc).astype(qkv_ref.dtype)
            tail = y[:, tt - (CONV_W - 1):tt, :]
            tail_s[:, :, cols] = tail
            conv_o[:, :, cols] = tail
        else:
            pcols = slice((j - n_conv) * tn, (j - n_conv + 1) * tn)
            proj_ref[:, :, pcols] = y.astype(proj_ref.dtype)

    y = project(0)
    for j in range(n_tiles):
        y_next = project(j + 1) if j + 1 < n_tiles else None
        finish(j, y)
        y = y_next


def _ln_mm_call(x, prev, gain, sc, sh, w, gates_w, *, bb, tt, tn, out_dtype, name, zero_rows=0):
    b, t, d = x.shape
    e = w.shape[1]
    rows = bb * tt
    nti = t // tt
    grid = (b // bb, nti)
    has_prev = prev is not None
    has_gates = gates_w is not None
    xspec = pl.BlockSpec((bb, tt, d), lambda bi, i: (bi, i, 0))
    mspec = pl.BlockSpec((bb, 1, d), lambda bi, i: (bi, 0, 0))
    in_specs, args = [xspec], [x]
    if has_prev:
        y2d, off, gp = prev
        in_specs += [pl.BlockSpec((rows, d), lambda bi, i: (off + bi * nti + i, 0)), mspec]
        args += [y2d, gp]
    in_specs += [pl.BlockSpec((1, 1, d), lambda bi, i: (0, 0, 0)), mspec, mspec,
                 pl.BlockSpec((d, e), lambda bi, i: (0, 0), pipeline_mode=pl.Buffered(1))]
    args += [gain.reshape(1, 1, d), sc, sh, w]
    tail_spec = pl.BlockSpec((bb, CONV_W - 1, QKV_A), lambda bi, i: (bi, 0, 0))
    if has_gates:
        wg, conv0, convw = gates_w
        in_specs += [pl.BlockSpec((d, LANES), lambda bi, i: (0, 0)), tail_spec,
                     pl.BlockSpec((CONV_W, QKV_A), lambda bi, i: (0, 0))]
        args += [wg, conv0, convw]
    e_proj = e - QKV_A if has_gates else e
    out_specs, out_shape = [], []
    if has_prev:
        out_specs.append(xspec)
        out_shape.append(jax.ShapeDtypeStruct((b, t, d), F32))
    out_specs.append(pl.BlockSpec((bb, tt, e_proj), lambda bi, i: (bi, i, 0)))
    out_shape.append(jax.ShapeDtypeStruct((b, t, e_proj), out_dtype))
    scratch = [pltpu.VMEM((rows, d), BF16)]
    if has_gates:
        out_specs += [pl.BlockSpec((bb, tt, LANES), lambda bi, i: (bi, i, 0)),
                      pl.BlockSpec((N_GATE_COLS, rows), lambda bi, i: (0, bi * nti + i)),
                      pl.BlockSpec((bb, tt, QKV_A), lambda bi, i: (bi, i, 0)), tail_spec]
        out_shape += [jax.ShapeDtypeStruct((b, t, LANES), F32),
                      jax.ShapeDtypeStruct((N_GATE_COLS, b * t), F32),
                      jax.ShapeDtypeStruct((b, t, QKV_A), BF16),
                      jax.ShapeDtypeStruct((b, CONV_W - 1, QKV_A), F32)]
        scratch.append(pltpu.VMEM((bb, CONV_W - 1, QKV_A), F32))
    if zero_rows:
        n_steps = grid[0] * grid[1]
        assert zero_rows % (8 * n_steps) == 0
        out_specs.append(pl.BlockSpec((zero_rows // n_steps, HX_W), lambda bi, i: (bi * nti + i, 0)))
        out_shape.append(jax.ShapeDtypeStruct((zero_rows, HX_W), F32))
    return pl.pallas_call(
        functools.partial(_ln_mm_kernel, has_prev=has_prev, has_gates=has_gates, tn=tn, zero_rows=zero_rows),
        grid=grid, in_specs=in_specs, out_specs=out_specs, out_shape=out_shape,
        scratch_shapes=scratch,
        compiler_params=_cparams(("parallel", "arbitrary" if has_gates else "parallel")),
        name=name,
    )(*args)


def _tri_inv_all(a_list, eye, length, mm):
    xs = [-a for a in a_list]
    ps = [eye + x for x in xs]
    n = 2
    while n < length:
        xs = [mm(x, x) for x in xs]
        ps = [p + mm(p, x) for p, x in zip(ps, xs)]
        n *= 2
    return ps


def _mixer_ab_kernel(*refs, L, CB, side_cast):
    n_in = 15 + int(side_cast)
    (qkv_ref, z_ref, qkb_ref, vb_ref, ob_ref, g_ref, gt_ref, sa0_ref, cb0_ref, nb0_ref, mb0_ref,
     gpar_ref, gpart_ref, na_ref, nbn_ref) = refs[:15]
    mixed_ref, sa_o, cb_o, nb_o, mb_o = refs[n_in:n_in + 5]
    if side_cast:
        refs[n_in + 5][...] = refs[15][...].astype(BF16)

    @pl.when(pl.program_id(1) == 0)
    def _():
        sa_o[...] = sa0_ref[...]
        cb_o[...] = cb0_ref[...]
        nb_o[...] = nb0_ref[...]
        mb_o[...] = mb0_ref[...]

    ii = lax.broadcasted_iota(I32, (L, L), 0)
    jj = lax.broadcasted_iota(I32, (L, L), 1)
    tri = ii >= jj
    strict = ii > jj
    eye = (ii == jj).astype(F32)
    tril_b = tri.astype(BF16)
    triu_b = (ii <= jj).astype(BF16)
    rowi = lax.broadcasted_iota(I32, (N_GATE_COLS, L), 0)
    gpar = gpar_ref[...]
    gpart = gpart_ref[...]
    neat = -jnp.exp(gpart[:, 0:1])
    norm_a = na_ref[...]
    norm_b = nbn_ref[...]

    mm = _mmb

    def qkv_tile(r0, col0):
        return qkv_ref[0, r0:r0 + L, col0:col0 + DK_A].astype(F32)

    chunks = range(CB)
    items = [(c, h) for c in chunks for h in range(H_A)]
    rs = [slice(c * L, (c + 1) * L) for c in chunks]

    pre = [g_ref[0, r, :] + gpar[1:2, :] for r in rs]
    pret = [gt_ref[0, :, r] + gpart[:, 1:2] for r in rs]
    csrct = [jnp.where(rowi < 4, neat * _softplus(p), jnp.where(rowi >= 12, -_softplus(-p), 0.0)) for p in pret]
    gcumt = [_mm_mask_r(x, triu_b) for x in csrct]
    gcum = [_mm_mask_l(tril_b, x, NT) for x in csrct]
    beta_all = [_sigmoid(p) for p in pre]

    q = [qkv_tile(c * L, h * DK_A) for c, h in items]
    k = [qkv_tile(c * L, H_A * DK_A + h * DK_A) for c, h in items]
    v = [qkv_tile(c * L, 2 * H_A * DK_A + h * DV_A) for c, h in items]
    q = [x * lax.rsqrt(jnp.sum(x * x, axis=-1, keepdims=True) + EPS) * (DK_A ** -0.5) for x in q]
    k = [x * lax.rsqrt(jnp.sum(x * x, axis=-1, keepdims=True) + EPS) for x in k]
    gc_c = [gcum[c][:, h:h + 1] for c, h in items]
    gc_r = [gcumt[c][h:h + 1, :] for c, h in items]
    decay = [jnp.where(tri, jnp.exp(jnp.where(tri, a - b, 0.0)), 0.0) for a, b in zip(gc_c, gc_r)]
    beta_c = [beta_all[c][:, 4 + h:5 + h] for c, h in items]
    kb = [x * b for x, b in zip(k, beta_c)]
    a_list = [jnp.where(strict, mm(x, y, NT) * d, 0.0) for x, y, d in zip(kb, k, decay)]
    attn = [jnp.where(tri, mm(x, y, NT) * d, 0.0) for x, y, d in zip(q, k, decay)]
    eg = [jnp.exp(x) for x in gc_c]
    gl = [x[L - 1:L, :] for x in gc_c]
    rhs = [jnp.concatenate([x * e, y * b], axis=-1) for x, e, y, b in zip(kb, eg, v, beta_c)]
    qeg = [x * e for x, e in zip(q, eg)]
    kdec = [x * jnp.exp(g - gc) for x, g, gc in zip(k, gl, gc_c)]
    sdec = [jnp.exp(g) for g in gl]
    a_list = [a.astype(BF16) for a in a_list]
    tinv = _tri_inv_all([a.astype(F32) for a in a_list], eye, L, mm)
    sol = [mm(t, r) for t, r in zip(tinv, rhs)]
    sol_hl = [_split2(s) for s in sol]
    resid = [r - (s + (_dot(a, hi) + _dot(a, lo))) for r, s, a, (hi, lo) in zip(rhs, sol, a_list, sol_hl)]
    sol = [s + mm(t, r) for s, t, r in zip(sol, tinv, resid)]

    heads = range(H_B)
    qb = [qkb_ref[0, rs[c], h * DQK_B:(h + 1) * DQK_B] * (DQK_B ** -0.5) for c, h in items]
    kbb = [qkb_ref[0, rs[c], H_B * DQK_B + h * DQK_B:H_B * DQK_B + (h + 1) * DQK_B] for c, h in items]
    b_c = [gcum[c][:, 12 + h:13 + h] for c, h in items]
    dm = [jnp.where(tri, gcum[c][:, 12 + h:13 + h] - gcumt[c][12 + h:13 + h, :] + pret[c][8 + h:9 + h, :], -jnp.inf)
          for c, h in items]
    dmax = [jnp.max(x, axis=-1, keepdims=True) for x in dm]
    qkm = [mm(x, y, NT) for x, y in zip(qb, kbb)]
    li_c = [pre[c][:, 8 + h:9 + h] for c, h in items]

    for c in chunks:
        ia = [c * H_A + h for h in heads]
        s = [sa_o[0, h] for h in heads]
        v_new = [sol[i][:, DK_A:] - mm(sol[i][:, :DK_A], s[h]) for h, i in enumerate(ia)]
        m_prev = [mb_o[0, h] for h in heads]
        a0 = [b_c[i] + m_prev[h] for h, i in enumerate(ia)]
        m_t = [jnp.maximum(a0[h], dmax[i]) for h, i in enumerate(ia)]
        w0 = [jnp.exp(a0[h] - m_t[h]) for h in heads]
        sm = [qkm[i] * jnp.exp(dm[i] - m_t[h]) for h, i in enumerate(ia)]
        cst = [cb_o[0, h] for h in heads]
        nst = [nb_o[0, h] for h in heads]
        vb = [vb_ref[0, rs[c], h * DV_B:(h + 1) * DV_B] for h in heads]
        o_a = [mm(qeg[i], s[h]) + mm(attn[i], v_new[h]) for h, i in enumerate(ia)]
        for h, i in enumerate(ia):
            sa_o[0, h] = s[h] * sdec[i] + mm(kdec[i], v_new[h], TN)
        num = [w0[h] * mm(qb[i], cst[h]) + mm(sm[h], vb[h]) for h, i in enumerate(ia)]
        den = [w0[h] * jnp.sum(qb[i] * nst[h], axis=-1, keepdims=True) + jnp.sum(sm[h], axis=-1, keepdims=True)
               for h, i in enumerate(ia)]
        hh = [num[h] / jnp.maximum(jnp.abs(den[h]), jnp.exp(-m_t[h])) for h in heads]
        m_new = [m_t[h][L - 1:L, :] for h in heads]
        kw = [kbb[i] * jnp.exp(b_c[i][L - 1:L, :] - b_c[i] + li_c[i] - m_new[h]) for h, i in enumerate(ia)]
        for h in heads:
            w0l = w0[h][L - 1:L, :]
            cb_o[0, h] = w0l * cst[h] + mm(kw[h], vb[h], TN)
            nb_o[0, h] = w0l * nst[h] + jnp.sum(kw[h], axis=0, keepdims=True)
            mb_o[0, h] = m_new[h]
        for h in heads:
            zg = z_ref[0, rs[c], h * DV_A:(h + 1) * DV_A]
            mixed_ref[0, rs[c], h * DV_A:(h + 1) * DV_A] = (_rms(o_a[h]) * norm_a * _silu(zg)).astype(mixed_ref.dtype)
            og = ob_ref[0, rs[c], h * DV_B:(h + 1) * DV_B]
            c0 = H_A * DV_A + h * DV_B
            mixed_ref[0, rs[c], c0:c0 + DV_B] = (_rms(hh[h]) * norm_b * _sigmoid(og)).astype(mixed_ref.dtype)


def _mixer_ab_call(qkv, proj, gates, gates_t, sa0, cb0, nb0, mb0, gpar, gpart, norm_a, norm_b, *, L, CB, name,
                   cast_w=None):
    b, t, _ = proj.shape
    tb = L * CB
    nt = t // tb

    def col(width, idx):
        return pl.BlockSpec((1, tb, width), lambda bi, i: (bi, i, idx))

    def const(shape):
        return pl.BlockSpec(shape, lambda bi, i: (0,) * len(shape))

    def state(shape):
        return pl.BlockSpec((1,) + shape, lambda bi, i: (bi,) + (0,) * len(shape))

    in_specs = [col(QKV_A, 0), col(512, 0), col(512, 1), col(512, 2), col(512, 3),
                col(LANES, 0), pl.BlockSpec((1, N_GATE_COLS, tb), lambda bi, i: (bi, 0, i)),
                state((H_A, DK_A, DV_A)), state((H_B, DQK_B, DV_B)),
                state((H_B, 1, DQK_B)), state((H_B, 1, 1)),
                const((8, LANES)), const((N_GATE_COLS, LANES)),
                const((1, DV_A)), const((1, DV_B))]
    out_specs = [pl.BlockSpec((1, tb, MIX_AB), lambda bi, i: (bi, i, 0)),
                 state((H_A, DK_A, DV_A)), state((H_B, DQK_B, DV_B)),
                 state((H_B, 1, DQK_B)), state((H_B, 1, 1))]
    out_shape = [jax.ShapeDtypeStruct((b, t, MIX_AB), BF16),
                 jax.ShapeDtypeStruct((b, H_A, DK_A, DV_A), F32),
                 jax.ShapeDtypeStruct((b, H_B, DQK_B, DV_B), F32),
                 jax.ShapeDtypeStruct((b, H_B, 1, DQK_B), F32),
                 jax.ShapeDtypeStruct((b, H_B, 1, 1), F32)]
    args = [qkv, proj, proj, proj, proj, gates, gates_t, sa0, cb0, nb0, mb0, gpar, gpart, norm_a, norm_b]
    if cast_w is not None:
        r, c = cast_w.shape
        rb = r // (b * nt)
        assert rb * b * nt == r and rb % 16 == 0
        wspec = pl.BlockSpec((rb, c), lambda bi, i: (bi * nt + i, 0))
        in_specs.append(wspec)
        out_specs.append(wspec)
        out_shape.append(jax.ShapeDtypeStruct((r, c), BF16))
        args.append(cast_w)
    return pl.pallas_call(
        functools.partial(_mixer_ab_kernel, L=L, CB=CB, side_cast=cast_w is not None),
        grid=(b, nt), in_specs=in_specs, out_specs=out_specs, out_shape=out_shape,
        compiler_params=_cparams(("parallel", "arbitrary")),
        name=name,
    )(*args)


def _mixer_c_kernel(q_ref, k_ref, v_ref, g_ref, cos_ref, sin_ref, s0_ref, mixed_ref, s_o, intra_s, *, L, CB):
    log_gamma = [float(np.log1p(-np.exp2(-5.0 - h))) for h in range(H_C)]

    @pl.when(pl.program_id(1) == 0)
    def _():
        s_o[...] = s0_ref[...]
        ii = lax.broadcasted_iota(I32, (L, L), 0)
        jj = lax.broadcasted_iota(I32, (L, L), 1)
        rel = (ii - jj).astype(F32)
        for h in range(H_C):
            intra_s[h] = jnp.where(rel >= 0, jnp.exp(log_gamma[h] * jnp.maximum(rel, 0.0)), 0.0)

    idx = lax.broadcasted_iota(I32, (L, 1), 0).astype(F32)
    half = DK_C // 2

    def rope(x, cos, sin):
        x1, x2 = x[:, :half], x[:, half:]
        return jnp.concatenate([x1 * cos - x2 * sin, x1 * sin + x2 * cos], axis=-1)

    for c in range(CB):
        r0 = c * L
        cos = cos_ref[r0:r0 + L, :]
        sin = sin_ref[r0:r0 + L, :]
        for h in range(H_C):
            lg = log_gamma[h]
            intra = intra_s[h]
            q_decay = jnp.exp(lg * (idx + 1.0))
            k_decay = jnp.exp(lg * (L - 1.0 - idx))
            s_decay = float(np.exp(lg * L))
            q = rope(q_ref[0, r0:r0 + L, h * DK_C:(h + 1) * DK_C].astype(F32), cos, sin)
            k = rope(k_ref[0, r0:r0 + L, h * DK_C:(h + 1) * DK_C].astype(F32), cos, sin) * (DK_C ** -0.5)
            v = v_ref[0, r0:r0 + L, h * DV_C:(h + 1) * DV_C]
            s = s_o[0, h]
            o = _mmb(_mmb(q, k, NT) * intra, v) + _mmb(q * q_decay, s)
            s_o[0, h] = s_decay * s + _mmb(k * k_decay, v, TN)
            gg = g_ref[0, r0:r0 + L, h * DV_C:(h + 1) * DV_C].astype(F32)
            o = _rms(o) * _silu(gg)
            mixed_ref[0, r0:r0 + L, h * DV_C:(h + 1) * DV_C] = o.astype(mixed_ref.dtype)


def _mixer_c_call(proj, cos, sin, s0, *, L, CB, name):
    b, t, _ = proj.shape
    tb = L * CB
    nt = t // tb
    qk_w, v_w = H_C * DK_C, H_C * DV_C
    sspec = pl.BlockSpec((1, H_C, DK_C, DV_C), lambda bi, i: (bi, 0, 0, 0))
    tspec = pl.BlockSpec((tb, DK_C // 2), lambda bi, i: (i, 0))
    return pl.pallas_call(
        functools.partial(_mixer_c_kernel, L=L, CB=CB),
        grid=(b, nt),
        in_specs=[pl.BlockSpec((1, tb, qk_w), lambda bi, i: (bi, i, 0)),
                  pl.BlockSpec((1, tb, qk_w), lambda bi, i: (bi, i, 1)),
                  pl.BlockSpec((1, tb, v_w), lambda bi, i: (bi, i, 1)),
                  pl.BlockSpec((1, tb, v_w), lambda bi, i: (bi, i, 2)),
                  tspec, tspec, sspec],
        out_specs=[pl.BlockSpec((1, tb, MIX_C), lambda bi, i: (bi, i, 0)), sspec],
        out_shape=[jax.ShapeDtypeStruct((b, t, MIX_C), BF16),
                   jax.ShapeDtypeStruct((b, H_C, DK_C, DV_C), F32)],
        scratch_shapes=[pltpu.VMEM((H_C, L, L), F32)],
        compiler_params=_cparams(("parallel", "arbitrary")),
        name=name,
    )(proj, proj, proj, proj, cos, sin, s0)


def _out_route_kernel(mixed_ref, wout_ref, h_ref, g1_ref, gain_ref, sc_ref, sh_ref, wrt_ref, rb_ref,
                      hnew_ref, hx_ref, bkt_ref):
    bb, tt, m = mixed_ref.shape
    d = h_ref.shape[-1]
    rows = bb * tt
    y = _dot(mixed_ref[...].reshape(rows, m), wout_ref[...])
    h = h_ref[...] + g1_ref[...] * y.reshape(bb, tt, d)
    hnew_ref[...] = h
    hn = _rms(h) * gain_ref[...]
    hn = (hn * (1.0 + sc_ref[...]) + sh_ref[...]).reshape(rows, d)
    hx_ref[:, 0:d] = hn

    logits = _mm3(wrt_ref[...], hn, NT)
    score = _sigmoid(logits)
    sel = score + rb_ref[...]

    def row(a, r):
        return a[r:r + 1, :]

    gscore = []
    for g in range(N_GROUPS):
        a, b, c, e = (row(sel, EPG * g + i) for i in range(EPG))
        hi_ab, lo_ab, hi_ce, lo_ce = jnp.maximum(a, b), jnp.minimum(a, b), jnp.maximum(c, e), jnp.minimum(c, e)
        top1 = jnp.maximum(hi_ab, hi_ce)
        top2 = jnp.maximum(jnp.maximum(lo_ab, lo_ce), jnp.minimum(hi_ab, hi_ce))
        gscore.append(top1 + top2)
    best = jnp.zeros((1, rows), I32)
    bestv = gscore[0]
    for g in range(1, N_GROUPS):
        upd = gscore[g] > bestv
        best = jnp.where(upd, g, best)
        bestv = jnp.where(upd, gscore[g], bestv)

    def pick(a, i):
        out = row(a, i)
        for g in range(1, N_GROUPS):
            out = jnp.where(best == g, row(a, EPG * g + i), out)
        return out

    vsel = [pick(sel, i) for i in range(EPG)]
    vsc = [pick(score, i) for i in range(EPG)]
    i1 = jnp.zeros((1, rows), I32)
    v1 = vsel[0]
    for i in range(1, EPG):
        upd = vsel[i] > v1
        i1 = jnp.where(upd, i, i1)
        v1 = jnp.where(upd, vsel[i], v1)
    i2 = jnp.full((1, rows), -1, I32)
    v2 = jnp.full((1, rows), -jnp.inf, F32)
    for i in range(EPG):
        cand = jnp.where(i1 == i, -jnp.inf, vsel[i])
        upd = cand > v2
        i2 = jnp.where(upd, i, i2)
        v2 = jnp.where(upd, cand, v2)
    s1 = jnp.zeros((1, rows), F32)
    s2 = jnp.zeros((1, rows), F32)
    for i in range(EPG):
        s1 = jnp.where(i1 == i, vsc[i], s1)
        s2 = jnp.where(i2 == i, vsc[i], s2)
    den = s1 + s2
    w1 = s1 / den
    w2 = s2 / den
    first_lo = i1 < i2
    lo = jnp.minimum(i1, i2)
    hi = jnp.maximum(i1, i2)
    wlo = jnp.where(first_lo, w1, w2)
    whi = jnp.where(first_lo, w2, w1)
    pair = jnp.where(lo == 0, hi - 1, jnp.where(lo == 1, 6 - hi, 5))
    bkt_ref[...] = (best * 6 + pair).reshape(1, 1, rows)
    wa = jnp.where(pair == 5, whi, wlo)
    wb = jnp.where(pair == 5, wlo, whi)

    aux = jnp.concatenate([wa, wb, jnp.zeros((LANES - 2, rows), F32)], axis=0)
    hx_ref[:, d:d + LANES] = aux.T


def _out_route_call(mixed, wout, h, g1, gain, sc, sh, wrt, rb, *, bb, tt, name):
    b, t, m = mixed.shape
    d = h.shape[-1]
    rows = bb * tt
    nti = t // tt
    xspec = pl.BlockSpec((bb, tt, d), lambda bi, i: (bi, i, 0))
    mspec = pl.BlockSpec((bb, 1, d), lambda bi, i: (bi, 0, 0))
    return pl.pallas_call(
        _out_route_kernel,
        grid=(b // bb, nti),
        in_specs=[pl.BlockSpec((bb, tt, m), lambda bi, i: (bi, i, 0)),
                  pl.BlockSpec((m, d), lambda bi, i: (0, 0)),
                  xspec, mspec,
                  pl.BlockSpec((1, 1, d), lambda bi, i: (0, 0, 0)), mspec, mspec,
                  pl.BlockSpec((N_EXPERTS, d), lambda bi, i: (0, 0)),
                  pl.BlockSpec((N_EXPERTS, 1), lambda bi, i: (0, 0))],
        out_specs=[xspec,
                   pl.BlockSpec((rows, HX_W), lambda bi, i: (bi * nti + i, 0)),
                   pl.BlockSpec((1, 1, rows), lambda bi, i: (bi * nti + i, 0, 0))],
        out_shape=[jax.ShapeDtypeStruct((b, t, d), F32),
                   jax.ShapeDtypeStruct((b * t, HX_W), F32),
                   jax.ShapeDtypeStruct(((b // bb) * nti, 1, rows), I32)],
        compiler_params=_cparams(("parallel", "parallel")),
        name=name,
    )(mixed, wout, h, g1, gain.reshape(1, 1, d), sc, sh, wrt, rb)


def _dispatch_kernel(pos_ref, hxp_ref, hxs_ref, xs_in, xs_out, sem, *, np_tiles):
    del xs_in
    i = pl.program_id(0)
    rows = hxp_ref.shape[0]

    def scatter(hx_ref):
        def row_copy(r):
            p = pos_ref[i * rows + r]
            return pltpu.make_async_copy(hx_ref.at[pl.ds(r, 1), :], xs_out.at[pl.ds(p, 1), :], sem)

        for r in range(rows):
            row_copy(r).start(priority=r % 2)
        for r in range(rows):
            row_copy(r).wait()

    @pl.when(i < np_tiles)
    def _():
        scatter(hxp_ref)

    @pl.when(i >= np_tiles)
    def _():
        scatter(hxs_ref)


def _dispatch_call(pos, hx_p, hx_s, xs_buf, rows):
    np_tiles, ns_tiles = hx_p.shape[0] // rows, hx_s.shape[0] // rows
    return pl.pallas_call(
        functools.partial(_dispatch_kernel, np_tiles=np_tiles),
        grid_spec=pltpu.PrefetchScalarGridSpec(
            num_scalar_prefetch=1, grid=(np_tiles + ns_tiles,),
            in_specs=[pl.BlockSpec((rows, HX_W), lambda i, ps: (jnp.minimum(i, np_tiles - 1), 0)),
                      pl.BlockSpec((rows, HX_W), lambda i, ps: (jnp.maximum(i - np_tiles, 0), 0)),
                      pl.BlockSpec(memory_space=pl.ANY)],
            out_specs=pl.BlockSpec(memory_space=pl.ANY),
            scratch_shapes=[pltpu.SemaphoreType.DMA(())]),
        out_shape=jax.ShapeDtypeStruct(xs_buf.shape, F32),
        input_output_aliases={3: 0},
        compiler_params=_cparams(("arbitrary",)),
        name="dispatch",
    )(pos, hx_p, hx_s, xs_buf)


def _moe_kernel(exp_ref, chg_ref, nxt_ref, valid_ref, xs_ref, wg_hbm, wu_hbm, wd_hbm,
                ys_ref, wg_f, wu_f, wd_f, wgu_s, wd_s, sem, *, layer):
    t = pl.program_id(0)

    def fetch(slot, e):
        return (pltpu.make_async_copy(wg_hbm.at[layer, e], wg_f.at[slot], sem.at[slot]),
                pltpu.make_async_copy(wu_hbm.at[layer, e], wu_f.at[slot], sem.at[slot]),
                pltpu.make_async_copy(wd_hbm.at[layer, e], wd_f.at[slot], sem.at[slot]))

    def recast(slot):
        rc = 128
        for i in range(D_MODEL // rc):
            rows = slice(i * rc, (i + 1) * rc)
            wgu_s[rows, slot * D_FF:(slot + 1) * D_FF] = wg_f[slot, rows, :].astype(BF16)
            wgu_s[rows, (2 + slot) * D_FF:(3 + slot) * D_FF] = wu_f[slot, rows, :].astype(BF16)
        for i in range(D_FF // rc):
            wd_s[slot * D_FF + i * rc:slot * D_FF + (i + 1) * rc, :] = wd_f[slot, i * rc:(i + 1) * rc, :].astype(BF16)

    for slot in range(2):
        @pl.when(t == 0)
        def _():
            for cp in fetch(slot, exp_ref[slot, 0]):
                cp.start()

        @pl.when(chg_ref[slot, t] > 0)
        def _():
            for cp in fetch(slot, exp_ref[slot, t]):
                cp.wait()
            recast(slot)

            @pl.when(nxt_ref[slot, t] >= 0)
            def _():
                for cp in fetch(slot, nxt_ref[slot, t]):
                    cp.start()

    @pl.when(valid_ref[t] > 0)
    def _():
        x = xs_ref[:, 0:D_MODEL].astype(BF16)
        w_a = xs_ref[:, D_MODEL:D_MODEL + 1]
        w_b = xs_ref[:, D_MODEL + 1:D_MODEL + 2]
        gu = _dot(x, wgu_s[...])
        lane = lax.broadcasted_iota(I32, (MOE_TM, 2 * D_FF), 1)
        act = _silu(gu[:, :2 * D_FF]) * gu[:, 2 * D_FF:] * jnp.where(lane < D_FF, w_a, w_b)
        ys_ref[...] = _dot(act.astype(BF16), wd_s[...])

    @pl.when(valid_ref[t] == 0)
    def _():
        ys_ref[...] = jnp.zeros_like(ys_ref)


def _moe_call(experts, chg, nxt, valid, xs, wg, wu, wd, layer):
    p = xs.shape[0]
    n_tiles = p // MOE_TM
    return pl.pallas_call(
        functools.partial(_moe_kernel, layer=layer),
        grid_spec=pltpu.PrefetchScalarGridSpec(
            num_scalar_prefetch=4, grid=(n_tiles,),
            in_specs=[pl.BlockSpec((MOE_TM, HX_W), lambda t, ex, ch, nx, va: (t, 0)),
                      pl.BlockSpec(memory_space=pl.ANY), pl.BlockSpec(memory_space=pl.ANY),
                      pl.BlockSpec(memory_space=pl.ANY)],
            out_specs=pl.BlockSpec((MOE_TM, D_MODEL), lambda t, ex, ch, nx, va: (t, 0)),
            scratch_shapes=[pltpu.VMEM((2, D_MODEL, D_FF), F32), pltpu.VMEM((2, D_MODEL, D_FF), F32),
                            pltpu.VMEM((2, D_FF, D_MODEL), F32),
                            pltpu.VMEM((D_MODEL, 4 * D_FF), BF16), pltpu.VMEM((2 * D_FF, D_MODEL), BF16),
                            pltpu.SemaphoreType.DMA((2,))]),
        out_shape=jax.ShapeDtypeStruct((p, D_MODEL), F32),
        compiler_params=_cparams(("arbitrary",)),
        name="moe",
    )(experts, chg, nxt, valid, xs, wg, wu, wd)


_PAIRS = [(0, 1), (0, 2), (0, 3), (1, 3), (1, 2), (3, 2)]
_EA_TABLE = np.array([g * EPG + p[0] for g in range(N_GROUPS) for p in _PAIRS], np.int32)
_EB_TABLE = np.array([g * EPG + p[1] for g in range(N_GROUPS) for p in _PAIRS], np.int32)


def _moe(hx_p, hx_s, bkt, xs_buf, wg, wu, wd, layer):
    n = hx_p.shape[0] + hx_s.shape[0]
    tm = MOE_TM
    ar = jnp.arange(N_BUCKETS, dtype=I32)
    blk = LANES
    nb = -(-n // blk)
    bkt_p = jnp.pad(bkt, (0, nb * blk - n), constant_values=N_BUCKETS)
    oh = (ar[:, None] == bkt_p[None, :]).astype(F32)
    triu = jnp.triu(jnp.ones((blk, blk), F32))
    inner = jnp.einsum('kbj,ji->kbi', oh.reshape(N_BUCKETS, nb, blk), triu)
    tot = inner[:, :, -1]
    outer = jnp.cumsum(tot, axis=1) - tot
    cs = (inner + outer[:, :, None]).reshape(N_BUCKETS, nb * blk)
    counts = (outer[:, -1] + tot[:, -1]).astype(I32)
    padded = ((counts + tm - 1) // tm) * tm
    ends = jnp.cumsum(padded)
    offs = ends - padded
    pos = jnp.sum(oh * (cs - 1.0 + offs.astype(F32)[:, None]), axis=0).astype(I32)[:n]
    n_tiles = -(-n // tm) + N_BUCKETS
    p = n_tiles * tm
    if xs_buf is None:
        xs_buf = jnp.zeros((p, HX_W), F32)
    tstart = jnp.arange(n_tiles, dtype=I32) * tm
    tb = jnp.sum((ends[None, :] <= tstart[:, None]).astype(I32), axis=1)
    n_used = ends[-1] // tm
    valid = (jnp.arange(n_tiles, dtype=I32) < n_used).astype(I32)
    last_b = tb[jnp.maximum(n_used - 1, 0)]
    tb = jnp.clip(jnp.where(valid > 0, tb, last_b), 0, N_BUCKETS - 1)
    ea = jnp.asarray(_EA_TABLE)[tb]
    eb = jnp.asarray(_EB_TABLE)[tb]
    experts = jnp.stack([ea, eb])
    chg = jnp.concatenate([jnp.ones((2, 1), I32), (experts[:, 1:] != experts[:, :-1]).astype(I32)], axis=1)
    tidx = jnp.arange(n_tiles, dtype=I32)
    chg_at = jnp.where(chg > 0, tidx[None, :], n_tiles)
    nxt_at = jnp.concatenate([lax.cummin(chg_at, axis=1, reverse=True)[:, 1:], jnp.full((2, 1), n_tiles, I32)], axis=1)
    nxt = jnp.where(nxt_at < n_tiles,
                    jnp.take_along_axis(experts, jnp.minimum(nxt_at, n_tiles - 1), axis=1), -1)
    xs = _dispatch_call(pos, hx_p, hx_s, xs_buf, tm)
    ys = _moe_call(experts, chg, nxt, valid, xs, wg, wu, wd, layer)
    return ys.at[pos].get(mode="promise_in_bounds", unique_indices=True), xs


def _final_kernel(h_ref, y_ref, g_ref, gain_ref, o_ref):
    bb, tt, d = h_ref.shape
    h = h_ref[...] + g_ref[...] * y_ref[...].reshape(bb, tt, d)
    o_ref[...] = _rms(h) * gain_ref[...]


def _final_call(h, y2d, off, g2, gain, *, bb, tt, name):
    b, t, d = h.shape
    nti = t // tt
    rows = bb * tt
    xspec = pl.BlockSpec((bb, tt, d), lambda bi, i: (bi, i, 0))
    return pl.pallas_call(
        _final_kernel,
        grid=(b // bb, nti),
        in_specs=[xspec, pl.BlockSpec((rows, d), lambda bi, i: (off + bi * nti + i, 0)),
                  pl.BlockSpec((bb, 1, d), lambda bi, i: (bi, 0, 0)),
                  pl.BlockSpec((1, 1, d), lambda bi, i: (0, 0, 0))],
        out_specs=xspec,
        out_shape=jax.ShapeDtypeStruct((b, t, d), F32),
        compiler_params=_cparams(("parallel", "parallel")),
        name=name,
    )(h, y2d, g2, gain.reshape(1, 1, d))


def _group_cfg(b, t):
    if t >= 512:
        return dict(bb=1, tt=512, tt_out=1024, L=CHUNK, CB=4, Lc=4 * CHUNK, CBc=1)
    return dict(bb=b, tt=t, tt_out=t, L=min(CHUNK, t), CB=1, Lc=min(CHUNK, t), CBc=1)


def _rope_tables(pos0, t):
    half = DK_C // 2
    inv = np.power(np.float64(ROPE_BASE), -np.linspace(0.0, 1.0, half, dtype=np.float64))
    ang = (np.float64(pos0) + np.arange(t, dtype=np.float64))[:, None] * inv[None, :]
    return jnp.asarray(np.cos(ang), F32), jnp.asarray(np.sin(ang), F32)


def kernel(x_prompt, x_sample, c_prompt, c_sample, state_a_conv, state_a_rec, state_b_c, state_b_n, state_b_m,
           state_c_rec, w_mod, b_mod, norm_mix, norm_ffn, w_in_ab, conv_a, a_log, dt_bias, norm_a, gate_bias_b,
           norm_b, w_out_ab, w_in_c, w_out_c, w_router, router_bias, w_gate, w_up, w_down, norm_final):
    d = D_MODEL
    bp, tp, _ = x_prompt.shape
    bs, ts, _ = x_sample.shape
    n_ab, n_c = w_in_ab.shape[0], w_in_c.shape[0]

    mod_all = _mod_call(jnp.concatenate([c_prompt, c_sample], axis=0), w_mod, b_mod)

    o = np.cumsum((0,) + AB_SIZES)
    w_ab_main = jnp.concatenate([w_in_ab[:, :, o[0]:o[1]].astype(BF16), w_in_ab[:, :, o[3]:o[7]].astype(BF16),
                                 w_in_ab[:, :, o[8]:o[9]].astype(BF16)], axis=-1)
    w_ab_gate = jnp.concatenate([w_in_ab[:, :, o[1]:o[3]], w_in_ab[:, :, o[7]:o[8]]], axis=-1)
    w_ab_gate_pad = jnp.pad(w_ab_gate, ((0, 0), (0, 0), (0, LANES - N_GATE_COLS)))
    w_c_b = {}
    w_out_ab_b = w_out_ab.astype(BF16)
    w_out_c_b = w_out_c.astype(BF16)
    wrt = w_router.T
    rb = router_bias.reshape(N_EXPERTS, 1)
    zeros4 = jnp.zeros((n_ab, H_A), F32)
    gpar_row0 = jnp.concatenate([a_log, jnp.zeros((n_ab, LANES - H_A), F32)], axis=1)
    gpar_row1 = jnp.concatenate([dt_bias, zeros4, gate_bias_b, jnp.zeros((n_ab, LANES - 16), F32)], axis=1)
    gpar = jnp.concatenate([gpar_row0[:, None], gpar_row1[:, None], jnp.zeros((n_ab, 6, LANES), F32)], axis=1)
    gpart = jnp.swapaxes(jnp.concatenate([gpar[:, :2, :N_GATE_COLS], jnp.zeros((n_ab, LANES - 2, N_GATE_COLS), F32)],
                                         axis=1), 1, 2)

    groups = []
    zf = lambda *s: jnp.zeros(s, F32)
    groups.append(dict(
        x=x_prompt, b=bp, t=tp, pos0=0.0, mod=mod_all[:, :bp], row0=0,
        conv=zf(n_ab, bp, CONV_W - 1, QKV_A), sa=zf(n_ab, bp, H_A, DK_A, DV_A), cb=zf(n_ab, bp, H_B, DQK_B, DV_B),
        nb=zf(n_ab, bp, H_B, DQK_B), mb=zf(n_ab, bp, H_B), sc=zf(n_c, bp, H_C, DK_C, DV_C)))
    groups.append(dict(
        x=x_sample, b=bs, t=ts, pos0=float(PAST_LEN), mod=mod_all[:, bp:], row0=bp * tp,
        conv=state_a_conv, sa=state_a_rec, cb=state_b_c, nb=state_b_n, mb=state_b_m, sc=state_c_rec))
    for g in groups:
        g.update(_group_cfg(g["b"], g["t"]))
        g["h"] = g["x"]
        g["prev"] = None
        g["new_ab"], g["new_c"] = [], []
        g["cos"], g["sin"] = _rope_tables(g["pos0"], g["t"])
        g["roff"] = g["row0"] // (g["bb"] * g["tt"])

    xs_buf = None
    xs_rows = (-(-(bp * tp + bs * ts) // MOE_TM) + N_BUCKETS) * MOE_TM
    for layer in range(DEPTH):
        li = layer // 2
        bkts, hxs = [], []
        for gi, g in enumerate(groups):
            b, t, bb, tt = g["b"], g["t"], g["bb"], g["tt"]
            mods = [m.reshape(b, 1, d) for m in jnp.split(g["mod"][layer], N_MOD, axis=-1)]
            sh1, sc1, g1, sh2, sc2, g2 = mods
            tag = f"l{layer}g{gi}"
            if layer % 2 == 0:
                outs = _ln_mm_call(g["h"], g["prev"], norm_mix[layer], sc1, sh1, w_ab_main[li],
                                   (w_ab_gate_pad[li], g["conv"][li], conv_a[li]), bb=bb, tt=tt, tn=512,
                                   out_dtype=F32, name="ln_ab_" + tag,
                                   zero_rows=xs_rows if (xs_buf is None and gi == 0) else 0)
                if g["prev"] is not None:
                    g["h"], outs = outs[0], outs[1:]
                if xs_buf is None and gi == 0:
                    xs_buf, outs = outs[-1], outs[:-1]
                proj, gates, gates_t, qkv, conv_n = outs
                gates_t = gates_t.reshape(N_GATE_COLS, b, t).transpose(1, 0, 2)
                cast_next = gi == 0 and layer + 1 < DEPTH
                mixed, sa_n, cb_n, nb_n, mb_n, *cast = _mixer_ab_call(
                    qkv, proj, gates, gates_t, g["sa"][li], g["cb"][li],
                    g["nb"][li].reshape(b, H_B, 1, DQK_B), g["mb"][li].reshape(b, H_B, 1, 1),
                    gpar[li], gpart[li], norm_a[li].reshape(1, DV_A), norm_b[li].reshape(1, DV_B),
                    L=g["L"], CB=g["CB"], name="mixer_ab_" + tag,
                    cast_w=w_in_c[(layer + 1) // 2] if cast_next else None)
                if cast_next:
                    w_c_b[(layer + 1) // 2] = cast[0]
                g["new_ab"].append((conv_n, sa_n, cb_n, nb_n.reshape(b, H_B, DQK_B), mb_n.reshape(b, H_B)))
                wout = w_out_ab_b[li]
            else:
                outs = _ln_mm_call(g["h"], g["prev"], norm_mix[layer], sc1, sh1, w_c_b[li], None,
                                   bb=bb, tt=tt, tn=512, out_dtype=BF16, name="ln_c_" + tag)
                if g["prev"] is not None:
                    g["h"], outs = outs[0], outs[1:]
                (proj,) = outs
                mixed, sc_n = _mixer_c_call(proj, g["cos"], g["sin"], g["sc"][li], L=g["Lc"], CB=g["CBc"],
                                            name="mixer_c_" + tag)
                g["new_c"].append(sc_n)
                wout = w_out_c_b[li]
            tto = g["tt_out"]
            g["h"], hx, bkt = _out_route_call(mixed, wout, g["h"], g1, norm_ffn[layer], sc2, sh2, wrt, rb,
                                              bb=bb, tt=tto, name="out_route_" + tag)
            bkts.append(bkt.reshape(-1))
            hxs.append(hx)
            g["g2"] = g2
        y_moe, xs_buf = _moe(hxs[0], hxs[1], jnp.concatenate(bkts), xs_buf, w_gate, w_up, w_down, layer)
        for g in groups:
            g["prev"] = (y_moe, g["roff"], g["g2"])

    outs = []
    for gi, g in enumerate(groups):
        y2d, off, g2 = g["prev"]
        y = _final_call(g["h"], y2d, off, g2, norm_final, bb=g["bb"], tt=g["tt"], name=f"final_g{gi}")
        ab = [jnp.stack(s) for s in zip(*g["new_ab"])]
        outs.append((y, ab[0], ab[1], ab[2], ab[3], ab[4], jnp.stack(g["new_c"])))
    p, s = outs
    return (p[0], s[0], p[1], p[2], p[3], p[4], p[5], p[6], s[1], s[2], s[3], s[4], s[5], s[6])
```
